```python
import math
import jax, jax.numpy as jnp
from jax import lax
import numpy as np

D_MODEL = 1024
BATCH = 8
SEQ = 8192
DEPTH = 4

MEM_LEN = 256
D_MIX = 2 * D_MODEL
D_S5 = D_MIX // 2
D_SSD = D_MIX - D_S5
S5_GROUP = 16
S5_GROUPS = D_S5 // S5_GROUP
S5_STATE = 64
SSD_HEADDIM = 64
SSD_HEADS = D_SSD // SSD_HEADDIM
SSD_GROUPS = 4
SSD_HPG = SSD_HEADS // SSD_GROUPS
SSD_STATE = 128
SSD_CONV = 4
SSD_CHUNK = 128
D_CONV_CH = D_SSD + 2 * SSD_GROUPS * SSD_STATE
D_IN_PROJ = D_S5 + D_SSD + D_CONV_CH + SSD_HEADS
XA_HEADS = 4
XA_HEAD_DIM = D_MODEL // XA_HEADS
D_FF = 4 * D_MODEL
EPS = 1e-5

kernel_name = "hybrid_s5_ssd_xattn_trunk"


def rms_norm(x, g):
    xf = x.astype(jnp.float32)
    y = xf * lax.rsqrt(jnp.mean(xf * xf, axis=-1, keepdims=True) + EPS)
    return (y * g.astype(jnp.float32)).astype(x.dtype)


def s5_mixer(u, a_re, a_im, log_dt, b_re, b_im, c_re, c_im, d, w_glu):
    bsz, seq, _ = u.shape
    f32 = jnp.float32
    uf = u.astype(f32).reshape(bsz, seq, S5_GROUPS, S5_GROUP)
    ar, ai = a_re.astype(f32), a_im.astype(f32)
    dt = jnp.exp(log_dt.astype(f32))[:, None]
    mag = jnp.exp(dt * ar)
    abar_r, abar_i = mag * jnp.cos(dt * ai), mag * jnp.sin(dt * ai)
    den = ar * ar + ai * ai
    zr, zi = abar_r - 1.0, abar_i
    fr = (zr * ar + zi * ai) / den
    fi = (zi * ar - zr * ai) / den
    br, bi = b_re.astype(f32), b_im.astype(f32)
    bbar_r = fr[..., None] * br - fi[..., None] * bi
    bbar_i = fr[..., None] * bi + fi[..., None] * br
    drive_r = jnp.einsum('bsgh,gph->bsgp', uf, bbar_r)
    drive_i = jnp.einsum('bsgh,gph->bsgp', uf, bbar_i)
    a_r = jnp.broadcast_to(abar_r, (1, seq) + abar_r.shape)
    a_i = jnp.broadcast_to(abar_i, (1, seq) + abar_i.shape)

    def combine(left, right):
        a1r, a1i, b1r, b1i = left
        a2r, a2i, b2r, b2i = right
        return (a2r * a1r - a2i * a1i,
                a2r * a1i + a2i * a1r,
                a2r * b1r - a2i * b1i + b2r,
                a2r * b1i + a2i * b1r + b2i)

    _, _, s_r, s_i = lax.associative_scan(combine, (a_r, a_i, drive_r, drive_i), axis=1)
    y = (jnp.einsum('bsgp,ghp->bsgh', s_r, c_re.astype(f32))
         - jnp.einsum('bsgp,ghp->bsgh', s_i, c_im.astype(f32))
         + d.astype(f32) * uf)
    y = jax.nn.gelu(y.reshape(bsz, seq, D_S5))
    y = y * jax.nn.sigmoid(y @ w_glu.astype(f32))
    return y.astype(u.dtype)


def segsum(a):
    t = a.shape[-1]
    aa = jnp.broadcast_to(a[..., :, None], a.shape + (t,))
    strict = jnp.tril(jnp.ones((t, t), dtype=bool), -1)
    cs = jnp.cumsum(jnp.where(strict, aa, 0.0), axis=-2)
    incl = jnp.tril(jnp.ones((t, t), dtype=bool), 0)
    return jnp.where(incl, cs, -jnp.inf)


def ssd_chunked(x, a, bm, cm):
    bsz, seq, g, r, p = x.shape
    n = bm.shape[-1]
    nc, L = seq // SSD_CHUNK, SSD_CHUNK
    x = x.reshape(bsz, nc, L, g, r, p)
    bm = bm.reshape(bsz, nc, L, g, n)
    cm = cm.reshape(bsz, nc, L, g, n)
    a = a.reshape(bsz, nc, L, g, r).transpose(0, 3, 4, 1, 2)
    a_cum = jnp.cumsum(a, axis=-1)
    lmat = jnp.exp(segsum(a))
    cb = jnp.einsum('bclgn,bcsgn->bcgls', cm, bm)
    y_diag = jnp.einsum('bcgls,bgrcls,bcsgrp->bclgrp', cb, lmat, x)
    decay_states = jnp.exp(a_cum[..., -1:] - a_cum)
    states = jnp.einsum('bcsgn,bgrcs,bcsgrp->bcgrpn', bm, decay_states, x)
    states = jnp.concatenate([jnp.zeros_like(states[:, :1]), states], axis=1)
    chunk_tot = jnp.pad(a_cum[..., -1], ((0, 0), (0, 0), (0, 0), (1, 0)))
    decay_chunk = jnp.exp(segsum(chunk_tot))
    new_states = jnp.einsum('bgrzc,bcgrpn->bzgrpn', decay_chunk, states)
    states_prev = new_states[:, :-1]
    y_off = jnp.einsum('bclgn,bcgrpn,bgrcl->bclgrp', cm, states_prev, jnp.exp(a_cum))
    return (y_diag + y_off).reshape(bsz, seq, g, r, p)


def ssd_mixer(z, xbc, dt_raw, conv_w, conv_b, dt_bias, a_log, d_skip, norm_g):
    bsz, seq, _ = xbc.shape
    f32 = jnp.float32
    pad = jnp.pad(xbc, ((0, 0), (SSD_CONV - 1, 0), (0, 0)))
    conv = conv_b + sum(pad[:, k:k + seq] * conv_w[k] for k in range(SSD_CONV))
    xbc = jax.nn.silu(conv).astype(f32)
    xs, bm, cm = jnp.split(xbc, [D_SSD, D_SSD + SSD_GROUPS * SSD_STATE], axis=-1)
    xs = xs.reshape(bsz, seq, SSD_GROUPS, SSD_HPG, SSD_HEADDIM)
    bm = bm.reshape(bsz, seq, SSD_GROUPS, SSD_STATE)
    cm = cm.reshape(bsz, seq, SSD_GROUPS, SSD_STATE)
    dt = jax.nn.softplus(dt_raw.astype(f32) + dt_bias.astype(f32))
    dt = dt.reshape(bsz, seq, SSD_GROUPS, SSD_HPG)
    a = -jnp.exp(a_log.astype(f32)).reshape(SSD_GROUPS, SSD_HPG)
    y = ssd_chunked(xs * dt[..., None], dt * a, bm, cm)
    y = y + d_skip.astype(f32).reshape(SSD_GROUPS, SSD_HPG, 1) * xs
    y = y.reshape(bsz, seq, D_SSD).astype(z.dtype)
    return rms_norm(y * jax.nn.silu(z), norm_g)


def cross_attention(h, m, wq, wk, wv, wo):
    bsz, seq, _ = h.shape
    q = (h @ wq).reshape(bsz, seq, XA_HEADS, XA_HEAD_DIM)
    k = (m @ wk).reshape(bsz, -1, XA_HEADS, XA_HEAD_DIM)
    v = (m @ wv).reshape(bsz, -1, XA_HEADS, XA_HEAD_DIM)
    scores = jnp.einsum('bshd,bmhd->bhsm', q, k).astype(jnp.float32) * (XA_HEAD_DIM ** -0.5)
    probs = jax.nn.softmax(scores, axis=-1).astype(v.dtype)
    o = jnp.einsum('bhsm,bmhd->bshd', probs, v).reshape(bsz, seq, D_MODEL)
    return o @ wo


def _fwd_setup_inputs(seed: int = 0) -> dict:
    key = jax.random.key(seed)
    ks = jax.random.split(key, 32)
    nrm = jax.random.normal
    f32 = jnp.float32
    Lr = DEPTH
    n_idx = jnp.arange(S5_STATE, dtype=f32)
    log_dt_lo, log_dt_hi = math.log(1e-3), math.log(1e-1)
    ssd_dt = jnp.exp(jax.random.uniform(ks[12], (Lr, SSD_HEADS), f32, log_dt_lo, log_dt_hi))
    return {
        "x": nrm(ks[0], (BATCH, SEQ, D_MODEL), f32),
        "mem": nrm(ks[1], (BATCH, MEM_LEN, D_MODEL), f32),
        "norm_mix": 1.0 + 0.01 * nrm(ks[2], (Lr, D_MODEL), f32),
        "w_in": nrm(ks[3], (Lr, D_MODEL, D_IN_PROJ), f32) * D_MODEL ** -0.5,
        "s5_a_re": -0.5 + 0.01 * nrm(ks[4], (Lr, S5_GROUPS, S5_STATE), f32),
        "s5_a_im": math.pi * n_idx + 0.01 * nrm(ks[5], (Lr, S5_GROUPS, S5_STATE), f32),
        "s5_log_dt": jax.random.uniform(ks[6], (Lr, S5_GROUPS), f32, log_dt_lo, log_dt_hi),
        "s5_b_re": nrm(ks[7], (Lr, S5_GROUPS, S5_STATE, S5_GROUP), f32) * (2 * S5_GROUP) ** -0.5,
        "s5_b_im": nrm(ks[8], (Lr, S5_GROUPS, S5_STATE, S5_GROUP), f32) * (2 * S5_GROUP) ** -0.5,
        "s5_c_re": nrm(ks[9], (Lr, S5_GROUPS, S5_GROUP, S5_STATE), f32) * S5_STATE ** -0.5,
        "s5_c_im": nrm(ks[10], (Lr, S5_GROUPS, S5_GROUP, S5_STATE), f32) * S5_STATE ** -0.5,
        "s5_d": nrm(ks[11], (Lr, S5_GROUPS, S5_GROUP), f32),
        "s5_w_glu": nrm(ks[13], (Lr, D_S5, D_S5), f32) * D_S5 ** -0.5,
        "ssd_conv_w": nrm(ks[14], (Lr, SSD_CONV, D_CONV_CH), f32) * SSD_CONV ** -0.5,
        "ssd_conv_b": 0.01 * nrm(ks[15], (Lr, D_CONV_CH), f32),
        "ssd_dt_bias": ssd_dt + jnp.log(-jnp.expm1(-ssd_dt)),
        "ssd_a_log": jnp.log(jax.random.uniform(ks[16], (Lr, SSD_HEADS), f32, 1.0, 16.0)),
        "ssd_d": 1.0 + 0.01 * nrm(ks[17], (Lr, SSD_HEADS), f32),
        "ssd_norm": 1.0 + 0.01 * nrm(ks[18], (Lr, D_SSD), f32),
        "w_out": nrm(ks[19], (Lr, D_MIX, D_MODEL), f32) * D_MIX ** -0.5,
        "norm_xattn": 1.0 + 0.01 * nrm(ks[20], (Lr, D_MODEL), f32),
        "norm_mem": 1.0 + 0.01 * nrm(ks[21], (Lr, D_MODEL), f32),
        "xa_wq": nrm(ks[22], (Lr, D_MODEL, D_MODEL), f32) * D_MODEL ** -0.5,
        "xa_wk": nrm(ks[23], (Lr, D_MODEL, D_MODEL), f32) * D_MODEL ** -0.5,
        "xa_wv": nrm(ks[24], (Lr, D_MODEL, D_MODEL), f32) * D_MODEL ** -0.5,
        "xa_wo": nrm(ks[25], (Lr, D_MODEL, D_MODEL), f32) * D_MODEL ** -0.5,
        "norm_mlp": 1.0 + 0.01 * nrm(ks[26], (Lr, D_MODEL), f32),
        "mlp_w1": nrm(ks[27], (Lr, D_MODEL, D_FF), f32) * D_MODEL ** -0.5,
        "mlp_w2": nrm(ks[28], (Lr, D_FF, D_MODEL), f32) * D_FF ** -0.5,
        "norm_final": 1.0 + 0.01 * nrm(ks[29], (D_MODEL,), f32),
    }


def _fwd_reference(x, mem, norm_mix, w_in, s5_a_re, s5_a_im, s5_log_dt, s5_b_re, s5_b_im,
              s5_c_re, s5_c_im, s5_d, s5_w_glu, ssd_conv_w, ssd_conv_b, ssd_dt_bias,
              ssd_a_log, ssd_d, ssd_norm, w_out, norm_xattn, norm_mem, xa_wq, xa_wk,
              xa_wv, xa_wo, norm_mlp, mlp_w1, mlp_w2, norm_final):
    splits = [D_S5, D_S5 + D_SSD, D_S5 + D_SSD + D_CONV_CH]
    for l in range(DEPTH):
        h = rms_norm(x, norm_mix[l])
        proj = h @ w_in[l]
        u, z, xbc, dt_raw = jnp.split(proj, splits, axis=-1)
        y_s5 = s5_mixer(u, s5_a_re[l], s5_a_im[l], s5_log_dt[l], s5_b_re[l], s5_b_im[l],
                        s5_c_re[l], s5_c_im[l], s5_d[l], s5_w_glu[l])
        y_ssd = ssd_mixer(z, xbc, dt_raw, ssd_conv_w[l], ssd_conv_b[l], ssd_dt_bias[l],
                          ssd_a_log[l], ssd_d[l], ssd_norm[l])
        x = x + jnp.concatenate([y_s5, y_ssd], axis=-1) @ w_out[l]
        h = rms_norm(x, norm_xattn[l])
        m = rms_norm(mem, norm_mem[l])
        x = x + cross_attention(h, m, xa_wq[l], xa_wk[l], xa_wv[l], xa_wo[l])
        h = rms_norm(x, norm_mlp[l])
        x = x + jnp.square(jax.nn.relu(h @ mlp_w1[l])) @ mlp_w2[l]
    return rms_norm(x, norm_final)


import jax as _jax
import jax.numpy as _jnp

TWIN_FORMAT = 'train_step'
FWD_PARAMS = ['x', 'mem', 'norm_mix', 'w_in', 's5_a_re', 's5_a_im', 's5_log_dt', 's5_b_re', 's5_b_im', 's5_c_re', 's5_c_im', 's5_d', 's5_w_glu', 'ssd_conv_w', 'ssd_conv_b', 'ssd_dt_bias', 'ssd_a_log', 'ssd_d', 'ssd_norm', 'w_out', 'norm_xattn', 'norm_mem', 'xa_wq', 'xa_wk', 'xa_wv', 'xa_wo', 'norm_mlp', 'mlp_w1', 'mlp_w2', 'norm_final']
TWIN_WEIGHTS = ['norm_mix', 'w_in', 's5_a_re', 's5_a_im', 's5_log_dt', 's5_b_re', 's5_b_im', 's5_c_re', 's5_c_im', 's5_d', 's5_w_glu', 'ssd_conv_w', 'ssd_conv_b', 'ssd_dt_bias', 'ssd_a_log', 'ssd_d', 'ssd_norm', 'w_out', 'norm_xattn', 'norm_mem', 'xa_wq', 'xa_wk', 'xa_wv', 'xa_wo', 'norm_mlp', 'mlp_w1', 'mlp_w2', 'norm_final']
TWIN_DIFF_INPUT = 'x'
TWIN_INPUTS = ['x', 'mem', 'norm_mix', 'w_in', 's5_a_re', 's5_a_im', 's5_log_dt', 's5_b_re', 's5_b_im', 's5_c_re', 's5_c_im', 's5_d', 's5_w_glu', 'ssd_conv_w', 'ssd_conv_b', 'ssd_dt_bias', 'ssd_a_log', 'ssd_d', 'ssd_norm', 'w_out', 'norm_xattn', 'norm_mem', 'xa_wq', 'xa_wk', 'xa_wv', 'xa_wo', 'norm_mlp', 'mlp_w1', 'mlp_w2', 'norm_final', 'loss_target', 'm_norm_mix', 'm_w_in', 'm_s5_a_re', 'm_s5_a_im', 'm_s5_log_dt', 'm_s5_b_re', 'm_s5_b_im', 'm_s5_c_re', 'm_s5_c_im', 'm_s5_d', 'm_s5_w_glu', 'm_ssd_conv_w', 'm_ssd_conv_b', 'm_ssd_dt_bias', 'm_ssd_a_log', 'm_ssd_d', 'm_ssd_norm', 'm_w_out', 'm_norm_xattn', 'm_norm_mem', 'm_xa_wq', 'm_xa_wk', 'm_xa_wv', 'm_xa_wo', 'm_norm_mlp', 'm_mlp_w1', 'm_mlp_w2', 'm_norm_final', 'v_norm_mix', 'v_w_in', 'v_s5_a_re', 'v_s5_a_im', 'v_s5_log_dt', 'v_s5_b_re', 'v_s5_b_im', 'v_s5_c_re', 'v_s5_c_im', 'v_s5_d', 'v_s5_w_glu', 'v_ssd_conv_w', 'v_ssd_conv_b', 'v_ssd_dt_bias', 'v_ssd_a_log', 'v_ssd_d', 'v_ssd_norm', 'v_w_out', 'v_norm_xattn', 'v_norm_mem', 'v_xa_wq', 'v_xa_wk', 'v_xa_wv', 'v_xa_wo', 'v_norm_mlp', 'v_mlp_w1', 'v_mlp_w2', 'v_norm_final']
TWIN_OUTPUTS = ['loss', 'grad_x', 'grad_norm_mix', 'grad_w_in', 'grad_s5_a_re', 'grad_s5_a_im', 'grad_s5_log_dt', 'grad_s5_b_re', 'grad_s5_b_im', 'grad_s5_c_re', 'grad_s5_c_im', 'grad_s5_d', 'grad_s5_w_glu', 'grad_ssd_conv_w', 'grad_ssd_conv_b', 'grad_ssd_dt_bias', 'grad_ssd_a_log', 'grad_ssd_d', 'grad_ssd_norm', 'grad_w_out', 'grad_norm_xattn', 'grad_norm_mem', 'grad_xa_wq', 'grad_xa_wk', 'grad_xa_wv', 'grad_xa_wo', 'grad_norm_mlp', 'grad_mlp_w1', 'grad_mlp_w2', 'grad_norm_final', 'delta_norm_mix', 'delta_w_in', 'delta_s5_a_re', 'delta_s5_a_im', 'delta_s5_log_dt', 'delta_s5_b_re', 'delta_s5_b_im', 'delta_s5_c_re', 'delta_s5_c_im', 'delta_s5_d', 'delta_s5_w_glu', 'delta_ssd_conv_w', 'delta_ssd_conv_b', 'delta_ssd_dt_bias', 'delta_ssd_a_log', 'delta_ssd_d', 'delta_ssd_norm', 'delta_w_out', 'delta_norm_xattn', 'delta_norm_mem', 'delta_xa_wq', 'delta_xa_wk', 'delta_xa_wv', 'delta_xa_wo', 'delta_norm_mlp', 'delta_mlp_w1', 'delta_mlp_w2', 'delta_norm_final', 'new_m_norm_mix', 'new_m_w_in', 'new_m_s5_a_re', 'new_m_s5_a_im', 'new_m_s5_log_dt', 'new_m_s5_b_re', 'new_m_s5_b_im', 'new_m_s5_c_re', 'new_m_s5_c_im', 'new_m_s5_d', 'new_m_s5_w_glu', 'new_m_ssd_conv_w', 'new_m_ssd_conv_b', 'new_m_ssd_dt_bias', 'new_m_ssd_a_log', 'new_m_ssd_d', 'new_m_ssd_norm', 'new_m_w_out', 'new_m_norm_xattn', 'new_m_norm_mem', 'new_m_xa_wq', 'new_m_xa_wk', 'new_m_xa_wv', 'new_m_xa_wo', 'new_m_norm_mlp', 'new_m_mlp_w1', 'new_m_mlp_w2', 'new_m_norm_final', 'new_v_norm_mix', 'new_v_w_in', 'new_v_s5_a_re', 'new_v_s5_a_im', 'new_v_s5_log_dt', 'new_v_s5_b_re', 'new_v_s5_b_im', 'new_v_s5_c_re', 'new_v_s5_c_im', 'new_v_s5_d', 'new_v_s5_w_glu', 'new_v_ssd_conv_w', 'new_v_ssd_conv_b', 'new_v_ssd_dt_bias', 'new_v_ssd_a_log', 'new_v_ssd_d', 'new_v_ssd_norm', 'new_v_w_out', 'new_v_norm_xattn', 'new_v_norm_mem', 'new_v_xa_wq', 'new_v_xa_wk', 'new_v_xa_wv', 'new_v_xa_wo', 'new_v_norm_mlp', 'new_v_mlp_w1', 'new_v_mlp_w2', 'new_v_norm_final']
TWIN_LEAF_KINDS = {'loss': 'loss', 'grad_x': 'grad_x', 'grad_norm_mix': 'grad_w', 'grad_w_in': 'grad_w', 'grad_s5_a_re': 'grad_w', 'grad_s5_a_im': 'grad_w', 'grad_s5_log_dt': 'grad_w', 'grad_s5_b_re': 'grad_w', 'grad_s5_b_im': 'grad_w', 'grad_s5_c_re': 'grad_w', 'grad_s5_c_im': 'grad_w', 'grad_s5_d': 'grad_w', 'grad_s5_w_glu': 'grad_w', 'grad_ssd_conv_w': 'grad_w', 'grad_ssd_conv_b': 'grad_w', 'grad_ssd_dt_bias': 'grad_w', 'grad_ssd_a_log': 'grad_w', 'grad_ssd_d': 'grad_w', 'grad_ssd_norm': 'grad_w', 'grad_w_out': 'grad_w', 'grad_norm_xattn': 'grad_w', 'grad_norm_mem': 'grad_w', 'grad_xa_wq': 'grad_w', 'grad_xa_wk': 'grad_w', 'grad_xa_wv': 'grad_w', 'grad_xa_wo': 'grad_w', 'grad_norm_mlp': 'grad_w', 'grad_mlp_w1': 'grad_w', 'grad_mlp_w2': 'grad_w', 'grad_norm_final': 'grad_w', 'delta_norm_mix': 'delta_w', 'delta_w_in': 'delta_w', 'delta_s5_a_re': 'delta_w', 'delta_s5_a_im': 'delta_w', 'delta_s5_log_dt': 'delta_w', 'delta_s5_b_re': 'delta_w', 'delta_s5_b_im': 'delta_w', 'delta_s5_c_re': 'delta_w', 'delta_s5_c_im': 'delta_w', 'delta_s5_d': 'delta_w', 'delta_s5_w_glu': 'delta_w', 'delta_ssd_conv_w': 'delta_w', 'delta_ssd_conv_b': 'delta_w', 'delta_ssd_dt_bias': 'delta_w', 'delta_ssd_a_log': 'delta_w', 'delta_ssd_d': 'delta_w', 'delta_ssd_norm': 'delta_w', 'delta_w_out': 'delta_w', 'delta_norm_xattn': 'delta_w', 'delta_norm_mem': 'delta_w', 'delta_xa_wq': 'delta_w', 'delta_xa_wk': 'delta_w', 'delta_xa_wv': 'delta_w', 'delta_xa_wo': 'delta_w', 'delta_norm_mlp': 'delta_w', 'delta_mlp_w1': 'delta_w', 'delta_mlp_w2': 'delta_w', 'delta_norm_final': 'delta_w', 'new_m_norm_mix': 'new_m', 'new_m_w_in': 'new_m', 'new_m_s5_a_re': 'new_m', 'new_m_s5_a_im': 'new_m', 'new_m_s5_log_dt': 'new_m', 'new_m_s5_b_re': 'new_m', 'new_m_s5_b_im': 'new_m', 'new_m_s5_c_re': 'new_m', 'new_m_s5_c_im': 'new_m', 'new_m_s5_d': 'new_m', 'new_m_s5_w_glu': 'new_m', 'new_m_ssd_conv_w': 'new_m', 'new_m_ssd_conv_b': 'new_m', 'new_m_ssd_dt_bias': 'new_m', 'new_m_ssd_a_log': 'new_m', 'new_m_ssd_d': 'new_m', 'new_m_ssd_norm': 'new_m', 'new_m_w_out': 'new_m', 'new_m_norm_xattn': 'new_m', 'new_m_norm_mem': 'new_m', 'new_m_xa_wq': 'new_m', 'new_m_xa_wk': 'new_m', 'new_m_xa_wv': 'new_m', 'new_m_xa_wo': 'new_m', 'new_m_norm_mlp': 'new_m', 'new_m_mlp_w1': 'new_m', 'new_m_mlp_w2': 'new_m', 'new_m_norm_final': 'new_m', 'new_v_norm_mix': 'new_v', 'new_v_w_in': 'new_v', 'new_v_s5_a_re': 'new_v', 'new_v_s5_a_im': 'new_v', 'new_v_s5_log_dt': 'new_v', 'new_v_s5_b_re': 'new_v', 'new_v_s5_b_im': 'new_v', 'new_v_s5_c_re': 'new_v', 'new_v_s5_c_im': 'new_v', 'new_v_s5_d': 'new_v', 'new_v_s5_w_glu': 'new_v', 'new_v_ssd_conv_w': 'new_v', 'new_v_ssd_conv_b': 'new_v', 'new_v_ssd_dt_bias': 'new_v', 'new_v_ssd_a_log': 'new_v', 'new_v_ssd_d': 'new_v', 'new_v_ssd_norm': 'new_v', 'new_v_w_out': 'new_v', 'new_v_norm_xattn': 'new_v', 'new_v_norm_mem': 'new_v', 'new_v_xa_wq': 'new_v', 'new_v_xa_wk': 'new_v', 'new_v_xa_wv': 'new_v', 'new_v_xa_wo': 'new_v', 'new_v_norm_mlp': 'new_v', 'new_v_mlp_w1': 'new_v', 'new_v_mlp_w2': 'new_v', 'new_v_norm_final': 'new_v'}


def _forward(args):
    return _fwd_reference(*[args[k] for k in FWD_PARAMS])


def _output_shape():
    def fwd():
        inp = _fwd_setup_inputs(0)
        return _fwd_reference(*[inp[k] for k in FWD_PARAMS])
    out = _jax.eval_shape(fwd)
    return out.shape, out.dtype

N_MICROBATCH = 1
ADAM_LR = 0.001
ADAM_B1 = 0.9
ADAM_B2 = 0.999
ADAM_EPS = 1e-08
ADAM_WD = 0.01
ADAM_STEP = 10
PER_EXAMPLE_BATCH_AXIS = {'x': 0, 'mem': 0, 'loss_target': 0}
SHARED_INPUTS = []
_WEIGHT_DTYPES = {'norm_mix': _jnp.float32, 'w_in': _jnp.float32, 's5_a_re': _jnp.float32, 's5_a_im': _jnp.float32, 's5_log_dt': _jnp.float32, 's5_b_re': _jnp.float32, 's5_b_im': _jnp.float32, 's5_c_re': _jnp.float32, 's5_c_im': _jnp.float32, 's5_d': _jnp.float32, 's5_w_glu': _jnp.float32, 'ssd_conv_w': _jnp.float32, 'ssd_conv_b': _jnp.float32, 'ssd_dt_bias': _jnp.float32, 'ssd_a_log': _jnp.float32, 'ssd_d': _jnp.float32, 'ssd_norm': _jnp.float32, 'w_out': _jnp.float32, 'norm_xattn': _jnp.float32, 'norm_mem': _jnp.float32, 'xa_wq': _jnp.float32, 'xa_wk': _jnp.float32, 'xa_wv': _jnp.float32, 'xa_wo': _jnp.float32, 'norm_mlp': _jnp.float32, 'mlp_w1': _jnp.float32, 'mlp_w2': _jnp.float32, 'norm_final': _jnp.float32}
MOMENT_SCALE = {'norm_mix': 1.969546e-01, 'w_in': 9.726220e-02, 's5_a_re': 4.768900e-03, 's5_a_im': 5.538719e-03, 's5_log_dt': 2.705430e+00, 's5_b_re': 2.750270e-03, 's5_b_im': 2.776279e-03, 's5_c_re': 3.675609e-03, 's5_c_im': 3.670854e-03, 's5_d': 5.041259e-02, 's5_w_glu': 1.295679e-02, 'ssd_conv_w': 9.898034e-02, 'ssd_conv_b': 1.390268e-01, 'ssd_dt_bias': 2.056384e-01, 'ssd_a_log': 4.930895e-01, 'ssd_d': 7.057435e-01, 'ssd_norm': 1.353010e-01, 'w_out': 1.364529e-01, 'norm_xattn': 1.885430e-02, 'norm_mem': 2.799634e-02, 'xa_wq': 1.849501e-02, 'xa_wk': 1.858505e-02, 'xa_wv': 1.929426e-02, 'xa_wo': 1.928859e-02, 'norm_mlp': 1.865163e-01, 'mlp_w1': 9.256056e-02, 'mlp_w2': 1.746738e-01, 'norm_final': 6.578461e+01}


def _to_microbatches(a, axis):
    t = _jnp.moveaxis(a, axis, 0)
    t = t.reshape((N_MICROBATCH, t.shape[0] // N_MICROBATCH) + t.shape[1:])
    return _jnp.moveaxis(t, 1, axis + 1)


def setup_inputs(seed: int = 0) -> dict:
    inp = _fwd_setup_inputs(seed)
    key = _jax.random.fold_in(_jax.random.key(seed), 7919)
    shape, _ = _output_shape()
    out = dict(inp)
    out["loss_target"] = _jax.random.normal(_jax.random.fold_in(key, 0), shape, _jnp.float32)
    for i, name in enumerate(TWIN_WEIGHTS):
        w = inp[name].astype(_jnp.float32)
        if MOMENT_SCALE is None:
            s = _jnp.sqrt(_jnp.mean(_jnp.square(w)) + 1e-30)
        else:
            s = MOMENT_SCALE[name]
        km, kv = _jax.random.split(_jax.random.fold_in(key, i + 1))
        out[name] = w
        out["m_" + name] = s * _jax.random.normal(km, w.shape, _jnp.float32)
        out["v_" + name] = (s * s) * _jax.random.uniform(kv, w.shape, _jnp.float32, 0.5, 1.5)
    if N_MICROBATCH > 1:
        for name, axis in PER_EXAMPLE_BATCH_AXIS.items():
            out[name] = _to_microbatches(out[name], axis)
    return {'x': out['x'], 'mem': out['mem'], 'norm_mix': out['norm_mix'], 'w_in': out['w_in'], 's5_a_re': out['s5_a_re'], 's5_a_im': out['s5_a_im'], 's5_log_dt': out['s5_log_dt'], 's5_b_re': out['s5_b_re'], 's5_b_im': out['s5_b_im'], 's5_c_re': out['s5_c_re'], 's5_c_im': out['s5_c_im'], 's5_d': out['s5_d'], 's5_w_glu': out['s5_w_glu'], 'ssd_conv_w': out['ssd_conv_w'], 'ssd_conv_b': out['ssd_conv_b'], 'ssd_dt_bias': out['ssd_dt_bias'], 'ssd_a_log': out['ssd_a_log'], 'ssd_d': out['ssd_d'], 'ssd_norm': out['ssd_norm'], 'w_out': out['w_out'], 'norm_xattn': out['norm_xattn'], 'norm_mem': out['norm_mem'], 'xa_wq': out['xa_wq'], 'xa_wk': out['xa_wk'], 'xa_wv': out['xa_wv'], 'xa_wo': out['xa_wo'], 'norm_mlp': out['norm_mlp'], 'mlp_w1': out['mlp_w1'], 'mlp_w2': out['mlp_w2'], 'norm_final': out['norm_final'], 'loss_target': out['loss_target'], 'm_norm_mix': out['m_norm_mix'], 'm_w_in': out['m_w_in'], 'm_s5_a_re': out['m_s5_a_re'], 'm_s5_a_im': out['m_s5_a_im'], 'm_s5_log_dt': out['m_s5_log_dt'], 'm_s5_b_re': out['m_s5_b_re'], 'm_s5_b_im': out['m_s5_b_im'], 'm_s5_c_re': out['m_s5_c_re'], 'm_s5_c_im': out['m_s5_c_im'], 'm_s5_d': out['m_s5_d'], 'm_s5_w_glu': out['m_s5_w_glu'], 'm_ssd_conv_w': out['m_ssd_conv_w'], 'm_ssd_conv_b': out['m_ssd_conv_b'], 'm_ssd_dt_bias': out['m_ssd_dt_bias'], 'm_ssd_a_log': out['m_ssd_a_log'], 'm_ssd_d': out['m_ssd_d'], 'm_ssd_norm': out['m_ssd_norm'], 'm_w_out': out['m_w_out'], 'm_norm_xattn': out['m_norm_xattn'], 'm_norm_mem': out['m_norm_mem'], 'm_xa_wq': out['m_xa_wq'], 'm_xa_wk': out['m_xa_wk'], 'm_xa_wv': out['m_xa_wv'], 'm_xa_wo': out['m_xa_wo'], 'm_norm_mlp': out['m_norm_mlp'], 'm_mlp_w1': out['m_mlp_w1'], 'm_mlp_w2': out['m_mlp_w2'], 'm_norm_final': out['m_norm_final'], 'v_norm_mix': out['v_norm_mix'], 'v_w_in': out['v_w_in'], 'v_s5_a_re': out['v_s5_a_re'], 'v_s5_a_im': out['v_s5_a_im'], 'v_s5_log_dt': out['v_s5_log_dt'], 'v_s5_b_re': out['v_s5_b_re'], 'v_s5_b_im': out['v_s5_b_im'], 'v_s5_c_re': out['v_s5_c_re'], 'v_s5_c_im': out['v_s5_c_im'], 'v_s5_d': out['v_s5_d'], 'v_s5_w_glu': out['v_s5_w_glu'], 'v_ssd_conv_w': out['v_ssd_conv_w'], 'v_ssd_conv_b': out['v_ssd_conv_b'], 'v_ssd_dt_bias': out['v_ssd_dt_bias'], 'v_ssd_a_log': out['v_ssd_a_log'], 'v_ssd_d': out['v_ssd_d'], 'v_ssd_norm': out['v_ssd_norm'], 'v_w_out': out['v_w_out'], 'v_norm_xattn': out['v_norm_xattn'], 'v_norm_mem': out['v_norm_mem'], 'v_xa_wq': out['v_xa_wq'], 'v_xa_wk': out['v_xa_wk'], 'v_xa_wv': out['v_xa_wv'], 'v_xa_wo': out['v_xa_wo'], 'v_norm_mlp': out['v_norm_mlp'], 'v_mlp_w1': out['v_mlp_w1'], 'v_mlp_w2': out['v_mlp_w2'], 'v_norm_final': out['v_norm_final']}


def _loss(weights, diff, rest, loss_target):
    with _jax.named_scope("forward"):
        args = {**rest, TWIN_DIFF_INPUT: diff, **{k: w.astype(_WEIGHT_DTYPES[k]) for k, w in weights.items()}}
        y = _forward(args)
    with _jax.named_scope("loss_head"):
        err = _jnp.square(y.astype(_jnp.float32) - loss_target)
        return 0.5 * _jnp.sum(_jnp.mean(err, axis=-1)) if err.ndim else 0.5 * err


def _adamw(w, g, m, v):
    m = ADAM_B1 * m + (1.0 - ADAM_B1) * g
    v = ADAM_B2 * v + (1.0 - ADAM_B2) * _jnp.square(g)
    m_hat = m / (1.0 - ADAM_B1 ** ADAM_STEP)
    v_hat = v / (1.0 - ADAM_B2 ** ADAM_STEP)
    delta = -ADAM_LR * (m_hat / (_jnp.sqrt(v_hat) + ADAM_EPS) + ADAM_WD * w)
    return delta, m, v


def reference(x, mem, norm_mix, w_in, s5_a_re, s5_a_im, s5_log_dt, s5_b_re, s5_b_im, s5_c_re, s5_c_im, s5_d, s5_w_glu, ssd_conv_w, ssd_conv_b, ssd_dt_bias, ssd_a_log, ssd_d, ssd_norm, w_out, norm_xattn, norm_mem, xa_wq, xa_wk, xa_wv, xa_wo, norm_mlp, mlp_w1, mlp_w2, norm_final, loss_target, m_norm_mix, m_w_in, m_s5_a_re, m_s5_a_im, m_s5_log_dt, m_s5_b_re, m_s5_b_im, m_s5_c_re, m_s5_c_im, m_s5_d, m_s5_w_glu, m_ssd_conv_w, m_ssd_conv_b, m_ssd_dt_bias, m_ssd_a_log, m_ssd_d, m_ssd_norm, m_w_out, m_norm_xattn, m_norm_mem, m_xa_wq, m_xa_wk, m_xa_wv, m_xa_wo, m_norm_mlp, m_mlp_w1, m_mlp_w2, m_norm_final, v_norm_mix, v_w_in, v_s5_a_re, v_s5_a_im, v_s5_log_dt, v_s5_b_re, v_s5_b_im, v_s5_c_re, v_s5_c_im, v_s5_d, v_s5_w_glu, v_ssd_conv_w, v_ssd_conv_b, v_ssd_dt_bias, v_ssd_a_log, v_ssd_d, v_ssd_norm, v_w_out, v_norm_xattn, v_norm_mem, v_xa_wq, v_xa_wk, v_xa_wv, v_xa_wo, v_norm_mlp, v_mlp_w1, v_mlp_w2, v_norm_final):
    given = dict(x=x, mem=mem, norm_mix=norm_mix, w_in=w_in, s5_a_re=s5_a_re, s5_a_im=s5_a_im, s5_log_dt=s5_log_dt, s5_b_re=s5_b_re, s5_b_im=s5_b_im, s5_c_re=s5_c_re, s5_c_im=s5_c_im, s5_d=s5_d, s5_w_glu=s5_w_glu, ssd_conv_w=ssd_conv_w, ssd_conv_b=ssd_conv_b, ssd_dt_bias=ssd_dt_bias, ssd_a_log=ssd_a_log, ssd_d=ssd_d, ssd_norm=ssd_norm, w_out=w_out, norm_xattn=norm_xattn, norm_mem=norm_mem, xa_wq=xa_wq, xa_wk=xa_wk, xa_wv=xa_wv, xa_wo=xa_wo, norm_mlp=norm_mlp, mlp_w1=mlp_w1, mlp_w2=mlp_w2, norm_final=norm_final, loss_target=loss_target, m_norm_mix=m_norm_mix, m_w_in=m_w_in, m_s5_a_re=m_s5_a_re, m_s5_a_im=m_s5_a_im, m_s5_log_dt=m_s5_log_dt, m_s5_b_re=m_s5_b_re, m_s5_b_im=m_s5_b_im, m_s5_c_re=m_s5_c_re, m_s5_c_im=m_s5_c_im, m_s5_d=m_s5_d, m_s5_w_glu=m_s5_w_glu, m_ssd_conv_w=m_ssd_conv_w, m_ssd_conv_b=m_ssd_conv_b, m_ssd_dt_bias=m_ssd_dt_bias, m_ssd_a_log=m_ssd_a_log, m_ssd_d=m_ssd_d, m_ssd_norm=m_ssd_norm, m_w_out=m_w_out, m_norm_xattn=m_norm_xattn, m_norm_mem=m_norm_mem, m_xa_wq=m_xa_wq, m_xa_wk=m_xa_wk, m_xa_wv=m_xa_wv, m_xa_wo=m_xa_wo, m_norm_mlp=m_norm_mlp, m_mlp_w1=m_mlp_w1, m_mlp_w2=m_mlp_w2, m_norm_final=m_norm_final, v_norm_mix=v_norm_mix, v_w_in=v_w_in, v_s5_a_re=v_s5_a_re, v_s5_a_im=v_s5_a_im, v_s5_log_dt=v_s5_log_dt, v_s5_b_re=v_s5_b_re, v_s5_b_im=v_s5_b_im, v_s5_c_re=v_s5_c_re, v_s5_c_im=v_s5_c_im, v_s5_d=v_s5_d, v_s5_w_glu=v_s5_w_glu, v_ssd_conv_w=v_ssd_conv_w, v_ssd_conv_b=v_ssd_conv_b, v_ssd_dt_bias=v_ssd_dt_bias, v_ssd_a_log=v_ssd_a_log, v_ssd_d=v_ssd_d, v_ssd_norm=v_ssd_norm, v_w_out=v_w_out, v_norm_xattn=v_norm_xattn, v_norm_mem=v_norm_mem, v_xa_wq=v_xa_wq, v_xa_wk=v_xa_wk, v_xa_wv=v_xa_wv, v_xa_wo=v_xa_wo, v_norm_mlp=v_norm_mlp, v_mlp_w1=v_mlp_w1, v_mlp_w2=v_mlp_w2, v_norm_final=v_norm_final)
    weights = {n: given[n] for n in TWIN_WEIGHTS}
    shared = {n: given[n] for n in SHARED_INPUTS}
    per_example = {n: given[n] for n in ['x', 'mem']}
    grad_fn = _jax.value_and_grad(_loss, argnums=(0, 1))

    def one_microbatch(ex, loss_target):
        ex = dict(ex)
        diff = ex.pop(TWIN_DIFF_INPUT)
        return grad_fn(weights, diff, {**shared, **ex}, loss_target)

    if N_MICROBATCH == 1:
        loss, (grad_w, grad_x) = one_microbatch(per_example, given["loss_target"])
    else:
        def body(carry, xs):
            loss_sum, grad_sum = carry
            l_k, (gw_k, gx_k) = one_microbatch(xs[0], xs[1])
            with _jax.named_scope("update"):
                return (loss_sum + l_k, _jax.tree.map(_jnp.add, grad_sum, gw_k)), gx_k

        init = (_jnp.zeros((), _jnp.float32), _jax.tree.map(_jnp.zeros_like, weights))
        (loss, grad_w), grad_x = _jax.lax.scan(body, init, (per_example, given["loss_target"]))
    with _jax.named_scope("update"):
        delta_w, new_m, new_v = {}, {}, {}
        for n in TWIN_WEIGHTS:
            delta_w[n], new_m[n], new_v[n] = _adamw(weights[n], grad_w[n], given["m_" + n], given["v_" + n])
    return (loss, grad_x, *[grad_w[n] for n in TWIN_WEIGHTS], *[delta_w[n] for n in TWIN_WEIGHTS],
            *[new_m[n] for n in TWIN_WEIGHTS], *[new_v[n] for n in TWIN_WEIGHTS])
```

```python
import functools
import math

import jax
import jax.numpy as jnp
import numpy as np
from jax import lax
from jax.experimental import pallas as pl
from jax.experimental.pallas import tpu as pltpu

F32 = jnp.float32
BF16 = jnp.bfloat16
HIGHEST = lax.Precision.HIGHEST

D_MODEL = 1024
DEPTH = 4
D_S5 = 1024
D_SSD = 1024
S5_GROUP = 16
S5_GROUPS = 64
S5_STATE = 64
SSD_HEADS = 16
SSD_HEADDIM = 64
SSD_NGROUPS = 4
SSD_STATE = 128
SSD_CONV = 4
SSD_CHUNK = 128
D_XBC = 2048
D_MAIN = 4096
D_DT_PAD = 128
XA_HEADS = 4
XA_HEAD_DIM = 256
D_FF = 4096
EPS = 1e-5
ADAM_LR, ADAM_B1, ADAM_B2, ADAM_EPS, ADAM_WD, ADAM_STEP = 0.001, 0.9, 0.999, 1e-08, 0.01, 10

VMEM_LIMIT = 56 * 1024 * 1024
MESH_T = pl.DeviceIdType.MESH


def _cp(*sem):
    return pltpu.CompilerParams(dimension_semantics=tuple(sem) if sem else None, vmem_limit_bytes=VMEM_LIMIT)


def _dot(a, b):
    return jnp.dot(a, b, preferred_element_type=F32)


def _dot_nt(a, b):
    return lax.dot_general(a, b, (((1,), (1,)), ((), ())), preferred_element_type=F32)


def _dot_tn(a, b):
    return lax.dot_general(a, b, (((0,), (0,)), ((), ())), preferred_element_type=F32)


def _dot_hi(a, b):
    return jnp.dot(a, b, precision=HIGHEST, preferred_element_type=F32)


def _gelu(x):
    c = math.sqrt(2.0 / math.pi)
    return 0.5 * x * (1.0 + jnp.tanh(c * (x + 0.044715 * x * x * x)))


def _gelu_grad(x):
    c = math.sqrt(2.0 / math.pi)
    t = jnp.tanh(c * (x + 0.044715 * x * x * x))
    return 0.5 * (1.0 + t) + 0.5 * x * (1.0 - t * t) * c * (1.0 + 3 * 0.044715 * x * x)


def _sigmoid(x):
    return 1.0 / (1.0 + jnp.exp(-x))


def _act(a, act):
    if act is None:
        return a.astype(BF16)
    a = a.astype(F32)
    if act == "relu2":
        a = jnp.maximum(a, 0.0)
        return (a * a).astype(BF16)
    if act == "gelu":
        return _gelu(a).astype(BF16)
    raise ValueError(act)


def norm_matmul(x, g, w, w2=None, *, tm, tn, name):
    T, D = x.shape
    N = w.shape[1]
    has2 = w2 is not None

    def body(x_ref, g_ref, w_ref, *rest):
        if has2:
            w2_ref, o_ref, h_ref, r_ref, o2_ref = rest
        else:
            o_ref, h_ref, r_ref = rest
        j = pl.program_id(1)

        @pl.when(j == 0)
        def _():
            xv = x_ref[...]
            r = lax.rsqrt(jnp.mean(xv * xv, axis=-1, keepdims=True) + EPS)
            h = (xv * r * g_ref[...]).astype(BF16)
            h_ref[...] = h
            r_ref[...] = r
            if has2:
                o2_ref[...] = _dot(h, w2_ref[...])

        o_ref[...] = _dot(h_ref[...], w_ref[...]).astype(o_ref.dtype)

    in_specs = [pl.BlockSpec((tm, D), lambda i, j: (i, 0)), pl.BlockSpec((1, D), lambda i, j: (0, 0)),
                pl.BlockSpec((D, tn), lambda i, j: (0, j))]
    out_shape = [jax.ShapeDtypeStruct((T, N), BF16), jax.ShapeDtypeStruct((T, D), BF16),
                 jax.ShapeDtypeStruct((T, 1), F32)]
    out_specs = [pl.BlockSpec((tm, tn), lambda i, j: (i, j)), pl.BlockSpec((tm, D), lambda i, j: (i, 0)),
                 pl.BlockSpec((tm, 1), lambda i, j: (i, 0))]
    args = [x, g, w]
    if has2:
        in_specs.append(pl.BlockSpec((D, D_DT_PAD), lambda i, j: (0, 0)))
        out_shape.append(jax.ShapeDtypeStruct((T, D_DT_PAD), F32))
        out_specs.append(pl.BlockSpec((tm, D_DT_PAD), lambda i, j: (i, 0)))
        args.append(w2)
    return pl.pallas_call(body, grid=(T // tm, N // tn), in_specs=in_specs, out_specs=out_specs,
                          out_shape=out_shape, compiler_params=_cp("parallel", "arbitrary"), name=name)(*args)


def matmul_res(a, w, r, *, act=None, tm, tn, tk, name):
    T, K = a.shape
    N = w.shape[1]
    nk = K // tk

    def body(a_ref, w_ref, r_ref, o_ref):
        k = pl.program_id(2)

        @pl.when(k == 0)
        def _():
            o_ref[...] = r_ref[...]

        o_ref[...] += _dot(_act(a_ref[...], act), w_ref[...])

    return pl.pallas_call(
        body, grid=(T // tm, N // tn, nk),
        in_specs=[pl.BlockSpec((tm, tk), lambda i, j, k: (i, k)), pl.BlockSpec((tk, tn), lambda i, j, k: (k, j)),
                  pl.BlockSpec((tm, tn), lambda i, j, k: (i, j))],
        out_specs=pl.BlockSpec((tm, tn), lambda i, j, k: (i, j)),
        out_shape=jax.ShapeDtypeStruct((T, N), F32),
        compiler_params=_cp("parallel", "parallel", "arbitrary"), name=name)(a, w, r)


def glu_fwd(ys, w, *, tm, name):
    T, N = ys.shape

    def body(y_ref, w_ref, o_ref, t_ref):
        a = _gelu(y_ref[...].astype(F32))
        t = _dot(a.astype(BF16), w_ref[...])
        o_ref[...] = (a * _sigmoid(t)).astype(BF16)
        t_ref[...] = t.astype(BF16)

    return pl.pallas_call(
        body, grid=(T // tm,),
        in_specs=[pl.BlockSpec((tm, N), lambda i: (i, 0)), pl.BlockSpec((N, N), lambda i: (0, 0))],
        out_specs=[pl.BlockSpec((tm, N), lambda i: (i, 0)), pl.BlockSpec((tm, N), lambda i: (i, 0))],
        out_shape=[jax.ShapeDtypeStruct((T, N), BF16), jax.ShapeDtypeStruct((T, N), BF16)],
        compiler_params=_cp("parallel"), name=name)(ys, w)


def glu_bwd_pre(dout, ys, t, *, tm, name):
    T, N = ys.shape

    def body(d_ref, y_ref, t_ref, dt_ref, da_ref):
        d = d_ref[...].astype(F32)
        a = _gelu(y_ref[...].astype(F32))
        s = _sigmoid(t_ref[...].astype(F32))
        dt_ref[...] = (d * a * s * (1.0 - s)).astype(BF16)
        da_ref[...] = (d * s).astype(BF16)

    spec = pl.BlockSpec((tm, N), lambda i: (i, 0))
    return pl.pallas_call(
        body, grid=(T // tm,), in_specs=[spec, spec, spec], out_specs=[spec, spec],
        out_shape=[jax.ShapeDtypeStruct((T, N), BF16), jax.ShapeDtypeStruct((T, N), BF16)],
        compiler_params=_cp("parallel"), name=name)(dout, ys, t)


def matmul_nt(g, w, *, epi=None, epi_args=(), g2=None, w2=None, tm, tko, tn, out_dtype=BF16, name):
    T, N = g.shape
    K = w.shape[0]
    nn = N // tn
    has2 = g2 is not None
    rms = epi == "rmsbwd"
    if rms:
        assert tko == K
    n_epi = len(epi_args)

    def body(*refs):
        g_ref, w_ref = refs[0], refs[1]
        pos = 2
        if has2:
            g2_ref, w2_ref = refs[2], refs[3]
            pos = 4
        e_refs = refs[pos:pos + n_epi]
        pos += n_epi
        o_ref = refs[pos]
        pos += 1
        if rms:
            dg_ref = refs[pos]
            pos += 1
        acc_ref = refs[pos]
        i = pl.program_id(0)
        n = pl.program_id(2)
        part = _dot_nt(g_ref[...].astype(BF16), w_ref[...])

        @pl.when(n == 0)
        def _():
            acc_ref[...] = part

        @pl.when(n > 0)
        def _():
            acc_ref[...] += part

        @pl.when(n == nn - 1)
        def _():
            acc = acc_ref[...]
            if has2:
                acc = acc + _dot_nt(g2_ref[...].astype(BF16), w2_ref[...])
            if epi is None:
                o_ref[...] = acc.astype(o_ref.dtype)
            elif epi == "relu2bwd":
                h1 = e_refs[0][...].astype(F32)
                o_ref[...] = (acc * 2.0 * jnp.maximum(h1, 0.0)).astype(o_ref.dtype)
            elif epi == "glubwd":
                da1 = e_refs[0][...].astype(F32)
                ys = e_refs[1][...].astype(F32)
                o_ref[...] = ((da1 + acc) * _gelu_grad(ys)).astype(o_ref.dtype)
            elif epi == "rmsbwd":
                xv, rs, gain, rv = e_refs[0][...], e_refs[1][...], e_refs[2][...], e_refs[3][...]
                xhat = xv * rs
                gd = acc * gain
                o_ref[...] = rv + rs * (gd - xhat * jnp.mean(gd * xhat, axis=-1, keepdims=True))
                part_g = jnp.sum(acc * xhat, axis=0, keepdims=True)

                @pl.when(i == 0)
                def _():
                    dg_ref[...] = part_g

                @pl.when(i > 0)
                def _():
                    dg_ref[...] += part_g

    in_specs = [pl.BlockSpec((tm, tn), lambda i, k, n: (i, n)), pl.BlockSpec((tko, tn), lambda i, k, n: (k, n))]
    args = [g, w]
    if has2:
        n2 = g2.shape[1]
        in_specs += [pl.BlockSpec((tm, n2), lambda i, k, n: (i, 0)), pl.BlockSpec((tko, n2), lambda i, k, n: (k, 0))]
        args += [g2, w2]
    if epi == "relu2bwd" or epi == "glubwd":
        in_specs += [pl.BlockSpec((tm, tko), lambda i, k, n: (i, k))] * n_epi
    elif rms:
        in_specs += [pl.BlockSpec((tm, K), lambda i, k, n: (i, 0)), pl.BlockSpec((tm, 1), lambda i, k, n: (i, 0)),
                     pl.BlockSpec((1, K), lambda i, k, n: (0, 0)), pl.BlockSpec((tm, K), lambda i, k, n: (i, 0))]
    args += list(epi_args)
    out_shape = [jax.ShapeDtypeStruct((T, K), F32 if rms else out_dtype)]
    out_specs = [pl.BlockSpec((tm, tko), lambda i, k, n: (i, k))]
    if rms:
        out_shape.append(jax.ShapeDtypeStruct((1, K), F32))
        out_specs.append(pl.BlockSpec((1, K), lambda i, k, n: (0, 0)))
    sem = ("arbitrary",) * 3 if rms else ("parallel", "parallel", "arbitrary")
    res = pl.pallas_call(body, grid=(T // tm, K // tko, nn), in_specs=in_specs, out_specs=out_specs,
                         out_shape=out_shape, scratch_shapes=[pltpu.VMEM((tm, tko), F32)],
                         compiler_params=_cp(*sem), name=name)(*args)
    return res if rms else res[0]


def matmul_tn(a, g, *, act=None, tk, tn, tt, name):
    T, K = a.shape
    N = g.shape[1]

    def body(a_ref, g_ref, o_ref):
        t = pl.program_id(2)
        part = _dot_tn(_act(a_ref[...], act), g_ref[...].astype(BF16))

        @pl.when(t == 0)
        def _():
            o_ref[...] = part

        @pl.when(t > 0)
        def _():
            o_ref[...] += part

    return pl.pallas_call(
        body, grid=(K // tk, N // tn, T // tt),
        in_specs=[pl.BlockSpec((tt, tk), lambda k, n, t: (t, k)), pl.BlockSpec((tt, tn), lambda k, n, t: (t, n))],
        out_specs=pl.BlockSpec((tk, tn), lambda k, n, t: (k, n)),
        out_shape=jax.ShapeDtypeStruct((K, N), F32),
        compiler_params=_cp("parallel", "parallel", "arbitrary"), name=name)(a, g)


def attn_fwd(q, kv, *, tm, name):
    T = q.shape[0]
    M = kv.shape[0]
    scale = XA_HEAD_DIM ** -0.5

    def body(q_ref, kv_ref, o_ref):
        for h in range(XA_HEADS):
            sl = slice(h * XA_HEAD_DIM, (h + 1) * XA_HEAD_DIM)
            kh = kv_ref[:, h * XA_HEAD_DIM:(h + 1) * XA_HEAD_DIM]
            vh = kv_ref[:, D_MODEL + h * XA_HEAD_DIM:D_MODEL + (h + 1) * XA_HEAD_DIM]
            s = _dot_nt(q_ref[:, sl], kh) * scale
            s = s - jnp.max(s, axis=-1, keepdims=True)
            p = jnp.exp(s)
            p = p / jnp.sum(p, axis=-1, keepdims=True)
            o_ref[:, sl] = _dot(p.astype(BF16), vh).astype(BF16)

    return pl.pallas_call(
        body, grid=(T // tm,),
        in_specs=[pl.BlockSpec((tm, D_MODEL), lambda i: (i, 0)), pl.BlockSpec((M, 2 * D_MODEL), lambda i: (0, 0))],
        out_specs=pl.BlockSpec((tm, D_MODEL), lambda i: (i, 0)),
        out_shape=jax.ShapeDtypeStruct((T, D_MODEL), BF16),
        compiler_params=_cp("parallel"), name=name)(q, kv)


def attn_bwd(q, kv, do, *, tm, name):
    T = q.shape[0]
    M = kv.shape[0]
    scale = XA_HEAD_DIM ** -0.5

    def body(q_ref, kv_ref, do_ref, dq_ref, dkv_ref):
        i = pl.program_id(0)

        @pl.when(i == 0)
        def _():
            dkv_ref[...] = jnp.zeros_like(dkv_ref)

        for h in range(XA_HEADS):
            sl = slice(h * XA_HEAD_DIM, (h + 1) * XA_HEAD_DIM)
            slv = slice(D_MODEL + h * XA_HEAD_DIM, D_MODEL + (h + 1) * XA_HEAD_DIM)
            qh = q_ref[:, sl]
            kh = kv_ref[:, sl]
            vh = kv_ref[:, slv]
            doh = do_ref[:, sl]
            s = _dot_nt(qh, kh) * scale
            s = s - jnp.max(s, axis=-1, keepdims=True)
            p = jnp.exp(s)
            p = p / jnp.sum(p, axis=-1, keepdims=True)
            pb = p.astype(BF16)
            dkv_ref[:, slv] += _dot_tn(pb, doh)
            dp = _dot_nt(doh, vh)
            ds = (p * (dp - jnp.sum(dp * p, axis=-1, keepdims=True)) * scale).astype(BF16)
            dq_ref[:, sl] = _dot(ds, kh).astype(BF16)
            dkv_ref[:, sl] += _dot_tn(ds, qh)

    spec = pl.BlockSpec((tm, D_MODEL), lambda i: (i, 0))
    kvspec = pl.BlockSpec((M, 2 * D_MODEL), lambda i: (0, 0))
    return pl.pallas_call(
        body, grid=(T // tm,), in_specs=[spec, kvspec, spec], out_specs=[spec, kvspec],
        out_shape=[jax.ShapeDtypeStruct((T, D_MODEL), BF16), jax.ShapeDtypeStruct((M, 2 * D_MODEL), F32)],
        compiler_params=_cp("arbitrary"), name=name)(q, kv, do)


def loss_head(x, g, target, *, tm, name):
    T, D = x.shape

    def body(x_ref, g_ref, t_ref, l_ref, dx_ref, dg_ref):
        i = pl.program_id(0)
        xv = x_ref[...]
        gain = g_ref[...]
        r = lax.rsqrt(jnp.mean(xv * xv, axis=-1, keepdims=True) + EPS)
        xhat = xv * r
        err = xhat * gain - t_ref[...]
        part_l = jnp.full((1, 128), 0.5 / D, F32) * jnp.sum(err * err)
        dy = err * (1.0 / D)
        gd = dy * gain
        dx_ref[...] = r * (gd - xhat * jnp.mean(gd * xhat, axis=-1, keepdims=True))
        part_g = jnp.sum(dy * xhat, axis=0, keepdims=True)

        @pl.when(i == 0)
        def _():
            l_ref[...] = part_l
            dg_ref[...] = part_g

        @pl.when(i > 0)
        def _():
            l_ref[...] += part_l
            dg_ref[...] += part_g

    spec = pl.BlockSpec((tm, D), lambda i: (i, 0))
    return pl.pallas_call(
        body, grid=(T // tm,), in_specs=[spec, pl.BlockSpec((1, D), lambda i: (0, 0)), spec],
        out_specs=[pl.BlockSpec((1, 128), lambda i: (0, 0)), spec, pl.BlockSpec((1, D), lambda i: (0, 0))],
        out_shape=[jax.ShapeDtypeStruct((1, 128), F32), jax.ShapeDtypeStruct((T, D), F32),
                   jax.ShapeDtypeStruct((1, D), F32)],
        compiler_params=_cp("arbitrary"), name=name)(x, g, target)


S5_LS = 128
S5_SEG = 8
S5_TB = S5_LS * S5_SEG
S5_CH = 8
S5_CW = 128
S5_NST = 512


def _cmul(ar, ai, br, bi):
    return ar * br - ai * bi, ar * bi + ai * br


def s5_params_fwd(a_re, a_im, log_dt, b_re, b_im, expand, *, name):
    G, P = a_re.shape

    def body(ar_ref, ai_ref, ld_ref, br_ref, bi_ref, e_ref, abr_ref, abi_ref, apr_ref, api_ref, bbr_ref, bbi_ref):
        ar, ai = ar_ref[...], ai_ref[...]
        dt = jnp.exp(ld_ref[...])
        mag = jnp.exp(dt * ar)
        abr, abi = mag * jnp.cos(dt * ai), mag * jnp.sin(dt * ai)
        den = ar * ar + ai * ai
        zr, zi = abr - 1.0, abi
        fr = (zr * ar + zi * ai) / den
        fi = (zi * ar - zr * ai) / den
        frx, fix = _dot_hi(fr, e_ref[...]), _dot_hi(fi, e_ref[...])
        br, bi = br_ref[...], bi_ref[...]
        bbr_ref[...] = frx * br - fix * bi
        bbi_ref[...] = frx * bi + fix * br
        abr_ref[...] = abr
        abi_ref[...] = abi
        pr, pi = abr, abi
        for _ in range(int(math.log2(S5_LS))):
            pr, pi = _cmul(pr, pi, pr, pi)
        apr_ref[...] = pr
        api_ref[...] = pi

    small = jax.ShapeDtypeStruct((G, P), F32)
    big = jax.ShapeDtypeStruct(b_re.shape, F32)
    return pl.pallas_call(body, out_shape=[small, small, small, small, big, big], name=name)(
        a_re, a_im, log_dt, b_re, b_im, expand)


def s5_params_bwd(a_re, a_im, log_dt, b_re, b_im, g_abr, g_abi, g_bbr, g_bbi, expand, expand_t, *, name):
    G, P = a_re.shape

    def body(ar_ref, ai_ref, ld_ref, br_ref, bi_ref, gar_ref, gai_ref, gbr_ref, gbi_ref, e_ref, et_ref,
             dar_ref, dai_ref, dld_ref, dbr_ref, dbi_ref):
        ar, ai = ar_ref[...], ai_ref[...]
        dt = jnp.exp(ld_ref[...])
        mag = jnp.exp(dt * ar)
        cs, sn = jnp.cos(dt * ai), jnp.sin(dt * ai)
        abr, abi = mag * cs, mag * sn
        den = ar * ar + ai * ai
        zr, zi = abr - 1.0, abi
        fr = (zr * ar + zi * ai) / den
        fi = (zi * ar - zr * ai) / den
        frx, fix = _dot_hi(fr, e_ref[...]), _dot_hi(fi, e_ref[...])
        br, bi = br_ref[...], bi_ref[...]
        gbr, gbi = gbr_ref[...], gbi_ref[...]
        dbr_ref[...] = frx * gbr + fix * gbi
        dbi_ref[...] = -fix * gbr + frx * gbi
        gfr = _dot_hi(br * gbr + bi * gbi, et_ref[...])
        gfi = _dot_hi(-bi * gbr + br * gbi, et_ref[...])
        g_zr = (gfr * ar - gfi * ai) / den
        g_zi = (gfr * ai + gfi * ar) / den
        g_ar = gfr * (zr - fr * 2.0 * ar) / den + gfi * (zi - fi * 2.0 * ar) / den
        g_ai = gfr * (zi - fr * 2.0 * ai) / den + gfi * (-zr - fi * 2.0 * ai) / den
        t_abr = gar_ref[...] + g_zr
        t_abi = gai_ref[...] + g_zi
        g_mag = t_abr * cs + t_abi * sn
        g_th = mag * (-t_abr * sn + t_abi * cs)
        dar_ref[...] = g_ar + g_mag * mag * dt
        dai_ref[...] = g_ai + g_th * dt
        g_dt = jnp.sum(g_mag * mag * ar + g_th * ai, axis=-1, keepdims=True)
        dld_ref[...] = g_dt * dt

    small = jax.ShapeDtypeStruct((G, P), F32)
    big = jax.ShapeDtypeStruct(b_re.shape, F32)
    return pl.pallas_call(body, out_shape=[small, small, jax.ShapeDtypeStruct((G, 1), F32), big, big], name=name)(
        a_re, a_im, log_dt, b_re, b_im, g_abr, g_abi, g_bbr, g_bbi, expand, expand_t)


def _s5_permute_in(src_ref, dst_ref):
    for i in range(S5_LS):
        dst_ref[pl.ds(8 * i, 8), :] = src_ref[pl.ds(i, 8, stride=S5_LS), :]


def _s5_permute_out(src_ref, dst_ref):
    for r in range(S5_SEG):
        for k in range(S5_LS // 8):
            dst_ref[pl.ds(r * S5_LS + 8 * k, 8), :] = src_ref[pl.ds(64 * k + r, 8, stride=8), :]


def _s5_scan(a_r, a_i, dr_ref, di_ref, init_r, init_i, store=None, reverse=False, conj=False):
    sgn = -1.0 if conj else 1.0

    def step(n, c):
        sr, si = c
        i = S5_LS - 1 - n if reverse else n
        nr = a_r * sr - sgn * a_i * si + dr_ref[i]
        ni = a_r * si + sgn * a_i * sr + di_ref[i]
        if store is not None:
            store(i, nr, ni, sr, si)
        return nr, ni

    return lax.fori_loop(0, S5_LS, step, (init_r, init_i), unroll=2)


def _s5_stitch(apr, api, fin_r, fin_i, car_r, car_i, reverse=False, conj=False):
    sgn = -1.0 if conj else 1.0
    rows_r, rows_i = [None] * S5_SEG, [None] * S5_SEG
    order = range(S5_SEG - 1, -1, -1) if reverse else range(S5_SEG)
    for r in order:
        rows_r[r], rows_i[r] = car_r, car_i
        fr, fi = fin_r[r:r + 1], fin_i[r:r + 1]
        car_r, car_i = (apr * car_r - sgn * api * car_i + fr, apr * car_i + sgn * api * car_r + fi)
    return jnp.concatenate(rows_r, 0), jnp.concatenate(rows_i, 0), car_r, car_i


def _s5_specs(nb, rev):
    blk = (lambda c, b: (nb - 1 - b, c)) if rev else (lambda c, b: (b, c))
    tok = pl.BlockSpec((S5_TB, S5_CW), blk)
    par_b = pl.BlockSpec((1, S5_CW, S5_NST), lambda c, b: (c, 0, 0))
    par_c = pl.BlockSpec((1, S5_NST, S5_CW), lambda c, b: (c, 0, 0))
    vec_s = pl.BlockSpec((1, 1, S5_NST), lambda c, b: (c, 0, 0))
    vec_c = pl.BlockSpec((1, 1, S5_CW), lambda c, b: (c, 0, 0))
    return tok, par_b, par_c, vec_s, vec_c


def s5_fwd(proj, bbr, bbi, ccr, cci, abr, abi, apr, api, dskip, *, name):
    T = proj.shape[0]
    nb = T // S5_TB
    zeros8 = functools.partial(jnp.zeros, (S5_SEG, S5_NST), F32)

    def body(u_ref, bbr_ref, bbi_ref, ccr_ref, cci_ref, ar_ref, ai_ref, apr_ref, api_ref, d_ref,
             y_ref, csr_ref, csi_ref, uf_ref, up_ref, dr_ref, di_ref, sr_ref, si_ref, yp_ref, car_ref, cai_ref):
        b = pl.program_id(1)

        @pl.when(b == 0)
        def _():
            car_ref[...] = jnp.zeros_like(car_ref)
            cai_ref[...] = jnp.zeros_like(cai_ref)

        csr_ref[0, 0] = car_ref[...]
        csi_ref[0, 0] = cai_ref[...]
        uf_ref[...] = u_ref[...].astype(F32)
        _s5_permute_in(uf_ref, up_ref)
        upb = up_ref[...].astype(BF16)
        dr_ref[...] = _dot(upb, bbr_ref[0]).reshape(S5_LS, S5_SEG, S5_NST)
        di_ref[...] = _dot(upb, bbi_ref[0]).reshape(S5_LS, S5_SEG, S5_NST)
        a_r = jnp.broadcast_to(ar_ref[0], (S5_SEG, S5_NST))
        a_i = jnp.broadcast_to(ai_ref[0], (S5_SEG, S5_NST))
        fin_r, fin_i = _s5_scan(a_r, a_i, dr_ref, di_ref, zeros8(), zeros8())
        cin_r, cin_i, ncr, nci = _s5_stitch(apr_ref[0], api_ref[0], fin_r, fin_i, car_ref[...], cai_ref[...])
        car_ref[...] = ncr
        cai_ref[...] = nci

        def store(i, nr, ni, sr, si):
            sr_ref[i] = nr
            si_ref[i] = ni

        _s5_scan(a_r, a_i, dr_ref, di_ref, cin_r, cin_i, store=store)
        s_r = sr_ref[...].reshape(S5_TB, S5_NST).astype(BF16)
        s_i = si_ref[...].reshape(S5_TB, S5_NST).astype(BF16)
        yp_ref[...] = _dot(s_r, ccr_ref[0]) - _dot(s_i, cci_ref[0]) + d_ref[0] * up_ref[...]
        _s5_permute_out(yp_ref, uf_ref)
        y_ref[...] = uf_ref[...].astype(BF16)

    tok, par_b, par_c, vec_s, vec_c = _s5_specs(nb, False)
    cs_spec = pl.BlockSpec((1, 1, 1, S5_NST), lambda c, b: (b, c, 0, 0))
    cs_shape = jax.ShapeDtypeStruct((nb, S5_CH, 1, S5_NST), F32)
    tokbuf = pltpu.VMEM((S5_TB, S5_CW), F32)
    stbuf = pltpu.VMEM((S5_LS, S5_SEG, S5_NST), F32)
    return pl.pallas_call(
        body, grid=(S5_CH, nb),
        in_specs=[tok, par_b, par_b, par_c, par_c, vec_s, vec_s, vec_s, vec_s, vec_c],
        out_specs=[tok, cs_spec, cs_spec],
        out_shape=[jax.ShapeDtypeStruct((T, D_S5), BF16), cs_shape, cs_shape],
        scratch_shapes=[tokbuf, tokbuf, stbuf, stbuf, stbuf, stbuf, tokbuf,
                        pltpu.VMEM((1, S5_NST), F32), pltpu.VMEM((1, S5_NST), F32)],
        compiler_params=_cp("parallel", "arbitrary"), name=name)(
            proj, bbr, bbi, ccr, cci, abr, abi, apr, api, dskip)


def s5_bwd(proj, dys, csr, csi, bbr, bbi, ccr, cci, abr, abi, apr, api, dskip, *, name):
    T = proj.shape[0]
    nb = T // S5_TB
    zeros8 = functools.partial(jnp.zeros, (S5_SEG, S5_NST), F32)

    def body(u_ref, gy_ref, csr_ref, csi_ref, bbr_ref, bbi_ref, ccr_ref, cci_ref, ar_ref, ai_ref, apr_ref, api_ref,
             d_ref, du_ref, dbr_ref, dbi_ref, dcr_ref, dci_ref, dd_ref, dar_ref, dai_ref,
             tmp_ref, up_ref, gyp_ref, dr_ref, di_ref, sr_ref, si_ref, gr_ref, gi_ref, car_ref, cai_ref):
        b = pl.program_id(1)

        @pl.when(b == 0)
        def _():
            car_ref[...] = jnp.zeros_like(car_ref)
            cai_ref[...] = jnp.zeros_like(cai_ref)
            for ref in (dbr_ref, dbi_ref, dcr_ref, dci_ref, dd_ref, dar_ref, dai_ref):
                ref[...] = jnp.zeros_like(ref)

        tmp_ref[...] = u_ref[...].astype(F32)
        _s5_permute_in(tmp_ref, up_ref)
        tmp_ref[...] = gy_ref[...].astype(F32)
        _s5_permute_in(tmp_ref, gyp_ref)
        upb = up_ref[...].astype(BF16)
        gyp = gyp_ref[...]
        gypb = gyp.astype(BF16)
        dr_ref[...] = _dot(upb, bbr_ref[0]).reshape(S5_LS, S5_SEG, S5_NST)
        di_ref[...] = _dot(upb, bbi_ref[0]).reshape(S5_LS, S5_SEG, S5_NST)
        a_r = jnp.broadcast_to(ar_ref[0], (S5_SEG, S5_NST))
        a_i = jnp.broadcast_to(ai_ref[0], (S5_SEG, S5_NST))
        fin_r, fin_i = _s5_scan(a_r, a_i, dr_ref, di_ref, zeros8(), zeros8())
        cin_r, cin_i, _, _ = _s5_stitch(apr_ref[0], api_ref[0], fin_r, fin_i, csr_ref[0, 0], csi_ref[0, 0])
        sr_ref[0] = cin_r
        si_ref[0] = cin_i

        def store_s(i, nr, ni, sr, si):
            sr_ref[i + 1] = nr
            si_ref[i + 1] = ni

        _s5_scan(a_r, a_i, dr_ref, di_ref, cin_r, cin_i, store=store_s)
        s_r = sr_ref[pl.ds(1, S5_LS)].reshape(S5_TB, S5_NST).astype(BF16)
        s_i = si_ref[pl.ds(1, S5_LS)].reshape(S5_TB, S5_NST).astype(BF16)
        dcr_ref[0] += _dot_tn(s_r, gypb)
        dci_ref[0] -= _dot_tn(s_i, gypb)
        dd_ref[0] += jnp.sum(gyp * up_ref[...], axis=0, keepdims=True)
        dr_ref[...] = _dot_nt(gypb, ccr_ref[0]).reshape(S5_LS, S5_SEG, S5_NST)
        di_ref[...] = (-_dot_nt(gypb, cci_ref[0])).reshape(S5_LS, S5_SEG, S5_NST)
        fin_r, fin_i = _s5_scan(a_r, a_i, dr_ref, di_ref, zeros8(), zeros8(), reverse=True, conj=True)
        gin_r, gin_i, ncr, nci = _s5_stitch(apr_ref[0], api_ref[0], fin_r, fin_i, car_ref[...], cai_ref[...],
                                            reverse=True, conj=True)
        car_ref[...] = ncr
        cai_ref[...] = nci
        gr_ref[S5_LS] = zeros8()
        gi_ref[S5_LS] = zeros8()

        def store_g(i, nr, ni, sr, si):
            gr_ref[i] = nr
            gi_ref[i] = ni
            pr, pi = sr_ref[i], si_ref[i]
            gr_ref[S5_LS] += nr * pr + ni * pi
            gi_ref[S5_LS] += ni * pr - nr * pi

        _s5_scan(a_r, a_i, dr_ref, di_ref, gin_r, gin_i, store=store_g, reverse=True, conj=True)
        dar_ref[0] += jnp.sum(gr_ref[S5_LS], axis=0, keepdims=True)
        dai_ref[0] += jnp.sum(gi_ref[S5_LS], axis=0, keepdims=True)
        g_r = gr_ref[pl.ds(0, S5_LS)].reshape(S5_TB, S5_NST).astype(BF16)
        g_i = gi_ref[pl.ds(0, S5_LS)].reshape(S5_TB, S5_NST).astype(BF16)
        dbr_ref[0] += _dot_tn(upb, g_r)
        dbi_ref[0] += _dot_tn(upb, g_i)
        gyp_ref[...] = _dot_nt(g_r, bbr_ref[0]) + _dot_nt(g_i, bbi_ref[0]) + d_ref[0] * gyp
        _s5_permute_out(gyp_ref, tmp_ref)
        du_ref[...] = tmp_ref[...].astype(BF16)

    tok, par_b, par_c, vec_s, vec_c = _s5_specs(nb, True)
    cs_spec = pl.BlockSpec((1, 1, 1, S5_NST), lambda c, b: (nb - 1 - b, c, 0, 0))
    tokbuf = pltpu.VMEM((S5_TB, S5_CW), F32)
    stbuf = pltpu.VMEM((S5_LS, S5_SEG, S5_NST), F32)
    stbuf1 = pltpu.VMEM((S5_LS + 1, S5_SEG, S5_NST), F32)
    return pl.pallas_call(
        body, grid=(S5_CH, nb),
        in_specs=[tok, tok, cs_spec, cs_spec, par_b, par_b, par_c, par_c, vec_s, vec_s, vec_s, vec_s, vec_c],
        out_specs=[tok, par_b, par_b, par_c, par_c, vec_c, vec_s, vec_s],
        out_shape=[jax.ShapeDtypeStruct((T, D_S5), BF16),
                   jax.ShapeDtypeStruct((S5_CH, S5_CW, S5_NST), F32), jax.ShapeDtypeStruct((S5_CH, S5_CW, S5_NST), F32),
                   jax.ShapeDtypeStruct((S5_CH, S5_NST, S5_CW), F32), jax.ShapeDtypeStruct((S5_CH, S5_NST, S5_CW), F32),
                   jax.ShapeDtypeStruct((S5_CH, 1, S5_CW), F32),
                   jax.ShapeDtypeStruct((S5_CH, 1, S5_NST), F32), jax.ShapeDtypeStruct((S5_CH, 1, S5_NST), F32)],
        scratch_shapes=[tokbuf, tokbuf, tokbuf, stbuf, stbuf, stbuf1, stbuf1, stbuf1, stbuf1,
                        pltpu.VMEM((1, S5_NST), F32), pltpu.VMEM((1, S5_NST), F32)],
        compiler_params=_cp("parallel", "arbitrary"), name=name)(
            proj, dys, csr, csi, bbr, bbi, ccr, cci, abr, abi, apr, api, dskip)


SSD_L = SSD_CHUNK
SSD_GW = 256
NEG = -1e30


def _expand16(v):
    lane = lax.broadcasted_iota(jnp.int32, (v.shape[0], 128), 1)
    parts = [jnp.where(lane < SSD_HEADDIM, v[:, 2 * j:2 * j + 1], v[:, 2 * j + 1:2 * j + 2]) for j in range(8)]
    return jnp.concatenate(parts, axis=1)


def _softplus(x):
    return jnp.maximum(x, 0.0) + jnp.log(1.0 + jnp.exp(-jnp.abs(x)))


def _ssd_chunk_fwd(z, xbc, tail, dtraw, hprev, cw, cb, dtb, alog, dsk, nw, tril):
    L = SSD_L
    f = {}
    xe = jnp.concatenate([tail, xbc], axis=0)
    sh = [xbc] + [pltpu.roll(xe, s, 0)[8:] for s in (1, 2, 3)]
    conv = cb + cw[3:4] * sh[0] + cw[2:3] * sh[1] + cw[1:2] * sh[2] + cw[0:1] * sh[3]
    sig = _sigmoid(conv)
    xa = conv * sig
    xs, bm, cm = xa[:, :D_SSD], xa[:, D_SSD:D_SSD + 512], xa[:, D_SSD + 512:]
    pre = dtraw + dtb
    dt = _softplus(pre)
    a_h = -jnp.exp(alog)
    acum = _dot_hi(tril, dt * a_h)
    acum_t = acum.T
    alast = acum[L - 1:L]
    exp_a = jnp.exp(acum)
    dec = jnp.exp(alast - acum)
    exp_al = jnp.exp(alast)
    dt_x, dec_x, exp_a_x, exp_al_x = _expand16(dt), _expand16(dec), _expand16(exp_a), _expand16(exp_al)
    d_x = _expand16(dsk)
    xh = xs * dt_x
    xhb = xh.astype(BF16)
    xd = (xh * dec_x).astype(BF16)
    row = lax.broadcasted_iota(jnp.int32, (L, L), 0)
    col = lax.broadcasted_iota(jnp.int32, (L, L), 1)
    causal = row >= col
    lane = lax.broadcasted_iota(jnp.int32, (L, 128), 1)
    low = lane < SSD_HEADDIM
    hb = hprev.astype(BF16)
    y_pairs, yoff_parts, st_parts, cbs, lms = [], [], [], [], []
    for g in range(SSD_NGROUPS):
        bg = bm[:, g * 128:(g + 1) * 128].astype(BF16)
        cg = cm[:, g * 128:(g + 1) * 128].astype(BF16)
        cbg = _dot_nt(cg, bg)
        cbs.append(cbg)
        for j in (2 * g, 2 * g + 1):
            xp = xhb[:, j * 128:(j + 1) * 128]
            ys = []
            for h in (2 * j, 2 * j + 1):
                lm = jnp.exp(jnp.where(causal, acum[:, h:h + 1] - acum_t[h:h + 1, :], NEG))
                lms.append(lm)
                ys.append(_dot((cbg * lm).astype(BF16), xp))
            y_pairs.append(jnp.where(low, ys[0], ys[1]))
        gs = slice(g * SSD_GW, (g + 1) * SSD_GW)
        yoff_parts.append(_dot(cg, hb[:, gs]) * exp_a_x[:, gs])
        st_parts.append(_dot_tn(bg, xd[:, gs]))
    yoff = jnp.concatenate(yoff_parts, axis=1)
    y = jnp.concatenate(y_pairs, axis=1) + yoff + d_x * xs
    hnew = exp_al_x * hprev + jnp.concatenate(st_parts, axis=1)
    sz = _sigmoid(z)
    gz = y * (z * sz)
    r = lax.rsqrt(jnp.mean(gz * gz, axis=-1, keepdims=True) + EPS)
    out = gz * r * nw
    f.update(sh=sh, conv=conv, sig=sig, xs=xs, bm=bm, cm=cm, pre=pre, dt=dt, a_h=a_h, exp_a_x=exp_a_x, dec_x=dec_x,
             exp_al=exp_al, exp_al_x=exp_al_x, dt_x=dt_x, d_x=d_x, xh=xh, xhb=xhb, xd=xd, causal=causal, low=low, hb=hb,
             cbs=cbs, lms=lms, yoff=yoff, y=y, sz=sz, gz=gz, r=r)
    return out, hnew, f


def _ssd_params(conv_w, conv_b, dt_bias, a_log, d_skip, norm_w):
    pad16 = lambda v: jnp.pad(v.reshape(1, SSD_HEADS), ((0, 0), (0, 128 - SSD_HEADS)))
    return (jnp.pad(conv_w, ((0, 8 - SSD_CONV), (0, 0))), conv_b.reshape(1, D_XBC), pad16(dt_bias), pad16(a_log),
            pad16(d_skip), norm_w.reshape(1, D_SSD))


def _ssd_param_specs():
    full = lambda shape: pl.BlockSpec(shape, lambda i: (0, 0))
    return [full((8, D_XBC)), full((1, D_XBC)), full((1, 128)), full((1, 128)), full((1, 128)), full((1, D_SSD))]


def ssd_fwd(proj, dtraw, params, tril, *, name):
    T = proj.shape[0]
    nc = T // SSD_L

    def body(z_ref, x_ref, dt_ref, cw_ref, cb_ref, dtb_ref, al_ref, dsk_ref, nw_ref, tril_ref,
             o_ref, hs_ref, h_ref, tail_ref):
        i = pl.program_id(0)

        @pl.when(i == 0)
        def _():
            h_ref[...] = jnp.zeros_like(h_ref)
            tail_ref[...] = jnp.zeros_like(tail_ref)

        xbc = x_ref[...].astype(F32)
        hprev = h_ref[...]
        hs_ref[0] = hprev
        out, hnew, _ = _ssd_chunk_fwd(z_ref[...].astype(F32), xbc, tail_ref[...], dt_ref[...], hprev, cw_ref[...],
                                      cb_ref[...], dtb_ref[...], al_ref[...], dsk_ref[...], nw_ref[...], tril_ref[...])
        o_ref[...] = out.astype(BF16)
        h_ref[...] = hnew
        tail_ref[...] = xbc[SSD_L - 8:]

    return pl.pallas_call(
        body, grid=(nc,),
        in_specs=[pl.BlockSpec((SSD_L, D_SSD), lambda i: (i, 1)), pl.BlockSpec((SSD_L, D_XBC), lambda i: (i, 1)),
                  pl.BlockSpec((SSD_L, 128), lambda i: (i, 0))] + _ssd_param_specs()
                 + [pl.BlockSpec((SSD_L, SSD_L), lambda i: (0, 0))],
        out_specs=[pl.BlockSpec((SSD_L, D_SSD), lambda i: (i, 0)),
                   pl.BlockSpec((1, SSD_STATE, D_SSD), lambda i: (i, 0, 0))],
        out_shape=[jax.ShapeDtypeStruct((T, D_SSD), BF16), jax.ShapeDtypeStruct((nc, SSD_STATE, D_SSD), F32)],
        scratch_shapes=[pltpu.VMEM((SSD_STATE, D_SSD), F32), pltpu.VMEM((8, D_XBC), F32)],
        compiler_params=_cp("arbitrary"), name=name)(proj, proj, dtraw, *params, tril)


def ssd_bwd(proj, dtraw, hs, dout, params, tril, triu, trils, headsum, *, name):
    T = proj.shape[0]
    nc = T // SSD_L
    L = SSD_L

    def body(z_ref, x_ref, xprev_ref, dt_ref, hs_ref, do_ref, cw_ref, cb_ref, dtb_ref, al_ref, dsk_ref, nw_ref,
             tril_ref, triu_ref, trils_ref, hsum_ref,
             dz_ref, dx_ref, ddt_ref, dcw_ref, dcb_ref, ddtb_ref, dal_ref, ddsk_ref, dnw_ref, dh_ref, dnext_ref):
        i = pl.program_id(0)

        @pl.when(i == 0)
        def _():
            dh_ref[...] = jnp.zeros_like(dh_ref)
            dnext_ref[...] = jnp.zeros_like(dnext_ref)
            for ref in (dcw_ref, dcb_ref, ddtb_ref, dal_ref, ddsk_ref, dnw_ref):
                ref[...] = jnp.zeros_like(ref)

        z = z_ref[...].astype(F32)
        xbc = x_ref[...].astype(F32)
        tail = jnp.where(i == nc - 1, 0.0, xprev_ref[...].astype(F32))
        hprev = hs_ref[0]
        cw, nw = cw_ref[...], nw_ref[...]
        hsum = hsum_ref[...]
        _, _, f = _ssd_chunk_fwd(z, xbc, tail, dt_ref[...], hprev, cw, cb_ref[...], dtb_ref[...], al_ref[...],
                                 dsk_ref[...], nw, tril_ref[...])
        dout = do_ref[...].astype(F32)
        dh = dh_ref[...]
        ghat = f["gz"] * f["r"]
        dn = dout * nw
        dgz = f["r"] * (dn - ghat * jnp.mean(dn * ghat, axis=-1, keepdims=True))
        dnw_ref[...] += jnp.sum(dout * ghat, axis=0, keepdims=True)
        sz = f["sz"]
        dy = dgz * (z * sz)
        dz_ref[...] = (dgz * f["y"] * sz * (1.0 + z * (1.0 - sz))).astype(BF16)
        xs = f["xs"]
        ddsk_ref[...] += jnp.sum(_dot_hi(dy * xs, hsum), axis=0, keepdims=True)
        dyb = dy.astype(BF16)
        dye = (dy * f["exp_a_x"]).astype(BF16)
        dhb = dh.astype(BF16)
        lane = lax.broadcasted_iota(jnp.int32, (L, 128), 1)
        sub = lax.broadcasted_iota(jnp.int32, (128, L), 0)
        zero_b = jnp.zeros((L, 128), BF16)
        rsum = jnp.zeros((L, 128), F32)
        csum_t = jnp.zeros((128, L), F32)
        dx_pairs, dxst_parts, db_parts, dc_parts, dhp_parts = [], [], [], [], []
        for g in range(SSD_NGROUPS):
            gs = slice(g * SSD_GW, (g + 1) * SSD_GW)
            bg = f["bm"][:, g * 128:(g + 1) * 128].astype(BF16)
            cg = f["cm"][:, g * 128:(g + 1) * 128].astype(BF16)
            cbg = f["cbs"][g]
            dcb_g = jnp.zeros((L, L), F32)
            for j in (2 * g, 2 * g + 1):
                xp = f["xhb"][:, j * 128:(j + 1) * 128]
                dyp = dyb[:, j * 128:(j + 1) * 128]
                dxs = []
                for half, h in enumerate((2 * j, 2 * j + 1)):
                    lm = f["lms"][h]
                    dyh = jnp.where(f["low"], dyp, zero_b) if half == 0 else jnp.where(f["low"], zero_b, dyp)
                    dw = jnp.where(f["causal"], _dot_nt(dyh, xp), 0.0)
                    w = cbg * lm
                    e = dw * w
                    dcb_g = dcb_g + dw * lm
                    rsum = jnp.where(lane == h, jnp.sum(e, axis=1, keepdims=True), rsum)
                    csum_t = jnp.where(sub == h, jnp.sum(e, axis=0, keepdims=True), csum_t)
                    dxs.append(_dot_tn(w.astype(BF16), dyp))
                dx_pairs.append(jnp.where(f["low"], dxs[0], dxs[1]))
            dcbb = dcb_g.astype(BF16)
            dxst_parts.append(f["dec_x"][:, gs] * _dot(bg, dhb[:, gs]))
            dc_parts.append(_dot(dcbb, bg) + _dot_nt(dye[:, gs], f["hb"][:, gs]))
            db_parts.append(_dot_tn(dcbb, cg) + _dot_nt(f["xd"][:, gs], dhb[:, gs]))
            dhp_parts.append(f["exp_al_x"][:, gs] * dh[:, gs] + _dot_tn(cg, dye[:, gs]))
        dxst = jnp.concatenate(dxst_parts, axis=1)
        dxh = jnp.concatenate(dx_pairs, axis=1) + dxst
        q = _dot_hi(f["yoff"] * dy, hsum)
        dstate = _dot_hi(f["xh"] * dxst, hsum)
        h0t = jnp.sum(_dot_hi(dh * hprev, hsum), axis=0, keepdims=True) * f["exp_al"]
        da = _dot_hi(triu_ref[...], rsum - csum_t.T + q) + _dot_hi(trils_ref[...], dstate) + h0t
        dt, a_h = f["dt"], f["a_h"]
        ddt = _dot_hi(dxh * xs, hsum) + da * a_h
        dal_ref[...] += jnp.sum(da * dt, axis=0, keepdims=True) * a_h
        ddtraw = ddt * _sigmoid(f["pre"])
        first16 = lane < SSD_HEADS
        ddtraw = jnp.where(first16, ddtraw, 0.0)
        ddt_ref[...] = ddtraw
        ddtb_ref[...] += jnp.sum(ddtraw, axis=0, keepdims=True)
        dh_ref[...] = jnp.concatenate(dhp_parts, axis=1)
        dxa = jnp.concatenate([dxh * f["dt_x"] + f["d_x"] * dy] + db_parts + dc_parts, axis=1)
        sig, conv = f["sig"], f["conv"]
        dconv = dxa * sig * (1.0 + conv * (1.0 - sig))
        dcb_ref[...] += jnp.sum(dconv, axis=0, keepdims=True)
        rows = [jnp.sum(dconv * f["sh"][3 - k], axis=0, keepdims=True) for k in range(SSD_CONV)]
        dcw_ref[...] += jnp.concatenate(rows + [jnp.zeros((8 - SSD_CONV, D_XBC), F32)], axis=0)
        de = jnp.concatenate([dconv, dnext_ref[...]], axis=0)
        dxbc = cw[3:4] * dconv
        for s in (1, 2, 3):
            dxbc = dxbc + cw[3 - s:4 - s] * pltpu.roll(de, L + 8 - s, 0)[:L]
        dx_ref[...] = dxbc.astype(BF16)
        dnext_ref[...] = dconv[:8]

    rev = lambda i: nc - 1 - i
    acc = lambda shape: pl.BlockSpec(shape, lambda i: (0, 0))
    tri = pl.BlockSpec((L, L), lambda i: (0, 0))
    return pl.pallas_call(
        body, grid=(nc,),
        in_specs=[pl.BlockSpec((L, D_SSD), lambda i: (rev(i), 1)), pl.BlockSpec((L, D_XBC), lambda i: (rev(i), 1)),
                  pl.BlockSpec((8, D_XBC), lambda i: (jnp.maximum(rev(i) * (L // 8) - 1, 0), 1)),
                  pl.BlockSpec((L, 128), lambda i: (rev(i), 0)),
                  pl.BlockSpec((1, SSD_STATE, D_SSD), lambda i: (rev(i), 0, 0)),
                  pl.BlockSpec((L, D_SSD), lambda i: (rev(i), 0))] + _ssd_param_specs()
                 + [tri, tri, tri, pl.BlockSpec((D_SSD, 128), lambda i: (0, 0))],
        out_specs=[pl.BlockSpec((L, D_SSD), lambda i: (rev(i), 0)), pl.BlockSpec((L, D_XBC), lambda i: (rev(i), 0)),
                   pl.BlockSpec((L, 128), lambda i: (rev(i), 0)),
                   acc((8, D_XBC)), acc((1, D_XBC)), acc((1, 128)), acc((1, 128)), acc((1, 128)), acc((1, D_SSD))],
        out_shape=[jax.ShapeDtypeStruct((T, D_SSD), BF16), jax.ShapeDtypeStruct((T, D_XBC), BF16),
                   jax.ShapeDtypeStruct((T, 128), F32), jax.ShapeDtypeStruct((8, D_XBC), F32),
                   jax.ShapeDtypeStruct((1, D_XBC), F32), jax.ShapeDtypeStruct((1, 128), F32),
                   jax.ShapeDtypeStruct((1, 128), F32), jax.ShapeDtypeStruct((1, 128), F32),
                   jax.ShapeDtypeStruct((1, D_SSD), F32)],
        scratch_shapes=[pltpu.VMEM((SSD_STATE, D_SSD), F32), pltpu.VMEM((8, D_XBC), F32)],
        compiler_params=_cp("arbitrary"), name=name)(
            proj, proj, proj, dtraw, hs, dout, *params, tril, triu, trils, headsum)


def _s5_blockdiag(v, rows_per_group, cols_per_group):
    eye = jnp.eye(S5_SEG, dtype=v.dtype)
    w = v[:, :, :, None, :] * eye[None, :, None, :, None]
    return w.reshape(S5_CH, 8 * rows_per_group, 8 * cols_per_group)


def _s5_blockdiag_extract(w, rows_per_group, cols_per_group):
    eye = jnp.eye(S5_SEG, dtype=w.dtype)
    w5 = w.reshape(S5_CH, 8, rows_per_group, 8, cols_per_group)
    return jnp.sum(w5 * eye[None, :, None, :, None], axis=3)


TM = 512


def _consts():
    e = np.zeros((S5_STATE, S5_STATE * S5_GROUP), np.float32)
    for p in range(S5_STATE):
        e[p, p * S5_GROUP:(p + 1) * S5_GROUP] = 1.0
    hs = np.zeros((D_SSD, 128), np.float32)
    for h in range(SSD_HEADS):
        hs[h * SSD_HEADDIM:(h + 1) * SSD_HEADDIM, h] = 1.0
    ones = np.ones((SSD_L, SSD_L), np.float32)
    return dict(expand=jnp.asarray(e), expand_t=jnp.asarray(e.T), headsum=jnp.asarray(hs),
                tril=jnp.asarray(np.tril(ones)), triu=jnp.asarray(np.triu(ones)), trils=jnp.asarray(np.tril(ones, -1)))


def _s5_mats(w, l, cst):
    b_re = w["s5_b_re"][l].reshape(S5_GROUPS, S5_STATE * S5_GROUP)
    b_im = w["s5_b_im"][l].reshape(S5_GROUPS, S5_STATE * S5_GROUP)
    abr, abi, apr, api, bbr, bbi = s5_params_fwd(w["s5_a_re"][l], w["s5_a_im"][l], w["s5_log_dt"][l].reshape(S5_GROUPS, 1),
                                                 b_re, b_im, cst["expand"], name="s5_params_fwd")
    t = lambda v: v.reshape(S5_CH, 8, S5_STATE, S5_GROUP).transpose(0, 1, 3, 2)
    c4 = lambda v: v.reshape(S5_CH, 8, S5_GROUP, S5_STATE).transpose(0, 1, 3, 2)
    vec = lambda v: v.reshape(S5_CH, 1, S5_NST)
    return dict(bbr=_s5_blockdiag(t(bbr), S5_GROUP, S5_STATE).astype(BF16),
                bbi=_s5_blockdiag(t(bbi), S5_GROUP, S5_STATE).astype(BF16),
                ccr=_s5_blockdiag(c4(w["s5_c_re"][l]), S5_STATE, S5_GROUP).astype(BF16),
                cci=_s5_blockdiag(c4(w["s5_c_im"][l]), S5_STATE, S5_GROUP).astype(BF16),
                abr=vec(abr), abi=vec(abi), apr=vec(apr), api=vec(api),
                dsk=w["s5_d"][l].reshape(S5_CH, 1, S5_CW), b_re=b_re, b_im=b_im)


def _layer_fwd(x, mem, w, l, cst):
    sv = {}
    g = lambda n: w[n][l].reshape(1, -1)
    proj, h0, r0, dtraw = norm_matmul(x, g("norm_mix"), w["w_in_main"][l], w["w_in_dt"][l], tm=TM, tn=1024,
                                      name="in_proj")
    s5m = _s5_mats(w, l, cst)
    ys, csr, csi = s5_fwd(proj, s5m["bbr"], s5m["bbi"], s5m["ccr"], s5m["cci"], s5m["abr"], s5m["abi"], s5m["apr"],
                          s5m["api"], s5m["dsk"], name="s5_fwd")
    y_s5, tglu = glu_fwd(ys, w["s5_w_glu"][l], tm=TM, name="glu_fwd")
    ssdp = _ssd_params(w["ssd_conv_w"][l], w["ssd_conv_b"][l], w["ssd_dt_bias"][l], w["ssd_a_log"][l], w["ssd_d"][l],
                       w["ssd_norm"][l])
    y_ssd, hs = ssd_fwd(proj, dtraw, ssdp, cst["tril"], name="ssd_fwd")
    ymix = jnp.concatenate([y_s5, y_ssd], axis=1)
    x1 = matmul_res(ymix, w["w_out"][l], x, tm=TM, tn=1024, tk=1024, name="out_proj")
    q, h1, r1 = norm_matmul(x1, g("norm_xattn"), w["xa_wq"][l], tm=TM, tn=1024, name="q_proj")
    wkv = jnp.concatenate([w["xa_wk"][l], w["xa_wv"][l]], axis=1)
    kv, hm, rm = norm_matmul(mem, g("norm_mem"), wkv, tm=mem.shape[0], tn=1024, name="kv_proj")
    o = attn_fwd(q, kv, tm=TM, name="attn_fwd")
    x2 = matmul_res(o, w["xa_wo"][l], x1, tm=TM, tn=1024, tk=1024, name="attn_out")
    f1, h2, r2 = norm_matmul(x2, g("norm_mlp"), w["mlp_w1"][l], tm=TM, tn=1024, name="mlp_up")
    x3 = matmul_res(f1, w["mlp_w2"][l], x2, act="relu2", tm=TM, tn=1024, tk=1024, name="mlp_down")
    sv.update(x=x, proj=proj, h0=h0, r0=r0, dtraw=dtraw, s5m=s5m, ys=ys, csr=csr, csi=csi, tglu=tglu, ssdp=ssdp,
              hs=hs, ymix=ymix, x1=x1, q=q, h1=h1, r1=r1, wkv=wkv, kv=kv, hm=hm, rm=rm, o=o, x2=x2, f1=f1, h2=h2, r2=r2)
    return x3, sv


def _layer_bwd(dx3, mem, w, l, sv, cst):
    gr = {}
    g = lambda n: w[n][l].reshape(1, -1)
    T = dx3.shape[0]
    dh1 = matmul_nt(dx3, w["mlp_w2"][l], epi="relu2bwd", epi_args=(sv["f1"],), tm=TM, tko=1024, tn=1024,
                    name="mlp_down_dx")
    gr["mlp_w2"] = matmul_tn(sv["f1"], dx3, act="relu2", tk=1024, tn=1024, tt=TM, name="mlp_down_dw")
    gr["mlp_w1"] = matmul_tn(sv["h2"], dh1, tk=1024, tn=1024, tt=TM, name="mlp_up_dw")
    dx2, gr["norm_mlp"] = matmul_nt(dh1, w["mlp_w1"][l], epi="rmsbwd", epi_args=(sv["x2"], sv["r2"], g("norm_mlp"), dx3),
                                    tm=TM, tko=1024, tn=1024, name="mlp_up_dx")
    do = matmul_nt(dx2, w["xa_wo"][l], tm=TM, tko=1024, tn=1024, name="attn_out_dx")
    gr["xa_wo"] = matmul_tn(sv["o"], dx2, tk=1024, tn=1024, tt=TM, name="attn_out_dw")
    dq, dkv = attn_bwd(sv["q"], sv["kv"], do, tm=TM, name="attn_bwd")
    gr["xa_wq"] = matmul_tn(sv["h1"], dq, tk=1024, tn=1024, tt=TM, name="q_proj_dw")
    dx1, gr["norm_xattn"] = matmul_nt(dq, w["xa_wq"][l], epi="rmsbwd",
                                      epi_args=(sv["x1"], sv["r1"], g("norm_xattn"), dx2),
                                      tm=TM, tko=1024, tn=1024, name="q_proj_dx")
    M = mem.shape[0]
    dwkv = matmul_tn(sv["hm"], dkv, tk=1024, tn=1024, tt=M, name="kv_proj_dw")
    gr["xa_wk"], gr["xa_wv"] = dwkv[:, :D_MODEL], dwkv[:, D_MODEL:]
    _, gr["norm_mem"] = matmul_nt(dkv, sv["wkv"], epi="rmsbwd",
                                  epi_args=(mem, sv["rm"], g("norm_mem"), jnp.zeros_like(mem)),
                                  tm=M, tko=1024, tn=1024, name="kv_proj_dx")
    dymix = matmul_nt(dx1, w["w_out"][l], tm=TM, tko=1024, tn=1024, name="out_proj_dx")
    gr["w_out"] = matmul_tn(sv["ymix"], dx1, tk=1024, tn=1024, tt=TM, name="out_proj_dw")
    dy_s5, dy_ssd = dymix[:, :D_S5], dymix[:, D_S5:]
    dz, dxbc, ddtraw, dcw, dcb, ddtb, dal, ddsk, dnw = ssd_bwd(
        sv["proj"], sv["dtraw"], sv["hs"], dy_ssd, sv["ssdp"], cst["tril"], cst["triu"], cst["trils"], cst["headsum"],
        name="ssd_bwd")
    gr["ssd_conv_w"], gr["ssd_conv_b"] = dcw[:SSD_CONV], dcb[0]
    gr["ssd_dt_bias"], gr["ssd_a_log"], gr["ssd_d"] = ddtb[0, :SSD_HEADS], dal[0, :SSD_HEADS], ddsk[0, :SSD_HEADS]
    gr["ssd_norm"] = dnw[0]
    dtg, da1 = glu_bwd_pre(dy_s5, sv["ys"], sv["tglu"], tm=TM, name="glu_bwd_pre")
    gr["s5_w_glu"] = matmul_tn(sv["ys"], dtg, act="gelu", tk=1024, tn=1024, tt=TM, name="glu_dw")
    dys = matmul_nt(dtg, w["s5_w_glu"][l], epi="glubwd", epi_args=(da1, sv["ys"]), tm=TM, tko=1024, tn=1024,
                    name="glu_dx")
    s5m = sv["s5m"]
    du, dbbr, dbbi, dccr, dcci, dd, dabr, dabi = s5_bwd(
        sv["proj"], dys, sv["csr"], sv["csi"], s5m["bbr"], s5m["bbi"], s5m["ccr"], s5m["cci"], s5m["abr"], s5m["abi"],
        s5m["apr"], s5m["api"], s5m["dsk"], name="s5_bwd")
    tb = lambda v: _s5_blockdiag_extract(v, S5_GROUP, S5_STATE).transpose(0, 1, 3, 2).reshape(S5_GROUPS, -1)
    tc = lambda v: _s5_blockdiag_extract(v, S5_STATE, S5_GROUP).transpose(0, 1, 3, 2).reshape(S5_GROUPS, S5_GROUP, S5_STATE)
    gr["s5_c_re"], gr["s5_c_im"] = tc(dccr), tc(dcci)
    gr["s5_d"] = dd.reshape(S5_GROUPS, S5_GROUP)
    dar, dai, dld, dbr, dbi = s5_params_bwd(
        w["s5_a_re"][l], w["s5_a_im"][l], w["s5_log_dt"][l].reshape(S5_GROUPS, 1), s5m["b_re"], s5m["b_im"],
        dabr.reshape(S5_GROUPS, S5_STATE), dabi.reshape(S5_GROUPS, S5_STATE), tb(dbbr), tb(dbbi),
        cst["expand"], cst["expand_t"], name="s5_params_bwd")
    gr["s5_a_re"], gr["s5_a_im"], gr["s5_log_dt"] = dar, dai, dld[:, 0]
    gr["s5_b_re"] = dbr.reshape(S5_GROUPS, S5_STATE, S5_GROUP)
    gr["s5_b_im"] = dbi.reshape(S5_GROUPS, S5_STATE, S5_GROUP)
    dproj = jnp.concatenate([du, dz, dxbc], axis=1)
    gr["w_in_main"] = matmul_tn(sv["h0"], dproj, tk=1024, tn=1024, tt=TM, name="in_proj_dw")
    gr["w_in_dt"] = matmul_tn(sv["h0"], ddtraw, tk=1024, tn=D_DT_PAD, tt=TM, name="in_proj_dt_dw")
    dx0, gr["norm_mix"] = matmul_nt(dproj, w["w_in_main"][l], g2=ddtraw, w2=w["w_in_dt"][l], epi="rmsbwd",
                                    epi_args=(sv["x"], sv["r0"], g("norm_mix"), dx1), tm=TM, tko=1024, tn=1024,
                                    name="in_proj_dx")
    gr = {k: (v[0] if k.startswith("norm_") else v) for k, v in gr.items()}
    return dx0, gr


def _local_step(x, mem, target, w):
    cst = _consts()
    saved = []
    for l in range(DEPTH):
        x, sv = _layer_fwd(x, mem, w, l, cst)
        saved.append(sv)
    loss, dx, dgf = loss_head(x, w["norm_final"].reshape(1, -1), target, tm=TM, name="loss_head")
    grads = [None] * DEPTH
    for l in reversed(range(DEPTH)):
        dx, grads[l] = _layer_bwd(dx, mem, w, l, saved[l], cst)
    return loss, dx, grads, dgf[0]


SHARDED = (("w_in", 1028, "col"), ("s5_w_glu", 256, "row"), ("w_out", 512, "row"), ("xa_wq", 256, "row"),
           ("xa_wk", 256, "row"), ("xa_wv", 256, "row"), ("xa_wo", 256, "row"), ("mlp_w1", 1024, "col"),
           ("mlp_w2", 1024, "row"))
BIG_ROWS = DEPTH * sum(r for _, r, _ in SHARDED)
SMALL = ("norm_mix", "s5_a_re", "s5_a_im", "s5_log_dt", "s5_b_re", "s5_b_im", "s5_c_re", "s5_c_im", "s5_d",
         "ssd_conv_w", "ssd_conv_b", "ssd_dt_bias", "ssd_a_log", "ssd_d", "ssd_norm", "norm_xattn", "norm_mem",
         "norm_mlp", "norm_final")
SMALL_Q = 288
TAIL_ROWS = 304
CONV_ROWS = 24
PACK_ROWS = BIG_ROWS + TAIL_ROWS
HALF_ROWS = PACK_ROWS // 2
SUM_ROWS = 824
N_SHARD = 4


def _place():
    x, y, c = lax.axis_index("x"), lax.axis_index("y"), lax.axis_index("c")
    chips = [(1 - x, y), (x, 1 - y), (1 - x, 1 - y)]
    return x, y, c, 2 * x + y, chips, (x, y, 1 - c)


def _remote(src, dst, send_sem, recv_sem, to):
    return pltpu.make_async_remote_copy(src_ref=src, dst_ref=dst, send_sem=send_sem, recv_sem=recv_sem,
                                        device_id=to, device_id_type=MESH_T)


def _half(ref, c):
    return ref.at[pl.ds(pl.multiple_of(c * HALF_ROWS, 16), HALF_ROWS), :]


_ANY = pl.BlockSpec(memory_space=pl.ANY)


def gather_weights(wpack):
    def body(w_ref, out_ref, send_sems, recv_sems, local_sem):
        x, y, c, s, chips, sibling = _place()
        mine = pltpu.make_async_copy(w_ref, out_ref.at[s], local_sem)
        mine.start()
        first = [_remote(_half(w_ref, c), _half(out_ref.at[s], c), send_sems.at[j], recv_sems.at[j], (*chip, c))
                 for j, chip in enumerate(chips)]
        for cp in first:
            cp.start()
        passed = []
        for j, chip in enumerate(chips):
            landed = _half(out_ref.at[2 * chip[0] + chip[1]], c)
            _remote(landed, landed, send_sems.at[j], recv_sems.at[j], sibling).wait_recv()
            cp = _remote(landed, landed, send_sems.at[3 + j], recv_sems.at[3 + j], sibling)
            cp.start()
            passed.append(cp)
        for j, chip in enumerate(chips):
            other = _half(out_ref.at[2 * chip[0] + chip[1]], 1 - c)
            _remote(other, other, send_sems.at[3 + j], recv_sems.at[3 + j], sibling).wait_recv()
        for cp in first + passed:
            cp.wait_send()
        mine.wait()

    return pl.pallas_call(
        body, in_specs=[_ANY], out_specs=_ANY,
        out_shape=jax.ShapeDtypeStruct((N_SHARD,) + wpack.shape, wpack.dtype),
        scratch_shapes=[pltpu.SemaphoreType.DMA((6,)), pltpu.SemaphoreType.DMA((6,)), pltpu.SemaphoreType.DMA],
        name="gather_weights")(wpack)


def exchange_halves(gpack):
    def body(g_ref, out_ref, send_sem, recv_sem):
        x, y, c, s, chips, sibling = _place()
        src = g_ref.at[:, pl.ds(pl.multiple_of((1 - c) * HALF_ROWS, 16), HALF_ROWS), :]
        cp = _remote(src, out_ref, send_sem, recv_sem, sibling)
        cp.start()
        cp.wait()

    return pl.pallas_call(
        body, in_specs=[_ANY], out_specs=_ANY,
        out_shape=jax.ShapeDtypeStruct((N_SHARD, HALF_ROWS, D_MODEL), F32),
        scratch_shapes=[pltpu.SemaphoreType.DMA, pltpu.SemaphoreType.DMA], name="exchange_halves")(gpack)


def sum_halves(gpack, recv, c_idx):
    nb = HALF_ROWS // SUM_ROWS

    def body(c_ref, a_ref, b_ref, o_ref):
        o_ref[...] = a_ref[...] + b_ref[...]

    blk = (1, SUM_ROWS, D_MODEL)
    return pl.pallas_call(
        body,
        grid_spec=pltpu.PrefetchScalarGridSpec(
            num_scalar_prefetch=1, grid=(N_SHARD, nb),
            in_specs=[pl.BlockSpec(blk, lambda t, i, c_ref: (t, c_ref[0] * nb + i, 0)),
                      pl.BlockSpec(blk, lambda t, i, c_ref: (t, i, 0))],
            out_specs=pl.BlockSpec(blk, lambda t, i, c_ref: (t, i, 0))),
        out_shape=jax.ShapeDtypeStruct((N_SHARD, HALF_ROWS, D_MODEL), F32),
        compiler_params=_cp("parallel", "parallel"), name="sum_halves")(c_idx, gpack, recv)


def scatter_chips(csum):
    def body(c_ref, out_ref, send_sems, recv_sems, local_sem):
        x, y, c, s, chips, sibling = _place()
        mine = pltpu.make_async_copy(c_ref.at[s], out_ref.at[s], local_sem)
        mine.start()
        sends = [_remote(c_ref.at[2 * chip[0] + chip[1]], out_ref.at[s], send_sems.at[j], recv_sems.at[j], (*chip, c))
                 for j, chip in enumerate(chips)]
        for cp in sends:
            cp.start()
        for j, chip in enumerate(chips):
            t = 2 * chip[0] + chip[1]
            _remote(c_ref.at[t], out_ref.at[t], send_sems.at[j], recv_sems.at[j], (*chip, c)).wait_recv()
        for cp in sends:
            cp.wait_send()
        mine.wait()

    return pl.pallas_call(
        body, in_specs=[_ANY], out_specs=_ANY, out_shape=jax.ShapeDtypeStruct(csum.shape, csum.dtype),
        scratch_shapes=[pltpu.SemaphoreType.DMA((3,)), pltpu.SemaphoreType.DMA((3,)), pltpu.SemaphoreType.DMA],
        name="scatter_chips")(csum)


def sum_chips(parts):
    nb = HALF_ROWS // SUM_ROWS

    def body(p0, p1, p2, p3, o_ref):
        o_ref[...] = ((p0[0] + p1[0]) + p2[0]) + p3[0]

    specs = [pl.BlockSpec((1, SUM_ROWS, D_MODEL), functools.partial(lambda i, k: (k, i, 0), k=k)) for k in range(N_SHARD)]
    return pl.pallas_call(
        body, grid=(nb,), in_specs=specs, out_specs=pl.BlockSpec((SUM_ROWS, D_MODEL), lambda i: (i, 0)),
        out_shape=jax.ShapeDtypeStruct((HALF_ROWS, D_MODEL), F32),
        compiler_params=_cp("parallel"), name="sum_chips")(parts, parts, parts, parts)


def share_reduced(red):
    def body(r_ref, g_ref, sm_ref, send_sems, recv_sems, local_sems):
        x, y, c, s, chips, sibling = _place()
        my_half = _half(g_ref, c)
        tail = r_ref.at[pl.ds(HALF_ROWS - TAIL_ROWS, TAIL_ROWS), :]
        keep = pltpu.make_async_copy(r_ref, my_half, local_sems.at[0])
        keep.start()
        big = _remote(r_ref, my_half, send_sems.at[0], recv_sems.at[0], sibling)
        big.start()
        keep_tail = pltpu.make_async_copy(tail, sm_ref.at[s], local_sems.at[1])
        tails = [_remote(tail, sm_ref.at[s], send_sems.at[1 + j], recv_sems.at[1 + j], (*chip, 1))
                 for j, chip in enumerate(chips)]
        tails += [_remote(tail, sm_ref.at[s], send_sems.at[4 + j], recv_sems.at[1 + j], (*chip, 0))
                  for j, chip in enumerate(chips)]
        tails.append(_remote(tail, sm_ref.at[s], send_sems.at[7], recv_sems.at[4], sibling))

        @pl.when(c == 1)
        def _():
            keep_tail.start()
            for cp in tails:
                cp.start()

        other = _half(g_ref, 1 - c)
        _remote(other, other, send_sems.at[0], recv_sems.at[0], sibling).wait_recv()
        for j, chip in enumerate(chips):
            slot = sm_ref.at[2 * chip[0] + chip[1]]
            _remote(slot, slot, send_sems.at[1 + j], recv_sems.at[1 + j], sibling).wait_recv()
        big.wait_send()
        keep.wait()

        @pl.when(c == 0)
        def _():
            slot = sm_ref.at[s]
            _remote(slot, slot, send_sems.at[7], recv_sems.at[4], sibling).wait_recv()

        @pl.when(c == 1)
        def _():
            for cp in tails:
                cp.wait_send()
            keep_tail.wait()

    return pl.pallas_call(
        body, in_specs=[_ANY], out_specs=[_ANY, _ANY],
        out_shape=[jax.ShapeDtypeStruct((PACK_ROWS, D_MODEL), F32),
                   jax.ShapeDtypeStruct((N_SHARD, TAIL_ROWS, D_MODEL), F32)],
        scratch_shapes=[pltpu.SemaphoreType.DMA((8,)), pltpu.SemaphoreType.DMA((5,)), pltpu.SemaphoreType.DMA((2,))],
        name="share_reduced")(red)


def adamw(w, g, m, v, *, name):
    shape = w.shape
    cols = shape[-1]
    rows = w.size // cols
    tr = 512 if rows % 512 == 0 else rows
    c1 = 1.0 / (1.0 - ADAM_B1 ** ADAM_STEP)
    c2 = 1.0 / (1.0 - ADAM_B2 ** ADAM_STEP)

    def body(w_ref, g_ref, m_ref, v_ref, d_ref, nm_ref, nv_ref):
        gv = g_ref[...]
        nm = ADAM_B1 * m_ref[...] + (1.0 - ADAM_B1) * gv
        nv = ADAM_B2 * v_ref[...] + (1.0 - ADAM_B2) * (gv * gv)
        d_ref[...] = -ADAM_LR * ((nm * c1) / (jnp.sqrt(nv * c2) + ADAM_EPS) + ADAM_WD * w_ref[...])
        nm_ref[...] = nm
        nv_ref[...] = nv

    spec = pl.BlockSpec((tr, cols), lambda i: (i, 0))
    sds = jax.ShapeDtypeStruct((rows, cols), F32)
    outs = pl.pallas_call(body, grid=(rows // tr,), in_specs=[spec] * 4, out_specs=[spec] * 3, out_shape=[sds] * 3,
                          compiler_params=_cp("parallel"), name=name)(
                              *[a.reshape(rows, cols) for a in (w, g, m, v)])
    return [o.reshape(shape) for o in outs]


def _pack_shards(full, as_rows):
    pieces = []
    for l in range(DEPTH):
        for name, rows, kind in SHARDED:
            a = full[name][l]
            if kind == "col":
                k, n = a.shape
                a = a.reshape(k, N_SHARD, n // N_SHARD).transpose(1, 0, 2)
            pieces.append(a.reshape(N_SHARD, rows, D_MODEL))
    return jnp.concatenate(pieces, axis=1)


def _unpack_shards(packs):
    shapes = dict(w_in=(D_MODEL, 4112), s5_w_glu=(D_S5, D_S5), w_out=(2 * D_MODEL, D_MODEL), xa_wq=(D_MODEL, D_MODEL),
                  xa_wk=(D_MODEL, D_MODEL), xa_wv=(D_MODEL, D_MODEL), xa_wo=(D_MODEL, D_MODEL),
                  mlp_w1=(D_MODEL, D_FF), mlp_w2=(D_FF, D_MODEL))
    out = {name: [] for name, _, _ in SHARDED}
    off = 0
    for l in range(DEPTH):
        for name, rows, kind in SHARDED:
            a = packs[:, off:off + rows, :]
            off += rows
            k, n = shapes[name]
            if kind == "col":
                a = a.reshape(N_SHARD, k, n // N_SHARD).transpose(1, 0, 2)
            out[name].append(a.reshape(k, n))
    return {name: jnp.stack(v) for name, v in out.items()}


def _unpack_own(pack, shard_shapes):
    out = {name: [] for name, _, _ in SHARDED}
    off = 0
    for l in range(DEPTH):
        for name, rows, kind in SHARDED:
            out[name].append(pack[off:off + rows].reshape(shard_shapes[name][1:]))
            off += rows
    return {name: jnp.stack(v) for name, v in out.items()}


def kernel(x, mem, norm_mix, w_in, s5_a_re, s5_a_im, s5_log_dt, s5_b_re, s5_b_im, s5_c_re, s5_c_im, s5_d, s5_w_glu, ssd_conv_w, ssd_conv_b, ssd_dt_bias, ssd_a_log, ssd_d, ssd_norm, w_out, norm_xattn, norm_mem, xa_wq, xa_wk, xa_wv, xa_wo, norm_mlp, mlp_w1, mlp_w2, norm_final, loss_target, m_norm_mix, m_w_in, m_s5_a_re, m_s5_a_im, m_s5_log_dt, m_s5_b_re, m_s5_b_im, m_s5_c_re, m_s5_c_im, m_s5_d, m_s5_w_glu, m_ssd_conv_w, m_ssd_conv_b, m_ssd_dt_bias, m_ssd_a_log, m_ssd_d, m_ssd_norm, m_w_out, m_norm_xattn, m_norm_mem, m_xa_wq, m_xa_wk, m_xa_wv, m_xa_wo, m_norm_mlp, m_mlp_w1, m_mlp_w2, m_norm_final, v_norm_mix, v_w_in, v_s5_a_re, v_s5_a_im, v_s5_log_dt, v_s5_b_re, v_s5_b_im, v_s5_c_re, v_s5_c_im, v_s5_d, v_s5_w_glu, v_ssd_conv_w, v_ssd_conv_b, v_ssd_dt_bias, v_ssd_a_log, v_ssd_d, v_ssd_norm, v_w_out, v_norm_xattn, v_norm_mem, v_xa_wq, v_xa_wk, v_xa_wv, v_xa_wo, v_norm_mlp, v_mlp_w1, v_mlp_w2, v_norm_final):
    names = ("norm_mix", "w_in", "s5_a_re", "s5_a_im", "s5_log_dt", "s5_b_re", "s5_b_im", "s5_c_re", "s5_c_im", "s5_d",
             "s5_w_glu", "ssd_conv_w", "ssd_conv_b", "ssd_dt_bias", "ssd_a_log", "ssd_d", "ssd_norm", "w_out",
             "norm_xattn", "norm_mem", "xa_wq", "xa_wk", "xa_wv", "xa_wo", "norm_mlp", "mlp_w1", "mlp_w2", "norm_final")
    loc = locals()
    wts = {n: loc[n] for n in names}
    mom = {n: loc["m_" + n] for n in names}
    var = {n: loc["v_" + n] for n in names}
    shard = 2 * lax.axis_index("x") + lax.axis_index("y")
    core = lax.axis_index("c")

    own = [wts[name][l].reshape(rows, D_MODEL) for l in range(DEPTH) for name, rows, _ in SHARDED]
    cw_hi = lax.reduce_precision(ssd_conv_w.reshape(-1), 8, 7)
    cw_mid = lax.reduce_precision(ssd_conv_w.reshape(-1) - cw_hi, 8, 7)
    cw_lo = lax.reduce_precision(ssd_conv_w.reshape(-1) - cw_hi - cw_mid, 8, 7)
    conv_rows = jnp.concatenate([cw_hi, cw_mid, cw_lo]).reshape(CONV_ROWS, D_MODEL)
    wpack = jnp.concatenate(own + [conv_rows, jnp.zeros((TAIL_ROWS - CONV_ROWS, D_MODEL), F32)], axis=0).astype(BF16)
    gathered = gather_weights(wpack)
    full = _unpack_shards(gathered)
    w = {n: wts[n] for n in SMALL if n != "ssd_conv_w"}
    for name, _, _ in SHARDED:
        w[name] = full[name]
    w["w_in_main"] = full["w_in"][:, :, :D_MAIN]
    w["w_in_dt"] = jnp.pad(full["w_in"][:, :, D_MAIN:], ((0, 0), (0, 0), (0, D_DT_PAD - SSD_HEADS)))
    del w["w_in"]
    cw = gathered[:, BIG_ROWS:BIG_ROWS + CONV_ROWS].astype(F32).reshape(N_SHARD, 3, DEPTH, SSD_CONV, D_XBC // N_SHARD)
    cw = (cw[:, 0] + cw[:, 1]) + cw[:, 2]
    w["ssd_conv_w"] = cw.transpose(1, 2, 0, 3).reshape(DEPTH, SSD_CONV, D_XBC)

    loss, dx, grads, dgf = _local_step(x[0], mem[0], loss_target[0], w)

    gfull = {}
    for name, _, _ in SHARDED:
        if name == "w_in":
            gfull[name] = [jnp.concatenate([grads[l]["w_in_main"], grads[l]["w_in_dt"][:, :SSD_HEADS]], axis=1)
                           for l in range(DEPTH)]
        else:
            gfull[name] = [grads[l][name] for l in range(DEPTH)]
    small_flat = jnp.concatenate(
        [(dgf if n == "norm_final" else jnp.stack([grads[l][n] for l in range(DEPTH)])).reshape(-1) for n in SMALL])
    small_rows = jnp.pad(small_flat, (0, N_SHARD * SMALL_Q * D_MODEL - small_flat.size)).reshape(N_SHARD, SMALL_Q, D_MODEL)
    gpack = jnp.concatenate([_pack_shards(gfull, None), small_rows,
                             jnp.zeros((N_SHARD, TAIL_ROWS - SMALL_Q, D_MODEL), F32)], axis=1)
    recv = exchange_halves(gpack)
    csum = sum_halves(gpack, recv, core.astype(jnp.int32).reshape(1))
    red = sum_chips(scatter_chips(csum))
    gshard, small_all = share_reduced(red)

    g = _unpack_own(gshard, {name: wts[name].shape for name, _, _ in SHARDED})
    small_red = small_all[:, :SMALL_Q].reshape(-1)
    off = 0
    for n in SMALL:
        shape = (DEPTH, SSD_CONV, D_XBC) if n == "ssd_conv_w" else wts[n].shape
        size = math.prod(shape)
        g[n] = small_red[off:off + size].reshape(shape)
        off += size
    g["ssd_conv_w"] = lax.dynamic_slice_in_dim(g["ssd_conv_w"], shard * (D_XBC // N_SHARD), D_XBC // N_SHARD, axis=2)

    deltas, new_m, new_v = {}, {}, {}
    for n in names:
        deltas[n], new_m[n], new_v[n] = adamw(wts[n], g[n], mom[n], var[n], name="adamw_" + n)
    loss_all = lax.psum(loss[0, 0], ("x", "y", "c"))
    return (loss_all, dx[None], *[g[n] for n in names], *[deltas[n] for n in names], *[new_m[n] for n in names],
            *[new_v[n] for n in names])
```

```python
import functools
import math

import jax
import jax.numpy as jnp
import numpy as np
from jax import lax
from jax.experimental import pallas as pl
from jax.experimental.pallas import tpu as pltpu

F32 = jnp.float32
BF16 = jnp.bfloat16
HIGHEST = lax.Precision.HIGHEST

D_MODEL = 1024
DEPTH = 4
D_S5 = 1024
D_SSD = 1024
S5_GROUP = 16
S5_GROUPS = 64
S5_STATE = 64
SSD_HEADS = 16
SSD_HEADDIM = 64
SSD_NGROUPS = 4
SSD_STATE = 128
SSD_CONV = 4
SSD_CHUNK = 128
D_XBC = 2048
D_MAIN = 4096
D_DT_PAD = 128
XA_HEADS = 4
XA_HEAD_DIM = 256
D_FF = 4096
EPS = 1e-5
ADAM_LR, ADAM_B1, ADAM_B2, ADAM_EPS, ADAM_WD, ADAM_STEP = 0.001, 0.9, 0.999, 1e-08, 0.01, 10

VMEM_LIMIT = 56 * 1024 * 1024
MESH_T = pl.DeviceIdType.MESH


def _cp(*sem):
    return pltpu.CompilerParams(dimension_semantics=tuple(sem) if sem else None, vmem_limit_bytes=VMEM_LIMIT)


_ANY = pl.BlockSpec(memory_space=pl.ANY)


class Comm:
    def __init__(self, name, inputs, out_shapes, sems, start, wait, aliases=None):
        self.name, self.inputs, self.out_shapes, self.sems = name, list(inputs), list(out_shapes), list(sems)
        self.start, self.wait, self.aliases = start, wait, dict(aliases or {})


def _call(body, *, grid, in_specs, out_specs, out_shape, args, scratch_shapes=(), sem, name, comm=None, aliases=None):
    in_specs, out_specs, out_shape = list(in_specs), list(out_specs), list(out_shape)
    scratch_shapes = list(scratch_shapes)
    aliases = dict(aliases or {})
    if comm is None:
        res = pl.pallas_call(body, grid=grid, in_specs=in_specs, out_specs=out_specs, out_shape=out_shape,
                             scratch_shapes=scratch_shapes, compiler_params=_cp(*sem), name=name,
                             input_output_aliases=aliases)(*args)
        return list(res), []
    n_in, n_out, n_scr = len(in_specs), len(out_specs), len(scratch_shapes)
    c_in, c_out = len(comm.inputs), len(comm.out_shapes)

    def wrapped(*refs):
        a, refs = refs[:n_in], refs[n_in:]
        ci, refs = refs[:c_in], refs[c_in:]
        o, refs = refs[:n_out], refs[n_out:]
        co, refs = refs[:c_out], refs[c_out:]
        s, cs = refs[:n_scr], refs[n_scr:]
        first = functools.reduce(jnp.logical_and, [pl.program_id(d) == 0 for d in range(len(grid))])
        last = functools.reduce(jnp.logical_and, [pl.program_id(d) == grid[d] - 1 for d in range(len(grid))])

        @pl.when(first)
        def _():
            comm.start(ci, co, cs)

        body(*a, *o, *s)

        @pl.when(last)
        def _():
            comm.wait(ci, co, cs)

    for i, j in comm.aliases.items():
        aliases[n_in + i] = n_out + j
    res = pl.pallas_call(wrapped, grid=grid, in_specs=in_specs + [_ANY] * c_in, out_specs=out_specs + [_ANY] * c_out,
                         out_shape=out_shape + comm.out_shapes, scratch_shapes=scratch_shapes + comm.sems,
                         compiler_params=_cp(*(("arbitrary",) * len(grid))), name=name + "_" + comm.name,
                         input_output_aliases=aliases)(*args, *comm.inputs)
    return list(res[:n_out]), list(res[n_out:])


def _comm_only(comm):
    def body(*refs):
        ci, refs = refs[:len(comm.inputs)], refs[len(comm.inputs):]
        co, cs = refs[:len(comm.out_shapes)], refs[len(comm.out_shapes):]
        comm.start(ci, co, cs)
        comm.wait(ci, co, cs)

    res = pl.pallas_call(body, in_specs=[_ANY] * len(comm.inputs), out_specs=[_ANY] * len(comm.out_shapes),
                         out_shape=comm.out_shapes, scratch_shapes=comm.sems, name=comm.name,
                         input_output_aliases=comm.aliases)(*comm.inputs)
    return list(res)


def _dot(a, b):
    return jnp.dot(a, b, preferred_element_type=F32)


def _dot_nt(a, b):
    return lax.dot_general(a, b, (((1,), (1,)), ((), ())), preferred_element_type=F32)


def _dot_tn(a, b):
    return lax.dot_general(a, b, (((0,), (0,)), ((), ())), preferred_element_type=F32)


def _dot_hi(a, b):
    return jnp.dot(a, b, precision=HIGHEST, preferred_element_type=F32)


def _gelu(x):
    c = math.sqrt(2.0 / math.pi)
    return 0.5 * x * (1.0 + jnp.tanh(c * (x + 0.044715 * x * x * x)))


def _gelu_grad(x):
    c = math.sqrt(2.0 / math.pi)
    t = jnp.tanh(c * (x + 0.044715 * x * x * x))
    return 0.5 * (1.0 + t) + 0.5 * x * (1.0 - t * t) * c * (1.0 + 3 * 0.044715 * x * x)


def _sigmoid(x):
    return 1.0 / (1.0 + jnp.exp(-x))


def _act(a, act):
    if act is None:
        return a.astype(BF16)
    a = a.astype(F32)
    if act == "relu2":
        a = jnp.maximum(a, 0.0)
        return (a * a).astype(BF16)
    if act == "gelu":
        return _gelu(a).astype(BF16)
    raise ValueError(act)


def norm_matmul(x, g, w, w2=None, *, tm, tn, name, wspec=None, n_out=None):
    T, D = x.shape
    N = w.shape[1] if wspec is None else n_out
    wget = (lambda r: r[...]) if wspec is None else (lambda r: r[0])
    has2 = w2 is not None

    def body(x_ref, g_ref, w_ref, *rest):
        if has2:
            w2_ref, o_ref, h_ref, r_ref, o2_ref = rest
        else:
            o_ref, h_ref, r_ref = rest
        j = pl.program_id(1)

        @pl.when(j == 0)
        def _():
            xv = x_ref[...]
            r = lax.rsqrt(jnp.mean(xv * xv, axis=-1, keepdims=True) + EPS)
            h = (xv * r * g_ref[...]).astype(BF16)
            h_ref[...] = h
            r_ref[...] = r
            if has2:
                o2_ref[...] = _dot(h, w2_ref[...])

        o_ref[...] = _dot(h_ref[...], wget(w_ref)).astype(o_ref.dtype)

    in_specs = [pl.BlockSpec((tm, D), lambda i, j: (i, 0)), pl.BlockSpec((1, D), lambda i, j: (0, 0)),
                pl.BlockSpec((D, tn), lambda i, j: (0, j)) if wspec is None else wspec]
    out_shape = [jax.ShapeDtypeStruct((T, N), BF16), jax.ShapeDtypeStruct((T, D), BF16),
                 jax.ShapeDtypeStruct((T, 1), F32)]
    out_specs = [pl.BlockSpec((tm, tn), lambda i, j: (i, j)), pl.BlockSpec((tm, D), lambda i, j: (i, 0)),
                 pl.BlockSpec((tm, 1), lambda i, j: (i, 0))]
    args = [x, g, w]
    if has2:
        in_specs.append(pl.BlockSpec((D, D_DT_PAD), lambda i, j: (0, 0)))
        out_shape.append(jax.ShapeDtypeStruct((T, D_DT_PAD), F32))
        out_specs.append(pl.BlockSpec((tm, D_DT_PAD), lambda i, j: (i, 0)))
        args.append(w2)
    return pl.pallas_call(body, grid=(T // tm, N // tn), in_specs=in_specs, out_specs=out_specs,
                          out_shape=out_shape, compiler_params=_cp("parallel", "arbitrary"), name=name)(*args)


def matmul_res(a, w, r, *, act=None, tm, tn, tk, name, wspec=None):
    T, K = a.shape
    N = r.shape[1]
    wget = (lambda r_: r_[...]) if wspec is None else (lambda r_: r_[0])
    nk = K // tk

    def body(a_ref, w_ref, r_ref, o_ref):
        k = pl.program_id(2)

        @pl.when(k == 0)
        def _():
            o_ref[...] = r_ref[...]

        o_ref[...] += _dot(_act(a_ref[...], act), wget(w_ref))

    return pl.pallas_call(
        body, grid=(T // tm, N // tn, nk),
        in_specs=[pl.BlockSpec((tm, tk), lambda i, j, k: (i, k)),
                  pl.BlockSpec((tk, tn), lambda i, j, k: (k, j)) if wspec is None else wspec,
                  pl.BlockSpec((tm, tn), lambda i, j, k: (i, j))],
        out_specs=pl.BlockSpec((tm, tn), lambda i, j, k: (i, j)),
        out_shape=jax.ShapeDtypeStruct((T, N), F32),
        compiler_params=_cp("parallel", "parallel", "arbitrary"), name=name)(a, w, r)


def glu_fwd(ys, w, *, tm, name, comm=None):
    T, N = ys.shape

    def body(y_ref, w_ref, o_ref, t_ref):
        a = _gelu(y_ref[...].astype(F32))
        t = _dot(a.astype(BF16), w_ref[...])
        o_ref[...] = (a * _sigmoid(t)).astype(BF16)
        t_ref[...] = t.astype(BF16)

    return _call(
        body, grid=(T // tm,),
        in_specs=[pl.BlockSpec((tm, N), lambda i: (i, 0)), pl.BlockSpec((N, N), lambda i: (0, 0))],
        out_specs=[pl.BlockSpec((tm, N), lambda i: (i, 0)), pl.BlockSpec((tm, N), lambda i: (i, 0))],
        out_shape=[jax.ShapeDtypeStruct((T, 2 * N), BF16), jax.ShapeDtypeStruct((T, N), BF16)],
        sem=("parallel",), name=name, args=(ys, w), comm=comm)


def glu_bwd_pre(dout, ys, t, *, tm, name):
    T, N = ys.shape

    def body(d_ref, y_ref, t_ref, dt_ref, da_ref):
        d = d_ref[...].astype(F32)
        a = _gelu(y_ref[...].astype(F32))
        s = _sigmoid(t_ref[...].astype(F32))
        dt_ref[...] = (d * a * s * (1.0 - s)).astype(BF16)
        da_ref[...] = (d * s).astype(BF16)

    spec = pl.BlockSpec((tm, N), lambda i: (i, 0))
    return pl.pallas_call(
        body, grid=(T // tm,), in_specs=[spec, spec, spec], out_specs=[spec, spec],
        out_shape=[jax.ShapeDtypeStruct((T, N), BF16), jax.ShapeDtypeStruct((T, N), BF16)],
        compiler_params=_cp("parallel"), name=name)(dout, ys, t)


def matmul_nt(g, w, *, epi=None, epi_args=(), g2=None, w2=None, tm, tko, tn, out_dtype=BF16, name, wspec=None,
              k_out=None, comm=None):
    T, N = g.shape
    K = w.shape[0] if wspec is None else k_out
    wget = (lambda r_: r_[...]) if wspec is None else (lambda r_: r_[0])
    nn = N // tn
    has2 = g2 is not None
    rms = epi == "rmsbwd"
    if rms:
        assert tko == K
    n_epi = len(epi_args)

    def body(*refs):
        g_ref, w_ref = refs[0], refs[1]
        pos = 2
        if has2:
            g2_ref, w2_ref = refs[2], refs[3]
            pos = 4
        e_refs = refs[pos:pos + n_epi]
        pos += n_epi
        o_ref = refs[pos]
        pos += 1
        if rms:
            dg_ref = refs[pos]
            pos += 1
        acc_ref = refs[pos]
        i = pl.program_id(0)
        n = pl.program_id(2)
        part = _dot_nt(g_ref[...].astype(BF16), wget(w_ref))

        @pl.when(n == 0)
        def _():
            acc_ref[...] = part

        @pl.when(n > 0)
        def _():
            acc_ref[...] += part

        @pl.when(n == nn - 1)
        def _():
            acc = acc_ref[...]
            if has2:
                acc = acc + _dot_nt(g2_ref[...].astype(BF16), w2_ref[...])
            if epi is None:
                o_ref[...] = acc.astype(o_ref.dtype)
            elif epi == "relu2bwd":
                h1 = e_refs[0][...].astype(F32)
                o_ref[...] = (acc * 2.0 * jnp.maximum(h1, 0.0)).astype(o_ref.dtype)
            elif epi == "glubwd":
                da1 = e_refs[0][...].astype(F32)
                ys = e_refs[1][...].astype(F32)
                o_ref[...] = ((da1 + acc) * _gelu_grad(ys)).astype(o_ref.dtype)
            elif epi == "rmsbwd":
                xv, rs, gain, rv = e_refs[0][...], e_refs[1][...], e_refs[2][...], e_refs[3][...]
                xhat = xv * rs
                gd = acc * gain
                o_ref[...] = rv + rs * (gd - xhat * jnp.mean(gd * xhat, axis=-1, keepdims=True))
                part_g = jnp.sum(acc * xhat, axis=0, keepdims=True)

                @pl.when(i == 0)
                def _():
                    dg_ref[...] = part_g

                @pl.when(i > 0)
                def _():
                    dg_ref[...] += part_g

    in_specs = [pl.BlockSpec((tm, tn), lambda i, k, n: (i, n)),
                pl.BlockSpec((tko, tn), lambda i, k, n: (k, n)) if wspec is None else wspec]
    args = [g, w]
    if has2:
        n2 = g2.shape[1]
        in_specs += [pl.BlockSpec((tm, n2), lambda i, k, n: (i, 0)), pl.BlockSpec((tko, n2), lambda i, k, n: (k, 0))]
        args += [g2, w2]
    if epi == "relu2bwd" or epi == "glubwd":
        in_specs += [pl.BlockSpec((tm, tko), lambda i, k, n: (i, k))] * n_epi
    elif rms:
        in_specs += [pl.BlockSpec((tm, K), lambda i, k, n: (i, 0)), pl.BlockSpec((tm, 1), lambda i, k, n: (i, 0)),
                     pl.BlockSpec((1, K), lambda i, k, n: (0, 0)), pl.BlockSpec((tm, K), lambda i, k, n: (i, 0))]
    args += list(epi_args)
    out_shape = [jax.ShapeDtypeStruct((T, K), F32 if rms else out_dtype)]
    out_specs = [pl.BlockSpec((tm, tko), lambda i, k, n: (i, k))]
    if rms:
        out_shape.append(jax.ShapeDtypeStruct((1, K), F32))
        out_specs.append(pl.BlockSpec((1, K), lambda i, k, n: (0, 0)))
    sem = ("arbitrary",) * 3 if rms else ("parallel", "parallel", "arbitrary")
    res, cres = _call(body, grid=(T // tm, K // tko, nn), in_specs=in_specs, out_specs=out_specs, out_shape=out_shape,
                      scratch_shapes=[pltpu.VMEM((tm, tko), F32)], sem=sem, name=name, args=args, comm=comm)
    res = res if rms else res[0]
    return res if comm is None else (res, cres)


def matmul_tn(a, g, *, act=None, tk, tn, tt, name, pack=None, pack_spec=None, pack_shape=None):
    T, K = a.shape
    N = g.shape[1]
    to_pack = pack_spec is not None

    def body(a_ref, g_ref, *rest):
        o_ref = rest[-1]
        t = pl.program_id(2)
        part = _dot_tn(_act(a_ref[...], act), g_ref[...].astype(BF16))
        if to_pack:
            part = part[None]

        @pl.when(t == 0)
        def _():
            o_ref[...] = part

        @pl.when(t > 0)
        def _():
            o_ref[...] += part

    in_specs = [pl.BlockSpec((tt, tk), lambda k, n, t: (t, k)), pl.BlockSpec((tt, tn), lambda k, n, t: (t, n))]
    args = [a, g]
    aliases = {}
    if pack is not None:
        in_specs.append(_ANY)
        args.append(pack)
        aliases = {2: 0}
    return pl.pallas_call(
        body, grid=(K // tk, N // tn, T // tt), in_specs=in_specs,
        out_specs=pack_spec if to_pack else pl.BlockSpec((tk, tn), lambda k, n, t: (k, n)),
        out_shape=jax.ShapeDtypeStruct(pack_shape if to_pack else (K, N), F32), input_output_aliases=aliases,
        compiler_params=_cp("parallel", "parallel", "arbitrary"), name=name)(*args)


def attn_fwd(q, kv, *, tm, name):
    T = q.shape[0]
    M = kv.shape[0]
    scale = XA_HEAD_DIM ** -0.5

    def body(q_ref, kv_ref, o_ref):
        for h in range(XA_HEADS):
            sl = slice(h * XA_HEAD_DIM, (h + 1) * XA_HEAD_DIM)
            kh = kv_ref[:, h * XA_HEAD_DIM:(h + 1) * XA_HEAD_DIM]
            vh = kv_ref[:, D_MODEL + h * XA_HEAD_DIM:D_MODEL + (h + 1) * XA_HEAD_DIM]
            s = _dot_nt(q_ref[:, sl], kh) * scale
            s = s - jnp.max(s, axis=-1, keepdims=True)
            p = jnp.exp(s)
            p = p / jnp.sum(p, axis=-1, keepdims=True)
            o_ref[:, sl] = _dot(p.astype(BF16), vh).astype(BF16)

    return pl.pallas_call(
        body, grid=(T // tm,),
        in_specs=[pl.BlockSpec((tm, D_MODEL), lambda i: (i, 0)), pl.BlockSpec((M, 2 * D_MODEL), lambda i: (0, 0))],
        out_specs=pl.BlockSpec((tm, D_MODEL), lambda i: (i, 0)),
        out_shape=jax.ShapeDtypeStruct((T, D_MODEL), BF16),
        compiler_params=_cp("parallel"), name=name)(q, kv)


def attn_bwd(q, kv, do, *, tm, name):
    T = q.shape[0]
    M = kv.shape[0]
    scale = XA_HEAD_DIM ** -0.5

    def body(q_ref, kv_ref, do_ref, dq_ref, dkv_ref):
        i = pl.program_id(0)

        @pl.when(i == 0)
        def _():
            dkv_ref[...] = jnp.zeros_like(dkv_ref)

        for h in range(XA_HEADS):
            sl = slice(h * XA_HEAD_DIM, (h + 1) * XA_HEAD_DIM)
            slv = slice(D_MODEL + h * XA_HEAD_DIM, D_MODEL + (h + 1) * XA_HEAD_DIM)
            qh = q_ref[:, sl]
            kh = kv_ref[:, sl]
            vh = kv_ref[:, slv]
            doh = do_ref[:, sl]
            s = _dot_nt(qh, kh) * scale
            s = s - jnp.max(s, axis=-1, keepdims=True)
            p = jnp.exp(s)
            p = p / jnp.sum(p, axis=-1, keepdims=True)
            pb = p.astype(BF16)
            dkv_ref[:, slv] += _dot_tn(pb, doh)
            dp = _dot_nt(doh, vh)
            ds = (p * (dp - jnp.sum(dp * p, axis=-1, keepdims=True)) * scale).astype(BF16)
            dq_ref[:, sl] = _dot(ds, kh).astype(BF16)
            dkv_ref[:, sl] += _dot_tn(ds, qh)

    spec = pl.BlockSpec((tm, D_MODEL), lambda i: (i, 0))
    kvspec = pl.BlockSpec((M, 2 * D_MODEL), lambda i: (0, 0))
    return pl.pallas_call(
        body, grid=(T // tm,), in_specs=[spec, kvspec, spec], out_specs=[spec, kvspec],
        out_shape=[jax.ShapeDtypeStruct((T, D_MODEL), BF16), jax.ShapeDtypeStruct((M, 2 * D_MODEL), F32)],
        compiler_params=_cp("arbitrary"), name=name)(q, kv, do)


def loss_head(x, g, target, *, tm, name):
    T, D = x.shape

    def body(x_ref, g_ref, t_ref, l_ref, dx_ref, dg_ref):
        i = pl.program_id(0)
        xv = x_ref[...]
        gain = g_ref[...]
        r = lax.rsqrt(jnp.mean(xv * xv, axis=-1, keepdims=True) + EPS)
        xhat = xv * r
        err = xhat * gain - t_ref[...]
        part_l = jnp.full((1, 128), 0.5 / D, F32) * jnp.sum(err * err)
        dy = err * (1.0 / D)
        gd = dy * gain
        dx_ref[...] = r * (gd - xhat * jnp.mean(gd * xhat, axis=-1, keepdims=True))
        part_g = jnp.sum(dy * xhat, axis=0, keepdims=True)

        @pl.when(i == 0)
        def _():
            l_ref[...] = part_l
            dg_ref[...] = part_g

        @pl.when(i > 0)
        def _():
            l_ref[...] += part_l
            dg_ref[...] += part_g

    spec = pl.BlockSpec((tm, D), lambda i: (i, 0))
    return pl.pallas_call(
        body, grid=(T // tm,), in_specs=[spec, pl.BlockSpec((1, D), lambda i: (0, 0)), spec],
        out_specs=[pl.BlockSpec((1, 128), lambda i: (0, 0)), spec, pl.BlockSpec((1, D), lambda i: (0, 0))],
        out_shape=[jax.ShapeDtypeStruct((1, 128), F32), jax.ShapeDtypeStruct((T, D), F32),
                   jax.ShapeDtypeStruct((1, D), F32)],
        compiler_params=_cp("arbitrary"), name=name)(x, g, target)


S5_LS = 128
S5_SEG = 8
S5_TB = S5_LS * S5_SEG
S5_CH = 8
S5_CW = 128
S5_NST = 512


def _cmul(ar, ai, br, bi):
    return ar * br - ai * bi, ar * bi + ai * br


def s5_params_fwd(a_re, a_im, log_dt, b_re, b_im, expand, *, name):
    G, P = a_re.shape

    def body(ar_ref, ai_ref, ld_ref, br_ref, bi_ref, e_ref, abr_ref, abi_ref, apr_ref, api_ref, bbr_ref, bbi_ref):
        ar, ai = ar_ref[...], ai_ref[...]
        dt = jnp.exp(ld_ref[...])
        mag = jnp.exp(dt * ar)
        abr, abi = mag * jnp.cos(dt * ai), mag * jnp.sin(dt * ai)
        den = ar * ar + ai * ai
        zr, zi = abr - 1.0, abi
        fr = (zr * ar + zi * ai) / den
        fi = (zi * ar - zr * ai) / den
        frx, fix = _dot_hi(fr, e_ref[...]), _dot_hi(fi, e_ref[...])
        br, bi = br_ref[...], bi_ref[...]
        bbr_ref[...] = frx * br - fix * bi
        bbi_ref[...] = frx * bi + fix * br
        abr_ref[...] = abr
        abi_ref[...] = abi
        pr, pi = abr, abi
        for _ in range(int(math.log2(S5_LS))):
            pr, pi = _cmul(pr, pi, pr, pi)
        apr_ref[...] = pr
        api_ref[...] = pi

    small = jax.ShapeDtypeStruct((G, P), F32)
    big = jax.ShapeDtypeStruct(b_re.shape, F32)
    return pl.pallas_call(body, out_shape=[small, small, small, small, big, big], name=name)(
        a_re, a_im, log_dt, b_re, b_im, expand)


def s5_params_bwd(a_re, a_im, log_dt, b_re, b_im, g_abr, g_abi, g_bbr, g_bbi, expand, expand_t, *, name):
    G, P = a_re.shape

    def body(ar_ref, ai_ref, ld_ref, br_ref, bi_ref, gar_ref, gai_ref, gbr_ref, gbi_ref, e_ref, et_ref,
             dar_ref, dai_ref, dld_ref, dbr_ref, dbi_ref):
        ar, ai = ar_ref[...], ai_ref[...]
        dt = jnp.exp(ld_ref[...])
        mag = jnp.exp(dt * ar)
        cs, sn = jnp.cos(dt * ai), jnp.sin(dt * ai)
        abr, abi = mag * cs, mag * sn
        den = ar * ar + ai * ai
        zr, zi = abr - 1.0, abi
        fr = (zr * ar + zi * ai) / den
        fi = (zi * ar - zr * ai) / den
        frx, fix = _dot_hi(fr, e_ref[...]), _dot_hi(fi, e_ref[...])
        br, bi = br_ref[...], bi_ref[...]
        gbr, gbi = gbr_ref[...], gbi_ref[...]
        dbr_ref[...] = frx * gbr + fix * gbi
        dbi_ref[...] = -fix * gbr + frx * gbi
        gfr = _dot_hi(br * gbr + bi * gbi, et_ref[...])
        gfi = _dot_hi(-bi * gbr + br * gbi, et_ref[...])
        g_zr = (gfr * ar - gfi * ai) / den
        g_zi = (gfr * ai + gfi * ar) / den
        g_ar = gfr * (zr - fr * 2.0 * ar) / den + gfi * (zi - fi * 2.0 * ar) / den
        g_ai = gfr * (zi - fr * 2.0 * ai) / den + gfi * (-zr - fi * 2.0 * ai) / den
        t_abr = gar_ref[...] + g_zr
        t_abi = gai_ref[...] + g_zi
        g_mag = t_abr * cs + t_abi * sn
        g_th = mag * (-t_abr * sn + t_abi * cs)
        dar_ref[...] = g_ar + g_mag * mag * dt
        dai_ref[...] = g_ai + g_th * dt
        g_dt = jnp.sum(g_mag * mag * ar + g_th * ai, axis=-1, keepdims=True)
        dld_ref[...] = g_dt * dt

    small = jax.ShapeDtypeStruct((G, P), F32)
    big = jax.ShapeDtypeStruct(b_re.shape, F32)
    return pl.pallas_call(body, out_shape=[small, small, jax.ShapeDtypeStruct((G, 1), F32), big, big], name=name)(
        a_re, a_im, log_dt, b_re, b_im, g_abr, g_abi, g_bbr, g_bbi, expand, expand_t)


def _s5_permute_in(src_ref, dst_ref):
    for i in range(S5_LS):
        dst_ref[pl.ds(8 * i, 8), :] = src_ref[pl.ds(i, 8, stride=S5_LS), :]


def _s5_permute_out(src_ref, dst_ref):
    for r in range(S5_SEG):
        for k in range(S5_LS // 8):
            dst_ref[pl.ds(r * S5_LS + 8 * k, 8), :] = src_ref[pl.ds(64 * k + r, 8, stride=8), :]


def _s5_scan(a_r, a_i, dr_ref, di_ref, init_r, init_i, store=None, reverse=False, conj=False):
    sgn = -1.0 if conj else 1.0

    def step(n, c):
        sr, si = c
        i = S5_LS - 1 - n if reverse else n
        nr = a_r * sr - sgn * a_i * si + dr_ref[i]
        ni = a_r * si + sgn * a_i * sr + di_ref[i]
        if store is not None:
            store(i, nr, ni, sr, si)
        return nr, ni

    return lax.fori_loop(0, S5_LS, step, (init_r, init_i), unroll=2)


def _s5_stitch(apr, api, fin_r, fin_i, car_r, car_i, reverse=False, conj=False):
    sgn = -1.0 if conj else 1.0
    rows_r, rows_i = [None] * S5_SEG, [None] * S5_SEG
    order = range(S5_SEG - 1, -1, -1) if reverse else range(S5_SEG)
    for r in order:
        rows_r[r], rows_i[r] = car_r, car_i
        fr, fi = fin_r[r:r + 1], fin_i[r:r + 1]
        car_r, car_i = (apr * car_r - sgn * api * car_i + fr, apr * car_i + sgn * api * car_r + fi)
    return jnp.concatenate(rows_r, 0), jnp.concatenate(rows_i, 0), car_r, car_i


def _s5_specs(nb, rev):
    blk = (lambda c, b: (nb - 1 - b, c)) if rev else (lambda c, b: (b, c))
    tok = pl.BlockSpec((S5_TB, S5_CW), blk)
    par_b = pl.BlockSpec((1, S5_CW, S5_NST), lambda c, b: (c, 0, 0))
    par_c = pl.BlockSpec((1, S5_NST, S5_CW), lambda c, b: (c, 0, 0))
    vec_s = pl.BlockSpec((1, 1, S5_NST), lambda c, b: (c, 0, 0))
    vec_c = pl.BlockSpec((1, 1, S5_CW), lambda c, b: (c, 0, 0))
    return tok, par_b, par_c, vec_s, vec_c


def s5_fwd(proj, bbr, bbi, ccr, cci, abr, abi, apr, api, dskip, *, name, comm=None):
    T = proj.shape[0]
    nb = T // S5_TB
    zeros8 = functools.partial(jnp.zeros, (S5_SEG, S5_NST), F32)

    def body(u_ref, bbr_ref, bbi_ref, ccr_ref, cci_ref, ar_ref, ai_ref, apr_ref, api_ref, d_ref,
             y_ref, csr_ref, csi_ref, uf_ref, up_ref, dr_ref, di_ref, sr_ref, si_ref, yp_ref, car_ref, cai_ref):
        b = pl.program_id(1)

        @pl.when(b == 0)
        def _():
            car_ref[...] = jnp.zeros_like(car_ref)
            cai_ref[...] = jnp.zeros_like(cai_ref)

        csr_ref[0, 0] = car_ref[...]
        csi_ref[0, 0] = cai_ref[...]
        uf_ref[...] = u_ref[...].astype(F32)
        _s5_permute_in(uf_ref, up_ref)
        upb = up_ref[...].astype(BF16)
        dr_ref[...] = _dot(upb, bbr_ref[0]).reshape(S5_LS, S5_SEG, S5_NST)
        di_ref[...] = _dot(upb, bbi_ref[0]).reshape(S5_LS, S5_SEG, S5_NST)
        a_r = jnp.broadcast_to(ar_ref[0], (S5_SEG, S5_NST))
        a_i = jnp.broadcast_to(ai_ref[0], (S5_SEG, S5_NST))
        fin_r, fin_i = _s5_scan(a_r, a_i, dr_ref, di_ref, zeros8(), zeros8())
        cin_r, cin_i, ncr, nci = _s5_stitch(apr_ref[0], api_ref[0], fin_r, fin_i, car_ref[...], cai_ref[...])
        car_ref[...] = ncr
        cai_ref[...] = nci

        def store(i, nr, ni, sr, si):
            sr_ref[i] = nr
            si_ref[i] = ni

        _s5_scan(a_r, a_i, dr_ref, di_ref, cin_r, cin_i, store=store)
        s_r = sr_ref[...].reshape(S5_TB, S5_NST).astype(BF16)
        s_i = si_ref[...].reshape(S5_TB, S5_NST).astype(BF16)
        yp_ref[...] = _dot(s_r, ccr_ref[0]) - _dot(s_i, cci_ref[0]) + d_ref[0] * up_ref[...]
        _s5_permute_out(yp_ref, uf_ref)
        y_ref[...] = uf_ref[...].astype(BF16)

    tok, par_b, par_c, vec_s, vec_c = _s5_specs(nb, False)
    cs_spec = pl.BlockSpec((1, 1, 1, S5_NST), lambda c, b: (b, c, 0, 0))
    cs_shape = jax.ShapeDtypeStruct((nb, S5_CH, 1, S5_NST), F32)
    tokbuf = pltpu.VMEM((S5_TB, S5_CW), F32)
    stbuf = pltpu.VMEM((S5_LS, S5_SEG, S5_NST), F32)
    return _call(
        body, grid=(S5_CH, nb),
        in_specs=[tok, par_b, par_b, par_c, par_c, vec_s, vec_s, vec_s, vec_s, vec_c],
        out_specs=[tok, cs_spec, cs_spec],
        out_shape=[jax.ShapeDtypeStruct((T, D_S5), BF16), cs_shape, cs_shape],
        scratch_shapes=[tokbuf, tokbuf, stbuf, stbuf, stbuf, stbuf, tokbuf,
                        pltpu.VMEM((1, S5_NST), F32), pltpu.VMEM((1, S5_NST), F32)],
        sem=("parallel", "arbitrary"), name=name, comm=comm,
        args=(proj, bbr, bbi, ccr, cci, abr, abi, apr, api, dskip))


def s5_bwd(proj, dys, csr, csi, bbr, bbi, ccr, cci, abr, abi, apr, api, dskip, dproj, *, name):
    T = proj.shape[0]
    nb = T // S5_TB
    zeros8 = functools.partial(jnp.zeros, (S5_SEG, S5_NST), F32)

    def body(u_ref, gy_ref, csr_ref, csi_ref, bbr_ref, bbi_ref, ccr_ref, cci_ref, ar_ref, ai_ref, apr_ref, api_ref,
             d_ref, dproj_ref, du_ref, dbr_ref, dbi_ref, dcr_ref, dci_ref, dd_ref, dar_ref, dai_ref,
             tmp_ref, up_ref, gyp_ref, dr_ref, di_ref, sr_ref, si_ref, gr_ref, gi_ref, car_ref, cai_ref):
        b = pl.program_id(1)

        @pl.when(b == 0)
        def _():
            car_ref[...] = jnp.zeros_like(car_ref)
            cai_ref[...] = jnp.zeros_like(cai_ref)
            for ref in (dbr_ref, dbi_ref, dcr_ref, dci_ref, dd_ref, dar_ref, dai_ref):
                ref[...] = jnp.zeros_like(ref)

        tmp_ref[...] = u_ref[...].astype(F32)
        _s5_permute_in(tmp_ref, up_ref)
        tmp_ref[...] = gy_ref[...].astype(F32)
        _s5_permute_in(tmp_ref, gyp_ref)
        upb = up_ref[...].astype(BF16)
        gyp = gyp_ref[...]
        gypb = gyp.astype(BF16)
        dr_ref[...] = _dot(upb, bbr_ref[0]).reshape(S5_LS, S5_SEG, S5_NST)
        di_ref[...] = _dot(upb, bbi_ref[0]).reshape(S5_LS, S5_SEG, S5_NST)
        a_r = jnp.broadcast_to(ar_ref[0], (S5_SEG, S5_NST))
        a_i = jnp.broadcast_to(ai_ref[0], (S5_SEG, S5_NST))
        fin_r, fin_i = _s5_scan(a_r, a_i, dr_ref, di_ref, zeros8(), zeros8())
        cin_r, cin_i, _, _ = _s5_stitch(apr_ref[0], api_ref[0], fin_r, fin_i, csr_ref[0, 0], csi_ref[0, 0])
        sr_ref[0] = cin_r
        si_ref[0] = cin_i

        def store_s(i, nr, ni, sr, si):
            sr_ref[i + 1] = nr
            si_ref[i + 1] = ni

        _s5_scan(a_r, a_i, dr_ref, di_ref, cin_r, cin_i, store=store_s)
        s_r = sr_ref[pl.ds(1, S5_LS)].reshape(S5_TB, S5_NST).astype(BF16)
        s_i = si_ref[pl.ds(1, S5_LS)].reshape(S5_TB, S5_NST).astype(BF16)
        dcr_ref[0] += _dot_tn(s_r, gypb)
        dci_ref[0] -= _dot_tn(s_i, gypb)
        dd_ref[0] += jnp.sum(gyp * up_ref[...], axis=0, keepdims=True)
        dr_ref[...] = _dot_nt(gypb, ccr_ref[0]).reshape(S5_LS, S5_SEG, S5_NST)
        di_ref[...] = (-_dot_nt(gypb, cci_ref[0])).reshape(S5_LS, S5_SEG, S5_NST)
        fin_r, fin_i = _s5_scan(a_r, a_i, dr_ref, di_ref, zeros8(), zeros8(), reverse=True, conj=True)
        gin_r, gin_i, ncr, nci = _s5_stitch(apr_ref[0], api_ref[0], fin_r, fin_i, car_ref[...], cai_ref[...],
                                            reverse=True, conj=True)
        car_ref[...] = ncr
        cai_ref[...] = nci
        gr_ref[S5_LS] = zeros8()
        gi_ref[S5_LS] = zeros8()

        def store_g(i, nr, ni, sr, si):
            gr_ref[i] = nr
            gi_ref[i] = ni
            pr, pi = sr_ref[i], si_ref[i]
            gr_ref[S5_LS] += nr * pr + ni * pi
            gi_ref[S5_LS] += ni * pr - nr * pi

        _s5_scan(a_r, a_i, dr_ref, di_ref, gin_r, gin_i, store=store_g, reverse=True, conj=True)
        dar_ref[0] += jnp.sum(gr_ref[S5_LS], axis=0, keepdims=True)
        dai_ref[0] += jnp.sum(gi_ref[S5_LS], axis=0, keepdims=True)
        g_r = gr_ref[pl.ds(0, S5_LS)].reshape(S5_TB, S5_NST).astype(BF16)
        g_i = gi_ref[pl.ds(0, S5_LS)].reshape(S5_TB, S5_NST).astype(BF16)
        dbr_ref[0] += _dot_tn(upb, g_r)
        dbi_ref[0] += _dot_tn(upb, g_i)
        gyp_ref[...] = _dot_nt(g_r, bbr_ref[0]) + _dot_nt(g_i, bbi_ref[0]) + d_ref[0] * gyp
        _s5_permute_out(gyp_ref, tmp_ref)
        du_ref[...] = tmp_ref[...].astype(BF16)

    tok, par_b, par_c, vec_s, vec_c = _s5_specs(nb, True)
    cs_spec = pl.BlockSpec((1, 1, 1, S5_NST), lambda c, b: (nb - 1 - b, c, 0, 0))
    tokbuf = pltpu.VMEM((S5_TB, S5_CW), F32)
    stbuf = pltpu.VMEM((S5_LS, S5_SEG, S5_NST), F32)
    stbuf1 = pltpu.VMEM((S5_LS + 1, S5_SEG, S5_NST), F32)
    return pl.pallas_call(
        body, grid=(S5_CH, nb),
        in_specs=[tok, tok, cs_spec, cs_spec, par_b, par_b, par_c, par_c, vec_s, vec_s, vec_s, vec_s, vec_c, _ANY],
        out_specs=[tok, par_b, par_b, par_c, par_c, vec_c, vec_s, vec_s], input_output_aliases={13: 0},
        out_shape=[jax.ShapeDtypeStruct(dproj.shape, BF16),
                   jax.ShapeDtypeStruct((S5_CH, S5_CW, S5_NST), F32), jax.ShapeDtypeStruct((S5_CH, S5_CW, S5_NST), F32),
                   jax.ShapeDtypeStruct((S5_CH, S5_NST, S5_CW), F32), jax.ShapeDtypeStruct((S5_CH, S5_NST, S5_CW), F32),
                   jax.ShapeDtypeStruct((S5_CH, 1, S5_CW), F32),
                   jax.ShapeDtypeStruct((S5_CH, 1, S5_NST), F32), jax.ShapeDtypeStruct((S5_CH, 1, S5_NST), F32)],
        scratch_shapes=[tokbuf, tokbuf, tokbuf, stbuf, stbuf, stbuf1, stbuf1, stbuf1, stbuf1,
                        pltpu.VMEM((1, S5_NST), F32), pltpu.VMEM((1, S5_NST), F32)],
        compiler_params=_cp("parallel", "arbitrary"), name=name)(
            proj, dys, csr, csi, bbr, bbi, ccr, cci, abr, abi, apr, api, dskip, dproj)


SSD_L = SSD_CHUNK
SSD_GW = 256
NEG = -1e30


def _expand16(v):
    lane = lax.broadcasted_iota(jnp.int32, (v.shape[0], 128), 1)
    parts = [jnp.where(lane < SSD_HEADDIM, v[:, 2 * j:2 * j + 1], v[:, 2 * j + 1:2 * j + 2]) for j in range(8)]
    return jnp.concatenate(parts, axis=1)


def _softplus(x):
    return jnp.maximum(x, 0.0) + jnp.log(1.0 + jnp.exp(-jnp.abs(x)))


def _ssd_chunk_fwd(z, xbc, tail, dtraw, hprev, cw, cb, dtb, alog, dsk, nw, tril):
    L = SSD_L
    f = {}
    xe = jnp.concatenate([tail, xbc], axis=0)
    sh = [xbc] + [pltpu.roll(xe, s, 0)[8:] for s in (1, 2, 3)]
    conv = cb + cw[3:4] * sh[0] + cw[2:3] * sh[1] + cw[1:2] * sh[2] + cw[0:1] * sh[3]
    sig = _sigmoid(conv)
    xa = conv * sig
    xs, bm, cm = xa[:, :D_SSD], xa[:, D_SSD:D_SSD + 512], xa[:, D_SSD + 512:]
    pre = dtraw + dtb
    dt = _softplus(pre)
    a_h = -jnp.exp(alog)
    acum = _dot_hi(tril, dt * a_h)
    acum_t = acum.T
    alast = acum[L - 1:L]
    exp_a = jnp.exp(acum)
    dec = jnp.exp(alast - acum)
    exp_al = jnp.exp(alast)
    dt_x, dec_x, exp_a_x, exp_al_x = _expand16(dt), _expand16(dec), _expand16(exp_a), _expand16(exp_al)
    d_x = _expand16(dsk)
    xh = xs * dt_x
    xhb = xh.astype(BF16)
    xd = (xh * dec_x).astype(BF16)
    row = lax.broadcasted_iota(jnp.int32, (L, L), 0)
    col = lax.broadcasted_iota(jnp.int32, (L, L), 1)
    causal = row >= col
    lane = lax.broadcasted_iota(jnp.int32, (L, 128), 1)
    low = lane < SSD_HEADDIM
    hb = hprev.astype(BF16)
    y_pairs, yoff_parts, st_parts, cbs, lms = [], [], [], [], []
    for g in range(SSD_NGROUPS):
        bg = bm[:, g * 128:(g + 1) * 128].astype(BF16)
        cg = cm[:, g * 128:(g + 1) * 128].astype(BF16)
        cbg = _dot_nt(cg, bg)
        cbs.append(cbg)
        for j in (2 * g, 2 * g + 1):
            xp = xhb[:, j * 128:(j + 1) * 128]
            ys = []
            for h in (2 * j, 2 * j + 1):
                lm = jnp.exp(jnp.where(causal, acum[:, h:h + 1] - acum_t[h:h + 1, :], NEG))
                lms.append(lm)
                ys.append(_dot((cbg * lm).astype(BF16), xp))
            y_pairs.append(jnp.where(low, ys[0], ys[1]))
        gs = slice(g * SSD_GW, (g + 1) * SSD_GW)
        yoff_parts.append(_dot(cg, hb[:, gs]) * exp_a_x[:, gs])
        st_parts.append(_dot_tn(bg, xd[:, gs]))
    yoff = jnp.concatenate(yoff_parts, axis=1)
    y = jnp.concatenate(y_pairs, axis=1) + yoff + d_x * xs
    hnew = exp_al_x * hprev + jnp.concatenate(st_parts, axis=1)
    sz = _sigmoid(z)
    gz = y * (z * sz)
    r = lax.rsqrt(jnp.mean(gz * gz, axis=-1, keepdims=True) + EPS)
    out = gz * r * nw
    f.update(sh=sh, conv=conv, sig=sig, xs=xs, bm=bm, cm=cm, pre=pre, dt=dt, a_h=a_h, exp_a_x=exp_a_x, dec_x=dec_x,
             exp_al=exp_al, exp_al_x=exp_al_x, dt_x=dt_x, d_x=d_x, xh=xh, xhb=xhb, xd=xd, causal=causal, low=low, hb=hb,
             cbs=cbs, lms=lms, yoff=yoff, y=y, sz=sz, gz=gz, r=r)
    return out, hnew, f


def _ssd_params(conv_w, conv_b, dt_bias, a_log, d_skip, norm_w):
    pad16 = lambda v: jnp.pad(v.reshape(1, SSD_HEADS), ((0, 0), (0, 128 - SSD_HEADS)))
    return (jnp.pad(conv_w, ((0, 8 - SSD_CONV), (0, 0))), conv_b.reshape(1, D_XBC), pad16(dt_bias), pad16(a_log),
            pad16(d_skip), norm_w.reshape(1, D_SSD))


def _ssd_param_specs():
    full = lambda shape: pl.BlockSpec(shape, lambda i: (0, 0))
    return [full((8, D_XBC)), full((1, D_XBC)), full((1, 128)), full((1, 128)), full((1, 128)), full((1, D_SSD))]


def ssd_fwd(proj, dtraw, params, tril, ymix, *, name, comm=None):
    T = proj.shape[0]
    nc = T // SSD_L

    def body(z_ref, x_ref, dt_ref, cw_ref, cb_ref, dtb_ref, al_ref, dsk_ref, nw_ref, tril_ref, ymix_ref,
             o_ref, hs_ref, h_ref, tail_ref):
        i = pl.program_id(0)

        @pl.when(i == 0)
        def _():
            h_ref[...] = jnp.zeros_like(h_ref)
            tail_ref[...] = jnp.zeros_like(tail_ref)

        xbc = x_ref[...].astype(F32)
        hprev = h_ref[...]
        hs_ref[0] = hprev
        out, hnew, _ = _ssd_chunk_fwd(z_ref[...].astype(F32), xbc, tail_ref[...], dt_ref[...], hprev, cw_ref[...],
                                      cb_ref[...], dtb_ref[...], al_ref[...], dsk_ref[...], nw_ref[...], tril_ref[...])
        o_ref[...] = out.astype(BF16)
        h_ref[...] = hnew
        tail_ref[...] = xbc[SSD_L - 8:]

    return _call(
        body, grid=(nc,),
        in_specs=[pl.BlockSpec((SSD_L, D_SSD), lambda i: (i, 1)), pl.BlockSpec((SSD_L, D_XBC), lambda i: (i, 1)),
                  pl.BlockSpec((SSD_L, 128), lambda i: (i, 0))] + _ssd_param_specs()
                 + [pl.BlockSpec((SSD_L, SSD_L), lambda i: (0, 0)), _ANY],
        out_specs=[pl.BlockSpec((SSD_L, D_SSD), lambda i: (i, 1)),
                   pl.BlockSpec((1, SSD_STATE, D_SSD), lambda i: (i, 0, 0))],
        out_shape=[jax.ShapeDtypeStruct(ymix.shape, BF16), jax.ShapeDtypeStruct((nc, SSD_STATE, D_SSD), F32)],
        scratch_shapes=[pltpu.VMEM((SSD_STATE, D_SSD), F32), pltpu.VMEM((8, D_XBC), F32)],
        sem=("arbitrary",), name=name, args=(proj, proj, dtraw, *params, tril, ymix), aliases={10: 0}, comm=comm)


def ssd_bwd(proj, dtraw, hs, dymix, params, tril, triu, trils, headsum, *, name, comm=None):
    T = proj.shape[0]
    nc = T // SSD_L
    L = SSD_L

    def body(z_ref, x_ref, xprev_ref, dt_ref, hs_ref, do_ref, cw_ref, cb_ref, dtb_ref, al_ref, dsk_ref, nw_ref,
             tril_ref, triu_ref, trils_ref, hsum_ref,
             dp_ref, ddt_ref, dcw_ref, dcb_ref, ddtb_ref, dal_ref, ddsk_ref, dnw_ref, dh_ref, dnext_ref):
        i = pl.program_id(0)

        @pl.when(i == 0)
        def _():
            dh_ref[...] = jnp.zeros_like(dh_ref)
            dnext_ref[...] = jnp.zeros_like(dnext_ref)
            for ref in (dcw_ref, dcb_ref, ddtb_ref, dal_ref, ddsk_ref, dnw_ref):
                ref[...] = jnp.zeros_like(ref)

        z = z_ref[...].astype(F32)
        xbc = x_ref[...].astype(F32)
        tail = jnp.where(i == nc - 1, 0.0, xprev_ref[...].astype(F32))
        hprev = hs_ref[0]
        cw, nw = cw_ref[...], nw_ref[...]
        hsum = hsum_ref[...]
        _, _, f = _ssd_chunk_fwd(z, xbc, tail, dt_ref[...], hprev, cw, cb_ref[...], dtb_ref[...], al_ref[...],
                                 dsk_ref[...], nw, tril_ref[...])
        dout = do_ref[...].astype(F32)
        dh = dh_ref[...]
        ghat = f["gz"] * f["r"]
        dn = dout * nw
        dgz = f["r"] * (dn - ghat * jnp.mean(dn * ghat, axis=-1, keepdims=True))
        dnw_ref[...] += jnp.sum(dout * ghat, axis=0, keepdims=True)
        sz = f["sz"]
        dy = dgz * (z * sz)
        dp_ref[:, :D_S5] = jnp.zeros((L, D_S5), BF16)
        dp_ref[:, D_S5:D_S5 + D_SSD] = (dgz * f["y"] * sz * (1.0 + z * (1.0 - sz))).astype(BF16)
        xs = f["xs"]
        ddsk_ref[...] += jnp.sum(_dot_hi(dy * xs, hsum), axis=0, keepdims=True)
        dyb = dy.astype(BF16)
        dye = (dy * f["exp_a_x"]).astype(BF16)
        dhb = dh.astype(BF16)
        lane = lax.broadcasted_iota(jnp.int32, (L, 128), 1)
        sub = lax.broadcasted_iota(jnp.int32, (128, L), 0)
        zero_b = jnp.zeros((L, 128), BF16)
        rsum = jnp.zeros((L, 128), F32)
        csum_t = jnp.zeros((128, L), F32)
        dx_pairs, dxst_parts, db_parts, dc_parts, dhp_parts = [], [], [], [], []
        for g in range(SSD_NGROUPS):
            gs = slice(g * SSD_GW, (g + 1) * SSD_GW)
            bg = f["bm"][:, g * 128:(g + 1) * 128].astype(BF16)
            cg = f["cm"][:, g * 128:(g + 1) * 128].astype(BF16)
            cbg = f["cbs"][g]
            dcb_g = jnp.zeros((L, L), F32)
            for j in (2 * g, 2 * g + 1):
                xp = f["xhb"][:, j * 128:(j + 1) * 128]
                dyp = dyb[:, j * 128:(j + 1) * 128]
                dxs = []
                for half, h in enumerate((2 * j, 2 * j + 1)):
                    lm = f["lms"][h]
                    dyh = jnp.where(f["low"], dyp, zero_b) if half == 0 else jnp.where(f["low"], zero_b, dyp)
                    dw = jnp.where(f["causal"], _dot_nt(dyh, xp), 0.0)
                    w = cbg * lm
                    e = dw * w
                    dcb_g = dcb_g + dw * lm
                    rsum = jnp.where(lane == h, jnp.sum(e, axis=1, keepdims=True), rsum)
                    csum_t = jnp.where(sub == h, jnp.sum(e, axis=0, keepdims=True), csum_t)
                    dxs.append(_dot_tn(w.astype(BF16), dyp))
                dx_pairs.append(jnp.where(f["low"], dxs[0], dxs[1]))
            dcbb = dcb_g.astype(BF16)
            dxst_parts.append(f["dec_x"][:, gs] * _dot(bg, dhb[:, gs]))
            dc_parts.append(_dot(dcbb, bg) + _dot_nt(dye[:, gs], f["hb"][:, gs]))
            db_parts.append(_dot_tn(dcbb, cg) + _dot_nt(f["xd"][:, gs], dhb[:, gs]))
            dhp_parts.append(f["exp_al_x"][:, gs] * dh[:, gs] + _dot_tn(cg, dye[:, gs]))
        dxst = jnp.concatenate(dxst_parts, axis=1)
        dxh = jnp.concatenate(dx_pairs, axis=1) + dxst
        q = _dot_hi(f["yoff"] * dy, hsum)
        dstate = _dot_hi(f["xh"] * dxst, hsum)
        h0t = jnp.sum(_dot_hi(dh * hprev, hsum), axis=0, keepdims=True) * f["exp_al"]
        da = _dot_hi(triu_ref[...], rsum - csum_t.T + q) + _dot_hi(trils_ref[...], dstate) + h0t
        dt, a_h = f["dt"], f["a_h"]
        ddt = _dot_hi(dxh * xs, hsum) + da * a_h
        dal_ref[...] += jnp.sum(da * dt, axis=0, keepdims=True) * a_h
        ddtraw = ddt * _sigmoid(f["pre"])
        first16 = lane < SSD_HEADS
        ddtraw = jnp.where(first16, ddtraw, 0.0)
        ddt_ref[...] = ddtraw
        ddtb_ref[...] += jnp.sum(ddtraw, axis=0, keepdims=True)
        dh_ref[...] = jnp.concatenate(dhp_parts, axis=1)
        dxa = jnp.concatenate([dxh * f["dt_x"] + f["d_x"] * dy] + db_parts + dc_parts, axis=1)
        sig, conv = f["sig"], f["conv"]
        dconv = dxa * sig * (1.0 + conv * (1.0 - sig))
        dcb_ref[...] += jnp.sum(dconv, axis=0, keepdims=True)
        rows = [jnp.sum(dconv * f["sh"][3 - k], axis=0, keepdims=True) for k in range(SSD_CONV)]
        dcw_ref[...] += jnp.concatenate(rows + [jnp.zeros((8 - SSD_CONV, D_XBC), F32)], axis=0)
        de = jnp.concatenate([dconv, dnext_ref[...]], axis=0)
        dxbc = cw[3:4] * dconv
        for s in (1, 2, 3):
            dxbc = dxbc + cw[3 - s:4 - s] * pltpu.roll(de, L + 8 - s, 0)[:L]
        dp_ref[:, D_S5 + D_SSD:] = dxbc.astype(BF16)
        dnext_ref[...] = dconv[:8]

    rev = lambda i: nc - 1 - i
    acc = lambda shape: pl.BlockSpec(shape, lambda i: (0, 0))
    tri = pl.BlockSpec((L, L), lambda i: (0, 0))
    return _call(
        body, grid=(nc,),
        in_specs=[pl.BlockSpec((L, D_SSD), lambda i: (rev(i), 1)), pl.BlockSpec((L, D_XBC), lambda i: (rev(i), 1)),
                  pl.BlockSpec((8, D_XBC), lambda i: (jnp.maximum(rev(i) * (L // 8) - 1, 0), 1)),
                  pl.BlockSpec((L, 128), lambda i: (rev(i), 0)),
                  pl.BlockSpec((1, SSD_STATE, D_SSD), lambda i: (rev(i), 0, 0)),
                  pl.BlockSpec((L, D_SSD), lambda i: (rev(i), 1))] + _ssd_param_specs()
                 + [tri, tri, tri, pl.BlockSpec((D_SSD, 128), lambda i: (0, 0))],
        out_specs=[pl.BlockSpec((L, D_MAIN), lambda i: (rev(i), 0)), pl.BlockSpec((L, 128), lambda i: (rev(i), 0)),
                   acc((8, D_XBC)), acc((1, D_XBC)), acc((1, 128)), acc((1, 128)), acc((1, 128)), acc((1, D_SSD))],
        out_shape=[jax.ShapeDtypeStruct((T, D_MAIN), BF16),
                   jax.ShapeDtypeStruct((T, 128), F32), jax.ShapeDtypeStruct((8, D_XBC), F32),
                   jax.ShapeDtypeStruct((1, D_XBC), F32), jax.ShapeDtypeStruct((1, 128), F32),
                   jax.ShapeDtypeStruct((1, 128), F32), jax.ShapeDtypeStruct((1, 128), F32),
                   jax.ShapeDtypeStruct((1, D_SSD), F32)],
        scratch_shapes=[pltpu.VMEM((SSD_STATE, D_SSD), F32), pltpu.VMEM((8, D_XBC), F32)],
        sem=("arbitrary",), name=name, comm=comm,
        args=(proj, proj, proj, dtraw, hs, dymix, *params, tril, triu, trils, headsum))


def _s5_blockdiag(v, rows_per_group, cols_per_group):
    eye = jnp.eye(S5_SEG, dtype=v.dtype)
    w = v[:, :, :, None, :] * eye[None, :, None, :, None]
    return w.reshape(S5_CH, 8 * rows_per_group, 8 * cols_per_group)


def _s5_blockdiag_extract(w, rows_per_group, cols_per_group):
    eye = jnp.eye(S5_SEG, dtype=w.dtype)
    w5 = w.reshape(S5_CH, 8, rows_per_group, 8, cols_per_group)
    return jnp.sum(w5 * eye[None, :, None, :, None], axis=3)


TM = 512

OFF = dict(mlp_w2=0, mlp_w1=1024, w_out=2048, s5_w_glu=2560, xa_wq=2816, xa_wk=3072, xa_wv=3328, xa_wo=3584, w_in=3840)
ROWS = dict(mlp_w2=1024, mlp_w1=1024, w_out=512, s5_w_glu=256, xa_wq=256, xa_wk=256, xa_wv=256, xa_wo=256, w_in=1028)
TAIL_OFF = 4872
TAIL_ROWS = 120
PACK_ROWS = TAIL_OFF + TAIL_ROWS
HALF_ROWS = PACK_ROWS // 2
SUM_ROWS = 832
N_SHARD = 4
SMALL_L = ("norm_mix", "s5_a_re", "s5_a_im", "s5_log_dt", "s5_b_re", "s5_b_im", "s5_c_re", "s5_c_im", "s5_d",
           "ssd_conv_w", "ssd_conv_b", "ssd_dt_bias", "ssd_a_log", "ssd_d", "ssd_norm", "norm_xattn", "norm_mem",
           "norm_mlp")
SMALL_Q = 72
CONV_ROWS = 8


def _place():
    x, y, c = lax.axis_index("x"), lax.axis_index("y"), lax.axis_index("c")
    chips = [(1 - x, y), (x, 1 - y), (1 - x, 1 - y)]
    return x, y, c, 2 * x + y, chips, (x, y, 1 - c)


def _remote(src, dst, send_sem, recv_sem, to):
    return pltpu.make_async_remote_copy(src_ref=src, dst_ref=dst, send_sem=send_sem, recv_sem=recv_sem,
                                        device_id=to, device_id_type=MESH_T)


def _half(ref, c):
    return ref.at[pl.ds(pl.multiple_of(c * HALF_ROWS, 16), HALF_ROWS), :]


def _dma_sems(*counts):
    return [pltpu.SemaphoreType.DMA((n,)) for n in counts]


def gather_over_ici(wpack):
    def copies(ci, co, cs):
        (w_ref,), (out_ref,), (send, recv, loc) = ci, co, cs
        x, y, c, s, chips, sibling = _place()
        mine = pltpu.make_async_copy(w_ref, out_ref.at[s], loc.at[0])
        sends = [_remote(_half(w_ref, c), _half(out_ref.at[s], c), send.at[j], recv.at[j], (*chip, c))
                 for j, chip in enumerate(chips)]
        lands = [_half(out_ref.at[2 * chip[0] + chip[1]], c) for chip in chips]
        arrivals = [_remote(land, land, send.at[j], recv.at[j], sibling) for j, land in enumerate(lands)]
        return mine, sends, arrivals

    def start(ci, co, cs):
        mine, sends, _ = copies(ci, co, cs)
        mine.start()
        for cp in sends:
            cp.start()

    def wait(ci, co, cs):
        mine, sends, arrivals = copies(ci, co, cs)
        for cp in arrivals:
            cp.wait_recv()
        for cp in sends:
            cp.wait_send()
        mine.wait()

    return Comm("gather_ici", [wpack], [jax.ShapeDtypeStruct((N_SHARD,) + wpack.shape, wpack.dtype)],
                _dma_sems(3, 3, 1), start, wait)


def gather_to_sibling(landed):
    def copies(co, cs):
        (out_ref,), (send, recv) = co, cs
        x, y, c, s, chips, sibling = _place()
        slots = [out_ref.at[2 * chip[0] + chip[1]] for chip in chips]
        sends = [_remote(_half(slot, c), _half(slot, c), send.at[j], recv.at[j], sibling) for j, slot in enumerate(slots)]
        arrivals = [_remote(_half(slot, 1 - c), _half(slot, 1 - c), send.at[j], recv.at[j], sibling)
                    for j, slot in enumerate(slots)]
        return sends, arrivals

    def start(ci, co, cs):
        for cp in copies(co, cs)[0]:
            cp.start()

    def wait(ci, co, cs):
        sends, arrivals = copies(co, cs)
        for cp in arrivals:
            cp.wait_recv()
        for cp in sends:
            cp.wait_send()

    return Comm("gather_d2d", [landed], [jax.ShapeDtypeStruct(landed.shape, landed.dtype)], _dma_sems(3, 3), start, wait,
                aliases={0: 0})


def exchange_halves(gpack):
    def copy(ci, co, cs):
        (g_ref,), (out_ref,), (send, recv) = ci, co, cs
        x, y, c, s, chips, sibling = _place()
        src = g_ref.at[:, pl.ds(pl.multiple_of((1 - c) * HALF_ROWS, 16), HALF_ROWS), :]
        return _remote(src, out_ref, send.at[0], recv.at[0], sibling)

    return Comm("exchange", [gpack], [jax.ShapeDtypeStruct((N_SHARD, HALF_ROWS, D_MODEL), F32)], _dma_sems(1, 1),
                lambda ci, co, cs: copy(ci, co, cs).start(), lambda ci, co, cs: copy(ci, co, cs).wait())


def scatter_chips(csum):
    def copies(ci, co, cs):
        (c_ref,), (out_ref,), (send, recv, loc) = ci, co, cs
        x, y, c, s, chips, sibling = _place()
        mine = pltpu.make_async_copy(c_ref.at[s], out_ref.at[s], loc.at[0])
        sends = [_remote(c_ref.at[2 * chip[0] + chip[1]], out_ref.at[s], send.at[j], recv.at[j], (*chip, c))
                 for j, chip in enumerate(chips)]
        arrivals = [_remote(c_ref.at[2 * chip[0] + chip[1]], out_ref.at[2 * chip[0] + chip[1]], send.at[j], recv.at[j],
                            (*chip, c)) for j, chip in enumerate(chips)]
        return mine, sends, arrivals

    def start(ci, co, cs):
        mine, sends, _ = copies(ci, co, cs)
        mine.start()
        for cp in sends:
            cp.start()

    def wait(ci, co, cs):
        mine, sends, arrivals = copies(ci, co, cs)
        for cp in arrivals:
            cp.wait_recv()
        for cp in sends:
            cp.wait_send()
        mine.wait()

    return Comm("scatter", [csum], [jax.ShapeDtypeStruct(csum.shape, csum.dtype)], _dma_sems(3, 3, 1), start, wait)


def share_reduced(red, gshards, smalls, l):
    def copies(ci, co, cs):
        r_ref, (g_ref, sm_ref), (send, recv, loc) = ci[0], co, cs
        x, y, c, s, chips, sibling = _place()
        my_half = _half(g_ref.at[l], c)
        tail = r_ref.at[pl.ds(HALF_ROWS - TAIL_ROWS, TAIL_ROWS), :]
        keep = pltpu.make_async_copy(r_ref, my_half, loc.at[0])
        big = _remote(r_ref, my_half, send.at[0], recv.at[0], sibling)
        keep_tail = pltpu.make_async_copy(tail, sm_ref.at[l, s], loc.at[1])
        tails = [_remote(tail, sm_ref.at[l, s], send.at[1 + j], recv.at[1 + j], (*chip, 1)) for j, chip in enumerate(chips)]
        tails += [_remote(tail, sm_ref.at[l, s], send.at[4 + j], recv.at[1 + j], (*chip, 0)) for j, chip in enumerate(chips)]
        tails.append(_remote(tail, sm_ref.at[l, s], send.at[7], recv.at[4], sibling))
        other = _half(g_ref.at[l], 1 - c)
        big_in = _remote(other, other, send.at[0], recv.at[0], sibling)
        slots = [sm_ref.at[l, 2 * chip[0] + chip[1]] for chip in chips]
        tails_in = [_remote(slot, slot, send.at[1 + j], recv.at[1 + j], sibling) for j, slot in enumerate(slots)]
        sib_tail_in = _remote(sm_ref.at[l, s], sm_ref.at[l, s], send.at[7], recv.at[4], sibling)
        return c, keep, big, keep_tail, tails, big_in, tails_in, sib_tail_in

    def start(ci, co, cs):
        c, keep, big, keep_tail, tails, _, _, _ = copies(ci, co, cs)
        keep.start()
        big.start()

        @pl.when(c == 1)
        def _():
            keep_tail.start()
            for cp in tails:
                cp.start()

    def wait(ci, co, cs):
        c, keep, big, keep_tail, tails, big_in, tails_in, sib_tail_in = copies(ci, co, cs)
        big_in.wait_recv()
        for cp in tails_in:
            cp.wait_recv()
        big.wait_send()
        keep.wait()

        @pl.when(c == 0)
        def _():
            sib_tail_in.wait_recv()

        @pl.when(c == 1)
        def _():
            for cp in tails:
                cp.wait_send()
            keep_tail.wait()

    sds = lambda a: jax.ShapeDtypeStruct(a.shape, a.dtype)
    return Comm("share", [red, gshards, smalls], [sds(gshards), sds(smalls)], _dma_sems(8, 5, 2), start, wait,
                aliases={1: 0, 2: 1})


def _consts():
    e = np.zeros((S5_STATE, S5_STATE * S5_GROUP), np.float32)
    for p in range(S5_STATE):
        e[p, p * S5_GROUP:(p + 1) * S5_GROUP] = 1.0
    hs = np.zeros((D_SSD, 128), np.float32)
    for h in range(SSD_HEADS):
        hs[h * SSD_HEADDIM:(h + 1) * SSD_HEADDIM, h] = 1.0
    ones = np.ones((SSD_L, SSD_L), np.float32)
    return dict(expand=jnp.asarray(e), expand_t=jnp.asarray(e.T), headsum=jnp.asarray(hs),
                tril=jnp.asarray(np.tril(ones)), triu=jnp.asarray(np.triu(ones)), trils=jnp.asarray(np.tril(ones, -1)))


def _s5_mats(w, cst):
    b_re = w["s5_b_re"].reshape(S5_GROUPS, S5_STATE * S5_GROUP)
    b_im = w["s5_b_im"].reshape(S5_GROUPS, S5_STATE * S5_GROUP)
    abr, abi, apr, api, bbr, bbi = s5_params_fwd(w["s5_a_re"], w["s5_a_im"], w["s5_log_dt"].reshape(S5_GROUPS, 1),
                                                 b_re, b_im, cst["expand"], name="s5_params_fwd")
    t = lambda v: v.reshape(S5_CH, 8, S5_STATE, S5_GROUP).transpose(0, 1, 3, 2)
    c4 = lambda v: v.reshape(S5_CH, 8, S5_GROUP, S5_STATE).transpose(0, 1, 3, 2)
    vec = lambda v: v.reshape(S5_CH, 1, S5_NST)
    return dict(bbr=_s5_blockdiag(t(bbr), S5_GROUP, S5_STATE).astype(BF16),
                bbi=_s5_blockdiag(t(bbi), S5_GROUP, S5_STATE).astype(BF16),
                ccr=_s5_blockdiag(c4(w["s5_c_re"]), S5_STATE, S5_GROUP).astype(BF16),
                cci=_s5_blockdiag(c4(w["s5_c_im"]), S5_STATE, S5_GROUP).astype(BF16),
                abr=vec(abr), abi=vec(abi), apr=vec(apr), api=vec(api),
                dsk=w["s5_d"].reshape(S5_CH, 1, S5_CW), b_re=b_re, b_im=b_im)


def _layer_fwd(x, mem, w, cst, next_pack=None):
    sv = {}
    g = lambda n: w[n].reshape(1, -1)
    pack = w["pack"]
    proj, h0, r0, dtraw = norm_matmul(x, g("norm_mix"), w["w_in_main"], w["w_in_dt"], tm=TM, tn=1024, name="in_proj")
    s5m = _s5_mats(w, cst)
    (ys, csr, csi), landed = s5_fwd(proj, s5m["bbr"], s5m["bbi"], s5m["ccr"], s5m["cci"], s5m["abr"], s5m["abi"],
                                    s5m["apr"], s5m["api"], s5m["dsk"], name="s5_fwd",
                                    comm=None if next_pack is None else gather_over_ici(next_pack))
    (ymix, tglu), gathered = glu_fwd(ys, w["s5_w_glu"], tm=TM, name="glu_fwd",
                                     comm=None if next_pack is None else gather_to_sibling(landed[0]))
    ssdp = _ssd_params(w["ssd_conv_w"], w["ssd_conv_b"], w["ssd_dt_bias"], w["ssd_a_log"], w["ssd_d"], w["ssd_norm"])
    (ymix, hs), _ = ssd_fwd(proj, dtraw, ssdp, cst["tril"], ymix, name="ssd_fwd")
    x1 = matmul_res(ymix, pack, x, tm=TM, tn=1024, tk=512, name="out_proj",
                    wspec=pl.BlockSpec((1, 512, 1024), lambda i, j, k: (k, OFF["w_out"] // 512, j)))
    q, h1, r1 = norm_matmul(x1, g("norm_xattn"), w["xa_wq"], tm=TM, tn=1024, name="q_proj")
    kv, hm, rm = norm_matmul(mem, g("norm_mem"), w["xa_wkv"], tm=mem.shape[0], tn=1024, name="kv_proj")
    o = attn_fwd(q, kv, tm=TM, name="attn_fwd")
    x2 = matmul_res(o, w["xa_wo"], x1, tm=TM, tn=1024, tk=1024, name="attn_out")
    f1, h2, r2 = norm_matmul(x2, g("norm_mlp"), pack, tm=TM, tn=1024, name="mlp_up", n_out=D_FF,
                             wspec=pl.BlockSpec((1, D_MODEL, 1024), lambda i, j: (j, OFF["mlp_w1"] // 1024, 0)))
    x3 = matmul_res(f1, pack, x2, act="relu2", tm=TM, tn=1024, tk=1024, name="mlp_down",
                    wspec=pl.BlockSpec((1, 1024, 1024), lambda i, j, k: (k, OFF["mlp_w2"] // 1024, j)))
    sv.update(x=x, proj=proj, h0=h0, r0=r0, dtraw=dtraw, s5m=s5m, ys=ys, csr=csr, csi=csi, tglu=tglu, ssdp=ssdp,
              hs=hs, ymix=ymix, x1=x1, q=q, h1=h1, r1=r1, kv=kv, hm=hm, rm=rm, o=o, x2=x2, f1=f1, h2=h2, r2=r2)
    return x3, sv, (gathered[0] if next_pack is not None else None)


def _layer_bwd(dx3, mem, w, sv, cst, extra_small=None, reduce_hooks=None):
    gr = {}
    g = lambda n: w[n].reshape(1, -1)
    pack = w["pack"]
    pshape = (N_SHARD, PACK_ROWS, D_MODEL)
    ps = lambda rows, f: pl.BlockSpec((1, rows, 1024), f)
    dh1 = matmul_nt(dx3, pack, epi="relu2bwd", epi_args=(sv["f1"],), tm=TM, tko=1024, tn=1024, name="mlp_down_dx",
                    wspec=ps(1024, lambda i, k, n: (k, OFF["mlp_w2"] // 1024, n)), k_out=D_FF,
                    comm=reduce_hooks.exchange() if reduce_hooks else None)
    if reduce_hooks:
        dh1, recv = dh1
        reduce_hooks.after_exchange(recv[0])
    gp = matmul_tn(sv["f1"], dx3, act="relu2", tk=1024, tn=1024, tt=TM, name="mlp_down_dw", pack_shape=pshape,
                   pack_spec=ps(1024, lambda k, n, t: (k, OFF["mlp_w2"] // 1024, 0)))
    gp = matmul_tn(sv["h2"], dh1, tk=1024, tn=1024, tt=TM, name="mlp_up_dw", pack=gp, pack_shape=pshape,
                   pack_spec=ps(1024, lambda k, n, t: (n, OFF["mlp_w1"] // 1024, 0)))
    dx2, gr["norm_mlp"] = matmul_nt(dh1, pack, epi="rmsbwd", epi_args=(sv["x2"], sv["r2"], g("norm_mlp"), dx3),
                                    tm=TM, tko=1024, tn=1024, name="mlp_up_dx", k_out=D_MODEL,
                                    wspec=ps(1024, lambda i, k, n: (n, OFF["mlp_w1"] // 1024, 0)))
    do = matmul_nt(dx2, w["xa_wo"], tm=TM, tko=1024, tn=1024, name="attn_out_dx")
    gp = matmul_tn(sv["o"], dx2, tk=256, tn=1024, tt=TM, name="attn_out_dw", pack=gp, pack_shape=pshape,
                   pack_spec=ps(256, lambda k, n, t: (k, OFF["xa_wo"] // 256, 0)))
    dq, dkv = attn_bwd(sv["q"], sv["kv"], do, tm=TM, name="attn_bwd")
    gp = matmul_tn(sv["h1"], dq, tk=256, tn=1024, tt=TM, name="q_proj_dw", pack=gp, pack_shape=pshape,
                   pack_spec=ps(256, lambda k, n, t: (k, OFF["xa_wq"] // 256, 0)))
    dx1, gr["norm_xattn"] = matmul_nt(dq, w["xa_wq"], epi="rmsbwd", epi_args=(sv["x1"], sv["r1"], g("norm_xattn"), dx2),
                                      tm=TM, tko=1024, tn=1024, name="q_proj_dx")
    M = mem.shape[0]
    gp = matmul_tn(sv["hm"], dkv, tk=256, tn=1024, tt=M, name="kv_proj_dw", pack=gp, pack_shape=pshape,
                   pack_spec=ps(256, lambda k, n, t: (k, OFF["xa_wk"] // 256 + n, 0)))
    _, gr["norm_mem"] = matmul_nt(dkv, w["xa_wkv"], epi="rmsbwd",
                                  epi_args=(mem, sv["rm"], g("norm_mem"), jnp.zeros_like(mem)),
                                  tm=M, tko=1024, tn=1024, name="kv_proj_dx")
    dymix = matmul_nt(dx1, pack, tm=TM, tko=512, tn=1024, name="out_proj_dx", k_out=2 * D_MODEL,
                      wspec=ps(512, lambda i, k, n: (k, OFF["w_out"] // 512, n)))
    gp = matmul_tn(sv["ymix"], dx1, tk=512, tn=1024, tt=TM, name="out_proj_dw", pack=gp, pack_shape=pshape,
                   pack_spec=ps(512, lambda k, n, t: (k, OFF["w_out"] // 512, 0)))
    (dproj, ddtraw, dcw, dcb, ddtb, dal, ddsk, dnw), parts = ssd_bwd(
        sv["proj"], sv["dtraw"], sv["hs"], dymix, sv["ssdp"], cst["tril"], cst["triu"], cst["trils"], cst["headsum"],
        name="ssd_bwd", comm=reduce_hooks.scatter() if reduce_hooks else None)
    if reduce_hooks:
        reduce_hooks.after_scatter(parts[0])
    gr["ssd_conv_w"], gr["ssd_conv_b"] = dcw[:SSD_CONV], dcb[0]
    gr["ssd_dt_bias"], gr["ssd_a_log"], gr["ssd_d"] = ddtb[0, :SSD_HEADS], dal[0, :SSD_HEADS], ddsk[0, :SSD_HEADS]
    gr["ssd_norm"] = dnw[0]
    dtg, da1 = glu_bwd_pre(dymix, sv["ys"], sv["tglu"], tm=TM, name="glu_bwd_pre")
    gp = matmul_tn(sv["ys"], dtg, act="gelu", tk=256, tn=1024, tt=TM, name="glu_dw", pack=gp, pack_shape=pshape,
                   pack_spec=ps(256, lambda k, n, t: (k, OFF["s5_w_glu"] // 256, 0)))
    dys = matmul_nt(dtg, w["s5_w_glu"], epi="glubwd", epi_args=(da1, sv["ys"]), tm=TM, tko=1024, tn=1024,
                    name="glu_dx", comm=reduce_hooks.share() if reduce_hooks else None)
    if reduce_hooks:
        dys, shared = dys
        reduce_hooks.after_share(shared)
    s5m = sv["s5m"]
    dproj, dbbr, dbbi, dccr, dcci, dd, dabr, dabi = s5_bwd(
        sv["proj"], dys, sv["csr"], sv["csi"], s5m["bbr"], s5m["bbi"], s5m["ccr"], s5m["cci"], s5m["abr"], s5m["abi"],
        s5m["apr"], s5m["api"], s5m["dsk"], dproj, name="s5_bwd")
    tb = lambda v: _s5_blockdiag_extract(v, S5_GROUP, S5_STATE).transpose(0, 1, 3, 2).reshape(S5_GROUPS, -1)
    tc = lambda v: _s5_blockdiag_extract(v, S5_STATE, S5_GROUP).transpose(0, 1, 3, 2).reshape(S5_GROUPS, S5_GROUP, S5_STATE)
    gr["s5_c_re"], gr["s5_c_im"] = tc(dccr), tc(dcci)
    gr["s5_d"] = dd.reshape(S5_GROUPS, S5_GROUP)
    dar, dai, dld, dbr, dbi = s5_params_bwd(
        w["s5_a_re"], w["s5_a_im"], w["s5_log_dt"].reshape(S5_GROUPS, 1), s5m["b_re"], s5m["b_im"],
        dabr.reshape(S5_GROUPS, S5_STATE), dabi.reshape(S5_GROUPS, S5_STATE), tb(dbbr), tb(dbbi),
        cst["expand"], cst["expand_t"], name="s5_params_bwd")
    gr["s5_a_re"], gr["s5_a_im"], gr["s5_log_dt"] = dar, dai, dld[:, 0]
    gr["s5_b_re"] = dbr.reshape(S5_GROUPS, S5_STATE, S5_GROUP)
    gr["s5_b_im"] = dbi.reshape(S5_GROUPS, S5_STATE, S5_GROUP)
    dw_main = matmul_tn(sv["h0"], dproj, tk=1024, tn=1024, tt=TM, name="in_proj_dw")
    dw_dt = matmul_tn(sv["h0"], ddtraw, tk=1024, tn=D_DT_PAD, tt=TM, name="in_proj_dt_dw")
    dx0, gr["norm_mix"] = matmul_nt(dproj, w["w_in_main"], g2=ddtraw, w2=w["w_in_dt"], epi="rmsbwd",
                                    epi_args=(sv["x"], sv["r0"], g("norm_mix"), dx1), tm=TM, tko=1024, tn=1024,
                                    name="in_proj_dx")
    gr = {k: (v[0] if k.startswith("norm_") else v) for k, v in gr.items()}
    dw_in = jnp.concatenate([dw_main, dw_dt[:, :SSD_HEADS]], axis=1)
    dw_in = dw_in.reshape(D_MODEL, N_SHARD, ROWS["w_in"]).transpose(1, 0, 2).reshape(N_SHARD, ROWS["w_in"], D_MODEL)
    small = [gr[n].reshape(-1) for n in SMALL_L] + ([] if extra_small is None else [extra_small.reshape(-1)])
    small = jnp.concatenate(small)
    small = jnp.pad(small, (0, N_SHARD * SMALL_Q * D_MODEL - small.size)).reshape(N_SHARD, SMALL_Q, D_MODEL)
    rest = jnp.concatenate([dw_in, jnp.zeros((N_SHARD, TAIL_OFF - OFF["w_in"] - ROWS["w_in"], D_MODEL), F32), small,
                            jnp.zeros((N_SHARD, TAIL_ROWS - SMALL_Q, D_MODEL), F32)], axis=1)
    gp = lax.dynamic_update_slice(gp, rest, (0, OFF["w_in"], 0))
    return dx0, gp


def _local_step(x, mem, target, layers, norm_final):
    cst = _consts()
    saved = []
    for l in range(DEPTH):
        x, sv, _ = _layer_fwd(x, mem, layers[l], cst)
        saved.append(sv)
    loss, dx, dgf = loss_head(x, norm_final.reshape(1, -1), target, tm=TM, name="loss_head")
    packs = [None] * DEPTH
    for l in reversed(range(DEPTH)):
        dx, packs[l] = _layer_bwd(dx, mem, layers[l], saved[l], cst, extra_small=dgf[0] if l == DEPTH - 1 else None)
    return loss, dx, packs


def sum_halves(gpack, recv, c_idx):
    nb = HALF_ROWS // SUM_ROWS

    def body(c_ref, a_ref, b_ref, o_ref):
        o_ref[...] = (a_ref[...] + b_ref[...]).astype(BF16)

    blk = (1, SUM_ROWS, D_MODEL)
    return pl.pallas_call(
        body,
        grid_spec=pltpu.PrefetchScalarGridSpec(
            num_scalar_prefetch=1, grid=(N_SHARD, nb),
            in_specs=[pl.BlockSpec(blk, lambda t, i, c_ref: (t, c_ref[0] * nb + i, 0)),
                      pl.BlockSpec(blk, lambda t, i, c_ref: (t, i, 0))],
            out_specs=pl.BlockSpec(blk, lambda t, i, c_ref: (t, i, 0))),
        out_shape=jax.ShapeDtypeStruct((N_SHARD, HALF_ROWS, D_MODEL), BF16),
        compiler_params=_cp("parallel", "parallel"), name="sum_halves")(c_idx, gpack, recv)


def sum_chips(parts):
    nb = HALF_ROWS // SUM_ROWS

    def body(p0, p1, p2, p3, o_ref):
        o_ref[...] = ((p0[0].astype(F32) + p1[0].astype(F32)) + p2[0].astype(F32)) + p3[0].astype(F32)

    specs = [pl.BlockSpec((1, SUM_ROWS, D_MODEL), functools.partial(lambda i, k: (k, i, 0), k=k)) for k in range(N_SHARD)]
    return pl.pallas_call(
        body, grid=(nb,), in_specs=specs, out_specs=pl.BlockSpec((SUM_ROWS, D_MODEL), lambda i: (i, 0)),
        out_shape=jax.ShapeDtypeStruct((HALF_ROWS, D_MODEL), F32),
        compiler_params=_cp("parallel"), name="sum_chips")(parts, parts, parts, parts)


class _Reduction:
    def __init__(self, gpack, layer, core_idx, gshards, smalls):
        self.gpack, self.layer, self.core_idx, self.gshards, self.smalls = gpack, layer, core_idx, gshards, smalls

    def exchange(self):
        return exchange_halves(self.gpack)

    def after_exchange(self, recv):
        self.csum = sum_halves(self.gpack, recv, self.core_idx)

    def scatter(self):
        return scatter_chips(self.csum)

    def after_scatter(self, parts):
        self.red = sum_chips(parts)

    def share(self):
        return share_reduced(self.red, self.gshards, self.smalls, self.layer)

    def after_share(self, shared):
        self.gshards, self.smalls = shared

    def run_alone(self):
        self.after_exchange(_comm_only(self.exchange())[0])
        self.after_scatter(_comm_only(self.scatter())[0])
        self.after_share(_comm_only(self.share()))
        return self.gshards, self.smalls


def adamw(w, g, m, v, *, name):
    shape = w.shape
    cols = shape[-1]
    rows = w.size // cols
    tr = 512 if rows % 512 == 0 else rows
    c1 = 1.0 / (1.0 - ADAM_B1 ** ADAM_STEP)
    c2 = 1.0 / (1.0 - ADAM_B2 ** ADAM_STEP)

    def body(w_ref, g_ref, m_ref, v_ref, d_ref, nm_ref, nv_ref):
        gv = g_ref[...]
        nm = ADAM_B1 * m_ref[...] + (1.0 - ADAM_B1) * gv
        nv = ADAM_B2 * v_ref[...] + (1.0 - ADAM_B2) * (gv * gv)
        d_ref[...] = -ADAM_LR * ((nm * c1) / (jnp.sqrt(nv * c2) + ADAM_EPS) + ADAM_WD * w_ref[...])
        nm_ref[...] = nm
        nv_ref[...] = nv

    spec = pl.BlockSpec((tr, cols), lambda i: (i, 0))
    sds = jax.ShapeDtypeStruct((rows, cols), F32)
    outs = pl.pallas_call(body, grid=(rows // tr,), in_specs=[spec] * 4, out_specs=[spec] * 3, out_shape=[sds] * 3,
                          compiler_params=_cp("parallel"), name=name)(
                              *[a.reshape(rows, cols) for a in (w, g, m, v)])
    return [o.reshape(shape) for o in outs]


def _own_pack(wts, l):
    rows = [wts[n][l].reshape(ROWS[n], D_MODEL) for n in sorted(OFF, key=OFF.get)]
    cw = wts["ssd_conv_w"][l].reshape(-1)
    hi = lax.reduce_precision(cw, 8, 7)
    mid = lax.reduce_precision(cw - hi, 8, 7)
    lo = lax.reduce_precision(cw - hi - mid, 8, 7)
    conv = jnp.pad(jnp.concatenate([hi, mid, lo]), (0, CONV_ROWS * D_MODEL - 3 * cw.size)).reshape(CONV_ROWS, D_MODEL)
    gap = jnp.zeros((TAIL_OFF - OFF["w_in"] - ROWS["w_in"], D_MODEL), F32)
    rest = jnp.zeros((TAIL_ROWS - CONV_ROWS, D_MODEL), F32)
    return jnp.concatenate(rows + [gap, conv, rest], axis=0).astype(BF16)


def _layer_weights(gathered, wts, l):
    w = {n: wts[n][l] for n in SMALL_L if n != "ssd_conv_w"}
    w["pack"] = gathered
    square = lambda n: gathered[:, OFF[n]:OFF[n] + ROWS[n]].reshape(N_SHARD * ROWS[n], D_MODEL)
    w["s5_w_glu"], w["xa_wq"], w["xa_wo"] = square("s5_w_glu"), square("xa_wq"), square("xa_wo")
    w["xa_wkv"] = jnp.concatenate([square("xa_wk"), square("xa_wv")], axis=1)
    cols = ROWS["w_in"]
    w_in = gathered[:, OFF["w_in"]:OFF["w_in"] + cols].reshape(N_SHARD, D_MODEL, cols).transpose(1, 0, 2)
    w_in = w_in.reshape(D_MODEL, N_SHARD * cols)
    w["w_in_main"] = w_in[:, :D_MAIN]
    w["w_in_dt"] = jnp.pad(w_in[:, D_MAIN:], ((0, 0), (0, D_DT_PAD - SSD_HEADS)))
    per = SSD_CONV * D_XBC // N_SHARD
    cw = gathered[:, TAIL_OFF:TAIL_OFF + CONV_ROWS].astype(F32).reshape(N_SHARD, -1)[:, :3 * per]
    cw = cw.reshape(N_SHARD, 3, SSD_CONV, D_XBC // N_SHARD)
    cw = (cw[:, 0] + cw[:, 1]) + cw[:, 2]
    w["ssd_conv_w"] = cw.transpose(1, 0, 2).reshape(SSD_CONV, D_XBC)
    return w


def kernel(x, mem, norm_mix, w_in, s5_a_re, s5_a_im, s5_log_dt, s5_b_re, s5_b_im, s5_c_re, s5_c_im, s5_d, s5_w_glu, ssd_conv_w, ssd_conv_b, ssd_dt_bias, ssd_a_log, ssd_d, ssd_norm, w_out, norm_xattn, norm_mem, xa_wq, xa_wk, xa_wv, xa_wo, norm_mlp, mlp_w1, mlp_w2, norm_final, loss_target, m_norm_mix, m_w_in, m_s5_a_re, m_s5_a_im, m_s5_log_dt, m_s5_b_re, m_s5_b_im, m_s5_c_re, m_s5_c_im, m_s5_d, m_s5_w_glu, m_ssd_conv_w, m_ssd_conv_b, m_ssd_dt_bias, m_ssd_a_log, m_ssd_d, m_ssd_norm, m_w_out, m_norm_xattn, m_norm_mem, m_xa_wq, m_xa_wk, m_xa_wv, m_xa_wo, m_norm_mlp, m_mlp_w1, m_mlp_w2, m_norm_final, v_norm_mix, v_w_in, v_s5_a_re, v_s5_a_im, v_s5_log_dt, v_s5_b_re, v_s5_b_im, v_s5_c_re, v_s5_c_im, v_s5_d, v_s5_w_glu, v_ssd_conv_w, v_ssd_conv_b, v_ssd_dt_bias, v_ssd_a_log, v_ssd_d, v_ssd_norm, v_w_out, v_norm_xattn, v_norm_mem, v_xa_wq, v_xa_wk, v_xa_wv, v_xa_wo, v_norm_mlp, v_mlp_w1, v_mlp_w2, v_norm_final):
    names = ("norm_mix", "w_in", "s5_a_re", "s5_a_im", "s5_log_dt", "s5_b_re", "s5_b_im", "s5_c_re", "s5_c_im", "s5_d",
             "s5_w_glu", "ssd_conv_w", "ssd_conv_b", "ssd_dt_bias", "ssd_a_log", "ssd_d", "ssd_norm", "w_out",
             "norm_xattn", "norm_mem", "xa_wq", "xa_wk", "xa_wv", "xa_wo", "norm_mlp", "mlp_w1", "mlp_w2", "norm_final")
    loc = locals()
    wts = {n: loc[n] for n in names}
    mom = {n: loc["m_" + n] for n in names}
    var = {n: loc["v_" + n] for n in names}
    shard = 2 * lax.axis_index("x") + lax.axis_index("y")
    core = lax.axis_index("c")

    cst = _consts()
    core_idx = core.astype(jnp.int32).reshape(1)
    h, mem0 = x[0], mem[0]

    landed = _comm_only(gather_over_ici(_own_pack(wts, 0)))[0]
    gathered = _comm_only(gather_to_sibling(landed))[0]
    layers, saved = [], []
    for l in range(DEPTH):
        layers.append(_layer_weights(gathered, wts, l))
        h, sv, gathered = _layer_fwd(h, mem0, layers[l], cst, next_pack=_own_pack(wts, l + 1) if l + 1 < DEPTH else None)
        saved.append(sv)
    loss, dx, dgf = loss_head(h, norm_final.reshape(1, -1), loss_target[0], tm=TM, name="loss_head")

    gshards = jnp.zeros((DEPTH, PACK_ROWS, D_MODEL), F32)
    smalls = jnp.zeros((DEPTH, N_SHARD, TAIL_ROWS, D_MODEL), F32)
    pending = None
    for l in reversed(range(DEPTH)):
        dx, gpack = _layer_bwd(dx, mem0, layers[l], saved[l], cst, extra_small=dgf[0] if l == DEPTH - 1 else None,
                               reduce_hooks=pending)
        if pending is not None:
            gshards, smalls = pending.gshards, pending.smalls
        pending = _Reduction(gpack, l, core_idx, gshards, smalls)
    gshards, smalls = pending.run_alone()

    g = {n: gshards[:, OFF[n]:OFF[n] + ROWS[n]].reshape(wts[n].shape) for n in OFF}
    small_red = smalls[:, :, :SMALL_Q].reshape(DEPTH, -1)
    off = 0
    for n in SMALL_L:
        shape = (SSD_CONV, D_XBC) if n == "ssd_conv_w" else wts[n].shape[1:]
        size = math.prod(shape)
        g[n] = small_red[:, off:off + size].reshape((DEPTH,) + shape)
        off += size
    g["norm_final"] = small_red[DEPTH - 1, off:off + D_MODEL]
    g["ssd_conv_w"] = lax.dynamic_slice_in_dim(g["ssd_conv_w"], shard * (D_XBC // N_SHARD), D_XBC // N_SHARD, axis=2)

    deltas, new_m, new_v = {}, {}, {}
    for n in names:
        deltas[n], new_m[n], new_v[n] = adamw(wts[n], g[n], mom[n], var[n], name="adamw_" + n)
    loss_all = lax.psum(loss[0, 0], ("x", "y", "c"))
    return (loss_all, dx[None], *[g[n] for n in names], *[deltas[n] for n in names], *[new_m[n] for n in names],
            *[new_v[n] for n in names])
```

```python
import functools
import math

import jax
import jax.numpy as jnp
import numpy as np
from jax import lax
from jax.experimental import pallas as pl
from jax.experimental.pallas import tpu as pltpu

F32 = jnp.float32
BF16 = jnp.bfloat16
HIGHEST = lax.Precision.HIGHEST

D_MODEL = 1024
DEPTH = 4
D_S5 = 1024
D_SSD = 1024
S5_GROUP = 16
S5_GROUPS = 64
S5_STATE = 64
SSD_HEADS = 16
SSD_HEADDIM = 64
SSD_NGROUPS = 4
SSD_STATE = 128
SSD_CONV = 4
SSD_CHUNK = 128
D_XBC = 2048
D_MAIN = 4096
D_DT_PAD = 128
XA_HEADS = 4
XA_HEAD_DIM = 256
D_FF = 4096
EPS = 1e-5
ADAM_LR, ADAM_B1, ADAM_B2, ADAM_EPS, ADAM_WD, ADAM_STEP = 0.001, 0.9, 0.999, 1e-08, 0.01, 10

VMEM_LIMIT = 56 * 1024 * 1024
MESH_T = pl.DeviceIdType.MESH


def _cp(*sem):
    return pltpu.CompilerParams(dimension_semantics=tuple(sem) if sem else None, vmem_limit_bytes=VMEM_LIMIT)


_ANY = pl.BlockSpec(memory_space=pl.ANY)


class Comm:
    def __init__(self, name, inputs, out_shapes, sems, start, wait, aliases=None):
        self.name, self.inputs, self.out_shapes, self.sems = name, list(inputs), list(out_shapes), list(sems)
        self.start, self.wait, self.aliases = start, wait, dict(aliases or {})


def _call(body, *, grid, in_specs, out_specs, out_shape, args, scratch_shapes=(), sem, name, comm=None, aliases=None):
    in_specs, out_specs, out_shape = list(in_specs), list(out_specs), list(out_shape)
    scratch_shapes = list(scratch_shapes)
    aliases = dict(aliases or {})
    if comm is None:
        res = pl.pallas_call(body, grid=grid, in_specs=in_specs, out_specs=out_specs, out_shape=out_shape,
                             scratch_shapes=scratch_shapes, compiler_params=_cp(*sem), name=name,
                             input_output_aliases=aliases)(*args)
        return list(res), []
    n_in, n_out, n_scr = len(in_specs), len(out_specs), len(scratch_shapes)
    c_in, c_out = len(comm.inputs), len(comm.out_shapes)

    def wrapped(*refs):
        a, refs = refs[:n_in], refs[n_in:]
        ci, refs = refs[:c_in], refs[c_in:]
        o, refs = refs[:n_out], refs[n_out:]
        co, refs = refs[:c_out], refs[c_out:]
        s, cs = refs[:n_scr], refs[n_scr:]
        first = functools.reduce(jnp.logical_and, [pl.program_id(d) == 0 for d in range(len(grid))])
        last = functools.reduce(jnp.logical_and, [pl.program_id(d) == grid[d] - 1 for d in range(len(grid))])

        @pl.when(first)
        def _():
            comm.start(ci, co, cs)

        body(*a, *o, *s)

        @pl.when(last)
        def _():
            comm.wait(ci, co, cs)

    for i, j in comm.aliases.items():
        aliases[n_in + i] = n_out + j
    res = pl.pallas_call(wrapped, grid=grid, in_specs=in_specs + [_ANY] * c_in, out_specs=out_specs + [_ANY] * c_out,
                         out_shape=out_shape + comm.out_shapes, scratch_shapes=scratch_shapes + comm.sems,
                         compiler_params=_cp(*(("arbitrary",) * len(grid))), name=name + "_" + comm.name,
                         input_output_aliases=aliases)(*args, *comm.inputs)
    return list(res[:n_out]), list(res[n_out:])


def _comm_only(comm):
    def body(*refs):
        ci, refs = refs[:len(comm.inputs)], refs[len(comm.inputs):]
        co, cs = refs[:len(comm.out_shapes)], refs[len(comm.out_shapes):]
        comm.start(ci, co, cs)
        comm.wait(ci, co, cs)

    res = pl.pallas_call(body, in_specs=[_ANY] * len(comm.inputs), out_specs=[_ANY] * len(comm.out_shapes),
                         out_shape=comm.out_shapes, scratch_shapes=comm.sems, name=comm.name,
                         input_output_aliases=comm.aliases)(*comm.inputs)
    return list(res)


def _dot(a, b):
    return jnp.dot(a, b, preferred_element_type=F32)


def _dot_nt(a, b):
    return lax.dot_general(a, b, (((1,), (1,)), ((), ())), preferred_element_type=F32)


def _dot_tn(a, b):
    return lax.dot_general(a, b, (((0,), (0,)), ((), ())), preferred_element_type=F32)


def _dot_hi(a, b):
    return jnp.dot(a, b, precision=HIGHEST, preferred_element_type=F32)


def _gelu(x):
    c = math.sqrt(2.0 / math.pi)
    return 0.5 * x * (1.0 + jnp.tanh(c * (x + 0.044715 * x * x * x)))


def _gelu_grad(x):
    c = math.sqrt(2.0 / math.pi)
    t = jnp.tanh(c * (x + 0.044715 * x * x * x))
    return 0.5 * (1.0 + t) + 0.5 * x * (1.0 - t * t) * c * (1.0 + 3 * 0.044715 * x * x)


def _sigmoid(x):
    return 1.0 / (1.0 + jnp.exp(-x))


def _act(a, act):
    if act is None:
        return a.astype(BF16)
    a = a.astype(F32)
    if act == "relu2":
        a = jnp.maximum(a, 0.0)
        return (a * a).astype(BF16)
    if act == "gelu":
        return _gelu(a).astype(BF16)
    raise ValueError(act)


def norm_matmul(x, g, w, w2=None, *, tm, tn, name, wspec=None, n_out=None):
    T, D = x.shape
    N = w.shape[1] if wspec is None else n_out
    wget = (lambda r: r[...]) if wspec is None else (lambda r: r[0])
    has2 = w2 is not None

    def body(x_ref, g_ref, w_ref, *rest):
        if has2:
            w2_ref, o_ref, h_ref, r_ref, o2_ref = rest
        else:
            o_ref, h_ref, r_ref = rest
        j = pl.program_id(1)

        @pl.when(j == 0)
        def _():
            xv = x_ref[...]
            r = lax.rsqrt(jnp.mean(xv * xv, axis=-1, keepdims=True) + EPS)
            h = (xv * r * g_ref[...]).astype(BF16)
            h_ref[...] = h
            r_ref[...] = r
            if has2:
                o2_ref[...] = _dot(h, w2_ref[...])

        o_ref[...] = _dot(h_ref[...], wget(w_ref)).astype(o_ref.dtype)

    in_specs = [pl.BlockSpec((tm, D), lambda i, j: (i, 0)), pl.BlockSpec((1, D), lambda i, j: (0, 0)),
                pl.BlockSpec((D, tn), lambda i, j: (0, j)) if wspec is None else wspec]
    out_shape = [jax.ShapeDtypeStruct((T, N), BF16), jax.ShapeDtypeStruct((T, D), BF16),
                 jax.ShapeDtypeStruct((T, 1), F32)]
    out_specs = [pl.BlockSpec((tm, tn), lambda i, j: (i, j)), pl.BlockSpec((tm, D), lambda i, j: (i, 0)),
                 pl.BlockSpec((tm, 1), lambda i, j: (i, 0))]
    args = [x, g, w]
    if has2:
        in_specs.append(pl.BlockSpec((D, D_DT_PAD), lambda i, j: (0, 0)))
        out_shape.append(jax.ShapeDtypeStruct((T, D_DT_PAD), F32))
        out_specs.append(pl.BlockSpec((tm, D_DT_PAD), lambda i, j: (i, 0)))
        args.append(w2)
    return pl.pallas_call(body, grid=(T // tm, N // tn), in_specs=in_specs, out_specs=out_specs,
                          out_shape=out_shape, compiler_params=_cp("parallel", "arbitrary"), name=name)(*args)


def matmul_res(a, w, r, *, act=None, tm, tn, tk, name, wspec=None):
    T, K = a.shape
    N = r.shape[1]
    wget = (lambda r_: r_[...]) if wspec is None else (lambda r_: r_[0])
    nk = K // tk

    def body(a_ref, w_ref, r_ref, o_ref):
        k = pl.program_id(2)

        @pl.when(k == 0)
        def _():
            o_ref[...] = r_ref[...]

        o_ref[...] += _dot(_act(a_ref[...], act), wget(w_ref))

    return pl.pallas_call(
        body, grid=(T // tm, N // tn, nk),
        in_specs=[pl.BlockSpec((tm, tk), lambda i, j, k: (i, k)),
                  pl.BlockSpec((tk, tn), lambda i, j, k: (k, j)) if wspec is None else wspec,
                  pl.BlockSpec((tm, tn), lambda i, j, k: (i, j))],
        out_specs=pl.BlockSpec((tm, tn), lambda i, j, k: (i, j)),
        out_shape=jax.ShapeDtypeStruct((T, N), F32),
        compiler_params=_cp("parallel", "parallel", "arbitrary"), name=name)(a, w, r)


def glu_fwd(ys, w, *, tm, name, comm=None):
    T, N = ys.shape

    def body(y_ref, w_ref, o_ref, t_ref):
        a = _gelu(y_ref[...].astype(F32))
        t = _dot(a.astype(BF16), w_ref[...])
        o_ref[...] = (a * _sigmoid(t)).astype(BF16)
        t_ref[...] = t.astype(BF16)

    return _call(
        body, grid=(T // tm,),
        in_specs=[pl.BlockSpec((tm, N), lambda i: (i, 0)), pl.BlockSpec((N, N), lambda i: (0, 0))],
        out_specs=[pl.BlockSpec((tm, N), lambda i: (i, 0)), pl.BlockSpec((tm, N), lambda i: (i, 0))],
        out_shape=[jax.ShapeDtypeStruct((T, 2 * N), BF16), jax.ShapeDtypeStruct((T, N), BF16)],
        sem=("parallel",), name=name, args=(ys, w), comm=comm)


def glu_bwd_pre(dout, ys, t, *, tm, name):
    T, N = ys.shape

    def body(d_ref, y_ref, t_ref, dt_ref, da_ref):
        d = d_ref[...].astype(F32)
        a = _gelu(y_ref[...].astype(F32))
        s = _sigmoid(t_ref[...].astype(F32))
        dt_ref[...] = (d * a * s * (1.0 - s)).astype(BF16)
        da_ref[...] = (d * s).astype(BF16)

    spec = pl.BlockSpec((tm, N), lambda i: (i, 0))
    return pl.pallas_call(
        body, grid=(T // tm,), in_specs=[spec, spec, spec], out_specs=[spec, spec],
        out_shape=[jax.ShapeDtypeStruct((T, N), BF16), jax.ShapeDtypeStruct((T, N), BF16)],
        compiler_params=_cp("parallel"), name=name)(dout, ys, t)


def matmul_nt(g, w, *, epi=None, epi_args=(), g2=None, w2=None, tm, tko, tn, out_dtype=BF16, name, wspec=None,
              k_out=None, comm=None):
    T, N = g.shape
    K = w.shape[0] if wspec is None else k_out
    wget = (lambda r_: r_[...]) if wspec is None else (lambda r_: r_[0])
    nn = N // tn
    has2 = g2 is not None
    rms = epi == "rmsbwd"
    if rms:
        assert tko == K
    n_epi = len(epi_args)

    def body(*refs):
        g_ref, w_ref = refs[0], refs[1]
        pos = 2
        if has2:
            g2_ref, w2_ref = refs[2], refs[3]
            pos = 4
        e_refs = refs[pos:pos + n_epi]
        pos += n_epi
        o_ref = refs[pos]
        pos += 1
        if rms:
            dg_ref = refs[pos]
            pos += 1
        acc_ref = refs[pos]
        i = pl.program_id(0)
        n = pl.program_id(2)
        part = _dot_nt(g_ref[...].astype(BF16), wget(w_ref))

        @pl.when(n == 0)
        def _():
            acc_ref[...] = part

        @pl.when(n > 0)
        def _():
            acc_ref[...] += part

        @pl.when(n == nn - 1)
        def _():
            acc = acc_ref[...]
            if has2:
                acc = acc + _dot_nt(g2_ref[...].astype(BF16), w2_ref[...])
            if epi is None:
                o_ref[...] = acc.astype(o_ref.dtype)
            elif epi == "relu2bwd":
                h1 = e_refs[0][...].astype(F32)
                o_ref[...] = (acc * 2.0 * jnp.maximum(h1, 0.0)).astype(o_ref.dtype)
            elif epi == "glubwd":
                da1 = e_refs[0][...].astype(F32)
                ys = e_refs[1][...].astype(F32)
                o_ref[...] = ((da1 + acc) * _gelu_grad(ys)).astype(o_ref.dtype)
            elif epi == "rmsbwd":
                xv, rs, gain, rv = e_refs[0][...], e_refs[1][...], e_refs[2][...], e_refs[3][...]
                xhat = xv * rs
                gd = acc * gain
                o_ref[...] = rv + rs * (gd - xhat * jnp.mean(gd * xhat, axis=-1, keepdims=True))
                part_g = jnp.sum(acc * xhat, axis=0, keepdims=True)

                @pl.when(i == 0)
                def _():
                    dg_ref[...] = part_g

                @pl.when(i > 0)
                def _():
                    dg_ref[...] += part_g

    in_specs = [pl.BlockSpec((tm, tn), lambda i, k, n: (i, n)),
                pl.BlockSpec((tko, tn), lambda i, k, n: (k, n)) if wspec is None else wspec]
    args = [g, w]
    if has2:
        n2 = g2.shape[1]
        in_specs += [pl.BlockSpec((tm, n2), lambda i, k, n: (i, 0)), pl.BlockSpec((tko, n2), lambda i, k, n: (k, 0))]
        args += [g2, w2]
    if epi == "relu2bwd" or epi == "glubwd":
        in_specs += [pl.BlockSpec((tm, tko), lambda i, k, n: (i, k))] * n_epi
    elif rms:
        in_specs += [pl.BlockSpec((tm, K), lambda i, k, n: (i, 0)), pl.BlockSpec((tm, 1), lambda i, k, n: (i, 0)),
                     pl.BlockSpec((1, K), lambda i, k, n: (0, 0)), pl.BlockSpec((tm, K), lambda i, k, n: (i, 0))]
    args += list(epi_args)
    out_shape = [jax.ShapeDtypeStruct((T, K), F32 if rms else out_dtype)]
    out_specs = [pl.BlockSpec((tm, tko), lambda i, k, n: (i, k))]
    if rms:
        out_shape.append(jax.ShapeDtypeStruct((1, K), F32))
        out_specs.append(pl.BlockSpec((1, K), lambda i, k, n: (0, 0)))
    sem = ("arbitrary",) * 3 if rms else ("parallel", "parallel", "arbitrary")
    res, cres = _call(body, grid=(T // tm, K // tko, nn), in_specs=in_specs, out_specs=out_specs, out_shape=out_shape,
                      scratch_shapes=[pltpu.VMEM((tm, tko), F32)], sem=sem, name=name, args=args, comm=comm)
    res = res if rms else res[0]
    return res if comm is None else (res, cres)


def matmul_tn(a, g, *, act=None, tk, tn, tt, name, pack=None, pack_spec=None, pack_shape=None):
    T, K = a.shape
    N = g.shape[1]
    to_pack = pack_spec is not None

    def body(a_ref, g_ref, *rest):
        o_ref = rest[-1]
        t = pl.program_id(2)
        part = _dot_tn(_act(a_ref[...], act), g_ref[...].astype(BF16))
        if to_pack:
            lead = pack_spec.block_shape[0]
            part = part.reshape(lead, tk // lead, tn)

        @pl.when(t == 0)
        def _():
            o_ref[...] = part

        @pl.when(t > 0)
        def _():
            o_ref[...] += part

    in_specs = [pl.BlockSpec((tt, tk), lambda k, n, t: (t, k)), pl.BlockSpec((tt, tn), lambda k, n, t: (t, n))]
    args = [a, g]
    aliases = {}
    if pack is not None:
        in_specs.append(_ANY)
        args.append(pack)
        aliases = {2: 0}
    return pl.pallas_call(
        body, grid=(K // tk, N // tn, T // tt), in_specs=in_specs,
        out_specs=pack_spec if to_pack else pl.BlockSpec((tk, tn), lambda k, n, t: (k, n)),
        out_shape=jax.ShapeDtypeStruct(pack_shape if to_pack else (K, N), F32), input_output_aliases=aliases,
        compiler_params=_cp("parallel", "parallel", "arbitrary"), name=name)(*args)


def attn_fwd(q, kv, *, tm, name):
    T = q.shape[0]
    M = kv.shape[0]
    scale = XA_HEAD_DIM ** -0.5

    def body(q_ref, kv_ref, o_ref):
        for h in range(XA_HEADS):
            sl = slice(h * XA_HEAD_DIM, (h + 1) * XA_HEAD_DIM)
            kh = kv_ref[:, h * XA_HEAD_DIM:(h + 1) * XA_HEAD_DIM]
            vh = kv_ref[:, D_MODEL + h * XA_HEAD_DIM:D_MODEL + (h + 1) * XA_HEAD_DIM]
            s = _dot_nt(q_ref[:, sl], kh) * scale
            s = s - jnp.max(s, axis=-1, keepdims=True)
            p = jnp.exp(s)
            p = p / jnp.sum(p, axis=-1, keepdims=True)
            o_ref[:, sl] = _dot(p.astype(BF16), vh).astype(BF16)

    return pl.pallas_call(
        body, grid=(T // tm,),
        in_specs=[pl.BlockSpec((tm, D_MODEL), lambda i: (i, 0)), pl.BlockSpec((M, 2 * D_MODEL), lambda i: (0, 0))],
        out_specs=pl.BlockSpec((tm, D_MODEL), lambda i: (i, 0)),
        out_shape=jax.ShapeDtypeStruct((T, D_MODEL), BF16),
        compiler_params=_cp("parallel"), name=name)(q, kv)


def attn_bwd(q, kv, do, *, tm, name):
    T = q.shape[0]
    M = kv.shape[0]
    scale = XA_HEAD_DIM ** -0.5

    def body(q_ref, kv_ref, do_ref, dq_ref, dkv_ref):
        i = pl.program_id(0)

        @pl.when(i == 0)
        def _():
            dkv_ref[...] = jnp.zeros_like(dkv_ref)

        for h in range(XA_HEADS):
            sl = slice(h * XA_HEAD_DIM, (h + 1) * XA_HEAD_DIM)
            slv = slice(D_MODEL + h * XA_HEAD_DIM, D_MODEL + (h + 1) * XA_HEAD_DIM)
            qh = q_ref[:, sl]
            kh = kv_ref[:, sl]
            vh = kv_ref[:, slv]
            doh = do_ref[:, sl]
            s = _dot_nt(qh, kh) * scale
            s = s - jnp.max(s, axis=-1, keepdims=True)
            p = jnp.exp(s)
            p = p / jnp.sum(p, axis=-1, keepdims=True)
            pb = p.astype(BF16)
            dkv_ref[:, slv] += _dot_tn(pb, doh)
            dp = _dot_nt(doh, vh)
            ds = (p * (dp - jnp.sum(dp * p, axis=-1, keepdims=True)) * scale).astype(BF16)
            dq_ref[:, sl] = _dot(ds, kh).astype(BF16)
            dkv_ref[:, sl] += _dot_tn(ds, qh)

    spec = pl.BlockSpec((tm, D_MODEL), lambda i: (i, 0))
    kvspec = pl.BlockSpec((M, 2 * D_MODEL), lambda i: (0, 0))
    return pl.pallas_call(
        body, grid=(T // tm,), in_specs=[spec, kvspec, spec], out_specs=[spec, kvspec],
        out_shape=[jax.ShapeDtypeStruct((T, D_MODEL), BF16), jax.ShapeDtypeStruct((M, 2 * D_MODEL), F32)],
        compiler_params=_cp("arbitrary"), name=name)(q, kv, do)


def loss_head(x, g, target, *, tm, name):
    T, D = x.shape

    def body(x_ref, g_ref, t_ref, l_ref, dx_ref, dg_ref):
        i = pl.program_id(0)
        xv = x_ref[...]
        gain = g_ref[...]
        r = lax.rsqrt(jnp.mean(xv * xv, axis=-1, keepdims=True) + EPS)
        xhat = xv * r
        err = xhat * gain - t_ref[...]
        part_l = jnp.full((1, 128), 0.5 / D, F32) * jnp.sum(err * err)
        dy = err * (1.0 / D)
        gd = dy * gain
        dx_ref[...] = r * (gd - xhat * jnp.mean(gd * xhat, axis=-1, keepdims=True))
        part_g = jnp.sum(dy * xhat, axis=0, keepdims=True)

        @pl.when(i == 0)
        def _():
            l_ref[...] = part_l
            dg_ref[...] = part_g

        @pl.when(i > 0)
        def _():
            l_ref[...] += part_l
            dg_ref[...] += part_g

    spec = pl.BlockSpec((tm, D), lambda i: (i, 0))
    return pl.pallas_call(
        body, grid=(T // tm,), in_specs=[spec, pl.BlockSpec((1, D), lambda i: (0, 0)), spec],
        out_specs=[pl.BlockSpec((1, 128), lambda i: (0, 0)), spec, pl.BlockSpec((1, D), lambda i: (0, 0))],
        out_shape=[jax.ShapeDtypeStruct((1, 128), F32), jax.ShapeDtypeStruct((T, D), F32),
                   jax.ShapeDtypeStruct((1, D), F32)],
        compiler_params=_cp("arbitrary"), name=name)(x, g, target)


S5_LS = 128
S5_SEG = 8
S5_TB = S5_LS * S5_SEG
S5_CH = 8
S5_CW = 128
S5_NST = 512


def _cmul(ar, ai, br, bi):
    return ar * br - ai * bi, ar * bi + ai * br


def s5_params_fwd(a_re, a_im, log_dt, b_re, b_im, expand, *, name):
    G, P = a_re.shape

    def body(ar_ref, ai_ref, ld_ref, br_ref, bi_ref, e_ref, abr_ref, abi_ref, apr_ref, api_ref, bbr_ref, bbi_ref):
        ar, ai = ar_ref[...], ai_ref[...]
        dt = jnp.exp(ld_ref[...])
        mag = jnp.exp(dt * ar)
        abr, abi = mag * jnp.cos(dt * ai), mag * jnp.sin(dt * ai)
        den = ar * ar + ai * ai
        zr, zi = abr - 1.0, abi
        fr = (zr * ar + zi * ai) / den
        fi = (zi * ar - zr * ai) / den
        frx, fix = _dot_hi(fr, e_ref[...]), _dot_hi(fi, e_ref[...])
        br, bi = br_ref[...], bi_ref[...]
        bbr_ref[...] = frx * br - fix * bi
        bbi_ref[...] = frx * bi + fix * br
        abr_ref[...] = abr
        abi_ref[...] = abi
        pr, pi = abr, abi
        for _ in range(int(math.log2(S5_LS))):
            pr, pi = _cmul(pr, pi, pr, pi)
        apr_ref[...] = pr
        api_ref[...] = pi

    small = jax.ShapeDtypeStruct((G, P), F32)
    big = jax.ShapeDtypeStruct(b_re.shape, F32)
    return pl.pallas_call(body, out_shape=[small, small, small, small, big, big], name=name)(
        a_re, a_im, log_dt, b_re, b_im, expand)


def s5_params_bwd(a_re, a_im, log_dt, b_re, b_im, g_abr, g_abi, g_bbr, g_bbi, expand, expand_t, *, name):
    G, P = a_re.shape

    def body(ar_ref, ai_ref, ld_ref, br_ref, bi_ref, gar_ref, gai_ref, gbr_ref, gbi_ref, e_ref, et_ref,
             dar_ref, dai_ref, dld_ref, dbr_ref, dbi_ref):
        ar, ai = ar_ref[...], ai_ref[...]
        dt = jnp.exp(ld_ref[...])
        mag = jnp.exp(dt * ar)
        cs, sn = jnp.cos(dt * ai), jnp.sin(dt * ai)
        abr, abi = mag * cs, mag * sn
        den = ar * ar + ai * ai
        zr, zi = abr - 1.0, abi
        fr = (zr * ar + zi * ai) / den
        fi = (zi * ar - zr * ai) / den
        frx, fix = _dot_hi(fr, e_ref[...]), _dot_hi(fi, e_ref[...])
        br, bi = br_ref[...], bi_ref[...]
        gbr, gbi = gbr_ref[...], gbi_ref[...]
        dbr_ref[...] = frx * gbr + fix * gbi
        dbi_ref[...] = -fix * gbr + frx * gbi
        gfr = _dot_hi(br * gbr + bi * gbi, et_ref[...])
        gfi = _dot_hi(-bi * gbr + br * gbi, et_ref[...])
        g_zr = (gfr * ar - gfi * ai) / den
        g_zi = (gfr * ai + gfi * ar) / den
        g_ar = gfr * (zr - fr * 2.0 * ar) / den + gfi * (zi - fi * 2.0 * ar) / den
        g_ai = gfr * (zi - fr * 2.0 * ai) / den + gfi * (-zr - fi * 2.0 * ai) / den
        t_abr = gar_ref[...] + g_zr
        t_abi = gai_ref[...] + g_zi
        g_mag = t_abr * cs + t_abi * sn
        g_th = mag * (-t_abr * sn + t_abi * cs)
        dar_ref[...] = g_ar + g_mag * mag * dt
        dai_ref[...] = g_ai + g_th * dt
        g_dt = jnp.sum(g_mag * mag * ar + g_th * ai, axis=-1, keepdims=True)
        dld_ref[...] = g_dt * dt

    small = jax.ShapeDtypeStruct((G, P), F32)
    big = jax.ShapeDtypeStruct(b_re.shape, F32)
    return pl.pallas_call(body, out_shape=[small, small, jax.ShapeDtypeStruct((G, 1), F32), big, big], name=name)(
        a_re, a_im, log_dt, b_re, b_im, g_abr, g_abi, g_bbr, g_bbi, expand, expand_t)


def _s5_permute_in(src_ref, dst_ref):
    for i in range(S5_LS):
        dst_ref[pl.ds(8 * i, 8), :] = src_ref[pl.ds(i, 8, stride=S5_LS), :]


def _s5_permute_out(src_ref, dst_ref):
    for r in range(S5_SEG):
        for k in range(S5_LS // 8):
            dst_ref[pl.ds(r * S5_LS + 8 * k, 8), :] = src_ref[pl.ds(64 * k + r, 8, stride=8), :]


def _s5_scan(a_r, a_i, dr_ref, di_ref, init_r, init_i, store=None, reverse=False, conj=False):
    sgn = -1.0 if conj else 1.0

    def step(n, c):
        sr, si = c
        i = S5_LS - 1 - n if reverse else n
        nr = a_r * sr - sgn * a_i * si + dr_ref[i]
        ni = a_r * si + sgn * a_i * sr + di_ref[i]
        if store is not None:
            store(i, nr, ni, sr, si)
        return nr, ni

    return lax.fori_loop(0, S5_LS, step, (init_r, init_i), unroll=2)


def _s5_stitch(apr, api, fin_r, fin_i, car_r, car_i, reverse=False, conj=False):
    sgn = -1.0 if conj else 1.0
    rows_r, rows_i = [None] * S5_SEG, [None] * S5_SEG
    order = range(S5_SEG - 1, -1, -1) if reverse else range(S5_SEG)
    for r in order:
        rows_r[r], rows_i[r] = car_r, car_i
        fr, fi = fin_r[r:r + 1], fin_i[r:r + 1]
        car_r, car_i = (apr * car_r - sgn * api * car_i + fr, apr * car_i + sgn * api * car_r + fi)
    return jnp.concatenate(rows_r, 0), jnp.concatenate(rows_i, 0), car_r, car_i


def _s5_specs(nb, rev):
    blk = (lambda c, b: (nb - 1 - b, c)) if rev else (lambda c, b: (b, c))
    tok = pl.BlockSpec((S5_TB, S5_CW), blk)
    par_b = pl.BlockSpec((1, S5_CW, S5_NST), lambda c, b: (c, 0, 0))
    par_c = pl.BlockSpec((1, S5_NST, S5_CW), lambda c, b: (c, 0, 0))
    vec_s = pl.BlockSpec((1, 1, S5_NST), lambda c, b: (c, 0, 0))
    vec_c = pl.BlockSpec((1, 1, S5_CW), lambda c, b: (c, 0, 0))
    return tok, par_b, par_c, vec_s, vec_c


def s5_fwd(proj, bbr, bbi, ccr, cci, abr, abi, apr, api, dskip, *, name, comm=None):
    T = proj.shape[0]
    nb = T // S5_TB
    zeros8 = functools.partial(jnp.zeros, (S5_SEG, S5_NST), F32)

    def body(u_ref, bbr_ref, bbi_ref, ccr_ref, cci_ref, ar_ref, ai_ref, apr_ref, api_ref, d_ref,
             y_ref, csr_ref, csi_ref, uf_ref, up_ref, dr_ref, di_ref, sr_ref, si_ref, yp_ref, car_ref, cai_ref):
        b = pl.program_id(1)

        @pl.when(b == 0)
        def _():
            car_ref[...] = jnp.zeros_like(car_ref)
            cai_ref[...] = jnp.zeros_like(cai_ref)

        csr_ref[0, 0] = car_ref[...]
        csi_ref[0, 0] = cai_ref[...]
        uf_ref[...] = u_ref[...].astype(F32)
        _s5_permute_in(uf_ref, up_ref)
        upb = up_ref[...].astype(BF16)
        dr_ref[...] = _dot(upb, bbr_ref[0]).reshape(S5_LS, S5_SEG, S5_NST)
        di_ref[...] = _dot(upb, bbi_ref[0]).reshape(S5_LS, S5_SEG, S5_NST)
        a_r = jnp.broadcast_to(ar_ref[0], (S5_SEG, S5_NST))
        a_i = jnp.broadcast_to(ai_ref[0], (S5_SEG, S5_NST))
        fin_r, fin_i = _s5_scan(a_r, a_i, dr_ref, di_ref, zeros8(), zeros8())
        cin_r, cin_i, ncr, nci = _s5_stitch(apr_ref[0], api_ref[0], fin_r, fin_i, car_ref[...], cai_ref[...])
        car_ref[...] = ncr
        cai_ref[...] = nci

        def store(i, nr, ni, sr, si):
            sr_ref[i] = nr
            si_ref[i] = ni

        _s5_scan(a_r, a_i, dr_ref, di_ref, cin_r, cin_i, store=store)
        s_r = sr_ref[...].reshape(S5_TB, S5_NST).astype(BF16)
        s_i = si_ref[...].reshape(S5_TB, S5_NST).astype(BF16)
        yp_ref[...] = _dot(s_r, ccr_ref[0]) - _dot(s_i, cci_ref[0]) + d_ref[0] * up_ref[...]
        _s5_permute_out(yp_ref, uf_ref)
        y_ref[...] = uf_ref[...].astype(BF16)

    tok, par_b, par_c, vec_s, vec_c = _s5_specs(nb, False)
    cs_spec = pl.BlockSpec((1, 1, 1, S5_NST), lambda c, b: (b, c, 0, 0))
    cs_shape = jax.ShapeDtypeStruct((nb, S5_CH, 1, S5_NST), F32)
    tokbuf = pltpu.VMEM((S5_TB, S5_CW), F32)
    stbuf = pltpu.VMEM((S5_LS, S5_SEG, S5_NST), F32)
    return _call(
        body, grid=(S5_CH, nb),
        in_specs=[tok, par_b, par_b, par_c, par_c, vec_s, vec_s, vec_s, vec_s, vec_c],
        out_specs=[tok, cs_spec, cs_spec],
        out_shape=[jax.ShapeDtypeStruct((T, D_S5), BF16), cs_shape, cs_shape],
        scratch_shapes=[tokbuf, tokbuf, stbuf, stbuf, stbuf, stbuf, tokbuf,
                        pltpu.VMEM((1, S5_NST), F32), pltpu.VMEM((1, S5_NST), F32)],
        sem=("parallel", "arbitrary"), name=name, comm=comm,
        args=(proj, bbr, bbi, ccr, cci, abr, abi, apr, api, dskip))


def s5_bwd(proj, dys, csr, csi, bbr, bbi, ccr, cci, abr, abi, apr, api, dskip, dproj, *, name):
    T = proj.shape[0]
    nb = T // S5_TB
    zeros8 = functools.partial(jnp.zeros, (S5_SEG, S5_NST), F32)

    def body(u_ref, gy_ref, csr_ref, csi_ref, bbr_ref, bbi_ref, ccr_ref, cci_ref, ar_ref, ai_ref, apr_ref, api_ref,
             d_ref, dproj_ref, du_ref, dbr_ref, dbi_ref, dcr_ref, dci_ref, dd_ref, dar_ref, dai_ref,
             tmp_ref, up_ref, gyp_ref, dr_ref, di_ref, sr_ref, si_ref, gr_ref, gi_ref, car_ref, cai_ref):
        b = pl.program_id(1)

        @pl.when(b == 0)
        def _():
            car_ref[...] = jnp.zeros_like(car_ref)
            cai_ref[...] = jnp.zeros_like(cai_ref)
            for ref in (dbr_ref, dbi_ref, dcr_ref, dci_ref, dd_ref, dar_ref, dai_ref):
                ref[...] = jnp.zeros_like(ref)

        tmp_ref[...] = u_ref[...].astype(F32)
        _s5_permute_in(tmp_ref, up_ref)
        tmp_ref[...] = gy_ref[...].astype(F32)
        _s5_permute_in(tmp_ref, gyp_ref)
        upb = up_ref[...].astype(BF16)
        gyp = gyp_ref[...]
        gypb = gyp.astype(BF16)
        dr_ref[...] = _dot(upb, bbr_ref[0]).reshape(S5_LS, S5_SEG, S5_NST)
        di_ref[...] = _dot(upb, bbi_ref[0]).reshape(S5_LS, S5_SEG, S5_NST)
        a_r = jnp.broadcast_to(ar_ref[0], (S5_SEG, S5_NST))
        a_i = jnp.broadcast_to(ai_ref[0], (S5_SEG, S5_NST))
        fin_r, fin_i = _s5_scan(a_r, a_i, dr_ref, di_ref, zeros8(), zeros8())
        cin_r, cin_i, _, _ = _s5_stitch(apr_ref[0], api_ref[0], fin_r, fin_i, csr_ref[0, 0], csi_ref[0, 0])
        sr_ref[0] = cin_r
        si_ref[0] = cin_i

        def store_s(i, nr, ni, sr, si):
            sr_ref[i + 1] = nr
            si_ref[i + 1] = ni

        _s5_scan(a_r, a_i, dr_ref, di_ref, cin_r, cin_i, store=store_s)
        s_r = sr_ref[pl.ds(1, S5_LS)].reshape(S5_TB, S5_NST).astype(BF16)
        s_i = si_ref[pl.ds(1, S5_LS)].reshape(S5_TB, S5_NST).astype(BF16)
        dcr_ref[0] += _dot_tn(s_r, gypb)
        dci_ref[0] -= _dot_tn(s_i, gypb)
        dd_ref[0] += jnp.sum(gyp * up_ref[...], axis=0, keepdims=True)
        dr_ref[...] = _dot_nt(gypb, ccr_ref[0]).reshape(S5_LS, S5_SEG, S5_NST)
        di_ref[...] = (-_dot_nt(gypb, cci_ref[0])).reshape(S5_LS, S5_SEG, S5_NST)
        fin_r, fin_i = _s5_scan(a_r, a_i, dr_ref, di_ref, zeros8(), zeros8(), reverse=True, conj=True)
        gin_r, gin_i, ncr, nci = _s5_stitch(apr_ref[0], api_ref[0], fin_r, fin_i, car_ref[...], cai_ref[...],
                                            reverse=True, conj=True)
        car_ref[...] = ncr
        cai_ref[...] = nci
        def step_g(n, carry):
            gr, gi, acc_r, acc_i = carry
            i = S5_LS - 1 - n
            nr = a_r * gr + a_i * gi + dr_ref[i]
            ni = a_r * gi - a_i * gr + di_ref[i]
            gr_ref[i] = nr
            gi_ref[i] = ni
            pr, pi = sr_ref[i], si_ref[i]
            return nr, ni, acc_r + (nr * pr + ni * pi), acc_i + (ni * pr - nr * pi)

        _, _, acc_r, acc_i = lax.fori_loop(0, S5_LS, step_g, (gin_r, gin_i, zeros8(), zeros8()), unroll=2)
        dar_ref[0] += jnp.sum(acc_r, axis=0, keepdims=True)
        dai_ref[0] += jnp.sum(acc_i, axis=0, keepdims=True)
        g_r = gr_ref[...].reshape(S5_TB, S5_NST).astype(BF16)
        g_i = gi_ref[...].reshape(S5_TB, S5_NST).astype(BF16)
        dbr_ref[0] += _dot_tn(upb, g_r)
        dbi_ref[0] += _dot_tn(upb, g_i)
        gyp_ref[...] = _dot_nt(g_r, bbr_ref[0]) + _dot_nt(g_i, bbi_ref[0]) + d_ref[0] * gyp
        _s5_permute_out(gyp_ref, tmp_ref)
        du_ref[...] = tmp_ref[...].astype(BF16)

    tok, par_b, par_c, vec_s, vec_c = _s5_specs(nb, True)
    cs_spec = pl.BlockSpec((1, 1, 1, S5_NST), lambda c, b: (nb - 1 - b, c, 0, 0))
    tokbuf = pltpu.VMEM((S5_TB, S5_CW), F32)
    stbuf = pltpu.VMEM((S5_LS, S5_SEG, S5_NST), F32)
    stbuf1 = pltpu.VMEM((S5_LS + 1, S5_SEG, S5_NST), F32)
    return pl.pallas_call(
        body, grid=(S5_CH, nb),
        in_specs=[tok, tok, cs_spec, cs_spec, par_b, par_b, par_c, par_c, vec_s, vec_s, vec_s, vec_s, vec_c, _ANY],
        out_specs=[tok, par_b, par_b, par_c, par_c, vec_c, vec_s, vec_s], input_output_aliases={13: 0},
        out_shape=[jax.ShapeDtypeStruct(dproj.shape, BF16),
                   jax.ShapeDtypeStruct((S5_CH, S5_CW, S5_NST), F32), jax.ShapeDtypeStruct((S5_CH, S5_CW, S5_NST), F32),
                   jax.ShapeDtypeStruct((S5_CH, S5_NST, S5_CW), F32), jax.ShapeDtypeStruct((S5_CH, S5_NST, S5_CW), F32),
                   jax.ShapeDtypeStruct((S5_CH, 1, S5_CW), F32),
                   jax.ShapeDtypeStruct((S5_CH, 1, S5_NST), F32), jax.ShapeDtypeStruct((S5_CH, 1, S5_NST), F32)],
        scratch_shapes=[tokbuf, tokbuf, tokbuf, stbuf, stbuf, stbuf1, stbuf1, stbuf, stbuf,
                        pltpu.VMEM((1, S5_NST), F32), pltpu.VMEM((1, S5_NST), F32)],
        compiler_params=_cp("parallel", "arbitrary"), name=name)(
            proj, dys, csr, csi, bbr, bbi, ccr, cci, abr, abi, apr, api, dskip, dproj)


SSD_L = SSD_CHUNK
SSD_GW = 256
NEG = -1e30


def _expand16(v):
    lane = lax.broadcasted_iota(jnp.int32, (v.shape[0], 128), 1)
    parts = [jnp.where(lane < SSD_HEADDIM, v[:, 2 * j:2 * j + 1], v[:, 2 * j + 1:2 * j + 2]) for j in range(8)]
    return jnp.concatenate(parts, axis=1)


def _headsum(v, hsum):
    hi = v.astype(BF16)
    lo = (v - hi.astype(F32)).astype(BF16)
    return _dot(hi, hsum) + _dot(lo, hsum)


def _softplus(x):
    return jnp.maximum(x, 0.0) + jnp.log(1.0 + jnp.exp(-jnp.abs(x)))


def _ssd_chunk_fwd(z, xbc, tail, dtraw, hprev, cw, cb, dtb, alog, dsk, nw, tril):
    L = SSD_L
    f = {}
    xe = jnp.concatenate([tail, xbc], axis=0)
    sh = [xbc] + [pltpu.roll(xe, s, 0)[8:] for s in (1, 2, 3)]
    conv = cb + cw[3:4] * sh[0] + cw[2:3] * sh[1] + cw[1:2] * sh[2] + cw[0:1] * sh[3]
    sig = _sigmoid(conv)
    xa = conv * sig
    xs, bm, cm = xa[:, :D_SSD], xa[:, D_SSD:D_SSD + 512], xa[:, D_SSD + 512:]
    pre = dtraw + dtb
    dt = _softplus(pre)
    a_h = -jnp.exp(alog)
    acum = _dot_hi(tril, dt * a_h)
    acum_t = acum.T
    alast = acum[L - 1:L]
    exp_a = jnp.exp(acum)
    dec = jnp.exp(alast - acum)
    exp_al = jnp.exp(alast)
    dt_x, dec_x, exp_a_x, exp_al_x = _expand16(dt), _expand16(dec), _expand16(exp_a), _expand16(exp_al)
    d_x = _expand16(dsk)
    xh = xs * dt_x
    xhb = xh.astype(BF16)
    xd = (xh * dec_x).astype(BF16)
    row = lax.broadcasted_iota(jnp.int32, (L, L), 0)
    col = lax.broadcasted_iota(jnp.int32, (L, L), 1)
    causal = row >= col
    lane = lax.broadcasted_iota(jnp.int32, (L, 128), 1)
    low = lane < SSD_HEADDIM
    hb = hprev.astype(BF16)
    y_pairs, yoff_parts, st_parts, cbs, lms = [], [], [], [], []
    for g in range(SSD_NGROUPS):
        bg = bm[:, g * 128:(g + 1) * 128].astype(BF16)
        cg = cm[:, g * 128:(g + 1) * 128].astype(BF16)
        cbg = _dot_nt(cg, bg)
        cbs.append(cbg)
        for j in (2 * g, 2 * g + 1):
            xp = xhb[:, j * 128:(j + 1) * 128]
            ys = []
            for h in (2 * j, 2 * j + 1):
                lm = jnp.exp(jnp.where(causal, acum[:, h:h + 1] - acum_t[h:h + 1, :], NEG))
                lms.append(lm)
                ys.append(_dot((cbg * lm).astype(BF16), xp))
            y_pairs.append(jnp.where(low, ys[0], ys[1]))
        gs = slice(g * SSD_GW, (g + 1) * SSD_GW)
        yoff_parts.append(_dot(cg, hb[:, gs]) * exp_a_x[:, gs])
        st_parts.append(_dot_tn(bg, xd[:, gs]))
    yoff = jnp.concatenate(yoff_parts, axis=1)
    y = jnp.concatenate(y_pairs, axis=1) + yoff + d_x * xs
    hnew = exp_al_x * hprev + jnp.concatenate(st_parts, axis=1)
    sz = _sigmoid(z)
    gz = y * (z * sz)
    r = lax.rsqrt(jnp.mean(gz * gz, axis=-1, keepdims=True) + EPS)
    out = gz * r * nw
    f.update(sh=sh, conv=conv, sig=sig, xs=xs, bm=bm, cm=cm, pre=pre, dt=dt, a_h=a_h, exp_a_x=exp_a_x, dec_x=dec_x,
             exp_al=exp_al, exp_al_x=exp_al_x, dt_x=dt_x, d_x=d_x, xh=xh, xhb=xhb, xd=xd, causal=causal, low=low, hb=hb,
             cbs=cbs, lms=lms, yoff=yoff, y=y, sz=sz, gz=gz, r=r)
    return out, hnew, f


def _ssd_params(conv_w, conv_b, dt_bias, a_log, d_skip, norm_w):
    pad16 = lambda v: jnp.pad(v.reshape(1, SSD_HEADS), ((0, 0), (0, 128 - SSD_HEADS)))
    return (jnp.pad(conv_w, ((0, 8 - SSD_CONV), (0, 0))), conv_b.reshape(1, D_XBC), pad16(dt_bias), pad16(a_log),
            pad16(d_skip), norm_w.reshape(1, D_SSD))


def _ssd_param_specs():
    full = lambda shape: pl.BlockSpec(shape, lambda i: (0, 0))
    return [full((8, D_XBC)), full((1, D_XBC)), full((1, 128)), full((1, 128)), full((1, 128)), full((1, D_SSD))]


def ssd_fwd(proj, dtraw, params, tril, ymix, *, name, comm=None):
    T = proj.shape[0]
    nc = T // SSD_L

    def body(z_ref, x_ref, dt_ref, cw_ref, cb_ref, dtb_ref, al_ref, dsk_ref, nw_ref, tril_ref, ymix_ref,
             o_ref, hs_ref, h_ref, tail_ref):
        i = pl.program_id(0)

        @pl.when(i == 0)
        def _():
            h_ref[...] = jnp.zeros_like(h_ref)
            tail_ref[...] = jnp.zeros_like(tail_ref)

        xbc = x_ref[...].astype(F32)
        hprev = h_ref[...]
        hs_ref[0] = hprev
        out, hnew, _ = _ssd_chunk_fwd(z_ref[...].astype(F32), xbc, tail_ref[...], dt_ref[...], hprev, cw_ref[...],
                                      cb_ref[...], dtb_ref[...], al_ref[...], dsk_ref[...], nw_ref[...], tril_ref[...])
        o_ref[...] = out.astype(BF16)
        h_ref[...] = hnew
        tail_ref[...] = xbc[SSD_L - 8:]

    return _call(
        body, grid=(nc,),
        in_specs=[pl.BlockSpec((SSD_L, D_SSD), lambda i: (i, 1)), pl.BlockSpec((SSD_L, D_XBC), lambda i: (i, 1)),
                  pl.BlockSpec((SSD_L, 128), lambda i: (i, 0))] + _ssd_param_specs()
                 + [pl.BlockSpec((SSD_L, SSD_L), lambda i: (0, 0)), _ANY],
        out_specs=[pl.BlockSpec((SSD_L, D_SSD), lambda i: (i, 1)),
                   pl.BlockSpec((1, SSD_STATE, D_SSD), lambda i: (i, 0, 0))],
        out_shape=[jax.ShapeDtypeStruct(ymix.shape, BF16), jax.ShapeDtypeStruct((nc, SSD_STATE, D_SSD), F32)],
        scratch_shapes=[pltpu.VMEM((SSD_STATE, D_SSD), F32), pltpu.VMEM((8, D_XBC), F32)],
        sem=("arbitrary",), name=name, args=(proj, proj, dtraw, *params, tril, ymix), aliases={10: 0}, comm=comm)


def ssd_bwd(proj, dtraw, hs, dymix, params, tril, triu, trils, headsum, *, name, comm=None):
    T = proj.shape[0]
    nc = T // SSD_L
    L = SSD_L

    def body(z_ref, x_ref, xprev_ref, dt_ref, hs_ref, do_ref, cw_ref, cb_ref, dtb_ref, al_ref, dsk_ref, nw_ref,
             tril_ref, triu_ref, trils_ref, hsum_ref,
             dp_ref, ddt_ref, dcw_ref, dcb_ref, ddtb_ref, dal_ref, ddsk_ref, dnw_ref, dh_ref, dnext_ref):
        i = pl.program_id(0)

        @pl.when(i == 0)
        def _():
            dh_ref[...] = jnp.zeros_like(dh_ref)
            dnext_ref[...] = jnp.zeros_like(dnext_ref)
            for ref in (dcw_ref, dcb_ref, ddtb_ref, dal_ref, ddsk_ref, dnw_ref):
                ref[...] = jnp.zeros_like(ref)

        z = z_ref[...].astype(F32)
        xbc = x_ref[...].astype(F32)
        tail = jnp.where(i == nc - 1, 0.0, xprev_ref[...].astype(F32))
        hprev = hs_ref[0]
        cw, nw = cw_ref[...], nw_ref[...]
        hsum = hsum_ref[...]
        _, _, f = _ssd_chunk_fwd(z, xbc, tail, dt_ref[...], hprev, cw, cb_ref[...], dtb_ref[...], al_ref[...],
                                 dsk_ref[...], nw, tril_ref[...])
        dout = do_ref[...].astype(F32)
        dh = dh_ref[...]
        ghat = f["gz"] * f["r"]
        dn = dout * nw
        dgz = f["r"] * (dn - ghat * jnp.mean(dn * ghat, axis=-1, keepdims=True))
        dnw_ref[...] += jnp.sum(dout * ghat, axis=0, keepdims=True)
        sz = f["sz"]
        dy = dgz * (z * sz)
        dp_ref[:, :D_S5] = jnp.zeros((L, D_S5), BF16)
        dp_ref[:, D_S5:D_S5 + D_SSD] = (dgz * f["y"] * sz * (1.0 + z * (1.0 - sz))).astype(BF16)
        xs = f["xs"]
        ddsk_ref[...] += jnp.sum(_headsum(dy * xs, hsum), axis=0, keepdims=True)
        dyb = dy.astype(BF16)
        dye = (dy * f["exp_a_x"]).astype(BF16)
        dhb = dh.astype(BF16)
        lane = lax.broadcasted_iota(jnp.int32, (L, 128), 1)
        sub = lax.broadcasted_iota(jnp.int32, (128, L), 0)
        zero_b = jnp.zeros((L, 128), BF16)
        rsum = jnp.zeros((L, 128), F32)
        csum_t = jnp.zeros((128, L), F32)
        dx_pairs, dxst_parts, db_parts, dc_parts, dhp_parts = [], [], [], [], []
        for g in range(SSD_NGROUPS):
            gs = slice(g * SSD_GW, (g + 1) * SSD_GW)
            bg = f["bm"][:, g * 128:(g + 1) * 128].astype(BF16)
            cg = f["cm"][:, g * 128:(g + 1) * 128].astype(BF16)
            cbg = f["cbs"][g]
            dcb_g = jnp.zeros((L, L), F32)
            for j in (2 * g, 2 * g + 1):
                xp = f["xhb"][:, j * 128:(j + 1) * 128]
                dyp = dyb[:, j * 128:(j + 1) * 128]
                dxs = []
                for half, h in enumerate((2 * j, 2 * j + 1)):
                    lm = f["lms"][h]
                    dyh = jnp.where(f["low"], dyp, zero_b) if half == 0 else jnp.where(f["low"], zero_b, dyp)
                    dw = jnp.where(f["causal"], _dot_nt(dyh, xp), 0.0)
                    w = cbg * lm
                    e = dw * w
                    dcb_g = dcb_g + dw * lm
                    rsum = jnp.where(lane == h, jnp.sum(e, axis=1, keepdims=True), rsum)
                    csum_t = jnp.where(sub == h, jnp.sum(e, axis=0, keepdims=True), csum_t)
                    dxs.append(_dot_tn(w.astype(BF16), dyp))
                dx_pairs.append(jnp.where(f["low"], dxs[0], dxs[1]))
            dcbb = dcb_g.astype(BF16)
            dxst_parts.append(f["dec_x"][:, gs] * _dot(bg, dhb[:, gs]))
            dc_parts.append(_dot(dcbb, bg) + _dot_nt(dye[:, gs], f["hb"][:, gs]))
            db_parts.append(_dot_tn(dcbb, cg) + _dot_nt(f["xd"][:, gs], dhb[:, gs]))
            dhp_parts.append(f["exp_al_x"][:, gs] * dh[:, gs] + _dot_tn(cg, dye[:, gs]))
        dxst = jnp.concatenate(dxst_parts, axis=1)
        dxh = jnp.concatenate(dx_pairs, axis=1) + dxst
        q = _headsum(f["yoff"] * dy, hsum)
        dstate = _headsum(f["xh"] * dxst, hsum)
        h0t = jnp.sum(_headsum(dh * hprev, hsum), axis=0, keepdims=True) * f["exp_al"]
        da = _dot_hi(triu_ref[...], rsum - csum_t.T + q) + _dot_hi(trils_ref[...], dstate) + h0t
        dt, a_h = f["dt"], f["a_h"]
        ddt = _headsum(dxh * xs, hsum) + da * a_h
        dal_ref[...] += jnp.sum(da * dt, axis=0, keepdims=True) * a_h
        ddtraw = ddt * _sigmoid(f["pre"])
        first16 = lane < SSD_HEADS
        ddtraw = jnp.where(first16, ddtraw, 0.0)
        ddt_ref[...] = ddtraw
        ddtb_ref[...] += jnp.sum(ddtraw, axis=0, keepdims=True)
        dh_ref[...] = jnp.concatenate(dhp_parts, axis=1)
        dxa = jnp.concatenate([dxh * f["dt_x"] + f["d_x"] * dy] + db_parts + dc_parts, axis=1)
        sig, conv = f["sig"], f["conv"]
        dconv = dxa * sig * (1.0 + conv * (1.0 - sig))
        dcb_ref[...] += jnp.sum(dconv, axis=0, keepdims=True)
        rows = [jnp.sum(dconv * f["sh"][3 - k], axis=0, keepdims=True) for k in range(SSD_CONV)]
        dcw_ref[...] += jnp.concatenate(rows + [jnp.zeros((8 - SSD_CONV, D_XBC), F32)], axis=0)
        de = jnp.concatenate([dconv, dnext_ref[...]], axis=0)
        dxbc = cw[3:4] * dconv
        for s in (1, 2, 3):
            dxbc = dxbc + cw[3 - s:4 - s] * pltpu.roll(de, L + 8 - s, 0)[:L]
        dp_ref[:, D_S5 + D_SSD:] = dxbc.astype(BF16)
        dnext_ref[...] = dconv[:8]

    rev = lambda i: nc - 1 - i
    acc = lambda shape: pl.BlockSpec(shape, lambda i: (0, 0))
    tri = pl.BlockSpec((L, L), lambda i: (0, 0))
    return _call(
        body, grid=(nc,),
        in_specs=[pl.BlockSpec((L, D_SSD), lambda i: (rev(i), 1)), pl.BlockSpec((L, D_XBC), lambda i: (rev(i), 1)),
                  pl.BlockSpec((8, D_XBC), lambda i: (jnp.maximum(rev(i) * (L // 8) - 1, 0), 1)),
                  pl.BlockSpec((L, 128), lambda i: (rev(i), 0)),
                  pl.BlockSpec((1, SSD_STATE, D_SSD), lambda i: (rev(i), 0, 0)),
                  pl.BlockSpec((L, D_SSD), lambda i: (rev(i), 1))] + _ssd_param_specs()
                 + [tri, tri, tri, pl.BlockSpec((D_SSD, 128), lambda i: (0, 0))],
        out_specs=[pl.BlockSpec((L, D_MAIN), lambda i: (rev(i), 0)), pl.BlockSpec((L, 128), lambda i: (rev(i), 0)),
                   acc((8, D_XBC)), acc((1, D_XBC)), acc((1, 128)), acc((1, 128)), acc((1, 128)), acc((1, D_SSD))],
        out_shape=[jax.ShapeDtypeStruct((T, D_MAIN), BF16),
                   jax.ShapeDtypeStruct((T, 128), F32), jax.ShapeDtypeStruct((8, D_XBC), F32),
                   jax.ShapeDtypeStruct((1, D_XBC), F32), jax.ShapeDtypeStruct((1, 128), F32),
                   jax.ShapeDtypeStruct((1, 128), F32), jax.ShapeDtypeStruct((1, 128), F32),
                   jax.ShapeDtypeStruct((1, D_SSD), F32)],
        scratch_shapes=[pltpu.VMEM((SSD_STATE, D_SSD), F32), pltpu.VMEM((8, D_XBC), F32)],
        sem=("arbitrary",), name=name, comm=comm,
        args=(proj, proj, proj, dtraw, hs, dymix, *params, tril, triu, trils, headsum))


def _s5_blockdiag(v, rows_per_group, cols_per_group):
    eye = jnp.eye(S5_SEG, dtype=v.dtype)
    w = v[:, :, :, None, :] * eye[None, :, None, :, None]
    return w.reshape(S5_CH, 8 * rows_per_group, 8 * cols_per_group)


def _s5_blockdiag_extract(w, rows_per_group, cols_per_group):
    eye = jnp.eye(S5_SEG, dtype=w.dtype)
    w5 = w.reshape(S5_CH, 8, rows_per_group, 8, cols_per_group)
    return jnp.sum(w5 * eye[None, :, None, :, None], axis=3)


TM = 512

OFF = dict(mlp_w2=0, mlp_w1=1024, w_out=2048, s5_w_glu=2560, xa_wq=2816, xa_wk=3072, xa_wv=3328, xa_wo=3584, w_in=3840)
ROWS = dict(mlp_w2=1024, mlp_w1=1024, w_out=512, s5_w_glu=256, xa_wq=256, xa_wk=256, xa_wv=256, xa_wo=256, w_in=1028)
TAIL_OFF = 4872
TAIL_ROWS = 120
PACK_ROWS = TAIL_OFF + TAIL_ROWS
HALF_ROWS = PACK_ROWS // 2
SUM_ROWS = 832
N_SHARD = 4
SMALL_L = ("norm_mix", "s5_a_re", "s5_a_im", "s5_log_dt", "s5_b_re", "s5_b_im", "s5_c_re", "s5_c_im", "s5_d",
           "ssd_conv_w", "ssd_conv_b", "ssd_dt_bias", "ssd_a_log", "ssd_d", "ssd_norm", "norm_xattn", "norm_mem",
           "norm_mlp")
SMALL_Q = 72
CONV_ROWS = 8


def _place():
    x, y, c = lax.axis_index("x"), lax.axis_index("y"), lax.axis_index("c")
    chips = [(1 - x, y), (x, 1 - y), (1 - x, 1 - y)]
    return x, y, c, 2 * x + y, chips, (x, y, 1 - c)


def _remote(src, dst, send_sem, recv_sem, to):
    return pltpu.make_async_remote_copy(src_ref=src, dst_ref=dst, send_sem=send_sem, recv_sem=recv_sem,
                                        device_id=to, device_id_type=MESH_T)


def _half(ref, c):
    return ref.at[pl.ds(pl.multiple_of(c * HALF_ROWS, 16), HALF_ROWS), :]


def _dma_sems(*counts):
    return [pltpu.SemaphoreType.DMA((n,)) for n in counts]


def place_own(wpack, shard_idx):
    tile = PACK_ROWS // 4

    def body(s_ref, w_ref, o_ref):
        o_ref[0] = w_ref[...]

    return pl.pallas_call(
        body,
        grid_spec=pltpu.PrefetchScalarGridSpec(
            num_scalar_prefetch=1, grid=(4,),
            in_specs=[pl.BlockSpec((tile, D_MODEL), lambda i, s_ref: (i, 0))],
            out_specs=pl.BlockSpec((1, tile, D_MODEL), lambda i, s_ref: (s_ref[0], i, 0))),
        out_shape=jax.ShapeDtypeStruct((N_SHARD,) + wpack.shape, wpack.dtype),
        compiler_params=_cp("parallel"), name="place_own")(shard_idx, wpack)


def gather_over_ici(wpack, placed):
    def copies(ci, co, cs):
        w_ref, (out_ref,), (send, recv) = ci[0], co, cs
        x, y, c, s, chips, sibling = _place()
        sends = [_remote(_half(w_ref, c), _half(out_ref.at[s], c), send.at[j], recv.at[j], (*chip, c))
                 for j, chip in enumerate(chips)]
        lands = [_half(out_ref.at[2 * chip[0] + chip[1]], c) for chip in chips]
        arrivals = [_remote(land, land, send.at[j], recv.at[j], sibling) for j, land in enumerate(lands)]
        return sends, arrivals

    def start(ci, co, cs):
        for cp in copies(ci, co, cs)[0]:
            cp.start()

    def wait(ci, co, cs):
        sends, arrivals = copies(ci, co, cs)
        for cp in arrivals:
            cp.wait_recv()
        for cp in sends:
            cp.wait_send()

    return Comm("gather_ici", [wpack, placed], [jax.ShapeDtypeStruct(placed.shape, placed.dtype)], _dma_sems(3, 3),
                start, wait, aliases={1: 0})


def gather_to_sibling(landed):
    def copies(co, cs):
        (out_ref,), (send, recv) = co, cs
        x, y, c, s, chips, sibling = _place()
        slots = [out_ref.at[2 * chip[0] + chip[1]] for chip in chips]
        sends = [_remote(_half(slot, c), _half(slot, c), send.at[j], recv.at[j], sibling) for j, slot in enumerate(slots)]
        arrivals = [_remote(_half(slot, 1 - c), _half(slot, 1 - c), send.at[j], recv.at[j], sibling)
                    for j, slot in enumerate(slots)]
        return sends, arrivals

    def start(ci, co, cs):
        for cp in copies(co, cs)[0]:
            cp.start()

    def wait(ci, co, cs):
        sends, arrivals = copies(co, cs)
        for cp in arrivals:
            cp.wait_recv()
        for cp in sends:
            cp.wait_send()

    return Comm("gather_d2d", [landed], [jax.ShapeDtypeStruct(landed.shape, landed.dtype)], _dma_sems(3, 3), start, wait,
                aliases={0: 0})


def exchange_halves(gpack):
    def copy(ci, co, cs):
        (g_ref,), (out_ref,), (send, recv) = ci, co, cs
        x, y, c, s, chips, sibling = _place()
        src = g_ref.at[:, pl.ds(pl.multiple_of((1 - c) * HALF_ROWS, 16), HALF_ROWS), :]
        return _remote(src, out_ref, send.at[0], recv.at[0], sibling)

    return Comm("exchange", [gpack], [jax.ShapeDtypeStruct((N_SHARD, HALF_ROWS, D_MODEL), F32)], _dma_sems(1, 1),
                lambda ci, co, cs: copy(ci, co, cs).start(), lambda ci, co, cs: copy(ci, co, cs).wait())


def scatter_chips(csum):
    def copies(ci, co, cs):
        (c_ref,), (out_ref,), (send, recv) = ci, co, cs
        x, y, c, s, chips, sibling = _place()
        sends = [_remote(c_ref.at[2 * chip[0] + chip[1]], out_ref.at[s], send.at[j], recv.at[j], (*chip, c))
                 for j, chip in enumerate(chips)]
        arrivals = [_remote(c_ref.at[2 * chip[0] + chip[1]], out_ref.at[2 * chip[0] + chip[1]], send.at[j], recv.at[j],
                            (*chip, c)) for j, chip in enumerate(chips)]
        return sends, arrivals

    def start(ci, co, cs):
        for cp in copies(ci, co, cs)[0]:
            cp.start()

    def wait(ci, co, cs):
        sends, arrivals = copies(ci, co, cs)
        for cp in arrivals:
            cp.wait_recv()
        for cp in sends:
            cp.wait_send()

    return Comm("scatter", [csum], [jax.ShapeDtypeStruct(csum.shape, csum.dtype)], _dma_sems(3, 3), start, wait)


def share_reduced(gshards, smalls, l):
    def copies(ci, co, cs):
        (g_ref, sm_ref), (send, recv, loc) = co, cs
        x, y, c, s, chips, sibling = _place()
        my_half = _half(g_ref.at[l], c)
        tail = g_ref.at[l, pl.ds(PACK_ROWS - TAIL_ROWS, TAIL_ROWS), :]
        big = _remote(my_half, my_half, send.at[0], recv.at[0], sibling)
        keep_tail = pltpu.make_async_copy(tail, sm_ref.at[l, s], loc.at[0])
        tails = [_remote(tail, sm_ref.at[l, s], send.at[1 + j], recv.at[1 + j], (*chip, 1)) for j, chip in enumerate(chips)]
        tails += [_remote(tail, sm_ref.at[l, s], send.at[4 + j], recv.at[1 + j], (*chip, 0)) for j, chip in enumerate(chips)]
        tails.append(_remote(tail, sm_ref.at[l, s], send.at[7], recv.at[4], sibling))
        other = _half(g_ref.at[l], 1 - c)
        big_in = _remote(other, other, send.at[0], recv.at[0], sibling)
        slots = [sm_ref.at[l, 2 * chip[0] + chip[1]] for chip in chips]
        tails_in = [_remote(slot, slot, send.at[1 + j], recv.at[1 + j], sibling) for j, slot in enumerate(slots)]
        sib_tail_in = _remote(sm_ref.at[l, s], sm_ref.at[l, s], send.at[7], recv.at[4], sibling)
        return c, big, keep_tail, tails, big_in, tails_in, sib_tail_in

    def start(ci, co, cs):
        c, big, keep_tail, tails, _, _, _ = copies(ci, co, cs)
        big.start()

        @pl.when(c == 1)
        def _():
            keep_tail.start()
            for cp in tails:
                cp.start()

    def wait(ci, co, cs):
        c, big, keep_tail, tails, big_in, tails_in, sib_tail_in = copies(ci, co, cs)
        big_in.wait_recv()
        for cp in tails_in:
            cp.wait_recv()
        big.wait_send()

        @pl.when(c == 0)
        def _():
            sib_tail_in.wait_recv()

        @pl.when(c == 1)
        def _():
            for cp in tails:
                cp.wait_send()
            keep_tail.wait()

    sds = lambda a: jax.ShapeDtypeStruct(a.shape, a.dtype)
    return Comm("share", [gshards, smalls], [sds(gshards), sds(smalls)], _dma_sems(8, 5, 1), start, wait,
                aliases={0: 0, 1: 1})


def _consts():
    e = np.zeros((S5_STATE, S5_STATE * S5_GROUP), np.float32)
    for p in range(S5_STATE):
        e[p, p * S5_GROUP:(p + 1) * S5_GROUP] = 1.0
    hs = np.zeros((D_SSD, 128), np.float32)
    for h in range(SSD_HEADS):
        hs[h * SSD_HEADDIM:(h + 1) * SSD_HEADDIM, h] = 1.0
    ones = np.ones((SSD_L, SSD_L), np.float32)
    return dict(expand=jnp.asarray(e), expand_t=jnp.asarray(e.T), headsum=jnp.asarray(hs).astype(BF16),
                tril=jnp.asarray(np.tril(ones)), triu=jnp.asarray(np.triu(ones)), trils=jnp.asarray(np.tril(ones, -1)))


def _s5_mats(w, cst):
    b_re = w["s5_b_re"].reshape(S5_GROUPS, S5_STATE * S5_GROUP)
    b_im = w["s5_b_im"].reshape(S5_GROUPS, S5_STATE * S5_GROUP)
    abr, abi, apr, api, bbr, bbi = s5_params_fwd(w["s5_a_re"], w["s5_a_im"], w["s5_log_dt"].reshape(S5_GROUPS, 1),
                                                 b_re, b_im, cst["expand"], name="s5_params_fwd")
    t = lambda v: v.reshape(S5_CH, 8, S5_STATE, S5_GROUP).transpose(0, 1, 3, 2)
    c4 = lambda v: v.reshape(S5_CH, 8, S5_GROUP, S5_STATE).transpose(0, 1, 3, 2)
    vec = lambda v: v.reshape(S5_CH, 1, S5_NST)
    return dict(bbr=_s5_blockdiag(t(bbr), S5_GROUP, S5_STATE).astype(BF16),
                bbi=_s5_blockdiag(t(bbi), S5_GROUP, S5_STATE).astype(BF16),
                ccr=_s5_blockdiag(c4(w["s5_c_re"]), S5_STATE, S5_GROUP).astype(BF16),
                cci=_s5_blockdiag(c4(w["s5_c_im"]), S5_STATE, S5_GROUP).astype(BF16),
                abr=vec(abr), abi=vec(abi), apr=vec(apr), api=vec(api),
                dsk=w["s5_d"].reshape(S5_CH, 1, S5_CW), b_re=b_re, b_im=b_im)


def _layer_fwd(x, mem, w, cst, next_pack=None):
    sv = {}
    g = lambda n: w[n].reshape(1, -1)
    pack = w["pack"]
    proj, h0, r0, dtraw = norm_matmul(x, g("norm_mix"), w["w_in_main"], w["w_in_dt"], tm=TM, tn=1024, name="in_proj")
    s5m = _s5_mats(w, cst)
    (ys, csr, csi), landed = s5_fwd(proj, s5m["bbr"], s5m["bbi"], s5m["ccr"], s5m["cci"], s5m["abr"], s5m["abi"],
                                    s5m["apr"], s5m["api"], s5m["dsk"], name="s5_fwd",
                                    comm=None if next_pack is None else gather_over_ici(*next_pack))
    (ymix, tglu), gathered = glu_fwd(ys, w["s5_w_glu"], tm=TM, name="glu_fwd",
                                     comm=None if next_pack is None else gather_to_sibling(landed[0]))
    ssdp = _ssd_params(w["ssd_conv_w"], w["ssd_conv_b"], w["ssd_dt_bias"], w["ssd_a_log"], w["ssd_d"], w["ssd_norm"])
    (ymix, hs), _ = ssd_fwd(proj, dtraw, ssdp, cst["tril"], ymix, name="ssd_fwd")
    x1 = matmul_res(ymix, pack, x, tm=TM, tn=1024, tk=512, name="out_proj",
                    wspec=pl.BlockSpec((1, 512, 1024), lambda i, j, k: (k, OFF["w_out"] // 512, j)))
    q, h1, r1 = norm_matmul(x1, g("norm_xattn"), w["xa_wq"], tm=TM, tn=1024, name="q_proj")
    kv, hm, rm = norm_matmul(mem, g("norm_mem"), w["xa_wkv"], tm=mem.shape[0], tn=1024, name="kv_proj")
    o = attn_fwd(q, kv, tm=TM, name="attn_fwd")
    x2 = matmul_res(o, w["xa_wo"], x1, tm=TM, tn=1024, tk=1024, name="attn_out")
    f1, h2, r2 = norm_matmul(x2, g("norm_mlp"), pack, tm=TM, tn=1024, name="mlp_up", n_out=D_FF,
                             wspec=pl.BlockSpec((1, D_MODEL, 1024), lambda i, j: (j, OFF["mlp_w1"] // 1024, 0)))
    x3 = matmul_res(f1, pack, x2, act="relu2", tm=TM, tn=1024, tk=1024, name="mlp_down",
                    wspec=pl.BlockSpec((1, 1024, 1024), lambda i, j, k: (k, OFF["mlp_w2"] // 1024, j)))
    sv.update(x=x, proj=proj, h0=h0, r0=r0, dtraw=dtraw, s5m=s5m, ys=ys, csr=csr, csi=csi, tglu=tglu, ssdp=ssdp,
              hs=hs, ymix=ymix, x1=x1, q=q, h1=h1, r1=r1, kv=kv, hm=hm, rm=rm, o=o, x2=x2, f1=f1, h2=h2, r2=r2)
    return x3, sv, (gathered[0] if next_pack is not None else None)


def _layer_bwd(dx3, mem, w, sv, cst, extra_small=None, reduce_hooks=None):
    gr = {}
    g = lambda n: w[n].reshape(1, -1)
    pack = w["pack"]
    pshape = (N_SHARD, PACK_ROWS, D_MODEL)
    ps = lambda rows, f: pl.BlockSpec((1, rows, 1024), f)
    ps4 = lambda rows, f: pl.BlockSpec((N_SHARD, rows, 1024), f)
    dh1 = matmul_nt(dx3, pack, epi="relu2bwd", epi_args=(sv["f1"],), tm=TM, tko=1024, tn=1024, name="mlp_down_dx",
                    wspec=ps(1024, lambda i, k, n: (k, OFF["mlp_w2"] // 1024, n)), k_out=D_FF,
                    comm=reduce_hooks.exchange() if reduce_hooks else None)
    if reduce_hooks:
        dh1, recv = dh1
        reduce_hooks.after_exchange(recv[0])
    gp = matmul_tn(sv["f1"], dx3, act="relu2", tk=1024, tn=1024, tt=TM, name="mlp_down_dw", pack_shape=pshape,
                   pack_spec=ps(1024, lambda k, n, t: (k, OFF["mlp_w2"] // 1024, 0)))
    gp = matmul_tn(sv["h2"], dh1, tk=1024, tn=1024, tt=TM, name="mlp_up_dw", pack=gp, pack_shape=pshape,
                   pack_spec=ps(1024, lambda k, n, t: (n, OFF["mlp_w1"] // 1024, 0)))
    dx2, gr["norm_mlp"] = matmul_nt(dh1, pack, epi="rmsbwd", epi_args=(sv["x2"], sv["r2"], g("norm_mlp"), dx3),
                                    tm=TM, tko=1024, tn=1024, name="mlp_up_dx", k_out=D_MODEL,
                                    wspec=ps(1024, lambda i, k, n: (n, OFF["mlp_w1"] // 1024, 0)))
    do = matmul_nt(dx2, w["xa_wo"], tm=TM, tko=1024, tn=1024, name="attn_out_dx")
    gp = matmul_tn(sv["o"], dx2, tk=1024, tn=1024, tt=TM, name="attn_out_dw", pack=gp, pack_shape=pshape,
                   pack_spec=ps4(256, lambda k, n, t: (0, OFF["xa_wo"] // 256, 0)))
    dq, dkv = attn_bwd(sv["q"], sv["kv"], do, tm=TM, name="attn_bwd")
    gp = matmul_tn(sv["h1"], dq, tk=1024, tn=1024, tt=TM, name="q_proj_dw", pack=gp, pack_shape=pshape,
                   pack_spec=ps4(256, lambda k, n, t: (0, OFF["xa_wq"] // 256, 0)))
    dx1, gr["norm_xattn"] = matmul_nt(dq, w["xa_wq"], epi="rmsbwd", epi_args=(sv["x1"], sv["r1"], g("norm_xattn"), dx2),
                                      tm=TM, tko=1024, tn=1024, name="q_proj_dx")
    M = mem.shape[0]
    gp = matmul_tn(sv["hm"], dkv, tk=1024, tn=1024, tt=M, name="kv_proj_dw", pack=gp, pack_shape=pshape,
                   pack_spec=ps4(256, lambda k, n, t: (0, OFF["xa_wk"] // 256 + n, 0)))
    _, gr["norm_mem"] = matmul_nt(dkv, w["xa_wkv"], epi="rmsbwd",
                                  epi_args=(mem, sv["rm"], g("norm_mem"), jnp.zeros_like(mem)),
                                  tm=M, tko=1024, tn=1024, name="kv_proj_dx")
    dymix = matmul_nt(dx1, pack, tm=TM, tko=512, tn=1024, name="out_proj_dx", k_out=2 * D_MODEL,
                      wspec=ps(512, lambda i, k, n: (k, OFF["w_out"] // 512, n)))
    gp = matmul_tn(sv["ymix"], dx1, tk=2048, tn=1024, tt=TM, name="out_proj_dw", pack=gp, pack_shape=pshape,
                   pack_spec=ps4(512, lambda k, n, t: (0, OFF["w_out"] // 512, 0)))
    (dproj, ddtraw, dcw, dcb, ddtb, dal, ddsk, dnw), parts = ssd_bwd(
        sv["proj"], sv["dtraw"], sv["hs"], dymix, sv["ssdp"], cst["tril"], cst["triu"], cst["trils"], cst["headsum"],
        name="ssd_bwd", comm=reduce_hooks.scatter() if reduce_hooks else None)
    if reduce_hooks:
        reduce_hooks.after_scatter(parts[0])
    gr["ssd_conv_w"], gr["ssd_conv_b"] = dcw[:SSD_CONV], dcb[0]
    gr["ssd_dt_bias"], gr["ssd_a_log"], gr["ssd_d"] = ddtb[0, :SSD_HEADS], dal[0, :SSD_HEADS], ddsk[0, :SSD_HEADS]
    gr["ssd_norm"] = dnw[0]
    dtg, da1 = glu_bwd_pre(dymix, sv["ys"], sv["tglu"], tm=TM, name="glu_bwd_pre")
    gp = matmul_tn(sv["ys"], dtg, act="gelu", tk=1024, tn=1024, tt=TM, name="glu_dw", pack=gp, pack_shape=pshape,
                   pack_spec=ps4(256, lambda k, n, t: (0, OFF["s5_w_glu"] // 256, 0)))
    dys = matmul_nt(dtg, w["s5_w_glu"], epi="glubwd", epi_args=(da1, sv["ys"]), tm=TM, tko=1024, tn=1024,
                    name="glu_dx", comm=reduce_hooks.share() if reduce_hooks else None)
    if reduce_hooks:
        dys, shared = dys
        reduce_hooks.after_share(shared)
    s5m = sv["s5m"]
    dproj, dbbr, dbbi, dccr, dcci, dd, dabr, dabi = s5_bwd(
        sv["proj"], dys, sv["csr"], sv["csi"], s5m["bbr"], s5m["bbi"], s5m["ccr"], s5m["cci"], s5m["abr"], s5m["abi"],
        s5m["apr"], s5m["api"], s5m["dsk"], dproj, name="s5_bwd")
    tb = lambda v: _s5_blockdiag_extract(v, S5_GROUP, S5_STATE).transpose(0, 1, 3, 2).reshape(S5_GROUPS, -1)
    tc = lambda v: _s5_blockdiag_extract(v, S5_STATE, S5_GROUP).transpose(0, 1, 3, 2).reshape(S5_GROUPS, S5_GROUP, S5_STATE)
    gr["s5_c_re"], gr["s5_c_im"] = tc(dccr), tc(dcci)
    gr["s5_d"] = dd.reshape(S5_GROUPS, S5_GROUP)
    dar, dai, dld, dbr, dbi = s5_params_bwd(
        w["s5_a_re"], w["s5_a_im"], w["s5_log_dt"].reshape(S5_GROUPS, 1), s5m["b_re"], s5m["b_im"],
        dabr.reshape(S5_GROUPS, S5_STATE), dabi.reshape(S5_GROUPS, S5_STATE), tb(dbbr), tb(dbbi),
        cst["expand"], cst["expand_t"], name="s5_params_bwd")
    gr["s5_a_re"], gr["s5_a_im"], gr["s5_log_dt"] = dar, dai, dld[:, 0]
    gr["s5_b_re"] = dbr.reshape(S5_GROUPS, S5_STATE, S5_GROUP)
    gr["s5_b_im"] = dbi.reshape(S5_GROUPS, S5_STATE, S5_GROUP)
    dw_main = matmul_tn(sv["h0"], dproj, tk=1024, tn=1024, tt=TM, name="in_proj_dw")
    dw_dt = matmul_tn(sv["h0"], ddtraw, tk=1024, tn=D_DT_PAD, tt=TM, name="in_proj_dt_dw")
    dx0, gr["norm_mix"] = matmul_nt(dproj, w["w_in_main"], g2=ddtraw, w2=w["w_in_dt"], epi="rmsbwd",
                                    epi_args=(sv["x"], sv["r0"], g("norm_mix"), dx1), tm=TM, tko=1024, tn=1024,
                                    name="in_proj_dx")
    gr = {k: (v[0] if k.startswith("norm_") else v) for k, v in gr.items()}
    dw_in = jnp.concatenate([dw_main, dw_dt[:, :SSD_HEADS]], axis=1)
    dw_in = dw_in.reshape(D_MODEL, N_SHARD, ROWS["w_in"]).transpose(1, 0, 2).reshape(N_SHARD, ROWS["w_in"], D_MODEL)
    small = [gr[n].reshape(-1) for n in SMALL_L] + ([] if extra_small is None else [extra_small.reshape(-1)])
    small = jnp.concatenate(small)
    small = jnp.pad(small, (0, N_SHARD * SMALL_Q * D_MODEL - small.size)).reshape(N_SHARD, SMALL_Q, D_MODEL)
    rest = jnp.concatenate([dw_in, jnp.zeros((N_SHARD, TAIL_OFF - OFF["w_in"] - ROWS["w_in"], D_MODEL), F32), small,
                            jnp.zeros((N_SHARD, TAIL_ROWS - SMALL_Q, D_MODEL), F32)], axis=1)
    gp = lax.dynamic_update_slice(gp, rest, (0, OFF["w_in"], 0))
    return dx0, gp


def _local_step(x, mem, target, layers, norm_final):
    cst = _consts()
    saved = []
    for l in range(DEPTH):
        x, sv, _ = _layer_fwd(x, mem, layers[l], cst)
        saved.append(sv)
    loss, dx, dgf = loss_head(x, norm_final.reshape(1, -1), target, tm=TM, name="loss_head")
    packs = [None] * DEPTH
    for l in reversed(range(DEPTH)):
        dx, packs[l] = _layer_bwd(dx, mem, layers[l], saved[l], cst, extra_small=dgf[0] if l == DEPTH - 1 else None)
    return loss, dx, packs


def sum_halves(gpack, recv, c_idx):
    nb = HALF_ROWS // SUM_ROWS

    def body(c_ref, a_ref, b_ref, o_ref):
        o_ref[...] = (a_ref[...] + b_ref[...]).astype(BF16)

    blk = (1, SUM_ROWS, D_MODEL)
    return pl.pallas_call(
        body,
        grid_spec=pltpu.PrefetchScalarGridSpec(
            num_scalar_prefetch=1, grid=(N_SHARD, nb),
            in_specs=[pl.BlockSpec(blk, lambda t, i, c_ref: (t, c_ref[0] * nb + i, 0)),
                      pl.BlockSpec(blk, lambda t, i, c_ref: (t, i, 0))],
            out_specs=pl.BlockSpec(blk, lambda t, i, c_ref: (t, i, 0))),
        out_shape=jax.ShapeDtypeStruct((N_SHARD, HALF_ROWS, D_MODEL), BF16),
        compiler_params=_cp("parallel", "parallel"), name="sum_halves")(c_idx, gpack, recv)


def sum_chips(parts, csum, gshards, l, place_idx):
    nb = HALF_ROWS // SUM_ROWS

    def body(pi_ref, p0, p1, p2, p3, own, g_ref, o_ref):
        s = pi_ref[0]
        vals = [jnp.where(s == k, own[0], p[0]).astype(F32) for k, p in enumerate((p0, p1, p2, p3))]
        o_ref[0] = ((vals[0] + vals[1]) + vals[2]) + vals[3]

    blk = (1, SUM_ROWS, D_MODEL)
    part_spec = lambda k: pl.BlockSpec(blk, lambda i, pi_ref: (jnp.where(pi_ref[0] == k, (k + 1) % N_SHARD, k), i, 0))
    return pl.pallas_call(
        body,
        grid_spec=pltpu.PrefetchScalarGridSpec(
            num_scalar_prefetch=1, grid=(nb,),
            in_specs=[part_spec(k) for k in range(N_SHARD)]
                     + [pl.BlockSpec(blk, lambda i, pi_ref: (pi_ref[0], i, 0)), _ANY],
            out_specs=pl.BlockSpec(blk, lambda i, pi_ref: (l, pi_ref[1] * nb + i, 0))),
        out_shape=jax.ShapeDtypeStruct(gshards.shape, F32), input_output_aliases={6: 0},
        compiler_params=_cp("parallel"), name="sum_chips")(place_idx, parts, parts, parts, parts, csum, gshards)


class _Reduction:
    def __init__(self, gpack, layer, place_idx, gshards, smalls):
        self.gpack, self.layer, self.place_idx, self.gshards, self.smalls = gpack, layer, place_idx, gshards, smalls

    def exchange(self):
        return exchange_halves(self.gpack)

    def after_exchange(self, recv):
        self.csum = sum_halves(self.gpack, recv, self.place_idx[1:])

    def scatter(self):
        return scatter_chips(self.csum)

    def after_scatter(self, parts):
        self.gshards = sum_chips(parts, self.csum, self.gshards, self.layer, self.place_idx)

    def share(self):
        return share_reduced(self.gshards, self.smalls, self.layer)

    def after_share(self, shared):
        self.gshards, self.smalls = shared

    def run_alone(self):
        self.after_exchange(_comm_only(self.exchange())[0])
        self.after_scatter(_comm_only(self.scatter())[0])
        self.after_share(_comm_only(self.share()))
        return self.gshards, self.smalls


def adamw(w, g, m, v, *, name):
    shape = w.shape
    cols = shape[-1]
    rows = w.size // cols
    tr = 512 if rows % 512 == 0 else rows
    c1 = 1.0 / (1.0 - ADAM_B1 ** ADAM_STEP)
    c2 = 1.0 / (1.0 - ADAM_B2 ** ADAM_STEP)

    def body(w_ref, g_ref, m_ref, v_ref, d_ref, nm_ref, nv_ref):
        gv = g_ref[...]
        nm = ADAM_B1 * m_ref[...] + (1.0 - ADAM_B1) * gv
        nv = ADAM_B2 * v_ref[...] + (1.0 - ADAM_B2) * (gv * gv)
        d_ref[...] = -ADAM_LR * ((nm * c1) / (jnp.sqrt(nv * c2) + ADAM_EPS) + ADAM_WD * w_ref[...])
        nm_ref[...] = nm
        nv_ref[...] = nv

    spec = pl.BlockSpec((tr, cols), lambda i: (i, 0))
    sds = jax.ShapeDtypeStruct((rows, cols), F32)
    outs = pl.pallas_call(body, grid=(rows // tr,), in_specs=[spec] * 4, out_specs=[spec] * 3, out_shape=[sds] * 3,
                          compiler_params=_cp("parallel"), name=name)(
                              *[a.reshape(rows, cols) for a in (w, g, m, v)])
    return [o.reshape(shape) for o in outs]


def _own_pack(wts, l):
    rows = [wts[n][l].reshape(ROWS[n], D_MODEL) for n in sorted(OFF, key=OFF.get)]
    cw = wts["ssd_conv_w"][l].reshape(-1)
    hi = lax.reduce_precision(cw, 8, 7)
    mid = lax.reduce_precision(cw - hi, 8, 7)
    lo = lax.reduce_precision(cw - hi - mid, 8, 7)
    conv = jnp.pad(jnp.concatenate([hi, mid, lo]), (0, CONV_ROWS * D_MODEL - 3 * cw.size)).reshape(CONV_ROWS, D_MODEL)
    gap = jnp.zeros((TAIL_OFF - OFF["w_in"] - ROWS["w_in"], D_MODEL), F32)
    rest = jnp.zeros((TAIL_ROWS - CONV_ROWS, D_MODEL), F32)
    return jnp.concatenate(rows + [gap, conv, rest], axis=0).astype(BF16)


def _layer_weights(gathered, wts, l):
    w = {n: wts[n][l] for n in SMALL_L if n != "ssd_conv_w"}
    w["pack"] = gathered
    square = lambda n: gathered[:, OFF[n]:OFF[n] + ROWS[n]].reshape(N_SHARD * ROWS[n], D_MODEL)
    w["s5_w_glu"], w["xa_wq"], w["xa_wo"] = square("s5_w_glu"), square("xa_wq"), square("xa_wo")
    w["xa_wkv"] = jnp.concatenate([square("xa_wk"), square("xa_wv")], axis=1)
    cols = ROWS["w_in"]
    w_in = gathered[:, OFF["w_in"]:OFF["w_in"] + cols].reshape(N_SHARD, D_MODEL, cols).transpose(1, 0, 2)
    w_in = w_in.reshape(D_MODEL, N_SHARD * cols)
    w["w_in_main"] = w_in[:, :D_MAIN]
    w["w_in_dt"] = jnp.pad(w_in[:, D_MAIN:], ((0, 0), (0, D_DT_PAD - SSD_HEADS)))
    per = SSD_CONV * D_XBC // N_SHARD
    cw = gathered[:, TAIL_OFF:TAIL_OFF + CONV_ROWS].astype(F32).reshape(N_SHARD, -1)[:, :3 * per]
    cw = cw.reshape(N_SHARD, 3, SSD_CONV, D_XBC // N_SHARD)
    cw = (cw[:, 0] + cw[:, 1]) + cw[:, 2]
    w["ssd_conv_w"] = cw.transpose(1, 0, 2).reshape(SSD_CONV, D_XBC)
    return w


def kernel(x, mem, norm_mix, w_in, s5_a_re, s5_a_im, s5_log_dt, s5_b_re, s5_b_im, s5_c_re, s5_c_im, s5_d, s5_w_glu, ssd_conv_w, ssd_conv_b, ssd_dt_bias, ssd_a_log, ssd_d, ssd_norm, w_out, norm_xattn, norm_mem, xa_wq, xa_wk, xa_wv, xa_wo, norm_mlp, mlp_w1, mlp_w2, norm_final, loss_target, m_norm_mix, m_w_in, m_s5_a_re, m_s5_a_im, m_s5_log_dt, m_s5_b_re, m_s5_b_im, m_s5_c_re, m_s5_c_im, m_s5_d, m_s5_w_glu, m_ssd_conv_w, m_ssd_conv_b, m_ssd_dt_bias, m_ssd_a_log, m_ssd_d, m_ssd_norm, m_w_out, m_norm_xattn, m_norm_mem, m_xa_wq, m_xa_wk, m_xa_wv, m_xa_wo, m_norm_mlp, m_mlp_w1, m_mlp_w2, m_norm_final, v_norm_mix, v_w_in, v_s5_a_re, v_s5_a_im, v_s5_log_dt, v_s5_b_re, v_s5_b_im, v_s5_c_re, v_s5_c_im, v_s5_d, v_s5_w_glu, v_ssd_conv_w, v_ssd_conv_b, v_ssd_dt_bias, v_ssd_a_log, v_ssd_d, v_ssd_norm, v_w_out, v_norm_xattn, v_norm_mem, v_xa_wq, v_xa_wk, v_xa_wv, v_xa_wo, v_norm_mlp, v_mlp_w1, v_mlp_w2, v_norm_final):
    names = ("norm_mix", "w_in", "s5_a_re", "s5_a_im", "s5_log_dt", "s5_b_re", "s5_b_im", "s5_c_re", "s5_c_im", "s5_d",
             "s5_w_glu", "ssd_conv_w", "ssd_conv_b", "ssd_dt_bias", "ssd_a_log", "ssd_d", "ssd_norm", "w_out",
             "norm_xattn", "norm_mem", "xa_wq", "xa_wk", "xa_wv", "xa_wo", "norm_mlp", "mlp_w1", "mlp_w2", "norm_final")
    loc = locals()
    wts = {n: loc[n] for n in names}
    mom = {n: loc["m_" + n] for n in names}
    var = {n: loc["v_" + n] for n in names}
    shard = 2 * lax.axis_index("x") + lax.axis_index("y")
    core = lax.axis_index("c")

    cst = _consts()
    place_idx = jnp.stack([shard, core]).astype(jnp.int32)
    h, mem0 = x[0], mem[0]

    own = _own_pack(wts, 0)
    landed = _comm_only(gather_over_ici(own, place_own(own, place_idx[:1])))[0]
    gathered = _comm_only(gather_to_sibling(landed))[0]
    layers, saved = [], []
    for l in range(DEPTH):
        layers.append(_layer_weights(gathered, wts, l))
        nxt = None
        if l + 1 < DEPTH:
            own = _own_pack(wts, l + 1)
            nxt = (own, place_own(own, place_idx[:1]))
        h, sv, gathered = _layer_fwd(h, mem0, layers[l], cst, next_pack=nxt)
        saved.append(sv)
    loss, dx, dgf = loss_head(h, norm_final.reshape(1, -1), loss_target[0], tm=TM, name="loss_head")

    gshards = jnp.zeros((DEPTH, PACK_ROWS, D_MODEL), F32)
    smalls = jnp.zeros((DEPTH, N_SHARD, TAIL_ROWS, D_MODEL), F32)
    pending = None
    for l in reversed(range(DEPTH)):
        dx, gpack = _layer_bwd(dx, mem0, layers[l], saved[l], cst, extra_small=dgf[0] if l == DEPTH - 1 else None,
                               reduce_hooks=pending)
        if pending is not None:
            gshards, smalls = pending.gshards, pending.smalls
        pending = _Reduction(gpack, l, place_idx, gshards, smalls)
    gshards, smalls = pending.run_alone()

    g = {n: gshards[:, OFF[n]:OFF[n] + ROWS[n]].reshape(wts[n].shape) for n in OFF}
    small_red = smalls[:, :, :SMALL_Q].reshape(DEPTH, -1)
    off = 0
    for n in SMALL_L:
        shape = (SSD_CONV, D_XBC) if n == "ssd_conv_w" else wts[n].shape[1:]
        size = math.prod(shape)
        g[n] = small_red[:, off:off + size].reshape((DEPTH,) + shape)
        off += size
    g["norm_final"] = small_red[DEPTH - 1, off:off + D_MODEL]
    g["ssd_conv_w"] = lax.dynamic_slice_in_dim(g["ssd_conv_w"], shard * (D_XBC // N_SHARD), D_XBC // N_SHARD, axis=2)

    deltas, new_m, new_v = {}, {}, {}
    for n in names:
        deltas[n], new_m[n], new_v[n] = adamw(wts[n], g[n], mom[n], var[n], name="adamw_" + n)
    loss_all = lax.psum(loss[0, 0], ("x", "y", "c"))
    return (loss_all, dx[None], *[g[n] for n in names], *[deltas[n] for n in names], *[new_m[n] for n in names],
            *[new_v[n] for n in names])
```

```python
import functools
import math

import jax
import jax.numpy as jnp
import numpy as np
from jax import lax
from jax.experimental import pallas as pl
from jax.experimental.pallas import tpu as pltpu

F32 = jnp.float32
BF16 = jnp.bfloat16
HIGHEST = lax.Precision.HIGHEST

D_MODEL = 1024
DEPTH = 4
D_S5 = 1024
D_SSD = 1024
S5_GROUP = 16
S5_GROUPS = 64
S5_STATE = 64
SSD_HEADS = 16
SSD_HEADDIM = 64
SSD_NGROUPS = 4
SSD_STATE = 128
SSD_CONV = 4
SSD_CHUNK = 128
D_XBC = 2048
D_MAIN = 4096
D_DT_PAD = 128
XA_HEADS = 4
XA_HEAD_DIM = 256
D_FF = 4096
EPS = 1e-5
ADAM_LR, ADAM_B1, ADAM_B2, ADAM_EPS, ADAM_WD, ADAM_STEP = 0.001, 0.9, 0.999, 1e-08, 0.01, 10

VMEM_LIMIT = 56 * 1024 * 1024
MESH_T = pl.DeviceIdType.MESH


def _cp(*sem):
    return pltpu.CompilerParams(dimension_semantics=tuple(sem) if sem else None, vmem_limit_bytes=VMEM_LIMIT)


_ANY = pl.BlockSpec(memory_space=pl.ANY)


class Comm:
    def __init__(self, name, inputs, out_shapes, sems, start, wait, aliases=None):
        self.name, self.inputs, self.out_shapes, self.sems = name, list(inputs), list(out_shapes), list(sems)
        self.start, self.wait, self.aliases = start, wait, dict(aliases or {})


def _call(body, *, grid, in_specs, out_specs, out_shape, args, scratch_shapes=(), sem, name, comm=None, aliases=None):
    in_specs, out_specs, out_shape = list(in_specs), list(out_specs), list(out_shape)
    scratch_shapes = list(scratch_shapes)
    aliases = dict(aliases or {})
    if comm is None:
        res = pl.pallas_call(body, grid=grid, in_specs=in_specs, out_specs=out_specs, out_shape=out_shape,
                             scratch_shapes=scratch_shapes, compiler_params=_cp(*sem), name=name,
                             input_output_aliases=aliases)(*args)
        return list(res), []
    n_in, n_out, n_scr = len(in_specs), len(out_specs), len(scratch_shapes)
    c_in, c_out = len(comm.inputs), len(comm.out_shapes)

    def wrapped(*refs):
        a, refs = refs[:n_in], refs[n_in:]
        ci, refs = refs[:c_in], refs[c_in:]
        o, refs = refs[:n_out], refs[n_out:]
        co, refs = refs[:c_out], refs[c_out:]
        s, cs = refs[:n_scr], refs[n_scr:]
        first = functools.reduce(jnp.logical_and, [pl.program_id(d) == 0 for d in range(len(grid))])
        last = functools.reduce(jnp.logical_and, [pl.program_id(d) == grid[d] - 1 for d in range(len(grid))])

        @pl.when(first)
        def _():
            comm.start(ci, co, cs)

        body(*a, *o, *s)

        @pl.when(last)
        def _():
            comm.wait(ci, co, cs)

    for i, j in comm.aliases.items():
        aliases[n_in + i] = n_out + j
    res = pl.pallas_call(wrapped, grid=grid, in_specs=in_specs + [_ANY] * c_in, out_specs=out_specs + [_ANY] * c_out,
                         out_shape=out_shape + comm.out_shapes, scratch_shapes=scratch_shapes + comm.sems,
                         compiler_params=_cp(*(("arbitrary",) * len(grid))), name=name + "_" + comm.name,
                         input_output_aliases=aliases)(*args, *comm.inputs)
    return list(res[:n_out]), list(res[n_out:])


def _comm_only(comm):
    def body(*refs):
        ci, refs = refs[:len(comm.inputs)], refs[len(comm.inputs):]
        co, cs = refs[:len(comm.out_shapes)], refs[len(comm.out_shapes):]
        comm.start(ci, co, cs)
        comm.wait(ci, co, cs)

    res = pl.pallas_call(body, in_specs=[_ANY] * len(comm.inputs), out_specs=[_ANY] * len(comm.out_shapes),
                         out_shape=comm.out_shapes, scratch_shapes=comm.sems, name=comm.name,
                         input_output_aliases=comm.aliases)(*comm.inputs)
    return list(res)


def _dot(a, b):
    return jnp.dot(a, b, preferred_element_type=F32)


def _dot_nt(a, b):
    return lax.dot_general(a, b, (((1,), (1,)), ((), ())), preferred_element_type=F32)


def _dot_tn(a, b):
    return lax.dot_general(a, b, (((0,), (0,)), ((), ())), preferred_element_type=F32)


def _dot_hi(a, b):
    return jnp.dot(a, b, precision=HIGHEST, preferred_element_type=F32)


def _gelu(x):
    c = math.sqrt(2.0 / math.pi)
    return 0.5 * x * (1.0 + jnp.tanh(c * (x + 0.044715 * x * x * x)))


def _gelu_grad(x):
    c = math.sqrt(2.0 / math.pi)
    t = jnp.tanh(c * (x + 0.044715 * x * x * x))
    return 0.5 * (1.0 + t) + 0.5 * x * (1.0 - t * t) * c * (1.0 + 3 * 0.044715 * x * x)


def _sigmoid(x):
    return 1.0 / (1.0 + jnp.exp(-x))


def _act(a, act):
    if act is None:
        return a.astype(BF16)
    a = a.astype(F32)
    if act == "relu2":
        a = jnp.maximum(a, 0.0)
        return (a * a).astype(BF16)
    if act == "gelu":
        return _gelu(a).astype(BF16)
    raise ValueError(act)


def _pack_block(ref):
    return ref[...].reshape(-1, ref.shape[-1])


def norm_matmul(x, g, w, w2=None, *, tm, tn, name, wspec=None, n_out=None, w_transposed=False):
    T, D = x.shape
    N = n_out if wspec is not None else (w.shape[0] if w_transposed else w.shape[1])
    wget = (lambda r: r[...]) if wspec is None else _pack_block
    mm = _dot_nt if w_transposed else _dot
    has2 = w2 is not None

    def body(x_ref, g_ref, w_ref, *rest):
        if has2:
            w2_ref, o_ref, h_ref, r_ref, o2_ref = rest
        else:
            o_ref, h_ref, r_ref = rest
        j = pl.program_id(1)

        @pl.when(j == 0)
        def _():
            xv = x_ref[...]
            r = lax.rsqrt(jnp.mean(xv * xv, axis=-1, keepdims=True) + EPS)
            h = (xv * r * g_ref[...]).astype(BF16)
            h_ref[...] = h
            r_ref[...] = r
            if has2:
                o2_ref[...] = mm(h, w2_ref[...])

        o_ref[...] = mm(h_ref[...], wget(w_ref)).astype(o_ref.dtype)

    if wspec is None:
        wspec = pl.BlockSpec((tn, D), lambda i, j: (j, 0)) if w_transposed else pl.BlockSpec((D, tn), lambda i, j: (0, j))
    in_specs = [pl.BlockSpec((tm, D), lambda i, j: (i, 0)), pl.BlockSpec((1, D), lambda i, j: (0, 0)), wspec]
    out_shape = [jax.ShapeDtypeStruct((T, N), BF16), jax.ShapeDtypeStruct((T, D), BF16),
                 jax.ShapeDtypeStruct((T, 1), F32)]
    out_specs = [pl.BlockSpec((tm, tn), lambda i, j: (i, j)), pl.BlockSpec((tm, D), lambda i, j: (i, 0)),
                 pl.BlockSpec((tm, 1), lambda i, j: (i, 0))]
    args = [x, g, w]
    if has2:
        in_specs.append(pl.BlockSpec(w2.shape, lambda i, j: (0, 0)))
        out_shape.append(jax.ShapeDtypeStruct((T, D_DT_PAD), F32))
        out_specs.append(pl.BlockSpec((tm, D_DT_PAD), lambda i, j: (i, 0)))
        args.append(w2)
    return pl.pallas_call(body, grid=(T // tm, N // tn), in_specs=in_specs, out_specs=out_specs,
                          out_shape=out_shape, compiler_params=_cp("parallel", "arbitrary"), name=name)(*args)


def matmul_res(a, w, r, *, act=None, tm, tn, tk, name, wspec=None):
    T, K = a.shape
    N = r.shape[1]
    wget = (lambda r_: r_[...]) if wspec is None else _pack_block
    nk = K // tk

    def body(a_ref, w_ref, r_ref, o_ref):
        k = pl.program_id(2)

        @pl.when(k == 0)
        def _():
            o_ref[...] = r_ref[...]

        o_ref[...] += _dot(_act(a_ref[...], act), wget(w_ref))

    return pl.pallas_call(
        body, grid=(T // tm, N // tn, nk),
        in_specs=[pl.BlockSpec((tm, tk), lambda i, j, k: (i, k)),
                  pl.BlockSpec((tk, tn), lambda i, j, k: (k, j)) if wspec is None else wspec,
                  pl.BlockSpec((tm, tn), lambda i, j, k: (i, j))],
        out_specs=pl.BlockSpec((tm, tn), lambda i, j, k: (i, j)),
        out_shape=jax.ShapeDtypeStruct((T, N), F32),
        compiler_params=_cp("parallel", "parallel", "arbitrary"), name=name)(a, w, r)


def glu_fwd(ys, w, *, tm, name, comm=None):
    T, N = ys.shape

    def body(y_ref, w_ref, o_ref, t_ref):
        a = _gelu(y_ref[...].astype(F32))
        t = _dot(a.astype(BF16), w_ref[...])
        o_ref[...] = (a * _sigmoid(t)).astype(BF16)
        t_ref[...] = t.astype(BF16)

    return _call(
        body, grid=(T // tm,),
        in_specs=[pl.BlockSpec((tm, N), lambda i: (i, 0)), pl.BlockSpec((N, N), lambda i: (0, 0))],
        out_specs=[pl.BlockSpec((tm, N), lambda i: (i, 0)), pl.BlockSpec((tm, N), lambda i: (i, 0))],
        out_shape=[jax.ShapeDtypeStruct((T, 2 * N), BF16), jax.ShapeDtypeStruct((T, N), BF16)],
        sem=("parallel",), name=name, args=(ys, w), comm=comm)


def glu_bwd_pre(dout, ys, t, *, tm, name):
    T, N = ys.shape

    def body(d_ref, y_ref, t_ref, dt_ref, da_ref):
        d = d_ref[...].astype(F32)
        a = _gelu(y_ref[...].astype(F32))
        s = _sigmoid(t_ref[...].astype(F32))
        dt_ref[...] = (d * a * s * (1.0 - s)).astype(BF16)
        da_ref[...] = (d * s).astype(BF16)

    spec = pl.BlockSpec((tm, N), lambda i: (i, 0))
    return pl.pallas_call(
        body, grid=(T // tm,), in_specs=[spec, spec, spec], out_specs=[spec, spec],
        out_shape=[jax.ShapeDtypeStruct((T, N), BF16), jax.ShapeDtypeStruct((T, N), BF16)],
        compiler_params=_cp("parallel"), name=name)(dout, ys, t)


def matmul_nt(g, w, *, epi=None, epi_args=(), g2=None, w2=None, tm, tko, tn, out_dtype=BF16, name, wspec=None,
              k_out=None, comm=None, w_is_nk=False):
    T, N = g.shape
    K = k_out if wspec is not None else (w.shape[1] if w_is_nk else w.shape[0])
    wget = (lambda r_: r_[...]) if wspec is None else _pack_block
    mm = _dot if w_is_nk else _dot_nt
    nn = N // tn
    has2 = g2 is not None
    rms = epi == "rmsbwd"
    if rms:
        assert tko == K
    n_epi = len(epi_args)

    def body(*refs):
        g_ref, w_ref = refs[0], refs[1]
        pos = 2
        if has2:
            g2_ref, w2_ref = refs[2], refs[3]
            pos = 4
        e_refs = refs[pos:pos + n_epi]
        pos += n_epi
        o_ref = refs[pos]
        pos += 1
        if rms:
            dg_ref = refs[pos]
            pos += 1
        acc_ref = refs[pos]
        i = pl.program_id(0)
        n = pl.program_id(2)
        part = mm(g_ref[...].astype(BF16), wget(w_ref))

        @pl.when(n == 0)
        def _():
            acc_ref[...] = part

        @pl.when(n > 0)
        def _():
            acc_ref[...] += part

        @pl.when(n == nn - 1)
        def _():
            acc = acc_ref[...]
            if has2:
                acc = acc + mm(g2_ref[...].astype(BF16), w2_ref[...])
            if epi is None:
                o_ref[...] = acc.astype(o_ref.dtype)
            elif epi == "relu2bwd":
                h1 = e_refs[0][...].astype(F32)
                o_ref[...] = (acc * 2.0 * jnp.maximum(h1, 0.0)).astype(o_ref.dtype)
            elif epi == "glubwd":
                da1 = e_refs[0][...].astype(F32)
                ys = e_refs[1][...].astype(F32)
                o_ref[...] = ((da1 + acc) * _gelu_grad(ys)).astype(o_ref.dtype)
            elif epi == "rmsbwd":
                xv, rs, gain, rv = e_refs[0][...], e_refs[1][...], e_refs[2][...], e_refs[3][...]
                xhat = xv * rs
                gd = acc * gain
                o_ref[...] = rv + rs * (gd - xhat * jnp.mean(gd * xhat, axis=-1, keepdims=True))
                part_g = jnp.sum(acc * xhat, axis=0, keepdims=True)

                @pl.when(i == 0)
                def _():
                    dg_ref[...] = part_g

                @pl.when(i > 0)
                def _():
                    dg_ref[...] += part_g

    if wspec is None:
        wspec = (pl.BlockSpec((tn, tko), lambda i, k, n: (n, k)) if w_is_nk
                 else pl.BlockSpec((tko, tn), lambda i, k, n: (k, n)))
    in_specs = [pl.BlockSpec((tm, tn), lambda i, k, n: (i, n)), wspec]
    args = [g, w]
    if has2:
        n2 = g2.shape[1]
        in_specs += [pl.BlockSpec((tm, n2), lambda i, k, n: (i, 0)),
                     pl.BlockSpec((n2, tko), lambda i, k, n: (0, k)) if w_is_nk
                     else pl.BlockSpec((tko, n2), lambda i, k, n: (k, 0))]
        args += [g2, w2]
    if epi == "relu2bwd" or epi == "glubwd":
        in_specs += [pl.BlockSpec((tm, tko), lambda i, k, n: (i, k))] * n_epi
    elif rms:
        in_specs += [pl.BlockSpec((tm, K), lambda i, k, n: (i, 0)), pl.BlockSpec((tm, 1), lambda i, k, n: (i, 0)),
                     pl.BlockSpec((1, K), lambda i, k, n: (0, 0)), pl.BlockSpec((tm, K), lambda i, k, n: (i, 0))]
    args += list(epi_args)
    out_shape = [jax.ShapeDtypeStruct((T, K), F32 if rms else out_dtype)]
    out_specs = [pl.BlockSpec((tm, tko), lambda i, k, n: (i, k))]
    if rms:
        out_shape.append(jax.ShapeDtypeStruct((1, K), F32))
        out_specs.append(pl.BlockSpec((1, K), lambda i, k, n: (0, 0)))
    sem = ("arbitrary",) * 3 if rms else ("parallel", "parallel", "arbitrary")
    res, cres = _call(body, grid=(T // tm, K // tko, nn), in_specs=in_specs, out_specs=out_specs, out_shape=out_shape,
                      scratch_shapes=[pltpu.VMEM((tm, tko), F32)], sem=sem, name=name, args=args, comm=comm)
    res = res if rms else res[0]
    return res if comm is None else (res, cres)


def matmul_tn(a, g, *, act=None, tk, tn, tt, name, pack=None, pack_spec=None, pack_shape=None):
    T, K = a.shape
    N = g.shape[1]
    to_pack = pack_spec is not None

    def body(a_ref, g_ref, *rest):
        o_ref = rest[-1]
        t = pl.program_id(2)
        part = _dot_tn(_act(a_ref[...], act), g_ref[...].astype(BF16))
        if to_pack:
            lead = pack_spec.block_shape[0]
            part = part.reshape(lead, tk // lead, tn)

        @pl.when(t == 0)
        def _():
            o_ref[...] = part

        @pl.when(t > 0)
        def _():
            o_ref[...] += part

    in_specs = [pl.BlockSpec((tt, tk), lambda k, n, t: (t, k)), pl.BlockSpec((tt, tn), lambda k, n, t: (t, n))]
    args = [a, g]
    aliases = {}
    if pack is not None:
        in_specs.append(_ANY)
        args.append(pack)
        aliases = {2: 0}
    return pl.pallas_call(
        body, grid=(K // tk, N // tn, T // tt), in_specs=in_specs,
        out_specs=pack_spec if to_pack else pl.BlockSpec((tk, tn), lambda k, n, t: (k, n)),
        out_shape=jax.ShapeDtypeStruct(pack_shape if to_pack else (K, N), F32), input_output_aliases=aliases,
        compiler_params=_cp("parallel", "parallel", "arbitrary"), name=name)(*args)


def attn_fwd(q, kv, *, tm, name):
    T = q.shape[0]
    M = kv.shape[0]
    scale = XA_HEAD_DIM ** -0.5

    def body(q_ref, kv_ref, o_ref):
        for h in range(XA_HEADS):
            sl = slice(h * XA_HEAD_DIM, (h + 1) * XA_HEAD_DIM)
            kh = kv_ref[:, h * XA_HEAD_DIM:(h + 1) * XA_HEAD_DIM]
            vh = kv_ref[:, D_MODEL + h * XA_HEAD_DIM:D_MODEL + (h + 1) * XA_HEAD_DIM]
            s = _dot_nt(q_ref[:, sl], kh) * scale
            s = s - jnp.max(s, axis=-1, keepdims=True)
            p = jnp.exp(s)
            p = p / jnp.sum(p, axis=-1, keepdims=True)
            o_ref[:, sl] = _dot(p.astype(BF16), vh).astype(BF16)

    return pl.pallas_call(
        body, grid=(T // tm,),
        in_specs=[pl.BlockSpec((tm, D_MODEL), lambda i: (i, 0)), pl.BlockSpec((M, 2 * D_MODEL), lambda i: (0, 0))],
        out_specs=pl.BlockSpec((tm, D_MODEL), lambda i: (i, 0)),
        out_shape=jax.ShapeDtypeStruct((T, D_MODEL), BF16),
        compiler_params=_cp("parallel"), name=name)(q, kv)


def attn_bwd(q, kv, do, *, tm, name):
    T = q.shape[0]
    M = kv.shape[0]
    scale = XA_HEAD_DIM ** -0.5

    def body(q_ref, kv_ref, do_ref, dq_ref, dkv_ref):
        i = pl.program_id(0)

        @pl.when(i == 0)
        def _():
            dkv_ref[...] = jnp.zeros_like(dkv_ref)

        for h in range(XA_HEADS):
            sl = slice(h * XA_HEAD_DIM, (h + 1) * XA_HEAD_DIM)
            slv = slice(D_MODEL + h * XA_HEAD_DIM, D_MODEL + (h + 1) * XA_HEAD_DIM)
            qh = q_ref[:, sl]
            kh = kv_ref[:, sl]
            vh = kv_ref[:, slv]
            doh = do_ref[:, sl]
            s = _dot_nt(qh, kh) * scale
            s = s - jnp.max(s, axis=-1, keepdims=True)
            p = jnp.exp(s)
            p = p / jnp.sum(p, axis=-1, keepdims=True)
            pb = p.astype(BF16)
            dkv_ref[:, slv] += _dot_tn(pb, doh)
            dp = _dot_nt(doh, vh)
            ds = (p * (dp - jnp.sum(dp * p, axis=-1, keepdims=True)) * scale).astype(BF16)
            dq_ref[:, sl] = _dot(ds, kh).astype(BF16)
            dkv_ref[:, sl] += _dot_tn(ds, qh)

    spec = pl.BlockSpec((tm, D_MODEL), lambda i: (i, 0))
    kvspec = pl.BlockSpec((M, 2 * D_MODEL), lambda i: (0, 0))
    return pl.pallas_call(
        body, grid=(T // tm,), in_specs=[spec, kvspec, spec], out_specs=[spec, kvspec],
        out_shape=[jax.ShapeDtypeStruct((T, D_MODEL), BF16), jax.ShapeDtypeStruct((M, 2 * D_MODEL), F32)],
        compiler_params=_cp("arbitrary"), name=name)(q, kv, do)


def loss_head(x, g, target, *, tm, name):
    T, D = x.shape

    def body(x_ref, g_ref, t_ref, l_ref, dx_ref, dg_ref):
        i = pl.program_id(0)
        xv = x_ref[...]
        gain = g_ref[...]
        r = lax.rsqrt(jnp.mean(xv * xv, axis=-1, keepdims=True) + EPS)
        xhat = xv * r
        err = xhat * gain - t_ref[...]
        part_l = jnp.full((1, 128), 0.5 / D, F32) * jnp.sum(err * err)
        dy = err * (1.0 / D)
        gd = dy * gain
        dx_ref[...] = r * (gd - xhat * jnp.mean(gd * xhat, axis=-1, keepdims=True))
        part_g = jnp.sum(dy * xhat, axis=0, keepdims=True)

        @pl.when(i == 0)
        def _():
            l_ref[...] = part_l
            dg_ref[...] = part_g

        @pl.when(i > 0)
        def _():
            l_ref[...] += part_l
            dg_ref[...] += part_g

    spec = pl.BlockSpec((tm, D), lambda i: (i, 0))
    return pl.pallas_call(
        body, grid=(T // tm,), in_specs=[spec, pl.BlockSpec((1, D), lambda i: (0, 0)), spec],
        out_specs=[pl.BlockSpec((1, 128), lambda i: (0, 0)), spec, pl.BlockSpec((1, D), lambda i: (0, 0))],
        out_shape=[jax.ShapeDtypeStruct((1, 128), F32), jax.ShapeDtypeStruct((T, D), F32),
                   jax.ShapeDtypeStruct((1, D), F32)],
        compiler_params=_cp("arbitrary"), name=name)(x, g, target)


S5_LS = 128
S5_SEG = 8
S5_TB = S5_LS * S5_SEG
S5_CH = 8
S5_CW = 128
S5_NST = 512


def _cmul(ar, ai, br, bi):
    return ar * br - ai * bi, ar * bi + ai * br


def s5_params_fwd(a_re, a_im, log_dt, b_re, b_im, expand, *, name):
    G, P = a_re.shape

    def body(ar_ref, ai_ref, ld_ref, br_ref, bi_ref, e_ref, abr_ref, abi_ref, apr_ref, api_ref, bbr_ref, bbi_ref):
        ar, ai = ar_ref[...], ai_ref[...]
        dt = jnp.exp(ld_ref[...])
        mag = jnp.exp(dt * ar)
        abr, abi = mag * jnp.cos(dt * ai), mag * jnp.sin(dt * ai)
        den = ar * ar + ai * ai
        zr, zi = abr - 1.0, abi
        fr = (zr * ar + zi * ai) / den
        fi = (zi * ar - zr * ai) / den
        frx, fix = _dot_hi(fr, e_ref[...]), _dot_hi(fi, e_ref[...])
        br, bi = br_ref[...], bi_ref[...]
        bbr_ref[...] = frx * br - fix * bi
        bbi_ref[...] = frx * bi + fix * br
        abr_ref[...] = abr
        abi_ref[...] = abi
        pr, pi = abr, abi
        for _ in range(int(math.log2(S5_LS))):
            pr, pi = _cmul(pr, pi, pr, pi)
        apr_ref[...] = pr
        api_ref[...] = pi

    small = jax.ShapeDtypeStruct((G, P), F32)
    big = jax.ShapeDtypeStruct(b_re.shape, F32)
    return pl.pallas_call(body, out_shape=[small, small, small, small, big, big], name=name)(
        a_re, a_im, log_dt, b_re, b_im, expand)


def s5_params_bwd(a_re, a_im, log_dt, b_re, b_im, g_abr, g_abi, g_bbr, g_bbi, expand, expand_t, *, name):
    G, P = a_re.shape

    def body(ar_ref, ai_ref, ld_ref, br_ref, bi_ref, gar_ref, gai_ref, gbr_ref, gbi_ref, e_ref, et_ref,
             dar_ref, dai_ref, dld_ref, dbr_ref, dbi_ref):
        ar, ai = ar_ref[...], ai_ref[...]
        dt = jnp.exp(ld_ref[...])
        mag = jnp.exp(dt * ar)
        cs, sn = jnp.cos(dt * ai), jnp.sin(dt * ai)
        abr, abi = mag * cs, mag * sn
        den = ar * ar + ai * ai
        zr, zi = abr - 1.0, abi
        fr = (zr * ar + zi * ai) / den
        fi = (zi * ar - zr * ai) / den
        frx, fix = _dot_hi(fr, e_ref[...]), _dot_hi(fi, e_ref[...])
        br, bi = br_ref[...], bi_ref[...]
        gbr, gbi = gbr_ref[...], gbi_ref[...]
        dbr_ref[...] = frx * gbr + fix * gbi
        dbi_ref[...] = -fix * gbr + frx * gbi
        gfr = _dot_hi(br * gbr + bi * gbi, et_ref[...])
        gfi = _dot_hi(-bi * gbr + br * gbi, et_ref[...])
        g_zr = (gfr * ar - gfi * ai) / den
        g_zi = (gfr * ai + gfi * ar) / den
        g_ar = gfr * (zr - fr * 2.0 * ar) / den + gfi * (zi - fi * 2.0 * ar) / den
        g_ai = gfr * (zi - fr * 2.0 * ai) / den + gfi * (-zr - fi * 2.0 * ai) / den
        t_abr = gar_ref[...] + g_zr
        t_abi = gai_ref[...] + g_zi
        g_mag = t_abr * cs + t_abi * sn
        g_th = mag * (-t_abr * sn + t_abi * cs)
        dar_ref[...] = g_ar + g_mag * mag * dt
        dai_ref[...] = g_ai + g_th * dt
        g_dt = jnp.sum(g_mag * mag * ar + g_th * ai, axis=-1, keepdims=True)
        dld_ref[...] = g_dt * dt

    small = jax.ShapeDtypeStruct((G, P), F32)
    big = jax.ShapeDtypeStruct(b_re.shape, F32)
    return pl.pallas_call(body, out_shape=[small, small, jax.ShapeDtypeStruct((G, 1), F32), big, big], name=name)(
        a_re, a_im, log_dt, b_re, b_im, g_abr, g_abi, g_bbr, g_bbi, expand, expand_t)


def _s5_permute_in(src_ref, dst_ref):
    for i in range(S5_LS):
        dst_ref[pl.ds(8 * i, 8), :] = src_ref[pl.ds(i, 8, stride=S5_LS), :]


def _s5_permute_out(src_ref, dst_ref):
    for r in range(S5_SEG):
        for k in range(S5_LS // 8):
            dst_ref[pl.ds(r * S5_LS + 8 * k, 8), :] = src_ref[pl.ds(64 * k + r, 8, stride=8), :]


def _s5_scan(a_r, a_i, dr_ref, di_ref, init_r, init_i, store=None, reverse=False, conj=False):
    sgn = -1.0 if conj else 1.0

    def step(n, c):
        sr, si = c
        i = S5_LS - 1 - n if reverse else n
        nr = a_r * sr - sgn * a_i * si + dr_ref[i]
        ni = a_r * si + sgn * a_i * sr + di_ref[i]
        if store is not None:
            store(i, nr, ni, sr, si)
        return nr, ni

    return lax.fori_loop(0, S5_LS, step, (init_r, init_i), unroll=2)


def _s5_stitch(apr, api, fin_r, fin_i, car_r, car_i, reverse=False, conj=False):
    sgn = -1.0 if conj else 1.0
    rows_r, rows_i = [None] * S5_SEG, [None] * S5_SEG
    order = range(S5_SEG - 1, -1, -1) if reverse else range(S5_SEG)
    for r in order:
        rows_r[r], rows_i[r] = car_r, car_i
        fr, fi = fin_r[r:r + 1], fin_i[r:r + 1]
        car_r, car_i = (apr * car_r - sgn * api * car_i + fr, apr * car_i + sgn * api * car_r + fi)
    return jnp.concatenate(rows_r, 0), jnp.concatenate(rows_i, 0), car_r, car_i


def _s5_specs(nb, rev):
    blk = (lambda c, b: (nb - 1 - b, c)) if rev else (lambda c, b: (b, c))
    tok = pl.BlockSpec((S5_TB, S5_CW), blk)
    par_b = pl.BlockSpec((1, S5_CW, S5_NST), lambda c, b: (c, 0, 0))
    par_c = pl.BlockSpec((1, S5_NST, S5_CW), lambda c, b: (c, 0, 0))
    vec_s = pl.BlockSpec((1, 1, S5_NST), lambda c, b: (c, 0, 0))
    vec_c = pl.BlockSpec((1, 1, S5_CW), lambda c, b: (c, 0, 0))
    return tok, par_b, par_c, vec_s, vec_c


def s5_fwd(proj, bbr, bbi, ccr, cci, abr, abi, apr, api, dskip, *, name, comm=None):
    T = proj.shape[0]
    nb = T // S5_TB
    zeros8 = functools.partial(jnp.zeros, (S5_SEG, S5_NST), F32)

    def body(u_ref, bbr_ref, bbi_ref, ccr_ref, cci_ref, ar_ref, ai_ref, apr_ref, api_ref, d_ref,
             y_ref, csr_ref, csi_ref, uf_ref, up_ref, dr_ref, di_ref, sr_ref, si_ref, yp_ref, car_ref, cai_ref):
        b = pl.program_id(1)

        @pl.when(b == 0)
        def _():
            car_ref[...] = jnp.zeros_like(car_ref)
            cai_ref[...] = jnp.zeros_like(cai_ref)

        csr_ref[0, 0] = car_ref[...]
        csi_ref[0, 0] = cai_ref[...]
        uf_ref[...] = u_ref[...].astype(F32)
        _s5_permute_in(uf_ref, up_ref)
        upb = up_ref[...].astype(BF16)
        dr_ref[...] = _dot(upb, bbr_ref[0]).reshape(S5_LS, S5_SEG, S5_NST)
        di_ref[...] = _dot(upb, bbi_ref[0]).reshape(S5_LS, S5_SEG, S5_NST)
        a_r = jnp.broadcast_to(ar_ref[0], (S5_SEG, S5_NST))
        a_i = jnp.broadcast_to(ai_ref[0], (S5_SEG, S5_NST))
        fin_r, fin_i = _s5_scan(a_r, a_i, dr_ref, di_ref, zeros8(), zeros8())
        cin_r, cin_i, ncr, nci = _s5_stitch(apr_ref[0], api_ref[0], fin_r, fin_i, car_ref[...], cai_ref[...])
        car_ref[...] = ncr
        cai_ref[...] = nci

        def store(i, nr, ni, sr, si):
            sr_ref[i] = nr
            si_ref[i] = ni

        _s5_scan(a_r, a_i, dr_ref, di_ref, cin_r, cin_i, store=store)
        s_r = sr_ref[...].reshape(S5_TB, S5_NST).astype(BF16)
        s_i = si_ref[...].reshape(S5_TB, S5_NST).astype(BF16)
        yp_ref[...] = _dot(s_r, ccr_ref[0]) - _dot(s_i, cci_ref[0]) + d_ref[0] * up_ref[...]
        _s5_permute_out(yp_ref, uf_ref)
        y_ref[...] = uf_ref[...].astype(BF16)

    tok, par_b, par_c, vec_s, vec_c = _s5_specs(nb, False)
    cs_spec = pl.BlockSpec((1, 1, 1, S5_NST), lambda c, b: (b, c, 0, 0))
    cs_shape = jax.ShapeDtypeStruct((nb, S5_CH, 1, S5_NST), F32)
    tokbuf = pltpu.VMEM((S5_TB, S5_CW), F32)
    stbuf = pltpu.VMEM((S5_LS, S5_SEG, S5_NST), F32)
    return _call(
        body, grid=(S5_CH, nb),
        in_specs=[tok, par_b, par_b, par_c, par_c, vec_s, vec_s, vec_s, vec_s, vec_c],
        out_specs=[tok, cs_spec, cs_spec],
        out_shape=[jax.ShapeDtypeStruct((T, D_S5), BF16), cs_shape, cs_shape],
        scratch_shapes=[tokbuf, tokbuf, stbuf, stbuf, stbuf, stbuf, tokbuf,
                        pltpu.VMEM((1, S5_NST), F32), pltpu.VMEM((1, S5_NST), F32)],
        sem=("parallel", "arbitrary"), name=name, comm=comm,
        args=(proj, bbr, bbi, ccr, cci, abr, abi, apr, api, dskip))


def s5_bwd(proj, dys, csr, csi, bbr, bbi, ccr, cci, abr, abi, apr, api, dskip, dproj, *, name):
    T = proj.shape[0]
    nb = T // S5_TB
    zeros8 = functools.partial(jnp.zeros, (S5_SEG, S5_NST), F32)

    def body(u_ref, gy_ref, csr_ref, csi_ref, bbr_ref, bbi_ref, ccr_ref, cci_ref, ar_ref, ai_ref, apr_ref, api_ref,
             d_ref, dproj_ref, du_ref, dbr_ref, dbi_ref, dcr_ref, dci_ref, dd_ref, dar_ref, dai_ref,
             tmp_ref, up_ref, gyp_ref, dr_ref, di_ref, sr_ref, si_ref, gr_ref, gi_ref, car_ref, cai_ref):
        b = pl.program_id(1)

        @pl.when(b == 0)
        def _():
            car_ref[...] = jnp.zeros_like(car_ref)
            cai_ref[...] = jnp.zeros_like(cai_ref)
            for ref in (dbr_ref, dbi_ref, dcr_ref, dci_ref, dd_ref, dar_ref, dai_ref):
                ref[...] = jnp.zeros_like(ref)

        tmp_ref[...] = u_ref[...].astype(F32)
        _s5_permute_in(tmp_ref, up_ref)
        tmp_ref[...] = gy_ref[...].astype(F32)
        _s5_permute_in(tmp_ref, gyp_ref)
        upb = up_ref[...].astype(BF16)
        gyp = gyp_ref[...]
        gypb = gyp.astype(BF16)
        dr_ref[...] = _dot(upb, bbr_ref[0]).reshape(S5_LS, S5_SEG, S5_NST)
        di_ref[...] = _dot(upb, bbi_ref[0]).reshape(S5_LS, S5_SEG, S5_NST)
        a_r = jnp.broadcast_to(ar_ref[0], (S5_SEG, S5_NST))
        a_i = jnp.broadcast_to(ai_ref[0], (S5_SEG, S5_NST))
        fin_r, fin_i = _s5_scan(a_r, a_i, dr_ref, di_ref, zeros8(), zeros8())
        cin_r, cin_i, _, _ = _s5_stitch(apr_ref[0], api_ref[0], fin_r, fin_i, csr_ref[0, 0], csi_ref[0, 0])
        sr_ref[0] = cin_r
        si_ref[0] = cin_i

        def store_s(i, nr, ni, sr, si):
            sr_ref[i + 1] = nr
            si_ref[i + 1] = ni

        _s5_scan(a_r, a_i, dr_ref, di_ref, cin_r, cin_i, store=store_s)
        s_r = sr_ref[pl.ds(1, S5_LS)].reshape(S5_TB, S5_NST).astype(BF16)
        s_i = si_ref[pl.ds(1, S5_LS)].reshape(S5_TB, S5_NST).astype(BF16)
        dcr_ref[0] += _dot_tn(s_r, gypb)
        dci_ref[0] -= _dot_tn(s_i, gypb)
        dd_ref[0] += jnp.sum(gyp * up_ref[...], axis=0, keepdims=True)
        dr_ref[...] = _dot_nt(gypb, ccr_ref[0]).reshape(S5_LS, S5_SEG, S5_NST)
        di_ref[...] = (-_dot_nt(gypb, cci_ref[0])).reshape(S5_LS, S5_SEG, S5_NST)
        fin_r, fin_i = _s5_scan(a_r, a_i, dr_ref, di_ref, zeros8(), zeros8(), reverse=True, conj=True)
        gin_r, gin_i, ncr, nci = _s5_stitch(apr_ref[0], api_ref[0], fin_r, fin_i, car_ref[...], cai_ref[...],
                                            reverse=True, conj=True)
        car_ref[...] = ncr
        cai_ref[...] = nci
        def step_g(n, carry):
            gr, gi, acc_r, acc_i = carry
            i = S5_LS - 1 - n
            nr = a_r * gr + a_i * gi + dr_ref[i]
            ni = a_r * gi - a_i * gr + di_ref[i]
            gr_ref[i] = nr
            gi_ref[i] = ni
            pr, pi = sr_ref[i], si_ref[i]
            return nr, ni, acc_r + (nr * pr + ni * pi), acc_i + (ni * pr - nr * pi)

        _, _, acc_r, acc_i = lax.fori_loop(0, S5_LS, step_g, (gin_r, gin_i, zeros8(), zeros8()), unroll=2)
        dar_ref[0] += jnp.sum(acc_r, axis=0, keepdims=True)
        dai_ref[0] += jnp.sum(acc_i, axis=0, keepdims=True)
        g_r = gr_ref[...].reshape(S5_TB, S5_NST).astype(BF16)
        g_i = gi_ref[...].reshape(S5_TB, S5_NST).astype(BF16)
        dbr_ref[0] += _dot_tn(upb, g_r)
        dbi_ref[0] += _dot_tn(upb, g_i)
        gyp_ref[...] = _dot_nt(g_r, bbr_ref[0]) + _dot_nt(g_i, bbi_ref[0]) + d_ref[0] * gyp
        _s5_permute_out(gyp_ref, tmp_ref)
        du_ref[...] = tmp_ref[...].astype(BF16)

    tok, par_b, par_c, vec_s, vec_c = _s5_specs(nb, True)
    cs_spec = pl.BlockSpec((1, 1, 1, S5_NST), lambda c, b: (nb - 1 - b, c, 0, 0))
    tokbuf = pltpu.VMEM((S5_TB, S5_CW), F32)
    stbuf = pltpu.VMEM((S5_LS, S5_SEG, S5_NST), F32)
    stbuf1 = pltpu.VMEM((S5_LS + 1, S5_SEG, S5_NST), F32)
    return pl.pallas_call(
        body, grid=(S5_CH, nb),
        in_specs=[tok, tok, cs_spec, cs_spec, par_b, par_b, par_c, par_c, vec_s, vec_s, vec_s, vec_s, vec_c, _ANY],
        out_specs=[tok, par_b, par_b, par_c, par_c, vec_c, vec_s, vec_s], input_output_aliases={13: 0},
        out_shape=[jax.ShapeDtypeStruct(dproj.shape, BF16),
                   jax.ShapeDtypeStruct((S5_CH, S5_CW, S5_NST), F32), jax.ShapeDtypeStruct((S5_CH, S5_CW, S5_NST), F32),
                   jax.ShapeDtypeStruct((S5_CH, S5_NST, S5_CW), F32), jax.ShapeDtypeStruct((S5_CH, S5_NST, S5_CW), F32),
                   jax.ShapeDtypeStruct((S5_CH, 1, S5_CW), F32),
                   jax.ShapeDtypeStruct((S5_CH, 1, S5_NST), F32), jax.ShapeDtypeStruct((S5_CH, 1, S5_NST), F32)],
        scratch_shapes=[tokbuf, tokbuf, tokbuf, stbuf, stbuf, stbuf1, stbuf1, stbuf, stbuf,
                        pltpu.VMEM((1, S5_NST), F32), pltpu.VMEM((1, S5_NST), F32)],
        compiler_params=_cp("parallel", "arbitrary"), name=name)(
            proj, dys, csr, csi, bbr, bbi, ccr, cci, abr, abi, apr, api, dskip, dproj)


SSD_L = SSD_CHUNK
SSD_GW = 256
NEG = -1e30


def _expand16(v):
    lane = lax.broadcasted_iota(jnp.int32, (v.shape[0], 128), 1)
    parts = [jnp.where(lane < SSD_HEADDIM, v[:, 2 * j:2 * j + 1], v[:, 2 * j + 1:2 * j + 2]) for j in range(8)]
    return jnp.concatenate(parts, axis=1)


def _headsum(v, hsum):
    hi = v.astype(BF16)
    lo = (v - hi.astype(F32)).astype(BF16)
    return _dot(hi, hsum) + _dot(lo, hsum)


def _softplus(x):
    return jnp.maximum(x, 0.0) + jnp.log(1.0 + jnp.exp(-jnp.abs(x)))


def _ssd_chunk_fwd(z, xbc, tail, dtraw, hprev, cw, cb, dtb, alog, dsk, nw, tril):
    L = SSD_L
    f = {}
    xe = jnp.concatenate([tail, xbc], axis=0)
    sh = [xbc] + [pltpu.roll(xe, s, 0)[8:] for s in (1, 2, 3)]
    conv = cb + cw[3:4] * sh[0] + cw[2:3] * sh[1] + cw[1:2] * sh[2] + cw[0:1] * sh[3]
    sig = _sigmoid(conv)
    xa = conv * sig
    xs, bm, cm = xa[:, :D_SSD], xa[:, D_SSD:D_SSD + 512], xa[:, D_SSD + 512:]
    pre = dtraw + dtb
    dt = _softplus(pre)
    a_h = -jnp.exp(alog)
    acum = _dot_hi(tril, dt * a_h)
    acum_t = acum.T
    alast = acum[L - 1:L]
    exp_a = jnp.exp(acum)
    dec = jnp.exp(alast - acum)
    exp_al = jnp.exp(alast)
    dt_x, dec_x, exp_a_x, exp_al_x = _expand16(dt), _expand16(dec), _expand16(exp_a), _expand16(exp_al)
    d_x = _expand16(dsk)
    xh = xs * dt_x
    xhb = xh.astype(BF16)
    xd = (xh * dec_x).astype(BF16)
    row = lax.broadcasted_iota(jnp.int32, (L, L), 0)
    col = lax.broadcasted_iota(jnp.int32, (L, L), 1)
    causal = row >= col
    lane = lax.broadcasted_iota(jnp.int32, (L, 128), 1)
    low = lane < SSD_HEADDIM
    hb = hprev.astype(BF16)
    y_pairs, yoff_parts, st_parts, cbs, lms = [], [], [], [], []
    for g in range(SSD_NGROUPS):
        bg = bm[:, g * 128:(g + 1) * 128].astype(BF16)
        cg = cm[:, g * 128:(g + 1) * 128].astype(BF16)
        cbg = _dot_nt(cg, bg)
        cbs.append(cbg)
        for j in (2 * g, 2 * g + 1):
            xp = xhb[:, j * 128:(j + 1) * 128]
            ys = []
            for h in (2 * j, 2 * j + 1):
                lm = jnp.exp(jnp.where(causal, acum[:, h:h + 1] - acum_t[h:h + 1, :], NEG))
                lms.append(lm)
                ys.append(_dot((cbg * lm).astype(BF16), xp))
            y_pairs.append(jnp.where(low, ys[0], ys[1]))
        gs = slice(g * SSD_GW, (g + 1) * SSD_GW)
        yoff_parts.append(_dot(cg, hb[:, gs]) * exp_a_x[:, gs])
        st_parts.append(_dot_tn(bg, xd[:, gs]))
    yoff = jnp.concatenate(yoff_parts, axis=1)
    y = jnp.concatenate(y_pairs, axis=1) + yoff + d_x * xs
    hnew = exp_al_x * hprev + jnp.concatenate(st_parts, axis=1)
    sz = _sigmoid(z)
    gz = y * (z * sz)
    r = lax.rsqrt(jnp.mean(gz * gz, axis=-1, keepdims=True) + EPS)
    out = gz * r * nw
    f.update(sh=sh, conv=conv, sig=sig, xs=xs, bm=bm, cm=cm, pre=pre, dt=dt, a_h=a_h, exp_a_x=exp_a_x, dec_x=dec_x,
             exp_al=exp_al, exp_al_x=exp_al_x, dt_x=dt_x, d_x=d_x, xh=xh, xhb=xhb, xd=xd, causal=causal, low=low, hb=hb,
             cbs=cbs, lms=lms, yoff=yoff, y=y, sz=sz, gz=gz, r=r)
    return out, hnew, f


def _ssd_params(conv_w, conv_b, dt_bias, a_log, d_skip, norm_w):
    pad16 = lambda v: jnp.pad(v.reshape(1, SSD_HEADS), ((0, 0), (0, 128 - SSD_HEADS)))
    return (jnp.pad(conv_w, ((0, 8 - SSD_CONV), (0, 0))), conv_b.reshape(1, D_XBC), pad16(dt_bias), pad16(a_log),
            pad16(d_skip), norm_w.reshape(1, D_SSD))


def _ssd_param_specs():
    full = lambda shape: pl.BlockSpec(shape, lambda i: (0, 0))
    return [full((8, D_XBC)), full((1, D_XBC)), full((1, 128)), full((1, 128)), full((1, 128)), full((1, D_SSD))]


def ssd_fwd(proj, dtraw, params, tril, ymix, *, name, comm=None):
    T = proj.shape[0]
    nc = T // SSD_L

    def body(z_ref, x_ref, dt_ref, cw_ref, cb_ref, dtb_ref, al_ref, dsk_ref, nw_ref, tril_ref, ymix_ref,
             o_ref, hs_ref, h_ref, tail_ref):
        i = pl.program_id(0)

        @pl.when(i == 0)
        def _():
            h_ref[...] = jnp.zeros_like(h_ref)
            tail_ref[...] = jnp.zeros_like(tail_ref)

        xbc = x_ref[...].astype(F32)
        hprev = h_ref[...]
        hs_ref[0] = hprev
        out, hnew, _ = _ssd_chunk_fwd(z_ref[...].astype(F32), xbc, tail_ref[...], dt_ref[...], hprev, cw_ref[...],
                                      cb_ref[...], dtb_ref[...], al_ref[...], dsk_ref[...], nw_ref[...], tril_ref[...])
        o_ref[...] = out.astype(BF16)
        h_ref[...] = hnew
        tail_ref[...] = xbc[SSD_L - 8:]

    return _call(
        body, grid=(nc,),
        in_specs=[pl.BlockSpec((SSD_L, D_SSD), lambda i: (i, 1)), pl.BlockSpec((SSD_L, D_XBC), lambda i: (i, 1)),
                  pl.BlockSpec((SSD_L, 128), lambda i: (i, 0))] + _ssd_param_specs()
                 + [pl.BlockSpec((SSD_L, SSD_L), lambda i: (0, 0)), _ANY],
        out_specs=[pl.BlockSpec((SSD_L, D_SSD), lambda i: (i, 1)),
                   pl.BlockSpec((1, SSD_STATE, D_SSD), lambda i: (i, 0, 0))],
        out_shape=[jax.ShapeDtypeStruct(ymix.shape, BF16), jax.ShapeDtypeStruct((nc, SSD_STATE, D_SSD), F32)],
        scratch_shapes=[pltpu.VMEM((SSD_STATE, D_SSD), F32), pltpu.VMEM((8, D_XBC), F32)],
        sem=("arbitrary",), name=name, args=(proj, proj, dtraw, *params, tril, ymix), aliases={10: 0}, comm=comm)


def ssd_bwd(proj, dtraw, hs, dymix, params, tril, triu, trils, headsum, *, name, comm=None):
    T = proj.shape[0]
    nc = T // SSD_L
    L = SSD_L

    def body(z_ref, x_ref, xprev_ref, dt_ref, hs_ref, do_ref, cw_ref, cb_ref, dtb_ref, al_ref, dsk_ref, nw_ref,
             tril_ref, triu_ref, trils_ref, hsum_ref,
             dp_ref, ddt_ref, dcw_ref, dcb_ref, ddtb_ref, dal_ref, ddsk_ref, dnw_ref, dh_ref, dnext_ref):
        i = pl.program_id(0)

        @pl.when(i == 0)
        def _():
            dh_ref[...] = jnp.zeros_like(dh_ref)
            dnext_ref[...] = jnp.zeros_like(dnext_ref)
            for ref in (dcw_ref, dcb_ref, ddtb_ref, dal_ref, ddsk_ref, dnw_ref):
                ref[...] = jnp.zeros_like(ref)

        z = z_ref[...].astype(F32)
        xbc = x_ref[...].astype(F32)
        tail = jnp.where(i == nc - 1, 0.0, xprev_ref[...].astype(F32))
        hprev = hs_ref[0]
        cw, nw = cw_ref[...], nw_ref[...]
        hsum = hsum_ref[...]
        _, _, f = _ssd_chunk_fwd(z, xbc, tail, dt_ref[...], hprev, cw, cb_ref[...], dtb_ref[...], al_ref[...],
                                 dsk_ref[...], nw, tril_ref[...])
        dout = do_ref[...].astype(F32)
        dh = dh_ref[...]
        ghat = f["gz"] * f["r"]
        dn = dout * nw
        dgz = f["r"] * (dn - ghat * jnp.mean(dn * ghat, axis=-1, keepdims=True))
        dnw_ref[...] += jnp.sum(dout * ghat, axis=0, keepdims=True)
        sz = f["sz"]
        dy = dgz * (z * sz)
        dp_ref[:, :D_S5] = jnp.zeros((L, D_S5), BF16)
        dp_ref[:, D_S5:D_S5 + D_SSD] = (dgz * f["y"] * sz * (1.0 + z * (1.0 - sz))).astype(BF16)
        xs = f["xs"]
        ddsk_ref[...] += jnp.sum(_headsum(dy * xs, hsum), axis=0, keepdims=True)
        dyb = dy.astype(BF16)
        dye = (dy * f["exp_a_x"]).astype(BF16)
        dhb = dh.astype(BF16)
        lane = lax.broadcasted_iota(jnp.int32, (L, 128), 1)
        sub = lax.broadcasted_iota(jnp.int32, (128, L), 0)
        zero_b = jnp.zeros((L, 128), BF16)
        rsum = jnp.zeros((L, 128), F32)
        csum_t = jnp.zeros((128, L), F32)
        dx_pairs, dxst_parts, db_parts, dc_parts, dhp_parts = [], [], [], [], []
        for g in range(SSD_NGROUPS):
            gs = slice(g * SSD_GW, (g + 1) * SSD_GW)
            bg = f["bm"][:, g * 128:(g + 1) * 128].astype(BF16)
            cg = f["cm"][:, g * 128:(g + 1) * 128].astype(BF16)
            cbg = f["cbs"][g]
            dcb_g = jnp.zeros((L, L), F32)
            for j in (2 * g, 2 * g + 1):
                xp = f["xhb"][:, j * 128:(j + 1) * 128]
                dyp = dyb[:, j * 128:(j + 1) * 128]
                dxs = []
                for half, h in enumerate((2 * j, 2 * j + 1)):
                    lm = f["lms"][h]
                    dyh = jnp.where(f["low"], dyp, zero_b) if half == 0 else jnp.where(f["low"], zero_b, dyp)
                    dw = jnp.where(f["causal"], _dot_nt(dyh, xp), 0.0)
                    w = cbg * lm
                    e = dw * w
                    dcb_g = dcb_g + dw * lm
                    rsum = jnp.where(lane == h, jnp.sum(e, axis=1, keepdims=True), rsum)
                    csum_t = jnp.where(sub == h, jnp.sum(e, axis=0, keepdims=True), csum_t)
                    dxs.append(_dot_tn(w.astype(BF16), dyp))
                dx_pairs.append(jnp.where(f["low"], dxs[0], dxs[1]))
            dcbb = dcb_g.astype(BF16)
            dxst_parts.append(f["dec_x"][:, gs] * _dot(bg, dhb[:, gs]))
            dc_parts.append(_dot(dcbb, bg) + _dot_nt(dye[:, gs], f["hb"][:, gs]))
            db_parts.append(_dot_tn(dcbb, cg) + _dot_nt(f["xd"][:, gs], dhb[:, gs]))
            dhp_parts.append(f["exp_al_x"][:, gs] * dh[:, gs] + _dot_tn(cg, dye[:, gs]))
        dxst = jnp.concatenate(dxst_parts, axis=1)
        dxh = jnp.concatenate(dx_pairs, axis=1) + dxst
        q = _headsum(f["yoff"] * dy, hsum)
        dstate = _headsum(f["xh"] * dxst, hsum)
        h0t = jnp.sum(_headsum(dh * hprev, hsum), axis=0, keepdims=True) * f["exp_al"]
        da = _dot_hi(triu_ref[...], rsum - csum_t.T + q) + _dot_hi(trils_ref[...], dstate) + h0t
        dt, a_h = f["dt"], f["a_h"]
        ddt = _headsum(dxh * xs, hsum) + da * a_h
        dal_ref[...] += jnp.sum(da * dt, axis=0, keepdims=True) * a_h
        ddtraw = ddt * _sigmoid(f["pre"])
        first16 = lane < SSD_HEADS
        ddtraw = jnp.where(first16, ddtraw, 0.0)
        ddt_ref[...] = ddtraw
        ddtb_ref[...] += jnp.sum(ddtraw, axis=0, keepdims=True)
        dh_ref[...] = jnp.concatenate(dhp_parts, axis=1)
        dxa = jnp.concatenate([dxh * f["dt_x"] + f["d_x"] * dy] + db_parts + dc_parts, axis=1)
        sig, conv = f["sig"], f["conv"]
        dconv = dxa * sig * (1.0 + conv * (1.0 - sig))
        dcb_ref[...] += jnp.sum(dconv, axis=0, keepdims=True)
        rows = [jnp.sum(dconv * f["sh"][3 - k], axis=0, keepdims=True) for k in range(SSD_CONV)]
        dcw_ref[...] += jnp.concatenate(rows + [jnp.zeros((8 - SSD_CONV, D_XBC), F32)], axis=0)
        de = jnp.concatenate([dconv, dnext_ref[...]], axis=0)
        dxbc = cw[3:4] * dconv
        for s in (1, 2, 3):
            dxbc = dxbc + cw[3 - s:4 - s] * pltpu.roll(de, L + 8 - s, 0)[:L]
        dp_ref[:, D_S5 + D_SSD:] = dxbc.astype(BF16)
        dnext_ref[...] = dconv[:8]

    rev = lambda i: nc - 1 - i
    acc = lambda shape: pl.BlockSpec(shape, lambda i: (0, 0))
    tri = pl.BlockSpec((L, L), lambda i: (0, 0))
    return _call(
        body, grid=(nc,),
        in_specs=[pl.BlockSpec((L, D_SSD), lambda i: (rev(i), 1)), pl.BlockSpec((L, D_XBC), lambda i: (rev(i), 1)),
                  pl.BlockSpec((8, D_XBC), lambda i: (jnp.maximum(rev(i) * (L // 8) - 1, 0), 1)),
                  pl.BlockSpec((L, 128), lambda i: (rev(i), 0)),
                  pl.BlockSpec((1, SSD_STATE, D_SSD), lambda i: (rev(i), 0, 0)),
                  pl.BlockSpec((L, D_SSD), lambda i: (rev(i), 1))] + _ssd_param_specs()
                 + [tri, tri, tri, pl.BlockSpec((D_SSD, 128), lambda i: (0, 0))],
        out_specs=[pl.BlockSpec((L, D_MAIN), lambda i: (rev(i), 0)), pl.BlockSpec((L, 128), lambda i: (rev(i), 0)),
                   acc((8, D_XBC)), acc((1, D_XBC)), acc((1, 128)), acc((1, 128)), acc((1, 128)), acc((1, D_SSD))],
        out_shape=[jax.ShapeDtypeStruct((T, D_MAIN), BF16),
                   jax.ShapeDtypeStruct((T, 128), F32), jax.ShapeDtypeStruct((8, D_XBC), F32),
                   jax.ShapeDtypeStruct((1, D_XBC), F32), jax.ShapeDtypeStruct((1, 128), F32),
                   jax.ShapeDtypeStruct((1, 128), F32), jax.ShapeDtypeStruct((1, 128), F32),
                   jax.ShapeDtypeStruct((1, D_SSD), F32)],
        scratch_shapes=[pltpu.VMEM((SSD_STATE, D_SSD), F32), pltpu.VMEM((8, D_XBC), F32)],
        sem=("arbitrary",), name=name, comm=comm,
        args=(proj, proj, proj, dtraw, hs, dymix, *params, tril, triu, trils, headsum))


def _s5_blockdiag(v, rows_per_group, cols_per_group):
    eye = jnp.eye(S5_SEG, dtype=v.dtype)
    w = v[:, :, :, None, :] * eye[None, :, None, :, None]
    return w.reshape(S5_CH, 8 * rows_per_group, 8 * cols_per_group)


def _s5_blockdiag_extract(w, rows_per_group, cols_per_group):
    eye = jnp.eye(S5_SEG, dtype=w.dtype)
    w5 = w.reshape(S5_CH, 8, rows_per_group, 8, cols_per_group)
    return jnp.sum(w5 * eye[None, :, None, :, None], axis=3)


TM = 512

OFF = dict(mlp_w2=0, mlp_w1=1024, w_out=2048, s5_w_glu=2560, xa_wq=2816, xa_wk=3072, xa_wv=3328, xa_wo=3584, w_in=3840)
ROWS = dict(mlp_w2=1024, mlp_w1=1024, w_out=512, s5_w_glu=256, xa_wq=256, xa_wk=256, xa_wv=256, xa_wo=256, w_in=1028)
TAIL_OFF = 4872
TAIL_ROWS = 120
PACK_ROWS = TAIL_OFF + TAIL_ROWS
HALF_ROWS = PACK_ROWS // 2
SUM_ROWS = 832
N_SHARD = 4
SMALL_L = ("norm_mix", "s5_a_re", "s5_a_im", "s5_log_dt", "s5_b_re", "s5_b_im", "s5_c_re", "s5_c_im", "s5_d",
           "ssd_conv_w", "ssd_conv_b", "ssd_dt_bias", "ssd_a_log", "ssd_d", "ssd_norm", "norm_xattn", "norm_mem",
           "norm_mlp")
SMALL_Q = 72
CONV_ROWS = 8


def _place():
    x, y, c = lax.axis_index("x"), lax.axis_index("y"), lax.axis_index("c")
    chips = [(1 - x, y), (x, 1 - y), (1 - x, 1 - y)]
    return x, y, c, 2 * x + y, chips, (x, y, 1 - c)


def _remote(src, dst, send_sem, recv_sem, to):
    return pltpu.make_async_remote_copy(src_ref=src, dst_ref=dst, send_sem=send_sem, recv_sem=recv_sem,
                                        device_id=to, device_id_type=MESH_T)


def _half(ref, c):
    return ref.at[pl.ds(pl.multiple_of(c * HALF_ROWS, 16), HALF_ROWS), :]


def _dma_sems(*counts):
    return [pltpu.SemaphoreType.DMA((n,)) for n in counts]


def place_own(wpack, shard_idx):
    tile = PACK_ROWS // 4

    def body(s_ref, w_ref, o_ref):
        o_ref[0] = w_ref[...]

    return pl.pallas_call(
        body,
        grid_spec=pltpu.PrefetchScalarGridSpec(
            num_scalar_prefetch=1, grid=(4,),
            in_specs=[pl.BlockSpec((tile, D_MODEL), lambda i, s_ref: (i, 0))],
            out_specs=pl.BlockSpec((1, tile, D_MODEL), lambda i, s_ref: (s_ref[0], i, 0))),
        out_shape=jax.ShapeDtypeStruct((N_SHARD,) + wpack.shape, wpack.dtype),
        compiler_params=_cp("parallel"), name="place_own")(shard_idx, wpack)


def gather_over_ici(wpack, placed):
    def copies(ci, co, cs):
        w_ref, (out_ref,), (send, recv) = ci[0], co, cs
        x, y, c, s, chips, sibling = _place()
        sends = [_remote(_half(w_ref, c), _half(out_ref.at[s], c), send.at[j], recv.at[j], (*chip, c))
                 for j, chip in enumerate(chips)]
        lands = [_half(out_ref.at[2 * chip[0] + chip[1]], c) for chip in chips]
        arrivals = [_remote(land, land, send.at[j], recv.at[j], sibling) for j, land in enumerate(lands)]
        return sends, arrivals

    def start(ci, co, cs):
        for cp in copies(ci, co, cs)[0]:
            cp.start()

    def wait(ci, co, cs):
        sends, arrivals = copies(ci, co, cs)
        for cp in arrivals:
            cp.wait_recv()
        for cp in sends:
            cp.wait_send()

    return Comm("gather_ici", [wpack, placed], [jax.ShapeDtypeStruct(placed.shape, placed.dtype)], _dma_sems(3, 3),
                start, wait, aliases={1: 0})


def gather_to_sibling(landed):
    def copies(co, cs):
        (out_ref,), (send, recv) = co, cs
        x, y, c, s, chips, sibling = _place()
        slots = [out_ref.at[2 * chip[0] + chip[1]] for chip in chips]
        sends = [_remote(_half(slot, c), _half(slot, c), send.at[j], recv.at[j], sibling) for j, slot in enumerate(slots)]
        arrivals = [_remote(_half(slot, 1 - c), _half(slot, 1 - c), send.at[j], recv.at[j], sibling)
                    for j, slot in enumerate(slots)]
        return sends, arrivals

    def start(ci, co, cs):
        for cp in copies(co, cs)[0]:
            cp.start()

    def wait(ci, co, cs):
        sends, arrivals = copies(co, cs)
        for cp in arrivals:
            cp.wait_recv()
        for cp in sends:
            cp.wait_send()

    return Comm("gather_d2d", [landed], [jax.ShapeDtypeStruct(landed.shape, landed.dtype)], _dma_sems(3, 3), start, wait,
                aliases={0: 0})


def exchange_halves(gpack):
    def copy(ci, co, cs):
        (g_ref,), (out_ref,), (send, recv) = ci, co, cs
        x, y, c, s, chips, sibling = _place()
        src = g_ref.at[:, pl.ds(pl.multiple_of((1 - c) * HALF_ROWS, 16), HALF_ROWS), :]
        return _remote(src, out_ref, send.at[0], recv.at[0], sibling)

    return Comm("exchange", [gpack], [jax.ShapeDtypeStruct((N_SHARD, HALF_ROWS, D_MODEL), F32)], _dma_sems(1, 1),
                lambda ci, co, cs: copy(ci, co, cs).start(), lambda ci, co, cs: copy(ci, co, cs).wait())


def scatter_chips(csum):
    def copies(ci, co, cs):
        (c_ref,), (out_ref,), (send, recv) = ci, co, cs
        x, y, c, s, chips, sibling = _place()
        sends = [_remote(c_ref.at[2 * chip[0] + chip[1]], out_ref.at[s], send.at[j], recv.at[j], (*chip, c))
                 for j, chip in enumerate(chips)]
        arrivals = [_remote(c_ref.at[2 * chip[0] + chip[1]], out_ref.at[2 * chip[0] + chip[1]], send.at[j], recv.at[j],
                            (*chip, c)) for j, chip in enumerate(chips)]
        return sends, arrivals

    def start(ci, co, cs):
        for cp in copies(ci, co, cs)[0]:
            cp.start()

    def wait(ci, co, cs):
        sends, arrivals = copies(ci, co, cs)
        for cp in arrivals:
            cp.wait_recv()
        for cp in sends:
            cp.wait_send()

    return Comm("scatter", [csum], [jax.ShapeDtypeStruct(csum.shape, csum.dtype)], _dma_sems(3, 3), start, wait)


def share_reduced(gshards, smalls, l):
    def copies(ci, co, cs):
        (g_ref, sm_ref), (send, recv, loc) = co, cs
        x, y, c, s, chips, sibling = _place()
        my_half = _half(g_ref.at[l], c)
        tail = g_ref.at[l, pl.ds(PACK_ROWS - TAIL_ROWS, TAIL_ROWS), :]
        big = _remote(my_half, my_half, send.at[0], recv.at[0], sibling)
        keep_tail = pltpu.make_async_copy(tail, sm_ref.at[l, s], loc.at[0])
        tails = [_remote(tail, sm_ref.at[l, s], send.at[1 + j], recv.at[1 + j], (*chip, 1)) for j, chip in enumerate(chips)]
        tails += [_remote(tail, sm_ref.at[l, s], send.at[4 + j], recv.at[1 + j], (*chip, 0)) for j, chip in enumerate(chips)]
        tails.append(_remote(tail, sm_ref.at[l, s], send.at[7], recv.at[4], sibling))
        other = _half(g_ref.at[l], 1 - c)
        big_in = _remote(other, other, send.at[0], recv.at[0], sibling)
        slots = [sm_ref.at[l, 2 * chip[0] + chip[1]] for chip in chips]
        tails_in = [_remote(slot, slot, send.at[1 + j], recv.at[1 + j], sibling) for j, slot in enumerate(slots)]
        sib_tail_in = _remote(sm_ref.at[l, s], sm_ref.at[l, s], send.at[7], recv.at[4], sibling)
        return c, big, keep_tail, tails, big_in, tails_in, sib_tail_in

    def start(ci, co, cs):
        c, big, keep_tail, tails, _, _, _ = copies(ci, co, cs)
        big.start()

        @pl.when(c == 1)
        def _():
            keep_tail.start()
            for cp in tails:
                cp.start()

    def wait(ci, co, cs):
        c, big, keep_tail, tails, big_in, tails_in, sib_tail_in = copies(ci, co, cs)
        big_in.wait_recv()
        for cp in tails_in:
            cp.wait_recv()
        big.wait_send()

        @pl.when(c == 0)
        def _():
            sib_tail_in.wait_recv()

        @pl.when(c == 1)
        def _():
            for cp in tails:
                cp.wait_send()
            keep_tail.wait()

    sds = lambda a: jax.ShapeDtypeStruct(a.shape, a.dtype)
    return Comm("share", [gshards, smalls], [sds(gshards), sds(smalls)], _dma_sems(8, 5, 1), start, wait,
                aliases={0: 0, 1: 1})


def _consts():
    e = np.zeros((S5_STATE, S5_STATE * S5_GROUP), np.float32)
    for p in range(S5_STATE):
        e[p, p * S5_GROUP:(p + 1) * S5_GROUP] = 1.0
    hs = np.zeros((D_SSD, 128), np.float32)
    for h in range(SSD_HEADS):
        hs[h * SSD_HEADDIM:(h + 1) * SSD_HEADDIM, h] = 1.0
    ones = np.ones((SSD_L, SSD_L), np.float32)
    return dict(expand=jnp.asarray(e), expand_t=jnp.asarray(e.T), headsum=jnp.asarray(hs).astype(BF16),
                tril=jnp.asarray(np.tril(ones)), triu=jnp.asarray(np.triu(ones)), trils=jnp.asarray(np.tril(ones, -1)))


def _s5_mats(w, cst):
    b_re = w["s5_b_re"].reshape(S5_GROUPS, S5_STATE * S5_GROUP)
    b_im = w["s5_b_im"].reshape(S5_GROUPS, S5_STATE * S5_GROUP)
    abr, abi, apr, api, bbr, bbi = s5_params_fwd(w["s5_a_re"], w["s5_a_im"], w["s5_log_dt"].reshape(S5_GROUPS, 1),
                                                 b_re, b_im, cst["expand"], name="s5_params_fwd")
    t = lambda v: v.reshape(S5_CH, 8, S5_STATE, S5_GROUP).transpose(0, 1, 3, 2)
    c4 = lambda v: v.reshape(S5_CH, 8, S5_GROUP, S5_STATE).transpose(0, 1, 3, 2)
    vec = lambda v: v.reshape(S5_CH, 1, S5_NST)
    return dict(bbr=_s5_blockdiag(t(bbr), S5_GROUP, S5_STATE).astype(BF16),
                bbi=_s5_blockdiag(t(bbi), S5_GROUP, S5_STATE).astype(BF16),
                ccr=_s5_blockdiag(c4(w["s5_c_re"]), S5_STATE, S5_GROUP).astype(BF16),
                cci=_s5_blockdiag(c4(w["s5_c_im"]), S5_STATE, S5_GROUP).astype(BF16),
                abr=vec(abr), abi=vec(abi), apr=vec(apr), api=vec(api),
                dsk=w["s5_d"].reshape(S5_CH, 1, S5_CW), b_re=b_re, b_im=b_im)


def _layer_fwd(x, mem, w, cst, next_pack=None):
    sv = {}
    g = lambda n: w[n].reshape(1, -1)
    pack = w["pack"]
    proj, h0, r0, dtraw = norm_matmul(x, g("norm_mix"), w["w_in_t"], w["w_dt_t"], tm=TM, tn=1024, name="in_proj",
                                      w_transposed=True)
    s5m = _s5_mats(w, cst)
    (ys, csr, csi), landed = s5_fwd(proj, s5m["bbr"], s5m["bbi"], s5m["ccr"], s5m["cci"], s5m["abr"], s5m["abi"],
                                    s5m["apr"], s5m["api"], s5m["dsk"], name="s5_fwd",
                                    comm=None if next_pack is None else gather_over_ici(*next_pack))
    (ymix, tglu), gathered = glu_fwd(ys, w["s5_w_glu"], tm=TM, name="glu_fwd",
                                     comm=None if next_pack is None else gather_to_sibling(landed[0]))
    ssdp = _ssd_params(w["ssd_conv_w"], w["ssd_conv_b"], w["ssd_dt_bias"], w["ssd_a_log"], w["ssd_d"], w["ssd_norm"])
    (ymix, hs), _ = ssd_fwd(proj, dtraw, ssdp, cst["tril"], ymix, name="ssd_fwd")
    x1 = matmul_res(ymix, pack, x, tm=TM, tn=1024, tk=1024, name="out_proj",
                    wspec=pl.BlockSpec((2, 512, 1024), lambda i, j, k: (k, OFF["w_out"] // 512, j)))
    q, h1, r1 = norm_matmul(x1, g("norm_xattn"), w["xa_wq"], tm=TM, tn=1024, name="q_proj")
    kv, hm, rm = norm_matmul(mem, g("norm_mem"), w["xa_wkv"], tm=mem.shape[0], tn=1024, name="kv_proj")
    o = attn_fwd(q, kv, tm=TM, name="attn_fwd")
    x2 = matmul_res(o, w["xa_wo"], x1, tm=TM, tn=1024, tk=1024, name="attn_out")
    f1, h2, r2 = norm_matmul(x2, g("norm_mlp"), pack, tm=TM, tn=1024, name="mlp_up", n_out=D_FF,
                             wspec=pl.BlockSpec((1, D_MODEL, 1024), lambda i, j: (j, OFF["mlp_w1"] // 1024, 0)))
    x3 = matmul_res(f1, pack, x2, act="relu2", tm=TM, tn=1024, tk=1024, name="mlp_down",
                    wspec=pl.BlockSpec((1, 1024, 1024), lambda i, j, k: (k, OFF["mlp_w2"] // 1024, j)))
    sv.update(x=x, proj=proj, h0=h0, r0=r0, dtraw=dtraw, s5m=s5m, ys=ys, csr=csr, csi=csi, tglu=tglu, ssdp=ssdp,
              hs=hs, ymix=ymix, x1=x1, q=q, h1=h1, r1=r1, kv=kv, hm=hm, rm=rm, o=o, x2=x2, f1=f1, h2=h2, r2=r2)
    return x3, sv, (gathered[0] if next_pack is not None else None)


def _layer_bwd(dx3, mem, w, sv, cst, extra_small=None, reduce_hooks=None):
    gr = {}
    g = lambda n: w[n].reshape(1, -1)
    pack = w["pack"]
    pshape = (N_SHARD, PACK_ROWS, D_MODEL)
    ps = lambda rows, f: pl.BlockSpec((1, rows, 1024), f)
    ps4 = lambda rows, f: pl.BlockSpec((N_SHARD, rows, 1024), f)
    dh1 = matmul_nt(dx3, pack, epi="relu2bwd", epi_args=(sv["f1"],), tm=TM, tko=1024, tn=1024, name="mlp_down_dx",
                    wspec=ps(1024, lambda i, k, n: (k, OFF["mlp_w2"] // 1024, n)), k_out=D_FF,
                    comm=reduce_hooks.exchange() if reduce_hooks else None)
    if reduce_hooks:
        dh1, recv = dh1
        reduce_hooks.after_exchange(recv[0])
    gp = matmul_tn(sv["f1"], dx3, act="relu2", tk=1024, tn=1024, tt=TM, name="mlp_down_dw", pack_shape=pshape,
                   pack_spec=ps(1024, lambda k, n, t: (k, OFF["mlp_w2"] // 1024, 0)))
    gp = matmul_tn(sv["h2"], dh1, tk=1024, tn=1024, tt=TM, name="mlp_up_dw", pack=gp, pack_shape=pshape,
                   pack_spec=ps(1024, lambda k, n, t: (n, OFF["mlp_w1"] // 1024, 0)))
    dx2, gr["norm_mlp"] = matmul_nt(dh1, pack, epi="rmsbwd", epi_args=(sv["x2"], sv["r2"], g("norm_mlp"), dx3),
                                    tm=TM, tko=1024, tn=1024, name="mlp_up_dx", k_out=D_MODEL,
                                    wspec=ps(1024, lambda i, k, n: (n, OFF["mlp_w1"] // 1024, 0)))
    do = matmul_nt(dx2, w["xa_wo"], tm=TM, tko=1024, tn=1024, name="attn_out_dx")
    gp = matmul_tn(sv["o"], dx2, tk=1024, tn=1024, tt=TM, name="attn_out_dw", pack=gp, pack_shape=pshape,
                   pack_spec=ps4(256, lambda k, n, t: (0, OFF["xa_wo"] // 256, 0)))
    dq, dkv = attn_bwd(sv["q"], sv["kv"], do, tm=TM, name="attn_bwd")
    gp = matmul_tn(sv["h1"], dq, tk=1024, tn=1024, tt=TM, name="q_proj_dw", pack=gp, pack_shape=pshape,
                   pack_spec=ps4(256, lambda k, n, t: (0, OFF["xa_wq"] // 256, 0)))
    dx1, gr["norm_xattn"] = matmul_nt(dq, w["xa_wq"], epi="rmsbwd", epi_args=(sv["x1"], sv["r1"], g("norm_xattn"), dx2),
                                      tm=TM, tko=1024, tn=1024, name="q_proj_dx")
    M = mem.shape[0]
    gp = matmul_tn(sv["hm"], dkv, tk=1024, tn=1024, tt=M, name="kv_proj_dw", pack=gp, pack_shape=pshape,
                   pack_spec=ps4(256, lambda k, n, t: (0, OFF["xa_wk"] // 256 + n, 0)))
    _, gr["norm_mem"] = matmul_nt(dkv, w["xa_wkv"], epi="rmsbwd",
                                  epi_args=(mem, sv["rm"], g("norm_mem"), jnp.zeros_like(mem)),
                                  tm=M, tko=1024, tn=1024, name="kv_proj_dx")
    dymix = matmul_nt(dx1, pack, tm=TM, tko=1024, tn=1024, name="out_proj_dx", k_out=2 * D_MODEL,
                      wspec=pl.BlockSpec((2, 512, 1024), lambda i, k, n: (k, OFF["w_out"] // 512, n)))
    gp = matmul_tn(sv["ymix"], dx1, tk=2048, tn=1024, tt=TM, name="out_proj_dw", pack=gp, pack_shape=pshape,
                   pack_spec=ps4(512, lambda k, n, t: (0, OFF["w_out"] // 512, 0)))
    (dproj, ddtraw, dcw, dcb, ddtb, dal, ddsk, dnw), parts = ssd_bwd(
        sv["proj"], sv["dtraw"], sv["hs"], dymix, sv["ssdp"], cst["tril"], cst["triu"], cst["trils"], cst["headsum"],
        name="ssd_bwd", comm=reduce_hooks.scatter() if reduce_hooks else None)
    if reduce_hooks:
        reduce_hooks.after_scatter(parts[0])
    gr["ssd_conv_w"], gr["ssd_conv_b"] = dcw[:SSD_CONV], dcb[0]
    gr["ssd_dt_bias"], gr["ssd_a_log"], gr["ssd_d"] = ddtb[0, :SSD_HEADS], dal[0, :SSD_HEADS], ddsk[0, :SSD_HEADS]
    gr["ssd_norm"] = dnw[0]
    dtg, da1 = glu_bwd_pre(dymix, sv["ys"], sv["tglu"], tm=TM, name="glu_bwd_pre")
    gp = matmul_tn(sv["ys"], dtg, act="gelu", tk=1024, tn=1024, tt=TM, name="glu_dw", pack=gp, pack_shape=pshape,
                   pack_spec=ps4(256, lambda k, n, t: (0, OFF["s5_w_glu"] // 256, 0)))
    dys = matmul_nt(dtg, w["s5_w_glu"], epi="glubwd", epi_args=(da1, sv["ys"]), tm=TM, tko=1024, tn=1024,
                    name="glu_dx", comm=reduce_hooks.share() if reduce_hooks else None)
    if reduce_hooks:
        dys, shared = dys
        reduce_hooks.after_share(shared)
    s5m = sv["s5m"]
    dproj, dbbr, dbbi, dccr, dcci, dd, dabr, dabi = s5_bwd(
        sv["proj"], dys, sv["csr"], sv["csi"], s5m["bbr"], s5m["bbi"], s5m["ccr"], s5m["cci"], s5m["abr"], s5m["abi"],
        s5m["apr"], s5m["api"], s5m["dsk"], dproj, name="s5_bwd")
    tb = lambda v: _s5_blockdiag_extract(v, S5_GROUP, S5_STATE).transpose(0, 1, 3, 2).reshape(S5_GROUPS, -1)
    tc = lambda v: _s5_blockdiag_extract(v, S5_STATE, S5_GROUP).transpose(0, 1, 3, 2).reshape(S5_GROUPS, S5_GROUP, S5_STATE)
    gr["s5_c_re"], gr["s5_c_im"] = tc(dccr), tc(dcci)
    gr["s5_d"] = dd.reshape(S5_GROUPS, S5_GROUP)
    dar, dai, dld, dbr, dbi = s5_params_bwd(
        w["s5_a_re"], w["s5_a_im"], w["s5_log_dt"].reshape(S5_GROUPS, 1), s5m["b_re"], s5m["b_im"],
        dabr.reshape(S5_GROUPS, S5_STATE), dabi.reshape(S5_GROUPS, S5_STATE), tb(dbbr), tb(dbbi),
        cst["expand"], cst["expand_t"], name="s5_params_bwd")
    gr["s5_a_re"], gr["s5_a_im"], gr["s5_log_dt"] = dar, dai, dld[:, 0]
    gr["s5_b_re"] = dbr.reshape(S5_GROUPS, S5_STATE, S5_GROUP)
    gr["s5_b_im"] = dbi.reshape(S5_GROUPS, S5_STATE, S5_GROUP)
    dwt_main = matmul_tn(dproj, sv["h0"], tk=1024, tn=1024, tt=TM, name="in_proj_dw")
    dwt_dt = matmul_tn(ddtraw, sv["h0"], tk=D_DT_PAD, tn=1024, tt=TM, name="in_proj_dt_dw")
    dx0, gr["norm_mix"] = matmul_nt(dproj, w["w_in_t"], g2=ddtraw, w2=w["w_dt_t"], w_is_nk=True, epi="rmsbwd",
                                    epi_args=(sv["x"], sv["r0"], g("norm_mix"), dx1), tm=TM, tko=1024, tn=1024,
                                    name="in_proj_dx")
    gr = {k: (v[0] if k.startswith("norm_") else v) for k, v in gr.items()}
    dw_in = jnp.concatenate([dwt_main, dwt_dt[:SSD_HEADS]], axis=0).reshape(N_SHARD, ROWS["w_in"], D_MODEL)
    small = [gr[n].reshape(-1) for n in SMALL_L] + ([] if extra_small is None else [extra_small.reshape(-1)])
    small = jnp.concatenate(small)
    small = jnp.pad(small, (0, N_SHARD * SMALL_Q * D_MODEL - small.size)).reshape(N_SHARD, SMALL_Q, D_MODEL)
    rest = jnp.concatenate([dw_in, jnp.zeros((N_SHARD, TAIL_OFF - OFF["w_in"] - ROWS["w_in"], D_MODEL), F32), small,
                            jnp.zeros((N_SHARD, TAIL_ROWS - SMALL_Q, D_MODEL), F32)], axis=1)
    gp = lax.dynamic_update_slice(gp, rest, (0, OFF["w_in"], 0))
    return dx0, gp


def _local_step(x, mem, target, layers, norm_final):
    cst = _consts()
    saved = []
    for l in range(DEPTH):
        x, sv, _ = _layer_fwd(x, mem, layers[l], cst)
        saved.append(sv)
    loss, dx, dgf = loss_head(x, norm_final.reshape(1, -1), target, tm=TM, name="loss_head")
    packs = [None] * DEPTH
    for l in reversed(range(DEPTH)):
        dx, packs[l] = _layer_bwd(dx, mem, layers[l], saved[l], cst, extra_small=dgf[0] if l == DEPTH - 1 else None)
    return loss, dx, packs


def sum_halves(gpack, recv, c_idx):
    nb = HALF_ROWS // SUM_ROWS

    def body(c_ref, a_ref, b_ref, o_ref):
        o_ref[...] = (a_ref[...] + b_ref[...]).astype(BF16)

    blk = (1, SUM_ROWS, D_MODEL)
    return pl.pallas_call(
        body,
        grid_spec=pltpu.PrefetchScalarGridSpec(
            num_scalar_prefetch=1, grid=(N_SHARD, nb),
            in_specs=[pl.BlockSpec(blk, lambda t, i, c_ref: (t, c_ref[0] * nb + i, 0)),
                      pl.BlockSpec(blk, lambda t, i, c_ref: (t, i, 0))],
            out_specs=pl.BlockSpec(blk, lambda t, i, c_ref: (t, i, 0))),
        out_shape=jax.ShapeDtypeStruct((N_SHARD, HALF_ROWS, D_MODEL), BF16),
        compiler_params=_cp("parallel", "parallel"), name="sum_halves")(c_idx, gpack, recv)


def sum_chips(parts, csum, gshards, l, place_idx):
    nb = HALF_ROWS // SUM_ROWS

    def body(pi_ref, p0, p1, p2, p3, own, g_ref, o_ref):
        s = pi_ref[0]
        vals = [jnp.where(s == k, own[0], p[0]).astype(F32) for k, p in enumerate((p0, p1, p2, p3))]
        o_ref[0] = ((vals[0] + vals[1]) + vals[2]) + vals[3]

    blk = (1, SUM_ROWS, D_MODEL)
    part_spec = lambda k: pl.BlockSpec(blk, lambda i, pi_ref: (jnp.where(pi_ref[0] == k, (k + 1) % N_SHARD, k), i, 0))
    return pl.pallas_call(
        body,
        grid_spec=pltpu.PrefetchScalarGridSpec(
            num_scalar_prefetch=1, grid=(nb,),
            in_specs=[part_spec(k) for k in range(N_SHARD)]
                     + [pl.BlockSpec(blk, lambda i, pi_ref: (pi_ref[0], i, 0)), _ANY],
            out_specs=pl.BlockSpec(blk, lambda i, pi_ref: (l, pi_ref[1] * nb + i, 0))),
        out_shape=jax.ShapeDtypeStruct(gshards.shape, F32), input_output_aliases={6: 0},
        compiler_params=_cp("parallel"), name="sum_chips")(place_idx, parts, parts, parts, parts, csum, gshards)


class _Reduction:
    def __init__(self, gpack, layer, place_idx, gshards, smalls):
        self.gpack, self.layer, self.place_idx, self.gshards, self.smalls = gpack, layer, place_idx, gshards, smalls

    def exchange(self):
        return exchange_halves(self.gpack)

    def after_exchange(self, recv):
        self.csum = sum_halves(self.gpack, recv, self.place_idx[1:])

    def scatter(self):
        return scatter_chips(self.csum)

    def after_scatter(self, parts):
        self.gshards = sum_chips(parts, self.csum, self.gshards, self.layer, self.place_idx)

    def share(self):
        return share_reduced(self.gshards, self.smalls, self.layer)

    def after_share(self, shared):
        self.gshards, self.smalls = shared

    def run_alone(self):
        self.after_exchange(_comm_only(self.exchange())[0])
        self.after_scatter(_comm_only(self.scatter())[0])
        self.after_share(_comm_only(self.share()))
        return self.gshards, self.smalls


def adamw(w, g, m, v, *, name):
    shape = w.shape
    cols = shape[-1]
    rows = w.size // cols
    tr = 512 if rows % 512 == 0 else rows
    c1 = 1.0 / (1.0 - ADAM_B1 ** ADAM_STEP)
    c2 = 1.0 / (1.0 - ADAM_B2 ** ADAM_STEP)

    def body(w_ref, g_ref, m_ref, v_ref, d_ref, nm_ref, nv_ref):
        gv = g_ref[...]
        nm = ADAM_B1 * m_ref[...] + (1.0 - ADAM_B1) * gv
        nv = ADAM_B2 * v_ref[...] + (1.0 - ADAM_B2) * (gv * gv)
        d_ref[...] = -ADAM_LR * ((nm * c1) / (jnp.sqrt(nv * c2) + ADAM_EPS) + ADAM_WD * w_ref[...])
        nm_ref[...] = nm
        nv_ref[...] = nv

    spec = pl.BlockSpec((tr, cols), lambda i: (i, 0))
    sds = jax.ShapeDtypeStruct((rows, cols), F32)
    outs = pl.pallas_call(body, grid=(rows // tr,), in_specs=[spec] * 4, out_specs=[spec] * 3, out_shape=[sds] * 3,
                          compiler_params=_cp("parallel"), name=name)(
                              *[a.reshape(rows, cols) for a in (w, g, m, v)])
    return [o.reshape(shape) for o in outs]


def _own_pack(wts, l):
    rows = [wts[n][l].T if n == "w_in" else wts[n][l].reshape(ROWS[n], D_MODEL) for n in sorted(OFF, key=OFF.get)]
    cw = wts["ssd_conv_w"][l].reshape(-1)
    hi = lax.reduce_precision(cw, 8, 7)
    mid = lax.reduce_precision(cw - hi, 8, 7)
    lo = lax.reduce_precision(cw - hi - mid, 8, 7)
    conv = jnp.pad(jnp.concatenate([hi, mid, lo]), (0, CONV_ROWS * D_MODEL - 3 * cw.size)).reshape(CONV_ROWS, D_MODEL)
    gap = jnp.zeros((TAIL_OFF - OFF["w_in"] - ROWS["w_in"], D_MODEL), F32)
    rest = jnp.zeros((TAIL_ROWS - CONV_ROWS, D_MODEL), F32)
    return jnp.concatenate(rows + [gap, conv, rest], axis=0).astype(BF16)


def _layer_weights(gathered, wts, l):
    w = {n: wts[n][l] for n in SMALL_L if n != "ssd_conv_w"}
    w["pack"] = gathered
    square = lambda n: gathered[:, OFF[n]:OFF[n] + ROWS[n]].reshape(N_SHARD * ROWS[n], D_MODEL)
    w["s5_w_glu"], w["xa_wq"], w["xa_wo"] = square("s5_w_glu"), square("xa_wq"), square("xa_wo")
    w["xa_wkv"] = jnp.concatenate([square("xa_wk"), square("xa_wv")], axis=1)
    w_in_t = square("w_in")
    w["w_in_t"] = w_in_t[:D_MAIN]
    w["w_dt_t"] = jnp.pad(w_in_t[D_MAIN:], ((0, D_DT_PAD - SSD_HEADS), (0, 0)))
    per = SSD_CONV * D_XBC // N_SHARD
    cw = gathered[:, TAIL_OFF:TAIL_OFF + CONV_ROWS].astype(F32).reshape(N_SHARD, -1)[:, :3 * per]
    cw = cw.reshape(N_SHARD, 3, SSD_CONV, D_XBC // N_SHARD)
    cw = (cw[:, 0] + cw[:, 1]) + cw[:, 2]
    w["ssd_conv_w"] = cw.transpose(1, 0, 2).reshape(SSD_CONV, D_XBC)
    return w


def kernel(x, mem, norm_mix, w_in, s5_a_re, s5_a_im, s5_log_dt, s5_b_re, s5_b_im, s5_c_re, s5_c_im, s5_d, s5_w_glu, ssd_conv_w, ssd_conv_b, ssd_dt_bias, ssd_a_log, ssd_d, ssd_norm, w_out, norm_xattn, norm_mem, xa_wq, xa_wk, xa_wv, xa_wo, norm_mlp, mlp_w1, mlp_w2, norm_final, loss_target, m_norm_mix, m_w_in, m_s5_a_re, m_s5_a_im, m_s5_log_dt, m_s5_b_re, m_s5_b_im, m_s5_c_re, m_s5_c_im, m_s5_d, m_s5_w_glu, m_ssd_conv_w, m_ssd_conv_b, m_ssd_dt_bias, m_ssd_a_log, m_ssd_d, m_ssd_norm, m_w_out, m_norm_xattn, m_norm_mem, m_xa_wq, m_xa_wk, m_xa_wv, m_xa_wo, m_norm_mlp, m_mlp_w1, m_mlp_w2, m_norm_final, v_norm_mix, v_w_in, v_s5_a_re, v_s5_a_im, v_s5_log_dt, v_s5_b_re, v_s5_b_im, v_s5_c_re, v_s5_c_im, v_s5_d, v_s5_w_glu, v_ssd_conv_w, v_ssd_conv_b, v_ssd_dt_bias, v_ssd_a_log, v_ssd_d, v_ssd_norm, v_w_out, v_norm_xattn, v_norm_mem, v_xa_wq, v_xa_wk, v_xa_wv, v_xa_wo, v_norm_mlp, v_mlp_w1, v_mlp_w2, v_norm_final):
    names = ("norm_mix", "w_in", "s5_a_re", "s5_a_im", "s5_log_dt", "s5_b_re", "s5_b_im", "s5_c_re", "s5_c_im", "s5_d",
             "s5_w_glu", "ssd_conv_w", "ssd_conv_b", "ssd_dt_bias", "ssd_a_log", "ssd_d", "ssd_norm", "w_out",
             "norm_xattn", "norm_mem", "xa_wq", "xa_wk", "xa_wv", "xa_wo", "norm_mlp", "mlp_w1", "mlp_w2", "norm_final")
    loc = locals()
    wts = {n: loc[n] for n in names}
    mom = {n: loc["m_" + n] for n in names}
    var = {n: loc["v_" + n] for n in names}
    shard = 2 * lax.axis_index("x") + lax.axis_index("y")
    core = lax.axis_index("c")

    cst = _consts()
    place_idx = jnp.stack([shard, core]).astype(jnp.int32)
    h, mem0 = x[0], mem[0]

    own = _own_pack(wts, 0)
    landed = _comm_only(gather_over_ici(own, place_own(own, place_idx[:1])))[0]
    gathered = _comm_only(gather_to_sibling(landed))[0]
    layers, saved = [], []
    for l in range(DEPTH):
        layers.append(_layer_weights(gathered, wts, l))
        nxt = None
        if l + 1 < DEPTH:
            own = _own_pack(wts, l + 1)
            nxt = (own, place_own(own, place_idx[:1]))
        h, sv, gathered = _layer_fwd(h, mem0, layers[l], cst, next_pack=nxt)
        saved.append(sv)
    loss, dx, dgf = loss_head(h, norm_final.reshape(1, -1), loss_target[0], tm=TM, name="loss_head")

    gshards = jnp.zeros((DEPTH, PACK_ROWS, D_MODEL), F32)
    smalls = jnp.zeros((DEPTH, N_SHARD, TAIL_ROWS, D_MODEL), F32)
    pending = None
    for l in reversed(range(DEPTH)):
        dx, gpack = _layer_bwd(dx, mem0, layers[l], saved[l], cst, extra_small=dgf[0] if l == DEPTH - 1 else None,
                               reduce_hooks=pending)
        if pending is not None:
            gshards, smalls = pending.gshards, pending.smalls
        pending = _Reduction(gpack, l, place_idx, gshards, smalls)
    gshards, smalls = pending.run_alone()

    g = {n: gshards[:, OFF[n]:OFF[n] + ROWS[n]].reshape(wts[n].shape) for n in OFF if n != "w_in"}
    g["w_in"] = gshards[:, OFF["w_in"]:OFF["w_in"] + ROWS["w_in"]].transpose(0, 2, 1)
    small_red = smalls[:, :, :SMALL_Q].reshape(DEPTH, -1)
    off = 0
    for n in SMALL_L:
        shape = (SSD_CONV, D_XBC) if n == "ssd_conv_w" else wts[n].shape[1:]
        size = math.prod(shape)
        g[n] = small_red[:, off:off + size].reshape((DEPTH,) + shape)
        off += size
    g["norm_final"] = small_red[DEPTH - 1, off:off + D_MODEL]
    g["ssd_conv_w"] = lax.dynamic_slice_in_dim(g["ssd_conv_w"], shard * (D_XBC // N_SHARD), D_XBC // N_SHARD, axis=2)

    deltas, new_m, new_v = {}, {}, {}
    for n in names:
        deltas[n], new_m[n], new_v[n] = adamw(wts[n], g[n], mom[n], var[n], name="adamw_" + n)
    loss_all = lax.psum(loss[0, 0], ("x", "y", "c"))
    return (loss_all, dx[None], *[g[n] for n in names], *[deltas[n] for n in names], *[new_m[n] for n in names],
            *[new_v[n] for n in names])
```

```python
import functools
import math

import jax
import jax.numpy as jnp
import numpy as np
from jax import lax
from jax.experimental import pallas as pl
from jax.experimental.pallas import tpu as pltpu

F32 = jnp.float32
BF16 = jnp.bfloat16
HIGHEST = lax.Precision.HIGHEST

D_MODEL = 1024
DEPTH = 4
D_S5 = 1024
D_SSD = 1024
S5_GROUP = 16
S5_GROUPS = 64
S5_STATE = 64
SSD_HEADS = 16
SSD_HEADDIM = 64
SSD_NGROUPS = 4
SSD_STATE = 128
SSD_CONV = 4
SSD_CHUNK = 128
D_XBC = 2048
D_MAIN = 4096
D_DT_PAD = 128
XA_HEADS = 4
XA_HEAD_DIM = 256
D_FF = 4096
EPS = 1e-5
ADAM_LR, ADAM_B1, ADAM_B2, ADAM_EPS, ADAM_WD, ADAM_STEP = 0.001, 0.9, 0.999, 1e-08, 0.01, 10

VMEM_LIMIT = 56 * 1024 * 1024
MESH_T = pl.DeviceIdType.MESH


def _cp(*sem):
    return pltpu.CompilerParams(dimension_semantics=tuple(sem) if sem else None, vmem_limit_bytes=VMEM_LIMIT)


_ANY = pl.BlockSpec(memory_space=pl.ANY)


class Comm:
    def __init__(self, name, inputs, out_shapes, sems, start, wait, aliases=None):
        self.name, self.inputs, self.out_shapes, self.sems = name, list(inputs), list(out_shapes), list(sems)
        self.start, self.wait, self.aliases = start, wait, dict(aliases or {})


def _call(body, *, grid, in_specs, out_specs, out_shape, args, scratch_shapes=(), sem, name, comm=None, aliases=None):
    in_specs, out_specs, out_shape = list(in_specs), list(out_specs), list(out_shape)
    scratch_shapes = list(scratch_shapes)
    aliases = dict(aliases or {})
    if comm is None:
        res = pl.pallas_call(body, grid=grid, in_specs=in_specs, out_specs=out_specs, out_shape=out_shape,
                             scratch_shapes=scratch_shapes, compiler_params=_cp(*sem), name=name,
                             input_output_aliases=aliases)(*args)
        return list(res), []
    n_in, n_out, n_scr = len(in_specs), len(out_specs), len(scratch_shapes)
    c_in, c_out = len(comm.inputs), len(comm.out_shapes)

    def wrapped(*refs):
        a, refs = refs[:n_in], refs[n_in:]
        ci, refs = refs[:c_in], refs[c_in:]
        o, refs = refs[:n_out], refs[n_out:]
        co, refs = refs[:c_out], refs[c_out:]
        s, cs = refs[:n_scr], refs[n_scr:]
        first = functools.reduce(jnp.logical_and, [pl.program_id(d) == 0 for d in range(len(grid))])
        last = functools.reduce(jnp.logical_and, [pl.program_id(d) == grid[d] - 1 for d in range(len(grid))])

        @pl.when(first)
        def _():
            comm.start(ci, co, cs)

        body(*a, *o, *s)

        @pl.when(last)
        def _():
            comm.wait(ci, co, cs)

    for i, j in comm.aliases.items():
        aliases[n_in + i] = n_out + j
    res = pl.pallas_call(wrapped, grid=grid, in_specs=in_specs + [_ANY] * c_in, out_specs=out_specs + [_ANY] * c_out,
                         out_shape=out_shape + comm.out_shapes, scratch_shapes=scratch_shapes + comm.sems,
                         compiler_params=_cp(*(("arbitrary",) * len(grid))), name=name + "_" + comm.name,
                         input_output_aliases=aliases)(*args, *comm.inputs)
    return list(res[:n_out]), list(res[n_out:])


def _combine(a, b):
    if a is None or b is None:
        return a or b
    ni, no, ns = len(a.inputs), len(a.out_shapes), len(a.sems)

    def start(ci, co, cs):
        a.start(ci[:ni], co[:no], cs[:ns])
        b.start(ci[ni:], co[no:], cs[ns:])

    def wait(ci, co, cs):
        a.wait(ci[:ni], co[:no], cs[:ns])
        b.wait(ci[ni:], co[no:], cs[ns:])

    aliases = dict(a.aliases)
    aliases.update({ni + i: no + j for i, j in b.aliases.items()})
    return Comm(a.name + "_" + b.name, a.inputs + b.inputs, a.out_shapes + b.out_shapes, a.sems + b.sems, start, wait,
                aliases)


def _comm_only(comm):
    def body(*refs):
        ci, refs = refs[:len(comm.inputs)], refs[len(comm.inputs):]
        co, cs = refs[:len(comm.out_shapes)], refs[len(comm.out_shapes):]
        comm.start(ci, co, cs)
        comm.wait(ci, co, cs)

    res = pl.pallas_call(body, in_specs=[_ANY] * len(comm.inputs), out_specs=[_ANY] * len(comm.out_shapes),
                         out_shape=comm.out_shapes, scratch_shapes=comm.sems, name=comm.name,
                         input_output_aliases=comm.aliases)(*comm.inputs)
    return list(res)


def _dot(a, b):
    return jnp.dot(a, b, preferred_element_type=F32)


def _dot_nt(a, b):
    return lax.dot_general(a, b, (((1,), (1,)), ((), ())), preferred_element_type=F32)


def _dot_tn(a, b):
    return lax.dot_general(a, b, (((0,), (0,)), ((), ())), preferred_element_type=F32)


def _dot_hi(a, b):
    return jnp.dot(a, b, precision=HIGHEST, preferred_element_type=F32)


def _gelu(x):
    c = math.sqrt(2.0 / math.pi)
    return 0.5 * x * (1.0 + jnp.tanh(c * (x + 0.044715 * x * x * x)))


def _gelu_grad(x):
    c = math.sqrt(2.0 / math.pi)
    t = jnp.tanh(c * (x + 0.044715 * x * x * x))
    return 0.5 * (1.0 + t) + 0.5 * x * (1.0 - t * t) * c * (1.0 + 3 * 0.044715 * x * x)


def _sigmoid(x):
    return 1.0 / (1.0 + jnp.exp(-x))


def _act(a, act):
    if act is None:
        return a.astype(BF16)
    a = a.astype(F32)
    if act == "relu2":
        a = jnp.maximum(a, 0.0)
        return (a * a).astype(BF16)
    if act == "gelu":
        return _gelu(a).astype(BF16)
    raise ValueError(act)


def _pack_block(ref):
    return ref[...].reshape(-1, ref.shape[-1])


def norm_matmul(x, g, w, w2=None, *, tm, tn, name, wspec=None, n_out=None, w_transposed=False):
    T, D = x.shape
    N = n_out if wspec is not None else (w.shape[0] if w_transposed else w.shape[1])
    wget = (lambda r: r[...]) if wspec is None else _pack_block
    mm = _dot_nt if w_transposed else _dot
    has2 = w2 is not None

    def body(x_ref, g_ref, w_ref, *rest):
        if has2:
            w2_ref, o_ref, h_ref, r_ref, o2_ref = rest
        else:
            o_ref, h_ref, r_ref = rest
        j = pl.program_id(1)

        @pl.when(j == 0)
        def _():
            xv = x_ref[...]
            r = lax.rsqrt(jnp.mean(xv * xv, axis=-1, keepdims=True) + EPS)
            h = (xv * r * g_ref[...]).astype(BF16)
            h_ref[...] = h
            r_ref[...] = r
            if has2:
                o2_ref[...] = mm(h, w2_ref[...])

        o_ref[...] = mm(h_ref[...], wget(w_ref)).astype(o_ref.dtype)

    if wspec is None:
        wspec = pl.BlockSpec((tn, D), lambda i, j: (j, 0)) if w_transposed else pl.BlockSpec((D, tn), lambda i, j: (0, j))
    in_specs = [pl.BlockSpec((tm, D), lambda i, j: (i, 0)), pl.BlockSpec((1, D), lambda i, j: (0, 0)), wspec]
    out_shape = [jax.ShapeDtypeStruct((T, N), BF16), jax.ShapeDtypeStruct((T, D), BF16),
                 jax.ShapeDtypeStruct((T, 1), F32)]
    out_specs = [pl.BlockSpec((tm, tn), lambda i, j: (i, j)), pl.BlockSpec((tm, D), lambda i, j: (i, 0)),
                 pl.BlockSpec((tm, 1), lambda i, j: (i, 0))]
    args = [x, g, w]
    if has2:
        in_specs.append(pl.BlockSpec(w2.shape, lambda i, j: (0, 0)))
        out_shape.append(jax.ShapeDtypeStruct((T, D_DT_PAD), F32))
        out_specs.append(pl.BlockSpec((tm, D_DT_PAD), lambda i, j: (i, 0)))
        args.append(w2)
    return pl.pallas_call(body, grid=(T // tm, N // tn), in_specs=in_specs, out_specs=out_specs,
                          out_shape=out_shape, compiler_params=_cp("parallel", "arbitrary"), name=name)(*args)


def matmul_res(a, w, r, *, act=None, tm, tn, tk, name, wspec=None):
    T, K = a.shape
    N = r.shape[1]
    wget = (lambda r_: r_[...]) if wspec is None else _pack_block
    nk = K // tk

    def body(a_ref, w_ref, r_ref, o_ref):
        k = pl.program_id(2)

        @pl.when(k == 0)
        def _():
            o_ref[...] = r_ref[...]

        o_ref[...] += _dot(_act(a_ref[...], act), wget(w_ref))

    return pl.pallas_call(
        body, grid=(T // tm, N // tn, nk),
        in_specs=[pl.BlockSpec((tm, tk), lambda i, j, k: (i, k)),
                  pl.BlockSpec((tk, tn), lambda i, j, k: (k, j)) if wspec is None else wspec,
                  pl.BlockSpec((tm, tn), lambda i, j, k: (i, j))],
        out_specs=pl.BlockSpec((tm, tn), lambda i, j, k: (i, j)),
        out_shape=jax.ShapeDtypeStruct((T, N), F32),
        compiler_params=_cp("parallel", "parallel", "arbitrary"), name=name)(a, w, r)


def glu_fwd(ys, w, *, tm, name, comm=None):
    T, N = ys.shape

    def body(y_ref, w_ref, o_ref, t_ref):
        a = _gelu(y_ref[...].astype(F32))
        t = _dot(a.astype(BF16), w_ref[...])
        o_ref[...] = (a * _sigmoid(t)).astype(BF16)
        t_ref[...] = t.astype(BF16)

    return _call(
        body, grid=(T // tm,),
        in_specs=[pl.BlockSpec((tm, N), lambda i: (i, 0)), pl.BlockSpec((N, N), lambda i: (0, 0))],
        out_specs=[pl.BlockSpec((tm, N), lambda i: (i, 0)), pl.BlockSpec((tm, N), lambda i: (i, 0))],
        out_shape=[jax.ShapeDtypeStruct((T, 2 * N), BF16), jax.ShapeDtypeStruct((T, N), BF16)],
        sem=("parallel",), name=name, args=(ys, w), comm=comm)


def glu_bwd_pre(dout, ys, t, *, tm, name):
    T, N = ys.shape

    def body(d_ref, y_ref, t_ref, dt_ref, da_ref):
        d = d_ref[...].astype(F32)
        a = _gelu(y_ref[...].astype(F32))
        s = _sigmoid(t_ref[...].astype(F32))
        dt_ref[...] = (d * a * s * (1.0 - s)).astype(BF16)
        da_ref[...] = (d * s).astype(BF16)

    spec = pl.BlockSpec((tm, N), lambda i: (i, 0))
    return pl.pallas_call(
        body, grid=(T // tm,), in_specs=[spec, spec, spec], out_specs=[spec, spec],
        out_shape=[jax.ShapeDtypeStruct((T, N), BF16), jax.ShapeDtypeStruct((T, N), BF16)],
        compiler_params=_cp("parallel"), name=name)(dout, ys, t)


def matmul_nt(g, w, *, epi=None, epi_args=(), g2=None, w2=None, tm, tko, tn, out_dtype=BF16, name, wspec=None,
              k_out=None, comm=None, w_is_nk=False):
    T, N = g.shape
    K = k_out if wspec is not None else (w.shape[1] if w_is_nk else w.shape[0])
    wget = (lambda r_: r_[...]) if wspec is None else _pack_block
    mm = _dot if w_is_nk else _dot_nt
    nn = N // tn
    has2 = g2 is not None
    rms = epi == "rmsbwd"
    if rms:
        assert tko == K
    n_epi = len(epi_args)

    def body(*refs):
        g_ref, w_ref = refs[0], refs[1]
        pos = 2
        if has2:
            g2_ref, w2_ref = refs[2], refs[3]
            pos = 4
        e_refs = refs[pos:pos + n_epi]
        pos += n_epi
        o_ref = refs[pos]
        pos += 1
        if rms:
            dg_ref = refs[pos]
            pos += 1
        acc_ref = refs[pos]
        i = pl.program_id(0)
        n = pl.program_id(2)
        part = mm(g_ref[...].astype(BF16), wget(w_ref))

        @pl.when(n == 0)
        def _():
            acc_ref[...] = part

        @pl.when(n > 0)
        def _():
            acc_ref[...] += part

        @pl.when(n == nn - 1)
        def _():
            acc = acc_ref[...]
            if has2:
                acc = acc + mm(g2_ref[...].astype(BF16), w2_ref[...])
            if epi is None:
                o_ref[...] = acc.astype(o_ref.dtype)
            elif epi == "relu2bwd":
                h1 = e_refs[0][...].astype(F32)
                o_ref[...] = (acc * 2.0 * jnp.maximum(h1, 0.0)).astype(o_ref.dtype)
            elif epi == "glubwd":
                da1 = e_refs[0][...].astype(F32)
                ys = e_refs[1][...].astype(F32)
                o_ref[...] = ((da1 + acc) * _gelu_grad(ys)).astype(o_ref.dtype)
            elif epi == "rmsbwd":
                xv, rs, gain, rv = e_refs[0][...], e_refs[1][...], e_refs[2][...], e_refs[3][...]
                xhat = xv * rs
                gd = acc * gain
                o_ref[...] = rv + rs * (gd - xhat * jnp.mean(gd * xhat, axis=-1, keepdims=True))
                part_g = jnp.sum(acc * xhat, axis=0, keepdims=True)

                @pl.when(i == 0)
                def _():
                    dg_ref[...] = part_g

                @pl.when(i > 0)
                def _():
                    dg_ref[...] += part_g

    if wspec is None:
        wspec = (pl.BlockSpec((tn, tko), lambda i, k, n: (n, k)) if w_is_nk
                 else pl.BlockSpec((tko, tn), lambda i, k, n: (k, n)))
    in_specs = [pl.BlockSpec((tm, tn), lambda i, k, n: (i, n)), wspec]
    args = [g, w]
    if has2:
        n2 = g2.shape[1]
        in_specs += [pl.BlockSpec((tm, n2), lambda i, k, n: (i, 0)),
                     pl.BlockSpec((n2, tko), lambda i, k, n: (0, k)) if w_is_nk
                     else pl.BlockSpec((tko, n2), lambda i, k, n: (k, 0))]
        args += [g2, w2]
    if epi == "relu2bwd" or epi == "glubwd":
        in_specs += [pl.BlockSpec((tm, tko), lambda i, k, n: (i, k))] * n_epi
    elif rms:
        in_specs += [pl.BlockSpec((tm, K), lambda i, k, n: (i, 0)), pl.BlockSpec((tm, 1), lambda i, k, n: (i, 0)),
                     pl.BlockSpec((1, K), lambda i, k, n: (0, 0)), pl.BlockSpec((tm, K), lambda i, k, n: (i, 0))]
    args += list(epi_args)
    out_shape = [jax.ShapeDtypeStruct((T, K), F32 if rms else out_dtype)]
    out_specs = [pl.BlockSpec((tm, tko), lambda i, k, n: (i, k))]
    if rms:
        out_shape.append(jax.ShapeDtypeStruct((1, K), F32))
        out_specs.append(pl.BlockSpec((1, K), lambda i, k, n: (0, 0)))
    sem = ("arbitrary",) * 3 if rms else ("parallel", "parallel", "arbitrary")
    res, cres = _call(body, grid=(T // tm, K // tko, nn), in_specs=in_specs, out_specs=out_specs, out_shape=out_shape,
                      scratch_shapes=[pltpu.VMEM((tm, tko), F32)], sem=sem, name=name, args=args, comm=comm)
    res = res if rms else res[0]
    return res if comm is None else (res, cres)


def matmul_tn(a, g, *, act=None, tk, tn, tt, name, pack=None, pack_spec=None, pack_shape=None):
    T, K = a.shape
    N = g.shape[1]
    to_pack = pack_spec is not None

    def body(a_ref, g_ref, *rest):
        o_ref = rest[-1]
        t = pl.program_id(2)
        part = _dot_tn(_act(a_ref[...], act), g_ref[...].astype(BF16))
        part = part.reshape(o_ref.shape)

        @pl.when(t == 0)
        def _():
            o_ref[...] = part

        @pl.when(t > 0)
        def _():
            o_ref[...] += part

    in_specs = [pl.BlockSpec((tt, tk), lambda k, n, t: (t, k)), pl.BlockSpec((tt, tn), lambda k, n, t: (t, n))]
    args = [a, g]
    aliases = {}
    if pack is not None:
        in_specs.append(_ANY)
        args.append(pack)
        aliases = {2: 0}
    return pl.pallas_call(
        body, grid=(K // tk, N // tn, T // tt), in_specs=in_specs,
        out_specs=pack_spec if to_pack else pl.BlockSpec((tk, tn), lambda k, n, t: (k, n)),
        out_shape=jax.ShapeDtypeStruct(pack_shape if to_pack else (K, N), F32), input_output_aliases=aliases,
        compiler_params=_cp("parallel", "parallel", "arbitrary"), name=name)(*args)


def attn_fwd(q, kv, *, tm, name):
    T = q.shape[0]
    M = kv.shape[0]
    scale = XA_HEAD_DIM ** -0.5

    def body(q_ref, kv_ref, o_ref):
        for h in range(XA_HEADS):
            sl = slice(h * XA_HEAD_DIM, (h + 1) * XA_HEAD_DIM)
            kh = kv_ref[:, h * XA_HEAD_DIM:(h + 1) * XA_HEAD_DIM]
            vh = kv_ref[:, D_MODEL + h * XA_HEAD_DIM:D_MODEL + (h + 1) * XA_HEAD_DIM]
            s = _dot_nt(q_ref[:, sl], kh) * scale
            s = s - jnp.max(s, axis=-1, keepdims=True)
            p = jnp.exp(s)
            p = p / jnp.sum(p, axis=-1, keepdims=True)
            o_ref[:, sl] = _dot(p.astype(BF16), vh).astype(BF16)

    return pl.pallas_call(
        body, grid=(T // tm,),
        in_specs=[pl.BlockSpec((tm, D_MODEL), lambda i: (i, 0)), pl.BlockSpec((M, 2 * D_MODEL), lambda i: (0, 0))],
        out_specs=pl.BlockSpec((tm, D_MODEL), lambda i: (i, 0)),
        out_shape=jax.ShapeDtypeStruct((T, D_MODEL), BF16),
        compiler_params=_cp("parallel"), name=name)(q, kv)


def attn_bwd(q, kv, do, *, tm, name):
    T = q.shape[0]
    M = kv.shape[0]
    scale = XA_HEAD_DIM ** -0.5

    def body(q_ref, kv_ref, do_ref, dq_ref, dkv_ref):
        i = pl.program_id(0)

        @pl.when(i == 0)
        def _():
            dkv_ref[...] = jnp.zeros_like(dkv_ref)

        for h in range(XA_HEADS):
            sl = slice(h * XA_HEAD_DIM, (h + 1) * XA_HEAD_DIM)
            slv = slice(D_MODEL + h * XA_HEAD_DIM, D_MODEL + (h + 1) * XA_HEAD_DIM)
            qh = q_ref[:, sl]
            kh = kv_ref[:, sl]
            vh = kv_ref[:, slv]
            doh = do_ref[:, sl]
            s = _dot_nt(qh, kh) * scale
            s = s - jnp.max(s, axis=-1, keepdims=True)
            p = jnp.exp(s)
            p = p / jnp.sum(p, axis=-1, keepdims=True)
            pb = p.astype(BF16)
            dkv_ref[:, slv] += _dot_tn(pb, doh)
            dp = _dot_nt(doh, vh)
            ds = (p * (dp - jnp.sum(dp * p, axis=-1, keepdims=True)) * scale).astype(BF16)
            dq_ref[:, sl] = _dot(ds, kh).astype(BF16)
            dkv_ref[:, sl] += _dot_tn(ds, qh)

    spec = pl.BlockSpec((tm, D_MODEL), lambda i: (i, 0))
    kvspec = pl.BlockSpec((M, 2 * D_MODEL), lambda i: (0, 0))
    return pl.pallas_call(
        body, grid=(T // tm,), in_specs=[spec, kvspec, spec], out_specs=[spec, kvspec],
        out_shape=[jax.ShapeDtypeStruct((T, D_MODEL), BF16), jax.ShapeDtypeStruct((M, 2 * D_MODEL), F32)],
        compiler_params=_cp("arbitrary"), name=name)(q, kv, do)


def loss_head(x, g, target, *, tm, name):
    T, D = x.shape

    def body(x_ref, g_ref, t_ref, l_ref, dx_ref, dg_ref):
        i = pl.program_id(0)
        xv = x_ref[...]
        gain = g_ref[...]
        r = lax.rsqrt(jnp.mean(xv * xv, axis=-1, keepdims=True) + EPS)
        xhat = xv * r
        err = xhat * gain - t_ref[...]
        part_l = jnp.full((1, 128), 0.5 / D, F32) * jnp.sum(err * err)
        dy = err * (1.0 / D)
        gd = dy * gain
        dx_ref[...] = r * (gd - xhat * jnp.mean(gd * xhat, axis=-1, keepdims=True))
        part_g = jnp.sum(dy * xhat, axis=0, keepdims=True)

        @pl.when(i == 0)
        def _():
            l_ref[...] = part_l
            dg_ref[...] = part_g

        @pl.when(i > 0)
        def _():
            l_ref[...] += part_l
            dg_ref[...] += part_g

    spec = pl.BlockSpec((tm, D), lambda i: (i, 0))
    return pl.pallas_call(
        body, grid=(T // tm,), in_specs=[spec, pl.BlockSpec((1, D), lambda i: (0, 0)), spec],
        out_specs=[pl.BlockSpec((1, 128), lambda i: (0, 0)), spec, pl.BlockSpec((1, D), lambda i: (0, 0))],
        out_shape=[jax.ShapeDtypeStruct((1, 128), F32), jax.ShapeDtypeStruct((T, D), F32),
                   jax.ShapeDtypeStruct((1, D), F32)],
        compiler_params=_cp("arbitrary"), name=name)(x, g, target)


S5_LS = 128
S5_SEG = 8
S5_TB = S5_LS * S5_SEG
S5_CH = 8
S5_CW = 128
S5_NST = 512


def _cmul(ar, ai, br, bi):
    return ar * br - ai * bi, ar * bi + ai * br


def s5_params_fwd(a_re, a_im, log_dt, b_re, b_im, expand, *, name):
    G, P = a_re.shape

    def body(ar_ref, ai_ref, ld_ref, br_ref, bi_ref, e_ref, abr_ref, abi_ref, apr_ref, api_ref, bbr_ref, bbi_ref):
        ar, ai = ar_ref[...], ai_ref[...]
        dt = jnp.exp(ld_ref[...])
        mag = jnp.exp(dt * ar)
        abr, abi = mag * jnp.cos(dt * ai), mag * jnp.sin(dt * ai)
        den = ar * ar + ai * ai
        zr, zi = abr - 1.0, abi
        fr = (zr * ar + zi * ai) / den
        fi = (zi * ar - zr * ai) / den
        frx, fix = _dot_hi(fr, e_ref[...]), _dot_hi(fi, e_ref[...])
        br, bi = br_ref[...], bi_ref[...]
        bbr_ref[...] = frx * br - fix * bi
        bbi_ref[...] = frx * bi + fix * br
        abr_ref[...] = abr
        abi_ref[...] = abi
        pr, pi = abr, abi
        for _ in range(int(math.log2(S5_LS))):
            pr, pi = _cmul(pr, pi, pr, pi)
        apr_ref[...] = pr
        api_ref[...] = pi

    small = jax.ShapeDtypeStruct((G, P), F32)
    big = jax.ShapeDtypeStruct(b_re.shape, F32)
    return pl.pallas_call(body, out_shape=[small, small, small, small, big, big], name=name)(
        a_re, a_im, log_dt, b_re, b_im, expand)


def s5_params_bwd(a_re, a_im, log_dt, b_re, b_im, g_abr, g_abi, g_bbr, g_bbi, expand, expand_t, *, name):
    G, P = a_re.shape

    def body(ar_ref, ai_ref, ld_ref, br_ref, bi_ref, gar_ref, gai_ref, gbr_ref, gbi_ref, e_ref, et_ref,
             dar_ref, dai_ref, dld_ref, dbr_ref, dbi_ref):
        ar, ai = ar_ref[...], ai_ref[...]
        dt = jnp.exp(ld_ref[...])
        mag = jnp.exp(dt * ar)
        cs, sn = jnp.cos(dt * ai), jnp.sin(dt * ai)
        abr, abi = mag * cs, mag * sn
        den = ar * ar + ai * ai
        zr, zi = abr - 1.0, abi
        fr = (zr * ar + zi * ai) / den
        fi = (zi * ar - zr * ai) / den
        frx, fix = _dot_hi(fr, e_ref[...]), _dot_hi(fi, e_ref[...])
        br, bi = br_ref[...], bi_ref[...]
        gbr, gbi = gbr_ref[...], gbi_ref[...]
        dbr_ref[...] = frx * gbr + fix * gbi
        dbi_ref[...] = -fix * gbr + frx * gbi
        gfr = _dot_hi(br * gbr + bi * gbi, et_ref[...])
        gfi = _dot_hi(-bi * gbr + br * gbi, et_ref[...])
        g_zr = (gfr * ar - gfi * ai) / den
        g_zi = (gfr * ai + gfi * ar) / den
        g_ar = gfr * (zr - fr * 2.0 * ar) / den + gfi * (zi - fi * 2.0 * ar) / den
        g_ai = gfr * (zi - fr * 2.0 * ai) / den + gfi * (-zr - fi * 2.0 * ai) / den
        t_abr = gar_ref[...] + g_zr
        t_abi = gai_ref[...] + g_zi
        g_mag = t_abr * cs + t_abi * sn
        g_th = mag * (-t_abr * sn + t_abi * cs)
        dar_ref[...] = g_ar + g_mag * mag * dt
        dai_ref[...] = g_ai + g_th * dt
        g_dt = jnp.sum(g_mag * mag * ar + g_th * ai, axis=-1, keepdims=True)
        dld_ref[...] = g_dt * dt

    small = jax.ShapeDtypeStruct((G, P), F32)
    big = jax.ShapeDtypeStruct(b_re.shape, F32)
    return pl.pallas_call(body, out_shape=[small, small, jax.ShapeDtypeStruct((G, 1), F32), big, big], name=name)(
        a_re, a_im, log_dt, b_re, b_im, g_abr, g_abi, g_bbr, g_bbi, expand, expand_t)


def _s5_permute_in(src_ref, dst_ref):
    for i in range(S5_LS):
        dst_ref[pl.ds(8 * i, 8), :] = src_ref[pl.ds(i, 8, stride=S5_LS), :]


def _s5_permute_out(src_ref, dst_ref):
    for r in range(S5_SEG):
        for k in range(S5_LS // 8):
            dst_ref[pl.ds(r * S5_LS + 8 * k, 8), :] = src_ref[pl.ds(64 * k + r, 8, stride=8), :]


def _s5_scan(a_r, a_i, dr_ref, di_ref, init_r, init_i, store=None, reverse=False, conj=False):
    sgn = -1.0 if conj else 1.0

    def step(n, c):
        sr, si = c
        i = S5_LS - 1 - n if reverse else n
        nr = a_r * sr - sgn * a_i * si + dr_ref[i]
        ni = a_r * si + sgn * a_i * sr + di_ref[i]
        if store is not None:
            store(i, nr, ni, sr, si)
        return nr, ni

    return lax.fori_loop(0, S5_LS, step, (init_r, init_i), unroll=2)


def _s5_stitch(apr, api, fin_r, fin_i, car_r, car_i, reverse=False, conj=False):
    sgn = -1.0 if conj else 1.0
    rows_r, rows_i = [None] * S5_SEG, [None] * S5_SEG
    order = range(S5_SEG - 1, -1, -1) if reverse else range(S5_SEG)
    for r in order:
        rows_r[r], rows_i[r] = car_r, car_i
        fr, fi = fin_r[r:r + 1], fin_i[r:r + 1]
        car_r, car_i = (apr * car_r - sgn * api * car_i + fr, apr * car_i + sgn * api * car_r + fi)
    return jnp.concatenate(rows_r, 0), jnp.concatenate(rows_i, 0), car_r, car_i


def _s5_specs(nb, rev):
    blk = (lambda c, b: (nb - 1 - b, c)) if rev else (lambda c, b: (b, c))
    tok = pl.BlockSpec((S5_TB, S5_CW), blk)
    par_b = pl.BlockSpec((1, S5_CW, S5_NST), lambda c, b: (c, 0, 0))
    par_c = pl.BlockSpec((1, S5_NST, S5_CW), lambda c, b: (c, 0, 0))
    vec_s = pl.BlockSpec((1, 1, S5_NST), lambda c, b: (c, 0, 0))
    vec_c = pl.BlockSpec((1, 1, S5_CW), lambda c, b: (c, 0, 0))
    return tok, par_b, par_c, vec_s, vec_c


def s5_fwd(proj, bbr, bbi, ccr, cci, abr, abi, apr, api, dskip, *, name, comm=None):
    T = proj.shape[0]
    nb = T // S5_TB
    zeros8 = functools.partial(jnp.zeros, (S5_SEG, S5_NST), F32)

    def body(u_ref, bbr_ref, bbi_ref, ccr_ref, cci_ref, ar_ref, ai_ref, apr_ref, api_ref, d_ref,
             y_ref, csr_ref, csi_ref, uf_ref, up_ref, dr_ref, di_ref, sr_ref, si_ref, yp_ref, car_ref, cai_ref):
        b = pl.program_id(1)

        @pl.when(b == 0)
        def _():
            car_ref[...] = jnp.zeros_like(car_ref)
            cai_ref[...] = jnp.zeros_like(cai_ref)

        csr_ref[0, 0] = car_ref[...]
        csi_ref[0, 0] = cai_ref[...]
        uf_ref[...] = u_ref[...].astype(F32)
        _s5_permute_in(uf_ref, up_ref)
        upb = up_ref[...].astype(BF16)
        dr_ref[...] = _dot(upb, bbr_ref[0]).reshape(S5_LS, S5_SEG, S5_NST)
        di_ref[...] = _dot(upb, bbi_ref[0]).reshape(S5_LS, S5_SEG, S5_NST)
        a_r = jnp.broadcast_to(ar_ref[0], (S5_SEG, S5_NST))
        a_i = jnp.broadcast_to(ai_ref[0], (S5_SEG, S5_NST))
        fin_r, fin_i = _s5_scan(a_r, a_i, dr_ref, di_ref, zeros8(), zeros8())
        cin_r, cin_i, ncr, nci = _s5_stitch(apr_ref[0], api_ref[0], fin_r, fin_i, car_ref[...], cai_ref[...])
        car_ref[...] = ncr
        cai_ref[...] = nci

        def store(i, nr, ni, sr, si):
            sr_ref[i] = nr
            si_ref[i] = ni

        _s5_scan(a_r, a_i, dr_ref, di_ref, cin_r, cin_i, store=store)
        s_r = sr_ref[...].reshape(S5_TB, S5_NST).astype(BF16)
        s_i = si_ref[...].reshape(S5_TB, S5_NST).astype(BF16)
        yp_ref[...] = _dot(s_r, ccr_ref[0]) - _dot(s_i, cci_ref[0]) + d_ref[0] * up_ref[...]
        _s5_permute_out(yp_ref, uf_ref)
        y_ref[...] = uf_ref[...].astype(BF16)

    tok, par_b, par_c, vec_s, vec_c = _s5_specs(nb, False)
    cs_spec = pl.BlockSpec((1, 1, 1, S5_NST), lambda c, b: (b, c, 0, 0))
    cs_shape = jax.ShapeDtypeStruct((nb, S5_CH, 1, S5_NST), F32)
    tokbuf = pltpu.VMEM((S5_TB, S5_CW), F32)
    stbuf = pltpu.VMEM((S5_LS, S5_SEG, S5_NST), F32)
    return _call(
        body, grid=(S5_CH, nb),
        in_specs=[tok, par_b, par_b, par_c, par_c, vec_s, vec_s, vec_s, vec_s, vec_c],
        out_specs=[tok, cs_spec, cs_spec],
        out_shape=[jax.ShapeDtypeStruct((T, D_S5), BF16), cs_shape, cs_shape],
        scratch_shapes=[tokbuf, tokbuf, stbuf, stbuf, stbuf, stbuf, tokbuf,
                        pltpu.VMEM((1, S5_NST), F32), pltpu.VMEM((1, S5_NST), F32)],
        sem=("parallel", "arbitrary"), name=name, comm=comm,
        args=(proj, bbr, bbi, ccr, cci, abr, abi, apr, api, dskip))


def s5_bwd(proj, dys, csr, csi, bbr, bbi, ccr, cci, abr, abi, apr, api, dskip, dproj, *, name, comm=None):
    T = proj.shape[0]
    nb = T // S5_TB
    zeros8 = functools.partial(jnp.zeros, (S5_SEG, S5_NST), F32)

    def body(u_ref, gy_ref, csr_ref, csi_ref, bbr_ref, bbi_ref, ccr_ref, cci_ref, ar_ref, ai_ref, apr_ref, api_ref,
             d_ref, dproj_ref, du_ref, dbr_ref, dbi_ref, dcr_ref, dci_ref, dd_ref, dar_ref, dai_ref,
             tmp_ref, up_ref, gyp_ref, dr_ref, di_ref, sr_ref, si_ref, gr_ref, gi_ref, car_ref, cai_ref):
        b = pl.program_id(1)

        @pl.when(b == 0)
        def _():
            car_ref[...] = jnp.zeros_like(car_ref)
            cai_ref[...] = jnp.zeros_like(cai_ref)
            for ref in (dbr_ref, dbi_ref, dcr_ref, dci_ref, dd_ref, dar_ref, dai_ref):
                ref[...] = jnp.zeros_like(ref)

        tmp_ref[...] = u_ref[...].astype(F32)
        _s5_permute_in(tmp_ref, up_ref)
        tmp_ref[...] = gy_ref[...].astype(F32)
        _s5_permute_in(tmp_ref, gyp_ref)
        upb = up_ref[...].astype(BF16)
        gyp = gyp_ref[...]
        gypb = gyp.astype(BF16)
        dr_ref[...] = _dot(upb, bbr_ref[0]).reshape(S5_LS, S5_SEG, S5_NST)
        di_ref[...] = _dot(upb, bbi_ref[0]).reshape(S5_LS, S5_SEG, S5_NST)
        a_r = jnp.broadcast_to(ar_ref[0], (S5_SEG, S5_NST))
        a_i = jnp.broadcast_to(ai_ref[0], (S5_SEG, S5_NST))
        fin_r, fin_i = _s5_scan(a_r, a_i, dr_ref, di_ref, zeros8(), zeros8())
        cin_r, cin_i, _, _ = _s5_stitch(apr_ref[0], api_ref[0], fin_r, fin_i, csr_ref[0, 0], csi_ref[0, 0])
        sr_ref[0] = cin_r
        si_ref[0] = cin_i

        def store_s(i, nr, ni, sr, si):
            sr_ref[i + 1] = nr
            si_ref[i + 1] = ni

        _s5_scan(a_r, a_i, dr_ref, di_ref, cin_r, cin_i, store=store_s)
        s_r = sr_ref[pl.ds(1, S5_LS)].reshape(S5_TB, S5_NST).astype(BF16)
        s_i = si_ref[pl.ds(1, S5_LS)].reshape(S5_TB, S5_NST).astype(BF16)
        dcr_ref[0] += _dot_tn(s_r, gypb)
        dci_ref[0] -= _dot_tn(s_i, gypb)
        dd_ref[0] += jnp.sum(gyp * up_ref[...], axis=0, keepdims=True)
        dr_ref[...] = _dot_nt(gypb, ccr_ref[0]).reshape(S5_LS, S5_SEG, S5_NST)
        di_ref[...] = (-_dot_nt(gypb, cci_ref[0])).reshape(S5_LS, S5_SEG, S5_NST)
        fin_r, fin_i = _s5_scan(a_r, a_i, dr_ref, di_ref, zeros8(), zeros8(), reverse=True, conj=True)
        gin_r, gin_i, ncr, nci = _s5_stitch(apr_ref[0], api_ref[0], fin_r, fin_i, car_ref[...], cai_ref[...],
                                            reverse=True, conj=True)
        car_ref[...] = ncr
        cai_ref[...] = nci
        def step_g(n, carry):
            gr, gi, acc_r, acc_i = carry
            i = S5_LS - 1 - n
            nr = a_r * gr + a_i * gi + dr_ref[i]
            ni = a_r * gi - a_i * gr + di_ref[i]
            gr_ref[i] = nr
            gi_ref[i] = ni
            pr, pi = sr_ref[i], si_ref[i]
            return nr, ni, acc_r + (nr * pr + ni * pi), acc_i + (ni * pr - nr * pi)

        _, _, acc_r, acc_i = lax.fori_loop(0, S5_LS, step_g, (gin_r, gin_i, zeros8(), zeros8()), unroll=2)
        dar_ref[0] += jnp.sum(acc_r, axis=0, keepdims=True)
        dai_ref[0] += jnp.sum(acc_i, axis=0, keepdims=True)
        g_r = gr_ref[...].reshape(S5_TB, S5_NST).astype(BF16)
        g_i = gi_ref[...].reshape(S5_TB, S5_NST).astype(BF16)
        dbr_ref[0] += _dot_tn(upb, g_r)
        dbi_ref[0] += _dot_tn(upb, g_i)
        gyp_ref[...] = _dot_nt(g_r, bbr_ref[0]) + _dot_nt(g_i, bbi_ref[0]) + d_ref[0] * gyp
        _s5_permute_out(gyp_ref, tmp_ref)
        du_ref[...] = tmp_ref[...].astype(BF16)

    tok, par_b, par_c, vec_s, vec_c = _s5_specs(nb, True)
    cs_spec = pl.BlockSpec((1, 1, 1, S5_NST), lambda c, b: (nb - 1 - b, c, 0, 0))
    tokbuf = pltpu.VMEM((S5_TB, S5_CW), F32)
    stbuf = pltpu.VMEM((S5_LS, S5_SEG, S5_NST), F32)
    stbuf1 = pltpu.VMEM((S5_LS + 1, S5_SEG, S5_NST), F32)
    return _call(
        body, grid=(S5_CH, nb), comm=comm,
        in_specs=[tok, tok, cs_spec, cs_spec, par_b, par_b, par_c, par_c, vec_s, vec_s, vec_s, vec_s, vec_c, _ANY],
        out_specs=[tok, par_b, par_b, par_c, par_c, vec_c, vec_s, vec_s], aliases={13: 0},
        out_shape=[jax.ShapeDtypeStruct(dproj.shape, BF16),
                   jax.ShapeDtypeStruct((S5_CH, S5_CW, S5_NST), F32), jax.ShapeDtypeStruct((S5_CH, S5_CW, S5_NST), F32),
                   jax.ShapeDtypeStruct((S5_CH, S5_NST, S5_CW), F32), jax.ShapeDtypeStruct((S5_CH, S5_NST, S5_CW), F32),
                   jax.ShapeDtypeStruct((S5_CH, 1, S5_CW), F32),
                   jax.ShapeDtypeStruct((S5_CH, 1, S5_NST), F32), jax.ShapeDtypeStruct((S5_CH, 1, S5_NST), F32)],
        scratch_shapes=[tokbuf, tokbuf, tokbuf, stbuf, stbuf, stbuf1, stbuf1, stbuf, stbuf,
                        pltpu.VMEM((1, S5_NST), F32), pltpu.VMEM((1, S5_NST), F32)],
        sem=("parallel", "arbitrary"), name=name,
        args=(proj, dys, csr, csi, bbr, bbi, ccr, cci, abr, abi, apr, api, dskip, dproj))


SSD_L = SSD_CHUNK
SSD_GW = 256
NEG = -1e30


def _expand16(v):
    lane = lax.broadcasted_iota(jnp.int32, (v.shape[0], 128), 1)
    parts = [jnp.where(lane < SSD_HEADDIM, v[:, 2 * j:2 * j + 1], v[:, 2 * j + 1:2 * j + 2]) for j in range(8)]
    return jnp.concatenate(parts, axis=1)


def _headsum(v, hsum):
    hi = v.astype(BF16)
    lo = (v - hi.astype(F32)).astype(BF16)
    return _dot(hi, hsum) + _dot(lo, hsum)


def _softplus(x):
    return jnp.maximum(x, 0.0) + jnp.log(1.0 + jnp.exp(-jnp.abs(x)))


def _ssd_chunk_fwd(z, xbc, tail, dtraw, hprev, cw, cb, dtb, alog, dsk, nw, tril):
    L = SSD_L
    f = {}
    xe = jnp.concatenate([tail, xbc], axis=0)
    sh = [xbc] + [pltpu.roll(xe, s, 0)[8:] for s in (1, 2, 3)]
    conv = cb + cw[3:4] * sh[0] + cw[2:3] * sh[1] + cw[1:2] * sh[2] + cw[0:1] * sh[3]
    sig = _sigmoid(conv)
    xa = conv * sig
    xs, bm, cm = xa[:, :D_SSD], xa[:, D_SSD:D_SSD + 512], xa[:, D_SSD + 512:]
    pre = dtraw + dtb
    dt = _softplus(pre)
    a_h = -jnp.exp(alog)
    acum = _dot_hi(tril, dt * a_h)
    acum_t = acum.T
    alast = acum[L - 1:L]
    exp_a = jnp.exp(acum)
    dec = jnp.exp(alast - acum)
    exp_al = jnp.exp(alast)
    dt_x, dec_x, exp_a_x, exp_al_x = _expand16(dt), _expand16(dec), _expand16(exp_a), _expand16(exp_al)
    d_x = _expand16(dsk)
    xh = xs * dt_x
    xhb = xh.astype(BF16)
    xd = (xh * dec_x).astype(BF16)
    row = lax.broadcasted_iota(jnp.int32, (L, L), 0)
    col = lax.broadcasted_iota(jnp.int32, (L, L), 1)
    causal = row >= col
    lane = lax.broadcasted_iota(jnp.int32, (L, 128), 1)
    low = lane < SSD_HEADDIM
    hb = hprev.astype(BF16)
    y_pairs, yoff_parts, st_parts, cbs, lms = [], [], [], [], []
    for g in range(SSD_NGROUPS):
        bg = bm[:, g * 128:(g + 1) * 128].astype(BF16)
        cg = cm[:, g * 128:(g + 1) * 128].astype(BF16)
        cbg = _dot_nt(cg, bg)
        cbs.append(cbg)
        for j in (2 * g, 2 * g + 1):
            xp = xhb[:, j * 128:(j + 1) * 128]
            ys = []
            for h in (2 * j, 2 * j + 1):
                lm = jnp.exp(jnp.where(causal, acum[:, h:h + 1] - acum_t[h:h + 1, :], NEG))
                lms.append(lm)
                ys.append(_dot((cbg * lm).astype(BF16), xp))
            y_pairs.append(jnp.where(low, ys[0], ys[1]))
        gs = slice(g * SSD_GW, (g + 1) * SSD_GW)
        yoff_parts.append(_dot(cg, hb[:, gs]) * exp_a_x[:, gs])
        st_parts.append(_dot_tn(bg, xd[:, gs]))
    yoff = jnp.concatenate(yoff_parts, axis=1)
    y = jnp.concatenate(y_pairs, axis=1) + yoff + d_x * xs
    hnew = exp_al_x * hprev + jnp.concatenate(st_parts, axis=1)
    sz = _sigmoid(z)
    gz = y * (z * sz)
    r = lax.rsqrt(jnp.mean(gz * gz, axis=-1, keepdims=True) + EPS)
    out = gz * r * nw
    f.update(sh=sh, conv=conv, sig=sig, xs=xs, bm=bm, cm=cm, pre=pre, dt=dt, a_h=a_h, exp_a_x=exp_a_x, dec_x=dec_x,
             exp_al=exp_al, exp_al_x=exp_al_x, dt_x=dt_x, d_x=d_x, xh=xh, xhb=xhb, xd=xd, causal=causal, low=low, hb=hb,
             cbs=cbs, lms=lms, yoff=yoff, y=y, sz=sz, gz=gz, r=r)
    return out, hnew, f


def _ssd_params(conv_w, conv_b, dt_bias, a_log, d_skip, norm_w):
    pad16 = lambda v: jnp.pad(v.reshape(1, SSD_HEADS), ((0, 0), (0, 128 - SSD_HEADS)))
    return (jnp.pad(conv_w, ((0, 8 - SSD_CONV), (0, 0))), conv_b.reshape(1, D_XBC), pad16(dt_bias), pad16(a_log),
            pad16(d_skip), norm_w.reshape(1, D_SSD))


def _ssd_param_specs():
    full = lambda shape: pl.BlockSpec(shape, lambda i: (0, 0))
    return [full((8, D_XBC)), full((1, D_XBC)), full((1, 128)), full((1, 128)), full((1, 128)), full((1, D_SSD))]


def ssd_fwd(proj, dtraw, params, tril, ymix, *, name, comm=None):
    T = proj.shape[0]
    nc = T // SSD_L

    def body(z_ref, x_ref, dt_ref, cw_ref, cb_ref, dtb_ref, al_ref, dsk_ref, nw_ref, tril_ref, ymix_ref,
             o_ref, hs_ref, h_ref, tail_ref):
        i = pl.program_id(0)

        @pl.when(i == 0)
        def _():
            h_ref[...] = jnp.zeros_like(h_ref)
            tail_ref[...] = jnp.zeros_like(tail_ref)

        xbc = x_ref[...].astype(F32)
        hprev = h_ref[...]
        hs_ref[0] = hprev
        out, hnew, _ = _ssd_chunk_fwd(z_ref[...].astype(F32), xbc, tail_ref[...], dt_ref[...], hprev, cw_ref[...],
                                      cb_ref[...], dtb_ref[...], al_ref[...], dsk_ref[...], nw_ref[...], tril_ref[...])
        o_ref[...] = out.astype(BF16)
        h_ref[...] = hnew
        tail_ref[...] = xbc[SSD_L - 8:]

    return _call(
        body, grid=(nc,),
        in_specs=[pl.BlockSpec((SSD_L, D_SSD), lambda i: (i, 1)), pl.BlockSpec((SSD_L, D_XBC), lambda i: (i, 1)),
                  pl.BlockSpec((SSD_L, 128), lambda i: (i, 0))] + _ssd_param_specs()
                 + [pl.BlockSpec((SSD_L, SSD_L), lambda i: (0, 0)), _ANY],
        out_specs=[pl.BlockSpec((SSD_L, D_SSD), lambda i: (i, 1)),
                   pl.BlockSpec((1, SSD_STATE, D_SSD), lambda i: (i, 0, 0))],
        out_shape=[jax.ShapeDtypeStruct(ymix.shape, BF16), jax.ShapeDtypeStruct((nc, SSD_STATE, D_SSD), F32)],
        scratch_shapes=[pltpu.VMEM((SSD_STATE, D_SSD), F32), pltpu.VMEM((8, D_XBC), F32)],
        sem=("arbitrary",), name=name, args=(proj, proj, dtraw, *params, tril, ymix), aliases={10: 0}, comm=comm)


def ssd_bwd(proj, dtraw, hs, dymix, params, tril, triu, trils, headsum, *, name, comm=None):
    T = proj.shape[0]
    nc = T // SSD_L
    L = SSD_L

    def body(z_ref, x_ref, xprev_ref, dt_ref, hs_ref, do_ref, cw_ref, cb_ref, dtb_ref, al_ref, dsk_ref, nw_ref,
             tril_ref, triu_ref, trils_ref, hsum_ref,
             dp_ref, ddt_ref, dcw_ref, dcb_ref, ddtb_ref, dal_ref, ddsk_ref, dnw_ref, dh_ref, dnext_ref):
        i = pl.program_id(0)

        @pl.when(i == 0)
        def _():
            dh_ref[...] = jnp.zeros_like(dh_ref)
            dnext_ref[...] = jnp.zeros_like(dnext_ref)
            for ref in (dcw_ref, dcb_ref, ddtb_ref, dal_ref, ddsk_ref, dnw_ref):
                ref[...] = jnp.zeros_like(ref)

        z = z_ref[...].astype(F32)
        xbc = x_ref[...].astype(F32)
        tail = jnp.where(i == nc - 1, 0.0, xprev_ref[...].astype(F32))
        hprev = hs_ref[0]
        cw, nw = cw_ref[...], nw_ref[...]
        hsum = hsum_ref[...]
        _, _, f = _ssd_chunk_fwd(z, xbc, tail, dt_ref[...], hprev, cw, cb_ref[...], dtb_ref[...], al_ref[...],
                                 dsk_ref[...], nw, tril_ref[...])
        dout = do_ref[...].astype(F32)
        dh = dh_ref[...]
        ghat = f["gz"] * f["r"]
        dn = dout * nw
        dgz = f["r"] * (dn - ghat * jnp.mean(dn * ghat, axis=-1, keepdims=True))
        dnw_ref[...] += jnp.sum(dout * ghat, axis=0, keepdims=True)
        sz = f["sz"]
        dy = dgz * (z * sz)
        dp_ref[:, :D_S5] = jnp.zeros((L, D_S5), BF16)
        dp_ref[:, D_S5:D_S5 + D_SSD] = (dgz * f["y"] * sz * (1.0 + z * (1.0 - sz))).astype(BF16)
        xs = f["xs"]
        ddsk_ref[...] += jnp.sum(_headsum(dy * xs, hsum), axis=0, keepdims=True)
        dyb = dy.astype(BF16)
        dye = (dy * f["exp_a_x"]).astype(BF16)
        dhb = dh.astype(BF16)
        lane = lax.broadcasted_iota(jnp.int32, (L, 128), 1)
        sub = lax.broadcasted_iota(jnp.int32, (128, L), 0)
        zero_b = jnp.zeros((L, 128), BF16)
        rsum = jnp.zeros((L, 128), F32)
        csum_t = jnp.zeros((128, L), F32)
        dx_pairs, dxst_parts, db_parts, dc_parts, dhp_parts = [], [], [], [], []
        for g in range(SSD_NGROUPS):
            gs = slice(g * SSD_GW, (g + 1) * SSD_GW)
            bg = f["bm"][:, g * 128:(g + 1) * 128].astype(BF16)
            cg = f["cm"][:, g * 128:(g + 1) * 128].astype(BF16)
            cbg = f["cbs"][g]
            dcb_g = jnp.zeros((L, L), F32)
            for j in (2 * g, 2 * g + 1):
                xp = f["xhb"][:, j * 128:(j + 1) * 128]
                dyp = dyb[:, j * 128:(j + 1) * 128]
                dxs = []
                for half, h in enumerate((2 * j, 2 * j + 1)):
                    lm = f["lms"][h]
                    dyh = jnp.where(f["low"], dyp, zero_b) if half == 0 else jnp.where(f["low"], zero_b, dyp)
                    dw = jnp.where(f["causal"], _dot_nt(dyh, xp), 0.0)
                    w = cbg * lm
                    e = dw * w
                    dcb_g = dcb_g + dw * lm
                    rsum = jnp.where(lane == h, jnp.sum(e, axis=1, keepdims=True), rsum)
                    csum_t = jnp.where(sub == h, jnp.sum(e, axis=0, keepdims=True), csum_t)
                    dxs.append(_dot_tn(w.astype(BF16), dyp))
                dx_pairs.append(jnp.where(f["low"], dxs[0], dxs[1]))
            dcbb = dcb_g.astype(BF16)
            dxst_parts.append(f["dec_x"][:, gs] * _dot(bg, dhb[:, gs]))
            dc_parts.append(_dot(dcbb, bg) + _dot_nt(dye[:, gs], f["hb"][:, gs]))
            db_parts.append(_dot_tn(dcbb, cg) + _dot_nt(f["xd"][:, gs], dhb[:, gs]))
            dhp_parts.append(f["exp_al_x"][:, gs] * dh[:, gs] + _dot_tn(cg, dye[:, gs]))
        dxst = jnp.concatenate(dxst_parts, axis=1)
        dxh = jnp.concatenate(dx_pairs, axis=1) + dxst
        q = _headsum(f["yoff"] * dy, hsum)
        dstate = _headsum(f["xh"] * dxst, hsum)
        h0t = jnp.sum(_headsum(dh * hprev, hsum), axis=0, keepdims=True) * f["exp_al"]
        da = _dot_hi(triu_ref[...], rsum - csum_t.T + q) + _dot_hi(trils_ref[...], dstate) + h0t
        dt, a_h = f["dt"], f["a_h"]
        ddt = _headsum(dxh * xs, hsum) + da * a_h
        dal_ref[...] += jnp.sum(da * dt, axis=0, keepdims=True) * a_h
        ddtraw = ddt * _sigmoid(f["pre"])
        first16 = lane < SSD_HEADS
        ddtraw = jnp.where(first16, ddtraw, 0.0)
        ddt_ref[...] = ddtraw
        ddtb_ref[...] += jnp.sum(ddtraw, axis=0, keepdims=True)
        dh_ref[...] = jnp.concatenate(dhp_parts, axis=1)
        dxa = jnp.concatenate([dxh * f["dt_x"] + f["d_x"] * dy] + db_parts + dc_parts, axis=1)
        sig, conv = f["sig"], f["conv"]
        dconv = dxa * sig * (1.0 + conv * (1.0 - sig))
        dcb_ref[...] += jnp.sum(dconv, axis=0, keepdims=True)
        rows = [jnp.sum(dconv * f["sh"][3 - k], axis=0, keepdims=True) for k in range(SSD_CONV)]
        dcw_ref[...] += jnp.concatenate(rows + [jnp.zeros((8 - SSD_CONV, D_XBC), F32)], axis=0)
        de = jnp.concatenate([dconv, dnext_ref[...]], axis=0)
        dxbc = cw[3:4] * dconv
        for s in (1, 2, 3):
            dxbc = dxbc + cw[3 - s:4 - s] * pltpu.roll(de, L + 8 - s, 0)[:L]
        dp_ref[:, D_S5 + D_SSD:] = dxbc.astype(BF16)
        dnext_ref[...] = dconv[:8]

    rev = lambda i: nc - 1 - i
    acc = lambda shape: pl.BlockSpec(shape, lambda i: (0, 0))
    tri = pl.BlockSpec((L, L), lambda i: (0, 0))
    return _call(
        body, grid=(nc,),
        in_specs=[pl.BlockSpec((L, D_SSD), lambda i: (rev(i), 1)), pl.BlockSpec((L, D_XBC), lambda i: (rev(i), 1)),
                  pl.BlockSpec((8, D_XBC), lambda i: (jnp.maximum(rev(i) * (L // 8) - 1, 0), 1)),
                  pl.BlockSpec((L, 128), lambda i: (rev(i), 0)),
                  pl.BlockSpec((1, SSD_STATE, D_SSD), lambda i: (rev(i), 0, 0)),
                  pl.BlockSpec((L, D_SSD), lambda i: (rev(i), 1))] + _ssd_param_specs()
                 + [tri, tri, tri, pl.BlockSpec((D_SSD, 128), lambda i: (0, 0))],
        out_specs=[pl.BlockSpec((L, D_MAIN), lambda i: (rev(i), 0)), pl.BlockSpec((L, 128), lambda i: (rev(i), 0)),
                   acc((8, D_XBC)), acc((1, D_XBC)), acc((1, 128)), acc((1, 128)), acc((1, 128)), acc((1, D_SSD))],
        out_shape=[jax.ShapeDtypeStruct((T, D_MAIN), BF16),
                   jax.ShapeDtypeStruct((T, 128), F32), jax.ShapeDtypeStruct((8, D_XBC), F32),
                   jax.ShapeDtypeStruct((1, D_XBC), F32), jax.ShapeDtypeStruct((1, 128), F32),
                   jax.ShapeDtypeStruct((1, 128), F32), jax.ShapeDtypeStruct((1, 128), F32),
                   jax.ShapeDtypeStruct((1, D_SSD), F32)],
        scratch_shapes=[pltpu.VMEM((SSD_STATE, D_SSD), F32), pltpu.VMEM((8, D_XBC), F32)],
        sem=("arbitrary",), name=name, comm=comm,
        args=(proj, proj, proj, dtraw, hs, dymix, *params, tril, triu, trils, headsum))


def _s5_blockdiag(v, rows_per_group, cols_per_group):
    eye = jnp.eye(S5_SEG, dtype=v.dtype)
    w = v[:, :, :, None, :] * eye[None, :, None, :, None]
    return w.reshape(S5_CH, 8 * rows_per_group, 8 * cols_per_group)


def _s5_blockdiag_extract(w, rows_per_group, cols_per_group):
    eye = jnp.eye(S5_SEG, dtype=w.dtype)
    w5 = w.reshape(S5_CH, 8, rows_per_group, 8, cols_per_group)
    return jnp.sum(w5 * eye[None, :, None, :, None], axis=3)


TM = 512

OFF = dict(mlp_w2=0, mlp_w1=1024, w_out=2048, s5_w_glu=2560, xa_wq=2816, xa_wk=3072, xa_wv=3328, xa_wo=3584, w_in=3840)
ROWS = dict(mlp_w2=1024, mlp_w1=1024, w_out=512, s5_w_glu=256, xa_wq=256, xa_wk=256, xa_wv=256, xa_wo=256, w_in=1028)
TAIL_OFF = 4872
TAIL_ROWS = 120
PACK_ROWS = TAIL_OFF + TAIL_ROWS
HALF_ROWS = PACK_ROWS // 2
ROWS_ALL = (0, PACK_ROWS, 832, True)
ROWS_EARLY = (0, OFF["w_in"], 640, False)
ROWS_LATE = (OFF["w_in"], PACK_ROWS - OFF["w_in"], 192, True)
N_SHARD = 4
SMALL_L = ("norm_mix", "s5_a_re", "s5_a_im", "s5_log_dt", "s5_b_re", "s5_b_im", "s5_c_re", "s5_c_im", "s5_d",
           "ssd_conv_w", "ssd_conv_b", "ssd_dt_bias", "ssd_a_log", "ssd_d", "ssd_norm", "norm_xattn", "norm_mem",
           "norm_mlp")
SMALL_Q = 72
CONV_ROWS = 8


def _place():
    x, y, c = lax.axis_index("x"), lax.axis_index("y"), lax.axis_index("c")
    chips = [(1 - x, y), (x, 1 - y), (1 - x, 1 - y)]
    return x, y, c, 2 * x + y, chips, (x, y, 1 - c)


def _remote(src, dst, send_sem, recv_sem, to):
    return pltpu.make_async_remote_copy(src_ref=src, dst_ref=dst, send_sem=send_sem, recv_sem=recv_sem,
                                        device_id=to, device_id_type=MESH_T)


def _half(ref, c):
    return ref.at[pl.ds(pl.multiple_of(c * HALF_ROWS, 16), HALF_ROWS), :]


def _dma_sems(*counts):
    return [pltpu.SemaphoreType.DMA((n,)) for n in counts]


def place_own(wpack, shard_idx):
    tile = PACK_ROWS // 4

    def body(s_ref, w_ref, o_ref):
        o_ref[0] = w_ref[...]

    return pl.pallas_call(
        body,
        grid_spec=pltpu.PrefetchScalarGridSpec(
            num_scalar_prefetch=1, grid=(4,),
            in_specs=[pl.BlockSpec((tile, D_MODEL), lambda i, s_ref: (i, 0))],
            out_specs=pl.BlockSpec((1, tile, D_MODEL), lambda i, s_ref: (s_ref[0], i, 0))),
        out_shape=jax.ShapeDtypeStruct((N_SHARD,) + wpack.shape, wpack.dtype),
        compiler_params=_cp("parallel"), name="place_own")(shard_idx, wpack)


def gather_over_ici(wpack, placed):
    def copies(ci, co, cs):
        w_ref, (out_ref,), (send, recv) = ci[0], co, cs
        x, y, c, s, chips, sibling = _place()
        sends = [_remote(_half(w_ref, c), _half(out_ref.at[s], c), send.at[j], recv.at[j], (*chip, c))
                 for j, chip in enumerate(chips)]
        lands = [_half(out_ref.at[2 * chip[0] + chip[1]], c) for chip in chips]
        arrivals = [_remote(land, land, send.at[j], recv.at[j], sibling) for j, land in enumerate(lands)]
        return sends, arrivals

    def start(ci, co, cs):
        for cp in copies(ci, co, cs)[0]:
            cp.start()

    def wait(ci, co, cs):
        sends, arrivals = copies(ci, co, cs)
        for cp in arrivals:
            cp.wait_recv()
        for cp in sends:
            cp.wait_send()

    return Comm("gather_ici", [wpack, placed], [jax.ShapeDtypeStruct(placed.shape, placed.dtype)], _dma_sems(3, 3),
                start, wait, aliases={1: 0})


def gather_to_sibling(landed):
    def copies(co, cs):
        (out_ref,), (send, recv) = co, cs
        x, y, c, s, chips, sibling = _place()
        slots = [out_ref.at[2 * chip[0] + chip[1]] for chip in chips]
        sends = [_remote(_half(slot, c), _half(slot, c), send.at[j], recv.at[j], sibling) for j, slot in enumerate(slots)]
        arrivals = [_remote(_half(slot, 1 - c), _half(slot, 1 - c), send.at[j], recv.at[j], sibling)
                    for j, slot in enumerate(slots)]
        return sends, arrivals

    def start(ci, co, cs):
        for cp in copies(co, cs)[0]:
            cp.start()

    def wait(ci, co, cs):
        sends, arrivals = copies(co, cs)
        for cp in arrivals:
            cp.wait_recv()
        for cp in sends:
            cp.wait_send()

    return Comm("gather_d2d", [landed], [jax.ShapeDtypeStruct(landed.shape, landed.dtype)], _dma_sems(3, 3), start, wait,
                aliases={0: 0})


def _range_half(ref, rows, c):
    half = rows[1] // 2
    return pl.ds(pl.multiple_of(rows[0] + c * half, 16), half)


def exchange_halves(gpack, rows):
    def copy(ci, co, cs):
        (g_ref,), (out_ref,), (send, recv) = ci, co, cs
        x, y, c, s, chips, sibling = _place()
        return _remote(g_ref.at[:, _range_half(g_ref, rows, 1 - c), :], out_ref, send.at[0], recv.at[0], sibling)

    return Comm("exchange", [gpack], [jax.ShapeDtypeStruct((N_SHARD, rows[1] // 2, D_MODEL), F32)], _dma_sems(1, 1),
                lambda ci, co, cs: copy(ci, co, cs).start(), lambda ci, co, cs: copy(ci, co, cs).wait())


def scatter_chips(csum):
    def copies(ci, co, cs):
        (c_ref,), (out_ref,), (send, recv) = ci, co, cs
        x, y, c, s, chips, sibling = _place()
        sends = [_remote(c_ref.at[2 * chip[0] + chip[1]], out_ref.at[s], send.at[j], recv.at[j], (*chip, c))
                 for j, chip in enumerate(chips)]
        arrivals = [_remote(c_ref.at[2 * chip[0] + chip[1]], out_ref.at[2 * chip[0] + chip[1]], send.at[j], recv.at[j],
                            (*chip, c)) for j, chip in enumerate(chips)]
        return sends, arrivals

    def start(ci, co, cs):
        for cp in copies(ci, co, cs)[0]:
            cp.start()

    def wait(ci, co, cs):
        sends, arrivals = copies(ci, co, cs)
        for cp in arrivals:
            cp.wait_recv()
        for cp in sends:
            cp.wait_send()

    return Comm("scatter", [csum], [jax.ShapeDtypeStruct(csum.shape, csum.dtype)], _dma_sems(3, 3), start, wait)


def share_reduced(gshards, smalls, l, rows):
    with_tail = rows[3]

    def copies(ci, co, cs):
        (g_ref, sm_ref), (send, recv, loc) = co, cs
        x, y, c, s, chips, sibling = _place()
        my_half = g_ref.at[l, _range_half(g_ref, rows, c), :]
        tail = g_ref.at[l, pl.ds(PACK_ROWS - TAIL_ROWS, TAIL_ROWS), :]
        big = _remote(my_half, my_half, send.at[0], recv.at[0], sibling)
        keep_tail = pltpu.make_async_copy(tail, sm_ref.at[l, s], loc.at[0])
        tails = [_remote(tail, sm_ref.at[l, s], send.at[1 + j], recv.at[1 + j], (*chip, 1)) for j, chip in enumerate(chips)]
        tails += [_remote(tail, sm_ref.at[l, s], send.at[4 + j], recv.at[1 + j], (*chip, 0)) for j, chip in enumerate(chips)]
        tails.append(_remote(tail, sm_ref.at[l, s], send.at[7], recv.at[4], sibling))
        other = g_ref.at[l, _range_half(g_ref, rows, 1 - c), :]
        big_in = _remote(other, other, send.at[0], recv.at[0], sibling)
        slots = [sm_ref.at[l, 2 * chip[0] + chip[1]] for chip in chips]
        tails_in = [_remote(slot, slot, send.at[1 + j], recv.at[1 + j], sibling) for j, slot in enumerate(slots)]
        sib_tail_in = _remote(sm_ref.at[l, s], sm_ref.at[l, s], send.at[7], recv.at[4], sibling)
        return c, big, keep_tail, tails, big_in, tails_in, sib_tail_in

    def start(ci, co, cs):
        c, big, keep_tail, tails, _, _, _ = copies(ci, co, cs)
        big.start()
        if with_tail:
            @pl.when(c == 1)
            def _():
                keep_tail.start()
                for cp in tails:
                    cp.start()

    def wait(ci, co, cs):
        c, big, keep_tail, tails, big_in, tails_in, sib_tail_in = copies(ci, co, cs)
        big_in.wait_recv()
        big.wait_send()
        if with_tail:
            for cp in tails_in:
                cp.wait_recv()

            @pl.when(c == 0)
            def _():
                sib_tail_in.wait_recv()

            @pl.when(c == 1)
            def _():
                for cp in tails:
                    cp.wait_send()
                keep_tail.wait()

    sds = lambda a: jax.ShapeDtypeStruct(a.shape, a.dtype)
    return Comm("share", [gshards, smalls], [sds(gshards), sds(smalls)], _dma_sems(8, 5, 1), start, wait,
                aliases={0: 0, 1: 1})


def _consts():
    e = np.zeros((S5_STATE, S5_STATE * S5_GROUP), np.float32)
    for p in range(S5_STATE):
        e[p, p * S5_GROUP:(p + 1) * S5_GROUP] = 1.0
    hs = np.zeros((D_SSD, 128), np.float32)
    for h in range(SSD_HEADS):
        hs[h * SSD_HEADDIM:(h + 1) * SSD_HEADDIM, h] = 1.0
    ones = np.ones((SSD_L, SSD_L), np.float32)
    return dict(expand=jnp.asarray(e), expand_t=jnp.asarray(e.T), headsum=jnp.asarray(hs).astype(BF16),
                tril=jnp.asarray(np.tril(ones)), triu=jnp.asarray(np.triu(ones)), trils=jnp.asarray(np.tril(ones, -1)))


def _s5_mats(w, cst):
    b_re = w["s5_b_re"].reshape(S5_GROUPS, S5_STATE * S5_GROUP)
    b_im = w["s5_b_im"].reshape(S5_GROUPS, S5_STATE * S5_GROUP)
    abr, abi, apr, api, bbr, bbi = s5_params_fwd(w["s5_a_re"], w["s5_a_im"], w["s5_log_dt"].reshape(S5_GROUPS, 1),
                                                 b_re, b_im, cst["expand"], name="s5_params_fwd")
    t = lambda v: v.reshape(S5_CH, 8, S5_STATE, S5_GROUP).transpose(0, 1, 3, 2)
    c4 = lambda v: v.reshape(S5_CH, 8, S5_GROUP, S5_STATE).transpose(0, 1, 3, 2)
    vec = lambda v: v.reshape(S5_CH, 1, S5_NST)
    return dict(bbr=_s5_blockdiag(t(bbr), S5_GROUP, S5_STATE).astype(BF16),
                bbi=_s5_blockdiag(t(bbi), S5_GROUP, S5_STATE).astype(BF16),
                ccr=_s5_blockdiag(c4(w["s5_c_re"]), S5_STATE, S5_GROUP).astype(BF16),
                cci=_s5_blockdiag(c4(w["s5_c_im"]), S5_STATE, S5_GROUP).astype(BF16),
                abr=vec(abr), abi=vec(abi), apr=vec(apr), api=vec(api),
                dsk=w["s5_d"].reshape(S5_CH, 1, S5_CW), b_re=b_re, b_im=b_im)


def _layer_fwd(x, mem, w, cst, next_pack=None):
    sv = {}
    g = lambda n: w[n].reshape(1, -1)
    pack = w["pack"]
    proj, h0, r0, dtraw = norm_matmul(x, g("norm_mix"), w["w_in_t"], w["w_dt_t"], tm=TM, tn=1024, name="in_proj",
                                      w_transposed=True)
    s5m = _s5_mats(w, cst)
    (ys, csr, csi), landed = s5_fwd(proj, s5m["bbr"], s5m["bbi"], s5m["ccr"], s5m["cci"], s5m["abr"], s5m["abi"],
                                    s5m["apr"], s5m["api"], s5m["dsk"], name="s5_fwd",
                                    comm=None if next_pack is None else gather_over_ici(*next_pack))
    (ymix, tglu), gathered = glu_fwd(ys, w["s5_w_glu"], tm=TM, name="glu_fwd",
                                     comm=None if next_pack is None else gather_to_sibling(landed[0]))
    ssdp = _ssd_params(w["ssd_conv_w"], w["ssd_conv_b"], w["ssd_dt_bias"], w["ssd_a_log"], w["ssd_d"], w["ssd_norm"])
    (ymix, hs), _ = ssd_fwd(proj, dtraw, ssdp, cst["tril"], ymix, name="ssd_fwd")
    x1 = matmul_res(ymix, pack, x, tm=TM, tn=1024, tk=1024, name="out_proj",
                    wspec=pl.BlockSpec((2, 512, 1024), lambda i, j, k: (k, OFF["w_out"] // 512, j)))
    q, h1, r1 = norm_matmul(x1, g("norm_xattn"), w["xa_wq"], tm=TM, tn=1024, name="q_proj")
    kv, hm, rm = norm_matmul(mem, g("norm_mem"), w["xa_wkv"], tm=mem.shape[0], tn=1024, name="kv_proj")
    o = attn_fwd(q, kv, tm=TM, name="attn_fwd")
    x2 = matmul_res(o, w["xa_wo"], x1, tm=TM, tn=1024, tk=1024, name="attn_out")
    f1, h2, r2 = norm_matmul(x2, g("norm_mlp"), pack, tm=TM, tn=1024, name="mlp_up", n_out=D_FF,
                             wspec=pl.BlockSpec((1, D_MODEL, 1024), lambda i, j: (j, OFF["mlp_w1"] // 1024, 0)))
    x3 = matmul_res(f1, pack, x2, act="relu2", tm=TM, tn=1024, tk=1024, name="mlp_down",
                    wspec=pl.BlockSpec((1, 1024, 1024), lambda i, j, k: (k, OFF["mlp_w2"] // 1024, j)))
    sv.update(x=x, proj=proj, h0=h0, r0=r0, dtraw=dtraw, s5m=s5m, ys=ys, csr=csr, csi=csi, tglu=tglu, ssdp=ssdp,
              hs=hs, ymix=ymix, x1=x1, q=q, h1=h1, r1=r1, kv=kv, hm=hm, rm=rm, o=o, x2=x2, f1=f1, h2=h2, r2=r2)
    return x3, sv, (gathered[0] if next_pack is not None else None)


def _layer_bwd(dx3, mem, w, sv, cst, extra_small=None, reduce_hooks=None, early=None):
    gr = {}
    g = lambda n: w[n].reshape(1, -1)
    pack = w["pack"]
    pshape = (N_SHARD, PACK_ROWS, D_MODEL)
    ps = lambda rows, f: pl.BlockSpec((1, rows, 1024), f)
    ps4 = lambda rows, f: pl.BlockSpec((N_SHARD, rows, 1024), f)
    dh1 = matmul_nt(dx3, pack, epi="relu2bwd", epi_args=(sv["f1"],), tm=TM, tko=1024, tn=1024, name="mlp_down_dx",
                    wspec=ps(1024, lambda i, k, n: (k, OFF["mlp_w2"] // 1024, n)), k_out=D_FF,
                    comm=reduce_hooks.exchange() if reduce_hooks else None)
    if reduce_hooks:
        dh1, recv = dh1
        reduce_hooks.after_exchange(recv[0])
    gp = matmul_tn(sv["f1"], dx3, act="relu2", tk=1024, tn=1024, tt=TM, name="mlp_down_dw", pack_shape=pshape,
                   pack_spec=ps(1024, lambda k, n, t: (k, OFF["mlp_w2"] // 1024, 0)))
    gp = matmul_tn(sv["h2"], dh1, tk=1024, tn=1024, tt=TM, name="mlp_up_dw", pack=gp, pack_shape=pshape,
                   pack_spec=ps(1024, lambda k, n, t: (n, OFF["mlp_w1"] // 1024, 0)))
    dx2, gr["norm_mlp"] = matmul_nt(dh1, pack, epi="rmsbwd", epi_args=(sv["x2"], sv["r2"], g("norm_mlp"), dx3),
                                    tm=TM, tko=1024, tn=1024, name="mlp_up_dx", k_out=D_MODEL,
                                    wspec=ps(1024, lambda i, k, n: (n, OFF["mlp_w1"] // 1024, 0)))
    do = matmul_nt(dx2, w["xa_wo"], tm=TM, tko=1024, tn=1024, name="attn_out_dx")
    gp = matmul_tn(sv["o"], dx2, tk=1024, tn=1024, tt=TM, name="attn_out_dw", pack=gp, pack_shape=pshape,
                   pack_spec=ps4(256, lambda k, n, t: (0, OFF["xa_wo"] // 256, 0)))
    dq, dkv = attn_bwd(sv["q"], sv["kv"], do, tm=TM, name="attn_bwd")
    gp = matmul_tn(sv["h1"], dq, tk=1024, tn=1024, tt=TM, name="q_proj_dw", pack=gp, pack_shape=pshape,
                   pack_spec=ps4(256, lambda k, n, t: (0, OFF["xa_wq"] // 256, 0)))
    dx1, gr["norm_xattn"] = matmul_nt(dq, w["xa_wq"], epi="rmsbwd", epi_args=(sv["x1"], sv["r1"], g("norm_xattn"), dx2),
                                      tm=TM, tko=1024, tn=1024, name="q_proj_dx")
    M = mem.shape[0]
    gp = matmul_tn(sv["hm"], dkv, tk=1024, tn=1024, tt=M, name="kv_proj_dw", pack=gp, pack_shape=pshape,
                   pack_spec=ps4(256, lambda k, n, t: (0, OFF["xa_wk"] // 256 + n, 0)))
    _, gr["norm_mem"] = matmul_nt(dkv, w["xa_wkv"], epi="rmsbwd",
                                  epi_args=(mem, sv["rm"], g("norm_mem"), jnp.zeros_like(mem)),
                                  tm=M, tko=1024, tn=1024, name="kv_proj_dx")
    dymix = matmul_nt(dx1, pack, tm=TM, tko=1024, tn=1024, name="out_proj_dx", k_out=2 * D_MODEL,
                      wspec=pl.BlockSpec((2, 512, 1024), lambda i, k, n: (k, OFF["w_out"] // 512, n)))
    gp = matmul_tn(sv["ymix"], dx1, tk=2048, tn=1024, tt=TM, name="out_proj_dw", pack=gp, pack_shape=pshape,
                   pack_spec=ps4(512, lambda k, n, t: (0, OFF["w_out"] // 512, 0)))
    (dproj, ddtraw, dcw, dcb, ddtb, dal, ddsk, dnw), parts = ssd_bwd(
        sv["proj"], sv["dtraw"], sv["hs"], dymix, sv["ssdp"], cst["tril"], cst["triu"], cst["trils"], cst["headsum"],
        name="ssd_bwd", comm=reduce_hooks.scatter() if reduce_hooks else None)
    if reduce_hooks:
        reduce_hooks.after_scatter(parts[0])
    gr["ssd_conv_w"], gr["ssd_conv_b"] = dcw[:SSD_CONV], dcb[0]
    gr["ssd_dt_bias"], gr["ssd_a_log"], gr["ssd_d"] = ddtb[0, :SSD_HEADS], dal[0, :SSD_HEADS], ddsk[0, :SSD_HEADS]
    gr["ssd_norm"] = dnw[0]
    dtg, da1 = glu_bwd_pre(dymix, sv["ys"], sv["tglu"], tm=TM, name="glu_bwd_pre")
    gp = matmul_tn(sv["ys"], dtg, act="gelu", tk=1024, tn=1024, tt=TM, name="glu_dw", pack=gp, pack_shape=pshape,
                   pack_spec=ps4(256, lambda k, n, t: (0, OFF["s5_w_glu"] // 256, 0)))
    first = _Reduction(gp, early[0], early[1], None, None, ROWS_EARLY) if early else None
    comm = _combine(reduce_hooks.share() if reduce_hooks else None, first.exchange() if first else None)
    dys = matmul_nt(dtg, w["s5_w_glu"], epi="glubwd", epi_args=(da1, sv["ys"]), tm=TM, tko=1024, tn=1024,
                    name="glu_dx", comm=comm)
    if comm is not None:
        dys, outs = dys
        if reduce_hooks:
            reduce_hooks.after_share(outs[:2])
            outs = outs[2:]
        if first:
            first.gshards, first.smalls = reduce_hooks.gshards, reduce_hooks.smalls
            first.after_exchange(outs[0])
    s5m = sv["s5m"]
    (dproj, dbbr, dbbi, dccr, dcci, dd, dabr, dabi), parts = s5_bwd(
        sv["proj"], dys, sv["csr"], sv["csi"], s5m["bbr"], s5m["bbi"], s5m["ccr"], s5m["cci"], s5m["abr"], s5m["abi"],
        s5m["apr"], s5m["api"], s5m["dsk"], dproj, name="s5_bwd", comm=first.scatter() if first else None)
    if first:
        first.after_scatter(parts[0])
    tb = lambda v: _s5_blockdiag_extract(v, S5_GROUP, S5_STATE).transpose(0, 1, 3, 2).reshape(S5_GROUPS, -1)
    tc = lambda v: _s5_blockdiag_extract(v, S5_STATE, S5_GROUP).transpose(0, 1, 3, 2).reshape(S5_GROUPS, S5_GROUP, S5_STATE)
    gr["s5_c_re"], gr["s5_c_im"] = tc(dccr), tc(dcci)
    gr["s5_d"] = dd.reshape(S5_GROUPS, S5_GROUP)
    dar, dai, dld, dbr, dbi = s5_params_bwd(
        w["s5_a_re"], w["s5_a_im"], w["s5_log_dt"].reshape(S5_GROUPS, 1), s5m["b_re"], s5m["b_im"],
        dabr.reshape(S5_GROUPS, S5_STATE), dabi.reshape(S5_GROUPS, S5_STATE), tb(dbbr), tb(dbbi),
        cst["expand"], cst["expand_t"], name="s5_params_bwd")
    gr["s5_a_re"], gr["s5_a_im"], gr["s5_log_dt"] = dar, dai, dld[:, 0]
    gr["s5_b_re"] = dbr.reshape(S5_GROUPS, S5_STATE, S5_GROUP)
    gr["s5_b_im"] = dbi.reshape(S5_GROUPS, S5_STATE, S5_GROUP)
    wt_shape = (D_MAIN + D_DT_PAD, D_MODEL)
    dwt = matmul_tn(dproj, sv["h0"], tk=1024, tn=1024, tt=TM, name="in_proj_dw", pack_shape=wt_shape,
                    pack_spec=pl.BlockSpec((1024, 1024), lambda k, n, t: (k, 0)))
    dwt = matmul_tn(ddtraw, sv["h0"], tk=D_DT_PAD, tn=1024, tt=TM, name="in_proj_dt_dw", pack=dwt, pack_shape=wt_shape,
                    pack_spec=pl.BlockSpec((D_DT_PAD, 1024), lambda k, n, t: (D_MAIN // D_DT_PAD, 0)))
    dx0 = matmul_nt(dproj, w["w_in_t"], g2=ddtraw, w2=w["w_dt_t"], w_is_nk=True, epi="rmsbwd",
                    epi_args=(sv["x"], sv["r0"], g("norm_mix"), dx1), tm=TM, tko=1024, tn=1024, name="in_proj_dx",
                    comm=first.share() if first else None)
    if first:
        dx0, shared = dx0
        first.after_share(shared)
    dx0, gr["norm_mix"] = dx0
    gr = {k: (v[0] if k.startswith("norm_") else v) for k, v in gr.items()}
    for t in range(N_SHARD):
        shard_rows = lax.slice_in_dim(dwt, t * ROWS["w_in"], (t + 1) * ROWS["w_in"], axis=0)
        gp = lax.dynamic_update_slice(gp, shard_rows[None], (t, OFF["w_in"], 0))
    small = [gr[n].reshape(-1) for n in SMALL_L] + ([] if extra_small is None else [extra_small.reshape(-1)])
    small = jnp.concatenate(small)
    small = jnp.pad(small, (0, N_SHARD * SMALL_Q * D_MODEL - small.size)).reshape(N_SHARD, SMALL_Q, D_MODEL)
    gap = TAIL_OFF - OFF["w_in"] - ROWS["w_in"]
    gp = lax.dynamic_update_slice(gp, jnp.pad(small, ((0, 0), (gap, TAIL_ROWS - SMALL_Q), (0, 0))),
                                  (0, TAIL_OFF - gap, 0))
    return dx0, gp, first


def _local_step(x, mem, target, layers, norm_final):
    cst = _consts()
    saved = []
    for l in range(DEPTH):
        x, sv, _ = _layer_fwd(x, mem, layers[l], cst)
        saved.append(sv)
    loss, dx, dgf = loss_head(x, norm_final.reshape(1, -1), target, tm=TM, name="loss_head")
    packs = [None] * DEPTH
    for l in reversed(range(DEPTH)):
        dx, packs[l], _ = _layer_bwd(dx, mem, layers[l], saved[l], cst, extra_small=dgf[0] if l == DEPTH - 1 else None)
    return loss, dx, packs


def sum_halves(gpack, recv, c_idx, rows):
    first, count, tile, _ = rows
    half = count // 2
    nb = half // tile

    def body(c_ref, a_ref, b_ref, o_ref):
        o_ref[...] = (a_ref[...] + b_ref[...]).astype(BF16)

    blk = (1, tile, D_MODEL)
    return pl.pallas_call(
        body,
        grid_spec=pltpu.PrefetchScalarGridSpec(
            num_scalar_prefetch=1, grid=(N_SHARD, nb),
            in_specs=[pl.BlockSpec(blk, lambda t, i, c_ref: (t, first // tile + c_ref[0] * nb + i, 0)),
                      pl.BlockSpec(blk, lambda t, i, c_ref: (t, i, 0))],
            out_specs=pl.BlockSpec(blk, lambda t, i, c_ref: (t, i, 0))),
        out_shape=jax.ShapeDtypeStruct((N_SHARD, half, D_MODEL), BF16),
        compiler_params=_cp("parallel", "parallel"), name="sum_halves")(c_idx, gpack, recv)


def sum_chips(parts, csum, gshards, l, place_idx, rows):
    first, count, tile, _ = rows
    nb = count // 2 // tile

    def body(pi_ref, p0, p1, p2, p3, own, g_ref, o_ref):
        s = pi_ref[0]
        vals = [jnp.where(s == k, own[0], p[0]).astype(F32) for k, p in enumerate((p0, p1, p2, p3))]
        o_ref[0] = ((vals[0] + vals[1]) + vals[2]) + vals[3]

    blk = (1, tile, D_MODEL)
    part_spec = lambda k: pl.BlockSpec(blk, lambda i, pi_ref: (jnp.where(pi_ref[0] == k, (k + 1) % N_SHARD, k), i, 0))
    return pl.pallas_call(
        body,
        grid_spec=pltpu.PrefetchScalarGridSpec(
            num_scalar_prefetch=1, grid=(nb,),
            in_specs=[part_spec(k) for k in range(N_SHARD)]
                     + [pl.BlockSpec(blk, lambda i, pi_ref: (pi_ref[0], i, 0)), _ANY],
            out_specs=pl.BlockSpec(blk, lambda i, pi_ref: (l, first // tile + pi_ref[1] * nb + i, 0))),
        out_shape=jax.ShapeDtypeStruct(gshards.shape, F32), input_output_aliases={6: 0},
        compiler_params=_cp("parallel"), name="sum_chips")(place_idx, parts, parts, parts, parts, csum, gshards)


class _Reduction:
    def __init__(self, gpack, layer, place_idx, gshards, smalls, rows=ROWS_ALL):
        self.gpack, self.layer, self.place_idx, self.gshards, self.smalls = gpack, layer, place_idx, gshards, smalls
        self.rows = rows

    def exchange(self):
        return exchange_halves(self.gpack, self.rows)

    def after_exchange(self, recv):
        self.csum = sum_halves(self.gpack, recv, self.place_idx[1:], self.rows)

    def scatter(self):
        return scatter_chips(self.csum)

    def after_scatter(self, parts):
        self.gshards = sum_chips(parts, self.csum, self.gshards, self.layer, self.place_idx, self.rows)

    def share(self):
        return share_reduced(self.gshards, self.smalls, self.layer, self.rows)

    def after_share(self, shared):
        self.gshards, self.smalls = shared

    def run_alone(self):
        self.after_exchange(_comm_only(self.exchange())[0])
        self.after_scatter(_comm_only(self.scatter())[0])
        self.after_share(_comm_only(self.share()))
        return self.gshards, self.smalls


def adamw(w, g, m, v, *, name):
    shape = w.shape
    cols = shape[-1]
    rows = w.size // cols
    tr = 512 if rows % 512 == 0 else rows
    c1 = 1.0 / (1.0 - ADAM_B1 ** ADAM_STEP)
    c2 = 1.0 / (1.0 - ADAM_B2 ** ADAM_STEP)

    def body(w_ref, g_ref, m_ref, v_ref, d_ref, nm_ref, nv_ref):
        gv = g_ref[...]
        nm = ADAM_B1 * m_ref[...] + (1.0 - ADAM_B1) * gv
        nv = ADAM_B2 * v_ref[...] + (1.0 - ADAM_B2) * (gv * gv)
        d_ref[...] = -ADAM_LR * ((nm * c1) / (jnp.sqrt(nv * c2) + ADAM_EPS) + ADAM_WD * w_ref[...])
        nm_ref[...] = nm
        nv_ref[...] = nv

    spec = pl.BlockSpec((tr, cols), lambda i: (i, 0))
    sds = jax.ShapeDtypeStruct((rows, cols), F32)
    outs = pl.pallas_call(body, grid=(rows // tr,), in_specs=[spec] * 4, out_specs=[spec] * 3, out_shape=[sds] * 3,
                          compiler_params=_cp("parallel"), name=name)(
                              *[a.reshape(rows, cols) for a in (w, g, m, v)])
    return [o.reshape(shape) for o in outs]


def _own_pack(wts, l):
    rows = [wts[n][l].T if n == "w_in" else wts[n][l].reshape(ROWS[n], D_MODEL) for n in sorted(OFF, key=OFF.get)]
    cw = wts["ssd_conv_w"][l].reshape(-1)
    hi = lax.reduce_precision(cw, 8, 7)
    mid = lax.reduce_precision(cw - hi, 8, 7)
    lo = lax.reduce_precision(cw - hi - mid, 8, 7)
    conv = jnp.pad(jnp.concatenate([hi, mid, lo]), (0, CONV_ROWS * D_MODEL - 3 * cw.size)).reshape(CONV_ROWS, D_MODEL)
    gap = jnp.zeros((TAIL_OFF - OFF["w_in"] - ROWS["w_in"], D_MODEL), F32)
    rest = jnp.zeros((TAIL_ROWS - CONV_ROWS, D_MODEL), F32)
    return jnp.concatenate(rows + [gap, conv, rest], axis=0).astype(BF16)


def _layer_weights(gathered, wts, l):
    w = {n: wts[n][l] for n in SMALL_L if n != "ssd_conv_w"}
    w["pack"] = gathered
    square = lambda n: gathered[:, OFF[n]:OFF[n] + ROWS[n]].reshape(N_SHARD * ROWS[n], D_MODEL)
    w["s5_w_glu"], w["xa_wq"], w["xa_wo"] = square("s5_w_glu"), square("xa_wq"), square("xa_wo")
    w["xa_wkv"] = jnp.concatenate([square("xa_wk"), square("xa_wv")], axis=1)
    w_in_t = square("w_in")
    w["w_in_t"] = w_in_t[:D_MAIN]
    w["w_dt_t"] = jnp.pad(w_in_t[D_MAIN:], ((0, D_DT_PAD - SSD_HEADS), (0, 0)))
    per = SSD_CONV * D_XBC // N_SHARD
    cw = gathered[:, TAIL_OFF:TAIL_OFF + CONV_ROWS].astype(F32).reshape(N_SHARD, -1)[:, :3 * per]
    cw = cw.reshape(N_SHARD, 3, SSD_CONV, D_XBC // N_SHARD)
    cw = (cw[:, 0] + cw[:, 1]) + cw[:, 2]
    w["ssd_conv_w"] = cw.transpose(1, 0, 2).reshape(SSD_CONV, D_XBC)
    return w


def kernel(x, mem, norm_mix, w_in, s5_a_re, s5_a_im, s5_log_dt, s5_b_re, s5_b_im, s5_c_re, s5_c_im, s5_d, s5_w_glu, ssd_conv_w, ssd_conv_b, ssd_dt_bias, ssd_a_log, ssd_d, ssd_norm, w_out, norm_xattn, norm_mem, xa_wq, xa_wk, xa_wv, xa_wo, norm_mlp, mlp_w1, mlp_w2, norm_final, loss_target, m_norm_mix, m_w_in, m_s5_a_re, m_s5_a_im, m_s5_log_dt, m_s5_b_re, m_s5_b_im, m_s5_c_re, m_s5_c_im, m_s5_d, m_s5_w_glu, m_ssd_conv_w, m_ssd_conv_b, m_ssd_dt_bias, m_ssd_a_log, m_ssd_d, m_ssd_norm, m_w_out, m_norm_xattn, m_norm_mem, m_xa_wq, m_xa_wk, m_xa_wv, m_xa_wo, m_norm_mlp, m_mlp_w1, m_mlp_w2, m_norm_final, v_norm_mix, v_w_in, v_s5_a_re, v_s5_a_im, v_s5_log_dt, v_s5_b_re, v_s5_b_im, v_s5_c_re, v_s5_c_im, v_s5_d, v_s5_w_glu, v_ssd_conv_w, v_ssd_conv_b, v_ssd_dt_bias, v_ssd_a_log, v_ssd_d, v_ssd_norm, v_w_out, v_norm_xattn, v_norm_mem, v_xa_wq, v_xa_wk, v_xa_wv, v_xa_wo, v_norm_mlp, v_mlp_w1, v_mlp_w2, v_norm_final):
    names = ("norm_mix", "w_in", "s5_a_re", "s5_a_im", "s5_log_dt", "s5_b_re", "s5_b_im", "s5_c_re", "s5_c_im", "s5_d",
             "s5_w_glu", "ssd_conv_w", "ssd_conv_b", "ssd_dt_bias", "ssd_a_log", "ssd_d", "ssd_norm", "w_out",
             "norm_xattn", "norm_mem", "xa_wq", "xa_wk", "xa_wv", "xa_wo", "norm_mlp", "mlp_w1", "mlp_w2", "norm_final")
    loc = locals()
    wts = {n: loc[n] for n in names}
    mom = {n: loc["m_" + n] for n in names}
    var = {n: loc["v_" + n] for n in names}
    shard = 2 * lax.axis_index("x") + lax.axis_index("y")
    core = lax.axis_index("c")

    cst = _consts()
    place_idx = jnp.stack([shard, core]).astype(jnp.int32)
    h, mem0 = x[0], mem[0]

    own = _own_pack(wts, 0)
    landed = _comm_only(gather_over_ici(own, place_own(own, place_idx[:1])))[0]
    gathered = _comm_only(gather_to_sibling(landed))[0]
    layers, saved = [], []
    for l in range(DEPTH):
        layers.append(_layer_weights(gathered, wts, l))
        nxt = None
        if l + 1 < DEPTH:
            own = _own_pack(wts, l + 1)
            nxt = (own, place_own(own, place_idx[:1]))
        h, sv, gathered = _layer_fwd(h, mem0, layers[l], cst, next_pack=nxt)
        saved.append(sv)
    loss, dx, dgf = loss_head(h, norm_final.reshape(1, -1), loss_target[0], tm=TM, name="loss_head")

    gshards = jnp.zeros((DEPTH, PACK_ROWS, D_MODEL), F32)
    smalls = jnp.zeros((DEPTH, N_SHARD, TAIL_ROWS, D_MODEL), F32)
    pending = None
    for l in reversed(range(DEPTH)):
        dx, gpack, first = _layer_bwd(dx, mem0, layers[l], saved[l], cst, extra_small=dgf[0] if l == DEPTH - 1 else None,
                                      reduce_hooks=pending, early=(l, place_idx) if l == 0 else None)
        if pending is not None:
            gshards, smalls = pending.gshards, pending.smalls
        pending = _Reduction(gpack, l, place_idx, gshards, smalls)
    gshards, smalls = _Reduction(gpack, 0, place_idx, first.gshards, first.smalls, ROWS_LATE).run_alone()

    g = {n: gshards[:, OFF[n]:OFF[n] + ROWS[n]].reshape(wts[n].shape) for n in OFF if n != "w_in"}
    g["w_in"] = gshards[:, OFF["w_in"]:OFF["w_in"] + ROWS["w_in"]].transpose(0, 2, 1)
    small_red = smalls[:, :, :SMALL_Q].reshape(DEPTH, -1)
    off = 0
    for n in SMALL_L:
        shape = (SSD_CONV, D_XBC) if n == "ssd_conv_w" else wts[n].shape[1:]
        size = math.prod(shape)
        g[n] = small_red[:, off:off + size].reshape((DEPTH,) + shape)
        off += size
    g["norm_final"] = small_red[DEPTH - 1, off:off + D_MODEL]
    g["ssd_conv_w"] = lax.dynamic_slice_in_dim(g["ssd_conv_w"], shard * (D_XBC // N_SHARD), D_XBC // N_SHARD, axis=2)

    deltas, new_m, new_v = {}, {}, {}
    for n in names:
        deltas[n], new_m[n], new_v[n] = adamw(wts[n], g[n], mom[n], var[n], name="adamw_" + n)
    loss_all = lax.psum(loss[0, 0], ("x", "y", "c"))
    return (loss_all, dx[None], *[g[n] for n in names], *[deltas[n] for n in names], *[new_m[n] for n in names],
            *[new_v[n] for n in names])
```

```python
import functools
import math

import jax
import jax.numpy as jnp
import numpy as np
from jax import lax
from jax.experimental import pallas as pl
from jax.experimental.pallas import tpu as pltpu

F32 = jnp.float32
BF16 = jnp.bfloat16
HIGHEST = lax.Precision.HIGHEST

D_MODEL = 1024
DEPTH = 4
D_S5 = 1024
D_SSD = 1024
S5_GROUP = 16
S5_GROUPS = 64
S5_STATE = 64
SSD_HEADS = 16
SSD_HEADDIM = 64
SSD_NGROUPS = 4
SSD_STATE = 128
SSD_CONV = 4
SSD_CHUNK = 128
D_XBC = 2048
D_MAIN = 4096
D_DT_PAD = 128
XA_HEADS = 4
XA_HEAD_DIM = 256
D_FF = 4096
EPS = 1e-5
ADAM_LR, ADAM_B1, ADAM_B2, ADAM_EPS, ADAM_WD, ADAM_STEP = 0.001, 0.9, 0.999, 1e-08, 0.01, 10

VMEM_LIMIT = 56 * 1024 * 1024
MESH_T = pl.DeviceIdType.MESH


def _cp(*sem):
    return pltpu.CompilerParams(dimension_semantics=tuple(sem) if sem else None, vmem_limit_bytes=VMEM_LIMIT)


_ANY = pl.BlockSpec(memory_space=pl.ANY)


class Comm:
    def __init__(self, name, inputs, out_shapes, sems, start, wait, aliases=None):
        self.name, self.inputs, self.out_shapes, self.sems = name, list(inputs), list(out_shapes), list(sems)
        self.start, self.wait, self.aliases = start, wait, dict(aliases or {})


def _call(body, *, grid, in_specs, out_specs, out_shape, args, scratch_shapes=(), sem, name, comm=None, aliases=None):
    in_specs, out_specs, out_shape = list(in_specs), list(out_specs), list(out_shape)
    scratch_shapes = list(scratch_shapes)
    aliases = dict(aliases or {})
    if comm is None:
        res = pl.pallas_call(body, grid=grid, in_specs=in_specs, out_specs=out_specs, out_shape=out_shape,
                             scratch_shapes=scratch_shapes, compiler_params=_cp(*sem), name=name,
                             input_output_aliases=aliases)(*args)
        return list(res), []
    n_in, n_out, n_scr = len(in_specs), len(out_specs), len(scratch_shapes)
    c_in, c_out = len(comm.inputs), len(comm.out_shapes)

    def wrapped(*refs):
        a, refs = refs[:n_in], refs[n_in:]
        ci, refs = refs[:c_in], refs[c_in:]
        o, refs = refs[:n_out], refs[n_out:]
        co, refs = refs[:c_out], refs[c_out:]
        s, cs = refs[:n_scr], refs[n_scr:]
        first = functools.reduce(jnp.logical_and, [pl.program_id(d) == 0 for d in range(len(grid))])
        last = functools.reduce(jnp.logical_and, [pl.program_id(d) == grid[d] - 1 for d in range(len(grid))])

        @pl.when(first)
        def _():
            comm.start(ci, co, cs)

        body(*a, *o, *s)

        @pl.when(last)
        def _():
            comm.wait(ci, co, cs)

    for i, j in comm.aliases.items():
        aliases[n_in + i] = n_out + j
    res = pl.pallas_call(wrapped, grid=grid, in_specs=in_specs + [_ANY] * c_in, out_specs=out_specs + [_ANY] * c_out,
                         out_shape=out_shape + comm.out_shapes, scratch_shapes=scratch_shapes + comm.sems,
                         compiler_params=_cp(*(("arbitrary",) * len(grid))), name=name + "_" + comm.name,
                         input_output_aliases=aliases)(*args, *comm.inputs)
    return list(res[:n_out]), list(res[n_out:])


def _combine(a, b):
    if a is None or b is None:
        return a or b
    ni, no, ns = len(a.inputs), len(a.out_shapes), len(a.sems)

    def start(ci, co, cs):
        a.start(ci[:ni], co[:no], cs[:ns])
        b.start(ci[ni:], co[no:], cs[ns:])

    def wait(ci, co, cs):
        a.wait(ci[:ni], co[:no], cs[:ns])
        b.wait(ci[ni:], co[no:], cs[ns:])

    aliases = dict(a.aliases)
    aliases.update({ni + i: no + j for i, j in b.aliases.items()})
    return Comm(a.name + "_" + b.name, a.inputs + b.inputs, a.out_shapes + b.out_shapes, a.sems + b.sems, start, wait,
                aliases)


def _comm_only(comm):
    def body(*refs):
        ci, refs = refs[:len(comm.inputs)], refs[len(comm.inputs):]
        co, cs = refs[:len(comm.out_shapes)], refs[len(comm.out_shapes):]
        comm.start(ci, co, cs)
        comm.wait(ci, co, cs)

    res = pl.pallas_call(body, in_specs=[_ANY] * len(comm.inputs), out_specs=[_ANY] * len(comm.out_shapes),
                         out_shape=comm.out_shapes, scratch_shapes=comm.sems, name=comm.name,
                         input_output_aliases=comm.aliases)(*comm.inputs)
    return list(res)


def _dot(a, b):
    return jnp.dot(a, b, preferred_element_type=F32)


def _dot_nt(a, b):
    return lax.dot_general(a, b, (((1,), (1,)), ((), ())), preferred_element_type=F32)


def _dot_tn(a, b):
    return lax.dot_general(a, b, (((0,), (0,)), ((), ())), preferred_element_type=F32)


def _dot_hi(a, b):
    return jnp.dot(a, b, precision=HIGHEST, preferred_element_type=F32)


def _gelu(x):
    c = math.sqrt(2.0 / math.pi)
    return 0.5 * x * (1.0 + jnp.tanh(c * (x + 0.044715 * x * x * x)))


def _gelu_grad(x):
    c = math.sqrt(2.0 / math.pi)
    t = jnp.tanh(c * (x + 0.044715 * x * x * x))
    return 0.5 * (1.0 + t) + 0.5 * x * (1.0 - t * t) * c * (1.0 + 3 * 0.044715 * x * x)


def _sigmoid(x):
    return 1.0 / (1.0 + jnp.exp(-x))


def _act(a, act):
    if act is None:
        return a.astype(BF16)
    a = a.astype(F32)
    if act == "relu2":
        a = jnp.maximum(a, 0.0)
        return (a * a).astype(BF16)
    if act == "gelu":
        return _gelu(a).astype(BF16)
    raise ValueError(act)


def _pack_block(ref):
    return ref[...].reshape(-1, ref.shape[-1])


def norm_matmul(x, g, w, w2=None, *, tm, tn, name, wspec=None, n_out=None, w_transposed=False):
    T, D = x.shape
    N = n_out if wspec is not None else (w.shape[0] if w_transposed else w.shape[1])
    wget = (lambda r: r[...]) if wspec is None else _pack_block
    mm = _dot_nt if w_transposed else _dot
    has2 = w2 is not None

    def body(x_ref, g_ref, w_ref, *rest):
        if has2:
            w2_ref, o_ref, h_ref, r_ref, o2_ref = rest
        else:
            o_ref, h_ref, r_ref = rest
        j = pl.program_id(1)

        @pl.when(j == 0)
        def _():
            xv = x_ref[...]
            r = lax.rsqrt(jnp.mean(xv * xv, axis=-1, keepdims=True) + EPS)
            h = (xv * r * g_ref[...]).astype(BF16)
            h_ref[...] = h
            r_ref[...] = r
            if has2:
                o2_ref[...] = mm(h, w2_ref[...])

        o_ref[...] = mm(h_ref[...], wget(w_ref)).astype(o_ref.dtype)

    if wspec is None:
        wspec = pl.BlockSpec((tn, D), lambda i, j: (j, 0)) if w_transposed else pl.BlockSpec((D, tn), lambda i, j: (0, j))
    in_specs = [pl.BlockSpec((tm, D), lambda i, j: (i, 0)), pl.BlockSpec((1, D), lambda i, j: (0, 0)), wspec]
    out_shape = [jax.ShapeDtypeStruct((T, N), BF16), jax.ShapeDtypeStruct((T, D), BF16),
                 jax.ShapeDtypeStruct((T, 1), F32)]
    out_specs = [pl.BlockSpec((tm, tn), lambda i, j: (i, j)), pl.BlockSpec((tm, D), lambda i, j: (i, 0)),
                 pl.BlockSpec((tm, 1), lambda i, j: (i, 0))]
    args = [x, g, w]
    if has2:
        in_specs.append(pl.BlockSpec(w2.shape, lambda i, j: (0, 0)))
        out_shape.append(jax.ShapeDtypeStruct((T, D_DT_PAD), F32))
        out_specs.append(pl.BlockSpec((tm, D_DT_PAD), lambda i, j: (i, 0)))
        args.append(w2)
    return pl.pallas_call(body, grid=(T // tm, N // tn), in_specs=in_specs, out_specs=out_specs,
                          out_shape=out_shape, compiler_params=_cp("parallel", "arbitrary"), name=name)(*args)


def matmul_res(a, w, r, *, act=None, tm, tn, tk, name, wspec=None):
    T, K = a.shape
    N = r.shape[1]
    wget = (lambda r_: r_[...]) if wspec is None else _pack_block
    nk = K // tk

    def body(a_ref, w_ref, r_ref, o_ref):
        k = pl.program_id(2)

        @pl.when(k == 0)
        def _():
            o_ref[...] = r_ref[...]

        o_ref[...] += _dot(_act(a_ref[...], act), wget(w_ref))

    return pl.pallas_call(
        body, grid=(T // tm, N // tn, nk),
        in_specs=[pl.BlockSpec((tm, tk), lambda i, j, k: (i, k)),
                  pl.BlockSpec((tk, tn), lambda i, j, k: (k, j)) if wspec is None else wspec,
                  pl.BlockSpec((tm, tn), lambda i, j, k: (i, j))],
        out_specs=pl.BlockSpec((tm, tn), lambda i, j, k: (i, j)),
        out_shape=jax.ShapeDtypeStruct((T, N), F32),
        compiler_params=_cp("parallel", "parallel", "arbitrary"), name=name)(a, w, r)


def glu_fwd(ys, w, *, tm, name, comm=None):
    T, N = ys.shape

    def body(y_ref, w_ref, o_ref, t_ref):
        a = _gelu(y_ref[...].astype(F32))
        t = _dot(a.astype(BF16), w_ref[...])
        o_ref[...] = (a * _sigmoid(t)).astype(BF16)
        t_ref[...] = t.astype(BF16)

    return _call(
        body, grid=(T // tm,),
        in_specs=[pl.BlockSpec((tm, N), lambda i: (i, 0)), pl.BlockSpec((N, N), lambda i: (0, 0))],
        out_specs=[pl.BlockSpec((tm, N), lambda i: (i, 0)), pl.BlockSpec((tm, N), lambda i: (i, 0))],
        out_shape=[jax.ShapeDtypeStruct((T, 2 * N), BF16), jax.ShapeDtypeStruct((T, N), BF16)],
        sem=("parallel",), name=name, args=(ys, w), comm=comm)


def glu_bwd(dymix, ys, t, w, *, tm, name, comm=None):
    T, N = ys.shape

    def body(d_ref, y_ref, t_ref, w_ref, dys_ref):
        d = d_ref[...].astype(F32)
        ysv = y_ref[...].astype(F32)
        s = _sigmoid(t_ref[...].astype(F32))
        dt = (d * _gelu(ysv) * s * (1.0 - s)).astype(BF16)
        dys_ref[...] = ((d * s + _dot_nt(dt, w_ref[...])) * _gelu_grad(ysv)).astype(BF16)

    spec = pl.BlockSpec((tm, N), lambda i: (i, 0))
    return _call(body, grid=(T // tm,), in_specs=[spec, spec, spec, pl.BlockSpec((N, N), lambda i: (0, 0))],
                 out_specs=[spec], out_shape=[jax.ShapeDtypeStruct((T, N), BF16)], sem=("parallel",), name=name,
                 args=(dymix, ys, t, w), comm=comm)


def glu_dw(dymix, ys, t, pack, *, pack_spec, tt, name):
    T, N = ys.shape

    def body(d_ref, y_ref, t_ref, pack_ref, o_ref):
        i = pl.program_id(0)
        a = _gelu(y_ref[...].astype(F32))
        s = _sigmoid(t_ref[...].astype(F32))
        dt = (d_ref[...].astype(F32) * a * s * (1.0 - s)).astype(BF16)
        part = _dot_tn(a.astype(BF16), dt).reshape(o_ref.shape)

        @pl.when(i == 0)
        def _():
            o_ref[...] = part

        @pl.when(i > 0)
        def _():
            o_ref[...] += part

    spec = pl.BlockSpec((tt, N), lambda i: (i, 0))
    return pl.pallas_call(body, grid=(T // tt,), in_specs=[spec, spec, spec, _ANY], out_specs=pack_spec,
                          out_shape=jax.ShapeDtypeStruct(pack.shape, F32), input_output_aliases={3: 0},
                          compiler_params=_cp("arbitrary"), name=name)(dymix, ys, t, pack)


def matmul_nt(g, w, *, epi=None, epi_args=(), g2=None, w2=None, tm, tko, tn, out_dtype=BF16, name, wspec=None,
              k_out=None, comm=None, w_is_nk=False):
    T, N = g.shape
    K = k_out if wspec is not None else (w.shape[1] if w_is_nk else w.shape[0])
    wget = (lambda r_: r_[...]) if wspec is None else _pack_block
    mm = _dot if w_is_nk else _dot_nt
    nn = N // tn
    has2 = g2 is not None
    rms = epi == "rmsbwd"
    if rms:
        assert tko == K
    n_epi = len(epi_args)

    def body(*refs):
        g_ref, w_ref = refs[0], refs[1]
        pos = 2
        if has2:
            g2_ref, w2_ref = refs[2], refs[3]
            pos = 4
        e_refs = refs[pos:pos + n_epi]
        pos += n_epi
        o_ref = refs[pos]
        pos += 1
        if rms:
            dg_ref = refs[pos]
            pos += 1
        acc_ref = refs[pos]
        i = pl.program_id(0)
        n = pl.program_id(2)
        part = mm(g_ref[...].astype(BF16), wget(w_ref))

        @pl.when(n == 0)
        def _():
            acc_ref[...] = part

        @pl.when(n > 0)
        def _():
            acc_ref[...] += part

        @pl.when(n == nn - 1)
        def _():
            acc = acc_ref[...]
            if has2:
                acc = acc + mm(g2_ref[...].astype(BF16), w2_ref[...])
            if epi is None:
                o_ref[...] = acc.astype(o_ref.dtype)
            elif epi == "relu2bwd":
                h1 = e_refs[0][...].astype(F32)
                o_ref[...] = (acc * 2.0 * jnp.maximum(h1, 0.0)).astype(o_ref.dtype)
            elif epi == "glubwd":
                da1 = e_refs[0][...].astype(F32)
                ys = e_refs[1][...].astype(F32)
                o_ref[...] = ((da1 + acc) * _gelu_grad(ys)).astype(o_ref.dtype)
            elif epi == "rmsbwd":
                xv, rs, gain, rv = e_refs[0][...], e_refs[1][...], e_refs[2][...], e_refs[3][...]
                xhat = xv * rs
                gd = acc * gain
                o_ref[...] = rv + rs * (gd - xhat * jnp.mean(gd * xhat, axis=-1, keepdims=True))
                part_g = jnp.sum(acc * xhat, axis=0, keepdims=True)

                @pl.when(i == 0)
                def _():
                    dg_ref[...] = part_g

                @pl.when(i > 0)
                def _():
                    dg_ref[...] += part_g

    if wspec is None:
        wspec = (pl.BlockSpec((tn, tko), lambda i, k, n: (n, k)) if w_is_nk
                 else pl.BlockSpec((tko, tn), lambda i, k, n: (k, n)))
    in_specs = [pl.BlockSpec((tm, tn), lambda i, k, n: (i, n)), wspec]
    args = [g, w]
    if has2:
        n2 = g2.shape[1]
        in_specs += [pl.BlockSpec((tm, n2), lambda i, k, n: (i, 0)),
                     pl.BlockSpec((n2, tko), lambda i, k, n: (0, k)) if w_is_nk
                     else pl.BlockSpec((tko, n2), lambda i, k, n: (k, 0))]
        args += [g2, w2]
    if epi == "relu2bwd" or epi == "glubwd":
        in_specs += [pl.BlockSpec((tm, tko), lambda i, k, n: (i, k))] * n_epi
    elif rms:
        in_specs += [pl.BlockSpec((tm, K), lambda i, k, n: (i, 0)), pl.BlockSpec((tm, 1), lambda i, k, n: (i, 0)),
                     pl.BlockSpec((1, K), lambda i, k, n: (0, 0)), pl.BlockSpec((tm, K), lambda i, k, n: (i, 0))]
    args += list(epi_args)
    out_shape = [jax.ShapeDtypeStruct((T, K), F32 if rms else out_dtype)]
    out_specs = [pl.BlockSpec((tm, tko), lambda i, k, n: (i, k))]
    if rms:
        out_shape.append(jax.ShapeDtypeStruct((1, K), F32))
        out_specs.append(pl.BlockSpec((1, K), lambda i, k, n: (0, 0)))
    sem = ("arbitrary",) * 3 if rms else ("parallel", "parallel", "arbitrary")
    res, cres = _call(body, grid=(T // tm, K // tko, nn), in_specs=in_specs, out_specs=out_specs, out_shape=out_shape,
                      scratch_shapes=[pltpu.VMEM((tm, tko), F32)], sem=sem, name=name, args=args, comm=comm)
    res = res if rms else res[0]
    return res if comm is None else (res, cres)


def matmul_tn(a, g, *, act=None, tk, tn, tt, name, pack=None, pack_spec=None, pack_shape=None):
    T, K = a.shape
    N = g.shape[1]
    to_pack = pack_spec is not None

    def body(a_ref, g_ref, *rest):
        o_ref = rest[-1]
        t = pl.program_id(2)
        part = _dot_tn(_act(a_ref[...], act), g_ref[...].astype(BF16))
        part = part.reshape(o_ref.shape)

        @pl.when(t == 0)
        def _():
            o_ref[...] = part

        @pl.when(t > 0)
        def _():
            o_ref[...] += part

    in_specs = [pl.BlockSpec((tt, tk), lambda k, n, t: (t, k)), pl.BlockSpec((tt, tn), lambda k, n, t: (t, n))]
    args = [a, g]
    aliases = {}
    if pack is not None:
        in_specs.append(_ANY)
        args.append(pack)
        aliases = {2: 0}
    return pl.pallas_call(
        body, grid=(K // tk, N // tn, T // tt), in_specs=in_specs,
        out_specs=pack_spec if to_pack else pl.BlockSpec((tk, tn), lambda k, n, t: (k, n)),
        out_shape=jax.ShapeDtypeStruct(pack_shape if to_pack else (K, N), F32), input_output_aliases=aliases,
        compiler_params=_cp("parallel", "parallel", "arbitrary"), name=name)(*args)


def attn_fwd(q, kv, *, tm, name):
    T = q.shape[0]
    M = kv.shape[0]
    scale = XA_HEAD_DIM ** -0.5

    def body(q_ref, kv_ref, o_ref):
        for h in range(XA_HEADS):
            sl = slice(h * XA_HEAD_DIM, (h + 1) * XA_HEAD_DIM)
            kh = kv_ref[:, h * XA_HEAD_DIM:(h + 1) * XA_HEAD_DIM]
            vh = kv_ref[:, D_MODEL + h * XA_HEAD_DIM:D_MODEL + (h + 1) * XA_HEAD_DIM]
            s = _dot_nt(q_ref[:, sl], kh) * scale
            s = s - jnp.max(s, axis=-1, keepdims=True)
            p = jnp.exp(s)
            p = p / jnp.sum(p, axis=-1, keepdims=True)
            o_ref[:, sl] = _dot(p.astype(BF16), vh).astype(BF16)

    return pl.pallas_call(
        body, grid=(T // tm,),
        in_specs=[pl.BlockSpec((tm, D_MODEL), lambda i: (i, 0)), pl.BlockSpec((M, 2 * D_MODEL), lambda i: (0, 0))],
        out_specs=pl.BlockSpec((tm, D_MODEL), lambda i: (i, 0)),
        out_shape=jax.ShapeDtypeStruct((T, D_MODEL), BF16),
        compiler_params=_cp("parallel"), name=name)(q, kv)


def attn_bwd(q, kv, do, *, tm, name):
    T = q.shape[0]
    M = kv.shape[0]
    scale = XA_HEAD_DIM ** -0.5

    def body(q_ref, kv_ref, do_ref, dq_ref, dkv_ref):
        i = pl.program_id(0)

        @pl.when(i == 0)
        def _():
            dkv_ref[...] = jnp.zeros_like(dkv_ref)

        for h in range(XA_HEADS):
            sl = slice(h * XA_HEAD_DIM, (h + 1) * XA_HEAD_DIM)
            slv = slice(D_MODEL + h * XA_HEAD_DIM, D_MODEL + (h + 1) * XA_HEAD_DIM)
            qh = q_ref[:, sl]
            kh = kv_ref[:, sl]
            vh = kv_ref[:, slv]
            doh = do_ref[:, sl]
            s = _dot_nt(qh, kh) * scale
            s = s - jnp.max(s, axis=-1, keepdims=True)
            p = jnp.exp(s)
            p = p / jnp.sum(p, axis=-1, keepdims=True)
            pb = p.astype(BF16)
            dkv_ref[:, slv] += _dot_tn(pb, doh)
            dp = _dot_nt(doh, vh)
            ds = (p * (dp - jnp.sum(dp * p, axis=-1, keepdims=True)) * scale).astype(BF16)
            dq_ref[:, sl] = _dot(ds, kh).astype(BF16)
            dkv_ref[:, sl] += _dot_tn(ds, qh)

    spec = pl.BlockSpec((tm, D_MODEL), lambda i: (i, 0))
    kvspec = pl.BlockSpec((M, 2 * D_MODEL), lambda i: (0, 0))
    return pl.pallas_call(
        body, grid=(T // tm,), in_specs=[spec, kvspec, spec], out_specs=[spec, kvspec],
        out_shape=[jax.ShapeDtypeStruct((T, D_MODEL), BF16), jax.ShapeDtypeStruct((M, 2 * D_MODEL), F32)],
        compiler_params=_cp("arbitrary"), name=name)(q, kv, do)


def loss_head(x, g, target, *, tm, name):
    T, D = x.shape

    def body(x_ref, g_ref, t_ref, l_ref, dx_ref, dg_ref):
        i = pl.program_id(0)
        xv = x_ref[...]
        gain = g_ref[...]
        r = lax.rsqrt(jnp.mean(xv * xv, axis=-1, keepdims=True) + EPS)
        xhat = xv * r
        err = xhat * gain - t_ref[...]
        part_l = jnp.full((1, 128), 0.5 / D, F32) * jnp.sum(err * err)
        dy = err * (1.0 / D)
        gd = dy * gain
        dx_ref[...] = r * (gd - xhat * jnp.mean(gd * xhat, axis=-1, keepdims=True))
        part_g = jnp.sum(dy * xhat, axis=0, keepdims=True)

        @pl.when(i == 0)
        def _():
            l_ref[...] = part_l
            dg_ref[...] = part_g

        @pl.when(i > 0)
        def _():
            l_ref[...] += part_l
            dg_ref[...] += part_g

    spec = pl.BlockSpec((tm, D), lambda i: (i, 0))
    return pl.pallas_call(
        body, grid=(T // tm,), in_specs=[spec, pl.BlockSpec((1, D), lambda i: (0, 0)), spec],
        out_specs=[pl.BlockSpec((1, 128), lambda i: (0, 0)), spec, pl.BlockSpec((1, D), lambda i: (0, 0))],
        out_shape=[jax.ShapeDtypeStruct((1, 128), F32), jax.ShapeDtypeStruct((T, D), F32),
                   jax.ShapeDtypeStruct((1, D), F32)],
        compiler_params=_cp("arbitrary"), name=name)(x, g, target)


S5_LS = 128
S5_SEG = 8
S5_TB = S5_LS * S5_SEG
S5_CH = 8
S5_CW = 128
S5_NST = 512


def _cmul(ar, ai, br, bi):
    return ar * br - ai * bi, ar * bi + ai * br


def s5_params_fwd(a_re, a_im, log_dt, b_re, b_im, expand, *, name):
    G, P = a_re.shape

    def body(ar_ref, ai_ref, ld_ref, br_ref, bi_ref, e_ref, abr_ref, abi_ref, apr_ref, api_ref, bbr_ref, bbi_ref):
        ar, ai = ar_ref[...], ai_ref[...]
        dt = jnp.exp(ld_ref[...])
        mag = jnp.exp(dt * ar)
        abr, abi = mag * jnp.cos(dt * ai), mag * jnp.sin(dt * ai)
        den = ar * ar + ai * ai
        zr, zi = abr - 1.0, abi
        fr = (zr * ar + zi * ai) / den
        fi = (zi * ar - zr * ai) / den
        frx, fix = _dot_hi(fr, e_ref[...]), _dot_hi(fi, e_ref[...])
        br, bi = br_ref[...], bi_ref[...]
        bbr_ref[...] = frx * br - fix * bi
        bbi_ref[...] = frx * bi + fix * br
        abr_ref[...] = abr
        abi_ref[...] = abi
        pr, pi = abr, abi
        for _ in range(int(math.log2(S5_LS))):
            pr, pi = _cmul(pr, pi, pr, pi)
        apr_ref[...] = pr
        api_ref[...] = pi

    small = jax.ShapeDtypeStruct((G, P), F32)
    big = jax.ShapeDtypeStruct(b_re.shape, F32)
    return pl.pallas_call(body, out_shape=[small, small, small, small, big, big], name=name)(
        a_re, a_im, log_dt, b_re, b_im, expand)


def s5_params_bwd(a_re, a_im, log_dt, b_re, b_im, g_abr, g_abi, g_bbr, g_bbi, expand, expand_t, *, name):
    G, P = a_re.shape

    def body(ar_ref, ai_ref, ld_ref, br_ref, bi_ref, gar_ref, gai_ref, gbr_ref, gbi_ref, e_ref, et_ref,
             dar_ref, dai_ref, dld_ref, dbr_ref, dbi_ref):
        ar, ai = ar_ref[...], ai_ref[...]
        dt = jnp.exp(ld_ref[...])
        mag = jnp.exp(dt * ar)
        cs, sn = jnp.cos(dt * ai), jnp.sin(dt * ai)
        abr, abi = mag * cs, mag * sn
        den = ar * ar + ai * ai
        zr, zi = abr - 1.0, abi
        fr = (zr * ar + zi * ai) / den
        fi = (zi * ar - zr * ai) / den
        frx, fix = _dot_hi(fr, e_ref[...]), _dot_hi(fi, e_ref[...])
        br, bi = br_ref[...], bi_ref[...]
        gbr, gbi = gbr_ref[...], gbi_ref[...]
        dbr_ref[...] = frx * gbr + fix * gbi
        dbi_ref[...] = -fix * gbr + frx * gbi
        gfr = _dot_hi(br * gbr + bi * gbi, et_ref[...])
        gfi = _dot_hi(-bi * gbr + br * gbi, et_ref[...])
        g_zr = (gfr * ar - gfi * ai) / den
        g_zi = (gfr * ai + gfi * ar) / den
        g_ar = gfr * (zr - fr * 2.0 * ar) / den + gfi * (zi - fi * 2.0 * ar) / den
        g_ai = gfr * (zi - fr * 2.0 * ai) / den + gfi * (-zr - fi * 2.0 * ai) / den
        t_abr = gar_ref[...] + g_zr
        t_abi = gai_ref[...] + g_zi
        g_mag = t_abr * cs + t_abi * sn
        g_th = mag * (-t_abr * sn + t_abi * cs)
        dar_ref[...] = g_ar + g_mag * mag * dt
        dai_ref[...] = g_ai + g_th * dt
        g_dt = jnp.sum(g_mag * mag * ar + g_th * ai, axis=-1, keepdims=True)
        dld_ref[...] = g_dt * dt

    small = jax.ShapeDtypeStruct((G, P), F32)
    big = jax.ShapeDtypeStruct(b_re.shape, F32)
    return pl.pallas_call(body, out_shape=[small, small, jax.ShapeDtypeStruct((G, 1), F32), big, big], name=name)(
        a_re, a_im, log_dt, b_re, b_im, g_abr, g_abi, g_bbr, g_bbi, expand, expand_t)


def _s5_permute_in(src_ref, dst_ref):
    for i in range(S5_LS):
        dst_ref[pl.ds(8 * i, 8), :] = src_ref[pl.ds(i, 8, stride=S5_LS), :]


def _s5_permute_out(src_ref, dst_ref):
    for r in range(S5_SEG):
        for k in range(S5_LS // 8):
            dst_ref[pl.ds(r * S5_LS + 8 * k, 8), :] = src_ref[pl.ds(64 * k + r, 8, stride=8), :]


def _s5_scan(a_r, a_i, dr_ref, di_ref, init_r, init_i, store=None, reverse=False, conj=False):
    sgn = -1.0 if conj else 1.0

    def step(n, c):
        sr, si = c
        i = S5_LS - 1 - n if reverse else n
        nr = a_r * sr - sgn * a_i * si + dr_ref[i]
        ni = a_r * si + sgn * a_i * sr + di_ref[i]
        if store is not None:
            store(i, nr, ni, sr, si)
        return nr, ni

    return lax.fori_loop(0, S5_LS, step, (init_r, init_i), unroll=2)


def _s5_stitch(apr, api, fin_r, fin_i, car_r, car_i, reverse=False, conj=False):
    sgn = -1.0 if conj else 1.0
    rows_r, rows_i = [None] * S5_SEG, [None] * S5_SEG
    order = range(S5_SEG - 1, -1, -1) if reverse else range(S5_SEG)
    for r in order:
        rows_r[r], rows_i[r] = car_r, car_i
        fr, fi = fin_r[r:r + 1], fin_i[r:r + 1]
        car_r, car_i = (apr * car_r - sgn * api * car_i + fr, apr * car_i + sgn * api * car_r + fi)
    return jnp.concatenate(rows_r, 0), jnp.concatenate(rows_i, 0), car_r, car_i


def _s5_specs(nb, rev):
    blk = (lambda c, b: (nb - 1 - b, c)) if rev else (lambda c, b: (b, c))
    tok = pl.BlockSpec((S5_TB, S5_CW), blk)
    par_b = pl.BlockSpec((1, S5_CW, S5_NST), lambda c, b: (c, 0, 0))
    par_c = pl.BlockSpec((1, S5_NST, S5_CW), lambda c, b: (c, 0, 0))
    vec_s = pl.BlockSpec((1, 1, S5_NST), lambda c, b: (c, 0, 0))
    vec_c = pl.BlockSpec((1, 1, S5_CW), lambda c, b: (c, 0, 0))
    return tok, par_b, par_c, vec_s, vec_c


def s5_fwd(proj, bbr, bbi, ccr, cci, abr, abi, apr, api, dskip, *, name, comm=None):
    T = proj.shape[0]
    nb = T // S5_TB
    zeros8 = functools.partial(jnp.zeros, (S5_SEG, S5_NST), F32)

    def body(u_ref, bbr_ref, bbi_ref, ccr_ref, cci_ref, ar_ref, ai_ref, apr_ref, api_ref, d_ref,
             y_ref, csr_ref, csi_ref, uf_ref, up_ref, dr_ref, di_ref, sr_ref, si_ref, yp_ref, car_ref, cai_ref):
        b = pl.program_id(1)

        @pl.when(b == 0)
        def _():
            car_ref[...] = jnp.zeros_like(car_ref)
            cai_ref[...] = jnp.zeros_like(cai_ref)

        csr_ref[0, 0] = car_ref[...]
        csi_ref[0, 0] = cai_ref[...]
        uf_ref[...] = u_ref[...].astype(F32)
        _s5_permute_in(uf_ref, up_ref)
        upb = up_ref[...].astype(BF16)
        dr_ref[...] = _dot(upb, bbr_ref[0]).reshape(S5_LS, S5_SEG, S5_NST)
        di_ref[...] = _dot(upb, bbi_ref[0]).reshape(S5_LS, S5_SEG, S5_NST)
        a_r = jnp.broadcast_to(ar_ref[0], (S5_SEG, S5_NST))
        a_i = jnp.broadcast_to(ai_ref[0], (S5_SEG, S5_NST))
        fin_r, fin_i = _s5_scan(a_r, a_i, dr_ref, di_ref, zeros8(), zeros8())
        cin_r, cin_i, ncr, nci = _s5_stitch(apr_ref[0], api_ref[0], fin_r, fin_i, car_ref[...], cai_ref[...])
        car_ref[...] = ncr
        cai_ref[...] = nci

        def store(i, nr, ni, sr, si):
            sr_ref[i] = nr
            si_ref[i] = ni

        _s5_scan(a_r, a_i, dr_ref, di_ref, cin_r, cin_i, store=store)
        s_r = sr_ref[...].reshape(S5_TB, S5_NST).astype(BF16)
        s_i = si_ref[...].reshape(S5_TB, S5_NST).astype(BF16)
        yp_ref[...] = _dot(s_r, ccr_ref[0]) - _dot(s_i, cci_ref[0]) + d_ref[0] * up_ref[...]
        _s5_permute_out(yp_ref, uf_ref)
        y_ref[...] = uf_ref[...].astype(BF16)

    tok, par_b, par_c, vec_s, vec_c = _s5_specs(nb, False)
    cs_spec = pl.BlockSpec((1, 1, 1, S5_NST), lambda c, b: (b, c, 0, 0))
    cs_shape = jax.ShapeDtypeStruct((nb, S5_CH, 1, S5_NST), F32)
    tokbuf = pltpu.VMEM((S5_TB, S5_CW), F32)
    stbuf = pltpu.VMEM((S5_LS, S5_SEG, S5_NST), F32)
    return _call(
        body, grid=(S5_CH, nb),
        in_specs=[tok, par_b, par_b, par_c, par_c, vec_s, vec_s, vec_s, vec_s, vec_c],
        out_specs=[tok, cs_spec, cs_spec],
        out_shape=[jax.ShapeDtypeStruct((T, D_S5), BF16), cs_shape, cs_shape],
        scratch_shapes=[tokbuf, tokbuf, stbuf, stbuf, stbuf, stbuf, tokbuf,
                        pltpu.VMEM((1, S5_NST), F32), pltpu.VMEM((1, S5_NST), F32)],
        sem=("parallel", "arbitrary"), name=name, comm=comm,
        args=(proj, bbr, bbi, ccr, cci, abr, abi, apr, api, dskip))


def s5_bwd(proj, dys, csr, csi, bbr, bbi, ccr, cci, abr, abi, apr, api, dskip, dproj, *, name, comm=None):
    T = proj.shape[0]
    nb = T // S5_TB
    zeros8 = functools.partial(jnp.zeros, (S5_SEG, S5_NST), F32)

    def body(u_ref, gy_ref, csr_ref, csi_ref, bbr_ref, bbi_ref, ccr_ref, cci_ref, ar_ref, ai_ref, apr_ref, api_ref,
             d_ref, dproj_ref, du_ref, dbr_ref, dbi_ref, dcr_ref, dci_ref, dd_ref, dar_ref, dai_ref,
             tmp_ref, up_ref, gyp_ref, dr_ref, di_ref, sr_ref, si_ref, gr_ref, gi_ref, car_ref, cai_ref):
        b = pl.program_id(1)

        @pl.when(b == 0)
        def _():
            car_ref[...] = jnp.zeros_like(car_ref)
            cai_ref[...] = jnp.zeros_like(cai_ref)
            for ref in (dbr_ref, dbi_ref, dcr_ref, dci_ref, dd_ref, dar_ref, dai_ref):
                ref[...] = jnp.zeros_like(ref)

        tmp_ref[...] = u_ref[...].astype(F32)
        _s5_permute_in(tmp_ref, up_ref)
        tmp_ref[...] = gy_ref[...].astype(F32)
        _s5_permute_in(tmp_ref, gyp_ref)
        upb = up_ref[...].astype(BF16)
        gyp = gyp_ref[...]
        gypb = gyp.astype(BF16)
        dr_ref[...] = _dot(upb, bbr_ref[0]).reshape(S5_LS, S5_SEG, S5_NST)
        di_ref[...] = _dot(upb, bbi_ref[0]).reshape(S5_LS, S5_SEG, S5_NST)
        a_r = jnp.broadcast_to(ar_ref[0], (S5_SEG, S5_NST))
        a_i = jnp.broadcast_to(ai_ref[0], (S5_SEG, S5_NST))
        fin_r, fin_i = _s5_scan(a_r, a_i, dr_ref, di_ref, zeros8(), zeros8())
        cin_r, cin_i, _, _ = _s5_stitch(apr_ref[0], api_ref[0], fin_r, fin_i, csr_ref[0, 0], csi_ref[0, 0])
        sr_ref[0] = cin_r
        si_ref[0] = cin_i

        def store_s(i, nr, ni, sr, si):
            sr_ref[i + 1] = nr
            si_ref[i + 1] = ni

        _s5_scan(a_r, a_i, dr_ref, di_ref, cin_r, cin_i, store=store_s)
        s_r = sr_ref[pl.ds(1, S5_LS)].reshape(S5_TB, S5_NST).astype(BF16)
        s_i = si_ref[pl.ds(1, S5_LS)].reshape(S5_TB, S5_NST).astype(BF16)
        dcr_ref[0] += _dot_tn(s_r, gypb)
        dci_ref[0] -= _dot_tn(s_i, gypb)
        dd_ref[0] += jnp.sum(gyp * up_ref[...], axis=0, keepdims=True)
        dr_ref[...] = _dot_nt(gypb, ccr_ref[0]).reshape(S5_LS, S5_SEG, S5_NST)
        di_ref[...] = (-_dot_nt(gypb, cci_ref[0])).reshape(S5_LS, S5_SEG, S5_NST)
        fin_r, fin_i = _s5_scan(a_r, a_i, dr_ref, di_ref, zeros8(), zeros8(), reverse=True, conj=True)
        gin_r, gin_i, ncr, nci = _s5_stitch(apr_ref[0], api_ref[0], fin_r, fin_i, car_ref[...], cai_ref[...],
                                            reverse=True, conj=True)
        car_ref[...] = ncr
        cai_ref[...] = nci
        def step_g(n, carry):
            gr, gi, acc_r, acc_i = carry
            i = S5_LS - 1 - n
            nr = a_r * gr + a_i * gi + dr_ref[i]
            ni = a_r * gi - a_i * gr + di_ref[i]
            gr_ref[i] = nr
            gi_ref[i] = ni
            pr, pi = sr_ref[i], si_ref[i]
            return nr, ni, acc_r + (nr * pr + ni * pi), acc_i + (ni * pr - nr * pi)

        _, _, acc_r, acc_i = lax.fori_loop(0, S5_LS, step_g, (gin_r, gin_i, zeros8(), zeros8()), unroll=2)
        dar_ref[0] += jnp.sum(acc_r, axis=0, keepdims=True)
        dai_ref[0] += jnp.sum(acc_i, axis=0, keepdims=True)
        g_r = gr_ref[...].reshape(S5_TB, S5_NST).astype(BF16)
        g_i = gi_ref[...].reshape(S5_TB, S5_NST).astype(BF16)
        dbr_ref[0] += _dot_tn(upb, g_r)
        dbi_ref[0] += _dot_tn(upb, g_i)
        gyp_ref[...] = _dot_nt(g_r, bbr_ref[0]) + _dot_nt(g_i, bbi_ref[0]) + d_ref[0] * gyp
        _s5_permute_out(gyp_ref, tmp_ref)
        du_ref[...] = tmp_ref[...].astype(BF16)

    tok, par_b, par_c, vec_s, vec_c = _s5_specs(nb, True)
    cs_spec = pl.BlockSpec((1, 1, 1, S5_NST), lambda c, b: (nb - 1 - b, c, 0, 0))
    tokbuf = pltpu.VMEM((S5_TB, S5_CW), F32)
    stbuf = pltpu.VMEM((S5_LS, S5_SEG, S5_NST), F32)
    stbuf1 = pltpu.VMEM((S5_LS + 1, S5_SEG, S5_NST), F32)
    return _call(
        body, grid=(S5_CH, nb), comm=comm,
        in_specs=[tok, tok, cs_spec, cs_spec, par_b, par_b, par_c, par_c, vec_s, vec_s, vec_s, vec_s, vec_c, _ANY],
        out_specs=[tok, par_b, par_b, par_c, par_c, vec_c, vec_s, vec_s], aliases={13: 0},
        out_shape=[jax.ShapeDtypeStruct(dproj.shape, BF16),
                   jax.ShapeDtypeStruct((S5_CH, S5_CW, S5_NST), F32), jax.ShapeDtypeStruct((S5_CH, S5_CW, S5_NST), F32),
                   jax.ShapeDtypeStruct((S5_CH, S5_NST, S5_CW), F32), jax.ShapeDtypeStruct((S5_CH, S5_NST, S5_CW), F32),
                   jax.ShapeDtypeStruct((S5_CH, 1, S5_CW), F32),
                   jax.ShapeDtypeStruct((S5_CH, 1, S5_NST), F32), jax.ShapeDtypeStruct((S5_CH, 1, S5_NST), F32)],
        scratch_shapes=[tokbuf, tokbuf, tokbuf, stbuf, stbuf, stbuf1, stbuf1, stbuf, stbuf,
                        pltpu.VMEM((1, S5_NST), F32), pltpu.VMEM((1, S5_NST), F32)],
        sem=("parallel", "arbitrary"), name=name,
        args=(proj, dys, csr, csi, bbr, bbi, ccr, cci, abr, abi, apr, api, dskip, dproj))


SSD_L = SSD_CHUNK
SSD_GW = 256
NEG = -1e30


def _expand16(v):
    lane = lax.broadcasted_iota(jnp.int32, (v.shape[0], 128), 1)
    parts = [jnp.where(lane < SSD_HEADDIM, v[:, 2 * j:2 * j + 1], v[:, 2 * j + 1:2 * j + 2]) for j in range(8)]
    return jnp.concatenate(parts, axis=1)


def _headsum(v, hsum):
    hi = v.astype(BF16)
    lo = (v - hi.astype(F32)).astype(BF16)
    return _dot(hi, hsum) + _dot(lo, hsum)


def _softplus(x):
    return jnp.maximum(x, 0.0) + jnp.log(1.0 + jnp.exp(-jnp.abs(x)))


def _ssd_chunk_fwd(z, xbc, tail, dtraw, hprev, cw, cb, dtb, alog, dsk, nw, tril):
    L = SSD_L
    f = {}
    xe = jnp.concatenate([tail, xbc], axis=0)
    sh = [xbc] + [pltpu.roll(xe, s, 0)[8:] for s in (1, 2, 3)]
    conv = cb + cw[3:4] * sh[0] + cw[2:3] * sh[1] + cw[1:2] * sh[2] + cw[0:1] * sh[3]
    sig = _sigmoid(conv)
    xa = conv * sig
    xs, bm, cm = xa[:, :D_SSD], xa[:, D_SSD:D_SSD + 512], xa[:, D_SSD + 512:]
    pre = dtraw + dtb
    dt = _softplus(pre)
    a_h = -jnp.exp(alog)
    acum = _dot_hi(tril, dt * a_h)
    acum_t = acum.T
    alast = acum[L - 1:L]
    exp_a = jnp.exp(acum)
    dec = jnp.exp(alast - acum)
    exp_al = jnp.exp(alast)
    dt_x, dec_x, exp_a_x, exp_al_x = _expand16(dt), _expand16(dec), _expand16(exp_a), _expand16(exp_al)
    d_x = _expand16(dsk)
    xh = xs * dt_x
    xhb = xh.astype(BF16)
    xd = (xh * dec_x).astype(BF16)
    row = lax.broadcasted_iota(jnp.int32, (L, L), 0)
    col = lax.broadcasted_iota(jnp.int32, (L, L), 1)
    causal = row >= col
    lane = lax.broadcasted_iota(jnp.int32, (L, 128), 1)
    low = lane < SSD_HEADDIM
    hb = hprev.astype(BF16)
    y_pairs, yoff_parts, st_parts, cbs, lms = [], [], [], [], []
    for g in range(SSD_NGROUPS):
        bg = bm[:, g * 128:(g + 1) * 128].astype(BF16)
        cg = cm[:, g * 128:(g + 1) * 128].astype(BF16)
        cbg = _dot_nt(cg, bg)
        cbs.append(cbg)
        for j in (2 * g, 2 * g + 1):
            xp = xhb[:, j * 128:(j + 1) * 128]
            ys = []
            for h in (2 * j, 2 * j + 1):
                lm = jnp.exp(jnp.where(causal, acum[:, h:h + 1] - acum_t[h:h + 1, :], NEG))
                lms.append(lm)
                ys.append(_dot((cbg * lm).astype(BF16), xp))
            y_pairs.append(jnp.where(low, ys[0], ys[1]))
        gs = slice(g * SSD_GW, (g + 1) * SSD_GW)
        yoff_parts.append(_dot(cg, hb[:, gs]) * exp_a_x[:, gs])
        st_parts.append(_dot_tn(bg, xd[:, gs]))
    yoff = jnp.concatenate(yoff_parts, axis=1)
    y = jnp.concatenate(y_pairs, axis=1) + yoff + d_x * xs
    hnew = exp_al_x * hprev + jnp.concatenate(st_parts, axis=1)
    sz = _sigmoid(z)
    gz = y * (z * sz)
    r = lax.rsqrt(jnp.mean(gz * gz, axis=-1, keepdims=True) + EPS)
    out = gz * r * nw
    f.update(sh=sh, conv=conv, sig=sig, xs=xs, bm=bm, cm=cm, pre=pre, dt=dt, a_h=a_h, exp_a_x=exp_a_x, dec_x=dec_x,
             exp_al=exp_al, exp_al_x=exp_al_x, dt_x=dt_x, d_x=d_x, xh=xh, xhb=xhb, xd=xd, causal=causal, low=low, hb=hb,
             cbs=cbs, lms=lms, yoff=yoff, y=y, sz=sz, gz=gz, r=r)
    return out, hnew, f


def _ssd_params(conv_w, conv_b, dt_bias, a_log, d_skip, norm_w):
    pad16 = lambda v: jnp.pad(v.reshape(1, SSD_HEADS), ((0, 0), (0, 128 - SSD_HEADS)))
    return (jnp.pad(conv_w, ((0, 8 - SSD_CONV), (0, 0))), conv_b.reshape(1, D_XBC), pad16(dt_bias), pad16(a_log),
            pad16(d_skip), norm_w.reshape(1, D_SSD))


def _ssd_param_specs():
    full = lambda shape: pl.BlockSpec(shape, lambda i: (0, 0))
    return [full((8, D_XBC)), full((1, D_XBC)), full((1, 128)), full((1, 128)), full((1, 128)), full((1, D_SSD))]


def ssd_fwd(proj, dtraw, params, tril, ymix, *, name, comm=None):
    T = proj.shape[0]
    nc = T // SSD_L

    def body(z_ref, x_ref, dt_ref, cw_ref, cb_ref, dtb_ref, al_ref, dsk_ref, nw_ref, tril_ref, ymix_ref,
             o_ref, hs_ref, h_ref, tail_ref):
        i = pl.program_id(0)

        @pl.when(i == 0)
        def _():
            h_ref[...] = jnp.zeros_like(h_ref)
            tail_ref[...] = jnp.zeros_like(tail_ref)

        xbc = x_ref[...].astype(F32)
        hprev = h_ref[...]
        hs_ref[0] = hprev
        out, hnew, _ = _ssd_chunk_fwd(z_ref[...].astype(F32), xbc, tail_ref[...], dt_ref[...], hprev, cw_ref[...],
                                      cb_ref[...], dtb_ref[...], al_ref[...], dsk_ref[...], nw_ref[...], tril_ref[...])
        o_ref[...] = out.astype(BF16)
        h_ref[...] = hnew
        tail_ref[...] = xbc[SSD_L - 8:]

    return _call(
        body, grid=(nc,),
        in_specs=[pl.BlockSpec((SSD_L, D_SSD), lambda i: (i, 1)), pl.BlockSpec((SSD_L, D_XBC), lambda i: (i, 1)),
                  pl.BlockSpec((SSD_L, 128), lambda i: (i, 0))] + _ssd_param_specs()
                 + [pl.BlockSpec((SSD_L, SSD_L), lambda i: (0, 0)), _ANY],
        out_specs=[pl.BlockSpec((SSD_L, D_SSD), lambda i: (i, 1)),
                   pl.BlockSpec((1, SSD_STATE, D_SSD), lambda i: (i, 0, 0))],
        out_shape=[jax.ShapeDtypeStruct(ymix.shape, BF16), jax.ShapeDtypeStruct((nc, SSD_STATE, D_SSD), F32)],
        scratch_shapes=[pltpu.VMEM((SSD_STATE, D_SSD), F32), pltpu.VMEM((8, D_XBC), F32)],
        sem=("arbitrary",), name=name, args=(proj, proj, dtraw, *params, tril, ymix), aliases={10: 0}, comm=comm)


def ssd_bwd(proj, dtraw, hs, dymix, params, tril, triu, trils, headsum, *, name, comm=None):
    T = proj.shape[0]
    nc = T // SSD_L
    L = SSD_L

    def body(z_ref, x_ref, xprev_ref, dt_ref, hs_ref, do_ref, cw_ref, cb_ref, dtb_ref, al_ref, dsk_ref, nw_ref,
             tril_ref, triu_ref, trils_ref, hsum_ref,
             dp_ref, ddt_ref, dcw_ref, dcb_ref, ddtb_ref, dal_ref, ddsk_ref, dnw_ref, dh_ref, dnext_ref):
        i = pl.program_id(0)

        @pl.when(i == 0)
        def _():
            dh_ref[...] = jnp.zeros_like(dh_ref)
            dnext_ref[...] = jnp.zeros_like(dnext_ref)
            for ref in (dcw_ref, dcb_ref, ddtb_ref, dal_ref, ddsk_ref, dnw_ref):
                ref[...] = jnp.zeros_like(ref)

        z = z_ref[...].astype(F32)
        xbc = x_ref[...].astype(F32)
        tail = jnp.where(i == nc - 1, 0.0, xprev_ref[...].astype(F32))
        hprev = hs_ref[0]
        cw, nw = cw_ref[...], nw_ref[...]
        hsum = hsum_ref[...]
        _, _, f = _ssd_chunk_fwd(z, xbc, tail, dt_ref[...], hprev, cw, cb_ref[...], dtb_ref[...], al_ref[...],
                                 dsk_ref[...], nw, tril_ref[...])
        dout = do_ref[...].astype(F32)
        dh = dh_ref[...]
        ghat = f["gz"] * f["r"]
        dn = dout * nw
        dgz = f["r"] * (dn - ghat * jnp.mean(dn * ghat, axis=-1, keepdims=True))
        dnw_ref[...] += jnp.sum(dout * ghat, axis=0, keepdims=True)
        sz = f["sz"]
        dy = dgz * (z * sz)
        dp_ref[:, :D_S5] = jnp.zeros((L, D_S5), BF16)
        dp_ref[:, D_S5:D_S5 + D_SSD] = (dgz * f["y"] * sz * (1.0 + z * (1.0 - sz))).astype(BF16)
        xs = f["xs"]
        ddsk_ref[...] += jnp.sum(_headsum(dy * xs, hsum), axis=0, keepdims=True)
        dyb = dy.astype(BF16)
        dye = (dy * f["exp_a_x"]).astype(BF16)
        dhb = dh.astype(BF16)
        lane = lax.broadcasted_iota(jnp.int32, (L, 128), 1)
        sub = lax.broadcasted_iota(jnp.int32, (128, L), 0)
        zero_b = jnp.zeros((L, 128), BF16)
        rsum = jnp.zeros((L, 128), F32)
        csum_t = jnp.zeros((128, L), F32)
        dx_pairs, dxst_parts, db_parts, dc_parts, dhp_parts = [], [], [], [], []
        for g in range(SSD_NGROUPS):
            gs = slice(g * SSD_GW, (g + 1) * SSD_GW)
            bg = f["bm"][:, g * 128:(g + 1) * 128].astype(BF16)
            cg = f["cm"][:, g * 128:(g + 1) * 128].astype(BF16)
            cbg = f["cbs"][g]
            dcb_g = jnp.zeros((L, L), F32)
            for j in (2 * g, 2 * g + 1):
                xp = f["xhb"][:, j * 128:(j + 1) * 128]
                dyp = dyb[:, j * 128:(j + 1) * 128]
                dxs = []
                for half, h in enumerate((2 * j, 2 * j + 1)):
                    lm = f["lms"][h]
                    dyh = jnp.where(f["low"], dyp, zero_b) if half == 0 else jnp.where(f["low"], zero_b, dyp)
                    dw = jnp.where(f["causal"], _dot_nt(dyh, xp), 0.0)
                    w = cbg * lm
                    e = dw * w
                    dcb_g = dcb_g + dw * lm
                    rsum = jnp.where(lane == h, jnp.sum(e, axis=1, keepdims=True), rsum)
                    csum_t = jnp.where(sub == h, jnp.sum(e, axis=0, keepdims=True), csum_t)
                    dxs.append(_dot_tn(w.astype(BF16), dyp))
                dx_pairs.append(jnp.where(f["low"], dxs[0], dxs[1]))
            dcbb = dcb_g.astype(BF16)
            dxst_parts.append(f["dec_x"][:, gs] * _dot(bg, dhb[:, gs]))
            dc_parts.append(_dot(dcbb, bg) + _dot_nt(dye[:, gs], f["hb"][:, gs]))
            db_parts.append(_dot_tn(dcbb, cg) + _dot_nt(f["xd"][:, gs], dhb[:, gs]))
            dhp_parts.append(f["exp_al_x"][:, gs] * dh[:, gs] + _dot_tn(cg, dye[:, gs]))
        dxst = jnp.concatenate(dxst_parts, axis=1)
        dxh = jnp.concatenate(dx_pairs, axis=1) + dxst
        q = _headsum(f["yoff"] * dy, hsum)
        dstate = _headsum(f["xh"] * dxst, hsum)
        h0t = jnp.sum(_headsum(dh * hprev, hsum), axis=0, keepdims=True) * f["exp_al"]
        da = _dot_hi(triu_ref[...], rsum - csum_t.T + q) + _dot_hi(trils_ref[...], dstate) + h0t
        dt, a_h = f["dt"], f["a_h"]
        ddt = _headsum(dxh * xs, hsum) + da * a_h
        dal_ref[...] += jnp.sum(da * dt, axis=0, keepdims=True) * a_h
        ddtraw = ddt * _sigmoid(f["pre"])
        first16 = lane < SSD_HEADS
        ddtraw = jnp.where(first16, ddtraw, 0.0)
        ddt_ref[...] = ddtraw
        ddtb_ref[...] += jnp.sum(ddtraw, axis=0, keepdims=True)
        dh_ref[...] = jnp.concatenate(dhp_parts, axis=1)
        dxa = jnp.concatenate([dxh * f["dt_x"] + f["d_x"] * dy] + db_parts + dc_parts, axis=1)
        sig, conv = f["sig"], f["conv"]
        dconv = dxa * sig * (1.0 + conv * (1.0 - sig))
        dcb_ref[...] += jnp.sum(dconv, axis=0, keepdims=True)
        rows = [jnp.sum(dconv * f["sh"][3 - k], axis=0, keepdims=True) for k in range(SSD_CONV)]
        dcw_ref[...] += jnp.concatenate(rows + [jnp.zeros((8 - SSD_CONV, D_XBC), F32)], axis=0)
        de = jnp.concatenate([dconv, dnext_ref[...]], axis=0)
        dxbc = cw[3:4] * dconv
        for s in (1, 2, 3):
            dxbc = dxbc + cw[3 - s:4 - s] * pltpu.roll(de, L + 8 - s, 0)[:L]
        dp_ref[:, D_S5 + D_SSD:] = dxbc.astype(BF16)
        dnext_ref[...] = dconv[:8]

    rev = lambda i: nc - 1 - i
    acc = lambda shape: pl.BlockSpec(shape, lambda i: (0, 0))
    tri = pl.BlockSpec((L, L), lambda i: (0, 0))
    return _call(
        body, grid=(nc,),
        in_specs=[pl.BlockSpec((L, D_SSD), lambda i: (rev(i), 1)), pl.BlockSpec((L, D_XBC), lambda i: (rev(i), 1)),
                  pl.BlockSpec((8, D_XBC), lambda i: (jnp.maximum(rev(i) * (L // 8) - 1, 0), 1)),
                  pl.BlockSpec((L, 128), lambda i: (rev(i), 0)),
                  pl.BlockSpec((1, SSD_STATE, D_SSD), lambda i: (rev(i), 0, 0)),
                  pl.BlockSpec((L, D_SSD), lambda i: (rev(i), 1))] + _ssd_param_specs()
                 + [tri, tri, tri, pl.BlockSpec((D_SSD, 128), lambda i: (0, 0))],
        out_specs=[pl.BlockSpec((L, D_MAIN), lambda i: (rev(i), 0)), pl.BlockSpec((L, 128), lambda i: (rev(i), 0)),
                   acc((8, D_XBC)), acc((1, D_XBC)), acc((1, 128)), acc((1, 128)), acc((1, 128)), acc((1, D_SSD))],
        out_shape=[jax.ShapeDtypeStruct((T, D_MAIN), BF16),
                   jax.ShapeDtypeStruct((T, 128), F32), jax.ShapeDtypeStruct((8, D_XBC), F32),
                   jax.ShapeDtypeStruct((1, D_XBC), F32), jax.ShapeDtypeStruct((1, 128), F32),
                   jax.ShapeDtypeStruct((1, 128), F32), jax.ShapeDtypeStruct((1, 128), F32),
                   jax.ShapeDtypeStruct((1, D_SSD), F32)],
        scratch_shapes=[pltpu.VMEM((SSD_STATE, D_SSD), F32), pltpu.VMEM((8, D_XBC), F32)],
        sem=("arbitrary",), name=name, comm=comm,
        args=(proj, proj, proj, dtraw, hs, dymix, *params, tril, triu, trils, headsum))


def _s5_blockdiag(v, rows_per_group, cols_per_group):
    eye = jnp.eye(S5_SEG, dtype=v.dtype)
    w = v[:, :, :, None, :] * eye[None, :, None, :, None]
    return w.reshape(S5_CH, 8 * rows_per_group, 8 * cols_per_group)


def _s5_blockdiag_extract(w, rows_per_group, cols_per_group):
    eye = jnp.eye(S5_SEG, dtype=w.dtype)
    w5 = w.reshape(S5_CH, 8, rows_per_group, 8, cols_per_group)
    return jnp.sum(w5 * eye[None, :, None, :, None], axis=3)


TM = 512

OFF = dict(mlp_w2=0, mlp_w1=1024, w_out=2048, s5_w_glu=2560, xa_wq=2816, xa_wk=3072, xa_wv=3328, xa_wo=3584, w_in=3840)
ROWS = dict(mlp_w2=1024, mlp_w1=1024, w_out=512, s5_w_glu=256, xa_wq=256, xa_wk=256, xa_wv=256, xa_wo=256, w_in=1028)
TAIL_OFF = 4872
TAIL_ROWS = 120
PACK_ROWS = TAIL_OFF + TAIL_ROWS
HALF_ROWS = PACK_ROWS // 2
ROWS_ALL = (0, PACK_ROWS, 832, True)
ROWS_EARLY = (0, OFF["w_in"], 640, False)
ROWS_LATE = (OFF["w_in"], PACK_ROWS - OFF["w_in"], 192, True)
N_SHARD = 4
SMALL_L = ("norm_mix", "s5_a_re", "s5_a_im", "s5_log_dt", "s5_b_re", "s5_b_im", "s5_c_re", "s5_c_im", "s5_d",
           "ssd_conv_w", "ssd_conv_b", "ssd_dt_bias", "ssd_a_log", "ssd_d", "ssd_norm", "norm_xattn", "norm_mem",
           "norm_mlp")
SMALL_Q = 72
CONV_ROWS = 8


def _place():
    x, y, c = lax.axis_index("x"), lax.axis_index("y"), lax.axis_index("c")
    chips = [(1 - x, y), (x, 1 - y), (1 - x, 1 - y)]
    return x, y, c, 2 * x + y, chips, (x, y, 1 - c)


def _remote(src, dst, send_sem, recv_sem, to):
    return pltpu.make_async_remote_copy(src_ref=src, dst_ref=dst, send_sem=send_sem, recv_sem=recv_sem,
                                        device_id=to, device_id_type=MESH_T)


def _half(ref, c):
    return ref.at[pl.ds(pl.multiple_of(c * HALF_ROWS, 16), HALF_ROWS), :]


def _dma_sems(*counts):
    return [pltpu.SemaphoreType.DMA((n,)) for n in counts]


def place_own(wpack, shard_idx):
    tile = PACK_ROWS // 4

    def body(s_ref, w_ref, o_ref):
        o_ref[0] = w_ref[...]

    return pl.pallas_call(
        body,
        grid_spec=pltpu.PrefetchScalarGridSpec(
            num_scalar_prefetch=1, grid=(4,),
            in_specs=[pl.BlockSpec((tile, D_MODEL), lambda i, s_ref: (i, 0))],
            out_specs=pl.BlockSpec((1, tile, D_MODEL), lambda i, s_ref: (s_ref[0], i, 0))),
        out_shape=jax.ShapeDtypeStruct((N_SHARD,) + wpack.shape, wpack.dtype),
        compiler_params=_cp("parallel"), name="place_own")(shard_idx, wpack)


def gather_over_ici(wpack, placed):
    def copies(ci, co, cs):
        w_ref, (out_ref,), (send, recv) = ci[0], co, cs
        x, y, c, s, chips, sibling = _place()
        sends = [_remote(_half(w_ref, c), _half(out_ref.at[s], c), send.at[j], recv.at[j], (*chip, c))
                 for j, chip in enumerate(chips)]
        lands = [_half(out_ref.at[2 * chip[0] + chip[1]], c) for chip in chips]
        arrivals = [_remote(land, land, send.at[j], recv.at[j], sibling) for j, land in enumerate(lands)]
        return sends, arrivals

    def start(ci, co, cs):
        for cp in copies(ci, co, cs)[0]:
            cp.start()

    def wait(ci, co, cs):
        sends, arrivals = copies(ci, co, cs)
        for cp in arrivals:
            cp.wait_recv()
        for cp in sends:
            cp.wait_send()

    return Comm("gather_ici", [wpack, placed], [jax.ShapeDtypeStruct(placed.shape, placed.dtype)], _dma_sems(3, 3),
                start, wait, aliases={1: 0})


def gather_to_sibling(landed):
    def copies(co, cs):
        (out_ref,), (send, recv) = co, cs
        x, y, c, s, chips, sibling = _place()
        slots = [out_ref.at[2 * chip[0] + chip[1]] for chip in chips]
        sends = [_remote(_half(slot, c), _half(slot, c), send.at[j], recv.at[j], sibling) for j, slot in enumerate(slots)]
        arrivals = [_remote(_half(slot, 1 - c), _half(slot, 1 - c), send.at[j], recv.at[j], sibling)
                    for j, slot in enumerate(slots)]
        return sends, arrivals

    def start(ci, co, cs):
        for cp in copies(co, cs)[0]:
            cp.start()

    def wait(ci, co, cs):
        sends, arrivals = copies(co, cs)
        for cp in arrivals:
            cp.wait_recv()
        for cp in sends:
            cp.wait_send()

    return Comm("gather_d2d", [landed], [jax.ShapeDtypeStruct(landed.shape, landed.dtype)], _dma_sems(3, 3), start, wait,
                aliases={0: 0})


def _range_half(ref, rows, c):
    half = rows[1] // 2
    return pl.ds(pl.multiple_of(rows[0] + c * half, 16), half)


def exchange_halves(gpack, rows):
    def copy(ci, co, cs):
        (g_ref,), (out_ref,), (send, recv) = ci, co, cs
        x, y, c, s, chips, sibling = _place()
        return _remote(g_ref.at[:, _range_half(g_ref, rows, 1 - c), :], out_ref, send.at[0], recv.at[0], sibling)

    return Comm("exchange", [gpack], [jax.ShapeDtypeStruct((N_SHARD, rows[1] // 2, D_MODEL), F32)], _dma_sems(1, 1),
                lambda ci, co, cs: copy(ci, co, cs).start(), lambda ci, co, cs: copy(ci, co, cs).wait())


def scatter_chips(csum):
    def copies(ci, co, cs):
        (c_ref,), (out_ref,), (send, recv) = ci, co, cs
        x, y, c, s, chips, sibling = _place()
        sends = [_remote(c_ref.at[2 * chip[0] + chip[1]], out_ref.at[s], send.at[j], recv.at[j], (*chip, c))
                 for j, chip in enumerate(chips)]
        arrivals = [_remote(c_ref.at[2 * chip[0] + chip[1]], out_ref.at[2 * chip[0] + chip[1]], send.at[j], recv.at[j],
                            (*chip, c)) for j, chip in enumerate(chips)]
        return sends, arrivals

    def start(ci, co, cs):
        for cp in copies(ci, co, cs)[0]:
            cp.start()

    def wait(ci, co, cs):
        sends, arrivals = copies(ci, co, cs)
        for cp in arrivals:
            cp.wait_recv()
        for cp in sends:
            cp.wait_send()

    return Comm("scatter", [csum], [jax.ShapeDtypeStruct(csum.shape, csum.dtype)], _dma_sems(3, 3), start, wait)


def share_reduced(gshards, smalls, l, rows):
    with_tail = rows[3]

    def copies(ci, co, cs):
        (g_ref, sm_ref), (send, recv, loc) = co, cs
        x, y, c, s, chips, sibling = _place()
        my_half = g_ref.at[l, _range_half(g_ref, rows, c), :]
        tail = g_ref.at[l, pl.ds(PACK_ROWS - TAIL_ROWS, TAIL_ROWS), :]
        big = _remote(my_half, my_half, send.at[0], recv.at[0], sibling)
        keep_tail = pltpu.make_async_copy(tail, sm_ref.at[l, s], loc.at[0])
        tails = [_remote(tail, sm_ref.at[l, s], send.at[1 + j], recv.at[1 + j], (*chip, 1)) for j, chip in enumerate(chips)]
        tails += [_remote(tail, sm_ref.at[l, s], send.at[4 + j], recv.at[1 + j], (*chip, 0)) for j, chip in enumerate(chips)]
        tails.append(_remote(tail, sm_ref.at[l, s], send.at[7], recv.at[4], sibling))
        other = g_ref.at[l, _range_half(g_ref, rows, 1 - c), :]
        big_in = _remote(other, other, send.at[0], recv.at[0], sibling)
        slots = [sm_ref.at[l, 2 * chip[0] + chip[1]] for chip in chips]
        tails_in = [_remote(slot, slot, send.at[1 + j], recv.at[1 + j], sibling) for j, slot in enumerate(slots)]
        sib_tail_in = _remote(sm_ref.at[l, s], sm_ref.at[l, s], send.at[7], recv.at[4], sibling)
        return c, big, keep_tail, tails, big_in, tails_in, sib_tail_in

    def start(ci, co, cs):
        c, big, keep_tail, tails, _, _, _ = copies(ci, co, cs)
        big.start()
        if with_tail:
            @pl.when(c == 1)
            def _():
                keep_tail.start()
                for cp in tails:
                    cp.start()

    def wait(ci, co, cs):
        c, big, keep_tail, tails, big_in, tails_in, sib_tail_in = copies(ci, co, cs)
        big_in.wait_recv()
        big.wait_send()
        if with_tail:
            for cp in tails_in:
                cp.wait_recv()

            @pl.when(c == 0)
            def _():
                sib_tail_in.wait_recv()

            @pl.when(c == 1)
            def _():
                for cp in tails:
                    cp.wait_send()
                keep_tail.wait()

    sds = lambda a: jax.ShapeDtypeStruct(a.shape, a.dtype)
    return Comm("share", [gshards, smalls], [sds(gshards), sds(smalls)], _dma_sems(8, 5, 1), start, wait,
                aliases={0: 0, 1: 1})


def _consts():
    e = np.zeros((S5_STATE, S5_STATE * S5_GROUP), np.float32)
    for p in range(S5_STATE):
        e[p, p * S5_GROUP:(p + 1) * S5_GROUP] = 1.0
    hs = np.zeros((D_SSD, 128), np.float32)
    for h in range(SSD_HEADS):
        hs[h * SSD_HEADDIM:(h + 1) * SSD_HEADDIM, h] = 1.0
    ones = np.ones((SSD_L, SSD_L), np.float32)
    return dict(expand=jnp.asarray(e), expand_t=jnp.asarray(e.T), headsum=jnp.asarray(hs).astype(BF16),
                tril=jnp.asarray(np.tril(ones)), triu=jnp.asarray(np.triu(ones)), trils=jnp.asarray(np.tril(ones, -1)))


def _s5_mats(w, cst):
    b_re = w["s5_b_re"].reshape(S5_GROUPS, S5_STATE * S5_GROUP)
    b_im = w["s5_b_im"].reshape(S5_GROUPS, S5_STATE * S5_GROUP)
    abr, abi, apr, api, bbr, bbi = s5_params_fwd(w["s5_a_re"], w["s5_a_im"], w["s5_log_dt"].reshape(S5_GROUPS, 1),
                                                 b_re, b_im, cst["expand"], name="s5_params_fwd")
    t = lambda v: v.reshape(S5_CH, 8, S5_STATE, S5_GROUP).transpose(0, 1, 3, 2)
    c4 = lambda v: v.reshape(S5_CH, 8, S5_GROUP, S5_STATE).transpose(0, 1, 3, 2)
    vec = lambda v: v.reshape(S5_CH, 1, S5_NST)
    return dict(bbr=_s5_blockdiag(t(bbr), S5_GROUP, S5_STATE).astype(BF16),
                bbi=_s5_blockdiag(t(bbi), S5_GROUP, S5_STATE).astype(BF16),
                ccr=_s5_blockdiag(c4(w["s5_c_re"]), S5_STATE, S5_GROUP).astype(BF16),
                cci=_s5_blockdiag(c4(w["s5_c_im"]), S5_STATE, S5_GROUP).astype(BF16),
                abr=vec(abr), abi=vec(abi), apr=vec(apr), api=vec(api),
                dsk=w["s5_d"].reshape(S5_CH, 1, S5_CW), b_re=b_re, b_im=b_im)


def _layer_fwd(x, mem, w, cst, next_pack=None):
    sv = {}
    g = lambda n: w[n].reshape(1, -1)
    pack = w["pack"]
    proj, h0, r0, dtraw = norm_matmul(x, g("norm_mix"), w["w_in_t"], w["w_dt_t"], tm=TM, tn=1024, name="in_proj",
                                      w_transposed=True)
    s5m = _s5_mats(w, cst)
    (ys, csr, csi), landed = s5_fwd(proj, s5m["bbr"], s5m["bbi"], s5m["ccr"], s5m["cci"], s5m["abr"], s5m["abi"],
                                    s5m["apr"], s5m["api"], s5m["dsk"], name="s5_fwd",
                                    comm=None if next_pack is None else gather_over_ici(*next_pack))
    (ymix, tglu), gathered = glu_fwd(ys, w["s5_w_glu"], tm=TM, name="glu_fwd",
                                     comm=None if next_pack is None else gather_to_sibling(landed[0]))
    ssdp = _ssd_params(w["ssd_conv_w"], w["ssd_conv_b"], w["ssd_dt_bias"], w["ssd_a_log"], w["ssd_d"], w["ssd_norm"])
    (ymix, hs), _ = ssd_fwd(proj, dtraw, ssdp, cst["tril"], ymix, name="ssd_fwd")
    x1 = matmul_res(ymix, pack, x, tm=TM, tn=1024, tk=1024, name="out_proj",
                    wspec=pl.BlockSpec((2, 512, 1024), lambda i, j, k: (k, OFF["w_out"] // 512, j)))
    q, h1, r1 = norm_matmul(x1, g("norm_xattn"), w["xa_wq"], tm=TM, tn=1024, name="q_proj")
    kv, hm, rm = norm_matmul(mem, g("norm_mem"), w["xa_wkv"], tm=mem.shape[0], tn=1024, name="kv_proj")
    o = attn_fwd(q, kv, tm=TM, name="attn_fwd")
    x2 = matmul_res(o, w["xa_wo"], x1, tm=TM, tn=1024, tk=1024, name="attn_out")
    f1, h2, r2 = norm_matmul(x2, g("norm_mlp"), pack, tm=TM, tn=1024, name="mlp_up", n_out=D_FF,
                             wspec=pl.BlockSpec((1, D_MODEL, 1024), lambda i, j: (j, OFF["mlp_w1"] // 1024, 0)))
    x3 = matmul_res(f1, pack, x2, act="relu2", tm=TM, tn=1024, tk=1024, name="mlp_down",
                    wspec=pl.BlockSpec((1, 1024, 1024), lambda i, j, k: (k, OFF["mlp_w2"] // 1024, j)))
    sv.update(x=x, proj=proj, h0=h0, r0=r0, dtraw=dtraw, s5m=s5m, ys=ys, csr=csr, csi=csi, tglu=tglu, ssdp=ssdp,
              hs=hs, ymix=ymix, x1=x1, q=q, h1=h1, r1=r1, kv=kv, hm=hm, rm=rm, o=o, x2=x2, f1=f1, h2=h2, r2=r2)
    return x3, sv, (gathered[0] if next_pack is not None else None)


def _layer_bwd(dx3, mem, w, sv, cst, extra_small=None, reduce_hooks=None, early=None):
    gr = {}
    g = lambda n: w[n].reshape(1, -1)
    pack = w["pack"]
    pshape = (N_SHARD, PACK_ROWS, D_MODEL)
    ps = lambda rows, f: pl.BlockSpec((1, rows, 1024), f)
    ps4 = lambda rows, f: pl.BlockSpec((N_SHARD, rows, 1024), f)
    dh1 = matmul_nt(dx3, pack, epi="relu2bwd", epi_args=(sv["f1"],), tm=TM, tko=1024, tn=1024, name="mlp_down_dx",
                    wspec=ps(1024, lambda i, k, n: (k, OFF["mlp_w2"] // 1024, n)), k_out=D_FF,
                    comm=reduce_hooks.exchange() if reduce_hooks else None)
    if reduce_hooks:
        dh1, recv = dh1
        reduce_hooks.after_exchange(recv[0])
    gp = matmul_tn(sv["f1"], dx3, act="relu2", tk=1024, tn=1024, tt=TM, name="mlp_down_dw", pack_shape=pshape,
                   pack_spec=ps(1024, lambda k, n, t: (k, OFF["mlp_w2"] // 1024, 0)))
    gp = matmul_tn(sv["h2"], dh1, tk=1024, tn=1024, tt=TM, name="mlp_up_dw", pack=gp, pack_shape=pshape,
                   pack_spec=ps(1024, lambda k, n, t: (n, OFF["mlp_w1"] // 1024, 0)))
    dx2, gr["norm_mlp"] = matmul_nt(dh1, pack, epi="rmsbwd", epi_args=(sv["x2"], sv["r2"], g("norm_mlp"), dx3),
                                    tm=TM, tko=1024, tn=1024, name="mlp_up_dx", k_out=D_MODEL,
                                    wspec=ps(1024, lambda i, k, n: (n, OFF["mlp_w1"] // 1024, 0)))
    do = matmul_nt(dx2, w["xa_wo"], tm=TM, tko=1024, tn=1024, name="attn_out_dx")
    gp = matmul_tn(sv["o"], dx2, tk=1024, tn=1024, tt=TM, name="attn_out_dw", pack=gp, pack_shape=pshape,
                   pack_spec=ps4(256, lambda k, n, t: (0, OFF["xa_wo"] // 256, 0)))
    dq, dkv = attn_bwd(sv["q"], sv["kv"], do, tm=TM, name="attn_bwd")
    gp = matmul_tn(sv["h1"], dq, tk=1024, tn=1024, tt=TM, name="q_proj_dw", pack=gp, pack_shape=pshape,
                   pack_spec=ps4(256, lambda k, n, t: (0, OFF["xa_wq"] // 256, 0)))
    dx1, gr["norm_xattn"] = matmul_nt(dq, w["xa_wq"], epi="rmsbwd", epi_args=(sv["x1"], sv["r1"], g("norm_xattn"), dx2),
                                      tm=TM, tko=1024, tn=1024, name="q_proj_dx")
    M = mem.shape[0]
    gp = matmul_tn(sv["hm"], dkv, tk=1024, tn=1024, tt=M, name="kv_proj_dw", pack=gp, pack_shape=pshape,
                   pack_spec=ps4(256, lambda k, n, t: (0, OFF["xa_wk"] // 256 + n, 0)))
    _, gr["norm_mem"] = matmul_nt(dkv, w["xa_wkv"], epi="rmsbwd",
                                  epi_args=(mem, sv["rm"], g("norm_mem"), jnp.zeros_like(mem)),
                                  tm=M, tko=1024, tn=1024, name="kv_proj_dx")
    dymix = matmul_nt(dx1, pack, tm=TM, tko=1024, tn=1024, name="out_proj_dx", k_out=2 * D_MODEL,
                      wspec=pl.BlockSpec((2, 512, 1024), lambda i, k, n: (k, OFF["w_out"] // 512, n)))
    gp = matmul_tn(sv["ymix"], dx1, tk=2048, tn=1024, tt=TM, name="out_proj_dw", pack=gp, pack_shape=pshape,
                   pack_spec=ps4(512, lambda k, n, t: (0, OFF["w_out"] // 512, 0)))
    (dproj, ddtraw, dcw, dcb, ddtb, dal, ddsk, dnw), parts = ssd_bwd(
        sv["proj"], sv["dtraw"], sv["hs"], dymix, sv["ssdp"], cst["tril"], cst["triu"], cst["trils"], cst["headsum"],
        name="ssd_bwd", comm=reduce_hooks.scatter() if reduce_hooks else None)
    if reduce_hooks:
        reduce_hooks.after_scatter(parts[0])
    gr["ssd_conv_w"], gr["ssd_conv_b"] = dcw[:SSD_CONV], dcb[0]
    gr["ssd_dt_bias"], gr["ssd_a_log"], gr["ssd_d"] = ddtb[0, :SSD_HEADS], dal[0, :SSD_HEADS], ddsk[0, :SSD_HEADS]
    gr["ssd_norm"] = dnw[0]
    gp = glu_dw(dymix, sv["ys"], sv["tglu"], gp, tt=TM, name="glu_dw",
                pack_spec=ps4(256, lambda i: (0, OFF["s5_w_glu"] // 256, 0)))
    first = _Reduction(gp, early[0], early[1], None, None, ROWS_EARLY) if early else None
    comm = _combine(reduce_hooks.share() if reduce_hooks else None, first.exchange() if first else None)
    (dys,), outs = glu_bwd(dymix, sv["ys"], sv["tglu"], w["s5_w_glu"], tm=TM, name="glu_dx", comm=comm)
    if comm is not None:
        if reduce_hooks:
            reduce_hooks.after_share(outs[:2])
            outs = outs[2:]
        if first:
            first.gshards, first.smalls = reduce_hooks.gshards, reduce_hooks.smalls
            first.after_exchange(outs[0])
    s5m = sv["s5m"]
    (dproj, dbbr, dbbi, dccr, dcci, dd, dabr, dabi), parts = s5_bwd(
        sv["proj"], dys, sv["csr"], sv["csi"], s5m["bbr"], s5m["bbi"], s5m["ccr"], s5m["cci"], s5m["abr"], s5m["abi"],
        s5m["apr"], s5m["api"], s5m["dsk"], dproj, name="s5_bwd", comm=first.scatter() if first else None)
    if first:
        first.after_scatter(parts[0])
    tb = lambda v: _s5_blockdiag_extract(v, S5_GROUP, S5_STATE).transpose(0, 1, 3, 2).reshape(S5_GROUPS, -1)
    tc = lambda v: _s5_blockdiag_extract(v, S5_STATE, S5_GROUP).transpose(0, 1, 3, 2).reshape(S5_GROUPS, S5_GROUP, S5_STATE)
    gr["s5_c_re"], gr["s5_c_im"] = tc(dccr), tc(dcci)
    gr["s5_d"] = dd.reshape(S5_GROUPS, S5_GROUP)
    dar, dai, dld, dbr, dbi = s5_params_bwd(
        w["s5_a_re"], w["s5_a_im"], w["s5_log_dt"].reshape(S5_GROUPS, 1), s5m["b_re"], s5m["b_im"],
        dabr.reshape(S5_GROUPS, S5_STATE), dabi.reshape(S5_GROUPS, S5_STATE), tb(dbbr), tb(dbbi),
        cst["expand"], cst["expand_t"], name="s5_params_bwd")
    gr["s5_a_re"], gr["s5_a_im"], gr["s5_log_dt"] = dar, dai, dld[:, 0]
    gr["s5_b_re"] = dbr.reshape(S5_GROUPS, S5_STATE, S5_GROUP)
    gr["s5_b_im"] = dbi.reshape(S5_GROUPS, S5_STATE, S5_GROUP)
    wt_shape = (D_MAIN + D_DT_PAD, D_MODEL)
    dwt = matmul_tn(dproj, sv["h0"], tk=1024, tn=1024, tt=TM, name="in_proj_dw", pack_shape=wt_shape,
                    pack_spec=pl.BlockSpec((1024, 1024), lambda k, n, t: (k, 0)))
    dwt = matmul_tn(ddtraw, sv["h0"], tk=D_DT_PAD, tn=1024, tt=TM, name="in_proj_dt_dw", pack=dwt, pack_shape=wt_shape,
                    pack_spec=pl.BlockSpec((D_DT_PAD, 1024), lambda k, n, t: (D_MAIN // D_DT_PAD, 0)))
    dx0 = matmul_nt(dproj, w["w_in_t"], g2=ddtraw, w2=w["w_dt_t"], w_is_nk=True, epi="rmsbwd",
                    epi_args=(sv["x"], sv["r0"], g("norm_mix"), dx1), tm=TM, tko=1024, tn=1024, name="in_proj_dx",
                    comm=first.share() if first else None)
    if first:
        dx0, shared = dx0
        first.after_share(shared)
    dx0, gr["norm_mix"] = dx0
    gr = {k: (v[0] if k.startswith("norm_") else v) for k, v in gr.items()}
    for t in range(N_SHARD):
        shard_rows = lax.slice_in_dim(dwt, t * ROWS["w_in"], (t + 1) * ROWS["w_in"], axis=0)
        gp = lax.dynamic_update_slice(gp, shard_rows[None], (t, OFF["w_in"], 0))
    small = [gr[n].reshape(-1) for n in SMALL_L] + ([] if extra_small is None else [extra_small.reshape(-1)])
    small = jnp.concatenate(small)
    small = jnp.pad(small, (0, N_SHARD * SMALL_Q * D_MODEL - small.size)).reshape(N_SHARD, SMALL_Q, D_MODEL)
    gap = TAIL_OFF - OFF["w_in"] - ROWS["w_in"]
    gp = lax.dynamic_update_slice(gp, jnp.pad(small, ((0, 0), (gap, TAIL_ROWS - SMALL_Q), (0, 0))),
                                  (0, TAIL_OFF - gap, 0))
    return dx0, gp, first


def _local_step(x, mem, target, layers, norm_final):
    cst = _consts()
    saved = []
    for l in range(DEPTH):
        x, sv, _ = _layer_fwd(x, mem, layers[l], cst)
        saved.append(sv)
    loss, dx, dgf = loss_head(x, norm_final.reshape(1, -1), target, tm=TM, name="loss_head")
    packs = [None] * DEPTH
    for l in reversed(range(DEPTH)):
        dx, packs[l], _ = _layer_bwd(dx, mem, layers[l], saved[l], cst, extra_small=dgf[0] if l == DEPTH - 1 else None)
    return loss, dx, packs


def sum_halves(gpack, recv, c_idx, rows):
    first, count, tile, _ = rows
    half = count // 2
    nb = half // tile

    def body(c_ref, a_ref, b_ref, o_ref):
        o_ref[...] = (a_ref[...] + b_ref[...]).astype(BF16)

    blk = (1, tile, D_MODEL)
    return pl.pallas_call(
        body,
        grid_spec=pltpu.PrefetchScalarGridSpec(
            num_scalar_prefetch=1, grid=(N_SHARD, nb),
            in_specs=[pl.BlockSpec(blk, lambda t, i, c_ref: (t, first // tile + c_ref[0] * nb + i, 0)),
                      pl.BlockSpec(blk, lambda t, i, c_ref: (t, i, 0))],
            out_specs=pl.BlockSpec(blk, lambda t, i, c_ref: (t, i, 0))),
        out_shape=jax.ShapeDtypeStruct((N_SHARD, half, D_MODEL), BF16),
        compiler_params=_cp("parallel", "parallel"), name="sum_halves")(c_idx, gpack, recv)


def sum_chips(parts, csum, gshards, l, place_idx, rows):
    first, count, tile, _ = rows
    nb = count // 2 // tile

    def body(pi_ref, p0, p1, p2, p3, own, g_ref, o_ref):
        s = pi_ref[0]
        vals = [jnp.where(s == k, own[0], p[0]).astype(F32) for k, p in enumerate((p0, p1, p2, p3))]
        o_ref[0] = ((vals[0] + vals[1]) + vals[2]) + vals[3]

    blk = (1, tile, D_MODEL)
    part_spec = lambda k: pl.BlockSpec(blk, lambda i, pi_ref: (jnp.where(pi_ref[0] == k, (k + 1) % N_SHARD, k), i, 0))
    return pl.pallas_call(
        body,
        grid_spec=pltpu.PrefetchScalarGridSpec(
            num_scalar_prefetch=1, grid=(nb,),
            in_specs=[part_spec(k) for k in range(N_SHARD)]
                     + [pl.BlockSpec(blk, lambda i, pi_ref: (pi_ref[0], i, 0)), _ANY],
            out_specs=pl.BlockSpec(blk, lambda i, pi_ref: (l, first // tile + pi_ref[1] * nb + i, 0))),
        out_shape=jax.ShapeDtypeStruct(gshards.shape, F32), input_output_aliases={6: 0},
        compiler_params=_cp("parallel"), name="sum_chips")(place_idx, parts, parts, parts, parts, csum, gshards)


class _Reduction:
    def __init__(self, gpack, layer, place_idx, gshards, smalls, rows=ROWS_ALL):
        self.gpack, self.layer, self.place_idx, self.gshards, self.smalls = gpack, layer, place_idx, gshards, smalls
        self.rows = rows

    def exchange(self):
        return exchange_halves(self.gpack, self.rows)

    def after_exchange(self, recv):
        self.csum = sum_halves(self.gpack, recv, self.place_idx[1:], self.rows)

    def scatter(self):
        return scatter_chips(self.csum)

    def after_scatter(self, parts):
        self.gshards = sum_chips(parts, self.csum, self.gshards, self.layer, self.place_idx, self.rows)

    def share(self):
        return share_reduced(self.gshards, self.smalls, self.layer, self.rows)

    def after_share(self, shared):
        self.gshards, self.smalls = shared

    def run_alone(self):
        self.after_exchange(_comm_only(self.exchange())[0])
        self.after_scatter(_comm_only(self.scatter())[0])
        self.after_share(_comm_only(self.share()))
        return self.gshards, self.smalls


def adamw(w, g, m, v, *, name):
    shape = w.shape
    cols = shape[-1]
    rows = w.size // cols
    tr = 512 if rows % 512 == 0 else rows
    c1 = 1.0 / (1.0 - ADAM_B1 ** ADAM_STEP)
    c2 = 1.0 / (1.0 - ADAM_B2 ** ADAM_STEP)

    def body(w_ref, g_ref, m_ref, v_ref, d_ref, nm_ref, nv_ref):
        gv = g_ref[...]
        nm = ADAM_B1 * m_ref[...] + (1.0 - ADAM_B1) * gv
        nv = ADAM_B2 * v_ref[...] + (1.0 - ADAM_B2) * (gv * gv)
        d_ref[...] = -ADAM_LR * ((nm * c1) / (jnp.sqrt(nv * c2) + ADAM_EPS) + ADAM_WD * w_ref[...])
        nm_ref[...] = nm
        nv_ref[...] = nv

    spec = pl.BlockSpec((tr, cols), lambda i: (i, 0))
    sds = jax.ShapeDtypeStruct((rows, cols), F32)
    outs = pl.pallas_call(body, grid=(rows // tr,), in_specs=[spec] * 4, out_specs=[spec] * 3, out_shape=[sds] * 3,
                          compiler_params=_cp("parallel"), name=name)(
                              *[a.reshape(rows, cols) for a in (w, g, m, v)])
    return [o.reshape(shape) for o in outs]


def _own_pack(wts, l):
    rows = [wts[n][l].T if n == "w_in" else wts[n][l].reshape(ROWS[n], D_MODEL) for n in sorted(OFF, key=OFF.get)]
    cw = wts["ssd_conv_w"][l].reshape(-1)
    hi = lax.reduce_precision(cw, 8, 7)
    mid = lax.reduce_precision(cw - hi, 8, 7)
    lo = lax.reduce_precision(cw - hi - mid, 8, 7)
    conv = jnp.pad(jnp.concatenate([hi, mid, lo]), (0, CONV_ROWS * D_MODEL - 3 * cw.size)).reshape(CONV_ROWS, D_MODEL)
    gap = jnp.zeros((TAIL_OFF - OFF["w_in"] - ROWS["w_in"], D_MODEL), F32)
    rest = jnp.zeros((TAIL_ROWS - CONV_ROWS, D_MODEL), F32)
    return jnp.concatenate(rows + [gap, conv, rest], axis=0).astype(BF16)


def _layer_weights(gathered, wts, l):
    w = {n: wts[n][l] for n in SMALL_L if n != "ssd_conv_w"}
    w["pack"] = gathered
    square = lambda n: gathered[:, OFF[n]:OFF[n] + ROWS[n]].reshape(N_SHARD * ROWS[n], D_MODEL)
    w["s5_w_glu"], w["xa_wq"], w["xa_wo"] = square("s5_w_glu"), square("xa_wq"), square("xa_wo")
    w["xa_wkv"] = jnp.concatenate([square("xa_wk"), square("xa_wv")], axis=1)
    w_in_t = square("w_in")
    w["w_in_t"] = w_in_t[:D_MAIN]
    w["w_dt_t"] = jnp.pad(w_in_t[D_MAIN:], ((0, D_DT_PAD - SSD_HEADS), (0, 0)))
    per = SSD_CONV * D_XBC // N_SHARD
    cw = gathered[:, TAIL_OFF:TAIL_OFF + CONV_ROWS].astype(F32).reshape(N_SHARD, -1)[:, :3 * per]
    cw = cw.reshape(N_SHARD, 3, SSD_CONV, D_XBC // N_SHARD)
    cw = (cw[:, 0] + cw[:, 1]) + cw[:, 2]
    w["ssd_conv_w"] = cw.transpose(1, 0, 2).reshape(SSD_CONV, D_XBC)
    return w


def kernel(x, mem, norm_mix, w_in, s5_a_re, s5_a_im, s5_log_dt, s5_b_re, s5_b_im, s5_c_re, s5_c_im, s5_d, s5_w_glu, ssd_conv_w, ssd_conv_b, ssd_dt_bias, ssd_a_log, ssd_d, ssd_norm, w_out, norm_xattn, norm_mem, xa_wq, xa_wk, xa_wv, xa_wo, norm_mlp, mlp_w1, mlp_w2, norm_final, loss_target, m_norm_mix, m_w_in, m_s5_a_re, m_s5_a_im, m_s5_log_dt, m_s5_b_re, m_s5_b_im, m_s5_c_re, m_s5_c_im, m_s5_d, m_s5_w_glu, m_ssd_conv_w, m_ssd_conv_b, m_ssd_dt_bias, m_ssd_a_log, m_ssd_d, m_ssd_norm, m_w_out, m_norm_xattn, m_norm_mem, m_xa_wq, m_xa_wk, m_xa_wv, m_xa_wo, m_norm_mlp, m_mlp_w1, m_mlp_w2, m_norm_final, v_norm_mix, v_w_in, v_s5_a_re, v_s5_a_im, v_s5_log_dt, v_s5_b_re, v_s5_b_im, v_s5_c_re, v_s5_c_im, v_s5_d, v_s5_w_glu, v_ssd_conv_w, v_ssd_conv_b, v_ssd_dt_bias, v_ssd_a_log, v_ssd_d, v_ssd_norm, v_w_out, v_norm_xattn, v_norm_mem, v_xa_wq, v_xa_wk, v_xa_wv, v_xa_wo, v_norm_mlp, v_mlp_w1, v_mlp_w2, v_norm_final):
    names = ("norm_mix", "w_in", "s5_a_re", "s5_a_im", "s5_log_dt", "s5_b_re", "s5_b_im", "s5_c_re", "s5_c_im", "s5_d",
             "s5_w_glu", "ssd_conv_w", "ssd_conv_b", "ssd_dt_bias", "ssd_a_log", "ssd_d", "ssd_norm", "w_out",
             "norm_xattn", "norm_mem", "xa_wq", "xa_wk", "xa_wv", "xa_wo", "norm_mlp", "mlp_w1", "mlp_w2", "norm_final")
    loc = locals()
    wts = {n: loc[n] for n in names}
    mom = {n: loc["m_" + n] for n in names}
    var = {n: loc["v_" + n] for n in names}
    shard = 2 * lax.axis_index("x") + lax.axis_index("y")
    core = lax.axis_index("c")

    cst = _consts()
    place_idx = jnp.stack([shard, core]).astype(jnp.int32)
    h, mem0 = x[0], mem[0]

    own = _own_pack(wts, 0)
    landed = _comm_only(gather_over_ici(own, place_own(own, place_idx[:1])))[0]
    gathered = _comm_only(gather_to_sibling(landed))[0]
    layers, saved = [], []
    for l in range(DEPTH):
        layers.append(_layer_weights(gathered, wts, l))
        nxt = None
        if l + 1 < DEPTH:
            own = _own_pack(wts, l + 1)
            nxt = (own, place_own(own, place_idx[:1]))
        h, sv, gathered = _layer_fwd(h, mem0, layers[l], cst, next_pack=nxt)
        saved.append(sv)
    loss, dx, dgf = loss_head(h, norm_final.reshape(1, -1), loss_target[0], tm=TM, name="loss_head")

    gshards = jnp.zeros((DEPTH, PACK_ROWS, D_MODEL), F32)
    smalls = jnp.zeros((DEPTH, N_SHARD, TAIL_ROWS, D_MODEL), F32)
    pending = None
    for l in reversed(range(DEPTH)):
        dx, gpack, first = _layer_bwd(dx, mem0, layers[l], saved[l], cst, extra_small=dgf[0] if l == DEPTH - 1 else None,
                                      reduce_hooks=pending, early=(l, place_idx) if l == 0 else None)
        if pending is not None:
            gshards, smalls = pending.gshards, pending.smalls
        pending = _Reduction(gpack, l, place_idx, gshards, smalls)
    gshards, smalls = _Reduction(gpack, 0, place_idx, first.gshards, first.smalls, ROWS_LATE).run_alone()

    g = {n: gshards[:, OFF[n]:OFF[n] + ROWS[n]].reshape(wts[n].shape) for n in OFF if n != "w_in"}
    g["w_in"] = gshards[:, OFF["w_in"]:OFF["w_in"] + ROWS["w_in"]].transpose(0, 2, 1)
    small_red = smalls[:, :, :SMALL_Q].reshape(DEPTH, -1)
    off = 0
    for n in SMALL_L:
        shape = (SSD_CONV, D_XBC) if n == "ssd_conv_w" else wts[n].shape[1:]
        size = math.prod(shape)
        g[n] = small_red[:, off:off + size].reshape((DEPTH,) + shape)
        off += size
    g["norm_final"] = small_red[DEPTH - 1, off:off + D_MODEL]
    g["ssd_conv_w"] = lax.dynamic_slice_in_dim(g["ssd_conv_w"], shard * (D_XBC // N_SHARD), D_XBC // N_SHARD, axis=2)

    deltas, new_m, new_v = {}, {}, {}
    for n in names:
        deltas[n], new_m[n], new_v[n] = adamw(wts[n], g[n], mom[n], var[n], name="adamw_" + n)
    loss_all = lax.psum(loss[0, 0], ("x", "y", "c"))
    return (loss_all, dx[None], *[g[n] for n in names], *[deltas[n] for n in names], *[new_m[n] for n in names],
            *[new_v[n] for n in names])
```

```python
import functools
import math

import jax
import jax.numpy as jnp
import numpy as np
from jax import lax
from jax.experimental import pallas as pl
from jax.experimental.pallas import tpu as pltpu

F32 = jnp.float32
BF16 = jnp.bfloat16
HIGHEST = lax.Precision.HIGHEST

D_MODEL = 1024
DEPTH = 4
D_S5 = 1024
D_SSD = 1024
S5_GROUP = 16
S5_GROUPS = 64
S5_STATE = 64
SSD_HEADS = 16
SSD_HEADDIM = 64
SSD_NGROUPS = 4
SSD_STATE = 128
SSD_CONV = 4
SSD_CHUNK = 128
D_XBC = 2048
D_MAIN = 4096
D_DT_PAD = 128
XA_HEADS = 4
XA_HEAD_DIM = 256
D_FF = 4096
EPS = 1e-5
ADAM_LR, ADAM_B1, ADAM_B2, ADAM_EPS, ADAM_WD, ADAM_STEP = 0.001, 0.9, 0.999, 1e-08, 0.01, 10

VMEM_LIMIT = 56 * 1024 * 1024
MESH_T = pl.DeviceIdType.MESH


def _cp(*sem):
    return pltpu.CompilerParams(dimension_semantics=tuple(sem) if sem else None, vmem_limit_bytes=VMEM_LIMIT)


_ANY = pl.BlockSpec(memory_space=pl.ANY)


class Comm:
    def __init__(self, name, inputs, out_shapes, sems, start, wait, aliases=None):
        self.name, self.inputs, self.out_shapes, self.sems = name, list(inputs), list(out_shapes), list(sems)
        self.start, self.wait, self.aliases = start, wait, dict(aliases or {})


def _call(body, *, grid, in_specs, out_specs, out_shape, args, scratch_shapes=(), sem, name, comm=None, aliases=None):
    in_specs, out_specs, out_shape = list(in_specs), list(out_specs), list(out_shape)
    scratch_shapes = list(scratch_shapes)
    aliases = dict(aliases or {})
    if comm is None:
        res = pl.pallas_call(body, grid=grid, in_specs=in_specs, out_specs=out_specs, out_shape=out_shape,
                             scratch_shapes=scratch_shapes, compiler_params=_cp(*sem), name=name,
                             input_output_aliases=aliases)(*args)
        return list(res), []
    n_in, n_out, n_scr = len(in_specs), len(out_specs), len(scratch_shapes)
    c_in, c_out = len(comm.inputs), len(comm.out_shapes)

    def wrapped(*refs):
        a, refs = refs[:n_in], refs[n_in:]
        ci, refs = refs[:c_in], refs[c_in:]
        o, refs = refs[:n_out], refs[n_out:]
        co, refs = refs[:c_out], refs[c_out:]
        s, cs = refs[:n_scr], refs[n_scr:]
        first = functools.reduce(jnp.logical_and, [pl.program_id(d) == 0 for d in range(len(grid))])
        last = functools.reduce(jnp.logical_and, [pl.program_id(d) == grid[d] - 1 for d in range(len(grid))])

        @pl.when(first)
        def _():
            comm.start(ci, co, cs)

        body(*a, *o, *s)

        @pl.when(last)
        def _():
            comm.wait(ci, co, cs)

    for i, j in comm.aliases.items():
        aliases[n_in + i] = n_out + j
    res = pl.pallas_call(wrapped, grid=grid, in_specs=in_specs + [_ANY] * c_in, out_specs=out_specs + [_ANY] * c_out,
                         out_shape=out_shape + comm.out_shapes, scratch_shapes=scratch_shapes + comm.sems,
                         compiler_params=_cp(*(("arbitrary",) * len(grid))), name=name + "_" + comm.name,
                         input_output_aliases=aliases)(*args, *comm.inputs)
    return list(res[:n_out]), list(res[n_out:])


def _combine(a, b):
    if a is None or b is None:
        return a or b
    ni, no, ns = len(a.inputs), len(a.out_shapes), len(a.sems)

    def start(ci, co, cs):
        a.start(ci[:ni], co[:no], cs[:ns])
        b.start(ci[ni:], co[no:], cs[ns:])

    def wait(ci, co, cs):
        a.wait(ci[:ni], co[:no], cs[:ns])
        b.wait(ci[ni:], co[no:], cs[ns:])

    aliases = dict(a.aliases)
    aliases.update({ni + i: no + j for i, j in b.aliases.items()})
    return Comm(a.name + "_" + b.name, a.inputs + b.inputs, a.out_shapes + b.out_shapes, a.sems + b.sems, start, wait,
                aliases)


def _comm_only(comm):
    def body(*refs):
        ci, refs = refs[:len(comm.inputs)], refs[len(comm.inputs):]
        co, cs = refs[:len(comm.out_shapes)], refs[len(comm.out_shapes):]
        comm.start(ci, co, cs)
        comm.wait(ci, co, cs)

    res = pl.pallas_call(body, in_specs=[_ANY] * len(comm.inputs), out_specs=[_ANY] * len(comm.out_shapes),
                         out_shape=comm.out_shapes, scratch_shapes=comm.sems, name=comm.name,
                         input_output_aliases=comm.aliases)(*comm.inputs)
    return list(res)


def _dot(a, b):
    return jnp.dot(a, b, preferred_element_type=F32)


def _dot_nt(a, b):
    return lax.dot_general(a, b, (((1,), (1,)), ((), ())), preferred_element_type=F32)


def _dot_tn(a, b):
    return lax.dot_general(a, b, (((0,), (0,)), ((), ())), preferred_element_type=F32)


def _dot_hi(a, b):
    return jnp.dot(a, b, precision=HIGHEST, preferred_element_type=F32)


def _gelu(x):
    c = math.sqrt(2.0 / math.pi)
    return 0.5 * x * (1.0 + jnp.tanh(c * (x + 0.044715 * x * x * x)))


def _gelu_grad(x):
    c = math.sqrt(2.0 / math.pi)
    t = jnp.tanh(c * (x + 0.044715 * x * x * x))
    return 0.5 * (1.0 + t) + 0.5 * x * (1.0 - t * t) * c * (1.0 + 3 * 0.044715 * x * x)


def _sigmoid(x):
    return 1.0 / (1.0 + jnp.exp(-x))


def _act(a, act):
    if act is None:
        return a.astype(BF16)
    a = a.astype(F32)
    if act == "relu2":
        a = jnp.maximum(a, 0.0)
        return (a * a).astype(BF16)
    if act == "gelu":
        return _gelu(a).astype(BF16)
    raise ValueError(act)


def _pack_block(ref):
    return ref[...].reshape(-1, ref.shape[-1])


def norm_matmul(x, g, w, w2=None, *, tm, tn, name, wspec=None, n_out=None, w_transposed=False):
    T, D = x.shape
    N = n_out if wspec is not None else (w.shape[0] if w_transposed else w.shape[1])
    wget = (lambda r: r[...]) if wspec is None else _pack_block
    mm = _dot_nt if w_transposed else _dot
    has2 = w2 is not None

    def body(x_ref, g_ref, w_ref, *rest):
        if has2:
            w2_ref, o_ref, h_ref, r_ref, o2_ref = rest
        else:
            o_ref, h_ref, r_ref = rest
        j = pl.program_id(1)

        @pl.when(j == 0)
        def _():
            xv = x_ref[...]
            r = lax.rsqrt(jnp.mean(xv * xv, axis=-1, keepdims=True) + EPS)
            h = (xv * r * g_ref[...]).astype(BF16)
            h_ref[...] = h
            r_ref[...] = r
            if has2:
                o2_ref[...] = mm(h, w2_ref[...])

        o_ref[...] = mm(h_ref[...], wget(w_ref)).astype(o_ref.dtype)

    if wspec is None:
        wspec = pl.BlockSpec((tn, D), lambda i, j: (j, 0)) if w_transposed else pl.BlockSpec((D, tn), lambda i, j: (0, j))
    in_specs = [pl.BlockSpec((tm, D), lambda i, j: (i, 0)), pl.BlockSpec((1, D), lambda i, j: (0, 0)), wspec]
    out_shape = [jax.ShapeDtypeStruct((T, N), BF16), jax.ShapeDtypeStruct((T, D), BF16),
                 jax.ShapeDtypeStruct((T, 1), F32)]
    out_specs = [pl.BlockSpec((tm, tn), lambda i, j: (i, j)), pl.BlockSpec((tm, D), lambda i, j: (i, 0)),
                 pl.BlockSpec((tm, 1), lambda i, j: (i, 0))]
    args = [x, g, w]
    if has2:
        in_specs.append(pl.BlockSpec(w2.shape, lambda i, j: (0, 0)))
        out_shape.append(jax.ShapeDtypeStruct((T, D_DT_PAD), F32))
        out_specs.append(pl.BlockSpec((tm, D_DT_PAD), lambda i, j: (i, 0)))
        args.append(w2)
    return pl.pallas_call(body, grid=(T // tm, N // tn), in_specs=in_specs, out_specs=out_specs,
                          out_shape=out_shape, compiler_params=_cp("parallel", "arbitrary"), name=name)(*args)


def matmul_res(a, w, r, *, act=None, tm, tn, tk, name, wspec=None):
    T, K = a.shape
    N = r.shape[1]
    wget = (lambda r_: r_[...]) if wspec is None else _pack_block
    nk = K // tk

    def body(a_ref, w_ref, r_ref, o_ref):
        k = pl.program_id(2)

        @pl.when(k == 0)
        def _():
            o_ref[...] = r_ref[...]

        o_ref[...] += _dot(_act(a_ref[...], act), wget(w_ref))

    return pl.pallas_call(
        body, grid=(T // tm, N // tn, nk),
        in_specs=[pl.BlockSpec((tm, tk), lambda i, j, k: (i, k)),
                  pl.BlockSpec((tk, tn), lambda i, j, k: (k, j)) if wspec is None else wspec,
                  pl.BlockSpec((tm, tn), lambda i, j, k: (i, j))],
        out_specs=pl.BlockSpec((tm, tn), lambda i, j, k: (i, j)),
        out_shape=jax.ShapeDtypeStruct((T, N), F32),
        compiler_params=_cp("parallel", "parallel", "arbitrary"), name=name)(a, w, r)


def glu_fwd(ys, w, *, tm, name, comm=None):
    T, N = ys.shape

    def body(y_ref, w_ref, o_ref, t_ref):
        a = _gelu(y_ref[...].astype(F32))
        t = _dot(a.astype(BF16), w_ref[...])
        o_ref[...] = (a * _sigmoid(t)).astype(BF16)
        t_ref[...] = t.astype(BF16)

    return _call(
        body, grid=(T // tm,),
        in_specs=[pl.BlockSpec((tm, N), lambda i: (i, 0)), pl.BlockSpec((N, N), lambda i: (0, 0))],
        out_specs=[pl.BlockSpec((tm, N), lambda i: (i, 0)), pl.BlockSpec((tm, N), lambda i: (i, 0))],
        out_shape=[jax.ShapeDtypeStruct((T, 2 * N), BF16), jax.ShapeDtypeStruct((T, N), BF16)],
        sem=("parallel",), name=name, args=(ys, w), comm=comm)


def glu_bwd(dymix, ys, t, w, *, tm, name, comm=None):
    T, N = ys.shape

    def body(d_ref, y_ref, t_ref, w_ref, dys_ref):
        d = d_ref[...].astype(F32)
        ysv = y_ref[...].astype(F32)
        s = _sigmoid(t_ref[...].astype(F32))
        dt = (d * _gelu(ysv) * s * (1.0 - s)).astype(BF16)
        dys_ref[...] = ((d * s + _dot_nt(dt, w_ref[...])) * _gelu_grad(ysv)).astype(BF16)

    spec = pl.BlockSpec((tm, N), lambda i: (i, 0))
    return _call(body, grid=(T // tm,), in_specs=[spec, spec, spec, pl.BlockSpec((N, N), lambda i: (0, 0))],
                 out_specs=[spec], out_shape=[jax.ShapeDtypeStruct((T, N), BF16)], sem=("parallel",), name=name,
                 args=(dymix, ys, t, w), comm=comm)


def glu_dw(dymix, ys, t, pack, *, pack_spec, tt, name):
    T, N = ys.shape

    def body(d_ref, y_ref, t_ref, pack_ref, o_ref):
        i = pl.program_id(0)
        a = _gelu(y_ref[...].astype(F32))
        s = _sigmoid(t_ref[...].astype(F32))
        dt = (d_ref[...].astype(F32) * a * s * (1.0 - s)).astype(BF16)
        part = _dot_tn(a.astype(BF16), dt).reshape(o_ref.shape)

        @pl.when(i == 0)
        def _():
            o_ref[...] = part

        @pl.when(i > 0)
        def _():
            o_ref[...] += part

    spec = pl.BlockSpec((tt, N), lambda i: (i, 0))
    return pl.pallas_call(body, grid=(T // tt,), in_specs=[spec, spec, spec, _ANY], out_specs=pack_spec,
                          out_shape=jax.ShapeDtypeStruct(pack.shape, F32), input_output_aliases={3: 0},
                          compiler_params=_cp("arbitrary"), name=name)(dymix, ys, t, pack)


def matmul_nt(g, w, *, epi=None, epi_args=(), g2=None, w2=None, tm, tko, tn, out_dtype=BF16, name, wspec=None,
              k_out=None, comm=None, w_is_nk=False):
    T, N = g.shape
    K = k_out if wspec is not None else (w.shape[1] if w_is_nk else w.shape[0])
    wget = (lambda r_: r_[...]) if wspec is None else _pack_block
    mm = _dot if w_is_nk else _dot_nt
    nn = N // tn
    has2 = g2 is not None
    rms = epi == "rmsbwd"
    if rms:
        assert tko == K
    n_epi = len(epi_args)

    def body(*refs):
        g_ref, w_ref = refs[0], refs[1]
        pos = 2
        if has2:
            g2_ref, w2_ref = refs[2], refs[3]
            pos = 4
        e_refs = refs[pos:pos + n_epi]
        pos += n_epi
        o_ref = refs[pos]
        pos += 1
        if rms:
            dg_ref = refs[pos]
            pos += 1
        acc_ref = refs[pos]
        i = pl.program_id(0)
        n = pl.program_id(2)
        part = mm(g_ref[...].astype(BF16), wget(w_ref))

        @pl.when(n == 0)
        def _():
            acc_ref[...] = part

        @pl.when(n > 0)
        def _():
            acc_ref[...] += part

        @pl.when(n == nn - 1)
        def _():
            acc = acc_ref[...]
            if has2:
                acc = acc + mm(g2_ref[...].astype(BF16), w2_ref[...])
            if epi is None:
                o_ref[...] = acc.astype(o_ref.dtype)
            elif epi == "relu2bwd":
                h1 = e_refs[0][...].astype(F32)
                o_ref[...] = (acc * 2.0 * jnp.maximum(h1, 0.0)).astype(o_ref.dtype)
            elif epi == "glubwd":
                da1 = e_refs[0][...].astype(F32)
                ys = e_refs[1][...].astype(F32)
                o_ref[...] = ((da1 + acc) * _gelu_grad(ys)).astype(o_ref.dtype)
            elif epi == "rmsbwd":
                xv, rs, gain, rv = e_refs[0][...], e_refs[1][...], e_refs[2][...], e_refs[3][...]
                xhat = xv * rs
                gd = acc * gain
                o_ref[...] = rv + rs * (gd - xhat * jnp.mean(gd * xhat, axis=-1, keepdims=True))
                part_g = jnp.sum(acc * xhat, axis=0, keepdims=True)

                @pl.when(i == 0)
                def _():
                    dg_ref[...] = part_g

                @pl.when(i > 0)
                def _():
                    dg_ref[...] += part_g

    if wspec is None:
        wspec = (pl.BlockSpec((tn, tko), lambda i, k, n: (n, k)) if w_is_nk
                 else pl.BlockSpec((tko, tn), lambda i, k, n: (k, n)))
    in_specs = [pl.BlockSpec((tm, tn), lambda i, k, n: (i, n)), wspec]
    args = [g, w]
    if has2:
        n2 = g2.shape[1]
        in_specs += [pl.BlockSpec((tm, n2), lambda i, k, n: (i, 0)),
                     pl.BlockSpec((n2, tko), lambda i, k, n: (0, k)) if w_is_nk
                     else pl.BlockSpec((tko, n2), lambda i, k, n: (k, 0))]
        args += [g2, w2]
    if epi == "relu2bwd" or epi == "glubwd":
        in_specs += [pl.BlockSpec((tm, tko), lambda i, k, n: (i, k))] * n_epi
    elif rms:
        in_specs += [pl.BlockSpec((tm, K), lambda i, k, n: (i, 0)), pl.BlockSpec((tm, 1), lambda i, k, n: (i, 0)),
                     pl.BlockSpec((1, K), lambda i, k, n: (0, 0)), pl.BlockSpec((tm, K), lambda i, k, n: (i, 0))]
    args += list(epi_args)
    out_shape = [jax.ShapeDtypeStruct((T, K), F32 if rms else out_dtype)]
    out_specs = [pl.BlockSpec((tm, tko), lambda i, k, n: (i, k))]
    if rms:
        out_shape.append(jax.ShapeDtypeStruct((1, K), F32))
        out_specs.append(pl.BlockSpec((1, K), lambda i, k, n: (0, 0)))
    sem = ("arbitrary",) * 3 if rms else ("parallel", "parallel", "arbitrary")
    res, cres = _call(body, grid=(T // tm, K // tko, nn), in_specs=in_specs, out_specs=out_specs, out_shape=out_shape,
                      scratch_shapes=[pltpu.VMEM((tm, tko), F32)], sem=sem, name=name, args=args, comm=comm)
    res = res if rms else res[0]
    return res if comm is None else (res, cres)


def matmul_tn(a, g, *, act=None, tk, tn, tt, name, pack=None, pack_spec=None, pack_shape=None, comm=None):
    T, K = a.shape
    N = g.shape[1]
    to_pack = pack_spec is not None

    def body(a_ref, g_ref, *rest):
        o_ref = rest[-1]
        t = pl.program_id(2)
        part = _dot_tn(_act(a_ref[...], act), g_ref[...].astype(BF16))
        part = part.reshape(o_ref.shape)

        @pl.when(t == 0)
        def _():
            o_ref[...] = part

        @pl.when(t > 0)
        def _():
            o_ref[...] += part

    in_specs = [pl.BlockSpec((tt, tk), lambda k, n, t: (t, k)), pl.BlockSpec((tt, tn), lambda k, n, t: (t, n))]
    args = [a, g]
    aliases = {}
    if pack is not None:
        in_specs.append(_ANY)
        args.append(pack)
        aliases = {2: 0}
    res, cres = _call(
        body, grid=(K // tk, N // tn, T // tt), in_specs=in_specs,
        out_specs=[pack_spec if to_pack else pl.BlockSpec((tk, tn), lambda k, n, t: (k, n))],
        out_shape=[jax.ShapeDtypeStruct(pack_shape if to_pack else (K, N), F32)], aliases=aliases,
        sem=("parallel", "parallel", "arbitrary"), name=name, args=args, comm=comm)
    return res[0] if comm is None else (res[0], cres)


def attn_fwd(q, kv, *, tm, name):
    T = q.shape[0]
    M = kv.shape[0]
    scale = XA_HEAD_DIM ** -0.5

    def body(q_ref, kv_ref, o_ref):
        for h in range(XA_HEADS):
            sl = slice(h * XA_HEAD_DIM, (h + 1) * XA_HEAD_DIM)
            kh = kv_ref[:, h * XA_HEAD_DIM:(h + 1) * XA_HEAD_DIM]
            vh = kv_ref[:, D_MODEL + h * XA_HEAD_DIM:D_MODEL + (h + 1) * XA_HEAD_DIM]
            s = _dot_nt(q_ref[:, sl], kh) * scale
            s = s - jnp.max(s, axis=-1, keepdims=True)
            p = jnp.exp(s)
            p = p / jnp.sum(p, axis=-1, keepdims=True)
            o_ref[:, sl] = _dot(p.astype(BF16), vh).astype(BF16)

    return pl.pallas_call(
        body, grid=(T // tm,),
        in_specs=[pl.BlockSpec((tm, D_MODEL), lambda i: (i, 0)), pl.BlockSpec((M, 2 * D_MODEL), lambda i: (0, 0))],
        out_specs=pl.BlockSpec((tm, D_MODEL), lambda i: (i, 0)),
        out_shape=jax.ShapeDtypeStruct((T, D_MODEL), BF16),
        compiler_params=_cp("parallel"), name=name)(q, kv)


def attn_bwd(q, kv, do, *, tm, name):
    T = q.shape[0]
    M = kv.shape[0]
    scale = XA_HEAD_DIM ** -0.5

    def body(q_ref, kv_ref, do_ref, dq_ref, dkv_ref):
        i = pl.program_id(0)

        @pl.when(i == 0)
        def _():
            dkv_ref[...] = jnp.zeros_like(dkv_ref)

        for h in range(XA_HEADS):
            sl = slice(h * XA_HEAD_DIM, (h + 1) * XA_HEAD_DIM)
            slv = slice(D_MODEL + h * XA_HEAD_DIM, D_MODEL + (h + 1) * XA_HEAD_DIM)
            qh = q_ref[:, sl]
            kh = kv_ref[:, sl]
            vh = kv_ref[:, slv]
            doh = do_ref[:, sl]
            s = _dot_nt(qh, kh) * scale
            s = s - jnp.max(s, axis=-1, keepdims=True)
            p = jnp.exp(s)
            p = p / jnp.sum(p, axis=-1, keepdims=True)
            pb = p.astype(BF16)
            dkv_ref[:, slv] += _dot_tn(pb, doh)
            dp = _dot_nt(doh, vh)
            ds = (p * (dp - jnp.sum(dp * p, axis=-1, keepdims=True)) * scale).astype(BF16)
            dq_ref[:, sl] = _dot(ds, kh).astype(BF16)
            dkv_ref[:, sl] += _dot_tn(ds, qh)

    spec = pl.BlockSpec((tm, D_MODEL), lambda i: (i, 0))
    kvspec = pl.BlockSpec((M, 2 * D_MODEL), lambda i: (0, 0))
    return pl.pallas_call(
        body, grid=(T // tm,), in_specs=[spec, kvspec, spec], out_specs=[spec, kvspec],
        out_shape=[jax.ShapeDtypeStruct((T, D_MODEL), BF16), jax.ShapeDtypeStruct((M, 2 * D_MODEL), F32)],
        compiler_params=_cp("arbitrary"), name=name)(q, kv, do)


def loss_head(x, g, target, *, tm, name):
    T, D = x.shape

    def body(x_ref, g_ref, t_ref, l_ref, dx_ref, dg_ref):
        i = pl.program_id(0)
        xv = x_ref[...]
        gain = g_ref[...]
        r = lax.rsqrt(jnp.mean(xv * xv, axis=-1, keepdims=True) + EPS)
        xhat = xv * r
        err = xhat * gain - t_ref[...]
        part_l = jnp.full((1, 128), 0.5 / D, F32) * jnp.sum(err * err)
        dy = err * (1.0 / D)
        gd = dy * gain
        dx_ref[...] = r * (gd - xhat * jnp.mean(gd * xhat, axis=-1, keepdims=True))
        part_g = jnp.sum(dy * xhat, axis=0, keepdims=True)

        @pl.when(i == 0)
        def _():
            l_ref[...] = part_l
            dg_ref[...] = part_g

        @pl.when(i > 0)
        def _():
            l_ref[...] += part_l
            dg_ref[...] += part_g

    spec = pl.BlockSpec((tm, D), lambda i: (i, 0))
    return pl.pallas_call(
        body, grid=(T // tm,), in_specs=[spec, pl.BlockSpec((1, D), lambda i: (0, 0)), spec],
        out_specs=[pl.BlockSpec((1, 128), lambda i: (0, 0)), spec, pl.BlockSpec((1, D), lambda i: (0, 0))],
        out_shape=[jax.ShapeDtypeStruct((1, 128), F32), jax.ShapeDtypeStruct((T, D), F32),
                   jax.ShapeDtypeStruct((1, D), F32)],
        compiler_params=_cp("arbitrary"), name=name)(x, g, target)


S5_LS = 128
S5_SEG = 8
S5_TB = S5_LS * S5_SEG
S5_CH = 8
S5_CW = 128
S5_NST = 512


def _cmul(ar, ai, br, bi):
    return ar * br - ai * bi, ar * bi + ai * br


def s5_params_fwd(a_re, a_im, log_dt, b_re, b_im, expand, *, name):
    G, P = a_re.shape

    def body(ar_ref, ai_ref, ld_ref, br_ref, bi_ref, e_ref, abr_ref, abi_ref, apr_ref, api_ref, bbr_ref, bbi_ref):
        ar, ai = ar_ref[...], ai_ref[...]
        dt = jnp.exp(ld_ref[...])
        mag = jnp.exp(dt * ar)
        abr, abi = mag * jnp.cos(dt * ai), mag * jnp.sin(dt * ai)
        den = ar * ar + ai * ai
        zr, zi = abr - 1.0, abi
        fr = (zr * ar + zi * ai) / den
        fi = (zi * ar - zr * ai) / den
        frx, fix = _dot_hi(fr, e_ref[...]), _dot_hi(fi, e_ref[...])
        br, bi = br_ref[...], bi_ref[...]
        bbr_ref[...] = frx * br - fix * bi
        bbi_ref[...] = frx * bi + fix * br
        abr_ref[...] = abr
        abi_ref[...] = abi
        pr, pi = abr, abi
        for _ in range(int(math.log2(S5_LS))):
            pr, pi = _cmul(pr, pi, pr, pi)
        apr_ref[...] = pr
        api_ref[...] = pi

    small = jax.ShapeDtypeStruct((G, P), F32)
    big = jax.ShapeDtypeStruct(b_re.shape, F32)
    return pl.pallas_call(body, out_shape=[small, small, small, small, big, big], name=name)(
        a_re, a_im, log_dt, b_re, b_im, expand)


def s5_params_bwd(a_re, a_im, log_dt, b_re, b_im, g_abr, g_abi, g_bbr, g_bbi, expand, expand_t, *, name):
    G, P = a_re.shape

    def body(ar_ref, ai_ref, ld_ref, br_ref, bi_ref, gar_ref, gai_ref, gbr_ref, gbi_ref, e_ref, et_ref,
             dar_ref, dai_ref, dld_ref, dbr_ref, dbi_ref):
        ar, ai = ar_ref[...], ai_ref[...]
        dt = jnp.exp(ld_ref[...])
        mag = jnp.exp(dt * ar)
        cs, sn = jnp.cos(dt * ai), jnp.sin(dt * ai)
        abr, abi = mag * cs, mag * sn
        den = ar * ar + ai * ai
        zr, zi = abr - 1.0, abi
        fr = (zr * ar + zi * ai) / den
        fi = (zi * ar - zr * ai) / den
        frx, fix = _dot_hi(fr, e_ref[...]), _dot_hi(fi, e_ref[...])
        br, bi = br_ref[...], bi_ref[...]
        gbr, gbi = gbr_ref[...], gbi_ref[...]
        dbr_ref[...] = frx * gbr + fix * gbi
        dbi_ref[...] = -fix * gbr + frx * gbi
        gfr = _dot_hi(br * gbr + bi * gbi, et_ref[...])
        gfi = _dot_hi(-bi * gbr + br * gbi, et_ref[...])
        g_zr = (gfr * ar - gfi * ai) / den
        g_zi = (gfr * ai + gfi * ar) / den
        g_ar = gfr * (zr - fr * 2.0 * ar) / den + gfi * (zi - fi * 2.0 * ar) / den
        g_ai = gfr * (zi - fr * 2.0 * ai) / den + gfi * (-zr - fi * 2.0 * ai) / den
        t_abr = gar_ref[...] + g_zr
        t_abi = gai_ref[...] + g_zi
        g_mag = t_abr * cs + t_abi * sn
        g_th = mag * (-t_abr * sn + t_abi * cs)
        dar_ref[...] = g_ar + g_mag * mag * dt
        dai_ref[...] = g_ai + g_th * dt
        g_dt = jnp.sum(g_mag * mag * ar + g_th * ai, axis=-1, keepdims=True)
        dld_ref[...] = g_dt * dt

    small = jax.ShapeDtypeStruct((G, P), F32)
    big = jax.ShapeDtypeStruct(b_re.shape, F32)
    return pl.pallas_call(body, out_shape=[small, small, jax.ShapeDtypeStruct((G, 1), F32), big, big], name=name)(
        a_re, a_im, log_dt, b_re, b_im, g_abr, g_abi, g_bbr, g_bbi, expand, expand_t)


def _s5_permute_in(src_ref, dst_ref):
    for i in range(S5_LS):
        dst_ref[pl.ds(8 * i, 8), :] = src_ref[pl.ds(i, 8, stride=S5_LS), :]


def _s5_permute_out(src_ref, dst_ref):
    for r in range(S5_SEG):
        for k in range(S5_LS // 8):
            dst_ref[pl.ds(r * S5_LS + 8 * k, 8), :] = src_ref[pl.ds(64 * k + r, 8, stride=8), :]


def _s5_scan(a_r, a_i, dr_ref, di_ref, init_r, init_i, store=None, reverse=False, conj=False):
    sgn = -1.0 if conj else 1.0

    def step(n, c):
        sr, si = c
        i = S5_LS - 1 - n if reverse else n
        nr = a_r * sr - sgn * a_i * si + dr_ref[i]
        ni = a_r * si + sgn * a_i * sr + di_ref[i]
        if store is not None:
            store(i, nr, ni, sr, si)
        return nr, ni

    return lax.fori_loop(0, S5_LS, step, (init_r, init_i), unroll=2)


def _s5_stitch(apr, api, fin_r, fin_i, car_r, car_i, reverse=False, conj=False):
    sgn = -1.0 if conj else 1.0
    rows_r, rows_i = [None] * S5_SEG, [None] * S5_SEG
    order = range(S5_SEG - 1, -1, -1) if reverse else range(S5_SEG)
    for r in order:
        rows_r[r], rows_i[r] = car_r, car_i
        fr, fi = fin_r[r:r + 1], fin_i[r:r + 1]
        car_r, car_i = (apr * car_r - sgn * api * car_i + fr, apr * car_i + sgn * api * car_r + fi)
    return jnp.concatenate(rows_r, 0), jnp.concatenate(rows_i, 0), car_r, car_i


def _s5_specs(nb, rev):
    blk = (lambda c, b: (nb - 1 - b, c)) if rev else (lambda c, b: (b, c))
    tok = pl.BlockSpec((S5_TB, S5_CW), blk)
    par_b = pl.BlockSpec((1, S5_CW, S5_NST), lambda c, b: (c, 0, 0))
    par_c = pl.BlockSpec((1, S5_NST, S5_CW), lambda c, b: (c, 0, 0))
    vec_s = pl.BlockSpec((1, 1, S5_NST), lambda c, b: (c, 0, 0))
    vec_c = pl.BlockSpec((1, 1, S5_CW), lambda c, b: (c, 0, 0))
    return tok, par_b, par_c, vec_s, vec_c


def s5_fwd(proj, bbr, bbi, ccr, cci, abr, abi, apr, api, dskip, *, name, comm=None):
    T = proj.shape[0]
    nb = T // S5_TB
    zeros8 = functools.partial(jnp.zeros, (S5_SEG, S5_NST), F32)

    def body(u_ref, bbr_ref, bbi_ref, ccr_ref, cci_ref, ar_ref, ai_ref, apr_ref, api_ref, d_ref,
             y_ref, csr_ref, csi_ref, uf_ref, up_ref, dr_ref, di_ref, sr_ref, si_ref, yp_ref, car_ref, cai_ref):
        b = pl.program_id(1)

        @pl.when(b == 0)
        def _():
            car_ref[...] = jnp.zeros_like(car_ref)
            cai_ref[...] = jnp.zeros_like(cai_ref)

        csr_ref[0, 0] = car_ref[...]
        csi_ref[0, 0] = cai_ref[...]
        uf_ref[...] = u_ref[...].astype(F32)
        _s5_permute_in(uf_ref, up_ref)
        upb = up_ref[...].astype(BF16)
        dr_ref[...] = _dot(upb, bbr_ref[0]).reshape(S5_LS, S5_SEG, S5_NST)
        di_ref[...] = _dot(upb, bbi_ref[0]).reshape(S5_LS, S5_SEG, S5_NST)
        a_r = jnp.broadcast_to(ar_ref[0], (S5_SEG, S5_NST))
        a_i = jnp.broadcast_to(ai_ref[0], (S5_SEG, S5_NST))
        fin_r, fin_i = _s5_scan(a_r, a_i, dr_ref, di_ref, zeros8(), zeros8())
        cin_r, cin_i, ncr, nci = _s5_stitch(apr_ref[0], api_ref[0], fin_r, fin_i, car_ref[...], cai_ref[...])
        car_ref[...] = ncr
        cai_ref[...] = nci

        def store(i, nr, ni, sr, si):
            sr_ref[i] = nr
            si_ref[i] = ni

        _s5_scan(a_r, a_i, dr_ref, di_ref, cin_r, cin_i, store=store)
        s_r = sr_ref[...].reshape(S5_TB, S5_NST).astype(BF16)
        s_i = si_ref[...].reshape(S5_TB, S5_NST).astype(BF16)
        yp_ref[...] = _dot(s_r, ccr_ref[0]) - _dot(s_i, cci_ref[0]) + d_ref[0] * up_ref[...]
        _s5_permute_out(yp_ref, uf_ref)
        y_ref[...] = uf_ref[...].astype(BF16)

    tok, par_b, par_c, vec_s, vec_c = _s5_specs(nb, False)
    cs_spec = pl.BlockSpec((1, 1, 1, S5_NST), lambda c, b: (b, c, 0, 0))
    cs_shape = jax.ShapeDtypeStruct((nb, S5_CH, 1, S5_NST), F32)
    tokbuf = pltpu.VMEM((S5_TB, S5_CW), F32)
    stbuf = pltpu.VMEM((S5_LS, S5_SEG, S5_NST), F32)
    return _call(
        body, grid=(S5_CH, nb),
        in_specs=[tok, par_b, par_b, par_c, par_c, vec_s, vec_s, vec_s, vec_s, vec_c],
        out_specs=[tok, cs_spec, cs_spec],
        out_shape=[jax.ShapeDtypeStruct((T, D_S5), BF16), cs_shape, cs_shape],
        scratch_shapes=[tokbuf, tokbuf, stbuf, stbuf, stbuf, stbuf, tokbuf,
                        pltpu.VMEM((1, S5_NST), F32), pltpu.VMEM((1, S5_NST), F32)],
        sem=("parallel", "arbitrary"), name=name, comm=comm,
        args=(proj, bbr, bbi, ccr, cci, abr, abi, apr, api, dskip))


def s5_bwd(proj, dys, csr, csi, bbr, bbi, ccr, cci, abr, abi, apr, api, dskip, *, name, comm=None, rider=None):
    T = proj.shape[0]
    nb = T // S5_TB
    zeros8 = functools.partial(jnp.zeros, (S5_SEG, S5_NST), F32)
    n_in, n_out, n_scr = 13, 8, 11
    r_in, r_out = (len(rider["in_specs"]), len(rider["out_specs"])) if rider else (0, 0)

    def body(*refs):
        ins, refs = refs[:n_in], refs[n_in:]
        r_ins, refs = refs[:r_in], refs[r_in:]
        outs, refs = refs[:n_out], refs[n_out:]
        r_outs, refs = refs[:r_out], refs[r_out:]
        scr, r_scr = refs[:n_scr], refs[n_scr:]
        (u_ref, gy_ref, csr_ref, csi_ref, bbr_ref, bbi_ref, ccr_ref, cci_ref, ar_ref, ai_ref, apr_ref, api_ref,
         d_ref) = ins
        du_ref, dbr_ref, dbi_ref, dcr_ref, dci_ref, dd_ref, dar_ref, dai_ref = outs
        tmp_ref, up_ref, gyp_ref, dr_ref, di_ref, sr_ref, si_ref, gr_ref, gi_ref, car_ref, cai_ref = scr
        b = pl.program_id(1)
        if rider:
            rider["init"](r_ins, r_outs, r_scr)

        @pl.when(b == 0)
        def _():
            car_ref[...] = jnp.zeros_like(car_ref)
            cai_ref[...] = jnp.zeros_like(cai_ref)
            for ref in (dbr_ref, dbi_ref, dcr_ref, dci_ref, dd_ref, dar_ref, dai_ref):
                ref[...] = jnp.zeros_like(ref)

        tmp_ref[...] = u_ref[...].astype(F32)
        _s5_permute_in(tmp_ref, up_ref)
        tmp_ref[...] = gy_ref[...].astype(F32)
        _s5_permute_in(tmp_ref, gyp_ref)
        upb = up_ref[...].astype(BF16)
        gyp = gyp_ref[...]
        gypb = gyp.astype(BF16)
        dr_ref[...] = _dot(upb, bbr_ref[0]).reshape(S5_LS, S5_SEG, S5_NST)
        di_ref[...] = _dot(upb, bbi_ref[0]).reshape(S5_LS, S5_SEG, S5_NST)
        a_r = jnp.broadcast_to(ar_ref[0], (S5_SEG, S5_NST))
        a_i = jnp.broadcast_to(ai_ref[0], (S5_SEG, S5_NST))
        fin_r, fin_i = _s5_scan(a_r, a_i, dr_ref, di_ref, zeros8(), zeros8())
        cin_r, cin_i, _, _ = _s5_stitch(apr_ref[0], api_ref[0], fin_r, fin_i, csr_ref[0, 0], csi_ref[0, 0])
        sr_ref[0] = cin_r
        si_ref[0] = cin_i

        def store_s(i, nr, ni, sr, si):
            sr_ref[i + 1] = nr
            si_ref[i + 1] = ni

        _s5_scan(a_r, a_i, dr_ref, di_ref, cin_r, cin_i, store=store_s)
        dr_ref[...] = _dot_nt(gypb, ccr_ref[0]).reshape(S5_LS, S5_SEG, S5_NST)
        di_ref[...] = (-_dot_nt(gypb, cci_ref[0])).reshape(S5_LS, S5_SEG, S5_NST)
        fin_r, fin_i = _s5_scan(a_r, a_i, dr_ref, di_ref, zeros8(), zeros8(), reverse=True, conj=True)
        gin_r, gin_i, ncr, nci = _s5_stitch(apr_ref[0], api_ref[0], fin_r, fin_i, car_ref[...], cai_ref[...],
                                            reverse=True, conj=True)
        car_ref[...] = ncr
        cai_ref[...] = nci
        def step_g(n, carry):
            gr, gi, acc_r, acc_i = carry
            i = S5_LS - 1 - n
            nr = a_r * gr + a_i * gi + dr_ref[i]
            ni = a_r * gi - a_i * gr + di_ref[i]
            gr_ref[i] = nr
            gi_ref[i] = ni
            pr, pi = sr_ref[i], si_ref[i]
            return nr, ni, acc_r + (nr * pr + ni * pi), acc_i + (ni * pr - nr * pi)

        _, _, acc_r, acc_i = lax.fori_loop(0, S5_LS, step_g, (gin_r, gin_i, zeros8(), zeros8()), unroll=2)
        dar_ref[0] += jnp.sum(acc_r, axis=0, keepdims=True)
        dai_ref[0] += jnp.sum(acc_i, axis=0, keepdims=True)
        upb = up_ref[...].astype(BF16)
        gyp = gyp_ref[...]
        gypb = gyp.astype(BF16)
        s_r = sr_ref[pl.ds(1, S5_LS)].reshape(S5_TB, S5_NST).astype(BF16)
        s_i = si_ref[pl.ds(1, S5_LS)].reshape(S5_TB, S5_NST).astype(BF16)
        dcr_ref[0] += _dot_tn(s_r, gypb)
        dci_ref[0] -= _dot_tn(s_i, gypb)
        dd_ref[0] += jnp.sum(gyp * up_ref[...], axis=0, keepdims=True)
        g_r = gr_ref[...].reshape(S5_TB, S5_NST).astype(BF16)
        g_i = gi_ref[...].reshape(S5_TB, S5_NST).astype(BF16)
        dbr_ref[0] += _dot_tn(upb, g_r)
        dbi_ref[0] += _dot_tn(upb, g_i)
        gyp_ref[...] = _dot_nt(g_r, bbr_ref[0]) + _dot_nt(g_i, bbi_ref[0]) + d_ref[0] * gyp
        _s5_permute_out(gyp_ref, tmp_ref)
        du_ref[...] = tmp_ref[...].astype(BF16)
        if rider:
            rider["body"](r_ins, r_outs, r_scr)

    tok, par_b, par_c, vec_s, vec_c = _s5_specs(nb, True)
    cs_spec = pl.BlockSpec((1, 1, 1, S5_NST), lambda c, b: (nb - 1 - b, c, 0, 0))
    tokbuf = pltpu.VMEM((S5_TB, S5_CW), F32)
    stbuf = pltpu.VMEM((S5_LS, S5_SEG, S5_NST), F32)
    stbuf1 = pltpu.VMEM((S5_LS + 1, S5_SEG, S5_NST), F32)
    rd = rider or dict(in_specs=[], out_specs=[], out_shape=[], scratch=[], args=[])
    return _call(
        body, grid=(S5_CH, nb), comm=comm,
        in_specs=[tok, tok, cs_spec, cs_spec, par_b, par_b, par_c, par_c, vec_s, vec_s, vec_s, vec_s, vec_c]
                 + rd["in_specs"],
        out_specs=[tok, par_b, par_b, par_c, par_c, vec_c, vec_s, vec_s] + rd["out_specs"],
        out_shape=[jax.ShapeDtypeStruct((T, D_S5), BF16),
                   jax.ShapeDtypeStruct((S5_CH, S5_CW, S5_NST), F32), jax.ShapeDtypeStruct((S5_CH, S5_CW, S5_NST), F32),
                   jax.ShapeDtypeStruct((S5_CH, S5_NST, S5_CW), F32), jax.ShapeDtypeStruct((S5_CH, S5_NST, S5_CW), F32),
                   jax.ShapeDtypeStruct((S5_CH, 1, S5_CW), F32),
                   jax.ShapeDtypeStruct((S5_CH, 1, S5_NST), F32), jax.ShapeDtypeStruct((S5_CH, 1, S5_NST), F32)]
                  + rd["out_shape"],
        scratch_shapes=[tokbuf, tokbuf, tokbuf, stbuf, stbuf, stbuf1, stbuf1, stbuf, stbuf,
                        pltpu.VMEM((1, S5_NST), F32), pltpu.VMEM((1, S5_NST), F32)] + rd["scratch"],
        sem=("arbitrary", "arbitrary") if rider else ("parallel", "arbitrary"), name=name,
        args=[proj, dys, csr, csi, bbr, bbi, ccr, cci, abr, abi, apr, api, dskip] + rd["args"])


SSD_L = SSD_CHUNK
SSD_GW = 256
NEG = -1e30


def _expand16(v):
    lane = lax.broadcasted_iota(jnp.int32, (v.shape[0], 128), 1)
    parts = [jnp.where(lane < SSD_HEADDIM, v[:, 2 * j:2 * j + 1], v[:, 2 * j + 1:2 * j + 2]) for j in range(8)]
    return jnp.concatenate(parts, axis=1)


def _headsum(v, hsum):
    hi = v.astype(BF16)
    lo = (v - hi.astype(F32)).astype(BF16)
    return _dot(hi, hsum) + _dot(lo, hsum)


def _softplus(x):
    return jnp.maximum(x, 0.0) + jnp.log(1.0 + jnp.exp(-jnp.abs(x)))


def _ssd_chunk_fwd(z, xbc, tail, dtraw, hprev, cw, cb, dtb, alog, dsk, nw, tril):
    L = SSD_L
    f = {}
    xe = jnp.concatenate([tail, xbc], axis=0)
    sh = [xbc] + [pltpu.roll(xe, s, 0)[8:] for s in (1, 2, 3)]
    conv = cb + cw[3:4] * sh[0] + cw[2:3] * sh[1] + cw[1:2] * sh[2] + cw[0:1] * sh[3]
    sig = _sigmoid(conv)
    xa = conv * sig
    xs, bm, cm = xa[:, :D_SSD], xa[:, D_SSD:D_SSD + 512], xa[:, D_SSD + 512:]
    pre = dtraw + dtb
    dt = _softplus(pre)
    a_h = -jnp.exp(alog)
    acum = _dot_hi(tril, dt * a_h)
    acum_t = acum.T
    alast = acum[L - 1:L]
    exp_a = jnp.exp(acum)
    dec = jnp.exp(alast - acum)
    exp_al = jnp.exp(alast)
    dt_x, dec_x, exp_a_x, exp_al_x = _expand16(dt), _expand16(dec), _expand16(exp_a), _expand16(exp_al)
    d_x = _expand16(dsk)
    xh = xs * dt_x
    xhb = xh.astype(BF16)
    xd = (xh * dec_x).astype(BF16)
    row = lax.broadcasted_iota(jnp.int32, (L, L), 0)
    col = lax.broadcasted_iota(jnp.int32, (L, L), 1)
    causal = row >= col
    lane = lax.broadcasted_iota(jnp.int32, (L, 128), 1)
    low = lane < SSD_HEADDIM
    hb = hprev.astype(BF16)
    y_pairs, yoff_parts, st_parts, cbs, lms = [], [], [], [], []
    for g in range(SSD_NGROUPS):
        bg = bm[:, g * 128:(g + 1) * 128].astype(BF16)
        cg = cm[:, g * 128:(g + 1) * 128].astype(BF16)
        cbg = _dot_nt(cg, bg)
        cbs.append(cbg)
        for j in (2 * g, 2 * g + 1):
            xp = xhb[:, j * 128:(j + 1) * 128]
            ys = []
            for h in (2 * j, 2 * j + 1):
                lm = jnp.exp(jnp.where(causal, acum[:, h:h + 1] - acum_t[h:h + 1, :], NEG))
                lms.append(lm)
                ys.append(_dot((cbg * lm).astype(BF16), xp))
            y_pairs.append(jnp.where(low, ys[0], ys[1]))
        gs = slice(g * SSD_GW, (g + 1) * SSD_GW)
        yoff_parts.append(_dot(cg, hb[:, gs]) * exp_a_x[:, gs])
        st_parts.append(_dot_tn(bg, xd[:, gs]))
    yoff = jnp.concatenate(yoff_parts, axis=1)
    y = jnp.concatenate(y_pairs, axis=1) + yoff + d_x * xs
    hnew = exp_al_x * hprev + jnp.concatenate(st_parts, axis=1)
    sz = _sigmoid(z)
    gz = y * (z * sz)
    r = lax.rsqrt(jnp.mean(gz * gz, axis=-1, keepdims=True) + EPS)
    out = gz * r * nw
    f.update(sh=sh, conv=conv, sig=sig, xs=xs, bm=bm, cm=cm, pre=pre, dt=dt, a_h=a_h, exp_a_x=exp_a_x, dec_x=dec_x,
             exp_al=exp_al, exp_al_x=exp_al_x, dt_x=dt_x, d_x=d_x, xh=xh, xhb=xhb, xd=xd, causal=causal, low=low, hb=hb,
             cbs=cbs, lms=lms, yoff=yoff, y=y, sz=sz, gz=gz, r=r)
    return out, hnew, f


def _ssd_params(conv_w, conv_b, dt_bias, a_log, d_skip, norm_w):
    pad16 = lambda v: jnp.pad(v.reshape(1, SSD_HEADS), ((0, 0), (0, 128 - SSD_HEADS)))
    return (jnp.pad(conv_w, ((0, 8 - SSD_CONV), (0, 0))), conv_b.reshape(1, D_XBC), pad16(dt_bias), pad16(a_log),
            pad16(d_skip), norm_w.reshape(1, D_SSD))


def _ssd_param_specs():
    full = lambda shape: pl.BlockSpec(shape, lambda i: (0, 0))
    return [full((8, D_XBC)), full((1, D_XBC)), full((1, 128)), full((1, 128)), full((1, 128)), full((1, D_SSD))]


def ssd_fwd(proj, dtraw, params, tril, ymix, *, name, comm=None):
    T = proj.shape[0]
    nc = T // SSD_L

    def body(z_ref, x_ref, dt_ref, cw_ref, cb_ref, dtb_ref, al_ref, dsk_ref, nw_ref, tril_ref, ymix_ref,
             o_ref, hs_ref, h_ref, tail_ref):
        i = pl.program_id(0)

        @pl.when(i == 0)
        def _():
            h_ref[...] = jnp.zeros_like(h_ref)
            tail_ref[...] = jnp.zeros_like(tail_ref)

        xbc = x_ref[...].astype(F32)
        hprev = h_ref[...]
        hs_ref[0] = hprev
        out, hnew, _ = _ssd_chunk_fwd(z_ref[...].astype(F32), xbc, tail_ref[...], dt_ref[...], hprev, cw_ref[...],
                                      cb_ref[...], dtb_ref[...], al_ref[...], dsk_ref[...], nw_ref[...], tril_ref[...])
        o_ref[...] = out.astype(BF16)
        h_ref[...] = hnew
        tail_ref[...] = xbc[SSD_L - 8:]

    return _call(
        body, grid=(nc,),
        in_specs=[pl.BlockSpec((SSD_L, D_SSD), lambda i: (i, 1)), pl.BlockSpec((SSD_L, D_XBC), lambda i: (i, 1)),
                  pl.BlockSpec((SSD_L, 128), lambda i: (i, 0))] + _ssd_param_specs()
                 + [pl.BlockSpec((SSD_L, SSD_L), lambda i: (0, 0)), _ANY],
        out_specs=[pl.BlockSpec((SSD_L, D_SSD), lambda i: (i, 1)),
                   pl.BlockSpec((1, SSD_STATE, D_SSD), lambda i: (i, 0, 0))],
        out_shape=[jax.ShapeDtypeStruct(ymix.shape, BF16), jax.ShapeDtypeStruct((nc, SSD_STATE, D_SSD), F32)],
        scratch_shapes=[pltpu.VMEM((SSD_STATE, D_SSD), F32), pltpu.VMEM((8, D_XBC), F32)],
        sem=("arbitrary",), name=name, args=(proj, proj, dtraw, *params, tril, ymix), aliases={10: 0}, comm=comm)


def ssd_bwd(proj, dtraw, hs, dymix, params, tril, triu, trils, headsum, *, inner):
    T = proj.shape[0]
    nc = T // SSD_L
    L = SSD_L
    step = lambda: pl.program_id(0) * inner + pl.program_id(1)

    def init(ins, outs, scr):
        dh_ref, dnext_ref = scr

        @pl.when(step() == 0)
        def _():
            dh_ref[...] = jnp.zeros_like(dh_ref)
            dnext_ref[...] = jnp.zeros_like(dnext_ref)
            for ref in outs[2:]:
                ref[...] = jnp.zeros_like(ref)

    def body(ins, outs, scr):
        (z_ref, x_ref, xprev_ref, dt_ref, hs_ref, do_ref, cw_ref, cb_ref, dtb_ref, al_ref, dsk_ref, nw_ref,
         tril_ref, triu_ref, trils_ref, hsum_ref) = ins
        dp_ref, ddt_ref, dcw_ref, dcb_ref, ddtb_ref, dal_ref, ddsk_ref, dnw_ref = outs
        dh_ref, dnext_ref = scr
        i = step()
        z = z_ref[...].astype(F32)
        xbc = x_ref[...].astype(F32)
        tail = jnp.where(i == nc - 1, 0.0, xprev_ref[...].astype(F32))
        hprev = hs_ref[0]
        cw, nw = cw_ref[...], nw_ref[...]
        hsum = hsum_ref[...]
        _, _, f = _ssd_chunk_fwd(z, xbc, tail, dt_ref[...], hprev, cw, cb_ref[...], dtb_ref[...], al_ref[...],
                                 dsk_ref[...], nw, tril_ref[...])
        dout = do_ref[...].astype(F32)
        dh = dh_ref[...]
        ghat = f["gz"] * f["r"]
        dn = dout * nw
        dgz = f["r"] * (dn - ghat * jnp.mean(dn * ghat, axis=-1, keepdims=True))
        dnw_ref[...] += jnp.sum(dout * ghat, axis=0, keepdims=True)
        sz = f["sz"]
        dy = dgz * (z * sz)
        dp_ref[:, :D_S5] = jnp.zeros((L, D_S5), BF16)
        dp_ref[:, D_S5:D_S5 + D_SSD] = (dgz * f["y"] * sz * (1.0 + z * (1.0 - sz))).astype(BF16)
        xs = f["xs"]
        ddsk_ref[...] += jnp.sum(_headsum(dy * xs, hsum), axis=0, keepdims=True)
        dyb = dy.astype(BF16)
        dye = (dy * f["exp_a_x"]).astype(BF16)
        dhb = dh.astype(BF16)
        lane = lax.broadcasted_iota(jnp.int32, (L, 128), 1)
        sub = lax.broadcasted_iota(jnp.int32, (128, L), 0)
        zero_b = jnp.zeros((L, 128), BF16)
        rsum = jnp.zeros((L, 128), F32)
        csum_t = jnp.zeros((128, L), F32)
        dx_pairs, dxst_parts, db_parts, dc_parts, dhp_parts = [], [], [], [], []
        for g in range(SSD_NGROUPS):
            gs = slice(g * SSD_GW, (g + 1) * SSD_GW)
            bg = f["bm"][:, g * 128:(g + 1) * 128].astype(BF16)
            cg = f["cm"][:, g * 128:(g + 1) * 128].astype(BF16)
            cbg = f["cbs"][g]
            dcb_g = jnp.zeros((L, L), F32)
            for j in (2 * g, 2 * g + 1):
                xp = f["xhb"][:, j * 128:(j + 1) * 128]
                dyp = dyb[:, j * 128:(j + 1) * 128]
                dxs = []
                for half, h in enumerate((2 * j, 2 * j + 1)):
                    lm = f["lms"][h]
                    dyh = jnp.where(f["low"], dyp, zero_b) if half == 0 else jnp.where(f["low"], zero_b, dyp)
                    dw = jnp.where(f["causal"], _dot_nt(dyh, xp), 0.0)
                    w = cbg * lm
                    e = dw * w
                    dcb_g = dcb_g + dw * lm
                    rsum = jnp.where(lane == h, jnp.sum(e, axis=1, keepdims=True), rsum)
                    csum_t = jnp.where(sub == h, jnp.sum(e, axis=0, keepdims=True), csum_t)
                    dxs.append(_dot_tn(w.astype(BF16), dyp))
                dx_pairs.append(jnp.where(f["low"], dxs[0], dxs[1]))
            dcbb = dcb_g.astype(BF16)
            dxst_parts.append(f["dec_x"][:, gs] * _dot(bg, dhb[:, gs]))
            dc_parts.append(_dot(dcbb, bg) + _dot_nt(dye[:, gs], f["hb"][:, gs]))
            db_parts.append(_dot_tn(dcbb, cg) + _dot_nt(f["xd"][:, gs], dhb[:, gs]))
            dhp_parts.append(f["exp_al_x"][:, gs] * dh[:, gs] + _dot_tn(cg, dye[:, gs]))
        dxst = jnp.concatenate(dxst_parts, axis=1)
        dxh = jnp.concatenate(dx_pairs, axis=1) + dxst
        q = _headsum(f["yoff"] * dy, hsum)
        dstate = _headsum(f["xh"] * dxst, hsum)
        h0t = jnp.sum(_headsum(dh * hprev, hsum), axis=0, keepdims=True) * f["exp_al"]
        da = _dot_hi(triu_ref[...], rsum - csum_t.T + q) + _dot_hi(trils_ref[...], dstate) + h0t
        dt, a_h = f["dt"], f["a_h"]
        ddt = _headsum(dxh * xs, hsum) + da * a_h
        dal_ref[...] += jnp.sum(da * dt, axis=0, keepdims=True) * a_h
        ddtraw = ddt * _sigmoid(f["pre"])
        first16 = lane < SSD_HEADS
        ddtraw = jnp.where(first16, ddtraw, 0.0)
        ddt_ref[...] = ddtraw
        ddtb_ref[...] += jnp.sum(ddtraw, axis=0, keepdims=True)
        dh_ref[...] = jnp.concatenate(dhp_parts, axis=1)
        dxa = jnp.concatenate([dxh * f["dt_x"] + f["d_x"] * dy] + db_parts + dc_parts, axis=1)
        sig, conv = f["sig"], f["conv"]
        dconv = dxa * sig * (1.0 + conv * (1.0 - sig))
        dcb_ref[...] += jnp.sum(dconv, axis=0, keepdims=True)
        rows = [jnp.sum(dconv * f["sh"][3 - k], axis=0, keepdims=True) for k in range(SSD_CONV)]
        dcw_ref[...] += jnp.concatenate(rows + [jnp.zeros((8 - SSD_CONV, D_XBC), F32)], axis=0)
        de = jnp.concatenate([dconv, dnext_ref[...]], axis=0)
        dxbc = cw[3:4] * dconv
        for s in (1, 2, 3):
            dxbc = dxbc + cw[3 - s:4 - s] * pltpu.roll(de, L + 8 - s, 0)[:L]
        dp_ref[:, D_S5 + D_SSD:] = dxbc.astype(BF16)
        dnext_ref[...] = dconv[:8]

    rev = lambda c, b: nc - 1 - (c * inner + b)
    fixed = lambda shape: pl.BlockSpec(shape, lambda c, b: (0,) * len(shape))
    return dict(
        init=init, body=body,
        in_specs=[pl.BlockSpec((L, D_SSD), lambda c, b: (rev(c, b), 1)),
                  pl.BlockSpec((L, D_XBC), lambda c, b: (rev(c, b), 1)),
                  pl.BlockSpec((8, D_XBC), lambda c, b: (jnp.maximum(rev(c, b) * (L // 8) - 1, 0), 1)),
                  pl.BlockSpec((L, 128), lambda c, b: (rev(c, b), 0)),
                  pl.BlockSpec((1, SSD_STATE, D_SSD), lambda c, b: (rev(c, b), 0, 0)),
                  pl.BlockSpec((L, D_SSD), lambda c, b: (rev(c, b), 1)),
                  fixed((8, D_XBC)), fixed((1, D_XBC)), fixed((1, 128)), fixed((1, 128)), fixed((1, 128)),
                  fixed((1, D_SSD)), fixed((L, L)), fixed((L, L)), fixed((L, L)), fixed((D_SSD, 128))],
        out_specs=[pl.BlockSpec((L, D_MAIN), lambda c, b: (rev(c, b), 0)),
                   pl.BlockSpec((L, 128), lambda c, b: (rev(c, b), 0)),
                   fixed((8, D_XBC)), fixed((1, D_XBC)), fixed((1, 128)), fixed((1, 128)), fixed((1, 128)),
                   fixed((1, D_SSD))],
        out_shape=[jax.ShapeDtypeStruct((T, D_MAIN), BF16),
                   jax.ShapeDtypeStruct((T, 128), F32), jax.ShapeDtypeStruct((8, D_XBC), F32),
                   jax.ShapeDtypeStruct((1, D_XBC), F32), jax.ShapeDtypeStruct((1, 128), F32),
                   jax.ShapeDtypeStruct((1, 128), F32), jax.ShapeDtypeStruct((1, 128), F32),
                   jax.ShapeDtypeStruct((1, D_SSD), F32)],
        scratch=[pltpu.VMEM((SSD_STATE, D_SSD), F32), pltpu.VMEM((8, D_XBC), F32)],
        args=[proj, proj, proj, dtraw, hs, dymix, *params, tril, triu, trils, headsum])


def _s5_blockdiag(v, rows_per_group, cols_per_group):
    eye = jnp.eye(S5_SEG, dtype=v.dtype)
    w = v[:, :, :, None, :] * eye[None, :, None, :, None]
    return w.reshape(S5_CH, 8 * rows_per_group, 8 * cols_per_group)


def _s5_blockdiag_extract(w, rows_per_group, cols_per_group):
    eye = jnp.eye(S5_SEG, dtype=w.dtype)
    w5 = w.reshape(S5_CH, 8, rows_per_group, 8, cols_per_group)
    return jnp.sum(w5 * eye[None, :, None, :, None], axis=3)


TM = 512

OFF = dict(mlp_w2=0, mlp_w1=1024, w_out=2048, s5_w_glu=2560, xa_wq=2816, xa_wk=3072, xa_wv=3328, xa_wo=3584, w_in=3840)
ROWS = dict(mlp_w2=1024, mlp_w1=1024, w_out=512, s5_w_glu=256, xa_wq=256, xa_wk=256, xa_wv=256, xa_wo=256, w_in=1028)
TAIL_OFF = 4872
TAIL_ROWS = 120
PACK_ROWS = TAIL_OFF + TAIL_ROWS
HALF_ROWS = PACK_ROWS // 2
ROWS_ALL = (0, PACK_ROWS, 832, True)
ROWS_EARLY = (0, OFF["w_in"], 640, False)
ROWS_LATE = (OFF["w_in"], PACK_ROWS - OFF["w_in"], 192, True)
N_SHARD = 4
SMALL_L = ("norm_mix", "s5_a_re", "s5_a_im", "s5_log_dt", "s5_b_re", "s5_b_im", "s5_c_re", "s5_c_im", "s5_d",
           "ssd_conv_w", "ssd_conv_b", "ssd_dt_bias", "ssd_a_log", "ssd_d", "ssd_norm", "norm_xattn", "norm_mem",
           "norm_mlp")
SMALL_Q = 72
CONV_ROWS = 8


def _place():
    x, y, c = lax.axis_index("x"), lax.axis_index("y"), lax.axis_index("c")
    chips = [(1 - x, y), (x, 1 - y), (1 - x, 1 - y)]
    return x, y, c, 2 * x + y, chips, (x, y, 1 - c)


def _remote(src, dst, send_sem, recv_sem, to):
    return pltpu.make_async_remote_copy(src_ref=src, dst_ref=dst, send_sem=send_sem, recv_sem=recv_sem,
                                        device_id=to, device_id_type=MESH_T)


def _half(ref, c):
    return ref.at[pl.ds(pl.multiple_of(c * HALF_ROWS, 16), HALF_ROWS), :]


def _dma_sems(*counts):
    return [pltpu.SemaphoreType.DMA((n,)) for n in counts]


def place_own(wpack, shard_idx):
    tile = PACK_ROWS // 4

    def body(s_ref, w_ref, o_ref):
        o_ref[0] = w_ref[...]

    return pl.pallas_call(
        body,
        grid_spec=pltpu.PrefetchScalarGridSpec(
            num_scalar_prefetch=1, grid=(4,),
            in_specs=[pl.BlockSpec((tile, D_MODEL), lambda i, s_ref: (i, 0))],
            out_specs=pl.BlockSpec((1, tile, D_MODEL), lambda i, s_ref: (s_ref[0], i, 0))),
        out_shape=jax.ShapeDtypeStruct((N_SHARD,) + wpack.shape, wpack.dtype),
        compiler_params=_cp("parallel"), name="place_own")(shard_idx, wpack)


def gather_over_ici(wpack, placed):
    def copies(ci, co, cs):
        w_ref, (out_ref,), (send, recv) = ci[0], co, cs
        x, y, c, s, chips, sibling = _place()
        sends = [_remote(_half(w_ref, c), _half(out_ref.at[s], c), send.at[j], recv.at[j], (*chip, c))
                 for j, chip in enumerate(chips)]
        lands = [_half(out_ref.at[2 * chip[0] + chip[1]], c) for chip in chips]
        arrivals = [_remote(land, land, send.at[j], recv.at[j], sibling) for j, land in enumerate(lands)]
        return sends, arrivals

    def start(ci, co, cs):
        for cp in copies(ci, co, cs)[0]:
            cp.start()

    def wait(ci, co, cs):
        sends, arrivals = copies(ci, co, cs)
        for cp in arrivals:
            cp.wait_recv()
        for cp in sends:
            cp.wait_send()

    return Comm("gather_ici", [wpack, placed], [jax.ShapeDtypeStruct(placed.shape, placed.dtype)], _dma_sems(3, 3),
                start, wait, aliases={1: 0})


def gather_to_sibling(landed):
    def copies(co, cs):
        (out_ref,), (send, recv) = co, cs
        x, y, c, s, chips, sibling = _place()
        slots = [out_ref.at[2 * chip[0] + chip[1]] for chip in chips]
        sends = [_remote(_half(slot, c), _half(slot, c), send.at[j], recv.at[j], sibling) for j, slot in enumerate(slots)]
        arrivals = [_remote(_half(slot, 1 - c), _half(slot, 1 - c), send.at[j], recv.at[j], sibling)
                    for j, slot in enumerate(slots)]
        return sends, arrivals

    def start(ci, co, cs):
        for cp in copies(co, cs)[0]:
            cp.start()

    def wait(ci, co, cs):
        sends, arrivals = copies(co, cs)
        for cp in arrivals:
            cp.wait_recv()
        for cp in sends:
            cp.wait_send()

    return Comm("gather_d2d", [landed], [jax.ShapeDtypeStruct(landed.shape, landed.dtype)], _dma_sems(3, 3), start, wait,
                aliases={0: 0})


def _range_half(ref, rows, c):
    half = rows[1] // 2
    return pl.ds(pl.multiple_of(rows[0] + c * half, 16), half)


def exchange_halves(gpack, rows):
    def copy(ci, co, cs):
        (g_ref,), (out_ref,), (send, recv) = ci, co, cs
        x, y, c, s, chips, sibling = _place()
        return _remote(g_ref.at[:, _range_half(g_ref, rows, 1 - c), :], out_ref, send.at[0], recv.at[0], sibling)

    return Comm("exchange", [gpack], [jax.ShapeDtypeStruct((N_SHARD, rows[1] // 2, D_MODEL), F32)], _dma_sems(1, 1),
                lambda ci, co, cs: copy(ci, co, cs).start(), lambda ci, co, cs: copy(ci, co, cs).wait())


def scatter_chips(csum):
    def copies(ci, co, cs):
        (c_ref,), (out_ref,), (send, recv) = ci, co, cs
        x, y, c, s, chips, sibling = _place()
        sends = [_remote(c_ref.at[2 * chip[0] + chip[1]], out_ref.at[s], send.at[j], recv.at[j], (*chip, c))
                 for j, chip in enumerate(chips)]
        arrivals = [_remote(c_ref.at[2 * chip[0] + chip[1]], out_ref.at[2 * chip[0] + chip[1]], send.at[j], recv.at[j],
                            (*chip, c)) for j, chip in enumerate(chips)]
        return sends, arrivals

    def start(ci, co, cs):
        for cp in copies(ci, co, cs)[0]:
            cp.start()

    def wait(ci, co, cs):
        sends, arrivals = copies(ci, co, cs)
        for cp in arrivals:
            cp.wait_recv()
        for cp in sends:
            cp.wait_send()

    return Comm("scatter", [csum], [jax.ShapeDtypeStruct(csum.shape, csum.dtype)], _dma_sems(3, 3), start, wait)


def share_reduced(gshards, smalls, l, rows):
    with_tail = rows[3]

    def copies(ci, co, cs):
        (g_ref, sm_ref), (send, recv, loc) = co, cs
        x, y, c, s, chips, sibling = _place()
        my_half = g_ref.at[l, _range_half(g_ref, rows, c), :]
        tail = g_ref.at[l, pl.ds(PACK_ROWS - TAIL_ROWS, TAIL_ROWS), :]
        big = _remote(my_half, my_half, send.at[0], recv.at[0], sibling)
        keep_tail = pltpu.make_async_copy(tail, sm_ref.at[l, s], loc.at[0])
        tails = [_remote(tail, sm_ref.at[l, s], send.at[1 + j], recv.at[1 + j], (*chip, 1)) for j, chip in enumerate(chips)]
        tails += [_remote(tail, sm_ref.at[l, s], send.at[4 + j], recv.at[1 + j], (*chip, 0)) for j, chip in enumerate(chips)]
        tails.append(_remote(tail, sm_ref.at[l, s], send.at[7], recv.at[4], sibling))
        other = g_ref.at[l, _range_half(g_ref, rows, 1 - c), :]
        big_in = _remote(other, other, send.at[0], recv.at[0], sibling)
        slots = [sm_ref.at[l, 2 * chip[0] + chip[1]] for chip in chips]
        tails_in = [_remote(slot, slot, send.at[1 + j], recv.at[1 + j], sibling) for j, slot in enumerate(slots)]
        sib_tail_in = _remote(sm_ref.at[l, s], sm_ref.at[l, s], send.at[7], recv.at[4], sibling)
        return c, big, keep_tail, tails, big_in, tails_in, sib_tail_in

    def start(ci, co, cs):
        c, big, keep_tail, tails, _, _, _ = copies(ci, co, cs)
        big.start()
        if with_tail:
            @pl.when(c == 1)
            def _():
                keep_tail.start()
                for cp in tails:
                    cp.start()

    def wait(ci, co, cs):
        c, big, keep_tail, tails, big_in, tails_in, sib_tail_in = copies(ci, co, cs)
        big_in.wait_recv()
        big.wait_send()
        if with_tail:
            for cp in tails_in:
                cp.wait_recv()

            @pl.when(c == 0)
            def _():
                sib_tail_in.wait_recv()

            @pl.when(c == 1)
            def _():
                for cp in tails:
                    cp.wait_send()
                keep_tail.wait()

    sds = lambda a: jax.ShapeDtypeStruct(a.shape, a.dtype)
    return Comm("share", [gshards, smalls], [sds(gshards), sds(smalls)], _dma_sems(8, 5, 1), start, wait,
                aliases={0: 0, 1: 1})


def _consts():
    e = np.zeros((S5_STATE, S5_STATE * S5_GROUP), np.float32)
    for p in range(S5_STATE):
        e[p, p * S5_GROUP:(p + 1) * S5_GROUP] = 1.0
    hs = np.zeros((D_SSD, 128), np.float32)
    for h in range(SSD_HEADS):
        hs[h * SSD_HEADDIM:(h + 1) * SSD_HEADDIM, h] = 1.0
    ones = np.ones((SSD_L, SSD_L), np.float32)
    return dict(expand=jnp.asarray(e), expand_t=jnp.asarray(e.T), headsum=jnp.asarray(hs).astype(BF16),
                tril=jnp.asarray(np.tril(ones)), triu=jnp.asarray(np.triu(ones)), trils=jnp.asarray(np.tril(ones, -1)))


def _s5_mats(w, cst):
    b_re = w["s5_b_re"].reshape(S5_GROUPS, S5_STATE * S5_GROUP)
    b_im = w["s5_b_im"].reshape(S5_GROUPS, S5_STATE * S5_GROUP)
    abr, abi, apr, api, bbr, bbi = s5_params_fwd(w["s5_a_re"], w["s5_a_im"], w["s5_log_dt"].reshape(S5_GROUPS, 1),
                                                 b_re, b_im, cst["expand"], name="s5_params_fwd")
    t = lambda v: v.reshape(S5_CH, 8, S5_STATE, S5_GROUP).transpose(0, 1, 3, 2)
    c4 = lambda v: v.reshape(S5_CH, 8, S5_GROUP, S5_STATE).transpose(0, 1, 3, 2)
    vec = lambda v: v.reshape(S5_CH, 1, S5_NST)
    return dict(bbr=_s5_blockdiag(t(bbr), S5_GROUP, S5_STATE).astype(BF16),
                bbi=_s5_blockdiag(t(bbi), S5_GROUP, S5_STATE).astype(BF16),
                ccr=_s5_blockdiag(c4(w["s5_c_re"]), S5_STATE, S5_GROUP).astype(BF16),
                cci=_s5_blockdiag(c4(w["s5_c_im"]), S5_STATE, S5_GROUP).astype(BF16),
                abr=vec(abr), abi=vec(abi), apr=vec(apr), api=vec(api),
                dsk=w["s5_d"].reshape(S5_CH, 1, S5_CW), b_re=b_re, b_im=b_im)


def _layer_fwd(x, mem, w, cst, next_pack=None):
    sv = {}
    g = lambda n: w[n].reshape(1, -1)
    pack = w["pack"]
    proj, h0, r0, dtraw = norm_matmul(x, g("norm_mix"), w["w_in_t"], w["w_dt_t"], tm=TM, tn=1024, name="in_proj",
                                      w_transposed=True)
    s5m = _s5_mats(w, cst)
    (ys, csr, csi), landed = s5_fwd(proj, s5m["bbr"], s5m["bbi"], s5m["ccr"], s5m["cci"], s5m["abr"], s5m["abi"],
                                    s5m["apr"], s5m["api"], s5m["dsk"], name="s5_fwd",
                                    comm=None if next_pack is None else gather_over_ici(*next_pack))
    (ymix, tglu), gathered = glu_fwd(ys, w["s5_w_glu"], tm=TM, name="glu_fwd",
                                     comm=None if next_pack is None else gather_to_sibling(landed[0]))
    ssdp = _ssd_params(w["ssd_conv_w"], w["ssd_conv_b"], w["ssd_dt_bias"], w["ssd_a_log"], w["ssd_d"], w["ssd_norm"])
    (ymix, hs), _ = ssd_fwd(proj, dtraw, ssdp, cst["tril"], ymix, name="ssd_fwd")
    x1 = matmul_res(ymix, pack, x, tm=TM, tn=1024, tk=1024, name="out_proj",
                    wspec=pl.BlockSpec((2, 512, 1024), lambda i, j, k: (k, OFF["w_out"] // 512, j)))
    q, h1, r1 = norm_matmul(x1, g("norm_xattn"), w["xa_wq"], tm=TM, tn=1024, name="q_proj")
    kv, hm, rm = norm_matmul(mem, g("norm_mem"), w["xa_wkv"], tm=mem.shape[0], tn=1024, name="kv_proj")
    o = attn_fwd(q, kv, tm=TM, name="attn_fwd")
    x2 = matmul_res(o, w["xa_wo"], x1, tm=TM, tn=1024, tk=1024, name="attn_out")
    f1, h2, r2 = norm_matmul(x2, g("norm_mlp"), pack, tm=TM, tn=1024, name="mlp_up", n_out=D_FF,
                             wspec=pl.BlockSpec((1, D_MODEL, 1024), lambda i, j: (j, OFF["mlp_w1"] // 1024, 0)))
    x3 = matmul_res(f1, pack, x2, act="relu2", tm=TM, tn=1024, tk=1024, name="mlp_down",
                    wspec=pl.BlockSpec((1, 1024, 1024), lambda i, j, k: (k, OFF["mlp_w2"] // 1024, j)))
    sv.update(x=x, proj=proj, h0=h0, r0=r0, dtraw=dtraw, s5m=s5m, ys=ys, csr=csr, csi=csi, tglu=tglu, ssdp=ssdp,
              hs=hs, ymix=ymix, x1=x1, q=q, h1=h1, r1=r1, kv=kv, hm=hm, rm=rm, o=o, x2=x2, f1=f1, h2=h2, r2=r2)
    return x3, sv, (gathered[0] if next_pack is not None else None)


def _layer_bwd(dx3, mem, w, sv, cst, extra_small=None, reduce_hooks=None, early=None):
    gr = {}
    g = lambda n: w[n].reshape(1, -1)
    pack = w["pack"]
    pshape = (N_SHARD, PACK_ROWS, D_MODEL)
    ps = lambda rows, f: pl.BlockSpec((1, rows, 1024), f)
    ps4 = lambda rows, f: pl.BlockSpec((N_SHARD, rows, 1024), f)
    dh1 = matmul_nt(dx3, pack, epi="relu2bwd", epi_args=(sv["f1"],), tm=TM, tko=1024, tn=1024, name="mlp_down_dx",
                    wspec=ps(1024, lambda i, k, n: (k, OFF["mlp_w2"] // 1024, n)), k_out=D_FF,
                    comm=reduce_hooks.exchange() if reduce_hooks else None)
    if reduce_hooks:
        dh1, recv = dh1
        reduce_hooks.after_exchange(recv[0])
    gp = matmul_tn(sv["f1"], dx3, act="relu2", tk=1024, tn=1024, tt=TM, name="mlp_down_dw", pack_shape=pshape,
                   pack_spec=ps(1024, lambda k, n, t: (k, OFF["mlp_w2"] // 1024, 0)))
    gp = matmul_tn(sv["h2"], dh1, tk=1024, tn=1024, tt=TM, name="mlp_up_dw", pack=gp, pack_shape=pshape,
                   pack_spec=ps(1024, lambda k, n, t: (n, OFF["mlp_w1"] // 1024, 0)))
    dx2, gr["norm_mlp"] = matmul_nt(dh1, pack, epi="rmsbwd", epi_args=(sv["x2"], sv["r2"], g("norm_mlp"), dx3),
                                    tm=TM, tko=1024, tn=1024, name="mlp_up_dx", k_out=D_MODEL,
                                    wspec=ps(1024, lambda i, k, n: (n, OFF["mlp_w1"] // 1024, 0)))
    do = matmul_nt(dx2, w["xa_wo"], tm=TM, tko=1024, tn=1024, name="attn_out_dx")
    gp = matmul_tn(sv["o"], dx2, tk=1024, tn=1024, tt=TM, name="attn_out_dw", pack=gp, pack_shape=pshape,
                   pack_spec=ps4(256, lambda k, n, t: (0, OFF["xa_wo"] // 256, 0)))
    dq, dkv = attn_bwd(sv["q"], sv["kv"], do, tm=TM, name="attn_bwd")
    gp = matmul_tn(sv["h1"], dq, tk=1024, tn=1024, tt=TM, name="q_proj_dw", pack=gp, pack_shape=pshape,
                   pack_spec=ps4(256, lambda k, n, t: (0, OFF["xa_wq"] // 256, 0)))
    dx1, gr["norm_xattn"] = matmul_nt(dq, w["xa_wq"], epi="rmsbwd", epi_args=(sv["x1"], sv["r1"], g("norm_xattn"), dx2),
                                      tm=TM, tko=1024, tn=1024, name="q_proj_dx")
    M = mem.shape[0]
    gp = matmul_tn(sv["hm"], dkv, tk=1024, tn=1024, tt=M, name="kv_proj_dw", pack=gp, pack_shape=pshape,
                   pack_spec=ps4(256, lambda k, n, t: (0, OFF["xa_wk"] // 256 + n, 0)))
    _, gr["norm_mem"] = matmul_nt(dkv, w["xa_wkv"], epi="rmsbwd",
                                  epi_args=(mem, sv["rm"], g("norm_mem"), jnp.zeros_like(mem)),
                                  tm=M, tko=1024, tn=1024, name="kv_proj_dx")
    dymix = matmul_nt(dx1, pack, tm=TM, tko=1024, tn=1024, name="out_proj_dx", k_out=2 * D_MODEL,
                      wspec=pl.BlockSpec((2, 512, 1024), lambda i, k, n: (k, OFF["w_out"] // 512, n)))
    gp = matmul_tn(sv["ymix"], dx1, tk=2048, tn=1024, tt=TM, name="out_proj_dw", pack=gp, pack_shape=pshape,
                   pack_spec=ps4(512, lambda k, n, t: (0, OFF["w_out"] // 512, 0)))
    gp = glu_dw(dymix, sv["ys"], sv["tglu"], gp, tt=TM, name="glu_dw",
                pack_spec=ps4(256, lambda i: (0, OFF["s5_w_glu"] // 256, 0)))
    first = _Reduction(gp, early[0], early[1], None, None, ROWS_EARLY) if early else None
    (dys,), outs = glu_bwd(dymix, sv["ys"], sv["tglu"], w["s5_w_glu"], tm=TM, name="glu_dx",
                           comm=first.exchange() if first else None)
    if first:
        first.after_exchange(outs[0])
    s5m = sv["s5m"]
    rider = ssd_bwd(sv["proj"], sv["dtraw"], sv["hs"], dymix, sv["ssdp"], cst["tril"], cst["triu"], cst["trils"],
                    cst["headsum"], inner=dx3.shape[0] // S5_TB)
    comm = _combine(reduce_hooks.scatter() if reduce_hooks else None, first.scatter() if first else None)
    (du, dbbr, dbbi, dccr, dcci, dd, dabr, dabi, dproj, ddtraw, dcw, dcb, ddtb, dal, ddsk, dnw), parts = s5_bwd(
        sv["proj"], dys, sv["csr"], sv["csi"], s5m["bbr"], s5m["bbi"], s5m["ccr"], s5m["cci"], s5m["abr"], s5m["abi"],
        s5m["apr"], s5m["api"], s5m["dsk"], name="mix_bwd", comm=comm, rider=rider)
    if reduce_hooks:
        reduce_hooks.after_scatter(parts[0])
        parts = parts[1:]
    if first:
        first.gshards, first.smalls = reduce_hooks.gshards, reduce_hooks.smalls
        first.after_scatter(parts[0])
        reduce_hooks.gshards = first.gshards
    dproj = lax.dynamic_update_slice(dproj, du, (0, 0))
    gr["ssd_conv_w"], gr["ssd_conv_b"] = dcw[:SSD_CONV], dcb[0]
    gr["ssd_dt_bias"], gr["ssd_a_log"], gr["ssd_d"] = ddtb[0, :SSD_HEADS], dal[0, :SSD_HEADS], ddsk[0, :SSD_HEADS]
    gr["ssd_norm"] = dnw[0]
    tb = lambda v: _s5_blockdiag_extract(v, S5_GROUP, S5_STATE).transpose(0, 1, 3, 2).reshape(S5_GROUPS, -1)
    tc = lambda v: _s5_blockdiag_extract(v, S5_STATE, S5_GROUP).transpose(0, 1, 3, 2).reshape(S5_GROUPS, S5_GROUP, S5_STATE)
    gr["s5_c_re"], gr["s5_c_im"] = tc(dccr), tc(dcci)
    gr["s5_d"] = dd.reshape(S5_GROUPS, S5_GROUP)
    dar, dai, dld, dbr, dbi = s5_params_bwd(
        w["s5_a_re"], w["s5_a_im"], w["s5_log_dt"].reshape(S5_GROUPS, 1), s5m["b_re"], s5m["b_im"],
        dabr.reshape(S5_GROUPS, S5_STATE), dabi.reshape(S5_GROUPS, S5_STATE), tb(dbbr), tb(dbbi),
        cst["expand"], cst["expand_t"], name="s5_params_bwd")
    gr["s5_a_re"], gr["s5_a_im"], gr["s5_log_dt"] = dar, dai, dld[:, 0]
    gr["s5_b_re"] = dbr.reshape(S5_GROUPS, S5_STATE, S5_GROUP)
    gr["s5_b_im"] = dbi.reshape(S5_GROUPS, S5_STATE, S5_GROUP)
    wt_shape = (D_MAIN + D_DT_PAD, D_MODEL)
    dwt = matmul_tn(dproj, sv["h0"], tk=1024, tn=1024, tt=TM, name="in_proj_dw", pack_shape=wt_shape,
                    pack_spec=pl.BlockSpec((1024, 1024), lambda k, n, t: (k, 0)),
                    comm=reduce_hooks.share() if reduce_hooks else None)
    if reduce_hooks:
        dwt, shared = dwt
        reduce_hooks.after_share(shared)
        if first:
            first.gshards, first.smalls = reduce_hooks.gshards, reduce_hooks.smalls
    dwt = matmul_tn(ddtraw, sv["h0"], tk=D_DT_PAD, tn=1024, tt=TM, name="in_proj_dt_dw", pack=dwt, pack_shape=wt_shape,
                    pack_spec=pl.BlockSpec((D_DT_PAD, 1024), lambda k, n, t: (D_MAIN // D_DT_PAD, 0)))
    dx0 = matmul_nt(dproj, w["w_in_t"], g2=ddtraw, w2=w["w_dt_t"], w_is_nk=True, epi="rmsbwd",
                    epi_args=(sv["x"], sv["r0"], g("norm_mix"), dx1), tm=TM, tko=1024, tn=1024, name="in_proj_dx",
                    comm=first.share() if first else None)
    if first:
        dx0, shared = dx0
        first.after_share(shared)
    dx0, gr["norm_mix"] = dx0
    gr = {k: (v[0] if k.startswith("norm_") else v) for k, v in gr.items()}
    for t in range(N_SHARD):
        shard_rows = lax.slice_in_dim(dwt, t * ROWS["w_in"], (t + 1) * ROWS["w_in"], axis=0)
        gp = lax.dynamic_update_slice(gp, shard_rows[None], (t, OFF["w_in"], 0))
    small = [gr[n].reshape(-1) for n in SMALL_L] + ([] if extra_small is None else [extra_small.reshape(-1)])
    small = jnp.concatenate(small)
    small = jnp.pad(small, (0, N_SHARD * SMALL_Q * D_MODEL - small.size)).reshape(N_SHARD, SMALL_Q, D_MODEL)
    gap = TAIL_OFF - OFF["w_in"] - ROWS["w_in"]
    gp = lax.dynamic_update_slice(gp, jnp.pad(small, ((0, 0), (gap, TAIL_ROWS - SMALL_Q), (0, 0))),
                                  (0, TAIL_OFF - gap, 0))
    return dx0, gp, first


def _local_step(x, mem, target, layers, norm_final):
    cst = _consts()
    saved = []
    for l in range(DEPTH):
        x, sv, _ = _layer_fwd(x, mem, layers[l], cst)
        saved.append(sv)
    loss, dx, dgf = loss_head(x, norm_final.reshape(1, -1), target, tm=TM, name="loss_head")
    packs = [None] * DEPTH
    for l in reversed(range(DEPTH)):
        dx, packs[l], _ = _layer_bwd(dx, mem, layers[l], saved[l], cst, extra_small=dgf[0] if l == DEPTH - 1 else None)
    return loss, dx, packs


def sum_halves(gpack, recv, c_idx, rows):
    first, count, tile, _ = rows
    half = count // 2
    nb = half // tile

    def body(c_ref, a_ref, b_ref, o_ref):
        o_ref[...] = (a_ref[...] + b_ref[...]).astype(BF16)

    blk = (1, tile, D_MODEL)
    return pl.pallas_call(
        body,
        grid_spec=pltpu.PrefetchScalarGridSpec(
            num_scalar_prefetch=1, grid=(N_SHARD, nb),
            in_specs=[pl.BlockSpec(blk, lambda t, i, c_ref: (t, first // tile + c_ref[0] * nb + i, 0)),
                      pl.BlockSpec(blk, lambda t, i, c_ref: (t, i, 0))],
            out_specs=pl.BlockSpec(blk, lambda t, i, c_ref: (t, i, 0))),
        out_shape=jax.ShapeDtypeStruct((N_SHARD, half, D_MODEL), BF16),
        compiler_params=_cp("parallel", "parallel"), name="sum_halves")(c_idx, gpack, recv)


def sum_chips(parts, csum, gshards, l, place_idx, rows):
    first, count, tile, _ = rows
    nb = count // 2 // tile

    def body(pi_ref, p0, p1, p2, p3, own, g_ref, o_ref):
        s = pi_ref[0]
        vals = [jnp.where(s == k, own[0], p[0]).astype(F32) for k, p in enumerate((p0, p1, p2, p3))]
        o_ref[0] = ((vals[0] + vals[1]) + vals[2]) + vals[3]

    blk = (1, tile, D_MODEL)
    part_spec = lambda k: pl.BlockSpec(blk, lambda i, pi_ref: (jnp.where(pi_ref[0] == k, (k + 1) % N_SHARD, k), i, 0))
    return pl.pallas_call(
        body,
        grid_spec=pltpu.PrefetchScalarGridSpec(
            num_scalar_prefetch=1, grid=(nb,),
            in_specs=[part_spec(k) for k in range(N_SHARD)]
                     + [pl.BlockSpec(blk, lambda i, pi_ref: (pi_ref[0], i, 0)), _ANY],
            out_specs=pl.BlockSpec(blk, lambda i, pi_ref: (l, first // tile + pi_ref[1] * nb + i, 0))),
        out_shape=jax.ShapeDtypeStruct(gshards.shape, F32), input_output_aliases={6: 0},
        compiler_params=_cp("parallel"), name="sum_chips")(place_idx, parts, parts, parts, parts, csum, gshards)


class _Reduction:
    def __init__(self, gpack, layer, place_idx, gshards, smalls, rows=ROWS_ALL):
        self.gpack, self.layer, self.place_idx, self.gshards, self.smalls = gpack, layer, place_idx, gshards, smalls
        self.rows = rows

    def exchange(self):
        return exchange_halves(self.gpack, self.rows)

    def after_exchange(self, recv):
        self.csum = sum_halves(self.gpack, recv, self.place_idx[1:], self.rows)

    def scatter(self):
        return scatter_chips(self.csum)

    def after_scatter(self, parts):
        self.gshards = sum_chips(parts, self.csum, self.gshards, self.layer, self.place_idx, self.rows)

    def share(self):
        return share_reduced(self.gshards, self.smalls, self.layer, self.rows)

    def after_share(self, shared):
        self.gshards, self.smalls = shared

    def run_alone(self):
        self.after_exchange(_comm_only(self.exchange())[0])
        self.after_scatter(_comm_only(self.scatter())[0])
        self.after_share(_comm_only(self.share()))
        return self.gshards, self.smalls


def adamw(w, g, m, v, *, name):
    shape = w.shape
    cols = shape[-1]
    rows = w.size // cols
    tr = 512 if rows % 512 == 0 else rows
    c1 = 1.0 / (1.0 - ADAM_B1 ** ADAM_STEP)
    c2 = 1.0 / (1.0 - ADAM_B2 ** ADAM_STEP)

    def body(w_ref, g_ref, m_ref, v_ref, d_ref, nm_ref, nv_ref):
        gv = g_ref[...]
        nm = ADAM_B1 * m_ref[...] + (1.0 - ADAM_B1) * gv
        nv = ADAM_B2 * v_ref[...] + (1.0 - ADAM_B2) * (gv * gv)
        d_ref[...] = -ADAM_LR * ((nm * c1) / (jnp.sqrt(nv * c2) + ADAM_EPS) + ADAM_WD * w_ref[...])
        nm_ref[...] = nm
        nv_ref[...] = nv

    spec = pl.BlockSpec((tr, cols), lambda i: (i, 0))
    sds = jax.ShapeDtypeStruct((rows, cols), F32)
    outs = pl.pallas_call(body, grid=(rows // tr,), in_specs=[spec] * 4, out_specs=[spec] * 3, out_shape=[sds] * 3,
                          compiler_params=_cp("parallel"), name=name)(
                              *[a.reshape(rows, cols) for a in (w, g, m, v)])
    return [o.reshape(shape) for o in outs]


def _own_pack(wts, l):
    rows = [wts[n][l].T if n == "w_in" else wts[n][l].reshape(ROWS[n], D_MODEL) for n in sorted(OFF, key=OFF.get)]
    cw = wts["ssd_conv_w"][l].reshape(-1)
    hi = lax.reduce_precision(cw, 8, 7)
    mid = lax.reduce_precision(cw - hi, 8, 7)
    lo = lax.reduce_precision(cw - hi - mid, 8, 7)
    conv = jnp.pad(jnp.concatenate([hi, mid, lo]), (0, CONV_ROWS * D_MODEL - 3 * cw.size)).reshape(CONV_ROWS, D_MODEL)
    gap = jnp.zeros((TAIL_OFF - OFF["w_in"] - ROWS["w_in"], D_MODEL), F32)
    rest = jnp.zeros((TAIL_ROWS - CONV_ROWS, D_MODEL), F32)
    return jnp.concatenate(rows + [gap, conv, rest], axis=0).astype(BF16)


def _layer_weights(gathered, wts, l):
    w = {n: wts[n][l] for n in SMALL_L if n != "ssd_conv_w"}
    w["pack"] = gathered
    square = lambda n: gathered[:, OFF[n]:OFF[n] + ROWS[n]].reshape(N_SHARD * ROWS[n], D_MODEL)
    w["s5_w_glu"], w["xa_wq"], w["xa_wo"] = square("s5_w_glu"), square("xa_wq"), square("xa_wo")
    w["xa_wkv"] = jnp.concatenate([square("xa_wk"), square("xa_wv")], axis=1)
    w_in_t = square("w_in")
    w["w_in_t"] = w_in_t[:D_MAIN]
    w["w_dt_t"] = jnp.pad(w_in_t[D_MAIN:], ((0, D_DT_PAD - SSD_HEADS), (0, 0)))
    per = SSD_CONV * D_XBC // N_SHARD
    cw = gathered[:, TAIL_OFF:TAIL_OFF + CONV_ROWS].astype(F32).reshape(N_SHARD, -1)[:, :3 * per]
    cw = cw.reshape(N_SHARD, 3, SSD_CONV, D_XBC // N_SHARD)
    cw = (cw[:, 0] + cw[:, 1]) + cw[:, 2]
    w["ssd_conv_w"] = cw.transpose(1, 0, 2).reshape(SSD_CONV, D_XBC)
    return w


def kernel(x, mem, norm_mix, w_in, s5_a_re, s5_a_im, s5_log_dt, s5_b_re, s5_b_im, s5_c_re, s5_c_im, s5_d, s5_w_glu, ssd_conv_w, ssd_conv_b, ssd_dt_bias, ssd_a_log, ssd_d, ssd_norm, w_out, norm_xattn, norm_mem, xa_wq, xa_wk, xa_wv, xa_wo, norm_mlp, mlp_w1, mlp_w2, norm_final, loss_target, m_norm_mix, m_w_in, m_s5_a_re, m_s5_a_im, m_s5_log_dt, m_s5_b_re, m_s5_b_im, m_s5_c_re, m_s5_c_im, m_s5_d, m_s5_w_glu, m_ssd_conv_w, m_ssd_conv_b, m_ssd_dt_bias, m_ssd_a_log, m_ssd_d, m_ssd_norm, m_w_out, m_norm_xattn, m_norm_mem, m_xa_wq, m_xa_wk, m_xa_wv, m_xa_wo, m_norm_mlp, m_mlp_w1, m_mlp_w2, m_norm_final, v_norm_mix, v_w_in, v_s5_a_re, v_s5_a_im, v_s5_log_dt, v_s5_b_re, v_s5_b_im, v_s5_c_re, v_s5_c_im, v_s5_d, v_s5_w_glu, v_ssd_conv_w, v_ssd_conv_b, v_ssd_dt_bias, v_ssd_a_log, v_ssd_d, v_ssd_norm, v_w_out, v_norm_xattn, v_norm_mem, v_xa_wq, v_xa_wk, v_xa_wv, v_xa_wo, v_norm_mlp, v_mlp_w1, v_mlp_w2, v_norm_final):
    names = ("norm_mix", "w_in", "s5_a_re", "s5_a_im", "s5_log_dt", "s5_b_re", "s5_b_im", "s5_c_re", "s5_c_im", "s5_d",
             "s5_w_glu", "ssd_conv_w", "ssd_conv_b", "ssd_dt_bias", "ssd_a_log", "ssd_d", "ssd_norm", "w_out",
             "norm_xattn", "norm_mem", "xa_wq", "xa_wk", "xa_wv", "xa_wo", "norm_mlp", "mlp_w1", "mlp_w2", "norm_final")
    loc = locals()
    wts = {n: loc[n] for n in names}
    mom = {n: loc["m_" + n] for n in names}
    var = {n: loc["v_" + n] for n in names}
    shard = 2 * lax.axis_index("x") + lax.axis_index("y")
    core = lax.axis_index("c")

    cst = _consts()
    place_idx = jnp.stack([shard, core]).astype(jnp.int32)
    h, mem0 = x[0], mem[0]

    own = _own_pack(wts, 0)
    landed = _comm_only(gather_over_ici(own, place_own(own, place_idx[:1])))[0]
    gathered = _comm_only(gather_to_sibling(landed))[0]
    layers, saved = [], []
    for l in range(DEPTH):
        layers.append(_layer_weights(gathered, wts, l))
        nxt = None
        if l + 1 < DEPTH:
            own = _own_pack(wts, l + 1)
            nxt = (own, place_own(own, place_idx[:1]))
        h, sv, gathered = _layer_fwd(h, mem0, layers[l], cst, next_pack=nxt)
        saved.append(sv)
    loss, dx, dgf = loss_head(h, norm_final.reshape(1, -1), loss_target[0], tm=TM, name="loss_head")

    gshards = jnp.zeros((DEPTH, PACK_ROWS, D_MODEL), F32)
    smalls = jnp.zeros((DEPTH, N_SHARD, TAIL_ROWS, D_MODEL), F32)
    pending = None
    for l in reversed(range(DEPTH)):
        dx, gpack, first = _layer_bwd(dx, mem0, layers[l], saved[l], cst, extra_small=dgf[0] if l == DEPTH - 1 else None,
                                      reduce_hooks=pending, early=(l, place_idx) if l == 0 else None)
        if pending is not None:
            gshards, smalls = pending.gshards, pending.smalls
        pending = _Reduction(gpack, l, place_idx, gshards, smalls)
    gshards, smalls = _Reduction(gpack, 0, place_idx, first.gshards, first.smalls, ROWS_LATE).run_alone()

    g = {n: gshards[:, OFF[n]:OFF[n] + ROWS[n]].reshape(wts[n].shape) for n in OFF if n != "w_in"}
    g["w_in"] = gshards[:, OFF["w_in"]:OFF["w_in"] + ROWS["w_in"]].transpose(0, 2, 1)
    small_red = smalls[:, :, :SMALL_Q].reshape(DEPTH, -1)
    off = 0
    for n in SMALL_L:
        shape = (SSD_CONV, D_XBC) if n == "ssd_conv_w" else wts[n].shape[1:]
        size = math.prod(shape)
        g[n] = small_red[:, off:off + size].reshape((DEPTH,) + shape)
        off += size
    g["norm_final"] = small_red[DEPTH - 1, off:off + D_MODEL]
    g["ssd_conv_w"] = lax.dynamic_slice_in_dim(g["ssd_conv_w"], shard * (D_XBC // N_SHARD), D_XBC // N_SHARD, axis=2)

    deltas, new_m, new_v = {}, {}, {}
    for n in names:
        deltas[n], new_m[n], new_v[n] = adamw(wts[n], g[n], mom[n], var[n], name="adamw_" + n)
    loss_all = lax.psum(loss[0, 0], ("x", "y", "c"))
    return (loss_all, dx[None], *[g[n] for n in names], *[deltas[n] for n in names], *[new_m[n] for n in names],
            *[new_v[n] for n in names])
```

```python
import functools
import math

import jax
import jax.numpy as jnp
import numpy as np
from jax import lax
from jax.experimental import pallas as pl
from jax.experimental.pallas import tpu as pltpu

F32 = jnp.float32
BF16 = jnp.bfloat16
HIGHEST = lax.Precision.HIGHEST

D_MODEL = 1024
DEPTH = 4
D_S5 = 1024
D_SSD = 1024
S5_GROUP = 16
S5_GROUPS = 64
S5_STATE = 64
SSD_HEADS = 16
SSD_HEADDIM = 64
SSD_NGROUPS = 4
SSD_STATE = 128
SSD_CONV = 4
SSD_CHUNK = 128
D_XBC = 2048
D_MAIN = 4096
D_DT_PAD = 128
XA_HEADS = 4
XA_HEAD_DIM = 256
D_FF = 4096
EPS = 1e-5
ADAM_LR, ADAM_B1, ADAM_B2, ADAM_EPS, ADAM_WD, ADAM_STEP = 0.001, 0.9, 0.999, 1e-08, 0.01, 10

VMEM_LIMIT = 56 * 1024 * 1024
MESH_T = pl.DeviceIdType.MESH


def _cp(*sem):
    return pltpu.CompilerParams(dimension_semantics=tuple(sem) if sem else None, vmem_limit_bytes=VMEM_LIMIT)


_ANY = pl.BlockSpec(memory_space=pl.ANY)


class Comm:
    def __init__(self, name, inputs, out_shapes, sems, start, wait, aliases=None):
        self.name, self.inputs, self.out_shapes, self.sems = name, list(inputs), list(out_shapes), list(sems)
        self.start, self.wait, self.aliases = start, wait, dict(aliases or {})


def _call(body, *, grid, in_specs, out_specs, out_shape, args, scratch_shapes=(), sem, name, comm=None, aliases=None):
    in_specs, out_specs, out_shape = list(in_specs), list(out_specs), list(out_shape)
    scratch_shapes = list(scratch_shapes)
    aliases = dict(aliases or {})
    if comm is None:
        res = pl.pallas_call(body, grid=grid, in_specs=in_specs, out_specs=out_specs, out_shape=out_shape,
                             scratch_shapes=scratch_shapes, compiler_params=_cp(*sem), name=name,
                             input_output_aliases=aliases)(*args)
        return list(res), []
    n_in, n_out, n_scr = len(in_specs), len(out_specs), len(scratch_shapes)
    c_in, c_out = len(comm.inputs), len(comm.out_shapes)

    def wrapped(*refs):
        a, refs = refs[:n_in], refs[n_in:]
        ci, refs = refs[:c_in], refs[c_in:]
        o, refs = refs[:n_out], refs[n_out:]
        co, refs = refs[:c_out], refs[c_out:]
        s, cs = refs[:n_scr], refs[n_scr:]
        first = functools.reduce(jnp.logical_and, [pl.program_id(d) == 0 for d in range(len(grid))])
        last = functools.reduce(jnp.logical_and, [pl.program_id(d) == grid[d] - 1 for d in range(len(grid))])

        @pl.when(first)
        def _():
            comm.start(ci, co, cs)

        body(*a, *o, *s)

        @pl.when(last)
        def _():
            comm.wait(ci, co, cs)

    for i, j in comm.aliases.items():
        aliases[n_in + i] = n_out + j
    res = pl.pallas_call(wrapped, grid=grid, in_specs=in_specs + [_ANY] * c_in, out_specs=out_specs + [_ANY] * c_out,
                         out_shape=out_shape + comm.out_shapes, scratch_shapes=scratch_shapes + comm.sems,
                         compiler_params=_cp(*(("arbitrary",) * len(grid))), name=name + "_" + comm.name,
                         input_output_aliases=aliases)(*args, *comm.inputs)
    return list(res[:n_out]), list(res[n_out:])


def _combine(a, b):
    if a is None or b is None:
        return a or b
    ni, no, ns = len(a.inputs), len(a.out_shapes), len(a.sems)

    def start(ci, co, cs):
        a.start(ci[:ni], co[:no], cs[:ns])
        b.start(ci[ni:], co[no:], cs[ns:])

    def wait(ci, co, cs):
        a.wait(ci[:ni], co[:no], cs[:ns])
        b.wait(ci[ni:], co[no:], cs[ns:])

    aliases = dict(a.aliases)
    aliases.update({ni + i: no + j for i, j in b.aliases.items()})
    return Comm(a.name + "_" + b.name, a.inputs + b.inputs, a.out_shapes + b.out_shapes, a.sems + b.sems, start, wait,
                aliases)


def _comm_only(comm):
    def body(*refs):
        ci, refs = refs[:len(comm.inputs)], refs[len(comm.inputs):]
        co, cs = refs[:len(comm.out_shapes)], refs[len(comm.out_shapes):]
        comm.start(ci, co, cs)
        comm.wait(ci, co, cs)

    res = pl.pallas_call(body, in_specs=[_ANY] * len(comm.inputs), out_specs=[_ANY] * len(comm.out_shapes),
                         out_shape=comm.out_shapes, scratch_shapes=comm.sems, name=comm.name,
                         input_output_aliases=comm.aliases)(*comm.inputs)
    return list(res)


def _dot(a, b):
    return jnp.dot(a, b, preferred_element_type=F32)


def _dot_nt(a, b):
    return lax.dot_general(a, b, (((1,), (1,)), ((), ())), preferred_element_type=F32)


def _dot_tn(a, b):
    return lax.dot_general(a, b, (((0,), (0,)), ((), ())), preferred_element_type=F32)


def _dot_hi(a, b):
    return jnp.dot(a, b, precision=HIGHEST, preferred_element_type=F32)


def _gelu(x):
    c = math.sqrt(2.0 / math.pi)
    return 0.5 * x * (1.0 + jnp.tanh(c * (x + 0.044715 * x * x * x)))


def _gelu_grad(x):
    c = math.sqrt(2.0 / math.pi)
    t = jnp.tanh(c * (x + 0.044715 * x * x * x))
    return 0.5 * (1.0 + t) + 0.5 * x * (1.0 - t * t) * c * (1.0 + 3 * 0.044715 * x * x)


def _sigmoid(x):
    return 1.0 / (1.0 + jnp.exp(-x))


def _act(a, act):
    if act is None:
        return a.astype(BF16)
    a = a.astype(F32)
    if act == "relu2":
        a = jnp.maximum(a, 0.0)
        return (a * a).astype(BF16)
    if act == "gelu":
        return _gelu(a).astype(BF16)
    raise ValueError(act)


def _pack_block(ref):
    return ref[...].reshape(-1, ref.shape[-1])


def norm_matmul(x, g, w, w2=None, *, tm, tn, name, wspec=None, n_out=None, w_transposed=False, comm=None):
    T, D = x.shape
    N = n_out if wspec is not None else (w.shape[0] if w_transposed else w.shape[1])
    wget = (lambda r: r[...]) if wspec is None else _pack_block
    mm = _dot_nt if w_transposed else _dot
    has2 = w2 is not None

    def body(x_ref, g_ref, w_ref, *rest):
        if has2:
            w2_ref, o_ref, h_ref, r_ref, o2_ref = rest
        else:
            o_ref, h_ref, r_ref = rest
        j = pl.program_id(1)

        @pl.when(j == 0)
        def _():
            xv = x_ref[...]
            r = lax.rsqrt(jnp.mean(xv * xv, axis=-1, keepdims=True) + EPS)
            h = (xv * r * g_ref[...]).astype(BF16)
            h_ref[...] = h
            r_ref[...] = r
            if has2:
                o2_ref[...] = mm(h, w2_ref[...])

        o_ref[...] = mm(h_ref[...], wget(w_ref)).astype(o_ref.dtype)

    if wspec is None:
        wspec = pl.BlockSpec((tn, D), lambda i, j: (j, 0)) if w_transposed else pl.BlockSpec((D, tn), lambda i, j: (0, j))
    in_specs = [pl.BlockSpec((tm, D), lambda i, j: (i, 0)), pl.BlockSpec((1, D), lambda i, j: (0, 0)), wspec]
    out_shape = [jax.ShapeDtypeStruct((T, N), BF16), jax.ShapeDtypeStruct((T, D), BF16),
                 jax.ShapeDtypeStruct((T, 1), F32)]
    out_specs = [pl.BlockSpec((tm, tn), lambda i, j: (i, j)), pl.BlockSpec((tm, D), lambda i, j: (i, 0)),
                 pl.BlockSpec((tm, 1), lambda i, j: (i, 0))]
    args = [x, g, w]
    if has2:
        in_specs.append(pl.BlockSpec(w2.shape, lambda i, j: (0, 0)))
        out_shape.append(jax.ShapeDtypeStruct((T, D_DT_PAD), F32))
        out_specs.append(pl.BlockSpec((tm, D_DT_PAD), lambda i, j: (i, 0)))
        args.append(w2)
    res, cres = _call(body, grid=(T // tm, N // tn), in_specs=in_specs, out_specs=out_specs, out_shape=out_shape,
                      sem=("parallel", "arbitrary"), name=name, args=args, comm=comm)
    return res if comm is None else (res, cres)


def matmul_res(a, w, r, *, act=None, tm, tn, tk, name, wspec=None):
    T, K = a.shape
    N = r.shape[1]
    wget = (lambda r_: r_[...]) if wspec is None else _pack_block
    nk = K // tk

    def body(a_ref, w_ref, r_ref, o_ref):
        k = pl.program_id(2)

        @pl.when(k == 0)
        def _():
            o_ref[...] = r_ref[...]

        o_ref[...] += _dot(_act(a_ref[...], act), wget(w_ref))

    return pl.pallas_call(
        body, grid=(T // tm, N // tn, nk),
        in_specs=[pl.BlockSpec((tm, tk), lambda i, j, k: (i, k)),
                  pl.BlockSpec((tk, tn), lambda i, j, k: (k, j)) if wspec is None else wspec,
                  pl.BlockSpec((tm, tn), lambda i, j, k: (i, j))],
        out_specs=pl.BlockSpec((tm, tn), lambda i, j, k: (i, j)),
        out_shape=jax.ShapeDtypeStruct((T, N), F32),
        compiler_params=_cp("parallel", "parallel", "arbitrary"), name=name)(a, w, r)


def glu_fwd(ys, w, *, tm, name, comm=None):
    T, N = ys.shape

    def body(y_ref, w_ref, o_ref, t_ref):
        a = _gelu(y_ref[...].astype(F32))
        t = _dot(a.astype(BF16), w_ref[...])
        o_ref[...] = (a * _sigmoid(t)).astype(BF16)
        t_ref[...] = t.astype(BF16)

    return _call(
        body, grid=(T // tm,),
        in_specs=[pl.BlockSpec((tm, N), lambda i: (i, 0)), pl.BlockSpec((N, N), lambda i: (0, 0))],
        out_specs=[pl.BlockSpec((tm, N), lambda i: (i, 0)), pl.BlockSpec((tm, N), lambda i: (i, 0))],
        out_shape=[jax.ShapeDtypeStruct((T, 2 * N), BF16), jax.ShapeDtypeStruct((T, N), BF16)],
        sem=("parallel",), name=name, args=(ys, w), comm=comm)


def glu_bwd(dymix, ys, t, w, *, tm, name, comm=None):
    T, N = ys.shape

    def body(d_ref, y_ref, t_ref, w_ref, dys_ref):
        d = d_ref[...].astype(F32)
        ysv = y_ref[...].astype(F32)
        s = _sigmoid(t_ref[...].astype(F32))
        dt = (d * _gelu(ysv) * s * (1.0 - s)).astype(BF16)
        dys_ref[...] = ((d * s + _dot_nt(dt, w_ref[...])) * _gelu_grad(ysv)).astype(BF16)

    spec = pl.BlockSpec((tm, N), lambda i: (i, 0))
    return _call(body, grid=(T // tm,), in_specs=[spec, spec, spec, pl.BlockSpec((N, N), lambda i: (0, 0))],
                 out_specs=[spec], out_shape=[jax.ShapeDtypeStruct((T, N), BF16)], sem=("parallel",), name=name,
                 args=(dymix, ys, t, w), comm=comm)


def glu_dw(dymix, ys, t, pack, *, pack_spec, tt, name):
    T, N = ys.shape

    def body(d_ref, y_ref, t_ref, pack_ref, o_ref):
        i = pl.program_id(0)
        a = _gelu(y_ref[...].astype(F32))
        s = _sigmoid(t_ref[...].astype(F32))
        dt = (d_ref[...].astype(F32) * a * s * (1.0 - s)).astype(BF16)
        part = _dot_tn(a.astype(BF16), dt).reshape(o_ref.shape)

        @pl.when(i == 0)
        def _():
            o_ref[...] = part

        @pl.when(i > 0)
        def _():
            o_ref[...] += part

    spec = pl.BlockSpec((tt, N), lambda i: (i, 0))
    return pl.pallas_call(body, grid=(T // tt,), in_specs=[spec, spec, spec, _ANY], out_specs=pack_spec,
                          out_shape=jax.ShapeDtypeStruct(pack.shape, F32), input_output_aliases={3: 0},
                          compiler_params=_cp("arbitrary"), name=name)(dymix, ys, t, pack)


def matmul_nt(g, w, *, epi=None, epi_args=(), g2=None, w2=None, tm, tko, tn, out_dtype=BF16, name, wspec=None,
              k_out=None, comm=None, w_is_nk=False):
    T, N = g.shape
    K = k_out if wspec is not None else (w.shape[1] if w_is_nk else w.shape[0])
    wget = (lambda r_: r_[...]) if wspec is None else _pack_block
    mm = _dot if w_is_nk else _dot_nt
    nn = N // tn
    has2 = g2 is not None
    rms = epi == "rmsbwd"
    if rms:
        assert tko == K
    n_epi = len(epi_args)

    def body(*refs):
        g_ref, w_ref = refs[0], refs[1]
        pos = 2
        if has2:
            g2_ref, w2_ref = refs[2], refs[3]
            pos = 4
        e_refs = refs[pos:pos + n_epi]
        pos += n_epi
        o_ref = refs[pos]
        pos += 1
        if rms:
            dg_ref = refs[pos]
            pos += 1
        acc_ref = refs[pos]
        i = pl.program_id(0)
        n = pl.program_id(2)
        part = mm(g_ref[...].astype(BF16), wget(w_ref))

        @pl.when(n == 0)
        def _():
            acc_ref[...] = part

        @pl.when(n > 0)
        def _():
            acc_ref[...] += part

        @pl.when(n == nn - 1)
        def _():
            acc = acc_ref[...]
            if has2:
                acc = acc + mm(g2_ref[...].astype(BF16), w2_ref[...])
            if epi is None:
                o_ref[...] = acc.astype(o_ref.dtype)
            elif epi == "relu2bwd":
                h1 = e_refs[0][...].astype(F32)
                o_ref[...] = (acc * 2.0 * jnp.maximum(h1, 0.0)).astype(o_ref.dtype)
            elif epi == "glubwd":
                da1 = e_refs[0][...].astype(F32)
                ys = e_refs[1][...].astype(F32)
                o_ref[...] = ((da1 + acc) * _gelu_grad(ys)).astype(o_ref.dtype)
            elif epi == "rmsbwd":
                xv, rs, gain, rv = e_refs[0][...], e_refs[1][...], e_refs[2][...], e_refs[3][...]
                xhat = xv * rs
                gd = acc * gain
                o_ref[...] = rv + rs * (gd - xhat * jnp.mean(gd * xhat, axis=-1, keepdims=True))
                part_g = jnp.sum(acc * xhat, axis=0, keepdims=True)

                @pl.when(i == 0)
                def _():
                    dg_ref[...] = part_g

                @pl.when(i > 0)
                def _():
                    dg_ref[...] += part_g

    if wspec is None:
        wspec = (pl.BlockSpec((tn, tko), lambda i, k, n: (n, k)) if w_is_nk
                 else pl.BlockSpec((tko, tn), lambda i, k, n: (k, n)))
    in_specs = [pl.BlockSpec((tm, tn), lambda i, k, n: (i, n)), wspec]
    args = [g, w]
    if has2:
        n2 = g2.shape[1]
        in_specs += [pl.BlockSpec((tm, n2), lambda i, k, n: (i, 0)),
                     pl.BlockSpec((n2, tko), lambda i, k, n: (0, k)) if w_is_nk
                     else pl.BlockSpec((tko, n2), lambda i, k, n: (k, 0))]
        args += [g2, w2]
    if epi == "relu2bwd" or epi == "glubwd":
        in_specs += [pl.BlockSpec((tm, tko), lambda i, k, n: (i, k))] * n_epi
    elif rms:
        in_specs += [pl.BlockSpec((tm, K), lambda i, k, n: (i, 0)), pl.BlockSpec((tm, 1), lambda i, k, n: (i, 0)),
                     pl.BlockSpec((1, K), lambda i, k, n: (0, 0)), pl.BlockSpec((tm, K), lambda i, k, n: (i, 0))]
    args += list(epi_args)
    out_shape = [jax.ShapeDtypeStruct((T, K), F32 if rms else out_dtype)]
    out_specs = [pl.BlockSpec((tm, tko), lambda i, k, n: (i, k))]
    if rms:
        out_shape.append(jax.ShapeDtypeStruct((1, K), F32))
        out_specs.append(pl.BlockSpec((1, K), lambda i, k, n: (0, 0)))
    sem = ("arbitrary",) * 3 if rms else ("parallel", "parallel", "arbitrary")
    res, cres = _call(body, grid=(T // tm, K // tko, nn), in_specs=in_specs, out_specs=out_specs, out_shape=out_shape,
                      scratch_shapes=[pltpu.VMEM((tm, tko), F32)], sem=sem, name=name, args=args, comm=comm)
    res = res if rms else res[0]
    return res if comm is None else (res, cres)


def matmul_tn(a, g, *, act=None, tk, tn, tt, name, pack=None, pack_spec=None, pack_shape=None):
    T, K = a.shape
    N = g.shape[1]
    to_pack = pack_spec is not None

    def body(a_ref, g_ref, *rest):
        o_ref = rest[-1]
        t = pl.program_id(2)
        part = _dot_tn(_act(a_ref[...], act), g_ref[...].astype(BF16))
        part = part.reshape(o_ref.shape)

        @pl.when(t == 0)
        def _():
            o_ref[...] = part

        @pl.when(t > 0)
        def _():
            o_ref[...] += part

    in_specs = [pl.BlockSpec((tt, tk), lambda k, n, t: (t, k)), pl.BlockSpec((tt, tn), lambda k, n, t: (t, n))]
    args = [a, g]
    aliases = {}
    if pack is not None:
        in_specs.append(_ANY)
        args.append(pack)
        aliases = {2: 0}
    return pl.pallas_call(
        body, grid=(K // tk, N // tn, T // tt), in_specs=in_specs,
        out_specs=pack_spec if to_pack else pl.BlockSpec((tk, tn), lambda k, n, t: (k, n)),
        out_shape=jax.ShapeDtypeStruct(pack_shape if to_pack else (K, N), F32), input_output_aliases=aliases,
        compiler_params=_cp("parallel", "parallel", "arbitrary"), name=name)(*args)


def attn_fwd(q, kv, *, tm, name):
    T = q.shape[0]
    M = kv.shape[0]
    scale = XA_HEAD_DIM ** -0.5

    def body(q_ref, kv_ref, o_ref):
        for h in range(XA_HEADS):
            sl = slice(h * XA_HEAD_DIM, (h + 1) * XA_HEAD_DIM)
            kh = kv_ref[:, h * XA_HEAD_DIM:(h + 1) * XA_HEAD_DIM]
            vh = kv_ref[:, D_MODEL + h * XA_HEAD_DIM:D_MODEL + (h + 1) * XA_HEAD_DIM]
            s = _dot_nt(q_ref[:, sl], kh) * scale
            s = s - jnp.max(s, axis=-1, keepdims=True)
            p = jnp.exp(s)
            p = p / jnp.sum(p, axis=-1, keepdims=True)
            o_ref[:, sl] = _dot(p.astype(BF16), vh).astype(BF16)

    return pl.pallas_call(
        body, grid=(T // tm,),
        in_specs=[pl.BlockSpec((tm, D_MODEL), lambda i: (i, 0)), pl.BlockSpec((M, 2 * D_MODEL), lambda i: (0, 0))],
        out_specs=pl.BlockSpec((tm, D_MODEL), lambda i: (i, 0)),
        out_shape=jax.ShapeDtypeStruct((T, D_MODEL), BF16),
        compiler_params=_cp("parallel"), name=name)(q, kv)


def attn_bwd(q, kv, do, *, tm, name):
    T = q.shape[0]
    M = kv.shape[0]
    scale = XA_HEAD_DIM ** -0.5

    def body(q_ref, kv_ref, do_ref, dq_ref, dkv_ref):
        i = pl.program_id(0)

        @pl.when(i == 0)
        def _():
            dkv_ref[...] = jnp.zeros_like(dkv_ref)

        for h in range(XA_HEADS):
            sl = slice(h * XA_HEAD_DIM, (h + 1) * XA_HEAD_DIM)
            slv = slice(D_MODEL + h * XA_HEAD_DIM, D_MODEL + (h + 1) * XA_HEAD_DIM)
            qh = q_ref[:, sl]
            kh = kv_ref[:, sl]
            vh = kv_ref[:, slv]
            doh = do_ref[:, sl]
            s = _dot_nt(qh, kh) * scale
            s = s - jnp.max(s, axis=-1, keepdims=True)
            p = jnp.exp(s)
            p = p / jnp.sum(p, axis=-1, keepdims=True)
            pb = p.astype(BF16)
            dkv_ref[:, slv] += _dot_tn(pb, doh)
            dp = _dot_nt(doh, vh)
            ds = (p * (dp - jnp.sum(dp * p, axis=-1, keepdims=True)) * scale).astype(BF16)
            dq_ref[:, sl] = _dot(ds, kh).astype(BF16)
            dkv_ref[:, sl] += _dot_tn(ds, qh)

    spec = pl.BlockSpec((tm, D_MODEL), lambda i: (i, 0))
    kvspec = pl.BlockSpec((M, 2 * D_MODEL), lambda i: (0, 0))
    return pl.pallas_call(
        body, grid=(T // tm,), in_specs=[spec, kvspec, spec], out_specs=[spec, kvspec],
        out_shape=[jax.ShapeDtypeStruct((T, D_MODEL), BF16), jax.ShapeDtypeStruct((M, 2 * D_MODEL), F32)],
        compiler_params=_cp("arbitrary"), name=name)(q, kv, do)


def loss_head(x, g, target, *, tm, name):
    T, D = x.shape

    def body(x_ref, g_ref, t_ref, l_ref, dx_ref, dg_ref):
        i = pl.program_id(0)
        xv = x_ref[...]
        gain = g_ref[...]
        r = lax.rsqrt(jnp.mean(xv * xv, axis=-1, keepdims=True) + EPS)
        xhat = xv * r
        err = xhat * gain - t_ref[...]
        part_l = jnp.full((1, 128), 0.5 / D, F32) * jnp.sum(err * err)
        dy = err * (1.0 / D)
        gd = dy * gain
        dx_ref[...] = r * (gd - xhat * jnp.mean(gd * xhat, axis=-1, keepdims=True))
        part_g = jnp.sum(dy * xhat, axis=0, keepdims=True)

        @pl.when(i == 0)
        def _():
            l_ref[...] = part_l
            dg_ref[...] = part_g

        @pl.when(i > 0)
        def _():
            l_ref[...] += part_l
            dg_ref[...] += part_g

    spec = pl.BlockSpec((tm, D), lambda i: (i, 0))
    return pl.pallas_call(
        body, grid=(T // tm,), in_specs=[spec, pl.BlockSpec((1, D), lambda i: (0, 0)), spec],
        out_specs=[pl.BlockSpec((1, 128), lambda i: (0, 0)), spec, pl.BlockSpec((1, D), lambda i: (0, 0))],
        out_shape=[jax.ShapeDtypeStruct((1, 128), F32), jax.ShapeDtypeStruct((T, D), F32),
                   jax.ShapeDtypeStruct((1, D), F32)],
        compiler_params=_cp("arbitrary"), name=name)(x, g, target)


S5_LS = 128
S5_SEG = 8
S5_TB = S5_LS * S5_SEG
S5_CH = 8
S5_CW = 128
S5_NST = 512


def _cmul(ar, ai, br, bi):
    return ar * br - ai * bi, ar * bi + ai * br


def s5_params_fwd(a_re, a_im, log_dt, b_re, b_im, expand, *, name):
    G, P = a_re.shape

    def body(ar_ref, ai_ref, ld_ref, br_ref, bi_ref, e_ref, abr_ref, abi_ref, apr_ref, api_ref, bbr_ref, bbi_ref):
        ar, ai = ar_ref[...], ai_ref[...]
        dt = jnp.exp(ld_ref[...])
        mag = jnp.exp(dt * ar)
        abr, abi = mag * jnp.cos(dt * ai), mag * jnp.sin(dt * ai)
        den = ar * ar + ai * ai
        zr, zi = abr - 1.0, abi
        fr = (zr * ar + zi * ai) / den
        fi = (zi * ar - zr * ai) / den
        frx, fix = _dot_hi(fr, e_ref[...]), _dot_hi(fi, e_ref[...])
        br, bi = br_ref[...], bi_ref[...]
        bbr_ref[...] = frx * br - fix * bi
        bbi_ref[...] = frx * bi + fix * br
        abr_ref[...] = abr
        abi_ref[...] = abi
        pr, pi = abr, abi
        for _ in range(int(math.log2(S5_LS))):
            pr, pi = _cmul(pr, pi, pr, pi)
        apr_ref[...] = pr
        api_ref[...] = pi

    small = jax.ShapeDtypeStruct((G, P), F32)
    big = jax.ShapeDtypeStruct(b_re.shape, F32)
    return pl.pallas_call(body, out_shape=[small, small, small, small, big, big], name=name)(
        a_re, a_im, log_dt, b_re, b_im, expand)


def s5_params_bwd(a_re, a_im, log_dt, b_re, b_im, g_abr, g_abi, g_bbr, g_bbi, expand, expand_t, *, name):
    G, P = a_re.shape

    def body(ar_ref, ai_ref, ld_ref, br_ref, bi_ref, gar_ref, gai_ref, gbr_ref, gbi_ref, e_ref, et_ref,
             dar_ref, dai_ref, dld_ref, dbr_ref, dbi_ref):
        ar, ai = ar_ref[...], ai_ref[...]
        dt = jnp.exp(ld_ref[...])
        mag = jnp.exp(dt * ar)
        cs, sn = jnp.cos(dt * ai), jnp.sin(dt * ai)
        abr, abi = mag * cs, mag * sn
        den = ar * ar + ai * ai
        zr, zi = abr - 1.0, abi
        fr = (zr * ar + zi * ai) / den
        fi = (zi * ar - zr * ai) / den
        frx, fix = _dot_hi(fr, e_ref[...]), _dot_hi(fi, e_ref[...])
        br, bi = br_ref[...], bi_ref[...]
        gbr, gbi = gbr_ref[...], gbi_ref[...]
        dbr_ref[...] = frx * gbr + fix * gbi
        dbi_ref[...] = -fix * gbr + frx * gbi
        gfr = _dot_hi(br * gbr + bi * gbi, et_ref[...])
        gfi = _dot_hi(-bi * gbr + br * gbi, et_ref[...])
        g_zr = (gfr * ar - gfi * ai) / den
        g_zi = (gfr * ai + gfi * ar) / den
        g_ar = gfr * (zr - fr * 2.0 * ar) / den + gfi * (zi - fi * 2.0 * ar) / den
        g_ai = gfr * (zi - fr * 2.0 * ai) / den + gfi * (-zr - fi * 2.0 * ai) / den
        t_abr = gar_ref[...] + g_zr
        t_abi = gai_ref[...] + g_zi
        g_mag = t_abr * cs + t_abi * sn
        g_th = mag * (-t_abr * sn + t_abi * cs)
        dar_ref[...] = g_ar + g_mag * mag * dt
        dai_ref[...] = g_ai + g_th * dt
        g_dt = jnp.sum(g_mag * mag * ar + g_th * ai, axis=-1, keepdims=True)
        dld_ref[...] = g_dt * dt

    small = jax.ShapeDtypeStruct((G, P), F32)
    big = jax.ShapeDtypeStruct(b_re.shape, F32)
    return pl.pallas_call(body, out_shape=[small, small, jax.ShapeDtypeStruct((G, 1), F32), big, big], name=name)(
        a_re, a_im, log_dt, b_re, b_im, g_abr, g_abi, g_bbr, g_bbi, expand, expand_t)


def _s5_permute_in(src_ref, dst_ref):
    for i in range(S5_LS):
        dst_ref[pl.ds(8 * i, 8), :] = src_ref[pl.ds(i, 8, stride=S5_LS), :]


def _s5_permute_out(src_ref, dst_ref):
    for r in range(S5_SEG):
        for k in range(S5_LS // 8):
            dst_ref[pl.ds(r * S5_LS + 8 * k, 8), :] = src_ref[pl.ds(64 * k + r, 8, stride=8), :]


def _s5_scan(a_r, a_i, dr_ref, di_ref, init_r, init_i, store=None, reverse=False, conj=False):
    sgn = -1.0 if conj else 1.0

    def step(n, c):
        sr, si = c
        i = S5_LS - 1 - n if reverse else n
        nr = a_r * sr - sgn * a_i * si + dr_ref[i]
        ni = a_r * si + sgn * a_i * sr + di_ref[i]
        if store is not None:
            store(i, nr, ni, sr, si)
        return nr, ni

    return lax.fori_loop(0, S5_LS, step, (init_r, init_i), unroll=2)


def _s5_stitch(apr, api, fin_r, fin_i, car_r, car_i, reverse=False, conj=False):
    sgn = -1.0 if conj else 1.0
    rows_r, rows_i = [None] * S5_SEG, [None] * S5_SEG
    order = range(S5_SEG - 1, -1, -1) if reverse else range(S5_SEG)
    for r in order:
        rows_r[r], rows_i[r] = car_r, car_i
        fr, fi = fin_r[r:r + 1], fin_i[r:r + 1]
        car_r, car_i = (apr * car_r - sgn * api * car_i + fr, apr * car_i + sgn * api * car_r + fi)
    return jnp.concatenate(rows_r, 0), jnp.concatenate(rows_i, 0), car_r, car_i


def _s5_specs(nb, rev):
    blk = (lambda c, b: (nb - 1 - b, c)) if rev else (lambda c, b: (b, c))
    tok = pl.BlockSpec((S5_TB, S5_CW), blk)
    par_b = pl.BlockSpec((1, S5_CW, S5_NST), lambda c, b: (c, 0, 0))
    par_c = pl.BlockSpec((1, S5_NST, S5_CW), lambda c, b: (c, 0, 0))
    vec_s = pl.BlockSpec((1, 1, S5_NST), lambda c, b: (c, 0, 0))
    vec_c = pl.BlockSpec((1, 1, S5_CW), lambda c, b: (c, 0, 0))
    return tok, par_b, par_c, vec_s, vec_c


def s5_fwd(proj, bbr, bbi, ccr, cci, abr, abi, apr, api, dskip, *, name, comm=None):
    T = proj.shape[0]
    nb = T // S5_TB
    zeros8 = functools.partial(jnp.zeros, (S5_SEG, S5_NST), F32)

    def body(u_ref, bbr_ref, bbi_ref, ccr_ref, cci_ref, ar_ref, ai_ref, apr_ref, api_ref, d_ref,
             y_ref, csr_ref, csi_ref, uf_ref, up_ref, dr_ref, di_ref, sr_ref, si_ref, yp_ref, car_ref, cai_ref):
        b = pl.program_id(1)

        @pl.when(b == 0)
        def _():
            car_ref[...] = jnp.zeros_like(car_ref)
            cai_ref[...] = jnp.zeros_like(cai_ref)

        csr_ref[0, 0] = car_ref[...]
        csi_ref[0, 0] = cai_ref[...]
        uf_ref[...] = u_ref[...].astype(F32)
        _s5_permute_in(uf_ref, up_ref)
        upb = up_ref[...].astype(BF16)
        dr_ref[...] = _dot(upb, bbr_ref[0]).reshape(S5_LS, S5_SEG, S5_NST)
        di_ref[...] = _dot(upb, bbi_ref[0]).reshape(S5_LS, S5_SEG, S5_NST)
        a_r = jnp.broadcast_to(ar_ref[0], (S5_SEG, S5_NST))
        a_i = jnp.broadcast_to(ai_ref[0], (S5_SEG, S5_NST))
        fin_r, fin_i = _s5_scan(a_r, a_i, dr_ref, di_ref, zeros8(), zeros8())
        cin_r, cin_i, ncr, nci = _s5_stitch(apr_ref[0], api_ref[0], fin_r, fin_i, car_ref[...], cai_ref[...])
        car_ref[...] = ncr
        cai_ref[...] = nci

        def store(i, nr, ni, sr, si):
            sr_ref[i] = nr
            si_ref[i] = ni

        _s5_scan(a_r, a_i, dr_ref, di_ref, cin_r, cin_i, store=store)
        s_r = sr_ref[...].reshape(S5_TB, S5_NST).astype(BF16)
        s_i = si_ref[...].reshape(S5_TB, S5_NST).astype(BF16)
        yp_ref[...] = _dot(s_r, ccr_ref[0]) - _dot(s_i, cci_ref[0]) + d_ref[0] * up_ref[...]
        _s5_permute_out(yp_ref, uf_ref)
        y_ref[...] = uf_ref[...].astype(BF16)

    tok, par_b, par_c, vec_s, vec_c = _s5_specs(nb, False)
    cs_spec = pl.BlockSpec((1, 1, 1, S5_NST), lambda c, b: (b, c, 0, 0))
    cs_shape = jax.ShapeDtypeStruct((nb, S5_CH, 1, S5_NST), F32)
    tokbuf = pltpu.VMEM((S5_TB, S5_CW), F32)
    stbuf = pltpu.VMEM((S5_LS, S5_SEG, S5_NST), F32)
    return _call(
        body, grid=(S5_CH, nb),
        in_specs=[tok, par_b, par_b, par_c, par_c, vec_s, vec_s, vec_s, vec_s, vec_c],
        out_specs=[tok, cs_spec, cs_spec],
        out_shape=[jax.ShapeDtypeStruct((T, D_S5), BF16), cs_shape, cs_shape],
        scratch_shapes=[tokbuf, tokbuf, stbuf, stbuf, stbuf, stbuf, tokbuf,
                        pltpu.VMEM((1, S5_NST), F32), pltpu.VMEM((1, S5_NST), F32)],
        sem=("parallel", "arbitrary"), name=name, comm=comm,
        args=(proj, bbr, bbi, ccr, cci, abr, abi, apr, api, dskip))


def s5_bwd(proj, dys, csr, csi, bbr, bbi, ccr, cci, abr, abi, apr, api, dskip, dproj, *, name, comm=None):
    T = proj.shape[0]
    nb = T // S5_TB
    zeros8 = functools.partial(jnp.zeros, (S5_SEG, S5_NST), F32)

    def body(u_ref, gy_ref, csr_ref, csi_ref, bbr_ref, bbi_ref, ccr_ref, cci_ref, ar_ref, ai_ref, apr_ref, api_ref,
             d_ref, dproj_ref, du_ref, dbr_ref, dbi_ref, dcr_ref, dci_ref, dd_ref, dar_ref, dai_ref,
             tmp_ref, up_ref, gyp_ref, dr_ref, di_ref, sr_ref, si_ref, gr_ref, gi_ref, car_ref, cai_ref):
        b = pl.program_id(1)

        @pl.when(b == 0)
        def _():
            car_ref[...] = jnp.zeros_like(car_ref)
            cai_ref[...] = jnp.zeros_like(cai_ref)
            for ref in (dbr_ref, dbi_ref, dcr_ref, dci_ref, dd_ref, dar_ref, dai_ref):
                ref[...] = jnp.zeros_like(ref)

        tmp_ref[...] = u_ref[...].astype(F32)
        _s5_permute_in(tmp_ref, up_ref)
        tmp_ref[...] = gy_ref[...].astype(F32)
        _s5_permute_in(tmp_ref, gyp_ref)
        upb = up_ref[...].astype(BF16)
        gyp = gyp_ref[...]
        gypb = gyp.astype(BF16)
        dr_ref[...] = _dot(upb, bbr_ref[0]).reshape(S5_LS, S5_SEG, S5_NST)
        di_ref[...] = _dot(upb, bbi_ref[0]).reshape(S5_LS, S5_SEG, S5_NST)
        a_r = jnp.broadcast_to(ar_ref[0], (S5_SEG, S5_NST))
        a_i = jnp.broadcast_to(ai_ref[0], (S5_SEG, S5_NST))
        fin_r, fin_i = _s5_scan(a_r, a_i, dr_ref, di_ref, zeros8(), zeros8())
        cin_r, cin_i, _, _ = _s5_stitch(apr_ref[0], api_ref[0], fin_r, fin_i, csr_ref[0, 0], csi_ref[0, 0])
        sr_ref[0] = cin_r
        si_ref[0] = cin_i

        def store_s(i, nr, ni, sr, si):
            sr_ref[i + 1] = nr
            si_ref[i + 1] = ni

        _s5_scan(a_r, a_i, dr_ref, di_ref, cin_r, cin_i, store=store_s)
        s_r = sr_ref[pl.ds(1, S5_LS)].reshape(S5_TB, S5_NST).astype(BF16)
        s_i = si_ref[pl.ds(1, S5_LS)].reshape(S5_TB, S5_NST).astype(BF16)
        dcr_ref[0] += _dot_tn(s_r, gypb)
        dci_ref[0] -= _dot_tn(s_i, gypb)
        dd_ref[0] += jnp.sum(gyp * up_ref[...], axis=0, keepdims=True)
        dr_ref[...] = _dot_nt(gypb, ccr_ref[0]).reshape(S5_LS, S5_SEG, S5_NST)
        di_ref[...] = (-_dot_nt(gypb, cci_ref[0])).reshape(S5_LS, S5_SEG, S5_NST)
        fin_r, fin_i = _s5_scan(a_r, a_i, dr_ref, di_ref, zeros8(), zeros8(), reverse=True, conj=True)
        gin_r, gin_i, ncr, nci = _s5_stitch(apr_ref[0], api_ref[0], fin_r, fin_i, car_ref[...], cai_ref[...],
                                            reverse=True, conj=True)
        car_ref[...] = ncr
        cai_ref[...] = nci
        def step_g(n, carry):
            gr, gi, acc_r, acc_i = carry
            i = S5_LS - 1 - n
            nr = a_r * gr + a_i * gi + dr_ref[i]
            ni = a_r * gi - a_i * gr + di_ref[i]
            gr_ref[i] = nr
            gi_ref[i] = ni
            pr, pi = sr_ref[i], si_ref[i]
            return nr, ni, acc_r + (nr * pr + ni * pi), acc_i + (ni * pr - nr * pi)

        _, _, acc_r, acc_i = lax.fori_loop(0, S5_LS, step_g, (gin_r, gin_i, zeros8(), zeros8()), unroll=2)
        dar_ref[0] += jnp.sum(acc_r, axis=0, keepdims=True)
        dai_ref[0] += jnp.sum(acc_i, axis=0, keepdims=True)
        g_r = gr_ref[...].reshape(S5_TB, S5_NST).astype(BF16)
        g_i = gi_ref[...].reshape(S5_TB, S5_NST).astype(BF16)
        dbr_ref[0] += _dot_tn(upb, g_r)
        dbi_ref[0] += _dot_tn(upb, g_i)
        gyp_ref[...] = _dot_nt(g_r, bbr_ref[0]) + _dot_nt(g_i, bbi_ref[0]) + d_ref[0] * gyp
        _s5_permute_out(gyp_ref, tmp_ref)
        du_ref[...] = tmp_ref[...].astype(BF16)

    tok, par_b, par_c, vec_s, vec_c = _s5_specs(nb, True)
    cs_spec = pl.BlockSpec((1, 1, 1, S5_NST), lambda c, b: (nb - 1 - b, c, 0, 0))
    tokbuf = pltpu.VMEM((S5_TB, S5_CW), F32)
    stbuf = pltpu.VMEM((S5_LS, S5_SEG, S5_NST), F32)
    stbuf1 = pltpu.VMEM((S5_LS + 1, S5_SEG, S5_NST), F32)
    return _call(
        body, grid=(S5_CH, nb), comm=comm,
        in_specs=[tok, tok, cs_spec, cs_spec, par_b, par_b, par_c, par_c, vec_s, vec_s, vec_s, vec_s, vec_c, _ANY],
        out_specs=[tok, par_b, par_b, par_c, par_c, vec_c, vec_s, vec_s], aliases={13: 0},
        out_shape=[jax.ShapeDtypeStruct(dproj.shape, BF16),
                   jax.ShapeDtypeStruct((S5_CH, S5_CW, S5_NST), F32), jax.ShapeDtypeStruct((S5_CH, S5_CW, S5_NST), F32),
                   jax.ShapeDtypeStruct((S5_CH, S5_NST, S5_CW), F32), jax.ShapeDtypeStruct((S5_CH, S5_NST, S5_CW), F32),
                   jax.ShapeDtypeStruct((S5_CH, 1, S5_CW), F32),
                   jax.ShapeDtypeStruct((S5_CH, 1, S5_NST), F32), jax.ShapeDtypeStruct((S5_CH, 1, S5_NST), F32)],
        scratch_shapes=[tokbuf, tokbuf, tokbuf, stbuf, stbuf, stbuf1, stbuf1, stbuf, stbuf,
                        pltpu.VMEM((1, S5_NST), F32), pltpu.VMEM((1, S5_NST), F32)],
        sem=("parallel", "arbitrary"), name=name,
        args=(proj, dys, csr, csi, bbr, bbi, ccr, cci, abr, abi, apr, api, dskip, dproj))


SSD_L = SSD_CHUNK
SSD_GW = 256
NEG = -1e30


def _expand16(v):
    lane = lax.broadcasted_iota(jnp.int32, (v.shape[0], 128), 1)
    parts = [jnp.where(lane < SSD_HEADDIM, v[:, 2 * j:2 * j + 1], v[:, 2 * j + 1:2 * j + 2]) for j in range(8)]
    return jnp.concatenate(parts, axis=1)


def _headsum(v, hsum):
    hi = v.astype(BF16)
    lo = (v - hi.astype(F32)).astype(BF16)
    return _dot(hi, hsum) + _dot(lo, hsum)


def _softplus(x):
    return jnp.maximum(x, 0.0) + jnp.log(1.0 + jnp.exp(-jnp.abs(x)))


def _ssd_chunk_fwd(z, xbc, tail, dtraw, hprev, cw, cb, dtb, alog, dsk, nw, tril):
    L = SSD_L
    f = {}
    xe = jnp.concatenate([tail, xbc], axis=0)
    sh = [xbc] + [pltpu.roll(xe, s, 0)[8:] for s in (1, 2, 3)]
    conv = cb + cw[3:4] * sh[0] + cw[2:3] * sh[1] + cw[1:2] * sh[2] + cw[0:1] * sh[3]
    sig = _sigmoid(conv)
    xa = conv * sig
    xs, bm, cm = xa[:, :D_SSD], xa[:, D_SSD:D_SSD + 512], xa[:, D_SSD + 512:]
    pre = dtraw + dtb
    dt = _softplus(pre)
    a_h = -jnp.exp(alog)
    acum = _dot_hi(tril, dt * a_h)
    acum_t = acum.T
    alast = acum[L - 1:L]
    exp_a = jnp.exp(acum)
    dec = jnp.exp(alast - acum)
    exp_al = jnp.exp(alast)
    dt_x, dec_x, exp_a_x, exp_al_x = _expand16(dt), _expand16(dec), _expand16(exp_a), _expand16(exp_al)
    d_x = _expand16(dsk)
    xh = xs * dt_x
    xhb = xh.astype(BF16)
    xd = (xh * dec_x).astype(BF16)
    row = lax.broadcasted_iota(jnp.int32, (L, L), 0)
    col = lax.broadcasted_iota(jnp.int32, (L, L), 1)
    causal = row >= col
    lane = lax.broadcasted_iota(jnp.int32, (L, 128), 1)
    low = lane < SSD_HEADDIM
    hb = hprev.astype(BF16)
    y_pairs, yoff_parts, st_parts, cbs, lms = [], [], [], [], []
    for g in range(SSD_NGROUPS):
        bg = bm[:, g * 128:(g + 1) * 128].astype(BF16)
        cg = cm[:, g * 128:(g + 1) * 128].astype(BF16)
        cbg = _dot_nt(cg, bg)
        cbs.append(cbg)
        for j in (2 * g, 2 * g + 1):
            xp = xhb[:, j * 128:(j + 1) * 128]
            ys = []
            for h in (2 * j, 2 * j + 1):
                lm = jnp.exp(jnp.where(causal, acum[:, h:h + 1] - acum_t[h:h + 1, :], NEG))
                lms.append(lm)
                ys.append(_dot((cbg * lm).astype(BF16), xp))
            y_pairs.append(jnp.where(low, ys[0], ys[1]))
        gs = slice(g * SSD_GW, (g + 1) * SSD_GW)
        yoff_parts.append(_dot(cg, hb[:, gs]) * exp_a_x[:, gs])
        st_parts.append(_dot_tn(bg, xd[:, gs]))
    yoff = jnp.concatenate(yoff_parts, axis=1)
    y = jnp.concatenate(y_pairs, axis=1) + yoff + d_x * xs
    hnew = exp_al_x * hprev + jnp.concatenate(st_parts, axis=1)
    sz = _sigmoid(z)
    gz = y * (z * sz)
    r = lax.rsqrt(jnp.mean(gz * gz, axis=-1, keepdims=True) + EPS)
    out = gz * r * nw
    f.update(sh=sh, conv=conv, sig=sig, xs=xs, bm=bm, cm=cm, pre=pre, dt=dt, a_h=a_h, exp_a_x=exp_a_x, dec_x=dec_x,
             exp_al=exp_al, exp_al_x=exp_al_x, dt_x=dt_x, d_x=d_x, xh=xh, xhb=xhb, xd=xd, causal=causal, low=low, hb=hb,
             cbs=cbs, lms=lms, yoff=yoff, y=y, sz=sz, gz=gz, r=r)
    return out, hnew, f


def _ssd_params(conv_w, conv_b, dt_bias, a_log, d_skip, norm_w):
    pad16 = lambda v: jnp.pad(v.reshape(1, SSD_HEADS), ((0, 0), (0, 128 - SSD_HEADS)))
    return (jnp.pad(conv_w, ((0, 8 - SSD_CONV), (0, 0))), conv_b.reshape(1, D_XBC), pad16(dt_bias), pad16(a_log),
            pad16(d_skip), norm_w.reshape(1, D_SSD))


def _ssd_param_specs():
    full = lambda shape: pl.BlockSpec(shape, lambda i: (0, 0))
    return [full((8, D_XBC)), full((1, D_XBC)), full((1, 128)), full((1, 128)), full((1, 128)), full((1, D_SSD))]


def ssd_fwd(proj, dtraw, params, tril, ymix, *, name, comm=None):
    T = proj.shape[0]
    nc = T // SSD_L

    def body(z_ref, x_ref, dt_ref, cw_ref, cb_ref, dtb_ref, al_ref, dsk_ref, nw_ref, tril_ref, ymix_ref,
             o_ref, hs_ref, h_ref, tail_ref):
        i = pl.program_id(0)

        @pl.when(i == 0)
        def _():
            h_ref[...] = jnp.zeros_like(h_ref)
            tail_ref[...] = jnp.zeros_like(tail_ref)

        xbc = x_ref[...].astype(F32)
        hprev = h_ref[...]
        hs_ref[0] = hprev
        out, hnew, _ = _ssd_chunk_fwd(z_ref[...].astype(F32), xbc, tail_ref[...], dt_ref[...], hprev, cw_ref[...],
                                      cb_ref[...], dtb_ref[...], al_ref[...], dsk_ref[...], nw_ref[...], tril_ref[...])
        o_ref[...] = out.astype(BF16)
        h_ref[...] = hnew
        tail_ref[...] = xbc[SSD_L - 8:]

    return _call(
        body, grid=(nc,),
        in_specs=[pl.BlockSpec((SSD_L, D_SSD), lambda i: (i, 1)), pl.BlockSpec((SSD_L, D_XBC), lambda i: (i, 1)),
                  pl.BlockSpec((SSD_L, 128), lambda i: (i, 0))] + _ssd_param_specs()
                 + [pl.BlockSpec((SSD_L, SSD_L), lambda i: (0, 0)), _ANY],
        out_specs=[pl.BlockSpec((SSD_L, D_SSD), lambda i: (i, 1)),
                   pl.BlockSpec((1, SSD_STATE, D_SSD), lambda i: (i, 0, 0))],
        out_shape=[jax.ShapeDtypeStruct(ymix.shape, BF16), jax.ShapeDtypeStruct((nc, SSD_STATE, D_SSD), F32)],
        scratch_shapes=[pltpu.VMEM((SSD_STATE, D_SSD), F32), pltpu.VMEM((8, D_XBC), F32)],
        sem=("arbitrary",), name=name, args=(proj, proj, dtraw, *params, tril, ymix), aliases={10: 0}, comm=comm)


def ssd_bwd(proj, dtraw, hs, dymix, params, tril, triu, trils, headsum, *, name, comm=None):
    T = proj.shape[0]
    nc = T // SSD_L
    L = SSD_L

    def body(z_ref, x_ref, xprev_ref, dt_ref, hs_ref, do_ref, cw_ref, cb_ref, dtb_ref, al_ref, dsk_ref, nw_ref,
             tril_ref, triu_ref, trils_ref, hsum_ref,
             dp_ref, ddt_ref, dcw_ref, dcb_ref, ddtb_ref, dal_ref, ddsk_ref, dnw_ref, dh_ref, dnext_ref):
        i = pl.program_id(0)

        @pl.when(i == 0)
        def _():
            dh_ref[...] = jnp.zeros_like(dh_ref)
            dnext_ref[...] = jnp.zeros_like(dnext_ref)
            for ref in (dcw_ref, dcb_ref, ddtb_ref, dal_ref, ddsk_ref, dnw_ref):
                ref[...] = jnp.zeros_like(ref)

        z = z_ref[...].astype(F32)
        xbc = x_ref[...].astype(F32)
        tail = jnp.where(i == nc - 1, 0.0, xprev_ref[...].astype(F32))
        hprev = hs_ref[0]
        cw, nw = cw_ref[...], nw_ref[...]
        hsum = hsum_ref[...]
        _, _, f = _ssd_chunk_fwd(z, xbc, tail, dt_ref[...], hprev, cw, cb_ref[...], dtb_ref[...], al_ref[...],
                                 dsk_ref[...], nw, tril_ref[...])
        dout = do_ref[...].astype(F32)
        dh = dh_ref[...]
        ghat = f["gz"] * f["r"]
        dn = dout * nw
        dgz = f["r"] * (dn - ghat * jnp.mean(dn * ghat, axis=-1, keepdims=True))
        dnw_ref[...] += jnp.sum(dout * ghat, axis=0, keepdims=True)
        sz = f["sz"]
        dy = dgz * (z * sz)
        dp_ref[:, :D_S5] = jnp.zeros((L, D_S5), BF16)
        dp_ref[:, D_S5:D_S5 + D_SSD] = (dgz * f["y"] * sz * (1.0 + z * (1.0 - sz))).astype(BF16)
        xs = f["xs"]
        ddsk_ref[...] += jnp.sum(_headsum(dy * xs, hsum), axis=0, keepdims=True)
        dyb = dy.astype(BF16)
        dye = (dy * f["exp_a_x"]).astype(BF16)
        dhb = dh.astype(BF16)
        lane = lax.broadcasted_iota(jnp.int32, (L, 128), 1)
        sub = lax.broadcasted_iota(jnp.int32, (128, L), 0)
        zero_b = jnp.zeros((L, 128), BF16)
        rsum = jnp.zeros((L, 128), F32)
        csum_t = jnp.zeros((128, L), F32)
        dx_pairs, dxst_parts, db_parts, dc_parts, dhp_parts = [], [], [], [], []
        for g in range(SSD_NGROUPS):
            gs = slice(g * SSD_GW, (g + 1) * SSD_GW)
            bg = f["bm"][:, g * 128:(g + 1) * 128].astype(BF16)
            cg = f["cm"][:, g * 128:(g + 1) * 128].astype(BF16)
            cbg = f["cbs"][g]
            dcb_g = jnp.zeros((L, L), F32)
            for j in (2 * g, 2 * g + 1):
                xp = f["xhb"][:, j * 128:(j + 1) * 128]
                dyp = dyb[:, j * 128:(j + 1) * 128]
                dxs = []
                for half, h in enumerate((2 * j, 2 * j + 1)):
                    lm = f["lms"][h]
                    dyh = jnp.where(f["low"], dyp, zero_b) if half == 0 else jnp.where(f["low"], zero_b, dyp)
                    dw = jnp.where(f["causal"], _dot_nt(dyh, xp), 0.0)
                    w = cbg * lm
                    e = dw * w
                    dcb_g = dcb_g + dw * lm
                    rsum = jnp.where(lane == h, jnp.sum(e, axis=1, keepdims=True), rsum)
                    csum_t = jnp.where(sub == h, jnp.sum(e, axis=0, keepdims=True), csum_t)
                    dxs.append(_dot_tn(w.astype(BF16), dyp))
                dx_pairs.append(jnp.where(f["low"], dxs[0], dxs[1]))
            dcbb = dcb_g.astype(BF16)
            dxst_parts.append(f["dec_x"][:, gs] * _dot(bg, dhb[:, gs]))
            dc_parts.append(_dot(dcbb, bg) + _dot_nt(dye[:, gs], f["hb"][:, gs]))
            db_parts.append(_dot_tn(dcbb, cg) + _dot_nt(f["xd"][:, gs], dhb[:, gs]))
            dhp_parts.append(f["exp_al_x"][:, gs] * dh[:, gs] + _dot_tn(cg, dye[:, gs]))
        dxst = jnp.concatenate(dxst_parts, axis=1)
        dxh = jnp.concatenate(dx_pairs, axis=1) + dxst
        q = _headsum(f["yoff"] * dy, hsum)
        dstate = _headsum(f["xh"] * dxst, hsum)
        h0t = jnp.sum(_headsum(dh * hprev, hsum), axis=0, keepdims=True) * f["exp_al"]
        da = _dot_hi(triu_ref[...], rsum - csum_t.T + q) + _dot_hi(trils_ref[...], dstate) + h0t
        dt, a_h = f["dt"], f["a_h"]
        ddt = _headsum(dxh * xs, hsum) + da * a_h
        dal_ref[...] += jnp.sum(da * dt, axis=0, keepdims=True) * a_h
        ddtraw = ddt * _sigmoid(f["pre"])
        first16 = lane < SSD_HEADS
        ddtraw = jnp.where(first16, ddtraw, 0.0)
        ddt_ref[...] = ddtraw
        ddtb_ref[...] += jnp.sum(ddtraw, axis=0, keepdims=True)
        dh_ref[...] = jnp.concatenate(dhp_parts, axis=1)
        dxa = jnp.concatenate([dxh * f["dt_x"] + f["d_x"] * dy] + db_parts + dc_parts, axis=1)
        sig, conv = f["sig"], f["conv"]
        dconv = dxa * sig * (1.0 + conv * (1.0 - sig))
        dcb_ref[...] += jnp.sum(dconv, axis=0, keepdims=True)
        rows = [jnp.sum(dconv * f["sh"][3 - k], axis=0, keepdims=True) for k in range(SSD_CONV)]
        dcw_ref[...] += jnp.concatenate(rows + [jnp.zeros((8 - SSD_CONV, D_XBC), F32)], axis=0)
        de = jnp.concatenate([dconv, dnext_ref[...]], axis=0)
        dxbc = cw[3:4] * dconv
        for s in (1, 2, 3):
            dxbc = dxbc + cw[3 - s:4 - s] * pltpu.roll(de, L + 8 - s, 0)[:L]
        dp_ref[:, D_S5 + D_SSD:] = dxbc.astype(BF16)
        dnext_ref[...] = dconv[:8]

    rev = lambda i: nc - 1 - i
    acc = lambda shape: pl.BlockSpec(shape, lambda i: (0, 0))
    tri = pl.BlockSpec((L, L), lambda i: (0, 0))
    return _call(
        body, grid=(nc,),
        in_specs=[pl.BlockSpec((L, D_SSD), lambda i: (rev(i), 1)), pl.BlockSpec((L, D_XBC), lambda i: (rev(i), 1)),
                  pl.BlockSpec((8, D_XBC), lambda i: (jnp.maximum(rev(i) * (L // 8) - 1, 0), 1)),
                  pl.BlockSpec((L, 128), lambda i: (rev(i), 0)),
                  pl.BlockSpec((1, SSD_STATE, D_SSD), lambda i: (rev(i), 0, 0)),
                  pl.BlockSpec((L, D_SSD), lambda i: (rev(i), 1))] + _ssd_param_specs()
                 + [tri, tri, tri, pl.BlockSpec((D_SSD, 128), lambda i: (0, 0))],
        out_specs=[pl.BlockSpec((L, D_MAIN), lambda i: (rev(i), 0)), pl.BlockSpec((L, 128), lambda i: (rev(i), 0)),
                   acc((8, D_XBC)), acc((1, D_XBC)), acc((1, 128)), acc((1, 128)), acc((1, 128)), acc((1, D_SSD))],
        out_shape=[jax.ShapeDtypeStruct((T, D_MAIN), BF16),
                   jax.ShapeDtypeStruct((T, 128), F32), jax.ShapeDtypeStruct((8, D_XBC), F32),
                   jax.ShapeDtypeStruct((1, D_XBC), F32), jax.ShapeDtypeStruct((1, 128), F32),
                   jax.ShapeDtypeStruct((1, 128), F32), jax.ShapeDtypeStruct((1, 128), F32),
                   jax.ShapeDtypeStruct((1, D_SSD), F32)],
        scratch_shapes=[pltpu.VMEM((SSD_STATE, D_SSD), F32), pltpu.VMEM((8, D_XBC), F32)],
        sem=("arbitrary",), name=name, comm=comm,
        args=(proj, proj, proj, dtraw, hs, dymix, *params, tril, triu, trils, headsum))


def _s5_blockdiag(v, rows_per_group, cols_per_group):
    eye = jnp.eye(S5_SEG, dtype=v.dtype)
    w = v[:, :, :, None, :] * eye[None, :, None, :, None]
    return w.reshape(S5_CH, 8 * rows_per_group, 8 * cols_per_group)


def _s5_blockdiag_extract(w, rows_per_group, cols_per_group):
    eye = jnp.eye(S5_SEG, dtype=w.dtype)
    w5 = w.reshape(S5_CH, 8, rows_per_group, 8, cols_per_group)
    return jnp.sum(w5 * eye[None, :, None, :, None], axis=3)


TM = 512

OFF = dict(mlp_w2=0, mlp_w1=1024, w_out=2048, s5_w_glu=2560, xa_wq=2816, xa_wk=3072, xa_wv=3328, xa_wo=3584, w_in=3840)
ROWS = dict(mlp_w2=1024, mlp_w1=1024, w_out=512, s5_w_glu=256, xa_wq=256, xa_wk=256, xa_wv=256, xa_wo=256, w_in=1028)
TAIL_OFF = 4872
TAIL_ROWS = 120
PACK_ROWS = TAIL_OFF + TAIL_ROWS
ROWS_ALL = (0, PACK_ROWS, 832, True)
ROWS_EARLY = (0, OFF["w_in"], 640, False)
ROWS_LATE = (OFF["w_in"], PACK_ROWS - OFF["w_in"], 192, True)
N_SHARD = 4
SMALL_L = ("norm_mix", "s5_a_re", "s5_a_im", "s5_log_dt", "s5_b_re", "s5_b_im", "s5_c_re", "s5_c_im", "s5_d",
           "ssd_conv_w", "ssd_conv_b", "ssd_dt_bias", "ssd_a_log", "ssd_d", "ssd_norm", "norm_xattn", "norm_mem",
           "norm_mlp")
SMALL_Q = 72
CONV_ROWS = 8


def _place():
    x, y, c = lax.axis_index("x"), lax.axis_index("y"), lax.axis_index("c")
    chips = [(1 - x, y), (x, 1 - y), (1 - x, 1 - y)]
    return x, y, c, 2 * x + y, chips, (x, y, 1 - c)


def _remote(src, dst, send_sem, recv_sem, to):
    return pltpu.make_async_remote_copy(src_ref=src, dst_ref=dst, send_sem=send_sem, recv_sem=recv_sem,
                                        device_id=to, device_id_type=MESH_T)


def _dma_sems(*counts):
    return [pltpu.SemaphoreType.DMA((n,)) for n in counts]


def place_own(wpack, shard_idx):
    tile = PACK_ROWS // 4

    def body(s_ref, w_ref, o_ref):
        o_ref[0] = w_ref[...]

    return pl.pallas_call(
        body,
        grid_spec=pltpu.PrefetchScalarGridSpec(
            num_scalar_prefetch=1, grid=(4,),
            in_specs=[pl.BlockSpec((tile, D_MODEL), lambda i, s_ref: (i, 0))],
            out_specs=pl.BlockSpec((1, tile, D_MODEL), lambda i, s_ref: (s_ref[0], i, 0))),
        out_shape=jax.ShapeDtypeStruct((N_SHARD,) + wpack.shape, wpack.dtype),
        compiler_params=_cp("parallel"), name="place_own")(shard_idx, wpack)


def _range_half(ref, rows, c):
    half = rows[1] // 2
    return pl.ds(pl.multiple_of(rows[0] + c * half, 16), half)


def gather_over_ici(wpack, placed, rows=ROWS_ALL):
    def copies(ci, co, cs):
        w_ref, (out_ref,), (send, recv) = ci[0], co, cs
        x, y, c, s, chips, sibling = _place()
        mine = _range_half(w_ref, rows, c)
        sends = [_remote(w_ref.at[mine, :], out_ref.at[s, mine, :], send.at[j], recv.at[j], (*chip, c))
                 for j, chip in enumerate(chips)]
        lands = [out_ref.at[2 * chip[0] + chip[1], mine, :] for chip in chips]
        arrivals = [_remote(land, land, send.at[j], recv.at[j], sibling) for j, land in enumerate(lands)]
        return sends, arrivals

    def start(ci, co, cs):
        for cp in copies(ci, co, cs)[0]:
            cp.start()

    def wait(ci, co, cs):
        sends, arrivals = copies(ci, co, cs)
        for cp in arrivals:
            cp.wait_recv()
        for cp in sends:
            cp.wait_send()

    return Comm("gather_ici", [wpack, placed], [jax.ShapeDtypeStruct(placed.shape, placed.dtype)], _dma_sems(3, 3),
                start, wait, aliases={1: 0})


def gather_to_sibling(landed, rows=ROWS_ALL):
    def copies(co, cs):
        (out_ref,), (send, recv) = co, cs
        x, y, c, s, chips, sibling = _place()
        slots = [2 * chip[0] + chip[1] for chip in chips]
        mine, other = _range_half(out_ref, rows, c), _range_half(out_ref, rows, 1 - c)
        sends = [_remote(out_ref.at[t, mine, :], out_ref.at[t, mine, :], send.at[j], recv.at[j], sibling)
                 for j, t in enumerate(slots)]
        arrivals = [_remote(out_ref.at[t, other, :], out_ref.at[t, other, :], send.at[j], recv.at[j], sibling)
                    for j, t in enumerate(slots)]
        return sends, arrivals

    def start(ci, co, cs):
        for cp in copies(co, cs)[0]:
            cp.start()

    def wait(ci, co, cs):
        sends, arrivals = copies(co, cs)
        for cp in arrivals:
            cp.wait_recv()
        for cp in sends:
            cp.wait_send()

    return Comm("gather_d2d", [landed], [jax.ShapeDtypeStruct(landed.shape, landed.dtype)], _dma_sems(3, 3), start, wait,
                aliases={0: 0})


def exchange_halves(gpack, rows):
    def copy(ci, co, cs):
        (g_ref,), (out_ref,), (send, recv) = ci, co, cs
        x, y, c, s, chips, sibling = _place()
        return _remote(g_ref.at[:, _range_half(g_ref, rows, 1 - c), :], out_ref, send.at[0], recv.at[0], sibling)

    return Comm("exchange", [gpack], [jax.ShapeDtypeStruct((N_SHARD, rows[1] // 2, D_MODEL), F32)], _dma_sems(1, 1),
                lambda ci, co, cs: copy(ci, co, cs).start(), lambda ci, co, cs: copy(ci, co, cs).wait())


def scatter_chips(csum):
    def copies(ci, co, cs):
        (c_ref,), (out_ref,), (send, recv) = ci, co, cs
        x, y, c, s, chips, sibling = _place()
        sends = [_remote(c_ref.at[2 * chip[0] + chip[1]], out_ref.at[s], send.at[j], recv.at[j], (*chip, c))
                 for j, chip in enumerate(chips)]
        arrivals = [_remote(c_ref.at[2 * chip[0] + chip[1]], out_ref.at[2 * chip[0] + chip[1]], send.at[j], recv.at[j],
                            (*chip, c)) for j, chip in enumerate(chips)]
        return sends, arrivals

    def start(ci, co, cs):
        for cp in copies(ci, co, cs)[0]:
            cp.start()

    def wait(ci, co, cs):
        sends, arrivals = copies(ci, co, cs)
        for cp in arrivals:
            cp.wait_recv()
        for cp in sends:
            cp.wait_send()

    return Comm("scatter", [csum], [jax.ShapeDtypeStruct(csum.shape, csum.dtype)], _dma_sems(3, 3), start, wait)


def share_reduced(gshards, smalls, l, rows):
    with_tail = rows[3]

    def copies(ci, co, cs):
        (g_ref, sm_ref), (send, recv, loc) = co, cs
        x, y, c, s, chips, sibling = _place()
        my_half = g_ref.at[l, _range_half(g_ref, rows, c), :]
        tail = g_ref.at[l, pl.ds(PACK_ROWS - TAIL_ROWS, TAIL_ROWS), :]
        big = _remote(my_half, my_half, send.at[0], recv.at[0], sibling)
        keep_tail = pltpu.make_async_copy(tail, sm_ref.at[l, s], loc.at[0])
        tails = [_remote(tail, sm_ref.at[l, s], send.at[1 + j], recv.at[1 + j], (*chip, 1)) for j, chip in enumerate(chips)]
        tails += [_remote(tail, sm_ref.at[l, s], send.at[4 + j], recv.at[1 + j], (*chip, 0)) for j, chip in enumerate(chips)]
        tails.append(_remote(tail, sm_ref.at[l, s], send.at[7], recv.at[4], sibling))
        other = g_ref.at[l, _range_half(g_ref, rows, 1 - c), :]
        big_in = _remote(other, other, send.at[0], recv.at[0], sibling)
        slots = [sm_ref.at[l, 2 * chip[0] + chip[1]] for chip in chips]
        tails_in = [_remote(slot, slot, send.at[1 + j], recv.at[1 + j], sibling) for j, slot in enumerate(slots)]
        sib_tail_in = _remote(sm_ref.at[l, s], sm_ref.at[l, s], send.at[7], recv.at[4], sibling)
        return c, big, keep_tail, tails, big_in, tails_in, sib_tail_in

    def start(ci, co, cs):
        c, big, keep_tail, tails, _, _, _ = copies(ci, co, cs)
        big.start()
        if with_tail:
            @pl.when(c == 1)
            def _():
                keep_tail.start()
                for cp in tails:
                    cp.start()

    def wait(ci, co, cs):
        c, big, keep_tail, tails, big_in, tails_in, sib_tail_in = copies(ci, co, cs)
        big_in.wait_recv()
        big.wait_send()
        if with_tail:
            for cp in tails_in:
                cp.wait_recv()

            @pl.when(c == 0)
            def _():
                sib_tail_in.wait_recv()

            @pl.when(c == 1)
            def _():
                for cp in tails:
                    cp.wait_send()
                keep_tail.wait()

    sds = lambda a: jax.ShapeDtypeStruct(a.shape, a.dtype)
    return Comm("share", [gshards, smalls], [sds(gshards), sds(smalls)], _dma_sems(8, 5, 1), start, wait,
                aliases={0: 0, 1: 1})


def _consts():
    e = np.zeros((S5_STATE, S5_STATE * S5_GROUP), np.float32)
    for p in range(S5_STATE):
        e[p, p * S5_GROUP:(p + 1) * S5_GROUP] = 1.0
    hs = np.zeros((D_SSD, 128), np.float32)
    for h in range(SSD_HEADS):
        hs[h * SSD_HEADDIM:(h + 1) * SSD_HEADDIM, h] = 1.0
    ones = np.ones((SSD_L, SSD_L), np.float32)
    return dict(expand=jnp.asarray(e), expand_t=jnp.asarray(e.T), headsum=jnp.asarray(hs).astype(BF16),
                tril=jnp.asarray(np.tril(ones)), triu=jnp.asarray(np.triu(ones)), trils=jnp.asarray(np.tril(ones, -1)))


def _s5_mats(w, cst):
    b_re = w["s5_b_re"].reshape(S5_GROUPS, S5_STATE * S5_GROUP)
    b_im = w["s5_b_im"].reshape(S5_GROUPS, S5_STATE * S5_GROUP)
    abr, abi, apr, api, bbr, bbi = s5_params_fwd(w["s5_a_re"], w["s5_a_im"], w["s5_log_dt"].reshape(S5_GROUPS, 1),
                                                 b_re, b_im, cst["expand"], name="s5_params_fwd")
    t = lambda v: v.reshape(S5_CH, 8, S5_STATE, S5_GROUP).transpose(0, 1, 3, 2)
    c4 = lambda v: v.reshape(S5_CH, 8, S5_GROUP, S5_STATE).transpose(0, 1, 3, 2)
    vec = lambda v: v.reshape(S5_CH, 1, S5_NST)
    return dict(bbr=_s5_blockdiag(t(bbr), S5_GROUP, S5_STATE).astype(BF16),
                bbi=_s5_blockdiag(t(bbi), S5_GROUP, S5_STATE).astype(BF16),
                ccr=_s5_blockdiag(c4(w["s5_c_re"]), S5_STATE, S5_GROUP).astype(BF16),
                cci=_s5_blockdiag(c4(w["s5_c_im"]), S5_STATE, S5_GROUP).astype(BF16),
                abr=vec(abr), abi=vec(abi), apr=vec(apr), api=vec(api),
                dsk=w["s5_d"].reshape(S5_CH, 1, S5_CW), b_re=b_re, b_im=b_im)


def _layer_fwd(x, mem, w, cst, next_pack=None, rest=None):
    sv = {}
    g = lambda n: w[n].reshape(1, -1)
    res = norm_matmul(x, g("norm_mix"), w["w_in_t"], w["w_dt_t"], tm=TM, tn=1024, name="in_proj", w_transposed=True,
                      comm=gather_over_ici(rest[0], rest[1], ROWS_EARLY) if rest else None)
    (proj, h0, r0, dtraw), staged = res if rest else (res, None)
    s5m = _s5_mats(w, cst)
    comm = _combine(gather_to_sibling(staged[0], ROWS_EARLY) if rest else None,
                    gather_over_ici(*next_pack) if next_pack else None)
    (ys, csr, csi), landed = s5_fwd(proj, s5m["bbr"], s5m["bbi"], s5m["ccr"], s5m["cci"], s5m["abr"], s5m["abi"],
                                    s5m["apr"], s5m["api"], s5m["dsk"], name="s5_fwd", comm=comm)
    if rest:
        w = dict(w, **_pack_weights(landed[0]))
        landed = landed[1:]
    pack = w["pack"]
    (ymix, tglu), gathered = glu_fwd(ys, w["s5_w_glu"], tm=TM, name="glu_fwd",
                                     comm=None if next_pack is None else gather_to_sibling(landed[0]))
    ssdp = _ssd_params(w["ssd_conv_w"], w["ssd_conv_b"], w["ssd_dt_bias"], w["ssd_a_log"], w["ssd_d"], w["ssd_norm"])
    (ymix, hs), _ = ssd_fwd(proj, dtraw, ssdp, cst["tril"], ymix, name="ssd_fwd")
    x1 = matmul_res(ymix, pack, x, tm=TM, tn=1024, tk=1024, name="out_proj",
                    wspec=pl.BlockSpec((2, 512, 1024), lambda i, j, k: (k, OFF["w_out"] // 512, j)))
    q, h1, r1 = norm_matmul(x1, g("norm_xattn"), w["xa_wq"], tm=TM, tn=1024, name="q_proj")
    kv, hm, rm = norm_matmul(mem, g("norm_mem"), w["xa_wkv"], tm=mem.shape[0], tn=1024, name="kv_proj")
    o = attn_fwd(q, kv, tm=TM, name="attn_fwd")
    x2 = matmul_res(o, w["xa_wo"], x1, tm=TM, tn=1024, tk=1024, name="attn_out")
    f1, h2, r2 = norm_matmul(x2, g("norm_mlp"), pack, tm=TM, tn=1024, name="mlp_up", n_out=D_FF,
                             wspec=pl.BlockSpec((1, D_MODEL, 1024), lambda i, j: (j, OFF["mlp_w1"] // 1024, 0)))
    x3 = matmul_res(f1, pack, x2, act="relu2", tm=TM, tn=1024, tk=1024, name="mlp_down",
                    wspec=pl.BlockSpec((1, 1024, 1024), lambda i, j, k: (k, OFF["mlp_w2"] // 1024, j)))
    sv.update(x=x, proj=proj, h0=h0, r0=r0, dtraw=dtraw, s5m=s5m, ys=ys, csr=csr, csi=csi, tglu=tglu, ssdp=ssdp,
              hs=hs, ymix=ymix, x1=x1, q=q, h1=h1, r1=r1, kv=kv, hm=hm, rm=rm, o=o, x2=x2, f1=f1, h2=h2, r2=r2)
    return x3, sv, (gathered[0] if next_pack is not None else None), w


def _layer_bwd(dx3, mem, w, sv, cst, extra_small=None, reduce_hooks=None, early=None):
    gr = {}
    g = lambda n: w[n].reshape(1, -1)
    pack = w["pack"]
    pshape = (N_SHARD, PACK_ROWS, D_MODEL)
    ps = lambda rows, f: pl.BlockSpec((1, rows, 1024), f)
    ps4 = lambda rows, f: pl.BlockSpec((N_SHARD, rows, 1024), f)
    dh1 = matmul_nt(dx3, pack, epi="relu2bwd", epi_args=(sv["f1"],), tm=TM, tko=1024, tn=1024, name="mlp_down_dx",
                    wspec=ps(1024, lambda i, k, n: (k, OFF["mlp_w2"] // 1024, n)), k_out=D_FF,
                    comm=reduce_hooks.exchange() if reduce_hooks else None)
    if reduce_hooks:
        dh1, recv = dh1
        reduce_hooks.after_exchange(recv[0])
    gp = matmul_tn(sv["f1"], dx3, act="relu2", tk=1024, tn=1024, tt=TM, name="mlp_down_dw", pack_shape=pshape,
                   pack_spec=ps(1024, lambda k, n, t: (k, OFF["mlp_w2"] // 1024, 0)))
    gp = matmul_tn(sv["h2"], dh1, tk=1024, tn=1024, tt=TM, name="mlp_up_dw", pack=gp, pack_shape=pshape,
                   pack_spec=ps(1024, lambda k, n, t: (n, OFF["mlp_w1"] // 1024, 0)))
    dx2, gr["norm_mlp"] = matmul_nt(dh1, pack, epi="rmsbwd", epi_args=(sv["x2"], sv["r2"], g("norm_mlp"), dx3),
                                    tm=TM, tko=1024, tn=1024, name="mlp_up_dx", k_out=D_MODEL,
                                    wspec=ps(1024, lambda i, k, n: (n, OFF["mlp_w1"] // 1024, 0)))
    do = matmul_nt(dx2, w["xa_wo"], tm=TM, tko=1024, tn=1024, name="attn_out_dx")
    gp = matmul_tn(sv["o"], dx2, tk=1024, tn=1024, tt=TM, name="attn_out_dw", pack=gp, pack_shape=pshape,
                   pack_spec=ps4(256, lambda k, n, t: (0, OFF["xa_wo"] // 256, 0)))
    dq, dkv = attn_bwd(sv["q"], sv["kv"], do, tm=TM, name="attn_bwd")
    gp = matmul_tn(sv["h1"], dq, tk=1024, tn=1024, tt=TM, name="q_proj_dw", pack=gp, pack_shape=pshape,
                   pack_spec=ps4(256, lambda k, n, t: (0, OFF["xa_wq"] // 256, 0)))
    dx1, gr["norm_xattn"] = matmul_nt(dq, w["xa_wq"], epi="rmsbwd", epi_args=(sv["x1"], sv["r1"], g("norm_xattn"), dx2),
                                      tm=TM, tko=1024, tn=1024, name="q_proj_dx")
    M = mem.shape[0]
    gp = matmul_tn(sv["hm"], dkv, tk=1024, tn=1024, tt=M, name="kv_proj_dw", pack=gp, pack_shape=pshape,
                   pack_spec=ps4(256, lambda k, n, t: (0, OFF["xa_wk"] // 256 + n, 0)))
    _, gr["norm_mem"] = matmul_nt(dkv, w["xa_wkv"], epi="rmsbwd",
                                  epi_args=(mem, sv["rm"], g("norm_mem"), jnp.zeros_like(mem)),
                                  tm=M, tko=1024, tn=1024, name="kv_proj_dx")
    dymix = matmul_nt(dx1, pack, tm=TM, tko=1024, tn=1024, name="out_proj_dx", k_out=2 * D_MODEL,
                      wspec=pl.BlockSpec((2, 512, 1024), lambda i, k, n: (k, OFF["w_out"] // 512, n)))
    gp = matmul_tn(sv["ymix"], dx1, tk=2048, tn=1024, tt=TM, name="out_proj_dw", pack=gp, pack_shape=pshape,
                   pack_spec=ps4(512, lambda k, n, t: (0, OFF["w_out"] // 512, 0)))
    (dproj, ddtraw, dcw, dcb, ddtb, dal, ddsk, dnw), parts = ssd_bwd(
        sv["proj"], sv["dtraw"], sv["hs"], dymix, sv["ssdp"], cst["tril"], cst["triu"], cst["trils"], cst["headsum"],
        name="ssd_bwd", comm=reduce_hooks.scatter() if reduce_hooks else None)
    if reduce_hooks:
        reduce_hooks.after_scatter(parts[0])
    gr["ssd_conv_w"], gr["ssd_conv_b"] = dcw[:SSD_CONV], dcb[0]
    gr["ssd_dt_bias"], gr["ssd_a_log"], gr["ssd_d"] = ddtb[0, :SSD_HEADS], dal[0, :SSD_HEADS], ddsk[0, :SSD_HEADS]
    gr["ssd_norm"] = dnw[0]
    gp = glu_dw(dymix, sv["ys"], sv["tglu"], gp, tt=TM, name="glu_dw",
                pack_spec=ps4(256, lambda i: (0, OFF["s5_w_glu"] // 256, 0)))
    first = _Reduction(gp, early[0], early[1], None, None, ROWS_EARLY) if early else None
    comm = _combine(reduce_hooks.share() if reduce_hooks else None, first.exchange() if first else None)
    (dys,), outs = glu_bwd(dymix, sv["ys"], sv["tglu"], w["s5_w_glu"], tm=TM, name="glu_dx", comm=comm)
    if comm is not None:
        if reduce_hooks:
            reduce_hooks.after_share(outs[:2])
            outs = outs[2:]
        if first:
            first.gshards, first.smalls = reduce_hooks.gshards, reduce_hooks.smalls
            first.after_exchange(outs[0])
    s5m = sv["s5m"]
    (dproj, dbbr, dbbi, dccr, dcci, dd, dabr, dabi), parts = s5_bwd(
        sv["proj"], dys, sv["csr"], sv["csi"], s5m["bbr"], s5m["bbi"], s5m["ccr"], s5m["cci"], s5m["abr"], s5m["abi"],
        s5m["apr"], s5m["api"], s5m["dsk"], dproj, name="s5_bwd", comm=first.scatter() if first else None)
    if first:
        first.after_scatter(parts[0])
    tb = lambda v: _s5_blockdiag_extract(v, S5_GROUP, S5_STATE).transpose(0, 1, 3, 2).reshape(S5_GROUPS, -1)
    tc = lambda v: _s5_blockdiag_extract(v, S5_STATE, S5_GROUP).transpose(0, 1, 3, 2).reshape(S5_GROUPS, S5_GROUP, S5_STATE)
    gr["s5_c_re"], gr["s5_c_im"] = tc(dccr), tc(dcci)
    gr["s5_d"] = dd.reshape(S5_GROUPS, S5_GROUP)
    dar, dai, dld, dbr, dbi = s5_params_bwd(
        w["s5_a_re"], w["s5_a_im"], w["s5_log_dt"].reshape(S5_GROUPS, 1), s5m["b_re"], s5m["b_im"],
        dabr.reshape(S5_GROUPS, S5_STATE), dabi.reshape(S5_GROUPS, S5_STATE), tb(dbbr), tb(dbbi),
        cst["expand"], cst["expand_t"], name="s5_params_bwd")
    gr["s5_a_re"], gr["s5_a_im"], gr["s5_log_dt"] = dar, dai, dld[:, 0]
    gr["s5_b_re"] = dbr.reshape(S5_GROUPS, S5_STATE, S5_GROUP)
    gr["s5_b_im"] = dbi.reshape(S5_GROUPS, S5_STATE, S5_GROUP)
    wt_shape = (D_MAIN + D_DT_PAD, D_MODEL)
    dwt = matmul_tn(dproj, sv["h0"], tk=1024, tn=1024, tt=TM, name="in_proj_dw", pack_shape=wt_shape,
                    pack_spec=pl.BlockSpec((1024, 1024), lambda k, n, t: (k, 0)))
    dwt = matmul_tn(ddtraw, sv["h0"], tk=D_DT_PAD, tn=1024, tt=TM, name="in_proj_dt_dw", pack=dwt, pack_shape=wt_shape,
                    pack_spec=pl.BlockSpec((D_DT_PAD, 1024), lambda k, n, t: (D_MAIN // D_DT_PAD, 0)))
    dx0 = matmul_nt(dproj, w["w_in_t"], g2=ddtraw, w2=w["w_dt_t"], w_is_nk=True, epi="rmsbwd",
                    epi_args=(sv["x"], sv["r0"], g("norm_mix"), dx1), tm=TM, tko=1024, tn=1024, name="in_proj_dx",
                    comm=first.share() if first else None)
    if first:
        dx0, shared = dx0
        first.after_share(shared)
    dx0, gr["norm_mix"] = dx0
    gr = {k: (v[0] if k.startswith("norm_") else v) for k, v in gr.items()}
    for t in range(N_SHARD):
        shard_rows = lax.slice_in_dim(dwt, t * ROWS["w_in"], (t + 1) * ROWS["w_in"], axis=0)
        gp = lax.dynamic_update_slice(gp, shard_rows[None], (t, OFF["w_in"], 0))
    small = [gr[n].reshape(-1) for n in SMALL_L] + ([] if extra_small is None else [extra_small.reshape(-1)])
    small = jnp.concatenate(small)
    small = jnp.pad(small, (0, N_SHARD * SMALL_Q * D_MODEL - small.size)).reshape(N_SHARD, SMALL_Q, D_MODEL)
    gap = TAIL_OFF - OFF["w_in"] - ROWS["w_in"]
    gp = lax.dynamic_update_slice(gp, jnp.pad(small, ((0, 0), (gap, TAIL_ROWS - SMALL_Q), (0, 0))),
                                  (0, TAIL_OFF - gap, 0))
    return dx0, gp, first


def _local_step(x, mem, target, layers, norm_final):
    cst = _consts()
    saved = []
    for l in range(DEPTH):
        x, sv, _, _ = _layer_fwd(x, mem, layers[l], cst)
        saved.append(sv)
    loss, dx, dgf = loss_head(x, norm_final.reshape(1, -1), target, tm=TM, name="loss_head")
    packs = [None] * DEPTH
    for l in reversed(range(DEPTH)):
        dx, packs[l], _ = _layer_bwd(dx, mem, layers[l], saved[l], cst, extra_small=dgf[0] if l == DEPTH - 1 else None)
    return loss, dx, packs


def sum_halves(gpack, recv, c_idx, rows):
    first, count, tile, _ = rows
    half = count // 2
    nb = half // tile

    def body(c_ref, a_ref, b_ref, o_ref):
        o_ref[...] = (a_ref[...] + b_ref[...]).astype(BF16)

    blk = (1, tile, D_MODEL)
    return pl.pallas_call(
        body,
        grid_spec=pltpu.PrefetchScalarGridSpec(
            num_scalar_prefetch=1, grid=(N_SHARD, nb),
            in_specs=[pl.BlockSpec(blk, lambda t, i, c_ref: (t, first // tile + c_ref[0] * nb + i, 0)),
                      pl.BlockSpec(blk, lambda t, i, c_ref: (t, i, 0))],
            out_specs=pl.BlockSpec(blk, lambda t, i, c_ref: (t, i, 0))),
        out_shape=jax.ShapeDtypeStruct((N_SHARD, half, D_MODEL), BF16),
        compiler_params=_cp("parallel", "parallel"), name="sum_halves")(c_idx, gpack, recv)


def sum_chips(parts, csum, gshards, l, place_idx, rows):
    first, count, tile, _ = rows
    nb = count // 2 // tile

    def body(pi_ref, p0, p1, p2, p3, own, g_ref, o_ref):
        s = pi_ref[0]
        vals = [jnp.where(s == k, own[0], p[0]).astype(F32) for k, p in enumerate((p0, p1, p2, p3))]
        o_ref[0] = ((vals[0] + vals[1]) + vals[2]) + vals[3]

    blk = (1, tile, D_MODEL)
    part_spec = lambda k: pl.BlockSpec(blk, lambda i, pi_ref: (jnp.where(pi_ref[0] == k, (k + 1) % N_SHARD, k), i, 0))
    return pl.pallas_call(
        body,
        grid_spec=pltpu.PrefetchScalarGridSpec(
            num_scalar_prefetch=1, grid=(nb,),
            in_specs=[part_spec(k) for k in range(N_SHARD)]
                     + [pl.BlockSpec(blk, lambda i, pi_ref: (pi_ref[0], i, 0)), _ANY],
            out_specs=pl.BlockSpec(blk, lambda i, pi_ref: (l, first // tile + pi_ref[1] * nb + i, 0))),
        out_shape=jax.ShapeDtypeStruct(gshards.shape, F32), input_output_aliases={6: 0},
        compiler_params=_cp("parallel"), name="sum_chips")(place_idx, parts, parts, parts, parts, csum, gshards)


class _Reduction:
    def __init__(self, gpack, layer, place_idx, gshards, smalls, rows=ROWS_ALL):
        self.gpack, self.layer, self.place_idx, self.gshards, self.smalls = gpack, layer, place_idx, gshards, smalls
        self.rows = rows

    def exchange(self):
        return exchange_halves(self.gpack, self.rows)

    def after_exchange(self, recv):
        self.csum = sum_halves(self.gpack, recv, self.place_idx[1:], self.rows)

    def scatter(self):
        return scatter_chips(self.csum)

    def after_scatter(self, parts):
        self.gshards = sum_chips(parts, self.csum, self.gshards, self.layer, self.place_idx, self.rows)

    def share(self):
        return share_reduced(self.gshards, self.smalls, self.layer, self.rows)

    def after_share(self, shared):
        self.gshards, self.smalls = shared

    def run_alone(self):
        self.after_exchange(_comm_only(self.exchange())[0])
        self.after_scatter(_comm_only(self.scatter())[0])
        self.after_share(_comm_only(self.share()))
        return self.gshards, self.smalls


def adamw(w, g, m, v, *, name):
    shape = w.shape
    cols = shape[-1]
    rows = w.size // cols
    tr = 512 if rows % 512 == 0 else rows
    c1 = 1.0 / (1.0 - ADAM_B1 ** ADAM_STEP)
    c2 = 1.0 / (1.0 - ADAM_B2 ** ADAM_STEP)

    def body(w_ref, g_ref, m_ref, v_ref, d_ref, nm_ref, nv_ref):
        gv = g_ref[...]
        nm = ADAM_B1 * m_ref[...] + (1.0 - ADAM_B1) * gv
        nv = ADAM_B2 * v_ref[...] + (1.0 - ADAM_B2) * (gv * gv)
        d_ref[...] = -ADAM_LR * ((nm * c1) / (jnp.sqrt(nv * c2) + ADAM_EPS) + ADAM_WD * w_ref[...])
        nm_ref[...] = nm
        nv_ref[...] = nv

    spec = pl.BlockSpec((tr, cols), lambda i: (i, 0))
    sds = jax.ShapeDtypeStruct((rows, cols), F32)
    outs = pl.pallas_call(body, grid=(rows // tr,), in_specs=[spec] * 4, out_specs=[spec] * 3, out_shape=[sds] * 3,
                          compiler_params=_cp("parallel"), name=name)(
                              *[a.reshape(rows, cols) for a in (w, g, m, v)])
    return [o.reshape(shape) for o in outs]


def _own_pack(wts, l):
    rows = [wts[n][l].T if n == "w_in" else wts[n][l].reshape(ROWS[n], D_MODEL) for n in sorted(OFF, key=OFF.get)]
    cw = wts["ssd_conv_w"][l].reshape(-1)
    hi = lax.reduce_precision(cw, 8, 7)
    mid = lax.reduce_precision(cw - hi, 8, 7)
    lo = lax.reduce_precision(cw - hi - mid, 8, 7)
    conv = jnp.pad(jnp.concatenate([hi, mid, lo]), (0, CONV_ROWS * D_MODEL - 3 * cw.size)).reshape(CONV_ROWS, D_MODEL)
    gap = jnp.zeros((TAIL_OFF - OFF["w_in"] - ROWS["w_in"], D_MODEL), F32)
    rest = jnp.zeros((TAIL_ROWS - CONV_ROWS, D_MODEL), F32)
    return jnp.concatenate(rows + [gap, conv, rest], axis=0).astype(BF16)


def _square(gathered, n):
    return gathered[:, OFF[n]:OFF[n] + ROWS[n]].reshape(N_SHARD * ROWS[n], D_MODEL)


def _pack_weights(gathered):
    w = {"pack": gathered}
    w["s5_w_glu"], w["xa_wq"], w["xa_wo"] = _square(gathered, "s5_w_glu"), _square(gathered, "xa_wq"), _square(gathered, "xa_wo")
    w["xa_wkv"] = jnp.concatenate([_square(gathered, "xa_wk"), _square(gathered, "xa_wv")], axis=1)
    return w


def _layer_weights(gathered, wts, l, late_only=False):
    w = {n: wts[n][l] for n in SMALL_L if n != "ssd_conv_w"}
    if not late_only:
        w.update(_pack_weights(gathered))
    w_in_t = _square(gathered, "w_in")
    w["w_in_t"] = w_in_t[:D_MAIN]
    w["w_dt_t"] = jnp.pad(w_in_t[D_MAIN:], ((0, D_DT_PAD - SSD_HEADS), (0, 0)))
    per = SSD_CONV * D_XBC // N_SHARD
    cw = gathered[:, TAIL_OFF:TAIL_OFF + CONV_ROWS].astype(F32).reshape(N_SHARD, -1)[:, :3 * per]
    cw = cw.reshape(N_SHARD, 3, SSD_CONV, D_XBC // N_SHARD)
    cw = (cw[:, 0] + cw[:, 1]) + cw[:, 2]
    w["ssd_conv_w"] = cw.transpose(1, 0, 2).reshape(SSD_CONV, D_XBC)
    return w


def kernel(x, mem, norm_mix, w_in, s5_a_re, s5_a_im, s5_log_dt, s5_b_re, s5_b_im, s5_c_re, s5_c_im, s5_d, s5_w_glu, ssd_conv_w, ssd_conv_b, ssd_dt_bias, ssd_a_log, ssd_d, ssd_norm, w_out, norm_xattn, norm_mem, xa_wq, xa_wk, xa_wv, xa_wo, norm_mlp, mlp_w1, mlp_w2, norm_final, loss_target, m_norm_mix, m_w_in, m_s5_a_re, m_s5_a_im, m_s5_log_dt, m_s5_b_re, m_s5_b_im, m_s5_c_re, m_s5_c_im, m_s5_d, m_s5_w_glu, m_ssd_conv_w, m_ssd_conv_b, m_ssd_dt_bias, m_ssd_a_log, m_ssd_d, m_ssd_norm, m_w_out, m_norm_xattn, m_norm_mem, m_xa_wq, m_xa_wk, m_xa_wv, m_xa_wo, m_norm_mlp, m_mlp_w1, m_mlp_w2, m_norm_final, v_norm_mix, v_w_in, v_s5_a_re, v_s5_a_im, v_s5_log_dt, v_s5_b_re, v_s5_b_im, v_s5_c_re, v_s5_c_im, v_s5_d, v_s5_w_glu, v_ssd_conv_w, v_ssd_conv_b, v_ssd_dt_bias, v_ssd_a_log, v_ssd_d, v_ssd_norm, v_w_out, v_norm_xattn, v_norm_mem, v_xa_wq, v_xa_wk, v_xa_wv, v_xa_wo, v_norm_mlp, v_mlp_w1, v_mlp_w2, v_norm_final):
    names = ("norm_mix", "w_in", "s5_a_re", "s5_a_im", "s5_log_dt", "s5_b_re", "s5_b_im", "s5_c_re", "s5_c_im", "s5_d",
             "s5_w_glu", "ssd_conv_w", "ssd_conv_b", "ssd_dt_bias", "ssd_a_log", "ssd_d", "ssd_norm", "w_out",
             "norm_xattn", "norm_mem", "xa_wq", "xa_wk", "xa_wv", "xa_wo", "norm_mlp", "mlp_w1", "mlp_w2", "norm_final")
    loc = locals()
    wts = {n: loc[n] for n in names}
    mom = {n: loc["m_" + n] for n in names}
    var = {n: loc["v_" + n] for n in names}
    shard = 2 * lax.axis_index("x") + lax.axis_index("y")
    core = lax.axis_index("c")

    cst = _consts()
    place_idx = jnp.stack([shard, core]).astype(jnp.int32)
    h, mem0 = x[0], mem[0]

    own0 = _own_pack(wts, 0)
    staged = _comm_only(gather_over_ici(own0, place_own(own0, place_idx[:1]), ROWS_LATE))[0]
    gathered = _comm_only(gather_to_sibling(staged, ROWS_LATE))[0]
    layers, saved = [], []
    for l in range(DEPTH):
        nxt = None
        if l + 1 < DEPTH:
            own = _own_pack(wts, l + 1)
            nxt = (own, place_own(own, place_idx[:1]))
        h, sv, gathered, w = _layer_fwd(h, mem0, _layer_weights(gathered, wts, l, late_only=l == 0), cst, next_pack=nxt,
                                        rest=(own0, gathered) if l == 0 else None)
        layers.append(w)
        saved.append(sv)
    loss, dx, dgf = loss_head(h, norm_final.reshape(1, -1), loss_target[0], tm=TM, name="loss_head")

    gshards = jnp.zeros((DEPTH, PACK_ROWS, D_MODEL), F32)
    smalls = jnp.zeros((DEPTH, N_SHARD, TAIL_ROWS, D_MODEL), F32)
    pending = None
    for l in reversed(range(DEPTH)):
        dx, gpack, first = _layer_bwd(dx, mem0, layers[l], saved[l], cst, extra_small=dgf[0] if l == DEPTH - 1 else None,
                                      reduce_hooks=pending, early=(l, place_idx) if l == 0 else None)
        if pending is not None:
            gshards, smalls = pending.gshards, pending.smalls
        pending = _Reduction(gpack, l, place_idx, gshards, smalls)
    gshards, smalls = _Reduction(gpack, 0, place_idx, first.gshards, first.smalls, ROWS_LATE).run_alone()

    g = {n: gshards[:, OFF[n]:OFF[n] + ROWS[n]].reshape(wts[n].shape) for n in OFF if n != "w_in"}
    g["w_in"] = gshards[:, OFF["w_in"]:OFF["w_in"] + ROWS["w_in"]].transpose(0, 2, 1)
    small_red = smalls[:, :, :SMALL_Q].reshape(DEPTH, -1)
    off = 0
    for n in SMALL_L:
        shape = (SSD_CONV, D_XBC) if n == "ssd_conv_w" else wts[n].shape[1:]
        size = math.prod(shape)
        g[n] = small_red[:, off:off + size].reshape((DEPTH,) + shape)
        off += size
    g["norm_final"] = small_red[DEPTH - 1, off:off + D_MODEL]
    g["ssd_conv_w"] = lax.dynamic_slice_in_dim(g["ssd_conv_w"], shard * (D_XBC // N_SHARD), D_XBC // N_SHARD, axis=2)

    deltas, new_m, new_v = {}, {}, {}
    for n in names:
        deltas[n], new_m[n], new_v[n] = adamw(wts[n], g[n], mom[n], var[n], name="adamw_" + n)
    loss_all = lax.psum(loss[0, 0], ("x", "y", "c"))
    return (loss_all, dx[None], *[g[n] for n in names], *[deltas[n] for n in names], *[new_m[n] for n in names],
            *[new_v[n] for n in names])
```

```python
import functools
import math

import jax
import jax.numpy as jnp
import numpy as np
from jax import lax
from jax.experimental import pallas as pl
from jax.experimental.pallas import tpu as pltpu

F32 = jnp.float32
BF16 = jnp.bfloat16
HIGHEST = lax.Precision.HIGHEST

D_MODEL = 1024
DEPTH = 4
D_S5 = 1024
D_SSD = 1024
S5_GROUP = 16
S5_GROUPS = 64
S5_STATE = 64
SSD_HEADS = 16
SSD_HEADDIM = 64
SSD_NGROUPS = 4
SSD_STATE = 128
SSD_CONV = 4
SSD_CHUNK = 128
D_XBC = 2048
D_MAIN = 4096
D_DT_PAD = 128
XA_HEADS = 4
XA_HEAD_DIM = 256
D_FF = 4096
EPS = 1e-5
ADAM_LR, ADAM_B1, ADAM_B2, ADAM_EPS, ADAM_WD, ADAM_STEP = 0.001, 0.9, 0.999, 1e-08, 0.01, 10

VMEM_LIMIT = 56 * 1024 * 1024
MESH_T = pl.DeviceIdType.MESH


def _cp(*sem):
    return pltpu.CompilerParams(dimension_semantics=tuple(sem) if sem else None, vmem_limit_bytes=VMEM_LIMIT)


_ANY = pl.BlockSpec(memory_space=pl.ANY)


class Comm:
    def __init__(self, name, inputs, out_shapes, sems, start, wait, aliases=None):
        self.name, self.inputs, self.out_shapes, self.sems = name, list(inputs), list(out_shapes), list(sems)
        self.start, self.wait, self.aliases = start, wait, dict(aliases or {})


def _call(body, *, grid, in_specs, out_specs, out_shape, args, scratch_shapes=(), sem, name, comm=None, aliases=None):
    in_specs, out_specs, out_shape = list(in_specs), list(out_specs), list(out_shape)
    scratch_shapes = list(scratch_shapes)
    aliases = dict(aliases or {})
    if comm is None:
        res = pl.pallas_call(body, grid=grid, in_specs=in_specs, out_specs=out_specs, out_shape=out_shape,
                             scratch_shapes=scratch_shapes, compiler_params=_cp(*sem), name=name,
                             input_output_aliases=aliases)(*args)
        return list(res), []
    n_in, n_out, n_scr = len(in_specs), len(out_specs), len(scratch_shapes)
    c_in, c_out = len(comm.inputs), len(comm.out_shapes)

    def wrapped(*refs):
        a, refs = refs[:n_in], refs[n_in:]
        ci, refs = refs[:c_in], refs[c_in:]
        o, refs = refs[:n_out], refs[n_out:]
        co, refs = refs[:c_out], refs[c_out:]
        s, cs = refs[:n_scr], refs[n_scr:]
        first = functools.reduce(jnp.logical_and, [pl.program_id(d) == 0 for d in range(len(grid))])
        last = functools.reduce(jnp.logical_and, [pl.program_id(d) == grid[d] - 1 for d in range(len(grid))])

        @pl.when(first)
        def _():
            comm.start(ci, co, cs)

        body(*a, *o, *s)

        @pl.when(last)
        def _():
            comm.wait(ci, co, cs)

    for i, j in comm.aliases.items():
        aliases[n_in + i] = n_out + j
    res = pl.pallas_call(wrapped, grid=grid, in_specs=in_specs + [_ANY] * c_in, out_specs=out_specs + [_ANY] * c_out,
                         out_shape=out_shape + comm.out_shapes, scratch_shapes=scratch_shapes + comm.sems,
                         compiler_params=_cp(*(("arbitrary",) * len(grid))), name=name + "_" + comm.name,
                         input_output_aliases=aliases)(*args, *comm.inputs)
    return list(res[:n_out]), list(res[n_out:])


def _combine(a, b):
    if a is None or b is None:
        return a or b
    ni, no, ns = len(a.inputs), len(a.out_shapes), len(a.sems)

    def start(ci, co, cs):
        a.start(ci[:ni], co[:no], cs[:ns])
        b.start(ci[ni:], co[no:], cs[ns:])

    def wait(ci, co, cs):
        a.wait(ci[:ni], co[:no], cs[:ns])
        b.wait(ci[ni:], co[no:], cs[ns:])

    aliases = dict(a.aliases)
    aliases.update({ni + i: no + j for i, j in b.aliases.items()})
    return Comm(a.name + "_" + b.name, a.inputs + b.inputs, a.out_shapes + b.out_shapes, a.sems + b.sems, start, wait,
                aliases)


def _comm_only(comm):
    def body(*refs):
        ci, refs = refs[:len(comm.inputs)], refs[len(comm.inputs):]
        co, cs = refs[:len(comm.out_shapes)], refs[len(comm.out_shapes):]
        comm.start(ci, co, cs)
        comm.wait(ci, co, cs)

    res = pl.pallas_call(body, in_specs=[_ANY] * len(comm.inputs), out_specs=[_ANY] * len(comm.out_shapes),
                         out_shape=comm.out_shapes, scratch_shapes=comm.sems, name=comm.name,
                         input_output_aliases=comm.aliases)(*comm.inputs)
    return list(res)


def _dot(a, b):
    return jnp.dot(a, b, preferred_element_type=F32)


def _dot_nt(a, b):
    return lax.dot_general(a, b, (((1,), (1,)), ((), ())), preferred_element_type=F32)


def _dot_tn(a, b):
    return lax.dot_general(a, b, (((0,), (0,)), ((), ())), preferred_element_type=F32)


def _dot_hi(a, b):
    return jnp.dot(a, b, precision=HIGHEST, preferred_element_type=F32)


def _gelu(x):
    c = math.sqrt(2.0 / math.pi)
    return 0.5 * x * (1.0 + jnp.tanh(c * (x + 0.044715 * x * x * x)))


def _gelu_grad(x):
    c = math.sqrt(2.0 / math.pi)
    t = jnp.tanh(c * (x + 0.044715 * x * x * x))
    return 0.5 * (1.0 + t) + 0.5 * x * (1.0 - t * t) * c * (1.0 + 3 * 0.044715 * x * x)


def _sigmoid(x):
    return 1.0 / (1.0 + jnp.exp(-x))


def _act(a, act):
    if act is None:
        return a.astype(BF16)
    a = a.astype(F32)
    if act == "relu2":
        a = jnp.maximum(a, 0.0)
        return (a * a).astype(BF16)
    if act == "gelu":
        return _gelu(a).astype(BF16)
    raise ValueError(act)


def _pack_block(ref):
    return ref[...].reshape(-1, ref.shape[-1])


def norm_matmul(x, g, w, w2=None, *, tm, tn, name, wspec=None, n_out=None, w_transposed=False, comm=None):
    T, D = x.shape
    N = n_out if wspec is not None else (w.shape[0] if w_transposed else w.shape[1])
    wget = (lambda r: r[...]) if wspec is None else _pack_block
    mm = _dot_nt if w_transposed else _dot
    has2 = w2 is not None

    def body(x_ref, g_ref, w_ref, *rest):
        if has2:
            w2_ref, o_ref, h_ref, r_ref, o2_ref = rest
        else:
            o_ref, h_ref, r_ref = rest
        j = pl.program_id(1)

        @pl.when(j == 0)
        def _():
            xv = x_ref[...]
            r = lax.rsqrt(jnp.mean(xv * xv, axis=-1, keepdims=True) + EPS)
            h = (xv * r * g_ref[...]).astype(BF16)
            h_ref[...] = h
            r_ref[...] = r
            if has2:
                o2_ref[...] = mm(h, w2_ref[...])

        o_ref[...] = mm(h_ref[...], wget(w_ref)).astype(o_ref.dtype)

    if wspec is None:
        wspec = pl.BlockSpec((tn, D), lambda i, j: (j, 0)) if w_transposed else pl.BlockSpec((D, tn), lambda i, j: (0, j))
    in_specs = [pl.BlockSpec((tm, D), lambda i, j: (i, 0)), pl.BlockSpec((1, D), lambda i, j: (0, 0)), wspec]
    out_shape = [jax.ShapeDtypeStruct((T, N), BF16), jax.ShapeDtypeStruct((T, D), BF16),
                 jax.ShapeDtypeStruct((T, 1), F32)]
    out_specs = [pl.BlockSpec((tm, tn), lambda i, j: (i, j)), pl.BlockSpec((tm, D), lambda i, j: (i, 0)),
                 pl.BlockSpec((tm, 1), lambda i, j: (i, 0))]
    args = [x, g, w]
    if has2:
        in_specs.append(pl.BlockSpec(w2.shape, lambda i, j: (0, 0)))
        out_shape.append(jax.ShapeDtypeStruct((T, D_DT_PAD), F32))
        out_specs.append(pl.BlockSpec((tm, D_DT_PAD), lambda i, j: (i, 0)))
        args.append(w2)
    res, cres = _call(body, grid=(T // tm, N // tn), in_specs=in_specs, out_specs=out_specs, out_shape=out_shape,
                      sem=("parallel", "arbitrary"), name=name, args=args, comm=comm)
    return res if comm is None else (res, cres)


def matmul_res(a, w, r, *, act=None, tm, tn, tk, name, wspec=None):
    T, K = a.shape
    N = r.shape[1]
    wget = (lambda r_: r_[...]) if wspec is None else _pack_block
    nk = K // tk

    def body(a_ref, w_ref, r_ref, o_ref):
        k = pl.program_id(2)

        @pl.when(k == 0)
        def _():
            o_ref[...] = r_ref[...]

        o_ref[...] += _dot(_act(a_ref[...], act), wget(w_ref))

    return pl.pallas_call(
        body, grid=(T // tm, N // tn, nk),
        in_specs=[pl.BlockSpec((tm, tk), lambda i, j, k: (i, k)),
                  pl.BlockSpec((tk, tn), lambda i, j, k: (k, j)) if wspec is None else wspec,
                  pl.BlockSpec((tm, tn), lambda i, j, k: (i, j))],
        out_specs=pl.BlockSpec((tm, tn), lambda i, j, k: (i, j)),
        out_shape=jax.ShapeDtypeStruct((T, N), F32),
        compiler_params=_cp("parallel", "parallel", "arbitrary"), name=name)(a, w, r)


def glu_fwd(ys, w, *, tm, name, comm=None):
    T, N = ys.shape

    def body(y_ref, w_ref, o_ref, t_ref):
        a = _gelu(y_ref[...].astype(F32))
        t = _dot(a.astype(BF16), w_ref[...])
        o_ref[...] = (a * _sigmoid(t)).astype(BF16)
        t_ref[...] = t.astype(BF16)

    return _call(
        body, grid=(T // tm,),
        in_specs=[pl.BlockSpec((tm, N), lambda i: (i, 0)), pl.BlockSpec((N, N), lambda i: (0, 0))],
        out_specs=[pl.BlockSpec((tm, N), lambda i: (i, 0)), pl.BlockSpec((tm, N), lambda i: (i, 0))],
        out_shape=[jax.ShapeDtypeStruct((T, 2 * N), BF16), jax.ShapeDtypeStruct((T, N), BF16)],
        sem=("parallel",), name=name, args=(ys, w), comm=comm)


def glu_bwd(dymix, ys, t, w, *, tm, name, comm=None):
    T, N = ys.shape

    def body(d_ref, y_ref, t_ref, w_ref, dys_ref):
        d = d_ref[...].astype(F32)
        ysv = y_ref[...].astype(F32)
        s = _sigmoid(t_ref[...].astype(F32))
        dt = (d * _gelu(ysv) * s * (1.0 - s)).astype(BF16)
        dys_ref[...] = ((d * s + _dot_nt(dt, w_ref[...])) * _gelu_grad(ysv)).astype(BF16)

    spec = pl.BlockSpec((tm, N), lambda i: (i, 0))
    return _call(body, grid=(T // tm,), in_specs=[spec, spec, spec, pl.BlockSpec((N, N), lambda i: (0, 0))],
                 out_specs=[spec], out_shape=[jax.ShapeDtypeStruct((T, N), BF16)], sem=("parallel",), name=name,
                 args=(dymix, ys, t, w), comm=comm)


def glu_dw(dymix, ys, t, pack, *, pack_spec, tt, name):
    T, N = ys.shape

    def body(d_ref, y_ref, t_ref, pack_ref, o_ref):
        i = pl.program_id(0)
        a = _gelu(y_ref[...].astype(F32))
        s = _sigmoid(t_ref[...].astype(F32))
        dt = (d_ref[...].astype(F32) * a * s * (1.0 - s)).astype(BF16)
        part = _dot_tn(a.astype(BF16), dt).reshape(o_ref.shape)

        @pl.when(i == 0)
        def _():
            o_ref[...] = part

        @pl.when(i > 0)
        def _():
            o_ref[...] += part

    spec = pl.BlockSpec((tt, N), lambda i: (i, 0))
    return pl.pallas_call(body, grid=(T // tt,), in_specs=[spec, spec, spec, _ANY], out_specs=pack_spec,
                          out_shape=jax.ShapeDtypeStruct(pack.shape, F32), input_output_aliases={3: 0},
                          compiler_params=_cp("arbitrary"), name=name)(dymix, ys, t, pack)


def matmul_nt(g, w, *, epi=None, epi_args=(), g2=None, w2=None, tm, tko, tn, out_dtype=BF16, name, wspec=None,
              k_out=None, comm=None, w_is_nk=False):
    T, N = g.shape
    K = k_out if wspec is not None else (w.shape[1] if w_is_nk else w.shape[0])
    wget = (lambda r_: r_[...]) if wspec is None else _pack_block
    mm = _dot if w_is_nk else _dot_nt
    nn = N // tn
    has2 = g2 is not None
    rms = epi == "rmsbwd"
    if rms:
        assert tko == K
    n_epi = len(epi_args)

    def body(*refs):
        g_ref, w_ref = refs[0], refs[1]
        pos = 2
        if has2:
            g2_ref, w2_ref = refs[2], refs[3]
            pos = 4
        e_refs = refs[pos:pos + n_epi]
        pos += n_epi
        o_ref = refs[pos]
        pos += 1
        if rms:
            dg_ref = refs[pos]
            pos += 1
        acc_ref = refs[pos]
        i = pl.program_id(0)
        n = pl.program_id(2)
        part = mm(g_ref[...].astype(BF16), wget(w_ref))

        @pl.when(n == 0)
        def _():
            acc_ref[...] = part

        @pl.when(n > 0)
        def _():
            acc_ref[...] += part

        @pl.when(n == nn - 1)
        def _():
            acc = acc_ref[...]
            if has2:
                acc = acc + mm(g2_ref[...].astype(BF16), w2_ref[...])
            if epi is None:
                o_ref[...] = acc.astype(o_ref.dtype)
            elif epi == "relu2bwd":
                h1 = e_refs[0][...].astype(F32)
                o_ref[...] = (acc * 2.0 * jnp.maximum(h1, 0.0)).astype(o_ref.dtype)
            elif epi == "glubwd":
                da1 = e_refs[0][...].astype(F32)
                ys = e_refs[1][...].astype(F32)
                o_ref[...] = ((da1 + acc) * _gelu_grad(ys)).astype(o_ref.dtype)
            elif epi == "rmsbwd":
                xv, rs, gain, rv = e_refs[0][...], e_refs[1][...], e_refs[2][...], e_refs[3][...]
                xhat = xv * rs
                gd = acc * gain
                o_ref[...] = rv + rs * (gd - xhat * jnp.mean(gd * xhat, axis=-1, keepdims=True))
                part_g = jnp.sum(acc * xhat, axis=0, keepdims=True)

                @pl.when(i == 0)
                def _():
                    dg_ref[...] = part_g

                @pl.when(i > 0)
                def _():
                    dg_ref[...] += part_g

    if wspec is None:
        wspec = (pl.BlockSpec((tn, tko), lambda i, k, n: (n, k)) if w_is_nk
                 else pl.BlockSpec((tko, tn), lambda i, k, n: (k, n)))
    in_specs = [pl.BlockSpec((tm, tn), lambda i, k, n: (i, n)), wspec]
    args = [g, w]
    if has2:
        n2 = g2.shape[1]
        in_specs += [pl.BlockSpec((tm, n2), lambda i, k, n: (i, 0)),
                     pl.BlockSpec((n2, tko), lambda i, k, n: (0, k)) if w_is_nk
                     else pl.BlockSpec((tko, n2), lambda i, k, n: (k, 0))]
        args += [g2, w2]
    if epi == "relu2bwd" or epi == "glubwd":
        in_specs += [pl.BlockSpec((tm, tko), lambda i, k, n: (i, k))] * n_epi
    elif rms:
        in_specs += [pl.BlockSpec((tm, K), lambda i, k, n: (i, 0)), pl.BlockSpec((tm, 1), lambda i, k, n: (i, 0)),
                     pl.BlockSpec((1, K), lambda i, k, n: (0, 0)), pl.BlockSpec((tm, K), lambda i, k, n: (i, 0))]
    args += list(epi_args)
    out_shape = [jax.ShapeDtypeStruct((T, K), F32 if rms else out_dtype)]
    out_specs = [pl.BlockSpec((tm, tko), lambda i, k, n: (i, k))]
    if rms:
        out_shape.append(jax.ShapeDtypeStruct((1, K), F32))
        out_specs.append(pl.BlockSpec((1, K), lambda i, k, n: (0, 0)))
    sem = ("arbitrary",) * 3 if rms else ("parallel", "parallel", "arbitrary")
    res, cres = _call(body, grid=(T // tm, K // tko, nn), in_specs=in_specs, out_specs=out_specs, out_shape=out_shape,
                      scratch_shapes=[pltpu.VMEM((tm, tko), F32)], sem=sem, name=name, args=args, comm=comm)
    res = res if rms else res[0]
    return res if comm is None else (res, cres)


def matmul_tn(a, g, *, act=None, tk, tn, tt, name, pack=None, pack_spec=None, pack_shape=None):
    T, K = a.shape
    N = g.shape[1]
    to_pack = pack_spec is not None

    def body(a_ref, g_ref, *rest):
        o_ref = rest[-1]
        t = pl.program_id(2)
        part = _dot_tn(_act(a_ref[...], act), g_ref[...].astype(BF16))
        part = part.reshape(o_ref.shape)

        @pl.when(t == 0)
        def _():
            o_ref[...] = part

        @pl.when(t > 0)
        def _():
            o_ref[...] += part

    in_specs = [pl.BlockSpec((tt, tk), lambda k, n, t: (t, k)), pl.BlockSpec((tt, tn), lambda k, n, t: (t, n))]
    args = [a, g]
    aliases = {}
    if pack is not None:
        in_specs.append(_ANY)
        args.append(pack)
        aliases = {2: 0}
    return pl.pallas_call(
        body, grid=(K // tk, N // tn, T // tt), in_specs=in_specs,
        out_specs=pack_spec if to_pack else pl.BlockSpec((tk, tn), lambda k, n, t: (k, n)),
        out_shape=jax.ShapeDtypeStruct(pack_shape if to_pack else (K, N), F32), input_output_aliases=aliases,
        compiler_params=_cp("parallel", "parallel", "arbitrary"), name=name)(*args)


def attn_fwd(q, kv, *, tm, name):
    T = q.shape[0]
    M = kv.shape[0]
    scale = XA_HEAD_DIM ** -0.5

    def body(q_ref, kv_ref, o_ref):
        for h in range(XA_HEADS):
            sl = slice(h * XA_HEAD_DIM, (h + 1) * XA_HEAD_DIM)
            kh = kv_ref[:, h * XA_HEAD_DIM:(h + 1) * XA_HEAD_DIM]
            vh = kv_ref[:, D_MODEL + h * XA_HEAD_DIM:D_MODEL + (h + 1) * XA_HEAD_DIM]
            s = _dot_nt(q_ref[:, sl], kh) * scale
            s = s - jnp.max(s, axis=-1, keepdims=True)
            p = jnp.exp(s)
            p = p / jnp.sum(p, axis=-1, keepdims=True)
            o_ref[:, sl] = _dot(p.astype(BF16), vh).astype(BF16)

    return pl.pallas_call(
        body, grid=(T // tm,),
        in_specs=[pl.BlockSpec((tm, D_MODEL), lambda i: (i, 0)), pl.BlockSpec((M, 2 * D_MODEL), lambda i: (0, 0))],
        out_specs=pl.BlockSpec((tm, D_MODEL), lambda i: (i, 0)),
        out_shape=jax.ShapeDtypeStruct((T, D_MODEL), BF16),
        compiler_params=_cp("parallel"), name=name)(q, kv)


def attn_bwd(q, kv, do, *, tm, name):
    T = q.shape[0]
    M = kv.shape[0]
    scale = XA_HEAD_DIM ** -0.5

    def body(q_ref, kv_ref, do_ref, dq_ref, dkv_ref):
        i = pl.program_id(0)

        @pl.when(i == 0)
        def _():
            dkv_ref[...] = jnp.zeros_like(dkv_ref)

        for h in range(XA_HEADS):
            sl = slice(h * XA_HEAD_DIM, (h + 1) * XA_HEAD_DIM)
            slv = slice(D_MODEL + h * XA_HEAD_DIM, D_MODEL + (h + 1) * XA_HEAD_DIM)
            qh = q_ref[:, sl]
            kh = kv_ref[:, sl]
            vh = kv_ref[:, slv]
            doh = do_ref[:, sl]
            s = _dot_nt(qh, kh) * scale
            s = s - jnp.max(s, axis=-1, keepdims=True)
            p = jnp.exp(s)
            p = p / jnp.sum(p, axis=-1, keepdims=True)
            pb = p.astype(BF16)
            dkv_ref[:, slv] += _dot_tn(pb, doh)
            dp = _dot_nt(doh, vh)
            ds = (p * (dp - jnp.sum(dp * p, axis=-1, keepdims=True)) * scale).astype(BF16)
            dq_ref[:, sl] = _dot(ds, kh).astype(BF16)
            dkv_ref[:, sl] += _dot_tn(ds, qh)

    spec = pl.BlockSpec((tm, D_MODEL), lambda i: (i, 0))
    kvspec = pl.BlockSpec((M, 2 * D_MODEL), lambda i: (0, 0))
    return pl.pallas_call(
        body, grid=(T // tm,), in_specs=[spec, kvspec, spec], out_specs=[spec, kvspec],
        out_shape=[jax.ShapeDtypeStruct((T, D_MODEL), BF16), jax.ShapeDtypeStruct((M, 2 * D_MODEL), F32)],
        compiler_params=_cp("arbitrary"), name=name)(q, kv, do)


def loss_head(x, g, target, *, tm, name):
    T, D = x.shape

    def body(x_ref, g_ref, t_ref, l_ref, dx_ref, dg_ref):
        i = pl.program_id(0)
        xv = x_ref[...]
        gain = g_ref[...]
        r = lax.rsqrt(jnp.mean(xv * xv, axis=-1, keepdims=True) + EPS)
        xhat = xv * r
        err = xhat * gain - t_ref[...]
        part_l = jnp.full((1, 128), 0.5 / D, F32) * jnp.sum(err * err)
        dy = err * (1.0 / D)
        gd = dy * gain
        dx_ref[...] = r * (gd - xhat * jnp.mean(gd * xhat, axis=-1, keepdims=True))
        part_g = jnp.sum(dy * xhat, axis=0, keepdims=True)

        @pl.when(i == 0)
        def _():
            l_ref[...] = part_l
            dg_ref[...] = part_g

        @pl.when(i > 0)
        def _():
            l_ref[...] += part_l
            dg_ref[...] += part_g

    spec = pl.BlockSpec((tm, D), lambda i: (i, 0))
    return pl.pallas_call(
        body, grid=(T // tm,), in_specs=[spec, pl.BlockSpec((1, D), lambda i: (0, 0)), spec],
        out_specs=[pl.BlockSpec((1, 128), lambda i: (0, 0)), spec, pl.BlockSpec((1, D), lambda i: (0, 0))],
        out_shape=[jax.ShapeDtypeStruct((1, 128), F32), jax.ShapeDtypeStruct((T, D), F32),
                   jax.ShapeDtypeStruct((1, D), F32)],
        compiler_params=_cp("arbitrary"), name=name)(x, g, target)


S5_LS = 128
S5_SEG = 8
S5_TB = S5_LS * S5_SEG
S5_CH = 8
S5_CW = 128
S5_NST = 512
S5_UNROLL = 4


def _cmul(ar, ai, br, bi):
    return ar * br - ai * bi, ar * bi + ai * br


def s5_params_fwd(a_re, a_im, log_dt, b_re, b_im, expand, *, name):
    G, P = a_re.shape

    def body(ar_ref, ai_ref, ld_ref, br_ref, bi_ref, e_ref, abr_ref, abi_ref, apr_ref, api_ref, bbr_ref, bbi_ref):
        ar, ai = ar_ref[...], ai_ref[...]
        dt = jnp.exp(ld_ref[...])
        mag = jnp.exp(dt * ar)
        abr, abi = mag * jnp.cos(dt * ai), mag * jnp.sin(dt * ai)
        den = ar * ar + ai * ai
        zr, zi = abr - 1.0, abi
        fr = (zr * ar + zi * ai) / den
        fi = (zi * ar - zr * ai) / den
        frx, fix = _dot_hi(fr, e_ref[...]), _dot_hi(fi, e_ref[...])
        br, bi = br_ref[...], bi_ref[...]
        bbr_ref[...] = frx * br - fix * bi
        bbi_ref[...] = frx * bi + fix * br
        abr_ref[...] = abr
        abi_ref[...] = abi
        pr, pi = abr, abi
        for _ in range(int(math.log2(S5_LS))):
            pr, pi = _cmul(pr, pi, pr, pi)
        apr_ref[...] = pr
        api_ref[...] = pi

    small = jax.ShapeDtypeStruct((G, P), F32)
    big = jax.ShapeDtypeStruct(b_re.shape, F32)
    return pl.pallas_call(body, out_shape=[small, small, small, small, big, big], name=name)(
        a_re, a_im, log_dt, b_re, b_im, expand)


def s5_params_bwd(a_re, a_im, log_dt, b_re, b_im, g_abr, g_abi, g_bbr, g_bbi, expand, expand_t, *, name):
    G, P = a_re.shape

    def body(ar_ref, ai_ref, ld_ref, br_ref, bi_ref, gar_ref, gai_ref, gbr_ref, gbi_ref, e_ref, et_ref,
             dar_ref, dai_ref, dld_ref, dbr_ref, dbi_ref):
        ar, ai = ar_ref[...], ai_ref[...]
        dt = jnp.exp(ld_ref[...])
        mag = jnp.exp(dt * ar)
        cs, sn = jnp.cos(dt * ai), jnp.sin(dt * ai)
        abr, abi = mag * cs, mag * sn
        den = ar * ar + ai * ai
        zr, zi = abr - 1.0, abi
        fr = (zr * ar + zi * ai) / den
        fi = (zi * ar - zr * ai) / den
        frx, fix = _dot_hi(fr, e_ref[...]), _dot_hi(fi, e_ref[...])
        br, bi = br_ref[...], bi_ref[...]
        gbr, gbi = gbr_ref[...], gbi_ref[...]
        dbr_ref[...] = frx * gbr + fix * gbi
        dbi_ref[...] = -fix * gbr + frx * gbi
        gfr = _dot_hi(br * gbr + bi * gbi, et_ref[...])
        gfi = _dot_hi(-bi * gbr + br * gbi, et_ref[...])
        g_zr = (gfr * ar - gfi * ai) / den
        g_zi = (gfr * ai + gfi * ar) / den
        g_ar = gfr * (zr - fr * 2.0 * ar) / den + gfi * (zi - fi * 2.0 * ar) / den
        g_ai = gfr * (zi - fr * 2.0 * ai) / den + gfi * (-zr - fi * 2.0 * ai) / den
        t_abr = gar_ref[...] + g_zr
        t_abi = gai_ref[...] + g_zi
        g_mag = t_abr * cs + t_abi * sn
        g_th = mag * (-t_abr * sn + t_abi * cs)
        dar_ref[...] = g_ar + g_mag * mag * dt
        dai_ref[...] = g_ai + g_th * dt
        g_dt = jnp.sum(g_mag * mag * ar + g_th * ai, axis=-1, keepdims=True)
        dld_ref[...] = g_dt * dt

    small = jax.ShapeDtypeStruct((G, P), F32)
    big = jax.ShapeDtypeStruct(b_re.shape, F32)
    return pl.pallas_call(body, out_shape=[small, small, jax.ShapeDtypeStruct((G, 1), F32), big, big], name=name)(
        a_re, a_im, log_dt, b_re, b_im, g_abr, g_abi, g_bbr, g_bbi, expand, expand_t)


def _s5_permute_in(src_ref, dst_ref):
    for i in range(S5_LS):
        dst_ref[pl.ds(8 * i, 8), :] = src_ref[pl.ds(i, 8, stride=S5_LS), :]


def _s5_permute_out(src_ref, dst_ref):
    for r in range(S5_SEG):
        for k in range(S5_LS // 8):
            dst_ref[pl.ds(r * S5_LS + 8 * k, 8), :] = src_ref[pl.ds(64 * k + r, 8, stride=8), :]


def _s5_scan(a_r, a_i, dr_ref, di_ref, init_r, init_i, store=None, reverse=False, conj=False):
    sgn = -1.0 if conj else 1.0

    def steps(n, c):
        sr, si = c
        for u in range(S5_UNROLL):
            k = n * S5_UNROLL + u
            i = S5_LS - 1 - k if reverse else k
            nr = a_r * sr - sgn * a_i * si + dr_ref[i]
            ni = a_r * si + sgn * a_i * sr + di_ref[i]
            if store is not None:
                store(i, nr, ni, sr, si)
            sr, si = nr, ni
        return sr, si

    return lax.fori_loop(0, S5_LS // S5_UNROLL, steps, (init_r, init_i))


def _s5_stitch(apr, api, fin_r, fin_i, car_r, car_i, reverse=False, conj=False):
    sgn = -1.0 if conj else 1.0
    rows_r, rows_i = [None] * S5_SEG, [None] * S5_SEG
    order = range(S5_SEG - 1, -1, -1) if reverse else range(S5_SEG)
    for r in order:
        rows_r[r], rows_i[r] = car_r, car_i
        fr, fi = fin_r[r:r + 1], fin_i[r:r + 1]
        car_r, car_i = (apr * car_r - sgn * api * car_i + fr, apr * car_i + sgn * api * car_r + fi)
    return jnp.concatenate(rows_r, 0), jnp.concatenate(rows_i, 0), car_r, car_i


def _s5_specs(nb, rev):
    blk = (lambda c, b: (nb - 1 - b, c)) if rev else (lambda c, b: (b, c))
    tok = pl.BlockSpec((S5_TB, S5_CW), blk)
    par_b = pl.BlockSpec((1, S5_CW, S5_NST), lambda c, b: (c, 0, 0))
    par_c = pl.BlockSpec((1, S5_NST, S5_CW), lambda c, b: (c, 0, 0))
    vec_s = pl.BlockSpec((1, 1, S5_NST), lambda c, b: (c, 0, 0))
    vec_c = pl.BlockSpec((1, 1, S5_CW), lambda c, b: (c, 0, 0))
    return tok, par_b, par_c, vec_s, vec_c


def s5_fwd(proj, bbr, bbi, ccr, cci, abr, abi, apr, api, dskip, *, name, comm=None):
    T = proj.shape[0]
    nb = T // S5_TB
    zeros8 = functools.partial(jnp.zeros, (S5_SEG, S5_NST), F32)

    def body(u_ref, bbr_ref, bbi_ref, ccr_ref, cci_ref, ar_ref, ai_ref, apr_ref, api_ref, d_ref,
             y_ref, csr_ref, csi_ref, uf_ref, up_ref, dr_ref, di_ref, sr_ref, si_ref, yp_ref, car_ref, cai_ref):
        b = pl.program_id(1)

        @pl.when(b == 0)
        def _():
            car_ref[...] = jnp.zeros_like(car_ref)
            cai_ref[...] = jnp.zeros_like(cai_ref)

        csr_ref[0, 0] = car_ref[...]
        csi_ref[0, 0] = cai_ref[...]
        uf_ref[...] = u_ref[...].astype(F32)
        _s5_permute_in(uf_ref, up_ref)
        upb = up_ref[...].astype(BF16)
        dr_ref[...] = _dot(upb, bbr_ref[0]).reshape(S5_LS, S5_SEG, S5_NST)
        di_ref[...] = _dot(upb, bbi_ref[0]).reshape(S5_LS, S5_SEG, S5_NST)
        a_r = jnp.broadcast_to(ar_ref[0], (S5_SEG, S5_NST))
        a_i = jnp.broadcast_to(ai_ref[0], (S5_SEG, S5_NST))
        fin_r, fin_i = _s5_scan(a_r, a_i, dr_ref, di_ref, zeros8(), zeros8())
        cin_r, cin_i, ncr, nci = _s5_stitch(apr_ref[0], api_ref[0], fin_r, fin_i, car_ref[...], cai_ref[...])
        car_ref[...] = ncr
        cai_ref[...] = nci

        def store(i, nr, ni, sr, si):
            sr_ref[i] = nr
            si_ref[i] = ni

        _s5_scan(a_r, a_i, dr_ref, di_ref, cin_r, cin_i, store=store)
        s_r = sr_ref[...].reshape(S5_TB, S5_NST).astype(BF16)
        s_i = si_ref[...].reshape(S5_TB, S5_NST).astype(BF16)
        yp_ref[...] = _dot(s_r, ccr_ref[0]) - _dot(s_i, cci_ref[0]) + d_ref[0] * up_ref[...]
        _s5_permute_out(yp_ref, uf_ref)
        y_ref[...] = uf_ref[...].astype(BF16)

    tok, par_b, par_c, vec_s, vec_c = _s5_specs(nb, False)
    cs_spec = pl.BlockSpec((1, 1, 1, S5_NST), lambda c, b: (b, c, 0, 0))
    cs_shape = jax.ShapeDtypeStruct((nb, S5_CH, 1, S5_NST), F32)
    tokbuf = pltpu.VMEM((S5_TB, S5_CW), F32)
    stbuf = pltpu.VMEM((S5_LS, S5_SEG, S5_NST), F32)
    return _call(
        body, grid=(S5_CH, nb),
        in_specs=[tok, par_b, par_b, par_c, par_c, vec_s, vec_s, vec_s, vec_s, vec_c],
        out_specs=[tok, cs_spec, cs_spec],
        out_shape=[jax.ShapeDtypeStruct((T, D_S5), BF16), cs_shape, cs_shape],
        scratch_shapes=[tokbuf, tokbuf, stbuf, stbuf, stbuf, stbuf, tokbuf,
                        pltpu.VMEM((1, S5_NST), F32), pltpu.VMEM((1, S5_NST), F32)],
        sem=("parallel", "arbitrary"), name=name, comm=comm,
        args=(proj, bbr, bbi, ccr, cci, abr, abi, apr, api, dskip))


def s5_bwd(proj, dys, csr, csi, bbr, bbi, ccr, cci, abr, abi, apr, api, dskip, dproj, *, name, comm=None):
    T = proj.shape[0]
    nb = T // S5_TB
    zeros8 = functools.partial(jnp.zeros, (S5_SEG, S5_NST), F32)

    def body(u_ref, gy_ref, csr_ref, csi_ref, bbr_ref, bbi_ref, ccr_ref, cci_ref, ar_ref, ai_ref, apr_ref, api_ref,
             d_ref, dproj_ref, du_ref, dbr_ref, dbi_ref, dcr_ref, dci_ref, dd_ref, dar_ref, dai_ref,
             tmp_ref, up_ref, gyp_ref, dr_ref, di_ref, sr_ref, si_ref, gr_ref, gi_ref, car_ref, cai_ref):
        b = pl.program_id(1)

        @pl.when(b == 0)
        def _():
            car_ref[...] = jnp.zeros_like(car_ref)
            cai_ref[...] = jnp.zeros_like(cai_ref)
            for ref in (dbr_ref, dbi_ref, dcr_ref, dci_ref, dd_ref, dar_ref, dai_ref):
                ref[...] = jnp.zeros_like(ref)

        tmp_ref[...] = u_ref[...].astype(F32)
        _s5_permute_in(tmp_ref, up_ref)
        tmp_ref[...] = gy_ref[...].astype(F32)
        _s5_permute_in(tmp_ref, gyp_ref)
        upb = up_ref[...].astype(BF16)
        gyp = gyp_ref[...]
        gypb = gyp.astype(BF16)
        dr_ref[...] = _dot(upb, bbr_ref[0]).reshape(S5_LS, S5_SEG, S5_NST)
        di_ref[...] = _dot(upb, bbi_ref[0]).reshape(S5_LS, S5_SEG, S5_NST)
        a_r = jnp.broadcast_to(ar_ref[0], (S5_SEG, S5_NST))
        a_i = jnp.broadcast_to(ai_ref[0], (S5_SEG, S5_NST))
        fin_r, fin_i = _s5_scan(a_r, a_i, dr_ref, di_ref, zeros8(), zeros8())
        cin_r, cin_i, _, _ = _s5_stitch(apr_ref[0], api_ref[0], fin_r, fin_i, csr_ref[0, 0], csi_ref[0, 0])
        sr_ref[0] = cin_r
        si_ref[0] = cin_i

        def store_s(i, nr, ni, sr, si):
            sr_ref[i + 1] = nr
            si_ref[i + 1] = ni

        _s5_scan(a_r, a_i, dr_ref, di_ref, cin_r, cin_i, store=store_s)
        s_r = sr_ref[pl.ds(1, S5_LS)].reshape(S5_TB, S5_NST).astype(BF16)
        s_i = si_ref[pl.ds(1, S5_LS)].reshape(S5_TB, S5_NST).astype(BF16)
        dcr_ref[0] += _dot_tn(s_r, gypb)
        dci_ref[0] -= _dot_tn(s_i, gypb)
        dd_ref[0] += jnp.sum(gyp * up_ref[...], axis=0, keepdims=True)
        dr_ref[...] = _dot_nt(gypb, ccr_ref[0]).reshape(S5_LS, S5_SEG, S5_NST)
        di_ref[...] = (-_dot_nt(gypb, cci_ref[0])).reshape(S5_LS, S5_SEG, S5_NST)
        fin_r, fin_i = _s5_scan(a_r, a_i, dr_ref, di_ref, zeros8(), zeros8(), reverse=True, conj=True)
        gin_r, gin_i, ncr, nci = _s5_stitch(apr_ref[0], api_ref[0], fin_r, fin_i, car_ref[...], cai_ref[...],
                                            reverse=True, conj=True)
        car_ref[...] = ncr
        cai_ref[...] = nci
        def steps_g(n, carry):
            gr, gi, acc_r, acc_i = carry
            for u in range(S5_UNROLL):
                i = S5_LS - 1 - (n * S5_UNROLL + u)
                nr = a_r * gr + a_i * gi + dr_ref[i]
                ni = a_r * gi - a_i * gr + di_ref[i]
                gr_ref[i] = nr
                gi_ref[i] = ni
                pr, pi = sr_ref[i], si_ref[i]
                acc_r, acc_i = acc_r + (nr * pr + ni * pi), acc_i + (ni * pr - nr * pi)
                gr, gi = nr, ni
            return gr, gi, acc_r, acc_i

        _, _, acc_r, acc_i = lax.fori_loop(0, S5_LS // S5_UNROLL, steps_g, (gin_r, gin_i, zeros8(), zeros8()))
        dar_ref[0] += jnp.sum(acc_r, axis=0, keepdims=True)
        dai_ref[0] += jnp.sum(acc_i, axis=0, keepdims=True)
        g_r = gr_ref[...].reshape(S5_TB, S5_NST).astype(BF16)
        g_i = gi_ref[...].reshape(S5_TB, S5_NST).astype(BF16)
        dbr_ref[0] += _dot_tn(upb, g_r)
        dbi_ref[0] += _dot_tn(upb, g_i)
        gyp_ref[...] = _dot_nt(g_r, bbr_ref[0]) + _dot_nt(g_i, bbi_ref[0]) + d_ref[0] * gyp
        _s5_permute_out(gyp_ref, tmp_ref)
        du_ref[...] = tmp_ref[...].astype(BF16)

    tok, par_b, par_c, vec_s, vec_c = _s5_specs(nb, True)
    cs_spec = pl.BlockSpec((1, 1, 1, S5_NST), lambda c, b: (nb - 1 - b, c, 0, 0))
    tokbuf = pltpu.VMEM((S5_TB, S5_CW), F32)
    stbuf = pltpu.VMEM((S5_LS, S5_SEG, S5_NST), F32)
    stbuf1 = pltpu.VMEM((S5_LS + 1, S5_SEG, S5_NST), F32)
    return _call(
        body, grid=(S5_CH, nb), comm=comm,
        in_specs=[tok, tok, cs_spec, cs_spec, par_b, par_b, par_c, par_c, vec_s, vec_s, vec_s, vec_s, vec_c, _ANY],
        out_specs=[tok, par_b, par_b, par_c, par_c, vec_c, vec_s, vec_s], aliases={13: 0},
        out_shape=[jax.ShapeDtypeStruct(dproj.shape, BF16),
                   jax.ShapeDtypeStruct((S5_CH, S5_CW, S5_NST), F32), jax.ShapeDtypeStruct((S5_CH, S5_CW, S5_NST), F32),
                   jax.ShapeDtypeStruct((S5_CH, S5_NST, S5_CW), F32), jax.ShapeDtypeStruct((S5_CH, S5_NST, S5_CW), F32),
                   jax.ShapeDtypeStruct((S5_CH, 1, S5_CW), F32),
                   jax.ShapeDtypeStruct((S5_CH, 1, S5_NST), F32), jax.ShapeDtypeStruct((S5_CH, 1, S5_NST), F32)],
        scratch_shapes=[tokbuf, tokbuf, tokbuf, stbuf, stbuf, stbuf1, stbuf1, stbuf, stbuf,
                        pltpu.VMEM((1, S5_NST), F32), pltpu.VMEM((1, S5_NST), F32)],
        sem=("parallel", "arbitrary"), name=name,
        args=(proj, dys, csr, csi, bbr, bbi, ccr, cci, abr, abi, apr, api, dskip, dproj))


SSD_L = SSD_CHUNK
SSD_GW = 256
NEG = -1e30


def _expand16(v):
    lane = lax.broadcasted_iota(jnp.int32, (v.shape[0], 128), 1)
    parts = [jnp.where(lane < SSD_HEADDIM, v[:, 2 * j:2 * j + 1], v[:, 2 * j + 1:2 * j + 2]) for j in range(8)]
    return jnp.concatenate(parts, axis=1)


def _headsum(v, hsum):
    hi = v.astype(BF16)
    lo = (v - hi.astype(F32)).astype(BF16)
    return _dot(hi, hsum) + _dot(lo, hsum)


def _softplus(x):
    return jnp.maximum(x, 0.0) + jnp.log(1.0 + jnp.exp(-jnp.abs(x)))


def _ssd_chunk_fwd(z, xbc, tail, dtraw, hprev, cw, cb, dtb, alog, dsk, nw, tril):
    L = SSD_L
    f = {}
    xe = jnp.concatenate([tail, xbc], axis=0)
    sh = [xbc] + [pltpu.roll(xe, s, 0)[8:] for s in (1, 2, 3)]
    conv = cb + cw[3:4] * sh[0] + cw[2:3] * sh[1] + cw[1:2] * sh[2] + cw[0:1] * sh[3]
    sig = _sigmoid(conv)
    xa = conv * sig
    xs, bm, cm = xa[:, :D_SSD], xa[:, D_SSD:D_SSD + 512], xa[:, D_SSD + 512:]
    pre = dtraw + dtb
    dt = _softplus(pre)
    a_h = -jnp.exp(alog)
    acum = _dot_hi(tril, dt * a_h)
    acum_t = acum.T
    alast = acum[L - 1:L]
    exp_a = jnp.exp(acum)
    dec = jnp.exp(alast - acum)
    exp_al = jnp.exp(alast)
    dt_x, dec_x, exp_a_x, exp_al_x = _expand16(dt), _expand16(dec), _expand16(exp_a), _expand16(exp_al)
    d_x = _expand16(dsk)
    xh = xs * dt_x
    xhb = xh.astype(BF16)
    xd = (xh * dec_x).astype(BF16)
    row = lax.broadcasted_iota(jnp.int32, (L, L), 0)
    col = lax.broadcasted_iota(jnp.int32, (L, L), 1)
    causal = row >= col
    lane = lax.broadcasted_iota(jnp.int32, (L, 128), 1)
    low = lane < SSD_HEADDIM
    hb = hprev.astype(BF16)
    y_pairs, yoff_parts, st_parts, cbs, lms = [], [], [], [], []
    for g in range(SSD_NGROUPS):
        bg = bm[:, g * 128:(g + 1) * 128].astype(BF16)
        cg = cm[:, g * 128:(g + 1) * 128].astype(BF16)
        cbg = _dot_nt(cg, bg)
        cbs.append(cbg)
        for j in (2 * g, 2 * g + 1):
            xp = xhb[:, j * 128:(j + 1) * 128]
            ys = []
            for h in (2 * j, 2 * j + 1):
                lm = jnp.exp(jnp.where(causal, acum[:, h:h + 1] - acum_t[h:h + 1, :], NEG))
                lms.append(lm)
                ys.append(_dot((cbg * lm).astype(BF16), xp))
            y_pairs.append(jnp.where(low, ys[0], ys[1]))
        gs = slice(g * SSD_GW, (g + 1) * SSD_GW)
        yoff_parts.append(_dot(cg, hb[:, gs]) * exp_a_x[:, gs])
        st_parts.append(_dot_tn(bg, xd[:, gs]))
    yoff = jnp.concatenate(yoff_parts, axis=1)
    y = jnp.concatenate(y_pairs, axis=1) + yoff + d_x * xs
    hnew = exp_al_x * hprev + jnp.concatenate(st_parts, axis=1)
    sz = _sigmoid(z)
    gz = y * (z * sz)
    r = lax.rsqrt(jnp.mean(gz * gz, axis=-1, keepdims=True) + EPS)
    out = gz * r * nw
    f.update(sh=sh, conv=conv, sig=sig, xs=xs, bm=bm, cm=cm, pre=pre, dt=dt, a_h=a_h, exp_a_x=exp_a_x, dec_x=dec_x,
             exp_al=exp_al, exp_al_x=exp_al_x, dt_x=dt_x, d_x=d_x, xh=xh, xhb=xhb, xd=xd, causal=causal, low=low, hb=hb,
             cbs=cbs, lms=lms, yoff=yoff, y=y, sz=sz, gz=gz, r=r)
    return out, hnew, f


def _ssd_params(conv_w, conv_b, dt_bias, a_log, d_skip, norm_w):
    pad16 = lambda v: jnp.pad(v.reshape(1, SSD_HEADS), ((0, 0), (0, 128 - SSD_HEADS)))
    return (jnp.pad(conv_w, ((0, 8 - SSD_CONV), (0, 0))), conv_b.reshape(1, D_XBC), pad16(dt_bias), pad16(a_log),
            pad16(d_skip), norm_w.reshape(1, D_SSD))


def _ssd_param_specs():
    full = lambda shape: pl.BlockSpec(shape, lambda i: (0, 0))
    return [full((8, D_XBC)), full((1, D_XBC)), full((1, 128)), full((1, 128)), full((1, 128)), full((1, D_SSD))]


def ssd_fwd(proj, dtraw, params, tril, ymix, *, name, comm=None):
    T = proj.shape[0]
    nc = T // SSD_L

    def body(z_ref, x_ref, dt_ref, cw_ref, cb_ref, dtb_ref, al_ref, dsk_ref, nw_ref, tril_ref, ymix_ref,
             o_ref, hs_ref, h_ref, tail_ref):
        i = pl.program_id(0)

        @pl.when(i == 0)
        def _():
            h_ref[...] = jnp.zeros_like(h_ref)
            tail_ref[...] = jnp.zeros_like(tail_ref)

        xbc = x_ref[...].astype(F32)
        hprev = h_ref[...]
        hs_ref[0] = hprev
        out, hnew, _ = _ssd_chunk_fwd(z_ref[...].astype(F32), xbc, tail_ref[...], dt_ref[...], hprev, cw_ref[...],
                                      cb_ref[...], dtb_ref[...], al_ref[...], dsk_ref[...], nw_ref[...], tril_ref[...])
        o_ref[...] = out.astype(BF16)
        h_ref[...] = hnew
        tail_ref[...] = xbc[SSD_L - 8:]

    return _call(
        body, grid=(nc,),
        in_specs=[pl.BlockSpec((SSD_L, D_SSD), lambda i: (i, 1)), pl.BlockSpec((SSD_L, D_XBC), lambda i: (i, 1)),
                  pl.BlockSpec((SSD_L, 128), lambda i: (i, 0))] + _ssd_param_specs()
                 + [pl.BlockSpec((SSD_L, SSD_L), lambda i: (0, 0)), _ANY],
        out_specs=[pl.BlockSpec((SSD_L, D_SSD), lambda i: (i, 1)),
                   pl.BlockSpec((1, SSD_STATE, D_SSD), lambda i: (i, 0, 0))],
        out_shape=[jax.ShapeDtypeStruct(ymix.shape, BF16), jax.ShapeDtypeStruct((nc, SSD_STATE, D_SSD), F32)],
        scratch_shapes=[pltpu.VMEM((SSD_STATE, D_SSD), F32), pltpu.VMEM((8, D_XBC), F32)],
        sem=("arbitrary",), name=name, args=(proj, proj, dtraw, *params, tril, ymix), aliases={10: 0}, comm=comm)


def ssd_bwd(proj, dtraw, hs, dymix, params, tril, triu, trils, headsum, *, name, comm=None):
    T = proj.shape[0]
    nc = T // SSD_L
    L = SSD_L

    def body(z_ref, x_ref, xprev_ref, dt_ref, hs_ref, do_ref, cw_ref, cb_ref, dtb_ref, al_ref, dsk_ref, nw_ref,
             tril_ref, triu_ref, trils_ref, hsum_ref,
             dp_ref, ddt_ref, dcw_ref, dcb_ref, ddtb_ref, dal_ref, ddsk_ref, dnw_ref, dh_ref, dnext_ref):
        i = pl.program_id(0)

        @pl.when(i == 0)
        def _():
            dh_ref[...] = jnp.zeros_like(dh_ref)
            dnext_ref[...] = jnp.zeros_like(dnext_ref)
            for ref in (dcw_ref, dcb_ref, ddtb_ref, dal_ref, ddsk_ref, dnw_ref):
                ref[...] = jnp.zeros_like(ref)

        z = z_ref[...].astype(F32)
        xbc = x_ref[...].astype(F32)
        tail = jnp.where(i == nc - 1, 0.0, xprev_ref[...].astype(F32))
        hprev = hs_ref[0]
        cw, nw = cw_ref[...], nw_ref[...]
        hsum = hsum_ref[...]
        _, _, f = _ssd_chunk_fwd(z, xbc, tail, dt_ref[...], hprev, cw, cb_ref[...], dtb_ref[...], al_ref[...],
                                 dsk_ref[...], nw, tril_ref[...])
        dout = do_ref[...].astype(F32)
        dh = dh_ref[...]
        ghat = f["gz"] * f["r"]
        dn = dout * nw
        dgz = f["r"] * (dn - ghat * jnp.mean(dn * ghat, axis=-1, keepdims=True))
        dnw_ref[...] += jnp.sum(dout * ghat, axis=0, keepdims=True)
        sz = f["sz"]
        dy = dgz * (z * sz)
        dp_ref[:, :D_S5] = jnp.zeros((L, D_S5), BF16)
        dp_ref[:, D_S5:D_S5 + D_SSD] = (dgz * f["y"] * sz * (1.0 + z * (1.0 - sz))).astype(BF16)
        xs = f["xs"]
        ddsk_ref[...] += jnp.sum(_headsum(dy * xs, hsum), axis=0, keepdims=True)
        dyb = dy.astype(BF16)
        dye = (dy * f["exp_a_x"]).astype(BF16)
        dhb = dh.astype(BF16)
        lane = lax.broadcasted_iota(jnp.int32, (L, 128), 1)
        sub = lax.broadcasted_iota(jnp.int32, (128, L), 0)
        zero_b = jnp.zeros((L, 128), BF16)
        rsum = jnp.zeros((L, 128), F32)
        csum_t = jnp.zeros((128, L), F32)
        dx_pairs, dxst_parts, db_parts, dc_parts, dhp_parts = [], [], [], [], []
        for g in range(SSD_NGROUPS):
            gs = slice(g * SSD_GW, (g + 1) * SSD_GW)
            bg = f["bm"][:, g * 128:(g + 1) * 128].astype(BF16)
            cg = f["cm"][:, g * 128:(g + 1) * 128].astype(BF16)
            cbg = f["cbs"][g]
            dcb_g = jnp.zeros((L, L), F32)
            for j in (2 * g, 2 * g + 1):
                xp = f["xhb"][:, j * 128:(j + 1) * 128]
                dyp = dyb[:, j * 128:(j + 1) * 128]
                dxs = []
                for half, h in enumerate((2 * j, 2 * j + 1)):
                    lm = f["lms"][h]
                    dyh = jnp.where(f["low"], dyp, zero_b) if half == 0 else jnp.where(f["low"], zero_b, dyp)
                    dw = jnp.where(f["causal"], _dot_nt(dyh, xp), 0.0)
                    w = cbg * lm
                    e = dw * w
                    dcb_g = dcb_g + dw * lm
                    rsum = jnp.where(lane == h, jnp.sum(e, axis=1, keepdims=True), rsum)
                    csum_t = jnp.where(sub == h, jnp.sum(e, axis=0, keepdims=True), csum_t)
                    dxs.append(_dot_tn(w.astype(BF16), dyp))
                dx_pairs.append(jnp.where(f["low"], dxs[0], dxs[1]))
            dcbb = dcb_g.astype(BF16)
            dxst_parts.append(f["dec_x"][:, gs] * _dot(bg, dhb[:, gs]))
            dc_parts.append(_dot(dcbb, bg) + _dot_nt(dye[:, gs], f["hb"][:, gs]))
            db_parts.append(_dot_tn(dcbb, cg) + _dot_nt(f["xd"][:, gs], dhb[:, gs]))
            dhp_parts.append(f["exp_al_x"][:, gs] * dh[:, gs] + _dot_tn(cg, dye[:, gs]))
        dxst = jnp.concatenate(dxst_parts, axis=1)
        dxh = jnp.concatenate(dx_pairs, axis=1) + dxst
        q = _headsum(f["yoff"] * dy, hsum)
        dstate = _headsum(f["xh"] * dxst, hsum)
        h0t = jnp.sum(_headsum(dh * hprev, hsum), axis=0, keepdims=True) * f["exp_al"]
        da = _dot_hi(triu_ref[...], rsum - csum_t.T + q) + _dot_hi(trils_ref[...], dstate) + h0t
        dt, a_h = f["dt"], f["a_h"]
        ddt = _headsum(dxh * xs, hsum) + da * a_h
        dal_ref[...] += jnp.sum(da * dt, axis=0, keepdims=True) * a_h
        ddtraw = ddt * _sigmoid(f["pre"])
        first16 = lane < SSD_HEADS
        ddtraw = jnp.where(first16, ddtraw, 0.0)
        ddt_ref[...] = ddtraw
        ddtb_ref[...] += jnp.sum(ddtraw, axis=0, keepdims=True)
        dh_ref[...] = jnp.concatenate(dhp_parts, axis=1)
        dxa = jnp.concatenate([dxh * f["dt_x"] + f["d_x"] * dy] + db_parts + dc_parts, axis=1)
        sig, conv = f["sig"], f["conv"]
        dconv = dxa * sig * (1.0 + conv * (1.0 - sig))
        dcb_ref[...] += jnp.sum(dconv, axis=0, keepdims=True)
        rows = [jnp.sum(dconv * f["sh"][3 - k], axis=0, keepdims=True) for k in range(SSD_CONV)]
        dcw_ref[...] += jnp.concatenate(rows + [jnp.zeros((8 - SSD_CONV, D_XBC), F32)], axis=0)
        de = jnp.concatenate([dconv, dnext_ref[...]], axis=0)
        dxbc = cw[3:4] * dconv
        for s in (1, 2, 3):
            dxbc = dxbc + cw[3 - s:4 - s] * pltpu.roll(de, L + 8 - s, 0)[:L]
        dp_ref[:, D_S5 + D_SSD:] = dxbc.astype(BF16)
        dnext_ref[...] = dconv[:8]

    rev = lambda i: nc - 1 - i
    acc = lambda shape: pl.BlockSpec(shape, lambda i: (0, 0))
    tri = pl.BlockSpec((L, L), lambda i: (0, 0))
    return _call(
        body, grid=(nc,),
        in_specs=[pl.BlockSpec((L, D_SSD), lambda i: (rev(i), 1)), pl.BlockSpec((L, D_XBC), lambda i: (rev(i), 1)),
                  pl.BlockSpec((8, D_XBC), lambda i: (jnp.maximum(rev(i) * (L // 8) - 1, 0), 1)),
                  pl.BlockSpec((L, 128), lambda i: (rev(i), 0)),
                  pl.BlockSpec((1, SSD_STATE, D_SSD), lambda i: (rev(i), 0, 0)),
                  pl.BlockSpec((L, D_SSD), lambda i: (rev(i), 1))] + _ssd_param_specs()
                 + [tri, tri, tri, pl.BlockSpec((D_SSD, 128), lambda i: (0, 0))],
        out_specs=[pl.BlockSpec((L, D_MAIN), lambda i: (rev(i), 0)), pl.BlockSpec((L, 128), lambda i: (rev(i), 0)),
                   acc((8, D_XBC)), acc((1, D_XBC)), acc((1, 128)), acc((1, 128)), acc((1, 128)), acc((1, D_SSD))],
        out_shape=[jax.ShapeDtypeStruct((T, D_MAIN), BF16),
                   jax.ShapeDtypeStruct((T, 128), F32), jax.ShapeDtypeStruct((8, D_XBC), F32),
                   jax.ShapeDtypeStruct((1, D_XBC), F32), jax.ShapeDtypeStruct((1, 128), F32),
                   jax.ShapeDtypeStruct((1, 128), F32), jax.ShapeDtypeStruct((1, 128), F32),
                   jax.ShapeDtypeStruct((1, D_SSD), F32)],
        scratch_shapes=[pltpu.VMEM((SSD_STATE, D_SSD), F32), pltpu.VMEM((8, D_XBC), F32)],
        sem=("arbitrary",), name=name, comm=comm,
        args=(proj, proj, proj, dtraw, hs, dymix, *params, tril, triu, trils, headsum))


def _s5_blockdiag(v, rows_per_group, cols_per_group):
    eye = jnp.eye(S5_SEG, dtype=v.dtype)
    w = v[:, :, :, None, :] * eye[None, :, None, :, None]
    return w.reshape(S5_CH, 8 * rows_per_group, 8 * cols_per_group)


def _s5_blockdiag_extract(w, rows_per_group, cols_per_group):
    eye = jnp.eye(S5_SEG, dtype=w.dtype)
    w5 = w.reshape(S5_CH, 8, rows_per_group, 8, cols_per_group)
    return jnp.sum(w5 * eye[None, :, None, :, None], axis=3)


TM = 512

OFF = dict(mlp_w2=0, mlp_w1=1024, w_out=2048, s5_w_glu=2560, xa_wq=2816, xa_wk=3072, xa_wv=3328, xa_wo=3584, w_in=3840)
ROWS = dict(mlp_w2=1024, mlp_w1=1024, w_out=512, s5_w_glu=256, xa_wq=256, xa_wk=256, xa_wv=256, xa_wo=256, w_in=1028)
TAIL_OFF = 4872
TAIL_ROWS = 120
PACK_ROWS = TAIL_OFF + TAIL_ROWS
ROWS_ALL = (0, PACK_ROWS, 832, True)
ROWS_EARLY = (0, OFF["w_in"], 640, False)
ROWS_LATE = (OFF["w_in"], PACK_ROWS - OFF["w_in"], 192, True)
N_SHARD = 4
SMALL_L = ("norm_mix", "s5_a_re", "s5_a_im", "s5_log_dt", "s5_b_re", "s5_b_im", "s5_c_re", "s5_c_im", "s5_d",
           "ssd_conv_w", "ssd_conv_b", "ssd_dt_bias", "ssd_a_log", "ssd_d", "ssd_norm", "norm_xattn", "norm_mem",
           "norm_mlp")
SMALL_Q = 72
CONV_ROWS = 8


def _place():
    x, y, c = lax.axis_index("x"), lax.axis_index("y"), lax.axis_index("c")
    chips = [(1 - x, y), (x, 1 - y), (1 - x, 1 - y)]
    return x, y, c, 2 * x + y, chips, (x, y, 1 - c)


def _remote(src, dst, send_sem, recv_sem, to):
    return pltpu.make_async_remote_copy(src_ref=src, dst_ref=dst, send_sem=send_sem, recv_sem=recv_sem,
                                        device_id=to, device_id_type=MESH_T)


def _dma_sems(*counts):
    return [pltpu.SemaphoreType.DMA((n,)) for n in counts]


def place_own(wpack, shard_idx):
    tile = PACK_ROWS // 4

    def body(s_ref, w_ref, o_ref):
        o_ref[0] = w_ref[...]

    return pl.pallas_call(
        body,
        grid_spec=pltpu.PrefetchScalarGridSpec(
            num_scalar_prefetch=1, grid=(4,),
            in_specs=[pl.BlockSpec((tile, D_MODEL), lambda i, s_ref: (i, 0))],
            out_specs=pl.BlockSpec((1, tile, D_MODEL), lambda i, s_ref: (s_ref[0], i, 0))),
        out_shape=jax.ShapeDtypeStruct((N_SHARD,) + wpack.shape, wpack.dtype),
        compiler_params=_cp("parallel"), name="place_own")(shard_idx, wpack)


def _range_half(ref, rows, c):
    half = rows[1] // 2
    return pl.ds(pl.multiple_of(rows[0] + c * half, 16), half)


def gather_over_ici(wpack, placed, rows=ROWS_ALL):
    def copies(ci, co, cs):
        w_ref, (out_ref,), (send, recv) = ci[0], co, cs
        x, y, c, s, chips, sibling = _place()
        mine = _range_half(w_ref, rows, c)
        sends = [_remote(w_ref.at[mine, :], out_ref.at[s, mine, :], send.at[j], recv.at[j], (*chip, c))
                 for j, chip in enumerate(chips)]
        lands = [out_ref.at[2 * chip[0] + chip[1], mine, :] for chip in chips]
        arrivals = [_remote(land, land, send.at[j], recv.at[j], sibling) for j, land in enumerate(lands)]
        return sends, arrivals

    def start(ci, co, cs):
        for cp in copies(ci, co, cs)[0]:
            cp.start()

    def wait(ci, co, cs):
        sends, arrivals = copies(ci, co, cs)
        for cp in arrivals:
            cp.wait_recv()
        for cp in sends:
            cp.wait_send()

    return Comm("gather_ici", [wpack, placed], [jax.ShapeDtypeStruct(placed.shape, placed.dtype)], _dma_sems(3, 3),
                start, wait, aliases={1: 0})


def gather_to_sibling(landed, rows=ROWS_ALL):
    def copies(co, cs):
        (out_ref,), (send, recv) = co, cs
        x, y, c, s, chips, sibling = _place()
        slots = [2 * chip[0] + chip[1] for chip in chips]
        mine, other = _range_half(out_ref, rows, c), _range_half(out_ref, rows, 1 - c)
        sends = [_remote(out_ref.at[t, mine, :], out_ref.at[t, mine, :], send.at[j], recv.at[j], sibling)
                 for j, t in enumerate(slots)]
        arrivals = [_remote(out_ref.at[t, other, :], out_ref.at[t, other, :], send.at[j], recv.at[j], sibling)
                    for j, t in enumerate(slots)]
        return sends, arrivals

    def start(ci, co, cs):
        for cp in copies(co, cs)[0]:
            cp.start()

    def wait(ci, co, cs):
        sends, arrivals = copies(co, cs)
        for cp in arrivals:
            cp.wait_recv()
        for cp in sends:
            cp.wait_send()

    return Comm("gather_d2d", [landed], [jax.ShapeDtypeStruct(landed.shape, landed.dtype)], _dma_sems(3, 3), start, wait,
                aliases={0: 0})


def exchange_halves(gpack, rows):
    def copy(ci, co, cs):
        (g_ref,), (out_ref,), (send, recv) = ci, co, cs
        x, y, c, s, chips, sibling = _place()
        return _remote(g_ref.at[:, _range_half(g_ref, rows, 1 - c), :], out_ref, send.at[0], recv.at[0], sibling)

    return Comm("exchange", [gpack], [jax.ShapeDtypeStruct((N_SHARD, rows[1] // 2, D_MODEL), F32)], _dma_sems(1, 1),
                lambda ci, co, cs: copy(ci, co, cs).start(), lambda ci, co, cs: copy(ci, co, cs).wait())


def scatter_chips(csum):
    def copies(ci, co, cs):
        (c_ref,), (out_ref,), (send, recv) = ci, co, cs
        x, y, c, s, chips, sibling = _place()
        sends = [_remote(c_ref.at[2 * chip[0] + chip[1]], out_ref.at[s], send.at[j], recv.at[j], (*chip, c))
                 for j, chip in enumerate(chips)]
        arrivals = [_remote(c_ref.at[2 * chip[0] + chip[1]], out_ref.at[2 * chip[0] + chip[1]], send.at[j], recv.at[j],
                            (*chip, c)) for j, chip in enumerate(chips)]
        return sends, arrivals

    def start(ci, co, cs):
        for cp in copies(ci, co, cs)[0]:
            cp.start()

    def wait(ci, co, cs):
        sends, arrivals = copies(ci, co, cs)
        for cp in arrivals:
            cp.wait_recv()
        for cp in sends:
            cp.wait_send()

    return Comm("scatter", [csum], [jax.ShapeDtypeStruct(csum.shape, csum.dtype)], _dma_sems(3, 3), start, wait)


def share_reduced(gshards, smalls, l, rows):
    with_tail = rows[3]

    def copies(ci, co, cs):
        (g_ref, sm_ref), (send, recv, loc) = co, cs
        x, y, c, s, chips, sibling = _place()
        my_half = g_ref.at[l, _range_half(g_ref, rows, c), :]
        tail = g_ref.at[l, pl.ds(PACK_ROWS - TAIL_ROWS, TAIL_ROWS), :]
        big = _remote(my_half, my_half, send.at[0], recv.at[0], sibling)
        keep_tail = pltpu.make_async_copy(tail, sm_ref.at[l, s], loc.at[0])
        tails = [_remote(tail, sm_ref.at[l, s], send.at[1 + j], recv.at[1 + j], (*chip, 1)) for j, chip in enumerate(chips)]
        tails += [_remote(tail, sm_ref.at[l, s], send.at[4 + j], recv.at[1 + j], (*chip, 0)) for j, chip in enumerate(chips)]
        tails.append(_remote(tail, sm_ref.at[l, s], send.at[7], recv.at[4], sibling))
        other = g_ref.at[l, _range_half(g_ref, rows, 1 - c), :]
        big_in = _remote(other, other, send.at[0], recv.at[0], sibling)
        slots = [sm_ref.at[l, 2 * chip[0] + chip[1]] for chip in chips]
        tails_in = [_remote(slot, slot, send.at[1 + j], recv.at[1 + j], sibling) for j, slot in enumerate(slots)]
        sib_tail_in = _remote(sm_ref.at[l, s], sm_ref.at[l, s], send.at[7], recv.at[4], sibling)
        return c, big, keep_tail, tails, big_in, tails_in, sib_tail_in

    def start(ci, co, cs):
        c, big, keep_tail, tails, _, _, _ = copies(ci, co, cs)
        big.start()
        if with_tail:
            @pl.when(c == 1)
            def _():
                keep_tail.start()
                for cp in tails:
                    cp.start()

    def wait(ci, co, cs):
        c, big, keep_tail, tails, big_in, tails_in, sib_tail_in = copies(ci, co, cs)
        big_in.wait_recv()
        big.wait_send()
        if with_tail:
            for cp in tails_in:
                cp.wait_recv()

            @pl.when(c == 0)
            def _():
                sib_tail_in.wait_recv()

            @pl.when(c == 1)
            def _():
                for cp in tails:
                    cp.wait_send()
                keep_tail.wait()

    sds = lambda a: jax.ShapeDtypeStruct(a.shape, a.dtype)
    return Comm("share", [gshards, smalls], [sds(gshards), sds(smalls)], _dma_sems(8, 5, 1), start, wait,
                aliases={0: 0, 1: 1})


def _consts():
    e = np.zeros((S5_STATE, S5_STATE * S5_GROUP), np.float32)
    for p in range(S5_STATE):
        e[p, p * S5_GROUP:(p + 1) * S5_GROUP] = 1.0
    hs = np.zeros((D_SSD, 128), np.float32)
    for h in range(SSD_HEADS):
        hs[h * SSD_HEADDIM:(h + 1) * SSD_HEADDIM, h] = 1.0
    ones = np.ones((SSD_L, SSD_L), np.float32)
    return dict(expand=jnp.asarray(e), expand_t=jnp.asarray(e.T), headsum=jnp.asarray(hs).astype(BF16),
                tril=jnp.asarray(np.tril(ones)), triu=jnp.asarray(np.triu(ones)), trils=jnp.asarray(np.tril(ones, -1)))


def _s5_mats(w, cst):
    b_re = w["s5_b_re"].reshape(S5_GROUPS, S5_STATE * S5_GROUP)
    b_im = w["s5_b_im"].reshape(S5_GROUPS, S5_STATE * S5_GROUP)
    abr, abi, apr, api, bbr, bbi = s5_params_fwd(w["s5_a_re"], w["s5_a_im"], w["s5_log_dt"].reshape(S5_GROUPS, 1),
                                                 b_re, b_im, cst["expand"], name="s5_params_fwd")
    t = lambda v: v.reshape(S5_CH, 8, S5_STATE, S5_GROUP).transpose(0, 1, 3, 2)
    c4 = lambda v: v.reshape(S5_CH, 8, S5_GROUP, S5_STATE).transpose(0, 1, 3, 2)
    vec = lambda v: v.reshape(S5_CH, 1, S5_NST)
    return dict(bbr=_s5_blockdiag(t(bbr), S5_GROUP, S5_STATE).astype(BF16),
                bbi=_s5_blockdiag(t(bbi), S5_GROUP, S5_STATE).astype(BF16),
                ccr=_s5_blockdiag(c4(w["s5_c_re"]), S5_STATE, S5_GROUP).astype(BF16),
                cci=_s5_blockdiag(c4(w["s5_c_im"]), S5_STATE, S5_GROUP).astype(BF16),
                abr=vec(abr), abi=vec(abi), apr=vec(apr), api=vec(api),
                dsk=w["s5_d"].reshape(S5_CH, 1, S5_CW), b_re=b_re, b_im=b_im)


def _layer_fwd(x, mem, w, cst, next_pack=None, rest=None):
    sv = {}
    g = lambda n: w[n].reshape(1, -1)
    res = norm_matmul(x, g("norm_mix"), w["w_in_t"], w["w_dt_t"], tm=TM, tn=1024, name="in_proj", w_transposed=True,
                      comm=gather_over_ici(rest[0], rest[1], ROWS_EARLY) if rest else None)
    (proj, h0, r0, dtraw), staged = res if rest else (res, None)
    s5m = _s5_mats(w, cst)
    comm = _combine(gather_to_sibling(staged[0], ROWS_EARLY) if rest else None,
                    gather_over_ici(*next_pack) if next_pack else None)
    (ys, csr, csi), landed = s5_fwd(proj, s5m["bbr"], s5m["bbi"], s5m["ccr"], s5m["cci"], s5m["abr"], s5m["abi"],
                                    s5m["apr"], s5m["api"], s5m["dsk"], name="s5_fwd", comm=comm)
    if rest:
        w = dict(w, **_pack_weights(landed[0]))
        landed = landed[1:]
    pack = w["pack"]
    (ymix, tglu), gathered = glu_fwd(ys, w["s5_w_glu"], tm=TM, name="glu_fwd",
                                     comm=None if next_pack is None else gather_to_sibling(landed[0]))
    ssdp = _ssd_params(w["ssd_conv_w"], w["ssd_conv_b"], w["ssd_dt_bias"], w["ssd_a_log"], w["ssd_d"], w["ssd_norm"])
    (ymix, hs), _ = ssd_fwd(proj, dtraw, ssdp, cst["tril"], ymix, name="ssd_fwd")
    x1 = matmul_res(ymix, pack, x, tm=TM, tn=1024, tk=1024, name="out_proj",
                    wspec=pl.BlockSpec((2, 512, 1024), lambda i, j, k: (k, OFF["w_out"] // 512, j)))
    q, h1, r1 = norm_matmul(x1, g("norm_xattn"), w["xa_wq"], tm=TM, tn=1024, name="q_proj")
    kv, hm, rm = norm_matmul(mem, g("norm_mem"), w["xa_wkv"], tm=mem.shape[0], tn=1024, name="kv_proj")
    o = attn_fwd(q, kv, tm=TM, name="attn_fwd")
    x2 = matmul_res(o, w["xa_wo"], x1, tm=TM, tn=1024, tk=1024, name="attn_out")
    f1, h2, r2 = norm_matmul(x2, g("norm_mlp"), pack, tm=TM, tn=1024, name="mlp_up", n_out=D_FF,
                             wspec=pl.BlockSpec((1, D_MODEL, 1024), lambda i, j: (j, OFF["mlp_w1"] // 1024, 0)))
    x3 = matmul_res(f1, pack, x2, act="relu2", tm=TM, tn=1024, tk=1024, name="mlp_down",
                    wspec=pl.BlockSpec((1, 1024, 1024), lambda i, j, k: (k, OFF["mlp_w2"] // 1024, j)))
    sv.update(x=x, proj=proj, h0=h0, r0=r0, dtraw=dtraw, s5m=s5m, ys=ys, csr=csr, csi=csi, tglu=tglu, ssdp=ssdp,
              hs=hs, ymix=ymix, x1=x1, q=q, h1=h1, r1=r1, kv=kv, hm=hm, rm=rm, o=o, x2=x2, f1=f1, h2=h2, r2=r2)
    return x3, sv, (gathered[0] if next_pack is not None else None), w


def _layer_bwd(dx3, mem, w, sv, cst, extra_small=None, reduce_hooks=None, early=None):
    gr = {}
    g = lambda n: w[n].reshape(1, -1)
    pack = w["pack"]
    pshape = (N_SHARD, PACK_ROWS, D_MODEL)
    ps = lambda rows, f: pl.BlockSpec((1, rows, 1024), f)
    ps4 = lambda rows, f: pl.BlockSpec((N_SHARD, rows, 1024), f)
    dh1 = matmul_nt(dx3, pack, epi="relu2bwd", epi_args=(sv["f1"],), tm=TM, tko=1024, tn=1024, name="mlp_down_dx",
                    wspec=ps(1024, lambda i, k, n: (k, OFF["mlp_w2"] // 1024, n)), k_out=D_FF,
                    comm=reduce_hooks.exchange() if reduce_hooks else None)
    if reduce_hooks:
        dh1, recv = dh1
        reduce_hooks.after_exchange(recv[0])
    gp = matmul_tn(sv["f1"], dx3, act="relu2", tk=1024, tn=1024, tt=TM, name="mlp_down_dw", pack_shape=pshape,
                   pack_spec=ps(1024, lambda k, n, t: (k, OFF["mlp_w2"] // 1024, 0)))
    gp = matmul_tn(sv["h2"], dh1, tk=1024, tn=1024, tt=TM, name="mlp_up_dw", pack=gp, pack_shape=pshape,
                   pack_spec=ps(1024, lambda k, n, t: (n, OFF["mlp_w1"] // 1024, 0)))
    dx2, gr["norm_mlp"] = matmul_nt(dh1, pack, epi="rmsbwd", epi_args=(sv["x2"], sv["r2"], g("norm_mlp"), dx3),
                                    tm=TM, tko=1024, tn=1024, name="mlp_up_dx", k_out=D_MODEL,
                                    wspec=ps(1024, lambda i, k, n: (n, OFF["mlp_w1"] // 1024, 0)))
    do = matmul_nt(dx2, w["xa_wo"], tm=TM, tko=1024, tn=1024, name="attn_out_dx")
    gp = matmul_tn(sv["o"], dx2, tk=1024, tn=1024, tt=TM, name="attn_out_dw", pack=gp, pack_shape=pshape,
                   pack_spec=ps4(256, lambda k, n, t: (0, OFF["xa_wo"] // 256, 0)))
    dq, dkv = attn_bwd(sv["q"], sv["kv"], do, tm=TM, name="attn_bwd")
    gp = matmul_tn(sv["h1"], dq, tk=1024, tn=1024, tt=TM, name="q_proj_dw", pack=gp, pack_shape=pshape,
                   pack_spec=ps4(256, lambda k, n, t: (0, OFF["xa_wq"] // 256, 0)))
    dx1, gr["norm_xattn"] = matmul_nt(dq, w["xa_wq"], epi="rmsbwd", epi_args=(sv["x1"], sv["r1"], g("norm_xattn"), dx2),
                                      tm=TM, tko=1024, tn=1024, name="q_proj_dx")
    M = mem.shape[0]
    gp = matmul_tn(sv["hm"], dkv, tk=1024, tn=1024, tt=M, name="kv_proj_dw", pack=gp, pack_shape=pshape,
                   pack_spec=ps4(256, lambda k, n, t: (0, OFF["xa_wk"] // 256 + n, 0)))
    _, gr["norm_mem"] = matmul_nt(dkv, w["xa_wkv"], epi="rmsbwd",
                                  epi_args=(mem, sv["rm"], g("norm_mem"), jnp.zeros_like(mem)),
                                  tm=M, tko=1024, tn=1024, name="kv_proj_dx")
    dymix = matmul_nt(dx1, pack, tm=TM, tko=1024, tn=1024, name="out_proj_dx", k_out=2 * D_MODEL,
                      wspec=pl.BlockSpec((2, 512, 1024), lambda i, k, n: (k, OFF["w_out"] // 512, n)))
    gp = matmul_tn(sv["ymix"], dx1, tk=2048, tn=1024, tt=TM, name="out_proj_dw", pack=gp, pack_shape=pshape,
                   pack_spec=ps4(512, lambda k, n, t: (0, OFF["w_out"] // 512, 0)))
    (dproj, ddtraw, dcw, dcb, ddtb, dal, ddsk, dnw), parts = ssd_bwd(
        sv["proj"], sv["dtraw"], sv["hs"], dymix, sv["ssdp"], cst["tril"], cst["triu"], cst["trils"], cst["headsum"],
        name="ssd_bwd", comm=reduce_hooks.scatter() if reduce_hooks else None)
    if reduce_hooks:
        reduce_hooks.after_scatter(parts[0])
    gr["ssd_conv_w"], gr["ssd_conv_b"] = dcw[:SSD_CONV], dcb[0]
    gr["ssd_dt_bias"], gr["ssd_a_log"], gr["ssd_d"] = ddtb[0, :SSD_HEADS], dal[0, :SSD_HEADS], ddsk[0, :SSD_HEADS]
    gr["ssd_norm"] = dnw[0]
    gp = glu_dw(dymix, sv["ys"], sv["tglu"], gp, tt=TM, name="glu_dw",
                pack_spec=ps4(256, lambda i: (0, OFF["s5_w_glu"] // 256, 0)))
    first = _Reduction(gp, early[0], early[1], None, None, ROWS_EARLY) if early else None
    comm = _combine(reduce_hooks.share() if reduce_hooks else None, first.exchange() if first else None)
    (dys,), outs = glu_bwd(dymix, sv["ys"], sv["tglu"], w["s5_w_glu"], tm=TM, name="glu_dx", comm=comm)
    if comm is not None:
        if reduce_hooks:
            reduce_hooks.after_share(outs[:2])
            outs = outs[2:]
        if first:
            first.gshards, first.smalls = reduce_hooks.gshards, reduce_hooks.smalls
            first.after_exchange(outs[0])
    s5m = sv["s5m"]
    (dproj, dbbr, dbbi, dccr, dcci, dd, dabr, dabi), parts = s5_bwd(
        sv["proj"], dys, sv["csr"], sv["csi"], s5m["bbr"], s5m["bbi"], s5m["ccr"], s5m["cci"], s5m["abr"], s5m["abi"],
        s5m["apr"], s5m["api"], s5m["dsk"], dproj, name="s5_bwd", comm=first.scatter() if first else None)
    if first:
        first.after_scatter(parts[0])
    tb = lambda v: _s5_blockdiag_extract(v, S5_GROUP, S5_STATE).transpose(0, 1, 3, 2).reshape(S5_GROUPS, -1)
    tc = lambda v: _s5_blockdiag_extract(v, S5_STATE, S5_GROUP).transpose(0, 1, 3, 2).reshape(S5_GROUPS, S5_GROUP, S5_STATE)
    gr["s5_c_re"], gr["s5_c_im"] = tc(dccr), tc(dcci)
    gr["s5_d"] = dd.reshape(S5_GROUPS, S5_GROUP)
    dar, dai, dld, dbr, dbi = s5_params_bwd(
        w["s5_a_re"], w["s5_a_im"], w["s5_log_dt"].reshape(S5_GROUPS, 1), s5m["b_re"], s5m["b_im"],
        dabr.reshape(S5_GROUPS, S5_STATE), dabi.reshape(S5_GROUPS, S5_STATE), tb(dbbr), tb(dbbi),
        cst["expand"], cst["expand_t"], name="s5_params_bwd")
    gr["s5_a_re"], gr["s5_a_im"], gr["s5_log_dt"] = dar, dai, dld[:, 0]
    gr["s5_b_re"] = dbr.reshape(S5_GROUPS, S5_STATE, S5_GROUP)
    gr["s5_b_im"] = dbi.reshape(S5_GROUPS, S5_STATE, S5_GROUP)
    wt_shape = (D_MAIN + D_DT_PAD, D_MODEL)
    dwt = matmul_tn(dproj, sv["h0"], tk=1024, tn=1024, tt=TM, name="in_proj_dw", pack_shape=wt_shape,
                    pack_spec=pl.BlockSpec((1024, 1024), lambda k, n, t: (k, 0)))
    dwt = matmul_tn(ddtraw, sv["h0"], tk=D_DT_PAD, tn=1024, tt=TM, name="in_proj_dt_dw", pack=dwt, pack_shape=wt_shape,
                    pack_spec=pl.BlockSpec((D_DT_PAD, 1024), lambda k, n, t: (D_MAIN // D_DT_PAD, 0)))
    dx0 = matmul_nt(dproj, w["w_in_t"], g2=ddtraw, w2=w["w_dt_t"], w_is_nk=True, epi="rmsbwd",
                    epi_args=(sv["x"], sv["r0"], g("norm_mix"), dx1), tm=TM, tko=1024, tn=1024, name="in_proj_dx",
                    comm=first.share() if first else None)
    if first:
        dx0, shared = dx0
        first.after_share(shared)
    dx0, gr["norm_mix"] = dx0
    gr = {k: (v[0] if k.startswith("norm_") else v) for k, v in gr.items()}
    for t in range(N_SHARD):
        shard_rows = lax.slice_in_dim(dwt, t * ROWS["w_in"], (t + 1) * ROWS["w_in"], axis=0)
        gp = lax.dynamic_update_slice(gp, shard_rows[None], (t, OFF["w_in"], 0))
    small = [gr[n].reshape(-1) for n in SMALL_L] + ([] if extra_small is None else [extra_small.reshape(-1)])
    small = jnp.concatenate(small)
    small = jnp.pad(small, (0, N_SHARD * SMALL_Q * D_MODEL - small.size)).reshape(N_SHARD, SMALL_Q, D_MODEL)
    gap = TAIL_OFF - OFF["w_in"] - ROWS["w_in"]
    gp = lax.dynamic_update_slice(gp, jnp.pad(small, ((0, 0), (gap, TAIL_ROWS - SMALL_Q), (0, 0))),
                                  (0, TAIL_OFF - gap, 0))
    return dx0, gp, first


def _local_step(x, mem, target, layers, norm_final):
    cst = _consts()
    saved = []
    for l in range(DEPTH):
        x, sv, _, _ = _layer_fwd(x, mem, layers[l], cst)
        saved.append(sv)
    loss, dx, dgf = loss_head(x, norm_final.reshape(1, -1), target, tm=TM, name="loss_head")
    packs = [None] * DEPTH
    for l in reversed(range(DEPTH)):
        dx, packs[l], _ = _layer_bwd(dx, mem, layers[l], saved[l], cst, extra_small=dgf[0] if l == DEPTH - 1 else None)
    return loss, dx, packs


def sum_halves(gpack, recv, c_idx, rows):
    first, count, tile, _ = rows
    half = count // 2
    nb = half // tile

    def body(c_ref, a_ref, b_ref, o_ref):
        o_ref[...] = (a_ref[...] + b_ref[...]).astype(BF16)

    blk = (1, tile, D_MODEL)
    return pl.pallas_call(
        body,
        grid_spec=pltpu.PrefetchScalarGridSpec(
            num_scalar_prefetch=1, grid=(N_SHARD, nb),
            in_specs=[pl.BlockSpec(blk, lambda t, i, c_ref: (t, first // tile + c_ref[0] * nb + i, 0)),
                      pl.BlockSpec(blk, lambda t, i, c_ref: (t, i, 0))],
            out_specs=pl.BlockSpec(blk, lambda t, i, c_ref: (t, i, 0))),
        out_shape=jax.ShapeDtypeStruct((N_SHARD, half, D_MODEL), BF16),
        compiler_params=_cp("parallel", "parallel"), name="sum_halves")(c_idx, gpack, recv)


def sum_chips(parts, csum, gshards, l, place_idx, rows):
    first, count, tile, _ = rows
    nb = count // 2 // tile

    def body(pi_ref, p0, p1, p2, p3, own, g_ref, o_ref):
        s = pi_ref[0]
        vals = [jnp.where(s == k, own[0], p[0]).astype(F32) for k, p in enumerate((p0, p1, p2, p3))]
        o_ref[0] = ((vals[0] + vals[1]) + vals[2]) + vals[3]

    blk = (1, tile, D_MODEL)
    part_spec = lambda k: pl.BlockSpec(blk, lambda i, pi_ref: (jnp.where(pi_ref[0] == k, (k + 1) % N_SHARD, k), i, 0))
    return pl.pallas_call(
        body,
        grid_spec=pltpu.PrefetchScalarGridSpec(
            num_scalar_prefetch=1, grid=(nb,),
            in_specs=[part_spec(k) for k in range(N_SHARD)]
                     + [pl.BlockSpec(blk, lambda i, pi_ref: (pi_ref[0], i, 0)), _ANY],
            out_specs=pl.BlockSpec(blk, lambda i, pi_ref: (l, first // tile + pi_ref[1] * nb + i, 0))),
        out_shape=jax.ShapeDtypeStruct(gshards.shape, F32), input_output_aliases={6: 0},
        compiler_params=_cp("parallel"), name="sum_chips")(place_idx, parts, parts, parts, parts, csum, gshards)


class _Reduction:
    def __init__(self, gpack, layer, place_idx, gshards, smalls, rows=ROWS_ALL):
        self.gpack, self.layer, self.place_idx, self.gshards, self.smalls = gpack, layer, place_idx, gshards, smalls
        self.rows = rows

    def exchange(self):
        return exchange_halves(self.gpack, self.rows)

    def after_exchange(self, recv):
        self.csum = sum_halves(self.gpack, recv, self.place_idx[1:], self.rows)

    def scatter(self):
        return scatter_chips(self.csum)

    def after_scatter(self, parts):
        self.gshards = sum_chips(parts, self.csum, self.gshards, self.layer, self.place_idx, self.rows)

    def share(self):
        return share_reduced(self.gshards, self.smalls, self.layer, self.rows)

    def after_share(self, shared):
        self.gshards, self.smalls = shared

    def run_alone(self):
        self.after_exchange(_comm_only(self.exchange())[0])
        self.after_scatter(_comm_only(self.scatter())[0])
        self.after_share(_comm_only(self.share()))
        return self.gshards, self.smalls


def adamw(w, g, m, v, *, name):
    shape = w.shape
    cols = shape[-1]
    rows = w.size // cols
    tr = 512 if rows % 512 == 0 else rows
    c1 = 1.0 / (1.0 - ADAM_B1 ** ADAM_STEP)
    c2 = 1.0 / (1.0 - ADAM_B2 ** ADAM_STEP)

    def body(w_ref, g_ref, m_ref, v_ref, d_ref, nm_ref, nv_ref):
        gv = g_ref[...]
        nm = ADAM_B1 * m_ref[...] + (1.0 - ADAM_B1) * gv
        nv = ADAM_B2 * v_ref[...] + (1.0 - ADAM_B2) * (gv * gv)
        d_ref[...] = -ADAM_LR * ((nm * c1) / (jnp.sqrt(nv * c2) + ADAM_EPS) + ADAM_WD * w_ref[...])
        nm_ref[...] = nm
        nv_ref[...] = nv

    spec = pl.BlockSpec((tr, cols), lambda i: (i, 0))
    sds = jax.ShapeDtypeStruct((rows, cols), F32)
    outs = pl.pallas_call(body, grid=(rows // tr,), in_specs=[spec] * 4, out_specs=[spec] * 3, out_shape=[sds] * 3,
                          compiler_params=_cp("parallel"), name=name)(
                              *[a.reshape(rows, cols) for a in (w, g, m, v)])
    return [o.reshape(shape) for o in outs]


def _own_pack(wts, l):
    rows = [wts[n][l].T if n == "w_in" else wts[n][l].reshape(ROWS[n], D_MODEL) for n in sorted(OFF, key=OFF.get)]
    cw = wts["ssd_conv_w"][l].reshape(-1)
    hi = lax.reduce_precision(cw, 8, 7)
    mid = lax.reduce_precision(cw - hi, 8, 7)
    lo = lax.reduce_precision(cw - hi - mid, 8, 7)
    conv = jnp.pad(jnp.concatenate([hi, mid, lo]), (0, CONV_ROWS * D_MODEL - 3 * cw.size)).reshape(CONV_ROWS, D_MODEL)
    gap = jnp.zeros((TAIL_OFF - OFF["w_in"] - ROWS["w_in"], D_MODEL), F32)
    rest = jnp.zeros((TAIL_ROWS - CONV_ROWS, D_MODEL), F32)
    return jnp.concatenate(rows + [gap, conv, rest], axis=0).astype(BF16)


def _square(gathered, n):
    return gathered[:, OFF[n]:OFF[n] + ROWS[n]].reshape(N_SHARD * ROWS[n], D_MODEL)


def _pack_weights(gathered):
    w = {"pack": gathered}
    w["s5_w_glu"], w["xa_wq"], w["xa_wo"] = _square(gathered, "s5_w_glu"), _square(gathered, "xa_wq"), _square(gathered, "xa_wo")
    w["xa_wkv"] = jnp.concatenate([_square(gathered, "xa_wk"), _square(gathered, "xa_wv")], axis=1)
    return w


def _layer_weights(gathered, wts, l, late_only=False):
    w = {n: wts[n][l] for n in SMALL_L if n != "ssd_conv_w"}
    if not late_only:
        w.update(_pack_weights(gathered))
    w_in_t = _square(gathered, "w_in")
    w["w_in_t"] = w_in_t[:D_MAIN]
    w["w_dt_t"] = jnp.pad(w_in_t[D_MAIN:], ((0, D_DT_PAD - SSD_HEADS), (0, 0)))
    per = SSD_CONV * D_XBC // N_SHARD
    cw = gathered[:, TAIL_OFF:TAIL_OFF + CONV_ROWS].astype(F32).reshape(N_SHARD, -1)[:, :3 * per]
    cw = cw.reshape(N_SHARD, 3, SSD_CONV, D_XBC // N_SHARD)
    cw = (cw[:, 0] + cw[:, 1]) + cw[:, 2]
    w["ssd_conv_w"] = cw.transpose(1, 0, 2).reshape(SSD_CONV, D_XBC)
    return w


def kernel(x, mem, norm_mix, w_in, s5_a_re, s5_a_im, s5_log_dt, s5_b_re, s5_b_im, s5_c_re, s5_c_im, s5_d, s5_w_glu, ssd_conv_w, ssd_conv_b, ssd_dt_bias, ssd_a_log, ssd_d, ssd_norm, w_out, norm_xattn, norm_mem, xa_wq, xa_wk, xa_wv, xa_wo, norm_mlp, mlp_w1, mlp_w2, norm_final, loss_target, m_norm_mix, m_w_in, m_s5_a_re, m_s5_a_im, m_s5_log_dt, m_s5_b_re, m_s5_b_im, m_s5_c_re, m_s5_c_im, m_s5_d, m_s5_w_glu, m_ssd_conv_w, m_ssd_conv_b, m_ssd_dt_bias, m_ssd_a_log, m_ssd_d, m_ssd_norm, m_w_out, m_norm_xattn, m_norm_mem, m_xa_wq, m_xa_wk, m_xa_wv, m_xa_wo, m_norm_mlp, m_mlp_w1, m_mlp_w2, m_norm_final, v_norm_mix, v_w_in, v_s5_a_re, v_s5_a_im, v_s5_log_dt, v_s5_b_re, v_s5_b_im, v_s5_c_re, v_s5_c_im, v_s5_d, v_s5_w_glu, v_ssd_conv_w, v_ssd_conv_b, v_ssd_dt_bias, v_ssd_a_log, v_ssd_d, v_ssd_norm, v_w_out, v_norm_xattn, v_norm_mem, v_xa_wq, v_xa_wk, v_xa_wv, v_xa_wo, v_norm_mlp, v_mlp_w1, v_mlp_w2, v_norm_final):
    names = ("norm_mix", "w_in", "s5_a_re", "s5_a_im", "s5_log_dt", "s5_b_re", "s5_b_im", "s5_c_re", "s5_c_im", "s5_d",
             "s5_w_glu", "ssd_conv_w", "ssd_conv_b", "ssd_dt_bias", "ssd_a_log", "ssd_d", "ssd_norm", "w_out",
             "norm_xattn", "norm_mem", "xa_wq", "xa_wk", "xa_wv", "xa_wo", "norm_mlp", "mlp_w1", "mlp_w2", "norm_final")
    loc = locals()
    wts = {n: loc[n] for n in names}
    mom = {n: loc["m_" + n] for n in names}
    var = {n: loc["v_" + n] for n in names}
    shard = 2 * lax.axis_index("x") + lax.axis_index("y")
    core = lax.axis_index("c")

    cst = _consts()
    place_idx = jnp.stack([shard, core]).astype(jnp.int32)
    h, mem0 = x[0], mem[0]

    own0 = _own_pack(wts, 0)
    staged = _comm_only(gather_over_ici(own0, place_own(own0, place_idx[:1]), ROWS_LATE))[0]
    gathered = _comm_only(gather_to_sibling(staged, ROWS_LATE))[0]
    layers, saved = [], []
    for l in range(DEPTH):
        nxt = None
        if l + 1 < DEPTH:
            own = _own_pack(wts, l + 1)
            nxt = (own, place_own(own, place_idx[:1]))
        h, sv, gathered, w = _layer_fwd(h, mem0, _layer_weights(gathered, wts, l, late_only=l == 0), cst, next_pack=nxt,
                                        rest=(own0, gathered) if l == 0 else None)
        layers.append(w)
        saved.append(sv)
    loss, dx, dgf = loss_head(h, norm_final.reshape(1, -1), loss_target[0], tm=TM, name="loss_head")

    gshards = jnp.zeros((DEPTH, PACK_ROWS, D_MODEL), F32)
    smalls = jnp.zeros((DEPTH, N_SHARD, TAIL_ROWS, D_MODEL), F32)
    pending = None
    for l in reversed(range(DEPTH)):
        dx, gpack, first = _layer_bwd(dx, mem0, layers[l], saved[l], cst, extra_small=dgf[0] if l == DEPTH - 1 else None,
                                      reduce_hooks=pending, early=(l, place_idx) if l == 0 else None)
        if pending is not None:
            gshards, smalls = pending.gshards, pending.smalls
        pending = _Reduction(gpack, l, place_idx, gshards, smalls)
    gshards, smalls = _Reduction(gpack, 0, place_idx, first.gshards, first.smalls, ROWS_LATE).run_alone()

    g = {n: gshards[:, OFF[n]:OFF[n] + ROWS[n]].reshape(wts[n].shape) for n in OFF if n != "w_in"}
    g["w_in"] = gshards[:, OFF["w_in"]:OFF["w_in"] + ROWS["w_in"]].transpose(0, 2, 1)
    small_red = smalls[:, :, :SMALL_Q].reshape(DEPTH, -1)
    off = 0
    for n in SMALL_L:
        shape = (SSD_CONV, D_XBC) if n == "ssd_conv_w" else wts[n].shape[1:]
        size = math.prod(shape)
        g[n] = small_red[:, off:off + size].reshape((DEPTH,) + shape)
        off += size
    g["norm_final"] = small_red[DEPTH - 1, off:off + D_MODEL]
    g["ssd_conv_w"] = lax.dynamic_slice_in_dim(g["ssd_conv_w"], shard * (D_XBC // N_SHARD), D_XBC // N_SHARD, axis=2)

    deltas, new_m, new_v = {}, {}, {}
    for n in names:
        deltas[n], new_m[n], new_v[n] = adamw(wts[n], g[n], mom[n], var[n], name="adamw_" + n)
    loss_all = lax.psum(loss[0, 0], ("x", "y", "c"))
    return (loss_all, dx[None], *[g[n] for n in names], *[deltas[n] for n in names], *[new_m[n] for n in names],
            *[new_v[n] for n in names])
```

```python
import functools
import math

import jax
import jax.numpy as jnp
import numpy as np
from jax import lax
from jax.experimental import pallas as pl
from jax.experimental.pallas import tpu as pltpu

F32 = jnp.float32
BF16 = jnp.bfloat16
HIGHEST = lax.Precision.HIGHEST

D_MODEL = 1024
DEPTH = 4
D_S5 = 1024
D_SSD = 1024
S5_GROUP = 16
S5_GROUPS = 64
S5_STATE = 64
SSD_HEADS = 16
SSD_HEADDIM = 64
SSD_NGROUPS = 4
SSD_STATE = 128
SSD_CONV = 4
SSD_CHUNK = 128
D_XBC = 2048
D_MAIN = 4096
D_DT_PAD = 128
XA_HEADS = 4
XA_HEAD_DIM = 256
D_FF = 4096
EPS = 1e-5
ADAM_LR, ADAM_B1, ADAM_B2, ADAM_EPS, ADAM_WD, ADAM_STEP = 0.001, 0.9, 0.999, 1e-08, 0.01, 10

VMEM_LIMIT = 56 * 1024 * 1024
MESH_T = pl.DeviceIdType.MESH


def _cp(*sem):
    return pltpu.CompilerParams(dimension_semantics=tuple(sem) if sem else None, vmem_limit_bytes=VMEM_LIMIT)


_ANY = pl.BlockSpec(memory_space=pl.ANY)


class Comm:
    def __init__(self, name, inputs, out_shapes, sems, start, wait, aliases=None):
        self.name, self.inputs, self.out_shapes, self.sems = name, list(inputs), list(out_shapes), list(sems)
        self.start, self.wait, self.aliases = start, wait, dict(aliases or {})


def _call(body, *, grid, in_specs, out_specs, out_shape, args, scratch_shapes=(), sem, name, comm=None, aliases=None):
    in_specs, out_specs, out_shape = list(in_specs), list(out_specs), list(out_shape)
    scratch_shapes = list(scratch_shapes)
    aliases = dict(aliases or {})
    if comm is None:
        res = pl.pallas_call(body, grid=grid, in_specs=in_specs, out_specs=out_specs, out_shape=out_shape,
                             scratch_shapes=scratch_shapes, compiler_params=_cp(*sem), name=name,
                             input_output_aliases=aliases)(*args)
        return list(res), []
    n_in, n_out, n_scr = len(in_specs), len(out_specs), len(scratch_shapes)
    c_in, c_out = len(comm.inputs), len(comm.out_shapes)

    def wrapped(*refs):
        a, refs = refs[:n_in], refs[n_in:]
        ci, refs = refs[:c_in], refs[c_in:]
        o, refs = refs[:n_out], refs[n_out:]
        co, refs = refs[:c_out], refs[c_out:]
        s, cs = refs[:n_scr], refs[n_scr:]
        first = functools.reduce(jnp.logical_and, [pl.program_id(d) == 0 for d in range(len(grid))])
        last = functools.reduce(jnp.logical_and, [pl.program_id(d) == grid[d] - 1 for d in range(len(grid))])

        @pl.when(first)
        def _():
            comm.start(ci, co, cs)

        body(*a, *o, *s)

        @pl.when(last)
        def _():
            comm.wait(ci, co, cs)

    for i, j in comm.aliases.items():
        aliases[n_in + i] = n_out + j
    res = pl.pallas_call(wrapped, grid=grid, in_specs=in_specs + [_ANY] * c_in, out_specs=out_specs + [_ANY] * c_out,
                         out_shape=out_shape + comm.out_shapes, scratch_shapes=scratch_shapes + comm.sems,
                         compiler_params=_cp(*(("arbitrary",) * len(grid))), name=name + "_" + comm.name,
                         input_output_aliases=aliases)(*args, *comm.inputs)
    return list(res[:n_out]), list(res[n_out:])


def _combine(a, b):
    if a is None or b is None:
        return a or b
    ni, no, ns = len(a.inputs), len(a.out_shapes), len(a.sems)

    def start(ci, co, cs):
        a.start(ci[:ni], co[:no], cs[:ns])
        b.start(ci[ni:], co[no:], cs[ns:])

    def wait(ci, co, cs):
        a.wait(ci[:ni], co[:no], cs[:ns])
        b.wait(ci[ni:], co[no:], cs[ns:])

    aliases = dict(a.aliases)
    aliases.update({ni + i: no + j for i, j in b.aliases.items()})
    return Comm(a.name + "_" + b.name, a.inputs + b.inputs, a.out_shapes + b.out_shapes, a.sems + b.sems, start, wait,
                aliases)


def _comm_only(comm):
    def body(*refs):
        ci, refs = refs[:len(comm.inputs)], refs[len(comm.inputs):]
        co, cs = refs[:len(comm.out_shapes)], refs[len(comm.out_shapes):]
        comm.start(ci, co, cs)
        comm.wait(ci, co, cs)

    res = pl.pallas_call(body, in_specs=[_ANY] * len(comm.inputs), out_specs=[_ANY] * len(comm.out_shapes),
                         out_shape=comm.out_shapes, scratch_shapes=comm.sems, name=comm.name,
                         input_output_aliases=comm.aliases)(*comm.inputs)
    return list(res)


def _dot(a, b):
    return jnp.dot(a, b, preferred_element_type=F32)


def _dot_nt(a, b):
    return lax.dot_general(a, b, (((1,), (1,)), ((), ())), preferred_element_type=F32)


def _dot_tn(a, b):
    return lax.dot_general(a, b, (((0,), (0,)), ((), ())), preferred_element_type=F32)


def _dot_hi(a, b):
    return jnp.dot(a, b, precision=HIGHEST, preferred_element_type=F32)


def _gelu(x):
    c = math.sqrt(2.0 / math.pi)
    return 0.5 * x * (1.0 + jnp.tanh(c * (x + 0.044715 * x * x * x)))


def _gelu_grad(x):
    c = math.sqrt(2.0 / math.pi)
    t = jnp.tanh(c * (x + 0.044715 * x * x * x))
    return 0.5 * (1.0 + t) + 0.5 * x * (1.0 - t * t) * c * (1.0 + 3 * 0.044715 * x * x)


def _sigmoid(x):
    return 1.0 / (1.0 + jnp.exp(-x))


def _act(a, act):
    if act is None:
        return a.astype(BF16)
    a = a.astype(F32)
    if act == "relu2":
        a = jnp.maximum(a, 0.0)
        return (a * a).astype(BF16)
    if act == "gelu":
        return _gelu(a).astype(BF16)
    raise ValueError(act)


def _pack_block(ref):
    return ref[...].reshape(-1, ref.shape[-1])


def norm_matmul(x, g, w, w2=None, *, tm, tn, name, wspec=None, n_out=None, w_transposed=False, comm=None):
    T, D = x.shape
    N = n_out if wspec is not None else (w.shape[0] if w_transposed else w.shape[1])
    wget = (lambda r: r[...]) if wspec is None else _pack_block
    mm = _dot_nt if w_transposed else _dot
    has2 = w2 is not None

    def body(x_ref, g_ref, w_ref, *rest):
        if has2:
            w2_ref, o_ref, h_ref, r_ref, o2_ref = rest
        else:
            o_ref, h_ref, r_ref = rest
        j = pl.program_id(1)

        @pl.when(j == 0)
        def _():
            xv = x_ref[...]
            r = lax.rsqrt(jnp.mean(xv * xv, axis=-1, keepdims=True) + EPS)
            h = (xv * r * g_ref[...]).astype(BF16)
            h_ref[...] = h
            r_ref[...] = r
            if has2:
                o2_ref[...] = mm(h, w2_ref[...])

        o_ref[...] = mm(h_ref[...], wget(w_ref)).astype(o_ref.dtype)

    if wspec is None:
        wspec = pl.BlockSpec((tn, D), lambda i, j: (j, 0)) if w_transposed else pl.BlockSpec((D, tn), lambda i, j: (0, j))
    in_specs = [pl.BlockSpec((tm, D), lambda i, j: (i, 0)), pl.BlockSpec((1, D), lambda i, j: (0, 0)), wspec]
    out_shape = [jax.ShapeDtypeStruct((T, N), BF16), jax.ShapeDtypeStruct((T, D), BF16),
                 jax.ShapeDtypeStruct((T, 1), F32)]
    out_specs = [pl.BlockSpec((tm, tn), lambda i, j: (i, j)), pl.BlockSpec((tm, D), lambda i, j: (i, 0)),
                 pl.BlockSpec((tm, 1), lambda i, j: (i, 0))]
    args = [x, g, w]
    if has2:
        in_specs.append(pl.BlockSpec(w2.shape, lambda i, j: (0, 0)))
        out_shape.append(jax.ShapeDtypeStruct((T, D_DT_PAD), F32))
        out_specs.append(pl.BlockSpec((tm, D_DT_PAD), lambda i, j: (i, 0)))
        args.append(w2)
    res, cres = _call(body, grid=(T // tm, N // tn), in_specs=in_specs, out_specs=out_specs, out_shape=out_shape,
                      sem=("parallel", "arbitrary"), name=name, args=args, comm=comm)
    return res if comm is None else (res, cres)


def matmul_res(a, w, r, *, act=None, tm, tn, tk, name, wspec=None):
    T, K = a.shape
    N = r.shape[1]
    wget = (lambda r_: r_[...]) if wspec is None else _pack_block
    nk = K // tk

    def body(a_ref, w_ref, r_ref, o_ref):
        k = pl.program_id(2)

        @pl.when(k == 0)
        def _():
            o_ref[...] = r_ref[...]

        o_ref[...] += _dot(_act(a_ref[...], act), wget(w_ref))

    return pl.pallas_call(
        body, grid=(T // tm, N // tn, nk),
        in_specs=[pl.BlockSpec((tm, tk), lambda i, j, k: (i, k)),
                  pl.BlockSpec((tk, tn), lambda i, j, k: (k, j)) if wspec is None else wspec,
                  pl.BlockSpec((tm, tn), lambda i, j, k: (i, j))],
        out_specs=pl.BlockSpec((tm, tn), lambda i, j, k: (i, j)),
        out_shape=jax.ShapeDtypeStruct((T, N), F32),
        compiler_params=_cp("parallel", "parallel", "arbitrary"), name=name)(a, w, r)


def glu_fwd(ys, w, *, tm, name, comm=None):
    T, N = ys.shape

    def body(y_ref, w_ref, o_ref, t_ref):
        a = _gelu(y_ref[...].astype(F32))
        t = _dot(a.astype(BF16), w_ref[...])
        o_ref[...] = (a * _sigmoid(t)).astype(BF16)
        t_ref[...] = t.astype(BF16)

    return _call(
        body, grid=(T // tm,),
        in_specs=[pl.BlockSpec((tm, N), lambda i: (i, 0)), pl.BlockSpec((N, N), lambda i: (0, 0))],
        out_specs=[pl.BlockSpec((tm, N), lambda i: (i, 0)), pl.BlockSpec((tm, N), lambda i: (i, 0))],
        out_shape=[jax.ShapeDtypeStruct((T, 2 * N), BF16), jax.ShapeDtypeStruct((T, N), BF16)],
        sem=("parallel",), name=name, args=(ys, w), comm=comm)


def glu_bwd(dymix, ys, t, w, *, tm, name, comm=None):
    T, N = ys.shape

    def body(d_ref, y_ref, t_ref, w_ref, dys_ref):
        d = d_ref[...].astype(F32)
        ysv = y_ref[...].astype(F32)
        s = _sigmoid(t_ref[...].astype(F32))
        dt = (d * _gelu(ysv) * s * (1.0 - s)).astype(BF16)
        dys_ref[...] = ((d * s + _dot_nt(dt, w_ref[...])) * _gelu_grad(ysv)).astype(BF16)

    spec = pl.BlockSpec((tm, N), lambda i: (i, 0))
    return _call(body, grid=(T // tm,), in_specs=[spec, spec, spec, pl.BlockSpec((N, N), lambda i: (0, 0))],
                 out_specs=[spec], out_shape=[jax.ShapeDtypeStruct((T, N), BF16)], sem=("parallel",), name=name,
                 args=(dymix, ys, t, w), comm=comm)


def glu_dw(dymix, ys, t, pack, *, pack_spec, tt, name):
    T, N = ys.shape

    def body(d_ref, y_ref, t_ref, pack_ref, o_ref):
        i = pl.program_id(0)
        a = _gelu(y_ref[...].astype(F32))
        s = _sigmoid(t_ref[...].astype(F32))
        dt = (d_ref[...].astype(F32) * a * s * (1.0 - s)).astype(BF16)
        part = _dot_tn(a.astype(BF16), dt).reshape(o_ref.shape)

        @pl.when(i == 0)
        def _():
            o_ref[...] = part

        @pl.when(i > 0)
        def _():
            o_ref[...] += part

    spec = pl.BlockSpec((tt, N), lambda i: (i, 0))
    return pl.pallas_call(body, grid=(T // tt,), in_specs=[spec, spec, spec, _ANY], out_specs=pack_spec,
                          out_shape=jax.ShapeDtypeStruct(pack.shape, F32), input_output_aliases={3: 0},
                          compiler_params=_cp("arbitrary"), name=name)(dymix, ys, t, pack)


def matmul_nt(g, w, *, epi=None, epi_args=(), g2=None, w2=None, tm, tko, tn, out_dtype=BF16, name, wspec=None,
              k_out=None, comm=None, w_is_nk=False):
    T, N = g.shape
    K = k_out if wspec is not None else (w.shape[1] if w_is_nk else w.shape[0])
    wget = (lambda r_: r_[...]) if wspec is None else _pack_block
    mm = _dot if w_is_nk else _dot_nt
    nn = N // tn
    has2 = g2 is not None
    rms = epi == "rmsbwd"
    if rms:
        assert tko == K
    n_epi = len(epi_args)

    def body(*refs):
        g_ref, w_ref = refs[0], refs[1]
        pos = 2
        if has2:
            g2_ref, w2_ref = refs[2], refs[3]
            pos = 4
        e_refs = refs[pos:pos + n_epi]
        pos += n_epi
        o_ref = refs[pos]
        pos += 1
        if rms:
            dg_ref = refs[pos]
            pos += 1
        acc_ref = refs[pos]
        i = pl.program_id(0)
        n = pl.program_id(2)
        part = mm(g_ref[...].astype(BF16), wget(w_ref))

        @pl.when(n == 0)
        def _():
            acc_ref[...] = part

        @pl.when(n > 0)
        def _():
            acc_ref[...] += part

        @pl.when(n == nn - 1)
        def _():
            acc = acc_ref[...]
            if has2:
                acc = acc + mm(g2_ref[...].astype(BF16), w2_ref[...])
            if epi is None:
                o_ref[...] = acc.astype(o_ref.dtype)
            elif epi == "relu2bwd":
                h1 = e_refs[0][...].astype(F32)
                o_ref[...] = (acc * 2.0 * jnp.maximum(h1, 0.0)).astype(o_ref.dtype)
            elif epi == "glubwd":
                da1 = e_refs[0][...].astype(F32)
                ys = e_refs[1][...].astype(F32)
                o_ref[...] = ((da1 + acc) * _gelu_grad(ys)).astype(o_ref.dtype)
            elif epi == "rmsbwd":
                xv, rs, gain, rv = e_refs[0][...], e_refs[1][...], e_refs[2][...], e_refs[3][...]
                xhat = xv * rs
                gd = acc * gain
                o_ref[...] = rv + rs * (gd - xhat * jnp.mean(gd * xhat, axis=-1, keepdims=True))
                part_g = jnp.sum(acc * xhat, axis=0, keepdims=True)

                @pl.when(i == 0)
                def _():
                    dg_ref[...] = part_g

                @pl.when(i > 0)
                def _():
                    dg_ref[...] += part_g

    if wspec is None:
        wspec = (pl.BlockSpec((tn, tko), lambda i, k, n: (n, k)) if w_is_nk
                 else pl.BlockSpec((tko, tn), lambda i, k, n: (k, n)))
    in_specs = [pl.BlockSpec((tm, tn), lambda i, k, n: (i, n)), wspec]
    args = [g, w]
    if has2:
        n2 = g2.shape[1]
        in_specs += [pl.BlockSpec((tm, n2), lambda i, k, n: (i, 0)),
                     pl.BlockSpec((n2, tko), lambda i, k, n: (0, k)) if w_is_nk
                     else pl.BlockSpec((tko, n2), lambda i, k, n: (k, 0))]
        args += [g2, w2]
    if epi == "relu2bwd" or epi == "glubwd":
        in_specs += [pl.BlockSpec((tm, tko), lambda i, k, n: (i, k))] * n_epi
    elif rms:
        in_specs += [pl.BlockSpec((tm, K), lambda i, k, n: (i, 0)), pl.BlockSpec((tm, 1), lambda i, k, n: (i, 0)),
                     pl.BlockSpec((1, K), lambda i, k, n: (0, 0)), pl.BlockSpec((tm, K), lambda i, k, n: (i, 0))]
    args += list(epi_args)
    out_shape = [jax.ShapeDtypeStruct((T, K), F32 if rms else out_dtype)]
    out_specs = [pl.BlockSpec((tm, tko), lambda i, k, n: (i, k))]
    if rms:
        out_shape.append(jax.ShapeDtypeStruct((1, K), F32))
        out_specs.append(pl.BlockSpec((1, K), lambda i, k, n: (0, 0)))
    sem = ("arbitrary",) * 3 if rms else ("parallel", "parallel", "arbitrary")
    res, cres = _call(body, grid=(T // tm, K // tko, nn), in_specs=in_specs, out_specs=out_specs, out_shape=out_shape,
                      scratch_shapes=[pltpu.VMEM((tm, tko), F32)], sem=sem, name=name, args=args, comm=comm)
    res = res if rms else res[0]
    return res if comm is None else (res, cres)


def matmul_tn(a, g, *, act=None, tk, tn, tt, name, pack=None, pack_spec=None, pack_shape=None):
    T, K = a.shape
    N = g.shape[1]
    to_pack = pack_spec is not None

    def body(a_ref, g_ref, *rest):
        o_ref = rest[-1]
        t = pl.program_id(2)
        part = _dot_tn(_act(a_ref[...], act), g_ref[...].astype(BF16))
        part = part.reshape(o_ref.shape)

        @pl.when(t == 0)
        def _():
            o_ref[...] = part

        @pl.when(t > 0)
        def _():
            o_ref[...] += part

    in_specs = [pl.BlockSpec((tt, tk), lambda k, n, t: (t, k)), pl.BlockSpec((tt, tn), lambda k, n, t: (t, n))]
    args = [a, g]
    aliases = {}
    if pack is not None:
        in_specs.append(_ANY)
        args.append(pack)
        aliases = {2: 0}
    return pl.pallas_call(
        body, grid=(K // tk, N // tn, T // tt), in_specs=in_specs,
        out_specs=pack_spec if to_pack else pl.BlockSpec((tk, tn), lambda k, n, t: (k, n)),
        out_shape=jax.ShapeDtypeStruct(pack_shape if to_pack else (K, N), F32), input_output_aliases=aliases,
        compiler_params=_cp("parallel", "parallel", "arbitrary"), name=name)(*args)


def attn_fwd(q, kv, *, tm, name):
    T = q.shape[0]
    M = kv.shape[0]
    scale = XA_HEAD_DIM ** -0.5

    def body(q_ref, kv_ref, o_ref):
        for h in range(XA_HEADS):
            sl = slice(h * XA_HEAD_DIM, (h + 1) * XA_HEAD_DIM)
            kh = kv_ref[:, h * XA_HEAD_DIM:(h + 1) * XA_HEAD_DIM]
            vh = kv_ref[:, D_MODEL + h * XA_HEAD_DIM:D_MODEL + (h + 1) * XA_HEAD_DIM]
            s = _dot_nt(q_ref[:, sl], kh) * scale
            s = s - jnp.max(s, axis=-1, keepdims=True)
            p = jnp.exp(s)
            p = p / jnp.sum(p, axis=-1, keepdims=True)
            o_ref[:, sl] = _dot(p.astype(BF16), vh).astype(BF16)

    return pl.pallas_call(
        body, grid=(T // tm,),
        in_specs=[pl.BlockSpec((tm, D_MODEL), lambda i: (i, 0)), pl.BlockSpec((M, 2 * D_MODEL), lambda i: (0, 0))],
        out_specs=pl.BlockSpec((tm, D_MODEL), lambda i: (i, 0)),
        out_shape=jax.ShapeDtypeStruct((T, D_MODEL), BF16),
        compiler_params=_cp("parallel"), name=name)(q, kv)


def attn_bwd(q, kv, do, *, tm, name):
    T = q.shape[0]
    M = kv.shape[0]
    scale = XA_HEAD_DIM ** -0.5

    def body(q_ref, kv_ref, do_ref, dq_ref, dkv_ref):
        i = pl.program_id(0)

        @pl.when(i == 0)
        def _():
            dkv_ref[...] = jnp.zeros_like(dkv_ref)

        for h in range(XA_HEADS):
            sl = slice(h * XA_HEAD_DIM, (h + 1) * XA_HEAD_DIM)
            slv = slice(D_MODEL + h * XA_HEAD_DIM, D_MODEL + (h + 1) * XA_HEAD_DIM)
            qh = q_ref[:, sl]
            kh = kv_ref[:, sl]
            vh = kv_ref[:, slv]
            doh = do_ref[:, sl]
            s = _dot_nt(qh, kh) * scale
            s = s - jnp.max(s, axis=-1, keepdims=True)
            p = jnp.exp(s)
            p = p / jnp.sum(p, axis=-1, keepdims=True)
            pb = p.astype(BF16)
            dkv_ref[:, slv] += _dot_tn(pb, doh)
            dp = _dot_nt(doh, vh)
            ds = (p * (dp - jnp.sum(dp * p, axis=-1, keepdims=True)) * scale).astype(BF16)
            dq_ref[:, sl] = _dot(ds, kh).astype(BF16)
            dkv_ref[:, sl] += _dot_tn(ds, qh)

    spec = pl.BlockSpec((tm, D_MODEL), lambda i: (i, 0))
    kvspec = pl.BlockSpec((M, 2 * D_MODEL), lambda i: (0, 0))
    return pl.pallas_call(
        body, grid=(T // tm,), in_specs=[spec, kvspec, spec], out_specs=[spec, kvspec],
        out_shape=[jax.ShapeDtypeStruct((T, D_MODEL), BF16), jax.ShapeDtypeStruct((M, 2 * D_MODEL), F32)],
        compiler_params=_cp("arbitrary"), name=name)(q, kv, do)


def loss_head(x, g, target, *, tm, name):
    T, D = x.shape

    def body(x_ref, g_ref, t_ref, l_ref, dx_ref, dg_ref):
        i = pl.program_id(0)
        xv = x_ref[...]
        gain = g_ref[...]
        r = lax.rsqrt(jnp.mean(xv * xv, axis=-1, keepdims=True) + EPS)
        xhat = xv * r
        err = xhat * gain - t_ref[...]
        part_l = jnp.full((1, 128), 0.5 / D, F32) * jnp.sum(err * err)
        dy = err * (1.0 / D)
        gd = dy * gain
        dx_ref[...] = r * (gd - xhat * jnp.mean(gd * xhat, axis=-1, keepdims=True))
        part_g = jnp.sum(dy * xhat, axis=0, keepdims=True)

        @pl.when(i == 0)
        def _():
            l_ref[...] = part_l
            dg_ref[...] = part_g

        @pl.when(i > 0)
        def _():
            l_ref[...] += part_l
            dg_ref[...] += part_g

    spec = pl.BlockSpec((tm, D), lambda i: (i, 0))
    return pl.pallas_call(
        body, grid=(T // tm,), in_specs=[spec, pl.BlockSpec((1, D), lambda i: (0, 0)), spec],
        out_specs=[pl.BlockSpec((1, 128), lambda i: (0, 0)), spec, pl.BlockSpec((1, D), lambda i: (0, 0))],
        out_shape=[jax.ShapeDtypeStruct((1, 128), F32), jax.ShapeDtypeStruct((T, D), F32),
                   jax.ShapeDtypeStruct((1, D), F32)],
        compiler_params=_cp("arbitrary"), name=name)(x, g, target)


S5_LS = 128
S5_SEG = 8
S5_TB = S5_LS * S5_SEG
S5_CH = 8
S5_CW = 128
S5_NST = 512
S5_UNROLL = 4


def _cmul(ar, ai, br, bi):
    return ar * br - ai * bi, ar * bi + ai * br


def s5_params_fwd(a_re, a_im, log_dt, b_re, b_im, expand, *, name):
    G, P = a_re.shape

    def body(ar_ref, ai_ref, ld_ref, br_ref, bi_ref, e_ref, abr_ref, abi_ref, apr_ref, api_ref, bbr_ref, bbi_ref):
        ar, ai = ar_ref[...], ai_ref[...]
        dt = jnp.exp(ld_ref[...])
        mag = jnp.exp(dt * ar)
        abr, abi = mag * jnp.cos(dt * ai), mag * jnp.sin(dt * ai)
        den = ar * ar + ai * ai
        zr, zi = abr - 1.0, abi
        fr = (zr * ar + zi * ai) / den
        fi = (zi * ar - zr * ai) / den
        frx, fix = _dot_hi(fr, e_ref[...]), _dot_hi(fi, e_ref[...])
        br, bi = br_ref[...], bi_ref[...]
        bbr_ref[...] = frx * br - fix * bi
        bbi_ref[...] = frx * bi + fix * br
        abr_ref[...] = abr
        abi_ref[...] = abi
        pr, pi = abr, abi
        for _ in range(int(math.log2(S5_LS))):
            pr, pi = _cmul(pr, pi, pr, pi)
        apr_ref[...] = pr
        api_ref[...] = pi

    small = jax.ShapeDtypeStruct((G, P), F32)
    big = jax.ShapeDtypeStruct(b_re.shape, F32)
    return pl.pallas_call(body, out_shape=[small, small, small, small, big, big], name=name)(
        a_re, a_im, log_dt, b_re, b_im, expand)


def s5_params_bwd(a_re, a_im, log_dt, b_re, b_im, g_abr, g_abi, g_bbr, g_bbi, expand, expand_t, *, name):
    G, P = a_re.shape

    def body(ar_ref, ai_ref, ld_ref, br_ref, bi_ref, gar_ref, gai_ref, gbr_ref, gbi_ref, e_ref, et_ref,
             dar_ref, dai_ref, dld_ref, dbr_ref, dbi_ref):
        ar, ai = ar_ref[...], ai_ref[...]
        dt = jnp.exp(ld_ref[...])
        mag = jnp.exp(dt * ar)
        cs, sn = jnp.cos(dt * ai), jnp.sin(dt * ai)
        abr, abi = mag * cs, mag * sn
        den = ar * ar + ai * ai
        zr, zi = abr - 1.0, abi
        fr = (zr * ar + zi * ai) / den
        fi = (zi * ar - zr * ai) / den
        frx, fix = _dot_hi(fr, e_ref[...]), _dot_hi(fi, e_ref[...])
        br, bi = br_ref[...], bi_ref[...]
        gbr, gbi = gbr_ref[...], gbi_ref[...]
        dbr_ref[...] = frx * gbr + fix * gbi
        dbi_ref[...] = -fix * gbr + frx * gbi
        gfr = _dot_hi(br * gbr + bi * gbi, et_ref[...])
        gfi = _dot_hi(-bi * gbr + br * gbi, et_ref[...])
        g_zr = (gfr * ar - gfi * ai) / den
        g_zi = (gfr * ai + gfi * ar) / den
        g_ar = gfr * (zr - fr * 2.0 * ar) / den + gfi * (zi - fi * 2.0 * ar) / den
        g_ai = gfr * (zi - fr * 2.0 * ai) / den + gfi * (-zr - fi * 2.0 * ai) / den
        t_abr = gar_ref[...] + g_zr
        t_abi = gai_ref[...] + g_zi
        g_mag = t_abr * cs + t_abi * sn
        g_th = mag * (-t_abr * sn + t_abi * cs)
        dar_ref[...] = g_ar + g_mag * mag * dt
        dai_ref[...] = g_ai + g_th * dt
        g_dt = jnp.sum(g_mag * mag * ar + g_th * ai, axis=-1, keepdims=True)
        dld_ref[...] = g_dt * dt

    small = jax.ShapeDtypeStruct((G, P), F32)
    big = jax.ShapeDtypeStruct(b_re.shape, F32)
    return pl.pallas_call(body, out_shape=[small, small, jax.ShapeDtypeStruct((G, 1), F32), big, big], name=name)(
        a_re, a_im, log_dt, b_re, b_im, g_abr, g_abi, g_bbr, g_bbi, expand, expand_t)


def _s5_permute_in(src_ref, dst_ref):
    for i in range(S5_LS):
        dst_ref[pl.ds(8 * i, 8), :] = src_ref[pl.ds(i, 8, stride=S5_LS), :]


def _s5_permute_out(src_ref, dst_ref):
    for r in range(S5_SEG):
        for k in range(S5_LS // 8):
            dst_ref[pl.ds(r * S5_LS + 8 * k, 8), :] = src_ref[pl.ds(64 * k + r, 8, stride=8), :]


def _s5_scan(a_r, a_i, dr_ref, di_ref, init_r, init_i, store=None, reverse=False, conj=False):
    sgn = -1.0 if conj else 1.0

    def steps(n, c):
        sr, si = c
        for u in range(S5_UNROLL):
            k = n * S5_UNROLL + u
            i = S5_LS - 1 - k if reverse else k
            nr = a_r * sr - sgn * a_i * si + dr_ref[i]
            ni = a_r * si + sgn * a_i * sr + di_ref[i]
            if store is not None:
                store(i, nr, ni, sr, si)
            sr, si = nr, ni
        return sr, si

    return lax.fori_loop(0, S5_LS // S5_UNROLL, steps, (init_r, init_i))


def _s5_stitch(apr, api, fin_r, fin_i, car_r, car_i, reverse=False, conj=False):
    sgn = -1.0 if conj else 1.0
    rows_r, rows_i = [None] * S5_SEG, [None] * S5_SEG
    order = range(S5_SEG - 1, -1, -1) if reverse else range(S5_SEG)
    for r in order:
        rows_r[r], rows_i[r] = car_r, car_i
        fr, fi = fin_r[r:r + 1], fin_i[r:r + 1]
        car_r, car_i = (apr * car_r - sgn * api * car_i + fr, apr * car_i + sgn * api * car_r + fi)
    return jnp.concatenate(rows_r, 0), jnp.concatenate(rows_i, 0), car_r, car_i


def _s5_specs(nb, rev):
    blk = (lambda c, b: (nb - 1 - b, c)) if rev else (lambda c, b: (b, c))
    tok = pl.BlockSpec((S5_TB, S5_CW), blk)
    par_b = pl.BlockSpec((1, S5_CW, S5_NST), lambda c, b: (c, 0, 0))
    par_c = pl.BlockSpec((1, S5_NST, S5_CW), lambda c, b: (c, 0, 0))
    vec_s = pl.BlockSpec((1, 1, S5_NST), lambda c, b: (c, 0, 0))
    vec_c = pl.BlockSpec((1, 1, S5_CW), lambda c, b: (c, 0, 0))
    return tok, par_b, par_c, vec_s, vec_c


def s5_fwd(proj, bbr, bbi, ccr, cci, abr, abi, apr, api, dskip, *, name, comm=None):
    T = proj.shape[0]
    nb = T // S5_TB
    zeros8 = functools.partial(jnp.zeros, (S5_SEG, S5_NST), F32)

    def body(u_ref, bbr_ref, bbi_ref, ccr_ref, cci_ref, ar_ref, ai_ref, apr_ref, api_ref, d_ref,
             y_ref, csr_ref, csi_ref, uf_ref, up_ref, dr_ref, di_ref, sr_ref, si_ref, yp_ref, car_ref, cai_ref):
        b = pl.program_id(1)

        @pl.when(b == 0)
        def _():
            car_ref[...] = jnp.zeros_like(car_ref)
            cai_ref[...] = jnp.zeros_like(cai_ref)

        csr_ref[0, 0] = car_ref[...]
        csi_ref[0, 0] = cai_ref[...]
        uf_ref[...] = u_ref[...].astype(F32)
        _s5_permute_in(uf_ref, up_ref)
        upb = up_ref[...].astype(BF16)
        dr_ref[...] = _dot(upb, bbr_ref[0]).reshape(S5_LS, S5_SEG, S5_NST)
        di_ref[...] = _dot(upb, bbi_ref[0]).reshape(S5_LS, S5_SEG, S5_NST)
        a_r = jnp.broadcast_to(ar_ref[0], (S5_SEG, S5_NST))
        a_i = jnp.broadcast_to(ai_ref[0], (S5_SEG, S5_NST))
        fin_r, fin_i = _s5_scan(a_r, a_i, dr_ref, di_ref, zeros8(), zeros8())
        cin_r, cin_i, ncr, nci = _s5_stitch(apr_ref[0], api_ref[0], fin_r, fin_i, car_ref[...], cai_ref[...])
        car_ref[...] = ncr
        cai_ref[...] = nci

        def store(i, nr, ni, sr, si):
            sr_ref[i] = nr
            si_ref[i] = ni

        _s5_scan(a_r, a_i, dr_ref, di_ref, cin_r, cin_i, store=store)
        s_r = sr_ref[...].reshape(S5_TB, S5_NST).astype(BF16)
        s_i = si_ref[...].reshape(S5_TB, S5_NST).astype(BF16)
        yp_ref[...] = _dot(s_r, ccr_ref[0]) - _dot(s_i, cci_ref[0]) + d_ref[0] * up_ref[...]
        _s5_permute_out(yp_ref, uf_ref)
        y_ref[...] = uf_ref[...].astype(BF16)

    tok, par_b, par_c, vec_s, vec_c = _s5_specs(nb, False)
    cs_spec = pl.BlockSpec((1, 1, 1, S5_NST), lambda c, b: (b, c, 0, 0))
    cs_shape = jax.ShapeDtypeStruct((nb, S5_CH, 1, S5_NST), F32)
    tokbuf = pltpu.VMEM((S5_TB, S5_CW), F32)
    stbuf = pltpu.VMEM((S5_LS, S5_SEG, S5_NST), F32)
    return _call(
        body, grid=(S5_CH, nb),
        in_specs=[tok, par_b, par_b, par_c, par_c, vec_s, vec_s, vec_s, vec_s, vec_c],
        out_specs=[tok, cs_spec, cs_spec],
        out_shape=[jax.ShapeDtypeStruct((T, D_S5), BF16), cs_shape, cs_shape],
        scratch_shapes=[tokbuf, tokbuf, stbuf, stbuf, stbuf, stbuf, tokbuf,
                        pltpu.VMEM((1, S5_NST), F32), pltpu.VMEM((1, S5_NST), F32)],
        sem=("parallel", "arbitrary"), name=name, comm=comm,
        args=(proj, bbr, bbi, ccr, cci, abr, abi, apr, api, dskip))


def s5_bwd(proj, dys, csr, csi, bbr, bbi, ccr, cci, abr, abi, apr, api, dskip, dproj, *, name, comm=None):
    T = proj.shape[0]
    nb = T // S5_TB
    zeros8 = functools.partial(jnp.zeros, (S5_SEG, S5_NST), F32)

    def body(u_ref, gy_ref, csr_ref, csi_ref, bbr_ref, bbi_ref, ccr_ref, cci_ref, ar_ref, ai_ref, apr_ref, api_ref,
             d_ref, dproj_ref, du_ref, dbr_ref, dbi_ref, dcr_ref, dci_ref, dd_ref, dar_ref, dai_ref,
             tmp_ref, up_ref, gyp_ref, dr_ref, di_ref, sr_ref, si_ref, gr_ref, gi_ref, car_ref, cai_ref):
        b = pl.program_id(1)

        @pl.when(b == 0)
        def _():
            car_ref[...] = jnp.zeros_like(car_ref)
            cai_ref[...] = jnp.zeros_like(cai_ref)
            for ref in (dbr_ref, dbi_ref, dcr_ref, dci_ref, dd_ref, dar_ref, dai_ref):
                ref[...] = jnp.zeros_like(ref)

        tmp_ref[...] = u_ref[...].astype(F32)
        _s5_permute_in(tmp_ref, up_ref)
        tmp_ref[...] = gy_ref[...].astype(F32)
        _s5_permute_in(tmp_ref, gyp_ref)
        upb = up_ref[...].astype(BF16)
        gyp = gyp_ref[...]
        gypb = gyp.astype(BF16)
        dr_ref[...] = _dot(upb, bbr_ref[0]).reshape(S5_LS, S5_SEG, S5_NST)
        di_ref[...] = _dot(upb, bbi_ref[0]).reshape(S5_LS, S5_SEG, S5_NST)
        a_r = jnp.broadcast_to(ar_ref[0], (S5_SEG, S5_NST))
        a_i = jnp.broadcast_to(ai_ref[0], (S5_SEG, S5_NST))
        fin_r, fin_i = _s5_scan(a_r, a_i, dr_ref, di_ref, zeros8(), zeros8())
        cin_r, cin_i, _, _ = _s5_stitch(apr_ref[0], api_ref[0], fin_r, fin_i, csr_ref[0, 0], csi_ref[0, 0])
        sr_ref[0] = cin_r
        si_ref[0] = cin_i

        def store_s(i, nr, ni, sr, si):
            sr_ref[i + 1] = nr
            si_ref[i + 1] = ni

        _s5_scan(a_r, a_i, dr_ref, di_ref, cin_r, cin_i, store=store_s)
        s_r = sr_ref[pl.ds(1, S5_LS)].reshape(S5_TB, S5_NST).astype(BF16)
        s_i = si_ref[pl.ds(1, S5_LS)].reshape(S5_TB, S5_NST).astype(BF16)
        dcr_ref[0] += _dot_tn(s_r, gypb)
        dci_ref[0] -= _dot_tn(s_i, gypb)
        dd_ref[0] += jnp.sum(gyp * up_ref[...], axis=0, keepdims=True)
        dr_ref[...] = _dot_nt(gypb, ccr_ref[0]).reshape(S5_LS, S5_SEG, S5_NST)
        di_ref[...] = (-_dot_nt(gypb, cci_ref[0])).reshape(S5_LS, S5_SEG, S5_NST)
        fin_r, fin_i = _s5_scan(a_r, a_i, dr_ref, di_ref, zeros8(), zeros8(), reverse=True, conj=True)
        gin_r, gin_i, ncr, nci = _s5_stitch(apr_ref[0], api_ref[0], fin_r, fin_i, car_ref[...], cai_ref[...],
                                            reverse=True, conj=True)
        car_ref[...] = ncr
        cai_ref[...] = nci
        def steps_g(n, carry):
            gr, gi, acc_r, acc_i = carry
            for u in range(S5_UNROLL):
                i = S5_LS - 1 - (n * S5_UNROLL + u)
                nr = a_r * gr + a_i * gi + dr_ref[i]
                ni = a_r * gi - a_i * gr + di_ref[i]
                gr_ref[i] = nr
                gi_ref[i] = ni
                pr, pi = sr_ref[i], si_ref[i]
                acc_r, acc_i = acc_r + (nr * pr + ni * pi), acc_i + (ni * pr - nr * pi)
                gr, gi = nr, ni
            return gr, gi, acc_r, acc_i

        _, _, acc_r, acc_i = lax.fori_loop(0, S5_LS // S5_UNROLL, steps_g, (gin_r, gin_i, zeros8(), zeros8()))
        dar_ref[0] += jnp.sum(acc_r, axis=0, keepdims=True)
        dai_ref[0] += jnp.sum(acc_i, axis=0, keepdims=True)
        g_r = gr_ref[...].reshape(S5_TB, S5_NST).astype(BF16)
        g_i = gi_ref[...].reshape(S5_TB, S5_NST).astype(BF16)
        dbr_ref[0] += _dot_tn(upb, g_r)
        dbi_ref[0] += _dot_tn(upb, g_i)
        gyp_ref[...] = _dot_nt(g_r, bbr_ref[0]) + _dot_nt(g_i, bbi_ref[0]) + d_ref[0] * gyp
        _s5_permute_out(gyp_ref, tmp_ref)
        du_ref[...] = tmp_ref[...].astype(BF16)

    tok, par_b, par_c, vec_s, vec_c = _s5_specs(nb, True)
    cs_spec = pl.BlockSpec((1, 1, 1, S5_NST), lambda c, b: (nb - 1 - b, c, 0, 0))
    tokbuf = pltpu.VMEM((S5_TB, S5_CW), F32)
    stbuf = pltpu.VMEM((S5_LS, S5_SEG, S5_NST), F32)
    stbuf1 = pltpu.VMEM((S5_LS + 1, S5_SEG, S5_NST), F32)
    return _call(
        body, grid=(S5_CH, nb), comm=comm,
        in_specs=[tok, tok, cs_spec, cs_spec, par_b, par_b, par_c, par_c, vec_s, vec_s, vec_s, vec_s, vec_c, _ANY],
        out_specs=[tok, par_b, par_b, par_c, par_c, vec_c, vec_s, vec_s], aliases={13: 0},
        out_shape=[jax.ShapeDtypeStruct(dproj.shape, BF16),
                   jax.ShapeDtypeStruct((S5_CH, S5_CW, S5_NST), F32), jax.ShapeDtypeStruct((S5_CH, S5_CW, S5_NST), F32),
                   jax.ShapeDtypeStruct((S5_CH, S5_NST, S5_CW), F32), jax.ShapeDtypeStruct((S5_CH, S5_NST, S5_CW), F32),
                   jax.ShapeDtypeStruct((S5_CH, 1, S5_CW), F32),
                   jax.ShapeDtypeStruct((S5_CH, 1, S5_NST), F32), jax.ShapeDtypeStruct((S5_CH, 1, S5_NST), F32)],
        scratch_shapes=[tokbuf, tokbuf, tokbuf, stbuf, stbuf, stbuf1, stbuf1, stbuf, stbuf,
                        pltpu.VMEM((1, S5_NST), F32), pltpu.VMEM((1, S5_NST), F32)],
        sem=("parallel", "arbitrary"), name=name,
        args=(proj, dys, csr, csi, bbr, bbi, ccr, cci, abr, abi, apr, api, dskip, dproj))


SSD_L = SSD_CHUNK
SSD_GW = 256
NEG = -1e30


def _expand16(v):
    lane = lax.broadcasted_iota(jnp.int32, (v.shape[0], 128), 1)
    parts = [jnp.where(lane < SSD_HEADDIM, v[:, 2 * j:2 * j + 1], v[:, 2 * j + 1:2 * j + 2]) for j in range(8)]
    return jnp.concatenate(parts, axis=1)


def _headsum(v, hsum):
    hi = v.astype(BF16)
    lo = (v - hi.astype(F32)).astype(BF16)
    return _dot(hi, hsum) + _dot(lo, hsum)


def _softplus(x):
    return jnp.maximum(x, 0.0) + jnp.log(1.0 + jnp.exp(-jnp.abs(x)))


def _ssd_chunk_fwd(z, xbc, tail, dtraw, hprev, cw, cb, dtb, alog, dsk, nw, tril):
    L = SSD_L
    f = {}
    xe = jnp.concatenate([tail, xbc], axis=0)
    sh = [xbc] + [pltpu.roll(xe, s, 0)[8:] for s in (1, 2, 3)]
    conv = cb + cw[3:4] * sh[0] + cw[2:3] * sh[1] + cw[1:2] * sh[2] + cw[0:1] * sh[3]
    sig = _sigmoid(conv)
    xa = conv * sig
    xs, bm, cm = xa[:, :D_SSD], xa[:, D_SSD:D_SSD + 512], xa[:, D_SSD + 512:]
    pre = dtraw + dtb
    dt = _softplus(pre)
    a_h = -jnp.exp(alog)
    acum = _dot_hi(tril, dt * a_h)
    acum_t = acum.T
    alast = acum[L - 1:L]
    exp_a = jnp.exp(acum)
    dec = jnp.exp(alast - acum)
    exp_al = jnp.exp(alast)
    dt_x, dec_x, exp_a_x, exp_al_x = _expand16(dt), _expand16(dec), _expand16(exp_a), _expand16(exp_al)
    d_x = _expand16(dsk)
    xh = xs * dt_x
    xhb = xh.astype(BF16)
    xd = (xh * dec_x).astype(BF16)
    row = lax.broadcasted_iota(jnp.int32, (L, L), 0)
    col = lax.broadcasted_iota(jnp.int32, (L, L), 1)
    causal = row >= col
    lane = lax.broadcasted_iota(jnp.int32, (L, 128), 1)
    low = lane < SSD_HEADDIM
    hb = hprev.astype(BF16)
    y_pairs, yoff_parts, st_parts, cbs, lms = [], [], [], [], []
    for g in range(SSD_NGROUPS):
        bg = bm[:, g * 128:(g + 1) * 128].astype(BF16)
        cg = cm[:, g * 128:(g + 1) * 128].astype(BF16)
        cbg = _dot_nt(cg, bg)
        cbs.append(cbg)
        for j in (2 * g, 2 * g + 1):
            xp = xhb[:, j * 128:(j + 1) * 128]
            ys = []
            for h in (2 * j, 2 * j + 1):
                lm = jnp.exp(jnp.where(causal, acum[:, h:h + 1] - acum_t[h:h + 1, :], NEG))
                lms.append(lm)
                ys.append(_dot((cbg * lm).astype(BF16), xp))
            y_pairs.append(jnp.where(low, ys[0], ys[1]))
        gs = slice(g * SSD_GW, (g + 1) * SSD_GW)
        yoff_parts.append(_dot(cg, hb[:, gs]) * exp_a_x[:, gs])
        st_parts.append(_dot_tn(bg, xd[:, gs]))
    yoff = jnp.concatenate(yoff_parts, axis=1)
    y = jnp.concatenate(y_pairs, axis=1) + yoff + d_x * xs
    hnew = exp_al_x * hprev + jnp.concatenate(st_parts, axis=1)
    sz = _sigmoid(z)
    gz = y * (z * sz)
    r = lax.rsqrt(jnp.mean(gz * gz, axis=-1, keepdims=True) + EPS)
    out = gz * r * nw
    f.update(sh=sh, conv=conv, sig=sig, xs=xs, bm=bm, cm=cm, pre=pre, dt=dt, a_h=a_h, exp_a_x=exp_a_x, dec_x=dec_x,
             exp_al=exp_al, exp_al_x=exp_al_x, dt_x=dt_x, d_x=d_x, xh=xh, xhb=xhb, xd=xd, causal=causal, low=low, hb=hb,
             cbs=cbs, lms=lms, yoff=yoff, y=y, sz=sz, gz=gz, r=r)
    return out, hnew, f


def _ssd_params(conv_w, conv_b, dt_bias, a_log, d_skip, norm_w):
    pad16 = lambda v: jnp.pad(v.reshape(1, SSD_HEADS), ((0, 0), (0, 128 - SSD_HEADS)))
    return (jnp.pad(conv_w, ((0, 8 - SSD_CONV), (0, 0))), conv_b.reshape(1, D_XBC), pad16(dt_bias), pad16(a_log),
            pad16(d_skip), norm_w.reshape(1, D_SSD))


def _ssd_param_specs():
    full = lambda shape: pl.BlockSpec(shape, lambda i: (0, 0))
    return [full((8, D_XBC)), full((1, D_XBC)), full((1, 128)), full((1, 128)), full((1, 128)), full((1, D_SSD))]


def ssd_fwd(proj, dtraw, params, tril, ymix, *, name, comm=None):
    T = proj.shape[0]
    nc = T // SSD_L

    def body(z_ref, x_ref, dt_ref, cw_ref, cb_ref, dtb_ref, al_ref, dsk_ref, nw_ref, tril_ref, ymix_ref,
             o_ref, hs_ref, h_ref, tail_ref):
        i = pl.program_id(0)

        @pl.when(i == 0)
        def _():
            h_ref[...] = jnp.zeros_like(h_ref)
            tail_ref[...] = jnp.zeros_like(tail_ref)

        xbc = x_ref[...].astype(F32)
        hprev = h_ref[...]
        hs_ref[0] = hprev
        out, hnew, _ = _ssd_chunk_fwd(z_ref[...].astype(F32), xbc, tail_ref[...], dt_ref[...], hprev, cw_ref[...],
                                      cb_ref[...], dtb_ref[...], al_ref[...], dsk_ref[...], nw_ref[...], tril_ref[...])
        o_ref[...] = out.astype(BF16)
        h_ref[...] = hnew
        tail_ref[...] = xbc[SSD_L - 8:]

    return _call(
        body, grid=(nc,),
        in_specs=[pl.BlockSpec((SSD_L, D_SSD), lambda i: (i, 1)), pl.BlockSpec((SSD_L, D_XBC), lambda i: (i, 1)),
                  pl.BlockSpec((SSD_L, 128), lambda i: (i, 0))] + _ssd_param_specs()
                 + [pl.BlockSpec((SSD_L, SSD_L), lambda i: (0, 0)), _ANY],
        out_specs=[pl.BlockSpec((SSD_L, D_SSD), lambda i: (i, 1)),
                   pl.BlockSpec((1, SSD_STATE, D_SSD), lambda i: (i, 0, 0))],
        out_shape=[jax.ShapeDtypeStruct(ymix.shape, BF16), jax.ShapeDtypeStruct((nc, SSD_STATE, D_SSD), F32)],
        scratch_shapes=[pltpu.VMEM((SSD_STATE, D_SSD), F32), pltpu.VMEM((8, D_XBC), F32)],
        sem=("arbitrary",), name=name, args=(proj, proj, dtraw, *params, tril, ymix), aliases={10: 0}, comm=comm)


def ssd_bwd(proj, dtraw, hs, dymix, params, tril, triu, trils, headsum, *, name, comm=None):
    T = proj.shape[0]
    nc = T // SSD_L
    L = SSD_L

    def body(z_ref, x_ref, xprev_ref, dt_ref, hs_ref, do_ref, cw_ref, cb_ref, dtb_ref, al_ref, dsk_ref, nw_ref,
             tril_ref, triu_ref, trils_ref, hsum_ref,
             dp_ref, ddt_ref, dcw_ref, dcb_ref, ddtb_ref, dal_ref, ddsk_ref, dnw_ref, dh_ref, dnext_ref):
        i = pl.program_id(0)

        @pl.when(i == 0)
        def _():
            dh_ref[...] = jnp.zeros_like(dh_ref)
            dnext_ref[...] = jnp.zeros_like(dnext_ref)
            for ref in (dcw_ref, dcb_ref, ddtb_ref, dal_ref, ddsk_ref, dnw_ref):
                ref[...] = jnp.zeros_like(ref)

        z = z_ref[...].astype(F32)
        xbc = x_ref[...].astype(F32)
        tail = jnp.where(i == nc - 1, 0.0, xprev_ref[...].astype(F32))
        hprev = hs_ref[0]
        cw, nw = cw_ref[...], nw_ref[...]
        hsum = hsum_ref[...]
        _, _, f = _ssd_chunk_fwd(z, xbc, tail, dt_ref[...], hprev, cw, cb_ref[...], dtb_ref[...], al_ref[...],
                                 dsk_ref[...], nw, tril_ref[...])
        dout = do_ref[...].astype(F32)
        dh = dh_ref[...]
        ghat = f["gz"] * f["r"]
        dn = dout * nw
        dgz = f["r"] * (dn - ghat * jnp.mean(dn * ghat, axis=-1, keepdims=True))
        dnw_ref[...] += jnp.sum(dout * ghat, axis=0, keepdims=True)
        sz = f["sz"]
        dy = dgz * (z * sz)
        dp_ref[:, :D_S5] = jnp.zeros((L, D_S5), BF16)
        dp_ref[:, D_S5:D_S5 + D_SSD] = (dgz * f["y"] * sz * (1.0 + z * (1.0 - sz))).astype(BF16)
        xs = f["xs"]
        ddsk_ref[...] += jnp.sum(_headsum(dy * xs, hsum), axis=0, keepdims=True)
        dyb = dy.astype(BF16)
        dye = (dy * f["exp_a_x"]).astype(BF16)
        dhb = dh.astype(BF16)
        lane = lax.broadcasted_iota(jnp.int32, (L, 128), 1)
        sub = lax.broadcasted_iota(jnp.int32, (128, L), 0)
        zero_b = jnp.zeros((L, 128), BF16)
        rsum = jnp.zeros((L, 128), F32)
        csum_t = jnp.zeros((128, L), F32)
        dx_pairs, dxst_parts, db_parts, dc_parts, dhp_parts = [], [], [], [], []
        for g in range(SSD_NGROUPS):
            gs = slice(g * SSD_GW, (g + 1) * SSD_GW)
            bg = f["bm"][:, g * 128:(g + 1) * 128].astype(BF16)
            cg = f["cm"][:, g * 128:(g + 1) * 128].astype(BF16)
            cbg = f["cbs"][g]
            dcb_g = jnp.zeros((L, L), F32)
            for j in (2 * g, 2 * g + 1):
                xp = f["xhb"][:, j * 128:(j + 1) * 128]
                dyp = dyb[:, j * 128:(j + 1) * 128]
                dxs = []
                for half, h in enumerate((2 * j, 2 * j + 1)):
                    lm = f["lms"][h]
                    dyh = jnp.where(f["low"], dyp, zero_b) if half == 0 else jnp.where(f["low"], zero_b, dyp)
                    dw = jnp.where(f["causal"], _dot_nt(dyh, xp), 0.0)
                    w = cbg * lm
                    e = dw * w
                    dcb_g = dcb_g + dw * lm
                    rsum = jnp.where(lane == h, jnp.sum(e, axis=1, keepdims=True), rsum)
                    csum_t = jnp.where(sub == h, jnp.sum(e, axis=0, keepdims=True), csum_t)
                    dxs.append(_dot_tn(w.astype(BF16), dyp))
                dx_pairs.append(jnp.where(f["low"], dxs[0], dxs[1]))
            dcbb = dcb_g.astype(BF16)
            dxst_parts.append(f["dec_x"][:, gs] * _dot(bg, dhb[:, gs]))
            dc_parts.append(_dot(dcbb, bg) + _dot_nt(dye[:, gs], f["hb"][:, gs]))
            db_parts.append(_dot_tn(dcbb, cg) + _dot_nt(f["xd"][:, gs], dhb[:, gs]))
            dhp_parts.append(f["exp_al_x"][:, gs] * dh[:, gs] + _dot_tn(cg, dye[:, gs]))
        dxst = jnp.concatenate(dxst_parts, axis=1)
        dxh = jnp.concatenate(dx_pairs, axis=1) + dxst
        q = _headsum(f["yoff"] * dy, hsum)
        dstate = _headsum(f["xh"] * dxst, hsum)
        h0t = jnp.sum(_headsum(dh * hprev, hsum), axis=0, keepdims=True) * f["exp_al"]
        da = _dot_hi(triu_ref[...], rsum - csum_t.T + q) + _dot_hi(trils_ref[...], dstate) + h0t
        dt, a_h = f["dt"], f["a_h"]
        ddt = _headsum(dxh * xs, hsum) + da * a_h
        dal_ref[...] += jnp.sum(da * dt, axis=0, keepdims=True) * a_h
        ddtraw = ddt * _sigmoid(f["pre"])
        first16 = lane < SSD_HEADS
        ddtraw = jnp.where(first16, ddtraw, 0.0)
        ddt_ref[...] = ddtraw
        ddtb_ref[...] += jnp.sum(ddtraw, axis=0, keepdims=True)
        dh_ref[...] = jnp.concatenate(dhp_parts, axis=1)
        dxa = jnp.concatenate([dxh * f["dt_x"] + f["d_x"] * dy] + db_parts + dc_parts, axis=1)
        sig, conv = f["sig"], f["conv"]
        dconv = dxa * sig * (1.0 + conv * (1.0 - sig))
        dcb_ref[...] += jnp.sum(dconv, axis=0, keepdims=True)
        rows = [jnp.sum(dconv * f["sh"][3 - k], axis=0, keepdims=True) for k in range(SSD_CONV)]
        dcw_ref[...] += jnp.concatenate(rows + [jnp.zeros((8 - SSD_CONV, D_XBC), F32)], axis=0)
        de = jnp.concatenate([dconv, dnext_ref[...]], axis=0)
        dxbc = cw[3:4] * dconv
        for s in (1, 2, 3):
            dxbc = dxbc + cw[3 - s:4 - s] * pltpu.roll(de, L + 8 - s, 0)[:L]
        dp_ref[:, D_S5 + D_SSD:] = dxbc.astype(BF16)
        dnext_ref[...] = dconv[:8]

    rev = lambda i: nc - 1 - i
    acc = lambda shape: pl.BlockSpec(shape, lambda i: (0, 0))
    tri = pl.BlockSpec((L, L), lambda i: (0, 0))
    return _call(
        body, grid=(nc,),
        in_specs=[pl.BlockSpec((L, D_SSD), lambda i: (rev(i), 1)), pl.BlockSpec((L, D_XBC), lambda i: (rev(i), 1)),
                  pl.BlockSpec((8, D_XBC), lambda i: (jnp.maximum(rev(i) * (L // 8) - 1, 0), 1)),
                  pl.BlockSpec((L, 128), lambda i: (rev(i), 0)),
                  pl.BlockSpec((1, SSD_STATE, D_SSD), lambda i: (rev(i), 0, 0)),
                  pl.BlockSpec((L, D_SSD), lambda i: (rev(i), 1))] + _ssd_param_specs()
                 + [tri, tri, tri, pl.BlockSpec((D_SSD, 128), lambda i: (0, 0))],
        out_specs=[pl.BlockSpec((L, D_MAIN), lambda i: (rev(i), 0)), pl.BlockSpec((L, 128), lambda i: (rev(i), 0)),
                   acc((8, D_XBC)), acc((1, D_XBC)), acc((1, 128)), acc((1, 128)), acc((1, 128)), acc((1, D_SSD))],
        out_shape=[jax.ShapeDtypeStruct((T, D_MAIN), BF16),
                   jax.ShapeDtypeStruct((T, 128), F32), jax.ShapeDtypeStruct((8, D_XBC), F32),
                   jax.ShapeDtypeStruct((1, D_XBC), F32), jax.ShapeDtypeStruct((1, 128), F32),
                   jax.ShapeDtypeStruct((1, 128), F32), jax.ShapeDtypeStruct((1, 128), F32),
                   jax.ShapeDtypeStruct((1, D_SSD), F32)],
        scratch_shapes=[pltpu.VMEM((SSD_STATE, D_SSD), F32), pltpu.VMEM((8, D_XBC), F32)],
        sem=("arbitrary",), name=name, comm=comm,
        args=(proj, proj, proj, dtraw, hs, dymix, *params, tril, triu, trils, headsum))


def _s5_blockdiag(v, rows_per_group, cols_per_group):
    eye = jnp.eye(S5_SEG, dtype=v.dtype)
    w = v[:, :, :, None, :] * eye[None, :, None, :, None]
    return w.reshape(S5_CH, 8 * rows_per_group, 8 * cols_per_group)


def _s5_blockdiag_extract(w, rows_per_group, cols_per_group):
    eye = jnp.eye(S5_SEG, dtype=w.dtype)
    w5 = w.reshape(S5_CH, 8, rows_per_group, 8, cols_per_group)
    return jnp.sum(w5 * eye[None, :, None, :, None], axis=3)


TM = 512

OFF = dict(mlp_w2=0, mlp_w1=1024, w_out=2048, s5_w_glu=2560, xa_wq=2816, xa_wk=3072, xa_wv=3328, xa_wo=3584, w_in=3840)
ROWS = dict(mlp_w2=1024, mlp_w1=1024, w_out=512, s5_w_glu=256, xa_wq=256, xa_wk=256, xa_wv=256, xa_wo=256, w_in=1028)
TAIL_OFF = 4872
TAIL_ROWS = 120
PACK_ROWS = TAIL_OFF + TAIL_ROWS
ROWS_ALL = (0, PACK_ROWS, 832, True)
ROWS_EARLY = (0, OFF["w_in"], 640, False)
ROWS_LATE = (OFF["w_in"], PACK_ROWS - OFF["w_in"], 192, True)
N_SHARD = 4
SMALL_L = ("norm_mix", "s5_a_re", "s5_a_im", "s5_log_dt", "s5_b_re", "s5_b_im", "s5_c_re", "s5_c_im", "s5_d",
           "ssd_conv_w", "ssd_conv_b", "ssd_dt_bias", "ssd_a_log", "ssd_d", "ssd_norm", "norm_xattn", "norm_mem",
           "norm_mlp")
SMALL_Q = 72
CONV_ROWS = 8


def _place():
    x, y, c = lax.axis_index("x"), lax.axis_index("y"), lax.axis_index("c")
    chips = [(1 - x, y), (x, 1 - y), (1 - x, 1 - y)]
    return x, y, c, 2 * x + y, chips, (x, y, 1 - c)


def _remote(src, dst, send_sem, recv_sem, to):
    return pltpu.make_async_remote_copy(src_ref=src, dst_ref=dst, send_sem=send_sem, recv_sem=recv_sem,
                                        device_id=to, device_id_type=MESH_T)


def _dma_sems(*counts):
    return [pltpu.SemaphoreType.DMA((n,)) for n in counts]


def place_own(wpack, shard_idx):
    tile = PACK_ROWS // 4

    def body(s_ref, w_ref, o_ref):
        o_ref[0] = w_ref[...]

    return pl.pallas_call(
        body,
        grid_spec=pltpu.PrefetchScalarGridSpec(
            num_scalar_prefetch=1, grid=(4,),
            in_specs=[pl.BlockSpec((tile, D_MODEL), lambda i, s_ref: (i, 0))],
            out_specs=pl.BlockSpec((1, tile, D_MODEL), lambda i, s_ref: (s_ref[0], i, 0))),
        out_shape=jax.ShapeDtypeStruct((N_SHARD,) + wpack.shape, wpack.dtype),
        compiler_params=_cp("parallel"), name="place_own")(shard_idx, wpack)


def _range_half(ref, rows, c):
    half = rows[1] // 2
    return pl.ds(pl.multiple_of(rows[0] + c * half, 16), half)


def gather_over_ici(wpack, placed, rows=ROWS_ALL):
    def copies(ci, co, cs):
        w_ref, (out_ref,), (send, recv) = ci[0], co, cs
        x, y, c, s, chips, sibling = _place()
        mine = _range_half(w_ref, rows, c)
        sends = [_remote(w_ref.at[mine, :], out_ref.at[s, mine, :], send.at[j], recv.at[j], (*chip, c))
                 for j, chip in enumerate(chips)]
        lands = [out_ref.at[2 * chip[0] + chip[1], mine, :] for chip in chips]
        arrivals = [_remote(land, land, send.at[j], recv.at[j], sibling) for j, land in enumerate(lands)]
        return sends, arrivals

    def start(ci, co, cs):
        for cp in copies(ci, co, cs)[0]:
            cp.start()

    def wait(ci, co, cs):
        sends, arrivals = copies(ci, co, cs)
        for cp in arrivals:
            cp.wait_recv()
        for cp in sends:
            cp.wait_send()

    return Comm("gather_ici", [wpack, placed], [jax.ShapeDtypeStruct(placed.shape, placed.dtype)], _dma_sems(3, 3),
                start, wait, aliases={1: 0})


def gather_to_sibling(landed, rows=ROWS_ALL):
    def copies(co, cs):
        (out_ref,), (send, recv) = co, cs
        x, y, c, s, chips, sibling = _place()
        slots = [2 * chip[0] + chip[1] for chip in chips]
        mine, other = _range_half(out_ref, rows, c), _range_half(out_ref, rows, 1 - c)
        sends = [_remote(out_ref.at[t, mine, :], out_ref.at[t, mine, :], send.at[j], recv.at[j], sibling)
                 for j, t in enumerate(slots)]
        arrivals = [_remote(out_ref.at[t, other, :], out_ref.at[t, other, :], send.at[j], recv.at[j], sibling)
                    for j, t in enumerate(slots)]
        return sends, arrivals

    def start(ci, co, cs):
        for cp in copies(co, cs)[0]:
            cp.start()

    def wait(ci, co, cs):
        sends, arrivals = copies(co, cs)
        for cp in arrivals:
            cp.wait_recv()
        for cp in sends:
            cp.wait_send()

    return Comm("gather_d2d", [landed], [jax.ShapeDtypeStruct(landed.shape, landed.dtype)], _dma_sems(3, 3), start, wait,
                aliases={0: 0})


def exchange_halves(gpack, rows):
    def copy(ci, co, cs):
        (g_ref,), (out_ref,), (send, recv) = ci, co, cs
        x, y, c, s, chips, sibling = _place()
        return _remote(g_ref.at[:, _range_half(g_ref, rows, 1 - c), :], out_ref, send.at[0], recv.at[0], sibling)

    return Comm("exchange", [gpack], [jax.ShapeDtypeStruct((N_SHARD, rows[1] // 2, D_MODEL), F32)], _dma_sems(1, 1),
                lambda ci, co, cs: copy(ci, co, cs).start(), lambda ci, co, cs: copy(ci, co, cs).wait())


def scatter_chips(csum):
    def copies(ci, co, cs):
        (c_ref,), (out_ref,), (send, recv) = ci, co, cs
        x, y, c, s, chips, sibling = _place()
        sends = [_remote(c_ref.at[2 * chip[0] + chip[1]], out_ref.at[s], send.at[j], recv.at[j], (*chip, c))
                 for j, chip in enumerate(chips)]
        arrivals = [_remote(c_ref.at[2 * chip[0] + chip[1]], out_ref.at[2 * chip[0] + chip[1]], send.at[j], recv.at[j],
                            (*chip, c)) for j, chip in enumerate(chips)]
        return sends, arrivals

    def start(ci, co, cs):
        for cp in copies(ci, co, cs)[0]:
            cp.start()

    def wait(ci, co, cs):
        sends, arrivals = copies(ci, co, cs)
        for cp in arrivals:
            cp.wait_recv()
        for cp in sends:
            cp.wait_send()

    return Comm("scatter", [csum], [jax.ShapeDtypeStruct(csum.shape, csum.dtype)], _dma_sems(3, 3), start, wait)


def share_reduced(gshards, smalls, l, rows):
    with_tail = rows[3]

    def copies(ci, co, cs):
        (g_ref, sm_ref), (send, recv, loc) = co, cs
        x, y, c, s, chips, sibling = _place()
        my_half = g_ref.at[l, _range_half(g_ref, rows, c), :]
        tail = g_ref.at[l, pl.ds(PACK_ROWS - TAIL_ROWS, TAIL_ROWS), :]
        big = _remote(my_half, my_half, send.at[0], recv.at[0], sibling)
        keep_tail = pltpu.make_async_copy(tail, sm_ref.at[l, s], loc.at[0])
        tails = [_remote(tail, sm_ref.at[l, s], send.at[1 + j], recv.at[1 + j], (*chip, 1)) for j, chip in enumerate(chips)]
        tails += [_remote(tail, sm_ref.at[l, s], send.at[4 + j], recv.at[1 + j], (*chip, 0)) for j, chip in enumerate(chips)]
        tails.append(_remote(tail, sm_ref.at[l, s], send.at[7], recv.at[4], sibling))
        other = g_ref.at[l, _range_half(g_ref, rows, 1 - c), :]
        big_in = _remote(other, other, send.at[0], recv.at[0], sibling)
        slots = [sm_ref.at[l, 2 * chip[0] + chip[1]] for chip in chips]
        tails_in = [_remote(slot, slot, send.at[1 + j], recv.at[1 + j], sibling) for j, slot in enumerate(slots)]
        sib_tail_in = _remote(sm_ref.at[l, s], sm_ref.at[l, s], send.at[7], recv.at[4], sibling)
        return c, big, keep_tail, tails, big_in, tails_in, sib_tail_in

    def start(ci, co, cs):
        c, big, keep_tail, tails, _, _, _ = copies(ci, co, cs)
        big.start()
        if with_tail:
            @pl.when(c == 1)
            def _():
                keep_tail.start()
                for cp in tails:
                    cp.start()

    def wait(ci, co, cs):
        c, big, keep_tail, tails, big_in, tails_in, sib_tail_in = copies(ci, co, cs)
        big_in.wait_recv()
        big.wait_send()
        if with_tail:
            for cp in tails_in:
                cp.wait_recv()

            @pl.when(c == 0)
            def _():
                sib_tail_in.wait_recv()

            @pl.when(c == 1)
            def _():
                for cp in tails:
                    cp.wait_send()
                keep_tail.wait()

    sds = lambda a: jax.ShapeDtypeStruct(a.shape, a.dtype)
    return Comm("share", [gshards, smalls], [sds(gshards), sds(smalls)], _dma_sems(8, 5, 1), start, wait,
                aliases={0: 0, 1: 1})


def _consts():
    e = np.zeros((S5_STATE, S5_STATE * S5_GROUP), np.float32)
    for p in range(S5_STATE):
        e[p, p * S5_GROUP:(p + 1) * S5_GROUP] = 1.0
    hs = np.zeros((D_SSD, 128), np.float32)
    for h in range(SSD_HEADS):
        hs[h * SSD_HEADDIM:(h + 1) * SSD_HEADDIM, h] = 1.0
    ones = np.ones((SSD_L, SSD_L), np.float32)
    return dict(expand=jnp.asarray(e), expand_t=jnp.asarray(e.T), headsum=jnp.asarray(hs).astype(BF16),
                tril=jnp.asarray(np.tril(ones)), triu=jnp.asarray(np.triu(ones)), trils=jnp.asarray(np.tril(ones, -1)))


def _s5_mats(w, cst):
    b_re = w["s5_b_re"].reshape(S5_GROUPS, S5_STATE * S5_GROUP)
    b_im = w["s5_b_im"].reshape(S5_GROUPS, S5_STATE * S5_GROUP)
    abr, abi, apr, api, bbr, bbi = s5_params_fwd(w["s5_a_re"], w["s5_a_im"], w["s5_log_dt"].reshape(S5_GROUPS, 1),
                                                 b_re, b_im, cst["expand"], name="s5_params_fwd")
    t = lambda v: v.reshape(S5_CH, 8, S5_STATE, S5_GROUP).transpose(0, 1, 3, 2)
    c4 = lambda v: v.reshape(S5_CH, 8, S5_GROUP, S5_STATE).transpose(0, 1, 3, 2)
    vec = lambda v: v.reshape(S5_CH, 1, S5_NST)
    return dict(bbr=_s5_blockdiag(t(bbr), S5_GROUP, S5_STATE).astype(BF16),
                bbi=_s5_blockdiag(t(bbi), S5_GROUP, S5_STATE).astype(BF16),
                ccr=_s5_blockdiag(c4(w["s5_c_re"]), S5_STATE, S5_GROUP).astype(BF16),
                cci=_s5_blockdiag(c4(w["s5_c_im"]), S5_STATE, S5_GROUP).astype(BF16),
                abr=vec(abr), abi=vec(abi), apr=vec(apr), api=vec(api),
                dsk=w["s5_d"].reshape(S5_CH, 1, S5_CW), b_re=b_re, b_im=b_im)


def _layer_fwd(x, mem, w, cst, next_pack=None, rest=None):
    sv = {}
    g = lambda n: w[n].reshape(1, -1)
    res = norm_matmul(x, g("norm_mix"), w["w_in_t"], w["w_dt_t"], tm=TM, tn=1024, name="in_proj", w_transposed=True,
                      comm=gather_over_ici(rest[0], rest[1], ROWS_EARLY) if rest else None)
    (proj, h0, r0, dtraw), staged = res if rest else (res, None)
    s5m = _s5_mats(w, cst)
    comm = _combine(gather_to_sibling(staged[0], ROWS_EARLY) if rest else None,
                    gather_over_ici(*next_pack) if next_pack else None)
    (ys, csr, csi), landed = s5_fwd(proj, s5m["bbr"], s5m["bbi"], s5m["ccr"], s5m["cci"], s5m["abr"], s5m["abi"],
                                    s5m["apr"], s5m["api"], s5m["dsk"], name="s5_fwd", comm=comm)
    if rest:
        w = dict(w, **_pack_weights(landed[0]))
        landed = landed[1:]
    pack = w["pack"]
    (ymix, tglu), gathered = glu_fwd(ys, w["s5_w_glu"], tm=TM, name="glu_fwd",
                                     comm=None if next_pack is None else gather_to_sibling(landed[0]))
    ssdp = _ssd_params(w["ssd_conv_w"], w["ssd_conv_b"], w["ssd_dt_bias"], w["ssd_a_log"], w["ssd_d"], w["ssd_norm"])
    (ymix, hs), _ = ssd_fwd(proj, dtraw, ssdp, cst["tril"], ymix, name="ssd_fwd")
    x1 = matmul_res(ymix, pack, x, tm=TM, tn=1024, tk=1024, name="out_proj",
                    wspec=pl.BlockSpec((2, 512, 1024), lambda i, j, k: (k, OFF["w_out"] // 512, j)))
    q, h1, r1 = norm_matmul(x1, g("norm_xattn"), w["xa_wq"], tm=TM, tn=1024, name="q_proj")
    kv, hm, rm = norm_matmul(mem, g("norm_mem"), w["xa_wkv"], tm=mem.shape[0], tn=1024, name="kv_proj")
    o = attn_fwd(q, kv, tm=TM, name="attn_fwd")
    x2 = matmul_res(o, w["xa_wo"], x1, tm=TM, tn=1024, tk=1024, name="attn_out")
    f1, h2, r2 = norm_matmul(x2, g("norm_mlp"), pack, tm=TM, tn=1024, name="mlp_up", n_out=D_FF,
                             wspec=pl.BlockSpec((1, D_MODEL, 1024), lambda i, j: (j, OFF["mlp_w1"] // 1024, 0)))
    x3 = matmul_res(f1, pack, x2, act="relu2", tm=TM, tn=1024, tk=1024, name="mlp_down",
                    wspec=pl.BlockSpec((1, 1024, 1024), lambda i, j, k: (k, OFF["mlp_w2"] // 1024, j)))
    sv.update(x=x, proj=proj, h0=h0, r0=r0, dtraw=dtraw, s5m=s5m, ys=ys, csr=csr, csi=csi, tglu=tglu, ssdp=ssdp,
              hs=hs, ymix=ymix, x1=x1, q=q, h1=h1, r1=r1, kv=kv, hm=hm, rm=rm, o=o, x2=x2, f1=f1, h2=h2, r2=r2)
    return x3, sv, (gathered[0] if next_pack is not None else None), w


def _layer_bwd(dx3, mem, w, sv, cst, extra_small=None, reduce_hooks=None, early=None):
    gr = {}
    g = lambda n: w[n].reshape(1, -1)
    pack = w["pack"]
    pshape = (N_SHARD, PACK_ROWS, D_MODEL)
    ps = lambda rows, f: pl.BlockSpec((1, rows, 1024), f)
    ps4 = lambda rows, f: pl.BlockSpec((N_SHARD, rows, 1024), f)
    dh1 = matmul_nt(dx3, pack, epi="relu2bwd", epi_args=(sv["f1"],), tm=TM, tko=1024, tn=1024, name="mlp_down_dx",
                    wspec=ps(1024, lambda i, k, n: (k, OFF["mlp_w2"] // 1024, n)), k_out=D_FF,
                    comm=reduce_hooks.exchange() if reduce_hooks else None)
    if reduce_hooks:
        dh1, recv = dh1
        reduce_hooks.after_exchange(recv[0])
    gp = matmul_tn(sv["f1"], dx3, act="relu2", tk=1024, tn=1024, tt=TM, name="mlp_down_dw", pack_shape=pshape,
                   pack_spec=ps(1024, lambda k, n, t: (k, OFF["mlp_w2"] // 1024, 0)))
    gp = matmul_tn(sv["h2"], dh1, tk=1024, tn=1024, tt=TM, name="mlp_up_dw", pack=gp, pack_shape=pshape,
                   pack_spec=ps(1024, lambda k, n, t: (n, OFF["mlp_w1"] // 1024, 0)))
    dx2, gr["norm_mlp"] = matmul_nt(dh1, pack, epi="rmsbwd", epi_args=(sv["x2"], sv["r2"], g("norm_mlp"), dx3),
                                    tm=TM, tko=1024, tn=1024, name="mlp_up_dx", k_out=D_MODEL,
                                    wspec=ps(1024, lambda i, k, n: (n, OFF["mlp_w1"] // 1024, 0)))
    do = matmul_nt(dx2, w["xa_wo"], tm=TM, tko=1024, tn=1024, name="attn_out_dx")
    gp = matmul_tn(sv["o"], dx2, tk=1024, tn=1024, tt=TM, name="attn_out_dw", pack=gp, pack_shape=pshape,
                   pack_spec=ps4(256, lambda k, n, t: (0, OFF["xa_wo"] // 256, 0)))
    dq, dkv = attn_bwd(sv["q"], sv["kv"], do, tm=TM, name="attn_bwd")
    gp = matmul_tn(sv["h1"], dq, tk=1024, tn=1024, tt=TM, name="q_proj_dw", pack=gp, pack_shape=pshape,
                   pack_spec=ps4(256, lambda k, n, t: (0, OFF["xa_wq"] // 256, 0)))
    dx1, gr["norm_xattn"] = matmul_nt(dq, w["xa_wq"], epi="rmsbwd", epi_args=(sv["x1"], sv["r1"], g("norm_xattn"), dx2),
                                      tm=TM, tko=1024, tn=1024, name="q_proj_dx")
    M = mem.shape[0]
    gp = matmul_tn(sv["hm"], dkv, tk=1024, tn=1024, tt=M, name="kv_proj_dw", pack=gp, pack_shape=pshape,
                   pack_spec=ps4(256, lambda k, n, t: (0, OFF["xa_wk"] // 256 + n, 0)))
    _, gr["norm_mem"] = matmul_nt(dkv, w["xa_wkv"], epi="rmsbwd",
                                  epi_args=(mem, sv["rm"], g("norm_mem"), jnp.zeros_like(mem)),
                                  tm=M, tko=1024, tn=1024, name="kv_proj_dx")
    dymix = matmul_nt(dx1, pack, tm=TM, tko=1024, tn=1024, name="out_proj_dx", k_out=2 * D_MODEL,
                      wspec=pl.BlockSpec((2, 512, 1024), lambda i, k, n: (k, OFF["w_out"] // 512, n)))
    gp = matmul_tn(sv["ymix"], dx1, tk=2048, tn=1024, tt=TM, name="out_proj_dw", pack=gp, pack_shape=pshape,
                   pack_spec=ps4(512, lambda k, n, t: (0, OFF["w_out"] // 512, 0)))
    (dproj, ddtraw, dcw, dcb, ddtb, dal, ddsk, dnw), parts = ssd_bwd(
        sv["proj"], sv["dtraw"], sv["hs"], dymix, sv["ssdp"], cst["tril"], cst["triu"], cst["trils"], cst["headsum"],
        name="ssd_bwd", comm=reduce_hooks.scatter() if reduce_hooks else None)
    if reduce_hooks:
        reduce_hooks.after_scatter(parts[0])
    gr["ssd_conv_w"], gr["ssd_conv_b"] = dcw[:SSD_CONV], dcb[0]
    gr["ssd_dt_bias"], gr["ssd_a_log"], gr["ssd_d"] = ddtb[0, :SSD_HEADS], dal[0, :SSD_HEADS], ddsk[0, :SSD_HEADS]
    gr["ssd_norm"] = dnw[0]
    gp = glu_dw(dymix, sv["ys"], sv["tglu"], gp, tt=TM, name="glu_dw",
                pack_spec=ps4(256, lambda i: (0, OFF["s5_w_glu"] // 256, 0)))
    first = _Reduction(gp, early[0], early[1], None, None, ROWS_EARLY) if early else None
    comm = _combine(reduce_hooks.share() if reduce_hooks else None, first.exchange() if first else None)
    (dys,), outs = glu_bwd(dymix, sv["ys"], sv["tglu"], w["s5_w_glu"], tm=TM, name="glu_dx", comm=comm)
    if comm is not None:
        if reduce_hooks:
            reduce_hooks.after_share(outs[:2])
            outs = outs[2:]
        if first:
            first.gshards, first.smalls = reduce_hooks.gshards, reduce_hooks.smalls
            first.after_exchange(outs[0])
    s5m = sv["s5m"]
    (dproj, dbbr, dbbi, dccr, dcci, dd, dabr, dabi), parts = s5_bwd(
        sv["proj"], dys, sv["csr"], sv["csi"], s5m["bbr"], s5m["bbi"], s5m["ccr"], s5m["cci"], s5m["abr"], s5m["abi"],
        s5m["apr"], s5m["api"], s5m["dsk"], dproj, name="s5_bwd", comm=first.scatter() if first else None)
    if first:
        first.after_scatter(parts[0])
    tb = lambda v: _s5_blockdiag_extract(v, S5_GROUP, S5_STATE).transpose(0, 1, 3, 2).reshape(S5_GROUPS, -1)
    tc = lambda v: _s5_blockdiag_extract(v, S5_STATE, S5_GROUP).transpose(0, 1, 3, 2).reshape(S5_GROUPS, S5_GROUP, S5_STATE)
    gr["s5_c_re"], gr["s5_c_im"] = tc(dccr), tc(dcci)
    gr["s5_d"] = dd.reshape(S5_GROUPS, S5_GROUP)
    dar, dai, dld, dbr, dbi = s5_params_bwd(
        w["s5_a_re"], w["s5_a_im"], w["s5_log_dt"].reshape(S5_GROUPS, 1), s5m["b_re"], s5m["b_im"],
        dabr.reshape(S5_GROUPS, S5_STATE), dabi.reshape(S5_GROUPS, S5_STATE), tb(dbbr), tb(dbbi),
        cst["expand"], cst["expand_t"], name="s5_params_bwd")
    gr["s5_a_re"], gr["s5_a_im"], gr["s5_log_dt"] = dar, dai, dld[:, 0]
    gr["s5_b_re"] = dbr.reshape(S5_GROUPS, S5_STATE, S5_GROUP)
    gr["s5_b_im"] = dbi.reshape(S5_GROUPS, S5_STATE, S5_GROUP)
    wt_shape = (D_MAIN + D_DT_PAD, D_MODEL)
    dwt = matmul_tn(dproj, sv["h0"], tk=1024, tn=1024, tt=TM, name="in_proj_dw", pack_shape=wt_shape,
                    pack_spec=pl.BlockSpec((1024, 1024), lambda k, n, t: (k, 0)))
    dwt = matmul_tn(ddtraw, sv["h0"], tk=D_DT_PAD, tn=1024, tt=TM, name="in_proj_dt_dw", pack=dwt, pack_shape=wt_shape,
                    pack_spec=pl.BlockSpec((D_DT_PAD, 1024), lambda k, n, t: (D_MAIN // D_DT_PAD, 0)))
    dx0 = matmul_nt(dproj, w["w_in_t"], g2=ddtraw, w2=w["w_dt_t"], w_is_nk=True, epi="rmsbwd",
                    epi_args=(sv["x"], sv["r0"], g("norm_mix"), dx1), tm=TM, tko=1024, tn=1024, name="in_proj_dx",
                    comm=first.share() if first else None)
    if first:
        dx0, shared = dx0
        first.after_share(shared)
    dx0, gr["norm_mix"] = dx0
    gr = {k: (v[0] if k.startswith("norm_") else v) for k, v in gr.items()}
    for t in range(N_SHARD):
        shard_rows = lax.slice_in_dim(dwt, t * ROWS["w_in"], (t + 1) * ROWS["w_in"], axis=0)
        gp = lax.dynamic_update_slice(gp, shard_rows[None], (t, OFF["w_in"], 0))
    small = [gr[n].reshape(-1) for n in SMALL_L] + ([] if extra_small is None else [extra_small.reshape(-1)])
    small = jnp.concatenate(small)
    small = jnp.pad(small, (0, N_SHARD * SMALL_Q * D_MODEL - small.size)).reshape(N_SHARD, SMALL_Q, D_MODEL)
    gap = TAIL_OFF - OFF["w_in"] - ROWS["w_in"]
    gp = lax.dynamic_update_slice(gp, jnp.pad(small, ((0, 0), (gap, TAIL_ROWS - SMALL_Q), (0, 0))),
                                  (0, TAIL_OFF - gap, 0))
    return dx0, gp, first


def _local_step(x, mem, target, layers, norm_final):
    cst = _consts()
    saved = []
    for l in range(DEPTH):
        x, sv, _, _ = _layer_fwd(x, mem, layers[l], cst)
        saved.append(sv)
    loss, dx, dgf = loss_head(x, norm_final.reshape(1, -1), target, tm=TM, name="loss_head")
    packs = [None] * DEPTH
    for l in reversed(range(DEPTH)):
        dx, packs[l], _ = _layer_bwd(dx, mem, layers[l], saved[l], cst, extra_small=dgf[0] if l == DEPTH - 1 else None)
    return loss, dx, packs


def sum_halves(gpack, recv, c_idx, rows):
    first, count, tile, _ = rows
    half = count // 2
    nb = half // tile

    def body(c_ref, a_ref, b_ref, o_ref):
        o_ref[...] = (a_ref[...] + b_ref[...]).astype(BF16)

    blk = (1, tile, D_MODEL)
    return pl.pallas_call(
        body,
        grid_spec=pltpu.PrefetchScalarGridSpec(
            num_scalar_prefetch=1, grid=(N_SHARD, nb),
            in_specs=[pl.BlockSpec(blk, lambda t, i, c_ref: (t, first // tile + c_ref[0] * nb + i, 0)),
                      pl.BlockSpec(blk, lambda t, i, c_ref: (t, i, 0))],
            out_specs=pl.BlockSpec(blk, lambda t, i, c_ref: (t, i, 0))),
        out_shape=jax.ShapeDtypeStruct((N_SHARD, half, D_MODEL), BF16),
        compiler_params=_cp("parallel", "parallel"), name="sum_halves")(c_idx, gpack, recv)


def sum_chips(parts, csum, gshards, l, place_idx, rows):
    first, count, tile, _ = rows
    nb = count // 2 // tile

    def body(pi_ref, p0, p1, p2, p3, own, g_ref, o_ref):
        s = pi_ref[0]
        vals = [jnp.where(s == k, own[0], p[0]).astype(F32) for k, p in enumerate((p0, p1, p2, p3))]
        o_ref[0] = ((vals[0] + vals[1]) + vals[2]) + vals[3]

    blk = (1, tile, D_MODEL)
    part_spec = lambda k: pl.BlockSpec(blk, lambda i, pi_ref: (jnp.where(pi_ref[0] == k, (k + 1) % N_SHARD, k), i, 0))
    return pl.pallas_call(
        body,
        grid_spec=pltpu.PrefetchScalarGridSpec(
            num_scalar_prefetch=1, grid=(nb,),
            in_specs=[part_spec(k) for k in range(N_SHARD)]
                     + [pl.BlockSpec(blk, lambda i, pi_ref: (pi_ref[0], i, 0)), _ANY],
            out_specs=pl.BlockSpec(blk, lambda i, pi_ref: (l, first // tile + pi_ref[1] * nb + i, 0))),
        out_shape=jax.ShapeDtypeStruct(gshards.shape, F32), input_output_aliases={6: 0},
        compiler_params=_cp("parallel"), name="sum_chips")(place_idx, parts, parts, parts, parts, csum, gshards)


class _Reduction:
    def __init__(self, gpack, layer, place_idx, gshards, smalls, rows=ROWS_ALL):
        self.gpack, self.layer, self.place_idx, self.gshards, self.smalls = gpack, layer, place_idx, gshards, smalls
        self.rows = rows

    def exchange(self):
        return exchange_halves(self.gpack, self.rows)

    def after_exchange(self, recv):
        self.csum = sum_halves(self.gpack, recv, self.place_idx[1:], self.rows)

    def scatter(self):
        return scatter_chips(self.csum)

    def after_scatter(self, parts):
        self.gshards = sum_chips(parts, self.csum, self.gshards, self.layer, self.place_idx, self.rows)

    def share(self):
        return share_reduced(self.gshards, self.smalls, self.layer, self.rows)

    def after_share(self, shared):
        self.gshards, self.smalls = shared

    def run_alone(self):
        self.after_exchange(_comm_only(self.exchange())[0])
        self.after_scatter(_comm_only(self.scatter())[0])
        self.after_share(_comm_only(self.share()))
        return self.gshards, self.smalls


def adamw(w, g, m, v, *, name, comm=None):
    shape = w.shape
    cols = shape[-1]
    rows = w.size // cols
    tr = 512 if rows % 512 == 0 else rows
    c1 = 1.0 / (1.0 - ADAM_B1 ** ADAM_STEP)
    c2 = 1.0 / (1.0 - ADAM_B2 ** ADAM_STEP)

    def body(w_ref, g_ref, m_ref, v_ref, d_ref, nm_ref, nv_ref):
        gv = g_ref[...]
        nm = ADAM_B1 * m_ref[...] + (1.0 - ADAM_B1) * gv
        nv = ADAM_B2 * v_ref[...] + (1.0 - ADAM_B2) * (gv * gv)
        d_ref[...] = -ADAM_LR * ((nm * c1) / (jnp.sqrt(nv * c2) + ADAM_EPS) + ADAM_WD * w_ref[...])
        nm_ref[...] = nm
        nv_ref[...] = nv

    spec = pl.BlockSpec((tr, cols), lambda i: (i, 0))
    sds = jax.ShapeDtypeStruct((rows, cols), F32)
    outs, couts = _call(body, grid=(rows // tr,), in_specs=[spec] * 4, out_specs=[spec] * 3, out_shape=[sds] * 3,
                        sem=("parallel",), name=name, args=[a.reshape(rows, cols) for a in (w, g, m, v)], comm=comm)
    outs = [o.reshape(shape) for o in outs]
    return outs if comm is None else (outs, couts)


def _own_pack(wts, l):
    rows = [wts[n][l].T if n == "w_in" else wts[n][l].reshape(ROWS[n], D_MODEL) for n in sorted(OFF, key=OFF.get)]
    cw = wts["ssd_conv_w"][l].reshape(-1)
    hi = lax.reduce_precision(cw, 8, 7)
    mid = lax.reduce_precision(cw - hi, 8, 7)
    lo = lax.reduce_precision(cw - hi - mid, 8, 7)
    conv = jnp.pad(jnp.concatenate([hi, mid, lo]), (0, CONV_ROWS * D_MODEL - 3 * cw.size)).reshape(CONV_ROWS, D_MODEL)
    gap = jnp.zeros((TAIL_OFF - OFF["w_in"] - ROWS["w_in"], D_MODEL), F32)
    rest = jnp.zeros((TAIL_ROWS - CONV_ROWS, D_MODEL), F32)
    return jnp.concatenate(rows + [gap, conv, rest], axis=0).astype(BF16)


def _square(gathered, n):
    return gathered[:, OFF[n]:OFF[n] + ROWS[n]].reshape(N_SHARD * ROWS[n], D_MODEL)


def _pack_weights(gathered):
    w = {"pack": gathered}
    w["s5_w_glu"], w["xa_wq"], w["xa_wo"] = _square(gathered, "s5_w_glu"), _square(gathered, "xa_wq"), _square(gathered, "xa_wo")
    w["xa_wkv"] = jnp.concatenate([_square(gathered, "xa_wk"), _square(gathered, "xa_wv")], axis=1)
    return w


def _layer_weights(gathered, wts, l, late_only=False):
    w = {n: wts[n][l] for n in SMALL_L if n != "ssd_conv_w"}
    if not late_only:
        w.update(_pack_weights(gathered))
    w_in_t = _square(gathered, "w_in")
    w["w_in_t"] = w_in_t[:D_MAIN]
    w["w_dt_t"] = jnp.pad(w_in_t[D_MAIN:], ((0, D_DT_PAD - SSD_HEADS), (0, 0)))
    per = SSD_CONV * D_XBC // N_SHARD
    cw = gathered[:, TAIL_OFF:TAIL_OFF + CONV_ROWS].astype(F32).reshape(N_SHARD, -1)[:, :3 * per]
    cw = cw.reshape(N_SHARD, 3, SSD_CONV, D_XBC // N_SHARD)
    cw = (cw[:, 0] + cw[:, 1]) + cw[:, 2]
    w["ssd_conv_w"] = cw.transpose(1, 0, 2).reshape(SSD_CONV, D_XBC)
    return w


def kernel(x, mem, norm_mix, w_in, s5_a_re, s5_a_im, s5_log_dt, s5_b_re, s5_b_im, s5_c_re, s5_c_im, s5_d, s5_w_glu, ssd_conv_w, ssd_conv_b, ssd_dt_bias, ssd_a_log, ssd_d, ssd_norm, w_out, norm_xattn, norm_mem, xa_wq, xa_wk, xa_wv, xa_wo, norm_mlp, mlp_w1, mlp_w2, norm_final, loss_target, m_norm_mix, m_w_in, m_s5_a_re, m_s5_a_im, m_s5_log_dt, m_s5_b_re, m_s5_b_im, m_s5_c_re, m_s5_c_im, m_s5_d, m_s5_w_glu, m_ssd_conv_w, m_ssd_conv_b, m_ssd_dt_bias, m_ssd_a_log, m_ssd_d, m_ssd_norm, m_w_out, m_norm_xattn, m_norm_mem, m_xa_wq, m_xa_wk, m_xa_wv, m_xa_wo, m_norm_mlp, m_mlp_w1, m_mlp_w2, m_norm_final, v_norm_mix, v_w_in, v_s5_a_re, v_s5_a_im, v_s5_log_dt, v_s5_b_re, v_s5_b_im, v_s5_c_re, v_s5_c_im, v_s5_d, v_s5_w_glu, v_ssd_conv_w, v_ssd_conv_b, v_ssd_dt_bias, v_ssd_a_log, v_ssd_d, v_ssd_norm, v_w_out, v_norm_xattn, v_norm_mem, v_xa_wq, v_xa_wk, v_xa_wv, v_xa_wo, v_norm_mlp, v_mlp_w1, v_mlp_w2, v_norm_final):
    names = ("norm_mix", "w_in", "s5_a_re", "s5_a_im", "s5_log_dt", "s5_b_re", "s5_b_im", "s5_c_re", "s5_c_im", "s5_d",
             "s5_w_glu", "ssd_conv_w", "ssd_conv_b", "ssd_dt_bias", "ssd_a_log", "ssd_d", "ssd_norm", "w_out",
             "norm_xattn", "norm_mem", "xa_wq", "xa_wk", "xa_wv", "xa_wo", "norm_mlp", "mlp_w1", "mlp_w2", "norm_final")
    loc = locals()
    wts = {n: loc[n] for n in names}
    mom = {n: loc["m_" + n] for n in names}
    var = {n: loc["v_" + n] for n in names}
    shard = 2 * lax.axis_index("x") + lax.axis_index("y")
    core = lax.axis_index("c")

    cst = _consts()
    place_idx = jnp.stack([shard, core]).astype(jnp.int32)
    h, mem0 = x[0], mem[0]

    own0 = _own_pack(wts, 0)
    staged = _comm_only(gather_over_ici(own0, place_own(own0, place_idx[:1]), ROWS_LATE))[0]
    gathered = _comm_only(gather_to_sibling(staged, ROWS_LATE))[0]
    layers, saved = [], []
    for l in range(DEPTH):
        nxt = None
        if l + 1 < DEPTH:
            own = _own_pack(wts, l + 1)
            nxt = (own, place_own(own, place_idx[:1]))
        h, sv, gathered, w = _layer_fwd(h, mem0, _layer_weights(gathered, wts, l, late_only=l == 0), cst, next_pack=nxt,
                                        rest=(own0, gathered) if l == 0 else None)
        layers.append(w)
        saved.append(sv)
    loss, dx, dgf = loss_head(h, norm_final.reshape(1, -1), loss_target[0], tm=TM, name="loss_head")

    gshards = jnp.zeros((DEPTH, PACK_ROWS, D_MODEL), F32)
    smalls = jnp.zeros((DEPTH, N_SHARD, TAIL_ROWS, D_MODEL), F32)
    pending = None
    for l in reversed(range(DEPTH)):
        dx, gpack, first = _layer_bwd(dx, mem0, layers[l], saved[l], cst, extra_small=dgf[0] if l == DEPTH - 1 else None,
                                      reduce_hooks=pending, early=(l, place_idx) if l == 0 else None)
        if pending is not None:
            gshards, smalls = pending.gshards, pending.smalls
        pending = _Reduction(gpack, l, place_idx, gshards, smalls)
    g = {n: first.gshards[:, OFF[n]:OFF[n] + ROWS[n]].reshape(wts[n].shape) for n in OFF if n != "w_in"}
    late = _Reduction(gpack, 0, place_idx, first.gshards, first.smalls, ROWS_LATE)
    deltas, new_m, new_v = {}, {}, {}
    for n, make, take in (("mlp_w2", late.exchange, lambda o: late.after_exchange(o[0])),
                          ("mlp_w1", late.scatter, lambda o: late.after_scatter(o[0])),
                          ("w_out", late.share, late.after_share)):
        (deltas[n], new_m[n], new_v[n]), outs = adamw(wts[n], g[n], mom[n], var[n], name="adamw_" + n, comm=make())
        take(outs)
    gshards, smalls = late.gshards, late.smalls
    g["w_in"] = gshards[:, OFF["w_in"]:OFF["w_in"] + ROWS["w_in"]].transpose(0, 2, 1)
    small_red = smalls[:, :, :SMALL_Q].reshape(DEPTH, -1)
    off = 0
    for n in SMALL_L:
        shape = (SSD_CONV, D_XBC) if n == "ssd_conv_w" else wts[n].shape[1:]
        size = math.prod(shape)
        g[n] = small_red[:, off:off + size].reshape((DEPTH,) + shape)
        off += size
    g["norm_final"] = small_red[DEPTH - 1, off:off + D_MODEL]
    g["ssd_conv_w"] = lax.dynamic_slice_in_dim(g["ssd_conv_w"], shard * (D_XBC // N_SHARD), D_XBC // N_SHARD, axis=2)

    for n in names:
        if n not in deltas:
            deltas[n], new_m[n], new_v[n] = adamw(wts[n], g[n], mom[n], var[n], name="adamw_" + n)
    loss_all = lax.psum(loss[0, 0], ("x", "y", "c"))
    return (loss_all, dx[None], *[g[n] for n in names], *[deltas[n] for n in names], *[new_m[n] for n in names],
            *[new_v[n] for n in names])
```

```python
import functools
import math

import jax
import jax.numpy as jnp
import numpy as np
from jax import lax
from jax.experimental import pallas as pl
from jax.experimental.pallas import tpu as pltpu

F32 = jnp.float32
BF16 = jnp.bfloat16
HIGHEST = lax.Precision.HIGHEST

D_MODEL = 1024
DEPTH = 4
D_S5 = 1024
D_SSD = 1024
S5_GROUP = 16
S5_GROUPS = 64
S5_STATE = 64
SSD_HEADS = 16
SSD_HEADDIM = 64
SSD_NGROUPS = 4
SSD_STATE = 128
SSD_CONV = 4
SSD_CHUNK = 128
D_XBC = 2048
D_MAIN = 4096
D_DT_PAD = 128
XA_HEADS = 4
XA_HEAD_DIM = 256
D_FF = 4096
EPS = 1e-5
ADAM_LR, ADAM_B1, ADAM_B2, ADAM_EPS, ADAM_WD, ADAM_STEP = 0.001, 0.9, 0.999, 1e-08, 0.01, 10

VMEM_LIMIT = 56 * 1024 * 1024
MESH_T = pl.DeviceIdType.MESH


def _cp(*sem):
    return pltpu.CompilerParams(dimension_semantics=tuple(sem) if sem else None, vmem_limit_bytes=VMEM_LIMIT)


_ANY = pl.BlockSpec(memory_space=pl.ANY)


class Comm:
    def __init__(self, name, inputs, out_shapes, sems, start, wait, aliases=None):
        self.name, self.inputs, self.out_shapes, self.sems = name, list(inputs), list(out_shapes), list(sems)
        self.start, self.wait, self.aliases = start, wait, dict(aliases or {})


def _call(body, *, grid, in_specs, out_specs, out_shape, args, scratch_shapes=(), sem, name, comm=None, aliases=None):
    in_specs, out_specs, out_shape = list(in_specs), list(out_specs), list(out_shape)
    scratch_shapes = list(scratch_shapes)
    aliases = dict(aliases or {})
    if comm is None:
        res = pl.pallas_call(body, grid=grid, in_specs=in_specs, out_specs=out_specs, out_shape=out_shape,
                             scratch_shapes=scratch_shapes, compiler_params=_cp(*sem), name=name,
                             input_output_aliases=aliases)(*args)
        return list(res), []
    n_in, n_out, n_scr = len(in_specs), len(out_specs), len(scratch_shapes)
    c_in, c_out = len(comm.inputs), len(comm.out_shapes)

    def wrapped(*refs):
        a, refs = refs[:n_in], refs[n_in:]
        ci, refs = refs[:c_in], refs[c_in:]
        o, refs = refs[:n_out], refs[n_out:]
        co, refs = refs[:c_out], refs[c_out:]
        s, cs = refs[:n_scr], refs[n_scr:]
        first = functools.reduce(jnp.logical_and, [pl.program_id(d) == 0 for d in range(len(grid))])
        last = functools.reduce(jnp.logical_and, [pl.program_id(d) == grid[d] - 1 for d in range(len(grid))])

        @pl.when(first)
        def _():
            comm.start(ci, co, cs)

        body(*a, *o, *s)

        @pl.when(last)
        def _():
            comm.wait(ci, co, cs)

    for i, j in comm.aliases.items():
        aliases[n_in + i] = n_out + j
    res = pl.pallas_call(wrapped, grid=grid, in_specs=in_specs + [_ANY] * c_in, out_specs=out_specs + [_ANY] * c_out,
                         out_shape=out_shape + comm.out_shapes, scratch_shapes=scratch_shapes + comm.sems,
                         compiler_params=_cp(*(("arbitrary",) * len(grid))), name=name + "_" + comm.name,
                         input_output_aliases=aliases)(*args, *comm.inputs)
    return list(res[:n_out]), list(res[n_out:])


def _combine(a, b):
    if a is None or b is None:
        return a or b
    ni, no, ns = len(a.inputs), len(a.out_shapes), len(a.sems)

    def start(ci, co, cs):
        a.start(ci[:ni], co[:no], cs[:ns])
        b.start(ci[ni:], co[no:], cs[ns:])

    def wait(ci, co, cs):
        a.wait(ci[:ni], co[:no], cs[:ns])
        b.wait(ci[ni:], co[no:], cs[ns:])

    aliases = dict(a.aliases)
    aliases.update({ni + i: no + j for i, j in b.aliases.items()})
    return Comm(a.name + "_" + b.name, a.inputs + b.inputs, a.out_shapes + b.out_shapes, a.sems + b.sems, start, wait,
                aliases)


def _comm_only(comm):
    def body(*refs):
        ci, refs = refs[:len(comm.inputs)], refs[len(comm.inputs):]
        co, cs = refs[:len(comm.out_shapes)], refs[len(comm.out_shapes):]
        comm.start(ci, co, cs)
        comm.wait(ci, co, cs)

    res = pl.pallas_call(body, in_specs=[_ANY] * len(comm.inputs), out_specs=[_ANY] * len(comm.out_shapes),
                         out_shape=comm.out_shapes, scratch_shapes=comm.sems, name=comm.name,
                         input_output_aliases=comm.aliases)(*comm.inputs)
    return list(res)


def _dot(a, b):
    return jnp.dot(a, b, preferred_element_type=F32)


def _dot_nt(a, b):
    return lax.dot_general(a, b, (((1,), (1,)), ((), ())), preferred_element_type=F32)


def _dot_tn(a, b):
    return lax.dot_general(a, b, (((0,), (0,)), ((), ())), preferred_element_type=F32)


def _dot_hi(a, b):
    return jnp.dot(a, b, precision=HIGHEST, preferred_element_type=F32)


def _gelu(x):
    c = math.sqrt(2.0 / math.pi)
    return 0.5 * x * (1.0 + jnp.tanh(c * (x + 0.044715 * x * x * x)))


def _gelu_grad(x):
    c = math.sqrt(2.0 / math.pi)
    t = jnp.tanh(c * (x + 0.044715 * x * x * x))
    return 0.5 * (1.0 + t) + 0.5 * x * (1.0 - t * t) * c * (1.0 + 3 * 0.044715 * x * x)


def _sigmoid(x):
    return 1.0 / (1.0 + jnp.exp(-x))


def _act(a, act):
    if act is None:
        return a.astype(BF16)
    a = a.astype(F32)
    if act == "relu2":
        a = jnp.maximum(a, 0.0)
        return (a * a).astype(BF16)
    if act == "gelu":
        return _gelu(a).astype(BF16)
    raise ValueError(act)


def _pack_block(ref):
    return ref[...].reshape(-1, ref.shape[-1])


def norm_matmul(x, g, w, w2=None, *, tm, tn, name, wspec=None, n_out=None, w_transposed=False, comm=None):
    T, D = x.shape
    N = n_out if wspec is not None else (w.shape[0] if w_transposed else w.shape[1])
    wget = (lambda r: r[...]) if wspec is None else _pack_block
    mm = _dot_nt if w_transposed else _dot
    has2 = w2 is not None

    def body(x_ref, g_ref, w_ref, *rest):
        if has2:
            w2_ref, o_ref, h_ref, r_ref, o2_ref = rest
        else:
            o_ref, h_ref, r_ref = rest
        j = pl.program_id(1)

        @pl.when(j == 0)
        def _():
            xv = x_ref[...]
            r = lax.rsqrt(jnp.mean(xv * xv, axis=-1, keepdims=True) + EPS)
            h = (xv * r * g_ref[...]).astype(BF16)
            h_ref[...] = h
            r_ref[...] = r
            if has2:
                o2_ref[...] = mm(h, w2_ref[...])

        o_ref[...] = mm(h_ref[...], wget(w_ref)).astype(o_ref.dtype)

    if wspec is None:
        wspec = pl.BlockSpec((tn, D), lambda i, j: (j, 0)) if w_transposed else pl.BlockSpec((D, tn), lambda i, j: (0, j))
    in_specs = [pl.BlockSpec((tm, D), lambda i, j: (i, 0)), pl.BlockSpec((1, D), lambda i, j: (0, 0)), wspec]
    out_shape = [jax.ShapeDtypeStruct((T, N), BF16), jax.ShapeDtypeStruct((T, D), BF16),
                 jax.ShapeDtypeStruct((T, 1), F32)]
    out_specs = [pl.BlockSpec((tm, tn), lambda i, j: (i, j)), pl.BlockSpec((tm, D), lambda i, j: (i, 0)),
                 pl.BlockSpec((tm, 1), lambda i, j: (i, 0))]
    args = [x, g, w]
    if has2:
        in_specs.append(pl.BlockSpec(w2.shape, lambda i, j: (0, 0)))
        out_shape.append(jax.ShapeDtypeStruct((T, D_DT_PAD), F32))
        out_specs.append(pl.BlockSpec((tm, D_DT_PAD), lambda i, j: (i, 0)))
        args.append(w2)
    res, cres = _call(body, grid=(T // tm, N // tn), in_specs=in_specs, out_specs=out_specs, out_shape=out_shape,
                      sem=("parallel", "arbitrary"), name=name, args=args, comm=comm)
    return res if comm is None else (res, cres)


def matmul_res(a, w, r, *, act=None, tm, tn, tk, name, wspec=None):
    T, K = a.shape
    N = r.shape[1]
    wget = (lambda r_: r_[...]) if wspec is None else _pack_block
    nk = K // tk

    def body(a_ref, w_ref, r_ref, o_ref):
        k = pl.program_id(2)

        @pl.when(k == 0)
        def _():
            o_ref[...] = r_ref[...]

        o_ref[...] += _dot(_act(a_ref[...], act), wget(w_ref))

    return pl.pallas_call(
        body, grid=(T // tm, N // tn, nk),
        in_specs=[pl.BlockSpec((tm, tk), lambda i, j, k: (i, k)),
                  pl.BlockSpec((tk, tn), lambda i, j, k: (k, j)) if wspec is None else wspec,
                  pl.BlockSpec((tm, tn), lambda i, j, k: (i, j))],
        out_specs=pl.BlockSpec((tm, tn), lambda i, j, k: (i, j)),
        out_shape=jax.ShapeDtypeStruct((T, N), F32),
        compiler_params=_cp("parallel", "parallel", "arbitrary"), name=name)(a, w, r)


def glu_fwd(ys, w, *, tm, name, comm=None):
    T, N = ys.shape

    def body(y_ref, w_ref, o_ref, t_ref):
        a = _gelu(y_ref[...].astype(F32))
        t = _dot(a.astype(BF16), w_ref[...])
        o_ref[...] = (a * _sigmoid(t)).astype(BF16)
        t_ref[...] = t.astype(BF16)

    return _call(
        body, grid=(T // tm,),
        in_specs=[pl.BlockSpec((tm, N), lambda i: (i, 0)), pl.BlockSpec((N, N), lambda i: (0, 0))],
        out_specs=[pl.BlockSpec((tm, N), lambda i: (i, 0)), pl.BlockSpec((tm, N), lambda i: (i, 0))],
        out_shape=[jax.ShapeDtypeStruct((T, 2 * N), BF16), jax.ShapeDtypeStruct((T, N), BF16)],
        sem=("parallel",), name=name, args=(ys, w), comm=comm)


def glu_bwd(dymix, ys, t, w, *, tm, name, comm=None):
    T, N = ys.shape

    def body(d_ref, y_ref, t_ref, w_ref, dys_ref):
        d = d_ref[...].astype(F32)
        ysv = y_ref[...].astype(F32)
        s = _sigmoid(t_ref[...].astype(F32))
        dt = (d * _gelu(ysv) * s * (1.0 - s)).astype(BF16)
        dys_ref[...] = ((d * s + _dot_nt(dt, w_ref[...])) * _gelu_grad(ysv)).astype(BF16)

    spec = pl.BlockSpec((tm, N), lambda i: (i, 0))
    return _call(body, grid=(T // tm,), in_specs=[spec, spec, spec, pl.BlockSpec((N, N), lambda i: (0, 0))],
                 out_specs=[spec], out_shape=[jax.ShapeDtypeStruct((T, N), BF16)], sem=("parallel",), name=name,
                 args=(dymix, ys, t, w), comm=comm)


def glu_dw(dymix, ys, t, pack, *, pack_spec, tt, name):
    T, N = ys.shape

    def body(d_ref, y_ref, t_ref, pack_ref, o_ref):
        i = pl.program_id(0)
        a = _gelu(y_ref[...].astype(F32))
        s = _sigmoid(t_ref[...].astype(F32))
        dt = (d_ref[...].astype(F32) * a * s * (1.0 - s)).astype(BF16)
        part = _dot_tn(a.astype(BF16), dt).reshape(o_ref.shape)

        @pl.when(i == 0)
        def _():
            o_ref[...] = part

        @pl.when(i > 0)
        def _():
            o_ref[...] += part

    spec = pl.BlockSpec((tt, N), lambda i: (i, 0))
    return pl.pallas_call(body, grid=(T // tt,), in_specs=[spec, spec, spec, _ANY], out_specs=pack_spec,
                          out_shape=jax.ShapeDtypeStruct(pack.shape, F32), input_output_aliases={3: 0},
                          compiler_params=_cp("arbitrary"), name=name)(dymix, ys, t, pack)


def matmul_nt(g, w, *, epi=None, epi_args=(), g2=None, w2=None, tm, tko, tn, out_dtype=BF16, name, wspec=None,
              k_out=None, comm=None, w_is_nk=False):
    T, N = g.shape
    K = k_out if wspec is not None else (w.shape[1] if w_is_nk else w.shape[0])
    wget = (lambda r_: r_[...]) if wspec is None else _pack_block
    mm = _dot if w_is_nk else _dot_nt
    nn = N // tn
    has2 = g2 is not None
    rms = epi == "rmsbwd"
    if rms:
        assert tko == K
    n_epi = len(epi_args)

    def body(*refs):
        g_ref, w_ref = refs[0], refs[1]
        pos = 2
        if has2:
            g2_ref, w2_ref = refs[2], refs[3]
            pos = 4
        e_refs = refs[pos:pos + n_epi]
        pos += n_epi
        o_ref = refs[pos]
        pos += 1
        if rms:
            dg_ref = refs[pos]
            pos += 1
        acc_ref = refs[pos]
        i = pl.program_id(0)
        n = pl.program_id(2)
        part = mm(g_ref[...].astype(BF16), wget(w_ref))

        @pl.when(n == 0)
        def _():
            acc_ref[...] = part

        @pl.when(n > 0)
        def _():
            acc_ref[...] += part

        @pl.when(n == nn - 1)
        def _():
            acc = acc_ref[...]
            if has2:
                acc = acc + mm(g2_ref[...].astype(BF16), w2_ref[...])
            if epi is None:
                o_ref[...] = acc.astype(o_ref.dtype)
            elif epi == "relu2bwd":
                h1 = e_refs[0][...].astype(F32)
                o_ref[...] = (acc * 2.0 * jnp.maximum(h1, 0.0)).astype(o_ref.dtype)
            elif epi == "glubwd":
                da1 = e_refs[0][...].astype(F32)
                ys = e_refs[1][...].astype(F32)
                o_ref[...] = ((da1 + acc) * _gelu_grad(ys)).astype(o_ref.dtype)
            elif epi == "rmsbwd":
                xv, rs, gain, rv = e_refs[0][...], e_refs[1][...], e_refs[2][...], e_refs[3][...]
                xhat = xv * rs
                gd = acc * gain
                o_ref[...] = rv + rs * (gd - xhat * jnp.mean(gd * xhat, axis=-1, keepdims=True))
                part_g = jnp.sum(acc * xhat, axis=0, keepdims=True)

                @pl.when(i == 0)
                def _():
                    dg_ref[...] = part_g

                @pl.when(i > 0)
                def _():
                    dg_ref[...] += part_g

    if wspec is None:
        wspec = (pl.BlockSpec((tn, tko), lambda i, k, n: (n, k)) if w_is_nk
                 else pl.BlockSpec((tko, tn), lambda i, k, n: (k, n)))
    in_specs = [pl.BlockSpec((tm, tn), lambda i, k, n: (i, n)), wspec]
    args = [g, w]
    if has2:
        n2 = g2.shape[1]
        in_specs += [pl.BlockSpec((tm, n2), lambda i, k, n: (i, 0)),
                     pl.BlockSpec((n2, tko), lambda i, k, n: (0, k)) if w_is_nk
                     else pl.BlockSpec((tko, n2), lambda i, k, n: (k, 0))]
        args += [g2, w2]
    if epi == "relu2bwd" or epi == "glubwd":
        in_specs += [pl.BlockSpec((tm, tko), lambda i, k, n: (i, k))] * n_epi
    elif rms:
        in_specs += [pl.BlockSpec((tm, K), lambda i, k, n: (i, 0)), pl.BlockSpec((tm, 1), lambda i, k, n: (i, 0)),
                     pl.BlockSpec((1, K), lambda i, k, n: (0, 0)), pl.BlockSpec((tm, K), lambda i, k, n: (i, 0))]
    args += list(epi_args)
    out_shape = [jax.ShapeDtypeStruct((T, K), F32 if rms else out_dtype)]
    out_specs = [pl.BlockSpec((tm, tko), lambda i, k, n: (i, k))]
    if rms:
        out_shape.append(jax.ShapeDtypeStruct((1, K), F32))
        out_specs.append(pl.BlockSpec((1, K), lambda i, k, n: (0, 0)))
    sem = ("arbitrary",) * 3 if rms else ("parallel", "parallel", "arbitrary")
    res, cres = _call(body, grid=(T // tm, K // tko, nn), in_specs=in_specs, out_specs=out_specs, out_shape=out_shape,
                      scratch_shapes=[pltpu.VMEM((tm, tko), F32)], sem=sem, name=name, args=args, comm=comm)
    res = res if rms else res[0]
    return res if comm is None else (res, cres)


def matmul_tn(a, g, *, act=None, tk, tn, tt, name, pack=None, pack_spec=None, pack_shape=None):
    T, K = a.shape
    N = g.shape[1]
    to_pack = pack_spec is not None

    def body(a_ref, g_ref, *rest):
        o_ref = rest[-1]
        t = pl.program_id(2)
        part = _dot_tn(_act(a_ref[...], act), g_ref[...].astype(BF16))
        part = part.reshape(o_ref.shape)

        @pl.when(t == 0)
        def _():
            o_ref[...] = part

        @pl.when(t > 0)
        def _():
            o_ref[...] += part

    in_specs = [pl.BlockSpec((tt, tk), lambda k, n, t: (t, k)), pl.BlockSpec((tt, tn), lambda k, n, t: (t, n))]
    args = [a, g]
    aliases = {}
    if pack is not None:
        in_specs.append(_ANY)
        args.append(pack)
        aliases = {2: 0}
    return pl.pallas_call(
        body, grid=(K // tk, N // tn, T // tt), in_specs=in_specs,
        out_specs=pack_spec if to_pack else pl.BlockSpec((tk, tn), lambda k, n, t: (k, n)),
        out_shape=jax.ShapeDtypeStruct(pack_shape if to_pack else (K, N), F32), input_output_aliases=aliases,
        compiler_params=_cp("parallel", "parallel", "arbitrary"), name=name)(*args)


def attn_fwd(q, kv, *, tm, name):
    T = q.shape[0]
    M = kv.shape[0]
    scale = XA_HEAD_DIM ** -0.5

    def body(q_ref, kv_ref, o_ref):
        for h in range(XA_HEADS):
            sl = slice(h * XA_HEAD_DIM, (h + 1) * XA_HEAD_DIM)
            kh = kv_ref[:, h * XA_HEAD_DIM:(h + 1) * XA_HEAD_DIM]
            vh = kv_ref[:, D_MODEL + h * XA_HEAD_DIM:D_MODEL + (h + 1) * XA_HEAD_DIM]
            s = _dot_nt(q_ref[:, sl], kh) * scale
            s = s - jnp.max(s, axis=-1, keepdims=True)
            p = jnp.exp(s)
            p = p / jnp.sum(p, axis=-1, keepdims=True)
            o_ref[:, sl] = _dot(p.astype(BF16), vh).astype(BF16)

    return pl.pallas_call(
        body, grid=(T // tm,),
        in_specs=[pl.BlockSpec((tm, D_MODEL), lambda i: (i, 0)), pl.BlockSpec((M, 2 * D_MODEL), lambda i: (0, 0))],
        out_specs=pl.BlockSpec((tm, D_MODEL), lambda i: (i, 0)),
        out_shape=jax.ShapeDtypeStruct((T, D_MODEL), BF16),
        compiler_params=_cp("parallel"), name=name)(q, kv)


def attn_bwd(q, kv, do, *, tm, name):
    T = q.shape[0]
    M = kv.shape[0]
    scale = XA_HEAD_DIM ** -0.5

    def body(q_ref, kv_ref, do_ref, dq_ref, dkv_ref):
        i = pl.program_id(0)

        @pl.when(i == 0)
        def _():
            dkv_ref[...] = jnp.zeros_like(dkv_ref)

        for h in range(XA_HEADS):
            sl = slice(h * XA_HEAD_DIM, (h + 1) * XA_HEAD_DIM)
            slv = slice(D_MODEL + h * XA_HEAD_DIM, D_MODEL + (h + 1) * XA_HEAD_DIM)
            qh = q_ref[:, sl]
            kh = kv_ref[:, sl]
            vh = kv_ref[:, slv]
            doh = do_ref[:, sl]
            s = _dot_nt(qh, kh) * scale
            s = s - jnp.max(s, axis=-1, keepdims=True)
            p = jnp.exp(s)
            p = p / jnp.sum(p, axis=-1, keepdims=True)
            pb = p.astype(BF16)
            dkv_ref[:, slv] += _dot_tn(pb, doh)
            dp = _dot_nt(doh, vh)
            ds = (p * (dp - jnp.sum(dp * p, axis=-1, keepdims=True)) * scale).astype(BF16)
            dq_ref[:, sl] = _dot(ds, kh).astype(BF16)
            dkv_ref[:, sl] += _dot_tn(ds, qh)

    spec = pl.BlockSpec((tm, D_MODEL), lambda i: (i, 0))
    kvspec = pl.BlockSpec((M, 2 * D_MODEL), lambda i: (0, 0))
    return pl.pallas_call(
        body, grid=(T // tm,), in_specs=[spec, kvspec, spec], out_specs=[spec, kvspec],
        out_shape=[jax.ShapeDtypeStruct((T, D_MODEL), BF16), jax.ShapeDtypeStruct((M, 2 * D_MODEL), F32)],
        compiler_params=_cp("arbitrary"), name=name)(q, kv, do)


def loss_head(x, g, target, *, tm, name):
    T, D = x.shape

    def body(x_ref, g_ref, t_ref, l_ref, dx_ref, dg_ref):
        i = pl.program_id(0)
        xv = x_ref[...]
        gain = g_ref[...]
        r = lax.rsqrt(jnp.mean(xv * xv, axis=-1, keepdims=True) + EPS)
        xhat = xv * r
        err = xhat * gain - t_ref[...]
        part_l = jnp.full((1, 128), 0.5 / D, F32) * jnp.sum(err * err)
        dy = err * (1.0 / D)
        gd = dy * gain
        dx_ref[...] = r * (gd - xhat * jnp.mean(gd * xhat, axis=-1, keepdims=True))
        part_g = jnp.sum(dy * xhat, axis=0, keepdims=True)

        @pl.when(i == 0)
        def _():
            l_ref[...] = part_l
            dg_ref[...] = part_g

        @pl.when(i > 0)
        def _():
            l_ref[...] += part_l
            dg_ref[...] += part_g

    spec = pl.BlockSpec((tm, D), lambda i: (i, 0))
    return pl.pallas_call(
        body, grid=(T // tm,), in_specs=[spec, pl.BlockSpec((1, D), lambda i: (0, 0)), spec],
        out_specs=[pl.BlockSpec((1, 128), lambda i: (0, 0)), spec, pl.BlockSpec((1, D), lambda i: (0, 0))],
        out_shape=[jax.ShapeDtypeStruct((1, 128), F32), jax.ShapeDtypeStruct((T, D), F32),
                   jax.ShapeDtypeStruct((1, D), F32)],
        compiler_params=_cp("arbitrary"), name=name)(x, g, target)


S5_LS = 128
S5_SEG = 8
S5_TB = S5_LS * S5_SEG
S5_CH = 8
S5_CW = 128
S5_NST = 512
S5_UNROLL = 4


def _cmul(ar, ai, br, bi):
    return ar * br - ai * bi, ar * bi + ai * br


def s5_params_fwd(a_re, a_im, log_dt, b_re, b_im, expand, *, name):
    G, P = a_re.shape

    def body(ar_ref, ai_ref, ld_ref, br_ref, bi_ref, e_ref, abr_ref, abi_ref, apr_ref, api_ref, bbr_ref, bbi_ref):
        ar, ai = ar_ref[...], ai_ref[...]
        dt = jnp.exp(ld_ref[...])
        mag = jnp.exp(dt * ar)
        abr, abi = mag * jnp.cos(dt * ai), mag * jnp.sin(dt * ai)
        den = ar * ar + ai * ai
        zr, zi = abr - 1.0, abi
        fr = (zr * ar + zi * ai) / den
        fi = (zi * ar - zr * ai) / den
        frx, fix = _dot_hi(fr, e_ref[...]), _dot_hi(fi, e_ref[...])
        br, bi = br_ref[...], bi_ref[...]
        bbr_ref[...] = frx * br - fix * bi
        bbi_ref[...] = frx * bi + fix * br
        abr_ref[...] = abr
        abi_ref[...] = abi
        pr, pi = abr, abi
        for _ in range(int(math.log2(S5_LS))):
            pr, pi = _cmul(pr, pi, pr, pi)
        apr_ref[...] = pr
        api_ref[...] = pi

    small = jax.ShapeDtypeStruct((G, P), F32)
    big = jax.ShapeDtypeStruct(b_re.shape, F32)
    return pl.pallas_call(body, out_shape=[small, small, small, small, big, big], name=name)(
        a_re, a_im, log_dt, b_re, b_im, expand)


def s5_params_bwd(a_re, a_im, log_dt, b_re, b_im, g_abr, g_abi, g_bbr, g_bbi, expand, expand_t, *, name):
    G, P = a_re.shape

    def body(ar_ref, ai_ref, ld_ref, br_ref, bi_ref, gar_ref, gai_ref, gbr_ref, gbi_ref, e_ref, et_ref,
             dar_ref, dai_ref, dld_ref, dbr_ref, dbi_ref):
        ar, ai = ar_ref[...], ai_ref[...]
        dt = jnp.exp(ld_ref[...])
        mag = jnp.exp(dt * ar)
        cs, sn = jnp.cos(dt * ai), jnp.sin(dt * ai)
        abr, abi = mag * cs, mag * sn
        den = ar * ar + ai * ai
        zr, zi = abr - 1.0, abi
        fr = (zr * ar + zi * ai) / den
        fi = (zi * ar - zr * ai) / den
        frx, fix = _dot_hi(fr, e_ref[...]), _dot_hi(fi, e_ref[...])
        br, bi = br_ref[...], bi_ref[...]
        gbr, gbi = gbr_ref[...], gbi_ref[...]
        dbr_ref[...] = frx * gbr + fix * gbi
        dbi_ref[...] = -fix * gbr + frx * gbi
        gfr = _dot_hi(br * gbr + bi * gbi, et_ref[...])
        gfi = _dot_hi(-bi * gbr + br * gbi, et_ref[...])
        g_zr = (gfr * ar - gfi * ai) / den
        g_zi = (gfr * ai + gfi * ar) / den
        g_ar = gfr * (zr - fr * 2.0 * ar) / den + gfi * (zi - fi * 2.0 * ar) / den
        g_ai = gfr * (zi - fr * 2.0 * ai) / den + gfi * (-zr - fi * 2.0 * ai) / den
        t_abr = gar_ref[...] + g_zr
        t_abi = gai_ref[...] + g_zi
        g_mag = t_abr * cs + t_abi * sn
        g_th = mag * (-t_abr * sn + t_abi * cs)
        dar_ref[...] = g_ar + g_mag * mag * dt
        dai_ref[...] = g_ai + g_th * dt
        g_dt = jnp.sum(g_mag * mag * ar + g_th * ai, axis=-1, keepdims=True)
        dld_ref[...] = g_dt * dt

    small = jax.ShapeDtypeStruct((G, P), F32)
    big = jax.ShapeDtypeStruct(b_re.shape, F32)
    return pl.pallas_call(body, out_shape=[small, small, jax.ShapeDtypeStruct((G, 1), F32), big, big], name=name)(
        a_re, a_im, log_dt, b_re, b_im, g_abr, g_abi, g_bbr, g_bbi, expand, expand_t)


def _s5_permute_in(src_ref, dst_ref):
    for i in range(S5_LS):
        dst_ref[pl.ds(8 * i, 8), :] = src_ref[pl.ds(i, 8, stride=S5_LS), :]


def _s5_permute_out(src_ref, dst_ref):
    for r in range(S5_SEG):
        for k in range(S5_LS // 8):
            dst_ref[pl.ds(r * S5_LS + 8 * k, 8), :] = src_ref[pl.ds(64 * k + r, 8, stride=8), :]


def _s5_scan(a_r, a_i, dr_ref, di_ref, init_r, init_i, store=None, reverse=False, conj=False):
    sgn = -1.0 if conj else 1.0

    def steps(n, c):
        sr, si = c
        for u in range(S5_UNROLL):
            k = n * S5_UNROLL + u
            i = S5_LS - 1 - k if reverse else k
            nr = a_r * sr - sgn * a_i * si + dr_ref[i]
            ni = a_r * si + sgn * a_i * sr + di_ref[i]
            if store is not None:
                store(i, nr, ni, sr, si)
            sr, si = nr, ni
        return sr, si

    return lax.fori_loop(0, S5_LS // S5_UNROLL, steps, (init_r, init_i))


def _s5_stitch(apr, api, fin_r, fin_i, car_r, car_i, reverse=False, conj=False):
    sgn = -1.0 if conj else 1.0
    rows_r, rows_i = [None] * S5_SEG, [None] * S5_SEG
    order = range(S5_SEG - 1, -1, -1) if reverse else range(S5_SEG)
    for r in order:
        rows_r[r], rows_i[r] = car_r, car_i
        fr, fi = fin_r[r:r + 1], fin_i[r:r + 1]
        car_r, car_i = (apr * car_r - sgn * api * car_i + fr, apr * car_i + sgn * api * car_r + fi)
    return jnp.concatenate(rows_r, 0), jnp.concatenate(rows_i, 0), car_r, car_i


def _s5_specs(nb, rev):
    blk = (lambda c, b: (nb - 1 - b, c)) if rev else (lambda c, b: (b, c))
    tok = pl.BlockSpec((S5_TB, S5_CW), blk)
    par_b = pl.BlockSpec((1, S5_CW, S5_NST), lambda c, b: (c, 0, 0))
    par_c = pl.BlockSpec((1, S5_NST, S5_CW), lambda c, b: (c, 0, 0))
    vec_s = pl.BlockSpec((1, 1, S5_NST), lambda c, b: (c, 0, 0))
    vec_c = pl.BlockSpec((1, 1, S5_CW), lambda c, b: (c, 0, 0))
    return tok, par_b, par_c, vec_s, vec_c


def s5_fwd(proj, bbr, bbi, ccr, cci, abr, abi, apr, api, dskip, *, name, comm=None):
    T = proj.shape[0]
    nb = T // S5_TB
    zeros8 = functools.partial(jnp.zeros, (S5_SEG, S5_NST), F32)

    def body(u_ref, bbr_ref, bbi_ref, ccr_ref, cci_ref, ar_ref, ai_ref, apr_ref, api_ref, d_ref,
             y_ref, csr_ref, csi_ref, uf_ref, up_ref, dr_ref, di_ref, sr_ref, si_ref, yp_ref, car_ref, cai_ref):
        b = pl.program_id(1)

        @pl.when(b == 0)
        def _():
            car_ref[...] = jnp.zeros_like(car_ref)
            cai_ref[...] = jnp.zeros_like(cai_ref)

        csr_ref[0, 0] = car_ref[...]
        csi_ref[0, 0] = cai_ref[...]
        uf_ref[...] = u_ref[...].astype(F32)
        _s5_permute_in(uf_ref, up_ref)
        upb = up_ref[...].astype(BF16)
        dr_ref[...] = _dot(upb, bbr_ref[0]).reshape(S5_LS, S5_SEG, S5_NST)
        di_ref[...] = _dot(upb, bbi_ref[0]).reshape(S5_LS, S5_SEG, S5_NST)
        a_r = jnp.broadcast_to(ar_ref[0], (S5_SEG, S5_NST))
        a_i = jnp.broadcast_to(ai_ref[0], (S5_SEG, S5_NST))
        fin_r, fin_i = _s5_scan(a_r, a_i, dr_ref, di_ref, zeros8(), zeros8())
        cin_r, cin_i, ncr, nci = _s5_stitch(apr_ref[0], api_ref[0], fin_r, fin_i, car_ref[...], cai_ref[...])
        car_ref[...] = ncr
        cai_ref[...] = nci

        def store(i, nr, ni, sr, si):
            sr_ref[i] = nr
            si_ref[i] = ni

        _s5_scan(a_r, a_i, dr_ref, di_ref, cin_r, cin_i, store=store)
        s_r = sr_ref[...].reshape(S5_TB, S5_NST).astype(BF16)
        s_i = si_ref[...].reshape(S5_TB, S5_NST).astype(BF16)
        yp_ref[...] = _dot(s_r, ccr_ref[0]) - _dot(s_i, cci_ref[0]) + d_ref[0] * up_ref[...]
        _s5_permute_out(yp_ref, uf_ref)
        y_ref[...] = uf_ref[...].astype(BF16)

    tok, par_b, par_c, vec_s, vec_c = _s5_specs(nb, False)
    cs_spec = pl.BlockSpec((1, 1, 1, S5_NST), lambda c, b: (b, c, 0, 0))
    cs_shape = jax.ShapeDtypeStruct((nb, S5_CH, 1, S5_NST), F32)
    tokbuf = pltpu.VMEM((S5_TB, S5_CW), F32)
    stbuf = pltpu.VMEM((S5_LS, S5_SEG, S5_NST), F32)
    return _call(
        body, grid=(S5_CH, nb),
        in_specs=[tok, par_b, par_b, par_c, par_c, vec_s, vec_s, vec_s, vec_s, vec_c],
        out_specs=[tok, cs_spec, cs_spec],
        out_shape=[jax.ShapeDtypeStruct((T, D_S5), BF16), cs_shape, cs_shape],
        scratch_shapes=[tokbuf, tokbuf, stbuf, stbuf, stbuf, stbuf, tokbuf,
                        pltpu.VMEM((1, S5_NST), F32), pltpu.VMEM((1, S5_NST), F32)],
        sem=("parallel", "arbitrary"), name=name, comm=comm,
        args=(proj, bbr, bbi, ccr, cci, abr, abi, apr, api, dskip))


def s5_bwd(proj, dys, csr, csi, bbr, bbi, ccr, cci, abr, abi, apr, api, dskip, dproj, *, name, comm=None):
    T = proj.shape[0]
    nb = T // S5_TB
    zeros8 = functools.partial(jnp.zeros, (S5_SEG, S5_NST), F32)

    def body(u_ref, gy_ref, csr_ref, csi_ref, bbr_ref, bbi_ref, ccr_ref, cci_ref, ar_ref, ai_ref, apr_ref, api_ref,
             d_ref, dproj_ref, du_ref, dbr_ref, dbi_ref, dcr_ref, dci_ref, dd_ref, dar_ref, dai_ref,
             tmp_ref, up_ref, gyp_ref, dr_ref, di_ref, sr_ref, si_ref, gr_ref, gi_ref, car_ref, cai_ref):
        b = pl.program_id(1)

        @pl.when(b == 0)
        def _():
            car_ref[...] = jnp.zeros_like(car_ref)
            cai_ref[...] = jnp.zeros_like(cai_ref)
            for ref in (dbr_ref, dbi_ref, dcr_ref, dci_ref, dd_ref, dar_ref, dai_ref):
                ref[...] = jnp.zeros_like(ref)

        tmp_ref[...] = u_ref[...].astype(F32)
        _s5_permute_in(tmp_ref, up_ref)
        tmp_ref[...] = gy_ref[...].astype(F32)
        _s5_permute_in(tmp_ref, gyp_ref)
        upb = up_ref[...].astype(BF16)
        gyp = gyp_ref[...]
        gypb = gyp.astype(BF16)
        dr_ref[...] = _dot(upb, bbr_ref[0]).reshape(S5_LS, S5_SEG, S5_NST)
        di_ref[...] = _dot(upb, bbi_ref[0]).reshape(S5_LS, S5_SEG, S5_NST)
        a_r = jnp.broadcast_to(ar_ref[0], (S5_SEG, S5_NST))
        a_i = jnp.broadcast_to(ai_ref[0], (S5_SEG, S5_NST))
        fin_r, fin_i = _s5_scan(a_r, a_i, dr_ref, di_ref, zeros8(), zeros8())
        cin_r, cin_i, _, _ = _s5_stitch(apr_ref[0], api_ref[0], fin_r, fin_i, csr_ref[0, 0], csi_ref[0, 0])
        sr_ref[0] = cin_r
        si_ref[0] = cin_i

        def store_s(i, nr, ni, sr, si):
            sr_ref[i + 1] = nr
            si_ref[i + 1] = ni

        _s5_scan(a_r, a_i, dr_ref, di_ref, cin_r, cin_i, store=store_s)
        s_r = sr_ref[pl.ds(1, S5_LS)].reshape(S5_TB, S5_NST).astype(BF16)
        s_i = si_ref[pl.ds(1, S5_LS)].reshape(S5_TB, S5_NST).astype(BF16)
        dcr_ref[0] += _dot_tn(s_r, gypb)
        dci_ref[0] -= _dot_tn(s_i, gypb)
        dd_ref[0] += jnp.sum(gyp * up_ref[...], axis=0, keepdims=True)
        dr_ref[...] = _dot_nt(gypb, ccr_ref[0]).reshape(S5_LS, S5_SEG, S5_NST)
        di_ref[...] = (-_dot_nt(gypb, cci_ref[0])).reshape(S5_LS, S5_SEG, S5_NST)
        fin_r, fin_i = _s5_scan(a_r, a_i, dr_ref, di_ref, zeros8(), zeros8(), reverse=True, conj=True)
        gin_r, gin_i, ncr, nci = _s5_stitch(apr_ref[0], api_ref[0], fin_r, fin_i, car_ref[...], cai_ref[...],
                                            reverse=True, conj=True)
        car_ref[...] = ncr
        cai_ref[...] = nci
        def steps_g(n, carry):
            gr, gi, acc_r, acc_i = carry
            for u in range(S5_UNROLL):
                i = S5_LS - 1 - (n * S5_UNROLL + u)
                nr = a_r * gr + a_i * gi + dr_ref[i]
                ni = a_r * gi - a_i * gr + di_ref[i]
                gr_ref[i] = nr
                gi_ref[i] = ni
                pr, pi = sr_ref[i], si_ref[i]
                acc_r, acc_i = acc_r + (nr * pr + ni * pi), acc_i + (ni * pr - nr * pi)
                gr, gi = nr, ni
            return gr, gi, acc_r, acc_i

        _, _, acc_r, acc_i = lax.fori_loop(0, S5_LS // S5_UNROLL, steps_g, (gin_r, gin_i, zeros8(), zeros8()))
        dar_ref[0] += jnp.sum(acc_r, axis=0, keepdims=True)
        dai_ref[0] += jnp.sum(acc_i, axis=0, keepdims=True)
        g_r = gr_ref[...].reshape(S5_TB, S5_NST).astype(BF16)
        g_i = gi_ref[...].reshape(S5_TB, S5_NST).astype(BF16)
        dbr_ref[0] += _dot_tn(upb, g_r)
        dbi_ref[0] += _dot_tn(upb, g_i)
        gyp_ref[...] = _dot_nt(g_r, bbr_ref[0]) + _dot_nt(g_i, bbi_ref[0]) + d_ref[0] * gyp
        _s5_permute_out(gyp_ref, tmp_ref)
        du_ref[...] = tmp_ref[...].astype(BF16)

    tok, par_b, par_c, vec_s, vec_c = _s5_specs(nb, True)
    cs_spec = pl.BlockSpec((1, 1, 1, S5_NST), lambda c, b: (nb - 1 - b, c, 0, 0))
    tokbuf = pltpu.VMEM((S5_TB, S5_CW), F32)
    stbuf = pltpu.VMEM((S5_LS, S5_SEG, S5_NST), F32)
    stbuf1 = pltpu.VMEM((S5_LS + 1, S5_SEG, S5_NST), F32)
    return _call(
        body, grid=(S5_CH, nb), comm=comm,
        in_specs=[tok, tok, cs_spec, cs_spec, par_b, par_b, par_c, par_c, vec_s, vec_s, vec_s, vec_s, vec_c, _ANY],
        out_specs=[tok, par_b, par_b, par_c, par_c, vec_c, vec_s, vec_s], aliases={13: 0},
        out_shape=[jax.ShapeDtypeStruct(dproj.shape, BF16),
                   jax.ShapeDtypeStruct((S5_CH, S5_CW, S5_NST), F32), jax.ShapeDtypeStruct((S5_CH, S5_CW, S5_NST), F32),
                   jax.ShapeDtypeStruct((S5_CH, S5_NST, S5_CW), F32), jax.ShapeDtypeStruct((S5_CH, S5_NST, S5_CW), F32),
                   jax.ShapeDtypeStruct((S5_CH, 1, S5_CW), F32),
                   jax.ShapeDtypeStruct((S5_CH, 1, S5_NST), F32), jax.ShapeDtypeStruct((S5_CH, 1, S5_NST), F32)],
        scratch_shapes=[tokbuf, tokbuf, tokbuf, stbuf, stbuf, stbuf1, stbuf1, stbuf, stbuf,
                        pltpu.VMEM((1, S5_NST), F32), pltpu.VMEM((1, S5_NST), F32)],
        sem=("parallel", "arbitrary"), name=name,
        args=(proj, dys, csr, csi, bbr, bbi, ccr, cci, abr, abi, apr, api, dskip, dproj))


SSD_L = SSD_CHUNK
SSD_GW = 256
SSD_TAIL = 16
NEG = -1e30


def _expand16(v):
    lane = lax.broadcasted_iota(jnp.int32, (v.shape[0], 128), 1)
    parts = [jnp.where(lane < SSD_HEADDIM, v[:, 2 * j:2 * j + 1], v[:, 2 * j + 1:2 * j + 2]) for j in range(8)]
    return jnp.concatenate(parts, axis=1)


def _headsum(v, hsum):
    hi = v.astype(BF16)
    lo = (v - hi.astype(F32)).astype(BF16)
    return _dot(hi, hsum) + _dot(lo, hsum)


def _softplus(x):
    return jnp.maximum(x, 0.0) + jnp.log(1.0 + jnp.exp(-jnp.abs(x)))


def _ssd_chunk_fwd(z, xbc, tail, dtraw, hprev, cw, cb, dtb, alog, dsk, nw, tril, shift):
    L = SSD_L
    f = {}
    xe = jnp.concatenate([tail, xbc], axis=0).astype(BF16)
    shifted = _dot(shift, xe)
    sh = [xbc] + [shifted[(s - 1) * L:s * L] for s in (1, 2, 3)]
    conv = cb + cw[3:4] * sh[0] + cw[2:3] * sh[1] + cw[1:2] * sh[2] + cw[0:1] * sh[3]
    sig = _sigmoid(conv)
    xa = conv * sig
    xs, bm, cm = xa[:, :D_SSD], xa[:, D_SSD:D_SSD + 512], xa[:, D_SSD + 512:]
    pre = dtraw + dtb
    dt = _softplus(pre)
    a_h = -jnp.exp(alog)
    acum = _dot_hi(tril, dt * a_h)
    acum_t = acum.T
    alast = acum[L - 1:L]
    exp_a = jnp.exp(acum)
    dec = jnp.exp(alast - acum)
    exp_al = jnp.exp(alast)
    dt_x, dec_x, exp_a_x, exp_al_x = _expand16(dt), _expand16(dec), _expand16(exp_a), _expand16(exp_al)
    d_x = _expand16(dsk)
    xh = xs * dt_x
    xhb = xh.astype(BF16)
    xd = (xh * dec_x).astype(BF16)
    row = lax.broadcasted_iota(jnp.int32, (L, L), 0)
    col = lax.broadcasted_iota(jnp.int32, (L, L), 1)
    causal = row >= col
    lane = lax.broadcasted_iota(jnp.int32, (L, 128), 1)
    low = lane < SSD_HEADDIM
    hb = hprev.astype(BF16)
    y_pairs, yoff_parts, st_parts, cbs, lms = [], [], [], [], []
    for g in range(SSD_NGROUPS):
        bg = bm[:, g * 128:(g + 1) * 128].astype(BF16)
        cg = cm[:, g * 128:(g + 1) * 128].astype(BF16)
        cbg = _dot_nt(cg, bg)
        cbs.append(cbg)
        for j in (2 * g, 2 * g + 1):
            xp = xhb[:, j * 128:(j + 1) * 128]
            ys = []
            for h in (2 * j, 2 * j + 1):
                lm = jnp.exp(jnp.where(causal, acum[:, h:h + 1] - acum_t[h:h + 1, :], NEG))
                lms.append(lm)
                ys.append(_dot((cbg * lm).astype(BF16), xp))
            y_pairs.append(jnp.where(low, ys[0], ys[1]))
        gs = slice(g * SSD_GW, (g + 1) * SSD_GW)
        yoff_parts.append(_dot(cg, hb[:, gs]) * exp_a_x[:, gs])
        st_parts.append(_dot_tn(bg, xd[:, gs]))
    yoff = jnp.concatenate(yoff_parts, axis=1)
    y = jnp.concatenate(y_pairs, axis=1) + yoff + d_x * xs
    hnew = exp_al_x * hprev + jnp.concatenate(st_parts, axis=1)
    sz = _sigmoid(z)
    gz = y * (z * sz)
    r = lax.rsqrt(jnp.mean(gz * gz, axis=-1, keepdims=True) + EPS)
    out = gz * r * nw
    f.update(sh=sh, conv=conv, sig=sig, xs=xs, bm=bm, cm=cm, pre=pre, dt=dt, a_h=a_h, exp_a_x=exp_a_x, dec_x=dec_x,
             exp_al=exp_al, exp_al_x=exp_al_x, dt_x=dt_x, d_x=d_x, xh=xh, xhb=xhb, xd=xd, causal=causal, low=low, hb=hb,
             cbs=cbs, lms=lms, yoff=yoff, y=y, sz=sz, gz=gz, r=r)
    return out, hnew, f


def _ssd_params(conv_w, conv_b, dt_bias, a_log, d_skip, norm_w):
    pad16 = lambda v: jnp.pad(v.reshape(1, SSD_HEADS), ((0, 0), (0, 128 - SSD_HEADS)))
    return (jnp.pad(conv_w, ((0, 8 - SSD_CONV), (0, 0))), conv_b.reshape(1, D_XBC), pad16(dt_bias), pad16(a_log),
            pad16(d_skip), norm_w.reshape(1, D_SSD))


def _ssd_param_specs():
    full = lambda shape: pl.BlockSpec(shape, lambda i: (0, 0))
    return [full((8, D_XBC)), full((1, D_XBC)), full((1, 128)), full((1, 128)), full((1, 128)), full((1, D_SSD))]


def ssd_fwd(proj, dtraw, params, tril, shift, ymix, *, name, comm=None):
    T = proj.shape[0]
    nc = T // SSD_L

    def body(z_ref, x_ref, dt_ref, cw_ref, cb_ref, dtb_ref, al_ref, dsk_ref, nw_ref, tril_ref, shift_ref, ymix_ref,
             o_ref, hs_ref, h_ref, tail_ref):
        i = pl.program_id(0)

        @pl.when(i == 0)
        def _():
            h_ref[...] = jnp.zeros_like(h_ref)
            tail_ref[...] = jnp.zeros_like(tail_ref)

        xbc = x_ref[...].astype(F32)
        hprev = h_ref[...]
        hs_ref[0] = hprev
        out, hnew, _ = _ssd_chunk_fwd(z_ref[...].astype(F32), xbc, tail_ref[...], dt_ref[...], hprev, cw_ref[...],
                                      cb_ref[...], dtb_ref[...], al_ref[...], dsk_ref[...], nw_ref[...], tril_ref[...],
                                      shift_ref[...])
        o_ref[...] = out.astype(BF16)
        h_ref[...] = hnew
        tail_ref[...] = xbc[SSD_L - SSD_TAIL:]

    return _call(
        body, grid=(nc,),
        in_specs=[pl.BlockSpec((SSD_L, D_SSD), lambda i: (i, 1)), pl.BlockSpec((SSD_L, D_XBC), lambda i: (i, 1)),
                  pl.BlockSpec((SSD_L, 128), lambda i: (i, 0))] + _ssd_param_specs()
                 + [pl.BlockSpec((SSD_L, SSD_L), lambda i: (0, 0)), pl.BlockSpec(shift.shape, lambda i: (0, 0)), _ANY],
        out_specs=[pl.BlockSpec((SSD_L, D_SSD), lambda i: (i, 1)),
                   pl.BlockSpec((1, SSD_STATE, D_SSD), lambda i: (i, 0, 0))],
        out_shape=[jax.ShapeDtypeStruct(ymix.shape, BF16), jax.ShapeDtypeStruct((nc, SSD_STATE, D_SSD), F32)],
        scratch_shapes=[pltpu.VMEM((SSD_STATE, D_SSD), F32), pltpu.VMEM((SSD_TAIL, D_XBC), F32)],
        sem=("arbitrary",), name=name, args=(proj, proj, dtraw, *params, tril, shift, ymix), aliases={11: 0}, comm=comm)


def ssd_bwd(proj, dtraw, hs, dymix, params, tril, triu, trils, headsum, shift, *, name, comm=None):
    T = proj.shape[0]
    nc = T // SSD_L
    L = SSD_L

    def body(z_ref, x_ref, xprev_ref, dt_ref, hs_ref, do_ref, cw_ref, cb_ref, dtb_ref, al_ref, dsk_ref, nw_ref,
             tril_ref, triu_ref, trils_ref, hsum_ref, shift_ref,
             dp_ref, ddt_ref, dcw_ref, dcb_ref, ddtb_ref, dal_ref, ddsk_ref, dnw_ref, dh_ref, dnext_ref):
        i = pl.program_id(0)

        @pl.when(i == 0)
        def _():
            dh_ref[...] = jnp.zeros_like(dh_ref)
            dnext_ref[...] = jnp.zeros_like(dnext_ref)
            for ref in (dcw_ref, dcb_ref, ddtb_ref, dal_ref, ddsk_ref, dnw_ref):
                ref[...] = jnp.zeros_like(ref)

        z = z_ref[...].astype(F32)
        xbc = x_ref[...].astype(F32)
        tail = jnp.where(i == nc - 1, 0.0, xprev_ref[...].astype(F32))
        hprev = hs_ref[0]
        cw, nw = cw_ref[...], nw_ref[...]
        hsum = hsum_ref[...]
        _, _, f = _ssd_chunk_fwd(z, xbc, tail, dt_ref[...], hprev, cw, cb_ref[...], dtb_ref[...], al_ref[...],
                                 dsk_ref[...], nw, tril_ref[...], shift_ref[...])
        dout = do_ref[...].astype(F32)
        dh = dh_ref[...]
        ghat = f["gz"] * f["r"]
        dn = dout * nw
        dgz = f["r"] * (dn - ghat * jnp.mean(dn * ghat, axis=-1, keepdims=True))
        dnw_ref[...] += jnp.sum(dout * ghat, axis=0, keepdims=True)
        sz = f["sz"]
        dy = dgz * (z * sz)
        dp_ref[:, :D_S5] = jnp.zeros((L, D_S5), BF16)
        dp_ref[:, D_S5:D_S5 + D_SSD] = (dgz * f["y"] * sz * (1.0 + z * (1.0 - sz))).astype(BF16)
        xs = f["xs"]
        ddsk_ref[...] += jnp.sum(_headsum(dy * xs, hsum), axis=0, keepdims=True)
        dyb = dy.astype(BF16)
        dye = (dy * f["exp_a_x"]).astype(BF16)
        dhb = dh.astype(BF16)
        lane = lax.broadcasted_iota(jnp.int32, (L, 128), 1)
        sub = lax.broadcasted_iota(jnp.int32, (128, L), 0)
        zero_b = jnp.zeros((L, 128), BF16)
        rsum = jnp.zeros((L, 128), F32)
        csum_t = jnp.zeros((128, L), F32)
        dx_pairs, dxst_parts, db_parts, dc_parts, dhp_parts = [], [], [], [], []
        for g in range(SSD_NGROUPS):
            gs = slice(g * SSD_GW, (g + 1) * SSD_GW)
            bg = f["bm"][:, g * 128:(g + 1) * 128].astype(BF16)
            cg = f["cm"][:, g * 128:(g + 1) * 128].astype(BF16)
            cbg = f["cbs"][g]
            dcb_g = jnp.zeros((L, L), F32)
            for j in (2 * g, 2 * g + 1):
                xp = f["xhb"][:, j * 128:(j + 1) * 128]
                dyp = dyb[:, j * 128:(j + 1) * 128]
                dxs = []
                for half, h in enumerate((2 * j, 2 * j + 1)):
                    lm = f["lms"][h]
                    dyh = jnp.where(f["low"], dyp, zero_b) if half == 0 else jnp.where(f["low"], zero_b, dyp)
                    dw = jnp.where(f["causal"], _dot_nt(dyh, xp), 0.0)
                    w = cbg * lm
                    e = dw * w
                    dcb_g = dcb_g + dw * lm
                    rsum = jnp.where(lane == h, jnp.sum(e, axis=1, keepdims=True), rsum)
                    csum_t = jnp.where(sub == h, jnp.sum(e, axis=0, keepdims=True), csum_t)
                    dxs.append(_dot_tn(w.astype(BF16), dyp))
                dx_pairs.append(jnp.where(f["low"], dxs[0], dxs[1]))
            dcbb = dcb_g.astype(BF16)
            dxst_parts.append(f["dec_x"][:, gs] * _dot(bg, dhb[:, gs]))
            dc_parts.append(_dot(dcbb, bg) + _dot_nt(dye[:, gs], f["hb"][:, gs]))
            db_parts.append(_dot_tn(dcbb, cg) + _dot_nt(f["xd"][:, gs], dhb[:, gs]))
            dhp_parts.append(f["exp_al_x"][:, gs] * dh[:, gs] + _dot_tn(cg, dye[:, gs]))
        dxst = jnp.concatenate(dxst_parts, axis=1)
        dxh = jnp.concatenate(dx_pairs, axis=1) + dxst
        q = _headsum(f["yoff"] * dy, hsum)
        dstate = _headsum(f["xh"] * dxst, hsum)
        h0t = jnp.sum(_headsum(dh * hprev, hsum), axis=0, keepdims=True) * f["exp_al"]
        da = _dot_hi(triu_ref[...], rsum - csum_t.T + q) + _dot_hi(trils_ref[...], dstate) + h0t
        dt, a_h = f["dt"], f["a_h"]
        ddt = _headsum(dxh * xs, hsum) + da * a_h
        dal_ref[...] += jnp.sum(da * dt, axis=0, keepdims=True) * a_h
        ddtraw = ddt * _sigmoid(f["pre"])
        first16 = lane < SSD_HEADS
        ddtraw = jnp.where(first16, ddtraw, 0.0)
        ddt_ref[...] = ddtraw
        ddtb_ref[...] += jnp.sum(ddtraw, axis=0, keepdims=True)
        dh_ref[...] = jnp.concatenate(dhp_parts, axis=1)
        dxa = jnp.concatenate([dxh * f["dt_x"] + f["d_x"] * dy] + db_parts + dc_parts, axis=1)
        sig, conv = f["sig"], f["conv"]
        dconv = dxa * sig * (1.0 + conv * (1.0 - sig))
        dcb_ref[...] += jnp.sum(dconv, axis=0, keepdims=True)
        rows = [jnp.sum(dconv * f["sh"][3 - k], axis=0, keepdims=True) for k in range(SSD_CONV)]
        dcw_ref[...] += jnp.concatenate(rows + [jnp.zeros((8 - SSD_CONV, D_XBC), F32)], axis=0)
        de = jnp.concatenate([dconv, dnext_ref[...]], axis=0)
        dxbc = cw[3:4] * dconv
        for s in (1, 2, 3):
            dxbc = dxbc + cw[3 - s:4 - s] * pltpu.roll(de, L + 8 - s, 0)[:L]
        dp_ref[:, D_S5 + D_SSD:] = dxbc.astype(BF16)
        dnext_ref[...] = dconv[:8]

    rev = lambda i: nc - 1 - i
    acc = lambda shape: pl.BlockSpec(shape, lambda i: (0, 0))
    tri = pl.BlockSpec((L, L), lambda i: (0, 0))
    return _call(
        body, grid=(nc,),
        in_specs=[pl.BlockSpec((L, D_SSD), lambda i: (rev(i), 1)), pl.BlockSpec((L, D_XBC), lambda i: (rev(i), 1)),
                  pl.BlockSpec((SSD_TAIL, D_XBC), lambda i: (jnp.maximum(rev(i) * (L // SSD_TAIL) - 1, 0), 1)),
                  pl.BlockSpec((L, 128), lambda i: (rev(i), 0)),
                  pl.BlockSpec((1, SSD_STATE, D_SSD), lambda i: (rev(i), 0, 0)),
                  pl.BlockSpec((L, D_SSD), lambda i: (rev(i), 1))] + _ssd_param_specs()
                 + [tri, tri, tri, pl.BlockSpec((D_SSD, 128), lambda i: (0, 0)), pl.BlockSpec(shift.shape, lambda i: (0, 0))],
        out_specs=[pl.BlockSpec((L, D_MAIN), lambda i: (rev(i), 0)), pl.BlockSpec((L, 128), lambda i: (rev(i), 0)),
                   acc((8, D_XBC)), acc((1, D_XBC)), acc((1, 128)), acc((1, 128)), acc((1, 128)), acc((1, D_SSD))],
        out_shape=[jax.ShapeDtypeStruct((T, D_MAIN), BF16),
                   jax.ShapeDtypeStruct((T, 128), F32), jax.ShapeDtypeStruct((8, D_XBC), F32),
                   jax.ShapeDtypeStruct((1, D_XBC), F32), jax.ShapeDtypeStruct((1, 128), F32),
                   jax.ShapeDtypeStruct((1, 128), F32), jax.ShapeDtypeStruct((1, 128), F32),
                   jax.ShapeDtypeStruct((1, D_SSD), F32)],
        scratch_shapes=[pltpu.VMEM((SSD_STATE, D_SSD), F32), pltpu.VMEM((8, D_XBC), F32)],
        sem=("arbitrary",), name=name, comm=comm,
        args=(proj, proj, proj, dtraw, hs, dymix, *params, tril, triu, trils, headsum, shift))


def _s5_blockdiag(v, rows_per_group, cols_per_group):
    eye = jnp.eye(S5_SEG, dtype=v.dtype)
    w = v[:, :, :, None, :] * eye[None, :, None, :, None]
    return w.reshape(S5_CH, 8 * rows_per_group, 8 * cols_per_group)


def _s5_blockdiag_extract(w, rows_per_group, cols_per_group):
    eye = jnp.eye(S5_SEG, dtype=w.dtype)
    w5 = w.reshape(S5_CH, 8, rows_per_group, 8, cols_per_group)
    return jnp.sum(w5 * eye[None, :, None, :, None], axis=3)


TM = 512

OFF = dict(mlp_w2=0, mlp_w1=1024, w_out=2048, s5_w_glu=2560, xa_wq=2816, xa_wk=3072, xa_wv=3328, xa_wo=3584, w_in=3840)
ROWS = dict(mlp_w2=1024, mlp_w1=1024, w_out=512, s5_w_glu=256, xa_wq=256, xa_wk=256, xa_wv=256, xa_wo=256, w_in=1028)
TAIL_OFF = 4872
TAIL_ROWS = 120
PACK_ROWS = TAIL_OFF + TAIL_ROWS
ROWS_ALL = (0, PACK_ROWS, 832, True)
ROWS_EARLY = (0, OFF["w_in"], 640, False)
ROWS_LATE = (OFF["w_in"], PACK_ROWS - OFF["w_in"], 192, True)
N_SHARD = 4
SMALL_L = ("norm_mix", "s5_a_re", "s5_a_im", "s5_log_dt", "s5_b_re", "s5_b_im", "s5_c_re", "s5_c_im", "s5_d",
           "ssd_conv_w", "ssd_conv_b", "ssd_dt_bias", "ssd_a_log", "ssd_d", "ssd_norm", "norm_xattn", "norm_mem",
           "norm_mlp")
SMALL_Q = 72
CONV_ROWS = 8


def _place():
    x, y, c = lax.axis_index("x"), lax.axis_index("y"), lax.axis_index("c")
    chips = [(1 - x, y), (x, 1 - y), (1 - x, 1 - y)]
    return x, y, c, 2 * x + y, chips, (x, y, 1 - c)


def _remote(src, dst, send_sem, recv_sem, to):
    return pltpu.make_async_remote_copy(src_ref=src, dst_ref=dst, send_sem=send_sem, recv_sem=recv_sem,
                                        device_id=to, device_id_type=MESH_T)


def _dma_sems(*counts):
    return [pltpu.SemaphoreType.DMA((n,)) for n in counts]


def place_own(wpack, shard_idx):
    tile = PACK_ROWS // 4

    def body(s_ref, w_ref, o_ref):
        o_ref[0] = w_ref[...]

    return pl.pallas_call(
        body,
        grid_spec=pltpu.PrefetchScalarGridSpec(
            num_scalar_prefetch=1, grid=(4,),
            in_specs=[pl.BlockSpec((tile, D_MODEL), lambda i, s_ref: (i, 0))],
            out_specs=pl.BlockSpec((1, tile, D_MODEL), lambda i, s_ref: (s_ref[0], i, 0))),
        out_shape=jax.ShapeDtypeStruct((N_SHARD,) + wpack.shape, wpack.dtype),
        compiler_params=_cp("parallel"), name="place_own")(shard_idx, wpack)


def _range_half(ref, rows, c):
    half = rows[1] // 2
    return pl.ds(pl.multiple_of(rows[0] + c * half, 16), half)


def gather_over_ici(wpack, placed, rows=ROWS_ALL):
    def copies(ci, co, cs):
        w_ref, (out_ref,), (send, recv) = ci[0], co, cs
        x, y, c, s, chips, sibling = _place()
        mine = _range_half(w_ref, rows, c)
        sends = [_remote(w_ref.at[mine, :], out_ref.at[s, mine, :], send.at[j], recv.at[j], (*chip, c))
                 for j, chip in enumerate(chips)]
        lands = [out_ref.at[2 * chip[0] + chip[1], mine, :] for chip in chips]
        arrivals = [_remote(land, land, send.at[j], recv.at[j], sibling) for j, land in enumerate(lands)]
        return sends, arrivals

    def start(ci, co, cs):
        for cp in copies(ci, co, cs)[0]:
            cp.start()

    def wait(ci, co, cs):
        sends, arrivals = copies(ci, co, cs)
        for cp in arrivals:
            cp.wait_recv()
        for cp in sends:
            cp.wait_send()

    return Comm("gather_ici", [wpack, placed], [jax.ShapeDtypeStruct(placed.shape, placed.dtype)], _dma_sems(3, 3),
                start, wait, aliases={1: 0})


def gather_to_sibling(landed, rows=ROWS_ALL):
    def copies(co, cs):
        (out_ref,), (send, recv) = co, cs
        x, y, c, s, chips, sibling = _place()
        slots = [2 * chip[0] + chip[1] for chip in chips]
        mine, other = _range_half(out_ref, rows, c), _range_half(out_ref, rows, 1 - c)
        sends = [_remote(out_ref.at[t, mine, :], out_ref.at[t, mine, :], send.at[j], recv.at[j], sibling)
                 for j, t in enumerate(slots)]
        arrivals = [_remote(out_ref.at[t, other, :], out_ref.at[t, other, :], send.at[j], recv.at[j], sibling)
                    for j, t in enumerate(slots)]
        return sends, arrivals

    def start(ci, co, cs):
        for cp in copies(co, cs)[0]:
            cp.start()

    def wait(ci, co, cs):
        sends, arrivals = copies(co, cs)
        for cp in arrivals:
            cp.wait_recv()
        for cp in sends:
            cp.wait_send()

    return Comm("gather_d2d", [landed], [jax.ShapeDtypeStruct(landed.shape, landed.dtype)], _dma_sems(3, 3), start, wait,
                aliases={0: 0})


def exchange_halves(gpack, rows):
    def copy(ci, co, cs):
        (g_ref,), (out_ref,), (send, recv) = ci, co, cs
        x, y, c, s, chips, sibling = _place()
        return _remote(g_ref.at[:, _range_half(g_ref, rows, 1 - c), :], out_ref, send.at[0], recv.at[0], sibling)

    return Comm("exchange", [gpack], [jax.ShapeDtypeStruct((N_SHARD, rows[1] // 2, D_MODEL), F32)], _dma_sems(1, 1),
                lambda ci, co, cs: copy(ci, co, cs).start(), lambda ci, co, cs: copy(ci, co, cs).wait())


def scatter_chips(csum):
    def copies(ci, co, cs):
        (c_ref,), (out_ref,), (send, recv) = ci, co, cs
        x, y, c, s, chips, sibling = _place()
        sends = [_remote(c_ref.at[2 * chip[0] + chip[1]], out_ref.at[s], send.at[j], recv.at[j], (*chip, c))
                 for j, chip in enumerate(chips)]
        arrivals = [_remote(c_ref.at[2 * chip[0] + chip[1]], out_ref.at[2 * chip[0] + chip[1]], send.at[j], recv.at[j],
                            (*chip, c)) for j, chip in enumerate(chips)]
        return sends, arrivals

    def start(ci, co, cs):
        for cp in copies(ci, co, cs)[0]:
            cp.start()

    def wait(ci, co, cs):
        sends, arrivals = copies(ci, co, cs)
        for cp in arrivals:
            cp.wait_recv()
        for cp in sends:
            cp.wait_send()

    return Comm("scatter", [csum], [jax.ShapeDtypeStruct(csum.shape, csum.dtype)], _dma_sems(3, 3), start, wait)


def share_reduced(gshards, smalls, l, rows):
    with_tail = rows[3]

    def copies(ci, co, cs):
        (g_ref, sm_ref), (send, recv, loc) = co, cs
        x, y, c, s, chips, sibling = _place()
        my_half = g_ref.at[l, _range_half(g_ref, rows, c), :]
        tail = g_ref.at[l, pl.ds(PACK_ROWS - TAIL_ROWS, TAIL_ROWS), :]
        big = _remote(my_half, my_half, send.at[0], recv.at[0], sibling)
        keep_tail = pltpu.make_async_copy(tail, sm_ref.at[l, s], loc.at[0])
        tails = [_remote(tail, sm_ref.at[l, s], send.at[1 + j], recv.at[1 + j], (*chip, 1)) for j, chip in enumerate(chips)]
        tails += [_remote(tail, sm_ref.at[l, s], send.at[4 + j], recv.at[1 + j], (*chip, 0)) for j, chip in enumerate(chips)]
        tails.append(_remote(tail, sm_ref.at[l, s], send.at[7], recv.at[4], sibling))
        other = g_ref.at[l, _range_half(g_ref, rows, 1 - c), :]
        big_in = _remote(other, other, send.at[0], recv.at[0], sibling)
        slots = [sm_ref.at[l, 2 * chip[0] + chip[1]] for chip in chips]
        tails_in = [_remote(slot, slot, send.at[1 + j], recv.at[1 + j], sibling) for j, slot in enumerate(slots)]
        sib_tail_in = _remote(sm_ref.at[l, s], sm_ref.at[l, s], send.at[7], recv.at[4], sibling)
        return c, big, keep_tail, tails, big_in, tails_in, sib_tail_in

    def start(ci, co, cs):
        c, big, keep_tail, tails, _, _, _ = copies(ci, co, cs)
        big.start()
        if with_tail:
            @pl.when(c == 1)
            def _():
                keep_tail.start()
                for cp in tails:
                    cp.start()

    def wait(ci, co, cs):
        c, big, keep_tail, tails, big_in, tails_in, sib_tail_in = copies(ci, co, cs)
        big_in.wait_recv()
        big.wait_send()
        if with_tail:
            for cp in tails_in:
                cp.wait_recv()

            @pl.when(c == 0)
            def _():
                sib_tail_in.wait_recv()

            @pl.when(c == 1)
            def _():
                for cp in tails:
                    cp.wait_send()
                keep_tail.wait()

    sds = lambda a: jax.ShapeDtypeStruct(a.shape, a.dtype)
    return Comm("share", [gshards, smalls], [sds(gshards), sds(smalls)], _dma_sems(8, 5, 1), start, wait,
                aliases={0: 0, 1: 1})


def _consts():
    e = np.zeros((S5_STATE, S5_STATE * S5_GROUP), np.float32)
    for p in range(S5_STATE):
        e[p, p * S5_GROUP:(p + 1) * S5_GROUP] = 1.0
    hs = np.zeros((D_SSD, 128), np.float32)
    for h in range(SSD_HEADS):
        hs[h * SSD_HEADDIM:(h + 1) * SSD_HEADDIM, h] = 1.0
    ones = np.ones((SSD_L, SSD_L), np.float32)
    shift = np.zeros((3 * SSD_L, SSD_TAIL + SSD_L), np.float32)
    for s in (1, 2, 3):
        for t in range(SSD_L):
            shift[(s - 1) * SSD_L + t, SSD_TAIL + t - s] = 1.0
    return dict(expand=jnp.asarray(e), expand_t=jnp.asarray(e.T), headsum=jnp.asarray(hs).astype(BF16),
                shift=jnp.asarray(shift).astype(BF16),
                tril=jnp.asarray(np.tril(ones)), triu=jnp.asarray(np.triu(ones)), trils=jnp.asarray(np.tril(ones, -1)))


def _s5_mats(w, cst):
    b_re = w["s5_b_re"].reshape(S5_GROUPS, S5_STATE * S5_GROUP)
    b_im = w["s5_b_im"].reshape(S5_GROUPS, S5_STATE * S5_GROUP)
    abr, abi, apr, api, bbr, bbi = s5_params_fwd(w["s5_a_re"], w["s5_a_im"], w["s5_log_dt"].reshape(S5_GROUPS, 1),
                                                 b_re, b_im, cst["expand"], name="s5_params_fwd")
    t = lambda v: v.reshape(S5_CH, 8, S5_STATE, S5_GROUP).transpose(0, 1, 3, 2)
    c4 = lambda v: v.reshape(S5_CH, 8, S5_GROUP, S5_STATE).transpose(0, 1, 3, 2)
    vec = lambda v: v.reshape(S5_CH, 1, S5_NST)
    return dict(bbr=_s5_blockdiag(t(bbr), S5_GROUP, S5_STATE).astype(BF16),
                bbi=_s5_blockdiag(t(bbi), S5_GROUP, S5_STATE).astype(BF16),
                ccr=_s5_blockdiag(c4(w["s5_c_re"]), S5_STATE, S5_GROUP).astype(BF16),
                cci=_s5_blockdiag(c4(w["s5_c_im"]), S5_STATE, S5_GROUP).astype(BF16),
                abr=vec(abr), abi=vec(abi), apr=vec(apr), api=vec(api),
                dsk=w["s5_d"].reshape(S5_CH, 1, S5_CW), b_re=b_re, b_im=b_im)


def _layer_fwd(x, mem, w, cst, next_pack=None, rest=None):
    sv = {}
    g = lambda n: w[n].reshape(1, -1)
    res = norm_matmul(x, g("norm_mix"), w["w_in_t"], w["w_dt_t"], tm=TM, tn=1024, name="in_proj", w_transposed=True,
                      comm=gather_over_ici(rest[0], rest[1], ROWS_EARLY) if rest else None)
    (proj, h0, r0, dtraw), staged = res if rest else (res, None)
    s5m = _s5_mats(w, cst)
    comm = _combine(gather_to_sibling(staged[0], ROWS_EARLY) if rest else None,
                    gather_over_ici(*next_pack) if next_pack else None)
    (ys, csr, csi), landed = s5_fwd(proj, s5m["bbr"], s5m["bbi"], s5m["ccr"], s5m["cci"], s5m["abr"], s5m["abi"],
                                    s5m["apr"], s5m["api"], s5m["dsk"], name="s5_fwd", comm=comm)
    if rest:
        w = dict(w, **_pack_weights(landed[0]))
        landed = landed[1:]
    pack = w["pack"]
    (ymix, tglu), gathered = glu_fwd(ys, w["s5_w_glu"], tm=TM, name="glu_fwd",
                                     comm=None if next_pack is None else gather_to_sibling(landed[0]))
    ssdp = _ssd_params(w["ssd_conv_w"], w["ssd_conv_b"], w["ssd_dt_bias"], w["ssd_a_log"], w["ssd_d"], w["ssd_norm"])
    (ymix, hs), _ = ssd_fwd(proj, dtraw, ssdp, cst["tril"], cst["shift"], ymix, name="ssd_fwd")
    x1 = matmul_res(ymix, pack, x, tm=TM, tn=1024, tk=1024, name="out_proj",
                    wspec=pl.BlockSpec((2, 512, 1024), lambda i, j, k: (k, OFF["w_out"] // 512, j)))
    q, h1, r1 = norm_matmul(x1, g("norm_xattn"), w["xa_wq"], tm=TM, tn=1024, name="q_proj")
    kv, hm, rm = norm_matmul(mem, g("norm_mem"), w["xa_wkv"], tm=mem.shape[0], tn=1024, name="kv_proj")
    o = attn_fwd(q, kv, tm=TM, name="attn_fwd")
    x2 = matmul_res(o, w["xa_wo"], x1, tm=TM, tn=1024, tk=1024, name="attn_out")
    f1, h2, r2 = norm_matmul(x2, g("norm_mlp"), pack, tm=TM, tn=1024, name="mlp_up", n_out=D_FF,
                             wspec=pl.BlockSpec((1, D_MODEL, 1024), lambda i, j: (j, OFF["mlp_w1"] // 1024, 0)))
    x3 = matmul_res(f1, pack, x2, act="relu2", tm=TM, tn=1024, tk=1024, name="mlp_down",
                    wspec=pl.BlockSpec((1, 1024, 1024), lambda i, j, k: (k, OFF["mlp_w2"] // 1024, j)))
    sv.update(x=x, proj=proj, h0=h0, r0=r0, dtraw=dtraw, s5m=s5m, ys=ys, csr=csr, csi=csi, tglu=tglu, ssdp=ssdp,
              hs=hs, ymix=ymix, x1=x1, q=q, h1=h1, r1=r1, kv=kv, hm=hm, rm=rm, o=o, x2=x2, f1=f1, h2=h2, r2=r2)
    return x3, sv, (gathered[0] if next_pack is not None else None), w


def _layer_bwd(dx3, mem, w, sv, cst, extra_small=None, reduce_hooks=None, early=None):
    gr = {}
    g = lambda n: w[n].reshape(1, -1)
    pack = w["pack"]
    pshape = (N_SHARD, PACK_ROWS, D_MODEL)
    ps = lambda rows, f: pl.BlockSpec((1, rows, 1024), f)
    ps4 = lambda rows, f: pl.BlockSpec((N_SHARD, rows, 1024), f)
    dh1 = matmul_nt(dx3, pack, epi="relu2bwd", epi_args=(sv["f1"],), tm=TM, tko=1024, tn=1024, name="mlp_down_dx",
                    wspec=ps(1024, lambda i, k, n: (k, OFF["mlp_w2"] // 1024, n)), k_out=D_FF,
                    comm=reduce_hooks.exchange() if reduce_hooks else None)
    if reduce_hooks:
        dh1, recv = dh1
        reduce_hooks.after_exchange(recv[0])
    gp = matmul_tn(sv["f1"], dx3, act="relu2", tk=1024, tn=1024, tt=TM, name="mlp_down_dw", pack_shape=pshape,
                   pack_spec=ps(1024, lambda k, n, t: (k, OFF["mlp_w2"] // 1024, 0)))
    gp = matmul_tn(sv["h2"], dh1, tk=1024, tn=1024, tt=TM, name="mlp_up_dw", pack=gp, pack_shape=pshape,
                   pack_spec=ps(1024, lambda k, n, t: (n, OFF["mlp_w1"] // 1024, 0)))
    dx2, gr["norm_mlp"] = matmul_nt(dh1, pack, epi="rmsbwd", epi_args=(sv["x2"], sv["r2"], g("norm_mlp"), dx3),
                                    tm=TM, tko=1024, tn=1024, name="mlp_up_dx", k_out=D_MODEL,
                                    wspec=ps(1024, lambda i, k, n: (n, OFF["mlp_w1"] // 1024, 0)))
    do = matmul_nt(dx2, w["xa_wo"], tm=TM, tko=1024, tn=1024, name="attn_out_dx")
    gp = matmul_tn(sv["o"], dx2, tk=1024, tn=1024, tt=TM, name="attn_out_dw", pack=gp, pack_shape=pshape,
                   pack_spec=ps4(256, lambda k, n, t: (0, OFF["xa_wo"] // 256, 0)))
    dq, dkv = attn_bwd(sv["q"], sv["kv"], do, tm=TM, name="attn_bwd")
    gp = matmul_tn(sv["h1"], dq, tk=1024, tn=1024, tt=TM, name="q_proj_dw", pack=gp, pack_shape=pshape,
                   pack_spec=ps4(256, lambda k, n, t: (0, OFF["xa_wq"] // 256, 0)))
    dx1, gr["norm_xattn"] = matmul_nt(dq, w["xa_wq"], epi="rmsbwd", epi_args=(sv["x1"], sv["r1"], g("norm_xattn"), dx2),
                                      tm=TM, tko=1024, tn=1024, name="q_proj_dx")
    M = mem.shape[0]
    gp = matmul_tn(sv["hm"], dkv, tk=1024, tn=1024, tt=M, name="kv_proj_dw", pack=gp, pack_shape=pshape,
                   pack_spec=ps4(256, lambda k, n, t: (0, OFF["xa_wk"] // 256 + n, 0)))
    _, gr["norm_mem"] = matmul_nt(dkv, w["xa_wkv"], epi="rmsbwd",
                                  epi_args=(mem, sv["rm"], g("norm_mem"), jnp.zeros_like(mem)),
                                  tm=M, tko=1024, tn=1024, name="kv_proj_dx")
    dymix = matmul_nt(dx1, pack, tm=TM, tko=1024, tn=1024, name="out_proj_dx", k_out=2 * D_MODEL,
                      wspec=pl.BlockSpec((2, 512, 1024), lambda i, k, n: (k, OFF["w_out"] // 512, n)))
    gp = matmul_tn(sv["ymix"], dx1, tk=2048, tn=1024, tt=TM, name="out_proj_dw", pack=gp, pack_shape=pshape,
                   pack_spec=ps4(512, lambda k, n, t: (0, OFF["w_out"] // 512, 0)))
    (dproj, ddtraw, dcw, dcb, ddtb, dal, ddsk, dnw), parts = ssd_bwd(
        sv["proj"], sv["dtraw"], sv["hs"], dymix, sv["ssdp"], cst["tril"], cst["triu"], cst["trils"], cst["headsum"],
        cst["shift"], name="ssd_bwd", comm=reduce_hooks.scatter() if reduce_hooks else None)
    if reduce_hooks:
        reduce_hooks.after_scatter(parts[0])
    gr["ssd_conv_w"], gr["ssd_conv_b"] = dcw[:SSD_CONV], dcb[0]
    gr["ssd_dt_bias"], gr["ssd_a_log"], gr["ssd_d"] = ddtb[0, :SSD_HEADS], dal[0, :SSD_HEADS], ddsk[0, :SSD_HEADS]
    gr["ssd_norm"] = dnw[0]
    gp = glu_dw(dymix, sv["ys"], sv["tglu"], gp, tt=TM, name="glu_dw",
                pack_spec=ps4(256, lambda i: (0, OFF["s5_w_glu"] // 256, 0)))
    first = _Reduction(gp, early[0], early[1], None, None, ROWS_EARLY) if early else None
    comm = _combine(reduce_hooks.share() if reduce_hooks else None, first.exchange() if first else None)
    (dys,), outs = glu_bwd(dymix, sv["ys"], sv["tglu"], w["s5_w_glu"], tm=TM, name="glu_dx", comm=comm)
    if comm is not None:
        if reduce_hooks:
            reduce_hooks.after_share(outs[:2])
            outs = outs[2:]
        if first:
            first.gshards, first.smalls = reduce_hooks.gshards, reduce_hooks.smalls
            first.after_exchange(outs[0])
    s5m = sv["s5m"]
    (dproj, dbbr, dbbi, dccr, dcci, dd, dabr, dabi), parts = s5_bwd(
        sv["proj"], dys, sv["csr"], sv["csi"], s5m["bbr"], s5m["bbi"], s5m["ccr"], s5m["cci"], s5m["abr"], s5m["abi"],
        s5m["apr"], s5m["api"], s5m["dsk"], dproj, name="s5_bwd", comm=first.scatter() if first else None)
    if first:
        first.after_scatter(parts[0])
    tb = lambda v: _s5_blockdiag_extract(v, S5_GROUP, S5_STATE).transpose(0, 1, 3, 2).reshape(S5_GROUPS, -1)
    tc = lambda v: _s5_blockdiag_extract(v, S5_STATE, S5_GROUP).transpose(0, 1, 3, 2).reshape(S5_GROUPS, S5_GROUP, S5_STATE)
    gr["s5_c_re"], gr["s5_c_im"] = tc(dccr), tc(dcci)
    gr["s5_d"] = dd.reshape(S5_GROUPS, S5_GROUP)
    dar, dai, dld, dbr, dbi = s5_params_bwd(
        w["s5_a_re"], w["s5_a_im"], w["s5_log_dt"].reshape(S5_GROUPS, 1), s5m["b_re"], s5m["b_im"],
        dabr.reshape(S5_GROUPS, S5_STATE), dabi.reshape(S5_GROUPS, S5_STATE), tb(dbbr), tb(dbbi),
        cst["expand"], cst["expand_t"], name="s5_params_bwd")
    gr["s5_a_re"], gr["s5_a_im"], gr["s5_log_dt"] = dar, dai, dld[:, 0]
    gr["s5_b_re"] = dbr.reshape(S5_GROUPS, S5_STATE, S5_GROUP)
    gr["s5_b_im"] = dbi.reshape(S5_GROUPS, S5_STATE, S5_GROUP)
    wt_shape = (D_MAIN + D_DT_PAD, D_MODEL)
    dwt = matmul_tn(dproj, sv["h0"], tk=1024, tn=1024, tt=TM, name="in_proj_dw", pack_shape=wt_shape,
                    pack_spec=pl.BlockSpec((1024, 1024), lambda k, n, t: (k, 0)))
    dwt = matmul_tn(ddtraw, sv["h0"], tk=D_DT_PAD, tn=1024, tt=TM, name="in_proj_dt_dw", pack=dwt, pack_shape=wt_shape,
                    pack_spec=pl.BlockSpec((D_DT_PAD, 1024), lambda k, n, t: (D_MAIN // D_DT_PAD, 0)))
    dx0 = matmul_nt(dproj, w["w_in_t"], g2=ddtraw, w2=w["w_dt_t"], w_is_nk=True, epi="rmsbwd",
                    epi_args=(sv["x"], sv["r0"], g("norm_mix"), dx1), tm=TM, tko=1024, tn=1024, name="in_proj_dx",
                    comm=first.share() if first else None)
    if first:
        dx0, shared = dx0
        first.after_share(shared)
    dx0, gr["norm_mix"] = dx0
    gr = {k: (v[0] if k.startswith("norm_") else v) for k, v in gr.items()}
    for t in range(N_SHARD):
        shard_rows = lax.slice_in_dim(dwt, t * ROWS["w_in"], (t + 1) * ROWS["w_in"], axis=0)
        gp = lax.dynamic_update_slice(gp, shard_rows[None], (t, OFF["w_in"], 0))
    small = [gr[n].reshape(-1) for n in SMALL_L] + ([] if extra_small is None else [extra_small.reshape(-1)])
    small = jnp.concatenate(small)
    small = jnp.pad(small, (0, N_SHARD * SMALL_Q * D_MODEL - small.size)).reshape(N_SHARD, SMALL_Q, D_MODEL)
    gap = TAIL_OFF - OFF["w_in"] - ROWS["w_in"]
    gp = lax.dynamic_update_slice(gp, jnp.pad(small, ((0, 0), (gap, TAIL_ROWS - SMALL_Q), (0, 0))),
                                  (0, TAIL_OFF - gap, 0))
    return dx0, gp, first


def _local_step(x, mem, target, layers, norm_final):
    cst = _consts()
    saved = []
    for l in range(DEPTH):
        x, sv, _, _ = _layer_fwd(x, mem, layers[l], cst)
        saved.append(sv)
    loss, dx, dgf = loss_head(x, norm_final.reshape(1, -1), target, tm=TM, name="loss_head")
    packs = [None] * DEPTH
    for l in reversed(range(DEPTH)):
        dx, packs[l], _ = _layer_bwd(dx, mem, layers[l], saved[l], cst, extra_small=dgf[0] if l == DEPTH - 1 else None)
    return loss, dx, packs


def sum_halves(gpack, recv, c_idx, rows):
    first, count, tile, _ = rows
    half = count // 2
    nb = half // tile

    def body(c_ref, a_ref, b_ref, o_ref):
        o_ref[...] = (a_ref[...] + b_ref[...]).astype(BF16)

    blk = (1, tile, D_MODEL)
    return pl.pallas_call(
        body,
        grid_spec=pltpu.PrefetchScalarGridSpec(
            num_scalar_prefetch=1, grid=(N_SHARD, nb),
            in_specs=[pl.BlockSpec(blk, lambda t, i, c_ref: (t, first // tile + c_ref[0] * nb + i, 0)),
                      pl.BlockSpec(blk, lambda t, i, c_ref: (t, i, 0))],
            out_specs=pl.BlockSpec(blk, lambda t, i, c_ref: (t, i, 0))),
        out_shape=jax.ShapeDtypeStruct((N_SHARD, half, D_MODEL), BF16),
        compiler_params=_cp("parallel", "parallel"), name="sum_halves")(c_idx, gpack, recv)


def sum_chips(parts, csum, gshards, l, place_idx, rows):
    first, count, tile, _ = rows
    nb = count // 2 // tile

    def body(pi_ref, p0, p1, p2, p3, own, g_ref, o_ref):
        s = pi_ref[0]
        vals = [jnp.where(s == k, own[0], p[0]).astype(F32) for k, p in enumerate((p0, p1, p2, p3))]
        o_ref[0] = ((vals[0] + vals[1]) + vals[2]) + vals[3]

    blk = (1, tile, D_MODEL)
    part_spec = lambda k: pl.BlockSpec(blk, lambda i, pi_ref: (jnp.where(pi_ref[0] == k, (k + 1) % N_SHARD, k), i, 0))
    return pl.pallas_call(
        body,
        grid_spec=pltpu.PrefetchScalarGridSpec(
            num_scalar_prefetch=1, grid=(nb,),
            in_specs=[part_spec(k) for k in range(N_SHARD)]
                     + [pl.BlockSpec(blk, lambda i, pi_ref: (pi_ref[0], i, 0)), _ANY],
            out_specs=pl.BlockSpec(blk, lambda i, pi_ref: (l, first // tile + pi_ref[1] * nb + i, 0))),
        out_shape=jax.ShapeDtypeStruct(gshards.shape, F32), input_output_aliases={6: 0},
        compiler_params=_cp("parallel"), name="sum_chips")(place_idx, parts, parts, parts, parts, csum, gshards)


class _Reduction:
    def __init__(self, gpack, layer, place_idx, gshards, smalls, rows=ROWS_ALL):
        self.gpack, self.layer, self.place_idx, self.gshards, self.smalls = gpack, layer, place_idx, gshards, smalls
        self.rows = rows

    def exchange(self):
        return exchange_halves(self.gpack, self.rows)

    def after_exchange(self, recv):
        self.csum = sum_halves(self.gpack, recv, self.place_idx[1:], self.rows)

    def scatter(self):
        return scatter_chips(self.csum)

    def after_scatter(self, parts):
        self.gshards = sum_chips(parts, self.csum, self.gshards, self.layer, self.place_idx, self.rows)

    def share(self):
        return share_reduced(self.gshards, self.smalls, self.layer, self.rows)

    def after_share(self, shared):
        self.gshards, self.smalls = shared

    def run_alone(self):
        self.after_exchange(_comm_only(self.exchange())[0])
        self.after_scatter(_comm_only(self.scatter())[0])
        self.after_share(_comm_only(self.share()))
        return self.gshards, self.smalls


def adamw(w, g, m, v, *, name):
    shape = w.shape
    cols = shape[-1]
    rows = w.size // cols
    tr = 512 if rows % 512 == 0 else rows
    c1 = 1.0 / (1.0 - ADAM_B1 ** ADAM_STEP)
    c2 = 1.0 / (1.0 - ADAM_B2 ** ADAM_STEP)

    def body(w_ref, g_ref, m_ref, v_ref, d_ref, nm_ref, nv_ref):
        gv = g_ref[...]
        nm = ADAM_B1 * m_ref[...] + (1.0 - ADAM_B1) * gv
        nv = ADAM_B2 * v_ref[...] + (1.0 - ADAM_B2) * (gv * gv)
        d_ref[...] = -ADAM_LR * ((nm * c1) / (jnp.sqrt(nv * c2) + ADAM_EPS) + ADAM_WD * w_ref[...])
        nm_ref[...] = nm
        nv_ref[...] = nv

    spec = pl.BlockSpec((tr, cols), lambda i: (i, 0))
    sds = jax.ShapeDtypeStruct((rows, cols), F32)
    outs = pl.pallas_call(body, grid=(rows // tr,), in_specs=[spec] * 4, out_specs=[spec] * 3, out_shape=[sds] * 3,
                          compiler_params=_cp("parallel"), name=name)(
                              *[a.reshape(rows, cols) for a in (w, g, m, v)])
    return [o.reshape(shape) for o in outs]


def _own_pack(wts, l):
    rows = [wts[n][l].T if n == "w_in" else wts[n][l].reshape(ROWS[n], D_MODEL) for n in sorted(OFF, key=OFF.get)]
    cw = wts["ssd_conv_w"][l].reshape(-1)
    hi = lax.reduce_precision(cw, 8, 7)
    mid = lax.reduce_precision(cw - hi, 8, 7)
    lo = lax.reduce_precision(cw - hi - mid, 8, 7)
    conv = jnp.pad(jnp.concatenate([hi, mid, lo]), (0, CONV_ROWS * D_MODEL - 3 * cw.size)).reshape(CONV_ROWS, D_MODEL)
    gap = jnp.zeros((TAIL_OFF - OFF["w_in"] - ROWS["w_in"], D_MODEL), F32)
    rest = jnp.zeros((TAIL_ROWS - CONV_ROWS, D_MODEL), F32)
    return jnp.concatenate(rows + [gap, conv, rest], axis=0).astype(BF16)


def _square(gathered, n):
    return gathered[:, OFF[n]:OFF[n] + ROWS[n]].reshape(N_SHARD * ROWS[n], D_MODEL)


def _pack_weights(gathered):
    w = {"pack": gathered}
    w["s5_w_glu"], w["xa_wq"], w["xa_wo"] = _square(gathered, "s5_w_glu"), _square(gathered, "xa_wq"), _square(gathered, "xa_wo")
    w["xa_wkv"] = jnp.concatenate([_square(gathered, "xa_wk"), _square(gathered, "xa_wv")], axis=1)
    return w


def _layer_weights(gathered, wts, l, late_only=False):
    w = {n: wts[n][l] for n in SMALL_L if n != "ssd_conv_w"}
    if not late_only:
        w.update(_pack_weights(gathered))
    w_in_t = _square(gathered, "w_in")
    w["w_in_t"] = w_in_t[:D_MAIN]
    w["w_dt_t"] = jnp.pad(w_in_t[D_MAIN:], ((0, D_DT_PAD - SSD_HEADS), (0, 0)))
    per = SSD_CONV * D_XBC // N_SHARD
    cw = gathered[:, TAIL_OFF:TAIL_OFF + CONV_ROWS].astype(F32).reshape(N_SHARD, -1)[:, :3 * per]
    cw = cw.reshape(N_SHARD, 3, SSD_CONV, D_XBC // N_SHARD)
    cw = (cw[:, 0] + cw[:, 1]) + cw[:, 2]
    w["ssd_conv_w"] = cw.transpose(1, 0, 2).reshape(SSD_CONV, D_XBC)
    return w


def kernel(x, mem, norm_mix, w_in, s5_a_re, s5_a_im, s5_log_dt, s5_b_re, s5_b_im, s5_c_re, s5_c_im, s5_d, s5_w_glu, ssd_conv_w, ssd_conv_b, ssd_dt_bias, ssd_a_log, ssd_d, ssd_norm, w_out, norm_xattn, norm_mem, xa_wq, xa_wk, xa_wv, xa_wo, norm_mlp, mlp_w1, mlp_w2, norm_final, loss_target, m_norm_mix, m_w_in, m_s5_a_re, m_s5_a_im, m_s5_log_dt, m_s5_b_re, m_s5_b_im, m_s5_c_re, m_s5_c_im, m_s5_d, m_s5_w_glu, m_ssd_conv_w, m_ssd_conv_b, m_ssd_dt_bias, m_ssd_a_log, m_ssd_d, m_ssd_norm, m_w_out, m_norm_xattn, m_norm_mem, m_xa_wq, m_xa_wk, m_xa_wv, m_xa_wo, m_norm_mlp, m_mlp_w1, m_mlp_w2, m_norm_final, v_norm_mix, v_w_in, v_s5_a_re, v_s5_a_im, v_s5_log_dt, v_s5_b_re, v_s5_b_im, v_s5_c_re, v_s5_c_im, v_s5_d, v_s5_w_glu, v_ssd_conv_w, v_ssd_conv_b, v_ssd_dt_bias, v_ssd_a_log, v_ssd_d, v_ssd_norm, v_w_out, v_norm_xattn, v_norm_mem, v_xa_wq, v_xa_wk, v_xa_wv, v_xa_wo, v_norm_mlp, v_mlp_w1, v_mlp_w2, v_norm_final):
    names = ("norm_mix", "w_in", "s5_a_re", "s5_a_im", "s5_log_dt", "s5_b_re", "s5_b_im", "s5_c_re", "s5_c_im", "s5_d",
             "s5_w_glu", "ssd_conv_w", "ssd_conv_b", "ssd_dt_bias", "ssd_a_log", "ssd_d", "ssd_norm", "w_out",
             "norm_xattn", "norm_mem", "xa_wq", "xa_wk", "xa_wv", "xa_wo", "norm_mlp", "mlp_w1", "mlp_w2", "norm_final")
    loc = locals()
    wts = {n: loc[n] for n in names}
    mom = {n: loc["m_" + n] for n in names}
    var = {n: loc["v_" + n] for n in names}
    shard = 2 * lax.axis_index("x") + lax.axis_index("y")
    core = lax.axis_index("c")

    cst = _consts()
    place_idx = jnp.stack([shard, core]).astype(jnp.int32)
    h, mem0 = x[0], mem[0]

    own0 = _own_pack(wts, 0)
    staged = _comm_only(gather_over_ici(own0, place_own(own0, place_idx[:1]), ROWS_LATE))[0]
    gathered = _comm_only(gather_to_sibling(staged, ROWS_LATE))[0]
    layers, saved = [], []
    for l in range(DEPTH):
        nxt = None
        if l + 1 < DEPTH:
            own = _own_pack(wts, l + 1)
            nxt = (own, place_own(own, place_idx[:1]))
        h, sv, gathered, w = _layer_fwd(h, mem0, _layer_weights(gathered, wts, l, late_only=l == 0), cst, next_pack=nxt,
                                        rest=(own0, gathered) if l == 0 else None)
        layers.append(w)
        saved.append(sv)
    loss, dx, dgf = loss_head(h, norm_final.reshape(1, -1), loss_target[0], tm=TM, name="loss_head")

    gshards = jnp.zeros((DEPTH, PACK_ROWS, D_MODEL), F32)
    smalls = jnp.zeros((DEPTH, N_SHARD, TAIL_ROWS, D_MODEL), F32)
    pending = None
    for l in reversed(range(DEPTH)):
        dx, gpack, first = _layer_bwd(dx, mem0, layers[l], saved[l], cst, extra_small=dgf[0] if l == DEPTH - 1 else None,
                                      reduce_hooks=pending, early=(l, place_idx) if l == 0 else None)
        if pending is not None:
            gshards, smalls = pending.gshards, pending.smalls
        pending = _Reduction(gpack, l, place_idx, gshards, smalls)
    gshards, smalls = _Reduction(gpack, 0, place_idx, first.gshards, first.smalls, ROWS_LATE).run_alone()

    g = {n: gshards[:, OFF[n]:OFF[n] + ROWS[n]].reshape(wts[n].shape) for n in OFF if n != "w_in"}
    g["w_in"] = gshards[:, OFF["w_in"]:OFF["w_in"] + ROWS["w_in"]].transpose(0, 2, 1)
    small_red = smalls[:, :, :SMALL_Q].reshape(DEPTH, -1)
    off = 0
    for n in SMALL_L:
        shape = (SSD_CONV, D_XBC) if n == "ssd_conv_w" else wts[n].shape[1:]
        size = math.prod(shape)
        g[n] = small_red[:, off:off + size].reshape((DEPTH,) + shape)
        off += size
    g["norm_final"] = small_red[DEPTH - 1, off:off + D_MODEL]
    g["ssd_conv_w"] = lax.dynamic_slice_in_dim(g["ssd_conv_w"], shard * (D_XBC // N_SHARD), D_XBC // N_SHARD, axis=2)

    deltas, new_m, new_v = {}, {}, {}
    for n in names:
        deltas[n], new_m[n], new_v[n] = adamw(wts[n], g[n], mom[n], var[n], name="adamw_" + n)
    loss_all = lax.psum(loss[0, 0], ("x", "y", "c"))
    return (loss_all, dx[None], *[g[n] for n in names], *[deltas[n] for n in names], *[new_m[n] for n in names],
            *[new_v[n] for n in names])
```

```python
import functools
import math

import jax
import jax.numpy as jnp
import numpy as np
from jax import lax
from jax.experimental import pallas as pl
from jax.experimental.pallas import tpu as pltpu

F32 = jnp.float32
BF16 = jnp.bfloat16
HIGHEST = lax.Precision.HIGHEST

D_MODEL = 1024
DEPTH = 4
D_S5 = 1024
D_SSD = 1024
S5_GROUP = 16
S5_GROUPS = 64
S5_STATE = 64
SSD_HEADS = 16
SSD_HEADDIM = 64
SSD_NGROUPS = 4
SSD_STATE = 128
SSD_CONV = 4
SSD_CHUNK = 128
D_XBC = 2048
D_MAIN = 4096
D_DT_PAD = 128
XA_HEADS = 4
XA_HEAD_DIM = 256
D_FF = 4096
EPS = 1e-5
ADAM_LR, ADAM_B1, ADAM_B2, ADAM_EPS, ADAM_WD, ADAM_STEP = 0.001, 0.9, 0.999, 1e-08, 0.01, 10

VMEM_LIMIT = 56 * 1024 * 1024
MESH_T = pl.DeviceIdType.MESH


def _cp(*sem):
    return pltpu.CompilerParams(dimension_semantics=tuple(sem) if sem else None, vmem_limit_bytes=VMEM_LIMIT)


_ANY = pl.BlockSpec(memory_space=pl.ANY)


class Comm:
    def __init__(self, name, inputs, out_shapes, sems, start, wait, aliases=None):
        self.name, self.inputs, self.out_shapes, self.sems = name, list(inputs), list(out_shapes), list(sems)
        self.start, self.wait, self.aliases = start, wait, dict(aliases or {})


def _call(body, *, grid, in_specs, out_specs, out_shape, args, scratch_shapes=(), sem, name, comm=None, aliases=None):
    in_specs, out_specs, out_shape = list(in_specs), list(out_specs), list(out_shape)
    scratch_shapes = list(scratch_shapes)
    aliases = dict(aliases or {})
    if comm is None:
        res = pl.pallas_call(body, grid=grid, in_specs=in_specs, out_specs=out_specs, out_shape=out_shape,
                             scratch_shapes=scratch_shapes, compiler_params=_cp(*sem), name=name,
                             input_output_aliases=aliases)(*args)
        return list(res), []
    n_in, n_out, n_scr = len(in_specs), len(out_specs), len(scratch_shapes)
    c_in, c_out = len(comm.inputs), len(comm.out_shapes)

    def wrapped(*refs):
        a, refs = refs[:n_in], refs[n_in:]
        ci, refs = refs[:c_in], refs[c_in:]
        o, refs = refs[:n_out], refs[n_out:]
        co, refs = refs[:c_out], refs[c_out:]
        s, cs = refs[:n_scr], refs[n_scr:]
        first = functools.reduce(jnp.logical_and, [pl.program_id(d) == 0 for d in range(len(grid))])
        last = functools.reduce(jnp.logical_and, [pl.program_id(d) == grid[d] - 1 for d in range(len(grid))])

        @pl.when(first)
        def _():
            comm.start(ci, co, cs)

        body(*a, *o, *s)

        @pl.when(last)
        def _():
            comm.wait(ci, co, cs)

    for i, j in comm.aliases.items():
        aliases[n_in + i] = n_out + j
    res = pl.pallas_call(wrapped, grid=grid, in_specs=in_specs + [_ANY] * c_in, out_specs=out_specs + [_ANY] * c_out,
                         out_shape=out_shape + comm.out_shapes, scratch_shapes=scratch_shapes + comm.sems,
                         compiler_params=_cp(*(("arbitrary",) * len(grid))), name=name + "_" + comm.name,
                         input_output_aliases=aliases)(*args, *comm.inputs)
    return list(res[:n_out]), list(res[n_out:])


def _combine(a, b):
    if a is None or b is None:
        return a or b
    ni, no, ns = len(a.inputs), len(a.out_shapes), len(a.sems)

    def start(ci, co, cs):
        a.start(ci[:ni], co[:no], cs[:ns])
        b.start(ci[ni:], co[no:], cs[ns:])

    def wait(ci, co, cs):
        a.wait(ci[:ni], co[:no], cs[:ns])
        b.wait(ci[ni:], co[no:], cs[ns:])

    aliases = dict(a.aliases)
    aliases.update({ni + i: no + j for i, j in b.aliases.items()})
    return Comm(a.name + "_" + b.name, a.inputs + b.inputs, a.out_shapes + b.out_shapes, a.sems + b.sems, start, wait,
                aliases)


def _comm_only(comm):
    def body(*refs):
        ci, refs = refs[:len(comm.inputs)], refs[len(comm.inputs):]
        co, cs = refs[:len(comm.out_shapes)], refs[len(comm.out_shapes):]
        comm.start(ci, co, cs)
        comm.wait(ci, co, cs)

    res = pl.pallas_call(body, in_specs=[_ANY] * len(comm.inputs), out_specs=[_ANY] * len(comm.out_shapes),
                         out_shape=comm.out_shapes, scratch_shapes=comm.sems, name=comm.name,
                         input_output_aliases=comm.aliases)(*comm.inputs)
    return list(res)


def _dot(a, b):
    return jnp.dot(a, b, preferred_element_type=F32)


def _dot_nt(a, b):
    return lax.dot_general(a, b, (((1,), (1,)), ((), ())), preferred_element_type=F32)


def _dot_tn(a, b):
    return lax.dot_general(a, b, (((0,), (0,)), ((), ())), preferred_element_type=F32)


def _dot_hi(a, b):
    return jnp.dot(a, b, precision=HIGHEST, preferred_element_type=F32)


def _gelu(x):
    c = math.sqrt(2.0 / math.pi)
    return 0.5 * x * (1.0 + jnp.tanh(c * (x + 0.044715 * x * x * x)))


def _gelu_grad(x):
    c = math.sqrt(2.0 / math.pi)
    t = jnp.tanh(c * (x + 0.044715 * x * x * x))
    return 0.5 * (1.0 + t) + 0.5 * x * (1.0 - t * t) * c * (1.0 + 3 * 0.044715 * x * x)


def _sigmoid(x):
    return 1.0 / (1.0 + jnp.exp(-x))


def _act(a, act):
    if act is None:
        return a.astype(BF16)
    a = a.astype(F32)
    if act == "relu2":
        a = jnp.maximum(a, 0.0)
        return (a * a).astype(BF16)
    if act == "gelu":
        return _gelu(a).astype(BF16)
    raise ValueError(act)


def _pack_block(ref):
    return ref[...].reshape(-1, ref.shape[-1])


def norm_matmul(x, g, w, w2=None, *, tm, tn, name, wspec=None, n_out=None, w_transposed=False, comm=None):
    T, D = x.shape
    N = n_out if wspec is not None else (w.shape[0] if w_transposed else w.shape[1])
    wget = (lambda r: r[...]) if wspec is None else _pack_block
    mm = _dot_nt if w_transposed else _dot
    has2 = w2 is not None

    def body(x_ref, g_ref, w_ref, *rest):
        if has2:
            w2_ref, o_ref, h_ref, r_ref, o2_ref = rest
        else:
            o_ref, h_ref, r_ref = rest
        j = pl.program_id(1)

        @pl.when(j == 0)
        def _():
            xv = x_ref[...]
            r = lax.rsqrt(jnp.mean(xv * xv, axis=-1, keepdims=True) + EPS)
            h = (xv * r * g_ref[...]).astype(BF16)
            h_ref[...] = h
            r_ref[...] = r
            if has2:
                o2_ref[...] = mm(h, w2_ref[...])

        o_ref[...] = mm(h_ref[...], wget(w_ref)).astype(o_ref.dtype)

    if wspec is None:
        wspec = pl.BlockSpec((tn, D), lambda i, j: (j, 0)) if w_transposed else pl.BlockSpec((D, tn), lambda i, j: (0, j))
    in_specs = [pl.BlockSpec((tm, D), lambda i, j: (i, 0)), pl.BlockSpec((1, D), lambda i, j: (0, 0)), wspec]
    out_shape = [jax.ShapeDtypeStruct((T, N), BF16), jax.ShapeDtypeStruct((T, D), BF16),
                 jax.ShapeDtypeStruct((T, 1), F32)]
    out_specs = [pl.BlockSpec((tm, tn), lambda i, j: (i, j)), pl.BlockSpec((tm, D), lambda i, j: (i, 0)),
                 pl.BlockSpec((tm, 1), lambda i, j: (i, 0))]
    args = [x, g, w]
    if has2:
        in_specs.append(pl.BlockSpec(w2.shape, lambda i, j: (0, 0)))
        out_shape.append(jax.ShapeDtypeStruct((T, D_DT_PAD), F32))
        out_specs.append(pl.BlockSpec((tm, D_DT_PAD), lambda i, j: (i, 0)))
        args.append(w2)
    res, cres = _call(body, grid=(T // tm, N // tn), in_specs=in_specs, out_specs=out_specs, out_shape=out_shape,
                      sem=("parallel", "arbitrary"), name=name, args=args, comm=comm)
    return res if comm is None else (res, cres)


def matmul_res(a, w, r, *, act=None, tm, tn, tk, name, wspec=None):
    T, K = a.shape
    N = r.shape[1]
    wget = (lambda r_: r_[...]) if wspec is None else _pack_block
    nk = K // tk

    def body(a_ref, w_ref, r_ref, o_ref):
        k = pl.program_id(2)

        @pl.when(k == 0)
        def _():
            o_ref[...] = r_ref[...]

        o_ref[...] += _dot(_act(a_ref[...], act), wget(w_ref))

    return pl.pallas_call(
        body, grid=(T // tm, N // tn, nk),
        in_specs=[pl.BlockSpec((tm, tk), lambda i, j, k: (i, k)),
                  pl.BlockSpec((tk, tn), lambda i, j, k: (k, j)) if wspec is None else wspec,
                  pl.BlockSpec((tm, tn), lambda i, j, k: (i, j))],
        out_specs=pl.BlockSpec((tm, tn), lambda i, j, k: (i, j)),
        out_shape=jax.ShapeDtypeStruct((T, N), F32),
        compiler_params=_cp("parallel", "parallel", "arbitrary"), name=name)(a, w, r)


def glu_fwd(ys, w, *, tm, name, comm=None):
    T, N = ys.shape

    def body(y_ref, w_ref, o_ref, t_ref):
        a = _gelu(y_ref[...].astype(F32))
        t = _dot(a.astype(BF16), w_ref[...])
        o_ref[...] = (a * _sigmoid(t)).astype(BF16)
        t_ref[...] = t.astype(BF16)

    return _call(
        body, grid=(T // tm,),
        in_specs=[pl.BlockSpec((tm, N), lambda i: (i, 0)), pl.BlockSpec((N, N), lambda i: (0, 0))],
        out_specs=[pl.BlockSpec((tm, N), lambda i: (i, 0)), pl.BlockSpec((tm, N), lambda i: (i, 0))],
        out_shape=[jax.ShapeDtypeStruct((T, 2 * N), BF16), jax.ShapeDtypeStruct((T, N), BF16)],
        sem=("parallel",), name=name, args=(ys, w), comm=comm)


def glu_bwd(dymix, ys, t, w, *, tm, name, comm=None):
    T, N = ys.shape

    def body(d_ref, y_ref, t_ref, w_ref, dys_ref):
        d = d_ref[...].astype(F32)
        ysv = y_ref[...].astype(F32)
        s = _sigmoid(t_ref[...].astype(F32))
        dt = (d * _gelu(ysv) * s * (1.0 - s)).astype(BF16)
        dys_ref[...] = ((d * s + _dot_nt(dt, w_ref[...])) * _gelu_grad(ysv)).astype(BF16)

    spec = pl.BlockSpec((tm, N), lambda i: (i, 0))
    return _call(body, grid=(T // tm,), in_specs=[spec, spec, spec, pl.BlockSpec((N, N), lambda i: (0, 0))],
                 out_specs=[spec], out_shape=[jax.ShapeDtypeStruct((T, N), BF16)], sem=("parallel",), name=name,
                 args=(dymix, ys, t, w), comm=comm)


def glu_dw(dymix, ys, t, pack, *, pack_spec, tt, name):
    T, N = ys.shape

    def body(d_ref, y_ref, t_ref, pack_ref, o_ref):
        i = pl.program_id(0)
        a = _gelu(y_ref[...].astype(F32))
        s = _sigmoid(t_ref[...].astype(F32))
        dt = (d_ref[...].astype(F32) * a * s * (1.0 - s)).astype(BF16)
        part = _dot_tn(a.astype(BF16), dt).reshape(o_ref.shape)

        @pl.when(i == 0)
        def _():
            o_ref[...] = part

        @pl.when(i > 0)
        def _():
            o_ref[...] += part

    spec = pl.BlockSpec((tt, N), lambda i: (i, 0))
    return pl.pallas_call(body, grid=(T // tt,), in_specs=[spec, spec, spec, _ANY], out_specs=pack_spec,
                          out_shape=jax.ShapeDtypeStruct(pack.shape, F32), input_output_aliases={3: 0},
                          compiler_params=_cp("arbitrary"), name=name)(dymix, ys, t, pack)


def matmul_nt(g, w, *, epi=None, epi_args=(), g2=None, w2=None, tm, tko, tn, out_dtype=BF16, name, wspec=None,
              k_out=None, comm=None, w_is_nk=False):
    T, N = g.shape
    K = k_out if wspec is not None else (w.shape[1] if w_is_nk else w.shape[0])
    wget = (lambda r_: r_[...]) if wspec is None else _pack_block
    mm = _dot if w_is_nk else _dot_nt
    nn = N // tn
    has2 = g2 is not None
    rms = epi == "rmsbwd"
    if rms:
        assert tko == K
    n_epi = len(epi_args)

    def body(*refs):
        g_ref, w_ref = refs[0], refs[1]
        pos = 2
        if has2:
            g2_ref, w2_ref = refs[2], refs[3]
            pos = 4
        e_refs = refs[pos:pos + n_epi]
        pos += n_epi
        o_ref = refs[pos]
        pos += 1
        if rms:
            dg_ref = refs[pos]
            pos += 1
        acc_ref = refs[pos]
        i = pl.program_id(0)
        n = pl.program_id(2)
        part = mm(g_ref[...].astype(BF16), wget(w_ref))

        @pl.when(n == 0)
        def _():
            acc_ref[...] = part

        @pl.when(n > 0)
        def _():
            acc_ref[...] += part

        @pl.when(n == nn - 1)
        def _():
            acc = acc_ref[...]
            if has2:
                acc = acc + mm(g2_ref[...].astype(BF16), w2_ref[...])
            if epi is None:
                o_ref[...] = acc.astype(o_ref.dtype)
            elif epi == "relu2bwd":
                h1 = e_refs[0][...].astype(F32)
                o_ref[...] = (acc * 2.0 * jnp.maximum(h1, 0.0)).astype(o_ref.dtype)
            elif epi == "glubwd":
                da1 = e_refs[0][...].astype(F32)
                ys = e_refs[1][...].astype(F32)
                o_ref[...] = ((da1 + acc) * _gelu_grad(ys)).astype(o_ref.dtype)
            elif epi == "rmsbwd":
                xv, rs, gain, rv = e_refs[0][...], e_refs[1][...], e_refs[2][...], e_refs[3][...]
                xhat = xv * rs
                gd = acc * gain
                o_ref[...] = rv + rs * (gd - xhat * jnp.mean(gd * xhat, axis=-1, keepdims=True))
                part_g = jnp.sum(acc * xhat, axis=0, keepdims=True)

                @pl.when(i == 0)
                def _():
                    dg_ref[...] = part_g

                @pl.when(i > 0)
                def _():
                    dg_ref[...] += part_g

    if wspec is None:
        wspec = (pl.BlockSpec((tn, tko), lambda i, k, n: (n, k)) if w_is_nk
                 else pl.BlockSpec((tko, tn), lambda i, k, n: (k, n)))
    in_specs = [pl.BlockSpec((tm, tn), lambda i, k, n: (i, n)), wspec]
    args = [g, w]
    if has2:
        n2 = g2.shape[1]
        in_specs += [pl.BlockSpec((tm, n2), lambda i, k, n: (i, 0)),
                     pl.BlockSpec((n2, tko), lambda i, k, n: (0, k)) if w_is_nk
                     else pl.BlockSpec((tko, n2), lambda i, k, n: (k, 0))]
        args += [g2, w2]
    if epi == "relu2bwd" or epi == "glubwd":
        in_specs += [pl.BlockSpec((tm, tko), lambda i, k, n: (i, k))] * n_epi
    elif rms:
        in_specs += [pl.BlockSpec((tm, K), lambda i, k, n: (i, 0)), pl.BlockSpec((tm, 1), lambda i, k, n: (i, 0)),
                     pl.BlockSpec((1, K), lambda i, k, n: (0, 0)), pl.BlockSpec((tm, K), lambda i, k, n: (i, 0))]
    args += list(epi_args)
    out_shape = [jax.ShapeDtypeStruct((T, K), F32 if rms else out_dtype)]
    out_specs = [pl.BlockSpec((tm, tko), lambda i, k, n: (i, k))]
    if rms:
        out_shape.append(jax.ShapeDtypeStruct((1, K), F32))
        out_specs.append(pl.BlockSpec((1, K), lambda i, k, n: (0, 0)))
    sem = ("arbitrary",) * 3 if rms else ("parallel", "parallel", "arbitrary")
    res, cres = _call(body, grid=(T // tm, K // tko, nn), in_specs=in_specs, out_specs=out_specs, out_shape=out_shape,
                      scratch_shapes=[pltpu.VMEM((tm, tko), F32)], sem=sem, name=name, args=args, comm=comm)
    res = res if rms else res[0]
    return res if comm is None else (res, cres)


def matmul_tn(a, g, *, act=None, tk, tn, tt, name, pack=None, pack_spec=None, pack_shape=None):
    T, K = a.shape
    N = g.shape[1]
    to_pack = pack_spec is not None

    def body(a_ref, g_ref, *rest):
        o_ref = rest[-1]
        t = pl.program_id(2)
        part = _dot_tn(_act(a_ref[...], act), g_ref[...].astype(BF16))
        part = part.reshape(o_ref.shape)

        @pl.when(t == 0)
        def _():
            o_ref[...] = part

        @pl.when(t > 0)
        def _():
            o_ref[...] += part

    in_specs = [pl.BlockSpec((tt, tk), lambda k, n, t: (t, k)), pl.BlockSpec((tt, tn), lambda k, n, t: (t, n))]
    args = [a, g]
    aliases = {}
    if pack is not None:
        in_specs.append(_ANY)
        args.append(pack)
        aliases = {2: 0}
    return pl.pallas_call(
        body, grid=(K // tk, N // tn, T // tt), in_specs=in_specs,
        out_specs=pack_spec if to_pack else pl.BlockSpec((tk, tn), lambda k, n, t: (k, n)),
        out_shape=jax.ShapeDtypeStruct(pack_shape if to_pack else (K, N), F32), input_output_aliases=aliases,
        compiler_params=_cp("parallel", "parallel", "arbitrary"), name=name)(*args)


def attn_fwd(q, kv, *, tm, name):
    T = q.shape[0]
    M = kv.shape[0]
    scale = XA_HEAD_DIM ** -0.5

    def body(q_ref, kv_ref, o_ref):
        for h in range(XA_HEADS):
            sl = slice(h * XA_HEAD_DIM, (h + 1) * XA_HEAD_DIM)
            kh = kv_ref[:, h * XA_HEAD_DIM:(h + 1) * XA_HEAD_DIM]
            vh = kv_ref[:, D_MODEL + h * XA_HEAD_DIM:D_MODEL + (h + 1) * XA_HEAD_DIM]
            s = _dot_nt(q_ref[:, sl], kh) * scale
            s = s - jnp.max(s, axis=-1, keepdims=True)
            p = jnp.exp(s)
            p = p / jnp.sum(p, axis=-1, keepdims=True)
            o_ref[:, sl] = _dot(p.astype(BF16), vh).astype(BF16)

    return pl.pallas_call(
        body, grid=(T // tm,),
        in_specs=[pl.BlockSpec((tm, D_MODEL), lambda i: (i, 0)), pl.BlockSpec((M, 2 * D_MODEL), lambda i: (0, 0))],
        out_specs=pl.BlockSpec((tm, D_MODEL), lambda i: (i, 0)),
        out_shape=jax.ShapeDtypeStruct((T, D_MODEL), BF16),
        compiler_params=_cp("parallel"), name=name)(q, kv)


def attn_bwd(q, kv, do, *, tm, name):
    T = q.shape[0]
    M = kv.shape[0]
    scale = XA_HEAD_DIM ** -0.5

    def body(q_ref, kv_ref, do_ref, dq_ref, dkv_ref):
        i = pl.program_id(0)

        @pl.when(i == 0)
        def _():
            dkv_ref[...] = jnp.zeros_like(dkv_ref)

        for h in range(XA_HEADS):
            sl = slice(h * XA_HEAD_DIM, (h + 1) * XA_HEAD_DIM)
            slv = slice(D_MODEL + h * XA_HEAD_DIM, D_MODEL + (h + 1) * XA_HEAD_DIM)
            qh = q_ref[:, sl]
            kh = kv_ref[:, sl]
            vh = kv_ref[:, slv]
            doh = do_ref[:, sl]
            s = _dot_nt(qh, kh) * scale
            s = s - jnp.max(s, axis=-1, keepdims=True)
            p = jnp.exp(s)
            p = p / jnp.sum(p, axis=-1, keepdims=True)
            pb = p.astype(BF16)
            dkv_ref[:, slv] += _dot_tn(pb, doh)
            dp = _dot_nt(doh, vh)
            ds = (p * (dp - jnp.sum(dp * p, axis=-1, keepdims=True)) * scale).astype(BF16)
            dq_ref[:, sl] = _dot(ds, kh).astype(BF16)
            dkv_ref[:, sl] += _dot_tn(ds, qh)

    spec = pl.BlockSpec((tm, D_MODEL), lambda i: (i, 0))
    kvspec = pl.BlockSpec((M, 2 * D_MODEL), lambda i: (0, 0))
    return pl.pallas_call(
        body, grid=(T // tm,), in_specs=[spec, kvspec, spec], out_specs=[spec, kvspec],
        out_shape=[jax.ShapeDtypeStruct((T, D_MODEL), BF16), jax.ShapeDtypeStruct((M, 2 * D_MODEL), F32)],
        compiler_params=_cp("arbitrary"), name=name)(q, kv, do)


def loss_head(x, g, target, *, tm, name):
    T, D = x.shape

    def body(x_ref, g_ref, t_ref, l_ref, dx_ref, dg_ref):
        i = pl.program_id(0)
        xv = x_ref[...]
        gain = g_ref[...]
        r = lax.rsqrt(jnp.mean(xv * xv, axis=-1, keepdims=True) + EPS)
        xhat = xv * r
        err = xhat * gain - t_ref[...]
        part_l = jnp.full((1, 128), 0.5 / D, F32) * jnp.sum(err * err)
        dy = err * (1.0 / D)
        gd = dy * gain
        dx_ref[...] = r * (gd - xhat * jnp.mean(gd * xhat, axis=-1, keepdims=True))
        part_g = jnp.sum(dy * xhat, axis=0, keepdims=True)

        @pl.when(i == 0)
        def _():
            l_ref[...] = part_l
            dg_ref[...] = part_g

        @pl.when(i > 0)
        def _():
            l_ref[...] += part_l
            dg_ref[...] += part_g

    spec = pl.BlockSpec((tm, D), lambda i: (i, 0))
    return pl.pallas_call(
        body, grid=(T // tm,), in_specs=[spec, pl.BlockSpec((1, D), lambda i: (0, 0)), spec],
        out_specs=[pl.BlockSpec((1, 128), lambda i: (0, 0)), spec, pl.BlockSpec((1, D), lambda i: (0, 0))],
        out_shape=[jax.ShapeDtypeStruct((1, 128), F32), jax.ShapeDtypeStruct((T, D), F32),
                   jax.ShapeDtypeStruct((1, D), F32)],
        compiler_params=_cp("arbitrary"), name=name)(x, g, target)


S5_LS = 128
S5_SEG = 8
S5_TB = S5_LS * S5_SEG
S5_CH = 8
S5_CW = 128
S5_NST = 512
S5_UNROLL = 4


def _cmul(ar, ai, br, bi):
    return ar * br - ai * bi, ar * bi + ai * br


def s5_params_fwd(a_re, a_im, log_dt, b_re, b_im, expand, *, name):
    G, P = a_re.shape

    def body(ar_ref, ai_ref, ld_ref, br_ref, bi_ref, e_ref, abr_ref, abi_ref, apr_ref, api_ref, bbr_ref, bbi_ref):
        ar, ai = ar_ref[...], ai_ref[...]
        dt = jnp.exp(ld_ref[...])
        mag = jnp.exp(dt * ar)
        abr, abi = mag * jnp.cos(dt * ai), mag * jnp.sin(dt * ai)
        den = ar * ar + ai * ai
        zr, zi = abr - 1.0, abi
        fr = (zr * ar + zi * ai) / den
        fi = (zi * ar - zr * ai) / den
        frx, fix = _dot_hi(fr, e_ref[...]), _dot_hi(fi, e_ref[...])
        br, bi = br_ref[...], bi_ref[...]
        bbr_ref[...] = frx * br - fix * bi
        bbi_ref[...] = frx * bi + fix * br
        abr_ref[...] = abr
        abi_ref[...] = abi
        pr, pi = abr, abi
        for _ in range(int(math.log2(S5_LS))):
            pr, pi = _cmul(pr, pi, pr, pi)
        apr_ref[...] = pr
        api_ref[...] = pi

    small = jax.ShapeDtypeStruct((G, P), F32)
    big = jax.ShapeDtypeStruct(b_re.shape, F32)
    return pl.pallas_call(body, out_shape=[small, small, small, small, big, big], name=name)(
        a_re, a_im, log_dt, b_re, b_im, expand)


def s5_params_bwd(a_re, a_im, log_dt, b_re, b_im, g_abr, g_abi, g_bbr, g_bbi, expand, expand_t, *, name):
    G, P = a_re.shape

    def body(ar_ref, ai_ref, ld_ref, br_ref, bi_ref, gar_ref, gai_ref, gbr_ref, gbi_ref, e_ref, et_ref,
             dar_ref, dai_ref, dld_ref, dbr_ref, dbi_ref):
        ar, ai = ar_ref[...], ai_ref[...]
        dt = jnp.exp(ld_ref[...])
        mag = jnp.exp(dt * ar)
        cs, sn = jnp.cos(dt * ai), jnp.sin(dt * ai)
        abr, abi = mag * cs, mag * sn
        den = ar * ar + ai * ai
        zr, zi = abr - 1.0, abi
        fr = (zr * ar + zi * ai) / den
        fi = (zi * ar - zr * ai) / den
        frx, fix = _dot_hi(fr, e_ref[...]), _dot_hi(fi, e_ref[...])
        br, bi = br_ref[...], bi_ref[...]
        gbr, gbi = gbr_ref[...], gbi_ref[...]
        dbr_ref[...] = frx * gbr + fix * gbi
        dbi_ref[...] = -fix * gbr + frx * gbi
        gfr = _dot_hi(br * gbr + bi * gbi, et_ref[...])
        gfi = _dot_hi(-bi * gbr + br * gbi, et_ref[...])
        g_zr = (gfr * ar - gfi * ai) / den
        g_zi = (gfr * ai + gfi * ar) / den
        g_ar = gfr * (zr - fr * 2.0 * ar) / den + gfi * (zi - fi * 2.0 * ar) / den
        g_ai = gfr * (zi - fr * 2.0 * ai) / den + gfi * (-zr - fi * 2.0 * ai) / den
        t_abr = gar_ref[...] + g_zr
        t_abi = gai_ref[...] + g_zi
        g_mag = t_abr * cs + t_abi * sn
        g_th = mag * (-t_abr * sn + t_abi * cs)
        dar_ref[...] = g_ar + g_mag * mag * dt
        dai_ref[...] = g_ai + g_th * dt
        g_dt = jnp.sum(g_mag * mag * ar + g_th * ai, axis=-1, keepdims=True)
        dld_ref[...] = g_dt * dt

    small = jax.ShapeDtypeStruct((G, P), F32)
    big = jax.ShapeDtypeStruct(b_re.shape, F32)
    return pl.pallas_call(body, out_shape=[small, small, jax.ShapeDtypeStruct((G, 1), F32), big, big], name=name)(
        a_re, a_im, log_dt, b_re, b_im, g_abr, g_abi, g_bbr, g_bbi, expand, expand_t)


def _s5_permute_in(src_ref, dst_ref):
    for i in range(S5_LS):
        dst_ref[pl.ds(8 * i, 8), :] = src_ref[pl.ds(i, 8, stride=S5_LS), :]


def _s5_permute_out(src_ref, dst_ref):
    for r in range(S5_SEG):
        for k in range(S5_LS // 8):
            dst_ref[pl.ds(r * S5_LS + 8 * k, 8), :] = src_ref[pl.ds(64 * k + r, 8, stride=8), :]


def _s5_scan(a_r, a_i, dr_ref, di_ref, init_r, init_i, store=None, reverse=False, conj=False):
    sgn = -1.0 if conj else 1.0

    def steps(n, c):
        sr, si = c
        for u in range(S5_UNROLL):
            k = n * S5_UNROLL + u
            i = S5_LS - 1 - k if reverse else k
            nr = a_r * sr - sgn * a_i * si + dr_ref[i]
            ni = a_r * si + sgn * a_i * sr + di_ref[i]
            if store is not None:
                store(i, nr, ni, sr, si)
            sr, si = nr, ni
        return sr, si

    return lax.fori_loop(0, S5_LS // S5_UNROLL, steps, (init_r, init_i))


def _s5_stitch(apr, api, fin_r, fin_i, car_r, car_i, reverse=False, conj=False):
    sgn = -1.0 if conj else 1.0
    rows_r, rows_i = [None] * S5_SEG, [None] * S5_SEG
    order = range(S5_SEG - 1, -1, -1) if reverse else range(S5_SEG)
    for r in order:
        rows_r[r], rows_i[r] = car_r, car_i
        fr, fi = fin_r[r:r + 1], fin_i[r:r + 1]
        car_r, car_i = (apr * car_r - sgn * api * car_i + fr, apr * car_i + sgn * api * car_r + fi)
    return jnp.concatenate(rows_r, 0), jnp.concatenate(rows_i, 0), car_r, car_i


def _s5_specs(nb, rev):
    blk = (lambda c, b: (nb - 1 - b, c)) if rev else (lambda c, b: (b, c))
    tok = pl.BlockSpec((S5_TB, S5_CW), blk)
    par_b = pl.BlockSpec((1, S5_CW, S5_NST), lambda c, b: (c, 0, 0))
    par_c = pl.BlockSpec((1, S5_NST, S5_CW), lambda c, b: (c, 0, 0))
    vec_s = pl.BlockSpec((1, 1, S5_NST), lambda c, b: (c, 0, 0))
    vec_c = pl.BlockSpec((1, 1, S5_CW), lambda c, b: (c, 0, 0))
    return tok, par_b, par_c, vec_s, vec_c


def s5_fwd(proj, bbr, bbi, ccr, cci, abr, abi, apr, api, dskip, *, name, comm=None):
    T = proj.shape[0]
    nb = T // S5_TB
    zeros8 = functools.partial(jnp.zeros, (S5_SEG, S5_NST), F32)

    def body(u_ref, bbr_ref, bbi_ref, ccr_ref, cci_ref, ar_ref, ai_ref, apr_ref, api_ref, d_ref,
             y_ref, csr_ref, csi_ref, uf_ref, up_ref, dr_ref, di_ref, sr_ref, si_ref, yp_ref, car_ref, cai_ref):
        b = pl.program_id(1)

        @pl.when(b == 0)
        def _():
            car_ref[...] = jnp.zeros_like(car_ref)
            cai_ref[...] = jnp.zeros_like(cai_ref)

        csr_ref[0, 0] = car_ref[...]
        csi_ref[0, 0] = cai_ref[...]
        uf_ref[...] = u_ref[...].astype(F32)
        _s5_permute_in(uf_ref, up_ref)
        upb = up_ref[...].astype(BF16)
        dr_ref[...] = _dot(upb, bbr_ref[0]).reshape(S5_LS, S5_SEG, S5_NST)
        di_ref[...] = _dot(upb, bbi_ref[0]).reshape(S5_LS, S5_SEG, S5_NST)
        a_r = jnp.broadcast_to(ar_ref[0], (S5_SEG, S5_NST))
        a_i = jnp.broadcast_to(ai_ref[0], (S5_SEG, S5_NST))
        fin_r, fin_i = _s5_scan(a_r, a_i, dr_ref, di_ref, zeros8(), zeros8())
        cin_r, cin_i, ncr, nci = _s5_stitch(apr_ref[0], api_ref[0], fin_r, fin_i, car_ref[...], cai_ref[...])
        car_ref[...] = ncr
        cai_ref[...] = nci

        def store(i, nr, ni, sr, si):
            sr_ref[i] = nr
            si_ref[i] = ni

        _s5_scan(a_r, a_i, dr_ref, di_ref, cin_r, cin_i, store=store)
        s_r = sr_ref[...].reshape(S5_TB, S5_NST).astype(BF16)
        s_i = si_ref[...].reshape(S5_TB, S5_NST).astype(BF16)
        yp_ref[...] = _dot(s_r, ccr_ref[0]) - _dot(s_i, cci_ref[0]) + d_ref[0] * up_ref[...]
        _s5_permute_out(yp_ref, uf_ref)
        y_ref[...] = uf_ref[...].astype(BF16)

    tok, par_b, par_c, vec_s, vec_c = _s5_specs(nb, False)
    cs_spec = pl.BlockSpec((1, 1, 1, S5_NST), lambda c, b: (b, c, 0, 0))
    cs_shape = jax.ShapeDtypeStruct((nb, S5_CH, 1, S5_NST), F32)
    tokbuf = pltpu.VMEM((S5_TB, S5_CW), F32)
    stbuf = pltpu.VMEM((S5_LS, S5_SEG, S5_NST), F32)
    return _call(
        body, grid=(S5_CH, nb),
        in_specs=[tok, par_b, par_b, par_c, par_c, vec_s, vec_s, vec_s, vec_s, vec_c],
        out_specs=[tok, cs_spec, cs_spec],
        out_shape=[jax.ShapeDtypeStruct((T, D_S5), BF16), cs_shape, cs_shape],
        scratch_shapes=[tokbuf, tokbuf, stbuf, stbuf, stbuf, stbuf, tokbuf,
                        pltpu.VMEM((1, S5_NST), F32), pltpu.VMEM((1, S5_NST), F32)],
        sem=("parallel", "arbitrary"), name=name, comm=comm,
        args=(proj, bbr, bbi, ccr, cci, abr, abi, apr, api, dskip))


def s5_bwd(proj, dys, csr, csi, bbr, bbi, ccr, cci, abr, abi, apr, api, dskip, dproj, *, name, comm=None):
    T = proj.shape[0]
    nb = T // S5_TB
    zeros8 = functools.partial(jnp.zeros, (S5_SEG, S5_NST), F32)

    def body(u_ref, gy_ref, csr_ref, csi_ref, bbr_ref, bbi_ref, ccr_ref, cci_ref, ar_ref, ai_ref, apr_ref, api_ref,
             d_ref, dproj_ref, du_ref, dbr_ref, dbi_ref, dcr_ref, dci_ref, dd_ref, dar_ref, dai_ref,
             tmp_ref, up_ref, gyp_ref, dr_ref, di_ref, sr_ref, si_ref, gr_ref, gi_ref, car_ref, cai_ref):
        b = pl.program_id(1)

        @pl.when(b == 0)
        def _():
            car_ref[...] = jnp.zeros_like(car_ref)
            cai_ref[...] = jnp.zeros_like(cai_ref)
            for ref in (dbr_ref, dbi_ref, dcr_ref, dci_ref, dd_ref, dar_ref, dai_ref):
                ref[...] = jnp.zeros_like(ref)

        tmp_ref[...] = u_ref[...].astype(F32)
        _s5_permute_in(tmp_ref, up_ref)
        tmp_ref[...] = gy_ref[...].astype(F32)
        _s5_permute_in(tmp_ref, gyp_ref)
        dr_ref[...] = _dot(up_ref[...].astype(BF16), bbr_ref[0]).reshape(S5_LS, S5_SEG, S5_NST)
        di_ref[...] = _dot(up_ref[...].astype(BF16), bbi_ref[0]).reshape(S5_LS, S5_SEG, S5_NST)
        a_r = jnp.broadcast_to(ar_ref[0], (S5_SEG, S5_NST))
        a_i = jnp.broadcast_to(ai_ref[0], (S5_SEG, S5_NST))
        fin_r, fin_i = _s5_scan(a_r, a_i, dr_ref, di_ref, zeros8(), zeros8())
        cin_r, cin_i, _, _ = _s5_stitch(apr_ref[0], api_ref[0], fin_r, fin_i, csr_ref[0, 0], csi_ref[0, 0])
        sr_ref[0] = cin_r
        si_ref[0] = cin_i

        def store_s(i, nr, ni, sr, si):
            sr_ref[i + 1] = nr
            si_ref[i + 1] = ni

        _s5_scan(a_r, a_i, dr_ref, di_ref, cin_r, cin_i, store=store_s)
        dcr_ref[0] += _dot_tn(sr_ref[pl.ds(1, S5_LS)].reshape(S5_TB, S5_NST).astype(BF16), gyp_ref[...].astype(BF16))
        dci_ref[0] -= _dot_tn(si_ref[pl.ds(1, S5_LS)].reshape(S5_TB, S5_NST).astype(BF16), gyp_ref[...].astype(BF16))
        dd_ref[0] += jnp.sum(gyp_ref[...] * up_ref[...], axis=0, keepdims=True)
        dr_ref[...] = _dot_nt(gyp_ref[...].astype(BF16), ccr_ref[0]).reshape(S5_LS, S5_SEG, S5_NST)
        di_ref[...] = (-_dot_nt(gyp_ref[...].astype(BF16), cci_ref[0])).reshape(S5_LS, S5_SEG, S5_NST)
        fin_r, fin_i = _s5_scan(a_r, a_i, dr_ref, di_ref, zeros8(), zeros8(), reverse=True, conj=True)
        gin_r, gin_i, ncr, nci = _s5_stitch(apr_ref[0], api_ref[0], fin_r, fin_i, car_ref[...], cai_ref[...],
                                            reverse=True, conj=True)
        car_ref[...] = ncr
        cai_ref[...] = nci
        def steps_g(n, carry):
            gr, gi, acc_r, acc_i = carry
            for u in range(S5_UNROLL):
                i = S5_LS - 1 - (n * S5_UNROLL + u)
                nr = a_r * gr + a_i * gi + dr_ref[i]
                ni = a_r * gi - a_i * gr + di_ref[i]
                gr_ref[i] = nr
                gi_ref[i] = ni
                pr, pi = sr_ref[i], si_ref[i]
                acc_r, acc_i = acc_r + (nr * pr + ni * pi), acc_i + (ni * pr - nr * pi)
                gr, gi = nr, ni
            return gr, gi, acc_r, acc_i

        _, _, acc_r, acc_i = lax.fori_loop(0, S5_LS // S5_UNROLL, steps_g, (gin_r, gin_i, zeros8(), zeros8()))
        dar_ref[0] += jnp.sum(acc_r, axis=0, keepdims=True)
        dai_ref[0] += jnp.sum(acc_i, axis=0, keepdims=True)
        g_r = lambda: gr_ref[...].reshape(S5_TB, S5_NST).astype(BF16)
        g_i = lambda: gi_ref[...].reshape(S5_TB, S5_NST).astype(BF16)
        dbr_ref[0] += _dot_tn(up_ref[...].astype(BF16), g_r())
        dbi_ref[0] += _dot_tn(up_ref[...].astype(BF16), g_i())
        gyp_ref[...] = _dot_nt(g_r(), bbr_ref[0]) + _dot_nt(g_i(), bbi_ref[0]) + d_ref[0] * gyp_ref[...]
        _s5_permute_out(gyp_ref, tmp_ref)
        du_ref[...] = tmp_ref[...].astype(BF16)

    tok, par_b, par_c, vec_s, vec_c = _s5_specs(nb, True)
    cs_spec = pl.BlockSpec((1, 1, 1, S5_NST), lambda c, b: (nb - 1 - b, c, 0, 0))
    tokbuf = pltpu.VMEM((S5_TB, S5_CW), F32)
    stbuf = pltpu.VMEM((S5_LS, S5_SEG, S5_NST), F32)
    stbuf1 = pltpu.VMEM((S5_LS + 1, S5_SEG, S5_NST), F32)
    return _call(
        body, grid=(S5_CH, nb), comm=comm,
        in_specs=[tok, tok, cs_spec, cs_spec, par_b, par_b, par_c, par_c, vec_s, vec_s, vec_s, vec_s, vec_c, _ANY],
        out_specs=[tok, par_b, par_b, par_c, par_c, vec_c, vec_s, vec_s], aliases={13: 0},
        out_shape=[jax.ShapeDtypeStruct(dproj.shape, BF16),
                   jax.ShapeDtypeStruct((S5_CH, S5_CW, S5_NST), F32), jax.ShapeDtypeStruct((S5_CH, S5_CW, S5_NST), F32),
                   jax.ShapeDtypeStruct((S5_CH, S5_NST, S5_CW), F32), jax.ShapeDtypeStruct((S5_CH, S5_NST, S5_CW), F32),
                   jax.ShapeDtypeStruct((S5_CH, 1, S5_CW), F32),
                   jax.ShapeDtypeStruct((S5_CH, 1, S5_NST), F32), jax.ShapeDtypeStruct((S5_CH, 1, S5_NST), F32)],
        scratch_shapes=[tokbuf, tokbuf, tokbuf, stbuf, stbuf, stbuf1, stbuf1, stbuf, stbuf,
                        pltpu.VMEM((1, S5_NST), F32), pltpu.VMEM((1, S5_NST), F32)],
        sem=("parallel", "arbitrary"), name=name,
        args=(proj, dys, csr, csi, bbr, bbi, ccr, cci, abr, abi, apr, api, dskip, dproj))


SSD_L = SSD_CHUNK
SSD_GW = 256
NEG = -1e30


def _expand16(v):
    lane = lax.broadcasted_iota(jnp.int32, (v.shape[0], 128), 1)
    parts = [jnp.where(lane < SSD_HEADDIM, v[:, 2 * j:2 * j + 1], v[:, 2 * j + 1:2 * j + 2]) for j in range(8)]
    return jnp.concatenate(parts, axis=1)


def _headsum(v, hsum):
    hi = v.astype(BF16)
    lo = (v - hi.astype(F32)).astype(BF16)
    return _dot(hi, hsum) + _dot(lo, hsum)


def _softplus(x):
    return jnp.maximum(x, 0.0) + jnp.log(1.0 + jnp.exp(-jnp.abs(x)))


def _ssd_chunk_fwd(z, xbc, tail, dtraw, hprev, cw, cb, dtb, alog, dsk, nw, tril):
    L = SSD_L
    f = {}
    xe = jnp.concatenate([tail, xbc], axis=0)
    sh = [xbc] + [pltpu.roll(xe, s, 0)[8:] for s in (1, 2, 3)]
    conv = cb + cw[3:4] * sh[0] + cw[2:3] * sh[1] + cw[1:2] * sh[2] + cw[0:1] * sh[3]
    sig = _sigmoid(conv)
    xa = conv * sig
    xs, bm, cm = xa[:, :D_SSD], xa[:, D_SSD:D_SSD + 512], xa[:, D_SSD + 512:]
    pre = dtraw + dtb
    dt = _softplus(pre)
    a_h = -jnp.exp(alog)
    acum = _dot_hi(tril, dt * a_h)
    acum_t = acum.T
    alast = acum[L - 1:L]
    exp_a = jnp.exp(acum)
    dec = jnp.exp(alast - acum)
    exp_al = jnp.exp(alast)
    dt_x, dec_x, exp_a_x, exp_al_x = _expand16(dt), _expand16(dec), _expand16(exp_a), _expand16(exp_al)
    d_x = _expand16(dsk)
    xh = xs * dt_x
    xhb = xh.astype(BF16)
    xd = (xh * dec_x).astype(BF16)
    row = lax.broadcasted_iota(jnp.int32, (L, L), 0)
    col = lax.broadcasted_iota(jnp.int32, (L, L), 1)
    causal = row >= col
    lane = lax.broadcasted_iota(jnp.int32, (L, 128), 1)
    low = lane < SSD_HEADDIM
    hb = hprev.astype(BF16)
    y_pairs, yoff_parts, st_parts, cbs, lms = [], [], [], [], []
    for g in range(SSD_NGROUPS):
        bg = bm[:, g * 128:(g + 1) * 128].astype(BF16)
        cg = cm[:, g * 128:(g + 1) * 128].astype(BF16)
        cbg = _dot_nt(cg, bg)
        cbs.append(cbg)
        for j in (2 * g, 2 * g + 1):
            xp = xhb[:, j * 128:(j + 1) * 128]
            ys = []
            for h in (2 * j, 2 * j + 1):
                lm = jnp.exp(jnp.where(causal, acum[:, h:h + 1] - acum_t[h:h + 1, :], NEG))
                lms.append(lm)
                ys.append(_dot((cbg * lm).astype(BF16), xp))
            y_pairs.append(jnp.where(low, ys[0], ys[1]))
        gs = slice(g * SSD_GW, (g + 1) * SSD_GW)
        yoff_parts.append(_dot(cg, hb[:, gs]) * exp_a_x[:, gs])
        st_parts.append(_dot_tn(bg, xd[:, gs]))
    yoff = jnp.concatenate(yoff_parts, axis=1)
    y = jnp.concatenate(y_pairs, axis=1) + yoff + d_x * xs
    hnew = exp_al_x * hprev + jnp.concatenate(st_parts, axis=1)
    sz = _sigmoid(z)
    gz = y * (z * sz)
    r = lax.rsqrt(jnp.mean(gz * gz, axis=-1, keepdims=True) + EPS)
    out = gz * r * nw
    f.update(sh=sh, conv=conv, sig=sig, xs=xs, bm=bm, cm=cm, pre=pre, dt=dt, a_h=a_h, exp_a_x=exp_a_x, dec_x=dec_x,
             exp_al=exp_al, exp_al_x=exp_al_x, dt_x=dt_x, d_x=d_x, xh=xh, xhb=xhb, xd=xd, causal=causal, low=low, hb=hb,
             cbs=cbs, lms=lms, yoff=yoff, y=y, sz=sz, gz=gz, r=r)
    return out, hnew, f


def _ssd_params(conv_w, conv_b, dt_bias, a_log, d_skip, norm_w):
    pad16 = lambda v: jnp.pad(v.reshape(1, SSD_HEADS), ((0, 0), (0, 128 - SSD_HEADS)))
    return (jnp.pad(conv_w, ((0, 8 - SSD_CONV), (0, 0))), conv_b.reshape(1, D_XBC), pad16(dt_bias), pad16(a_log),
            pad16(d_skip), norm_w.reshape(1, D_SSD))


def _ssd_param_specs():
    full = lambda shape: pl.BlockSpec(shape, lambda i: (0, 0))
    return [full((8, D_XBC)), full((1, D_XBC)), full((1, 128)), full((1, 128)), full((1, 128)), full((1, D_SSD))]


def ssd_fwd(proj, dtraw, params, tril, ymix, *, name, comm=None):
    T = proj.shape[0]
    nc = T // SSD_L

    def body(z_ref, x_ref, dt_ref, cw_ref, cb_ref, dtb_ref, al_ref, dsk_ref, nw_ref, tril_ref, ymix_ref,
             o_ref, hs_ref, h_ref, tail_ref):
        i = pl.program_id(0)

        @pl.when(i == 0)
        def _():
            h_ref[...] = jnp.zeros_like(h_ref)
            tail_ref[...] = jnp.zeros_like(tail_ref)

        xbc = x_ref[...].astype(F32)
        hprev = h_ref[...]
        hs_ref[0] = hprev
        out, hnew, _ = _ssd_chunk_fwd(z_ref[...].astype(F32), xbc, tail_ref[...], dt_ref[...], hprev, cw_ref[...],
                                      cb_ref[...], dtb_ref[...], al_ref[...], dsk_ref[...], nw_ref[...], tril_ref[...])
        o_ref[...] = out.astype(BF16)
        h_ref[...] = hnew
        tail_ref[...] = xbc[SSD_L - 8:]

    return _call(
        body, grid=(nc,),
        in_specs=[pl.BlockSpec((SSD_L, D_SSD), lambda i: (i, 1)), pl.BlockSpec((SSD_L, D_XBC), lambda i: (i, 1)),
                  pl.BlockSpec((SSD_L, 128), lambda i: (i, 0))] + _ssd_param_specs()
                 + [pl.BlockSpec((SSD_L, SSD_L), lambda i: (0, 0)), _ANY],
        out_specs=[pl.BlockSpec((SSD_L, D_SSD), lambda i: (i, 1)),
                   pl.BlockSpec((1, SSD_STATE, D_SSD), lambda i: (i, 0, 0))],
        out_shape=[jax.ShapeDtypeStruct(ymix.shape, BF16), jax.ShapeDtypeStruct((nc, SSD_STATE, D_SSD), F32)],
        scratch_shapes=[pltpu.VMEM((SSD_STATE, D_SSD), F32), pltpu.VMEM((8, D_XBC), F32)],
        sem=("arbitrary",), name=name, args=(proj, proj, dtraw, *params, tril, ymix), aliases={10: 0}, comm=comm)


def ssd_bwd(proj, dtraw, hs, dymix, params, tril, triu, trils, headsum, *, name, comm=None):
    T = proj.shape[0]
    nc = T // SSD_L
    L = SSD_L

    def body(z_ref, x_ref, xprev_ref, dt_ref, hs_ref, do_ref, cw_ref, cb_ref, dtb_ref, al_ref, dsk_ref, nw_ref,
             tril_ref, triu_ref, trils_ref, hsum_ref,
             dp_ref, ddt_ref, dcw_ref, dcb_ref, ddtb_ref, dal_ref, ddsk_ref, dnw_ref, dh_ref, dnext_ref):
        i = pl.program_id(0)

        @pl.when(i == 0)
        def _():
            dh_ref[...] = jnp.zeros_like(dh_ref)
            dnext_ref[...] = jnp.zeros_like(dnext_ref)
            for ref in (dcw_ref, dcb_ref, ddtb_ref, dal_ref, ddsk_ref, dnw_ref):
                ref[...] = jnp.zeros_like(ref)

        z = z_ref[...].astype(F32)
        xbc = x_ref[...].astype(F32)
        tail = jnp.where(i == nc - 1, 0.0, xprev_ref[...].astype(F32))
        hprev = hs_ref[0]
        cw, nw = cw_ref[...], nw_ref[...]
        hsum = hsum_ref[...]
        _, _, f = _ssd_chunk_fwd(z, xbc, tail, dt_ref[...], hprev, cw, cb_ref[...], dtb_ref[...], al_ref[...],
                                 dsk_ref[...], nw, tril_ref[...])
        dout = do_ref[...].astype(F32)
        dh = dh_ref[...]
        ghat = f["gz"] * f["r"]
        dn = dout * nw
        dgz = f["r"] * (dn - ghat * jnp.mean(dn * ghat, axis=-1, keepdims=True))
        dnw_ref[...] += jnp.sum(dout * ghat, axis=0, keepdims=True)
        sz = f["sz"]
        dy = dgz * (z * sz)
        dp_ref[:, :D_S5] = jnp.zeros((L, D_S5), BF16)
        dp_ref[:, D_S5:D_S5 + D_SSD] = (dgz * f["y"] * sz * (1.0 + z * (1.0 - sz))).astype(BF16)
        xs = f["xs"]
        ddsk_ref[...] += jnp.sum(_headsum(dy * xs, hsum), axis=0, keepdims=True)
        dyb = dy.astype(BF16)
        dye = (dy * f["exp_a_x"]).astype(BF16)
        dhb = dh.astype(BF16)
        lane = lax.broadcasted_iota(jnp.int32, (L, 128), 1)
        sub = lax.broadcasted_iota(jnp.int32, (128, L), 0)
        zero_b = jnp.zeros((L, 128), BF16)
        rsum = jnp.zeros((L, 128), F32)
        csum_t = jnp.zeros((128, L), F32)
        dx_pairs, dxst_parts, db_parts, dc_parts, dhp_parts = [], [], [], [], []
        for g in range(SSD_NGROUPS):
            gs = slice(g * SSD_GW, (g + 1) * SSD_GW)
            bg = f["bm"][:, g * 128:(g + 1) * 128].astype(BF16)
            cg = f["cm"][:, g * 128:(g + 1) * 128].astype(BF16)
            cbg = f["cbs"][g]
            dcb_g = jnp.zeros((L, L), F32)
            for j in (2 * g, 2 * g + 1):
                xp = f["xhb"][:, j * 128:(j + 1) * 128]
                dyp = dyb[:, j * 128:(j + 1) * 128]
                dxs = []
                for half, h in enumerate((2 * j, 2 * j + 1)):
                    lm = f["lms"][h]
                    dyh = jnp.where(f["low"], dyp, zero_b) if half == 0 else jnp.where(f["low"], zero_b, dyp)
                    dw = jnp.where(f["causal"], _dot_nt(dyh, xp), 0.0)
                    w = cbg * lm
                    e = dw * w
                    dcb_g = dcb_g + dw * lm
                    rsum = jnp.where(lane == h, jnp.sum(e, axis=1, keepdims=True), rsum)
                    csum_t = jnp.where(sub == h, jnp.sum(e, axis=0, keepdims=True), csum_t)
                    dxs.append(_dot_tn(w.astype(BF16), dyp))
                dx_pairs.append(jnp.where(f["low"], dxs[0], dxs[1]))
            dcbb = dcb_g.astype(BF16)
            dxst_parts.append(f["dec_x"][:, gs] * _dot(bg, dhb[:, gs]))
            dc_parts.append(_dot(dcbb, bg) + _dot_nt(dye[:, gs], f["hb"][:, gs]))
            db_parts.append(_dot_tn(dcbb, cg) + _dot_nt(f["xd"][:, gs], dhb[:, gs]))
            dhp_parts.append(f["exp_al_x"][:, gs] * dh[:, gs] + _dot_tn(cg, dye[:, gs]))
        dxst = jnp.concatenate(dxst_parts, axis=1)
        dxh = jnp.concatenate(dx_pairs, axis=1) + dxst
        q = _headsum(f["yoff"] * dy, hsum)
        dstate = _headsum(f["xh"] * dxst, hsum)
        h0t = jnp.sum(_headsum(dh * hprev, hsum), axis=0, keepdims=True) * f["exp_al"]
        da = _dot_hi(triu_ref[...], rsum - csum_t.T + q) + _dot_hi(trils_ref[...], dstate) + h0t
        dt, a_h = f["dt"], f["a_h"]
        ddt = _headsum(dxh * xs, hsum) + da * a_h
        dal_ref[...] += jnp.sum(da * dt, axis=0, keepdims=True) * a_h
        ddtraw = ddt * _sigmoid(f["pre"])
        first16 = lane < SSD_HEADS
        ddtraw = jnp.where(first16, ddtraw, 0.0)
        ddt_ref[...] = ddtraw
        ddtb_ref[...] += jnp.sum(ddtraw, axis=0, keepdims=True)
        dh_ref[...] = jnp.concatenate(dhp_parts, axis=1)
        dxa = jnp.concatenate([dxh * f["dt_x"] + f["d_x"] * dy] + db_parts + dc_parts, axis=1)
        sig, conv = f["sig"], f["conv"]
        dconv = dxa * sig * (1.0 + conv * (1.0 - sig))
        dcb_ref[...] += jnp.sum(dconv, axis=0, keepdims=True)
        rows = [jnp.sum(dconv * f["sh"][3 - k], axis=0, keepdims=True) for k in range(SSD_CONV)]
        dcw_ref[...] += jnp.concatenate(rows + [jnp.zeros((8 - SSD_CONV, D_XBC), F32)], axis=0)
        de = jnp.concatenate([dconv, dnext_ref[...]], axis=0)
        dxbc = cw[3:4] * dconv
        for s in (1, 2, 3):
            dxbc = dxbc + cw[3 - s:4 - s] * pltpu.roll(de, L + 8 - s, 0)[:L]
        dp_ref[:, D_S5 + D_SSD:] = dxbc.astype(BF16)
        dnext_ref[...] = dconv[:8]

    rev = lambda i: nc - 1 - i
    acc = lambda shape: pl.BlockSpec(shape, lambda i: (0, 0))
    tri = pl.BlockSpec((L, L), lambda i: (0, 0))
    return _call(
        body, grid=(nc,),
        in_specs=[pl.BlockSpec((L, D_SSD), lambda i: (rev(i), 1)), pl.BlockSpec((L, D_XBC), lambda i: (rev(i), 1)),
                  pl.BlockSpec((8, D_XBC), lambda i: (jnp.maximum(rev(i) * (L // 8) - 1, 0), 1)),
                  pl.BlockSpec((L, 128), lambda i: (rev(i), 0)),
                  pl.BlockSpec((1, SSD_STATE, D_SSD), lambda i: (rev(i), 0, 0)),
                  pl.BlockSpec((L, D_SSD), lambda i: (rev(i), 1))] + _ssd_param_specs()
                 + [tri, tri, tri, pl.BlockSpec((D_SSD, 128), lambda i: (0, 0))],
        out_specs=[pl.BlockSpec((L, D_MAIN), lambda i: (rev(i), 0)), pl.BlockSpec((L, 128), lambda i: (rev(i), 0)),
                   acc((8, D_XBC)), acc((1, D_XBC)), acc((1, 128)), acc((1, 128)), acc((1, 128)), acc((1, D_SSD))],
        out_shape=[jax.ShapeDtypeStruct((T, D_MAIN), BF16),
                   jax.ShapeDtypeStruct((T, 128), F32), jax.ShapeDtypeStruct((8, D_XBC), F32),
                   jax.ShapeDtypeStruct((1, D_XBC), F32), jax.ShapeDtypeStruct((1, 128), F32),
                   jax.ShapeDtypeStruct((1, 128), F32), jax.ShapeDtypeStruct((1, 128), F32),
                   jax.ShapeDtypeStruct((1, D_SSD), F32)],
        scratch_shapes=[pltpu.VMEM((SSD_STATE, D_SSD), F32), pltpu.VMEM((8, D_XBC), F32)],
        sem=("arbitrary",), name=name, comm=comm,
        args=(proj, proj, proj, dtraw, hs, dymix, *params, tril, triu, trils, headsum))


def _s5_blockdiag(v, rows_per_group, cols_per_group):
    eye = jnp.eye(S5_SEG, dtype=v.dtype)
    w = v[:, :, :, None, :] * eye[None, :, None, :, None]
    return w.reshape(S5_CH, 8 * rows_per_group, 8 * cols_per_group)


def _s5_blockdiag_extract(w, rows_per_group, cols_per_group):
    eye = jnp.eye(S5_SEG, dtype=w.dtype)
    w5 = w.reshape(S5_CH, 8, rows_per_group, 8, cols_per_group)
    return jnp.sum(w5 * eye[None, :, None, :, None], axis=3)


TM = 512

OFF = dict(mlp_w2=0, mlp_w1=1024, w_out=2048, s5_w_glu=2560, xa_wq=2816, xa_wk=3072, xa_wv=3328, xa_wo=3584, w_in=3840)
ROWS = dict(mlp_w2=1024, mlp_w1=1024, w_out=512, s5_w_glu=256, xa_wq=256, xa_wk=256, xa_wv=256, xa_wo=256, w_in=1028)
TAIL_OFF = 4872
TAIL_ROWS = 120
PACK_ROWS = TAIL_OFF + TAIL_ROWS
ROWS_ALL = (0, PACK_ROWS, 832, True)
ROWS_EARLY = (0, OFF["w_in"], 640, False)
ROWS_LATE = (OFF["w_in"], PACK_ROWS - OFF["w_in"], 192, True)
N_SHARD = 4
SMALL_L = ("norm_mix", "s5_a_re", "s5_a_im", "s5_log_dt", "s5_b_re", "s5_b_im", "s5_c_re", "s5_c_im", "s5_d",
           "ssd_conv_w", "ssd_conv_b", "ssd_dt_bias", "ssd_a_log", "ssd_d", "ssd_norm", "norm_xattn", "norm_mem",
           "norm_mlp")
SMALL_Q = 72
CONV_ROWS = 8


def _place():
    x, y, c = lax.axis_index("x"), lax.axis_index("y"), lax.axis_index("c")
    chips = [(1 - x, y), (x, 1 - y), (1 - x, 1 - y)]
    return x, y, c, 2 * x + y, chips, (x, y, 1 - c)


def _remote(src, dst, send_sem, recv_sem, to):
    return pltpu.make_async_remote_copy(src_ref=src, dst_ref=dst, send_sem=send_sem, recv_sem=recv_sem,
                                        device_id=to, device_id_type=MESH_T)


def _dma_sems(*counts):
    return [pltpu.SemaphoreType.DMA((n,)) for n in counts]


def place_own(wpack, shard_idx):
    tile = PACK_ROWS // 4

    def body(s_ref, w_ref, o_ref):
        o_ref[0] = w_ref[...]

    return pl.pallas_call(
        body,
        grid_spec=pltpu.PrefetchScalarGridSpec(
            num_scalar_prefetch=1, grid=(4,),
            in_specs=[pl.BlockSpec((tile, D_MODEL), lambda i, s_ref: (i, 0))],
            out_specs=pl.BlockSpec((1, tile, D_MODEL), lambda i, s_ref: (s_ref[0], i, 0))),
        out_shape=jax.ShapeDtypeStruct((N_SHARD,) + wpack.shape, wpack.dtype),
        compiler_params=_cp("parallel"), name="place_own")(shard_idx, wpack)


def _range_half(ref, rows, c):
    half = rows[1] // 2
    return pl.ds(pl.multiple_of(rows[0] + c * half, 16), half)


def gather_over_ici(wpack, placed, rows=ROWS_ALL):
    def copies(ci, co, cs):
        w_ref, (out_ref,), (send, recv) = ci[0], co, cs
        x, y, c, s, chips, sibling = _place()
        mine = _range_half(w_ref, rows, c)
        sends = [_remote(w_ref.at[mine, :], out_ref.at[s, mine, :], send.at[j], recv.at[j], (*chip, c))
                 for j, chip in enumerate(chips)]
        lands = [out_ref.at[2 * chip[0] + chip[1], mine, :] for chip in chips]
        arrivals = [_remote(land, land, send.at[j], recv.at[j], sibling) for j, land in enumerate(lands)]
        return sends, arrivals

    def start(ci, co, cs):
        for cp in copies(ci, co, cs)[0]:
            cp.start()

    def wait(ci, co, cs):
        sends, arrivals = copies(ci, co, cs)
        for cp in arrivals:
            cp.wait_recv()
        for cp in sends:
            cp.wait_send()

    return Comm("gather_ici", [wpack, placed], [jax.ShapeDtypeStruct(placed.shape, placed.dtype)], _dma_sems(3, 3),
                start, wait, aliases={1: 0})


def gather_to_sibling(landed, rows=ROWS_ALL):
    def copies(co, cs):
        (out_ref,), (send, recv) = co, cs
        x, y, c, s, chips, sibling = _place()
        slots = [2 * chip[0] + chip[1] for chip in chips]
        mine, other = _range_half(out_ref, rows, c), _range_half(out_ref, rows, 1 - c)
        sends = [_remote(out_ref.at[t, mine, :], out_ref.at[t, mine, :], send.at[j], recv.at[j], sibling)
                 for j, t in enumerate(slots)]
        arrivals = [_remote(out_ref.at[t, other, :], out_ref.at[t, other, :], send.at[j], recv.at[j], sibling)
                    for j, t in enumerate(slots)]
        return sends, arrivals

    def start(ci, co, cs):
        for cp in copies(co, cs)[0]:
            cp.start()

    def wait(ci, co, cs):
        sends, arrivals = copies(co, cs)
        for cp in arrivals:
            cp.wait_recv()
        for cp in sends:
            cp.wait_send()

    return Comm("gather_d2d", [landed], [jax.ShapeDtypeStruct(landed.shape, landed.dtype)], _dma_sems(3, 3), start, wait,
                aliases={0: 0})


def exchange_halves(gpack, rows):
    def copy(ci, co, cs):
        (g_ref,), (out_ref,), (send, recv) = ci, co, cs
        x, y, c, s, chips, sibling = _place()
        return _remote(g_ref.at[:, _range_half(g_ref, rows, 1 - c), :], out_ref, send.at[0], recv.at[0], sibling)

    return Comm("exchange", [gpack], [jax.ShapeDtypeStruct((N_SHARD, rows[1] // 2, D_MODEL), F32)], _dma_sems(1, 1),
                lambda ci, co, cs: copy(ci, co, cs).start(), lambda ci, co, cs: copy(ci, co, cs).wait())


def scatter_chips(csum):
    def copies(ci, co, cs):
        (c_ref,), (out_ref,), (send, recv) = ci, co, cs
        x, y, c, s, chips, sibling = _place()
        sends = [_remote(c_ref.at[2 * chip[0] + chip[1]], out_ref.at[s], send.at[j], recv.at[j], (*chip, c))
                 for j, chip in enumerate(chips)]
        arrivals = [_remote(c_ref.at[2 * chip[0] + chip[1]], out_ref.at[2 * chip[0] + chip[1]], send.at[j], recv.at[j],
                            (*chip, c)) for j, chip in enumerate(chips)]
        return sends, arrivals

    def start(ci, co, cs):
        for cp in copies(ci, co, cs)[0]:
            cp.start()

    def wait(ci, co, cs):
        sends, arrivals = copies(ci, co, cs)
        for cp in arrivals:
            cp.wait_recv()
        for cp in sends:
            cp.wait_send()

    return Comm("scatter", [csum], [jax.ShapeDtypeStruct(csum.shape, csum.dtype)], _dma_sems(3, 3), start, wait)


def share_reduced(gshards, smalls, l, rows):
    with_tail = rows[3]

    def copies(ci, co, cs):
        (g_ref, sm_ref), (send, recv, loc) = co, cs
        x, y, c, s, chips, sibling = _place()
        my_half = g_ref.at[l, _range_half(g_ref, rows, c), :]
        tail = g_ref.at[l, pl.ds(PACK_ROWS - TAIL_ROWS, TAIL_ROWS), :]
        big = _remote(my_half, my_half, send.at[0], recv.at[0], sibling)
        keep_tail = pltpu.make_async_copy(tail, sm_ref.at[l, s], loc.at[0])
        tails = [_remote(tail, sm_ref.at[l, s], send.at[1 + j], recv.at[1 + j], (*chip, 1)) for j, chip in enumerate(chips)]
        tails += [_remote(tail, sm_ref.at[l, s], send.at[4 + j], recv.at[1 + j], (*chip, 0)) for j, chip in enumerate(chips)]
        tails.append(_remote(tail, sm_ref.at[l, s], send.at[7], recv.at[4], sibling))
        other = g_ref.at[l, _range_half(g_ref, rows, 1 - c), :]
        big_in = _remote(other, other, send.at[0], recv.at[0], sibling)
        slots = [sm_ref.at[l, 2 * chip[0] + chip[1]] for chip in chips]
        tails_in = [_remote(slot, slot, send.at[1 + j], recv.at[1 + j], sibling) for j, slot in enumerate(slots)]
        sib_tail_in = _remote(sm_ref.at[l, s], sm_ref.at[l, s], send.at[7], recv.at[4], sibling)
        return c, big, keep_tail, tails, big_in, tails_in, sib_tail_in

    def start(ci, co, cs):
        c, big, keep_tail, tails, _, _, _ = copies(ci, co, cs)
        big.start()
        if with_tail:
            @pl.when(c == 1)
            def _():
                keep_tail.start()
                for cp in tails:
                    cp.start()

    def wait(ci, co, cs):
        c, big, keep_tail, tails, big_in, tails_in, sib_tail_in = copies(ci, co, cs)
        big_in.wait_recv()
        big.wait_send()
        if with_tail:
            for cp in tails_in:
                cp.wait_recv()

            @pl.when(c == 0)
            def _():
                sib_tail_in.wait_recv()

            @pl.when(c == 1)
            def _():
                for cp in tails:
                    cp.wait_send()
                keep_tail.wait()

    sds = lambda a: jax.ShapeDtypeStruct(a.shape, a.dtype)
    return Comm("share", [gshards, smalls], [sds(gshards), sds(smalls)], _dma_sems(8, 5, 1), start, wait,
                aliases={0: 0, 1: 1})


def _consts():
    e = np.zeros((S5_STATE, S5_STATE * S5_GROUP), np.float32)
    for p in range(S5_STATE):
        e[p, p * S5_GROUP:(p + 1) * S5_GROUP] = 1.0
    hs = np.zeros((D_SSD, 128), np.float32)
    for h in range(SSD_HEADS):
        hs[h * SSD_HEADDIM:(h + 1) * SSD_HEADDIM, h] = 1.0
    ones = np.ones((SSD_L, SSD_L), np.float32)
    return dict(expand=jnp.asarray(e), expand_t=jnp.asarray(e.T), headsum=jnp.asarray(hs).astype(BF16),
                tril=jnp.asarray(np.tril(ones)), triu=jnp.asarray(np.triu(ones)), trils=jnp.asarray(np.tril(ones, -1)))


def _s5_mats(w, cst):
    b_re = w["s5_b_re"].reshape(S5_GROUPS, S5_STATE * S5_GROUP)
    b_im = w["s5_b_im"].reshape(S5_GROUPS, S5_STATE * S5_GROUP)
    abr, abi, apr, api, bbr, bbi = s5_params_fwd(w["s5_a_re"], w["s5_a_im"], w["s5_log_dt"].reshape(S5_GROUPS, 1),
                                                 b_re, b_im, cst["expand"], name="s5_params_fwd")
    t = lambda v: v.reshape(S5_CH, 8, S5_STATE, S5_GROUP).transpose(0, 1, 3, 2)
    c4 = lambda v: v.reshape(S5_CH, 8, S5_GROUP, S5_STATE).transpose(0, 1, 3, 2)
    vec = lambda v: v.reshape(S5_CH, 1, S5_NST)
    return dict(bbr=_s5_blockdiag(t(bbr), S5_GROUP, S5_STATE).astype(BF16),
                bbi=_s5_blockdiag(t(bbi), S5_GROUP, S5_STATE).astype(BF16),
                ccr=_s5_blockdiag(c4(w["s5_c_re"]), S5_STATE, S5_GROUP).astype(BF16),
                cci=_s5_blockdiag(c4(w["s5_c_im"]), S5_STATE, S5_GROUP).astype(BF16),
                abr=vec(abr), abi=vec(abi), apr=vec(apr), api=vec(api),
                dsk=w["s5_d"].reshape(S5_CH, 1, S5_CW), b_re=b_re, b_im=b_im)


def _layer_fwd(x, mem, w, cst, next_pack=None, rest=None):
    sv = {}
    g = lambda n: w[n].reshape(1, -1)
    res = norm_matmul(x, g("norm_mix"), w["w_in_t"], w["w_dt_t"], tm=TM, tn=1024, name="in_proj", w_transposed=True,
                      comm=gather_over_ici(rest[0], rest[1], ROWS_EARLY) if rest else None)
    (proj, h0, r0, dtraw), staged = res if rest else (res, None)
    s5m = _s5_mats(w, cst)
    comm = _combine(gather_to_sibling(staged[0], ROWS_EARLY) if rest else None,
                    gather_over_ici(*next_pack) if next_pack else None)
    (ys, csr, csi), landed = s5_fwd(proj, s5m["bbr"], s5m["bbi"], s5m["ccr"], s5m["cci"], s5m["abr"], s5m["abi"],
                                    s5m["apr"], s5m["api"], s5m["dsk"], name="s5_fwd", comm=comm)
    if rest:
        w = dict(w, **_pack_weights(landed[0]))
        landed = landed[1:]
    pack = w["pack"]
    (ymix, tglu), gathered = glu_fwd(ys, w["s5_w_glu"], tm=TM, name="glu_fwd",
                                     comm=None if next_pack is None else gather_to_sibling(landed[0]))
    ssdp = _ssd_params(w["ssd_conv_w"], w["ssd_conv_b"], w["ssd_dt_bias"], w["ssd_a_log"], w["ssd_d"], w["ssd_norm"])
    (ymix, hs), _ = ssd_fwd(proj, dtraw, ssdp, cst["tril"], ymix, name="ssd_fwd")
    x1 = matmul_res(ymix, pack, x, tm=TM, tn=1024, tk=1024, name="out_proj",
                    wspec=pl.BlockSpec((2, 512, 1024), lambda i, j, k: (k, OFF["w_out"] // 512, j)))
    q, h1, r1 = norm_matmul(x1, g("norm_xattn"), w["xa_wq"], tm=TM, tn=1024, name="q_proj")
    kv, hm, rm = norm_matmul(mem, g("norm_mem"), w["xa_wkv"], tm=mem.shape[0], tn=1024, name="kv_proj")
    o = attn_fwd(q, kv, tm=TM, name="attn_fwd")
    x2 = matmul_res(o, w["xa_wo"], x1, tm=TM, tn=1024, tk=1024, name="attn_out")
    f1, h2, r2 = norm_matmul(x2, g("norm_mlp"), pack, tm=TM, tn=1024, name="mlp_up", n_out=D_FF,
                             wspec=pl.BlockSpec((1, D_MODEL, 1024), lambda i, j: (j, OFF["mlp_w1"] // 1024, 0)))
    x3 = matmul_res(f1, pack, x2, act="relu2", tm=TM, tn=1024, tk=1024, name="mlp_down",
                    wspec=pl.BlockSpec((1, 1024, 1024), lambda i, j, k: (k, OFF["mlp_w2"] // 1024, j)))
    sv.update(x=x, proj=proj, h0=h0, r0=r0, dtraw=dtraw, s5m=s5m, ys=ys, csr=csr, csi=csi, tglu=tglu, ssdp=ssdp,
              hs=hs, ymix=ymix, x1=x1, q=q, h1=h1, r1=r1, kv=kv, hm=hm, rm=rm, o=o, x2=x2, f1=f1, h2=h2, r2=r2)
    return x3, sv, (gathered[0] if next_pack is not None else None), w


def _layer_bwd(dx3, mem, w, sv, cst, extra_small=None, reduce_hooks=None, early=None):
    gr = {}
    g = lambda n: w[n].reshape(1, -1)
    pack = w["pack"]
    pshape = (N_SHARD, PACK_ROWS, D_MODEL)
    ps = lambda rows, f: pl.BlockSpec((1, rows, 1024), f)
    ps4 = lambda rows, f: pl.BlockSpec((N_SHARD, rows, 1024), f)
    dh1 = matmul_nt(dx3, pack, epi="relu2bwd", epi_args=(sv["f1"],), tm=TM, tko=1024, tn=1024, name="mlp_down_dx",
                    wspec=ps(1024, lambda i, k, n: (k, OFF["mlp_w2"] // 1024, n)), k_out=D_FF,
                    comm=reduce_hooks.exchange() if reduce_hooks else None)
    if reduce_hooks:
        dh1, recv = dh1
        reduce_hooks.after_exchange(recv[0])
    gp = matmul_tn(sv["f1"], dx3, act="relu2", tk=1024, tn=1024, tt=TM, name="mlp_down_dw", pack_shape=pshape,
                   pack_spec=ps(1024, lambda k, n, t: (k, OFF["mlp_w2"] // 1024, 0)))
    gp = matmul_tn(sv["h2"], dh1, tk=1024, tn=1024, tt=TM, name="mlp_up_dw", pack=gp, pack_shape=pshape,
                   pack_spec=ps(1024, lambda k, n, t: (n, OFF["mlp_w1"] // 1024, 0)))
    dx2, gr["norm_mlp"] = matmul_nt(dh1, pack, epi="rmsbwd", epi_args=(sv["x2"], sv["r2"], g("norm_mlp"), dx3),
                                    tm=TM, tko=1024, tn=1024, name="mlp_up_dx", k_out=D_MODEL,
                                    wspec=ps(1024, lambda i, k, n: (n, OFF["mlp_w1"] // 1024, 0)))
    do = matmul_nt(dx2, w["xa_wo"], tm=TM, tko=1024, tn=1024, name="attn_out_dx")
    gp = matmul_tn(sv["o"], dx2, tk=1024, tn=1024, tt=TM, name="attn_out_dw", pack=gp, pack_shape=pshape,
                   pack_spec=ps4(256, lambda k, n, t: (0, OFF["xa_wo"] // 256, 0)))
    dq, dkv = attn_bwd(sv["q"], sv["kv"], do, tm=TM, name="attn_bwd")
    gp = matmul_tn(sv["h1"], dq, tk=1024, tn=1024, tt=TM, name="q_proj_dw", pack=gp, pack_shape=pshape,
                   pack_spec=ps4(256, lambda k, n, t: (0, OFF["xa_wq"] // 256, 0)))
    dx1, gr["norm_xattn"] = matmul_nt(dq, w["xa_wq"], epi="rmsbwd", epi_args=(sv["x1"], sv["r1"], g("norm_xattn"), dx2),
                                      tm=TM, tko=1024, tn=1024, name="q_proj_dx")
    M = mem.shape[0]
    gp = matmul_tn(sv["hm"], dkv, tk=1024, tn=1024, tt=M, name="kv_proj_dw", pack=gp, pack_shape=pshape,
                   pack_spec=ps4(256, lambda k, n, t: (0, OFF["xa_wk"] // 256 + n, 0)))
    _, gr["norm_mem"] = matmul_nt(dkv, w["xa_wkv"], epi="rmsbwd",
                                  epi_args=(mem, sv["rm"], g("norm_mem"), jnp.zeros_like(mem)),
                                  tm=M, tko=1024, tn=1024, name="kv_proj_dx")
    dymix = matmul_nt(dx1, pack, tm=TM, tko=1024, tn=1024, name="out_proj_dx", k_out=2 * D_MODEL,
                      wspec=pl.BlockSpec((2, 512, 1024), lambda i, k, n: (k, OFF["w_out"] // 512, n)))
    gp = matmul_tn(sv["ymix"], dx1, tk=2048, tn=1024, tt=TM, name="out_proj_dw", pack=gp, pack_shape=pshape,
                   pack_spec=ps4(512, lambda k, n, t: (0, OFF["w_out"] // 512, 0)))
    (dproj, ddtraw, dcw, dcb, ddtb, dal, ddsk, dnw), parts = ssd_bwd(
        sv["proj"], sv["dtraw"], sv["hs"], dymix, sv["ssdp"], cst["tril"], cst["triu"], cst["trils"], cst["headsum"],
        name="ssd_bwd", comm=reduce_hooks.scatter() if reduce_hooks else None)
    if reduce_hooks:
        reduce_hooks.after_scatter(parts[0])
    gr["ssd_conv_w"], gr["ssd_conv_b"] = dcw[:SSD_CONV], dcb[0]
    gr["ssd_dt_bias"], gr["ssd_a_log"], gr["ssd_d"] = ddtb[0, :SSD_HEADS], dal[0, :SSD_HEADS], ddsk[0, :SSD_HEADS]
    gr["ssd_norm"] = dnw[0]
    gp = glu_dw(dymix, sv["ys"], sv["tglu"], gp, tt=TM, name="glu_dw",
                pack_spec=ps4(256, lambda i: (0, OFF["s5_w_glu"] // 256, 0)))
    first = _Reduction(gp, early[0], early[1], None, None, ROWS_EARLY) if early else None
    comm = _combine(reduce_hooks.share() if reduce_hooks else None, first.exchange() if first else None)
    (dys,), outs = glu_bwd(dymix, sv["ys"], sv["tglu"], w["s5_w_glu"], tm=TM, name="glu_dx", comm=comm)
    if comm is not None:
        if reduce_hooks:
            reduce_hooks.after_share(outs[:2])
            outs = outs[2:]
        if first:
            first.gshards, first.smalls = reduce_hooks.gshards, reduce_hooks.smalls
            first.after_exchange(outs[0])
    s5m = sv["s5m"]
    (dproj, dbbr, dbbi, dccr, dcci, dd, dabr, dabi), parts = s5_bwd(
        sv["proj"], dys, sv["csr"], sv["csi"], s5m["bbr"], s5m["bbi"], s5m["ccr"], s5m["cci"], s5m["abr"], s5m["abi"],
        s5m["apr"], s5m["api"], s5m["dsk"], dproj, name="s5_bwd", comm=first.scatter() if first else None)
    if first:
        first.after_scatter(parts[0])
    tb = lambda v: _s5_blockdiag_extract(v, S5_GROUP, S5_STATE).transpose(0, 1, 3, 2).reshape(S5_GROUPS, -1)
    tc = lambda v: _s5_blockdiag_extract(v, S5_STATE, S5_GROUP).transpose(0, 1, 3, 2).reshape(S5_GROUPS, S5_GROUP, S5_STATE)
    gr["s5_c_re"], gr["s5_c_im"] = tc(dccr), tc(dcci)
    gr["s5_d"] = dd.reshape(S5_GROUPS, S5_GROUP)
    dar, dai, dld, dbr, dbi = s5_params_bwd(
        w["s5_a_re"], w["s5_a_im"], w["s5_log_dt"].reshape(S5_GROUPS, 1), s5m["b_re"], s5m["b_im"],
        dabr.reshape(S5_GROUPS, S5_STATE), dabi.reshape(S5_GROUPS, S5_STATE), tb(dbbr), tb(dbbi),
        cst["expand"], cst["expand_t"], name="s5_params_bwd")
    gr["s5_a_re"], gr["s5_a_im"], gr["s5_log_dt"] = dar, dai, dld[:, 0]
    gr["s5_b_re"] = dbr.reshape(S5_GROUPS, S5_STATE, S5_GROUP)
    gr["s5_b_im"] = dbi.reshape(S5_GROUPS, S5_STATE, S5_GROUP)
    wt_shape = (D_MAIN + D_DT_PAD, D_MODEL)
    dwt = matmul_tn(dproj, sv["h0"], tk=1024, tn=1024, tt=TM, name="in_proj_dw", pack_shape=wt_shape,
                    pack_spec=pl.BlockSpec((1024, 1024), lambda k, n, t: (k, 0)))
    dwt = matmul_tn(ddtraw, sv["h0"], tk=D_DT_PAD, tn=1024, tt=TM, name="in_proj_dt_dw", pack=dwt, pack_shape=wt_shape,
                    pack_spec=pl.BlockSpec((D_DT_PAD, 1024), lambda k, n, t: (D_MAIN // D_DT_PAD, 0)))
    dx0 = matmul_nt(dproj, w["w_in_t"], g2=ddtraw, w2=w["w_dt_t"], w_is_nk=True, epi="rmsbwd",
                    epi_args=(sv["x"], sv["r0"], g("norm_mix"), dx1), tm=TM, tko=1024, tn=1024, name="in_proj_dx",
                    comm=first.share() if first else None)
    if first:
        dx0, shared = dx0
        first.after_share(shared)
    dx0, gr["norm_mix"] = dx0
    gr = {k: (v[0] if k.startswith("norm_") else v) for k, v in gr.items()}
    for t in range(N_SHARD):
        shard_rows = lax.slice_in_dim(dwt, t * ROWS["w_in"], (t + 1) * ROWS["w_in"], axis=0)
        gp = lax.dynamic_update_slice(gp, shard_rows[None], (t, OFF["w_in"], 0))
    small = [gr[n].reshape(-1) for n in SMALL_L] + ([] if extra_small is None else [extra_small.reshape(-1)])
    small = jnp.concatenate(small)
    small = jnp.pad(small, (0, N_SHARD * SMALL_Q * D_MODEL - small.size)).reshape(N_SHARD, SMALL_Q, D_MODEL)
    gap = TAIL_OFF - OFF["w_in"] - ROWS["w_in"]
    gp = lax.dynamic_update_slice(gp, jnp.pad(small, ((0, 0), (gap, TAIL_ROWS - SMALL_Q), (0, 0))),
                                  (0, TAIL_OFF - gap, 0))
    return dx0, gp, first


def _local_step(x, mem, target, layers, norm_final):
    cst = _consts()
    saved = []
    for l in range(DEPTH):
        x, sv, _, _ = _layer_fwd(x, mem, layers[l], cst)
        saved.append(sv)
    loss, dx, dgf = loss_head(x, norm_final.reshape(1, -1), target, tm=TM, name="loss_head")
    packs = [None] * DEPTH
    for l in reversed(range(DEPTH)):
        dx, packs[l], _ = _layer_bwd(dx, mem, layers[l], saved[l], cst, extra_small=dgf[0] if l == DEPTH - 1 else None)
    return loss, dx, packs


def sum_halves(gpack, recv, c_idx, rows):
    first, count, tile, _ = rows
    half = count // 2
    nb = half // tile

    def body(c_ref, a_ref, b_ref, o_ref):
        o_ref[...] = (a_ref[...] + b_ref[...]).astype(BF16)

    blk = (1, tile, D_MODEL)
    return pl.pallas_call(
        body,
        grid_spec=pltpu.PrefetchScalarGridSpec(
            num_scalar_prefetch=1, grid=(N_SHARD, nb),
            in_specs=[pl.BlockSpec(blk, lambda t, i, c_ref: (t, first // tile + c_ref[0] * nb + i, 0)),
                      pl.BlockSpec(blk, lambda t, i, c_ref: (t, i, 0))],
            out_specs=pl.BlockSpec(blk, lambda t, i, c_ref: (t, i, 0))),
        out_shape=jax.ShapeDtypeStruct((N_SHARD, half, D_MODEL), BF16),
        compiler_params=_cp("parallel", "parallel"), name="sum_halves")(c_idx, gpack, recv)


def sum_chips(parts, csum, gshards, l, place_idx, rows):
    first, count, tile, _ = rows
    nb = count // 2 // tile

    def body(pi_ref, p0, p1, p2, p3, own, g_ref, o_ref):
        s = pi_ref[0]
        vals = [jnp.where(s == k, own[0], p[0]).astype(F32) for k, p in enumerate((p0, p1, p2, p3))]
        o_ref[0] = ((vals[0] + vals[1]) + vals[2]) + vals[3]

    blk = (1, tile, D_MODEL)
    part_spec = lambda k: pl.BlockSpec(blk, lambda i, pi_ref: (jnp.where(pi_ref[0] == k, (k + 1) % N_SHARD, k), i, 0))
    return pl.pallas_call(
        body,
        grid_spec=pltpu.PrefetchScalarGridSpec(
            num_scalar_prefetch=1, grid=(nb,),
            in_specs=[part_spec(k) for k in range(N_SHARD)]
                     + [pl.BlockSpec(blk, lambda i, pi_ref: (pi_ref[0], i, 0)), _ANY],
            out_specs=pl.BlockSpec(blk, lambda i, pi_ref: (l, first // tile + pi_ref[1] * nb + i, 0))),
        out_shape=jax.ShapeDtypeStruct(gshards.shape, F32), input_output_aliases={6: 0},
        compiler_params=_cp("parallel"), name="sum_chips")(place_idx, parts, parts, parts, parts, csum, gshards)


class _Reduction:
    def __init__(self, gpack, layer, place_idx, gshards, smalls, rows=ROWS_ALL):
        self.gpack, self.layer, self.place_idx, self.gshards, self.smalls = gpack, layer, place_idx, gshards, smalls
        self.rows = rows

    def exchange(self):
        return exchange_halves(self.gpack, self.rows)

    def after_exchange(self, recv):
        self.csum = sum_halves(self.gpack, recv, self.place_idx[1:], self.rows)

    def scatter(self):
        return scatter_chips(self.csum)

    def after_scatter(self, parts):
        self.gshards = sum_chips(parts, self.csum, self.gshards, self.layer, self.place_idx, self.rows)

    def share(self):
        return share_reduced(self.gshards, self.smalls, self.layer, self.rows)

    def after_share(self, shared):
        self.gshards, self.smalls = shared

    def run_alone(self):
        self.after_exchange(_comm_only(self.exchange())[0])
        self.after_scatter(_comm_only(self.scatter())[0])
        self.after_share(_comm_only(self.share()))
        return self.gshards, self.smalls


def adamw(w, g, m, v, *, name):
    shape = w.shape
    cols = shape[-1]
    rows = w.size // cols
    tr = 512 if rows % 512 == 0 else rows
    c1 = 1.0 / (1.0 - ADAM_B1 ** ADAM_STEP)
    c2 = 1.0 / (1.0 - ADAM_B2 ** ADAM_STEP)

    def body(w_ref, g_ref, m_ref, v_ref, d_ref, nm_ref, nv_ref):
        gv = g_ref[...]
        nm = ADAM_B1 * m_ref[...] + (1.0 - ADAM_B1) * gv
        nv = ADAM_B2 * v_ref[...] + (1.0 - ADAM_B2) * (gv * gv)
        d_ref[...] = -ADAM_LR * ((nm * c1) / (jnp.sqrt(nv * c2) + ADAM_EPS) + ADAM_WD * w_ref[...])
        nm_ref[...] = nm
        nv_ref[...] = nv

    spec = pl.BlockSpec((tr, cols), lambda i: (i, 0))
    sds = jax.ShapeDtypeStruct((rows, cols), F32)
    outs = pl.pallas_call(body, grid=(rows // tr,), in_specs=[spec] * 4, out_specs=[spec] * 3, out_shape=[sds] * 3,
                          compiler_params=_cp("parallel"), name=name)(
                              *[a.reshape(rows, cols) for a in (w, g, m, v)])
    return [o.reshape(shape) for o in outs]


def _own_pack(wts, l):
    rows = [wts[n][l].T if n == "w_in" else wts[n][l].reshape(ROWS[n], D_MODEL) for n in sorted(OFF, key=OFF.get)]
    cw = wts["ssd_conv_w"][l].reshape(-1)
    hi = lax.reduce_precision(cw, 8, 7)
    mid = lax.reduce_precision(cw - hi, 8, 7)
    lo = lax.reduce_precision(cw - hi - mid, 8, 7)
    conv = jnp.pad(jnp.concatenate([hi, mid, lo]), (0, CONV_ROWS * D_MODEL - 3 * cw.size)).reshape(CONV_ROWS, D_MODEL)
    gap = jnp.zeros((TAIL_OFF - OFF["w_in"] - ROWS["w_in"], D_MODEL), F32)
    rest = jnp.zeros((TAIL_ROWS - CONV_ROWS, D_MODEL), F32)
    return jnp.concatenate(rows + [gap, conv, rest], axis=0).astype(BF16)


def _square(gathered, n):
    return gathered[:, OFF[n]:OFF[n] + ROWS[n]].reshape(N_SHARD * ROWS[n], D_MODEL)


def _pack_weights(gathered):
    w = {"pack": gathered}
    w["s5_w_glu"], w["xa_wq"], w["xa_wo"] = _square(gathered, "s5_w_glu"), _square(gathered, "xa_wq"), _square(gathered, "xa_wo")
    w["xa_wkv"] = jnp.concatenate([_square(gathered, "xa_wk"), _square(gathered, "xa_wv")], axis=1)
    return w


def _layer_weights(gathered, wts, l, late_only=False):
    w = {n: wts[n][l] for n in SMALL_L if n != "ssd_conv_w"}
    if not late_only:
        w.update(_pack_weights(gathered))
    w_in_t = _square(gathered, "w_in")
    w["w_in_t"] = w_in_t[:D_MAIN]
    w["w_dt_t"] = jnp.pad(w_in_t[D_MAIN:], ((0, D_DT_PAD - SSD_HEADS), (0, 0)))
    per = SSD_CONV * D_XBC // N_SHARD
    cw = gathered[:, TAIL_OFF:TAIL_OFF + CONV_ROWS].astype(F32).reshape(N_SHARD, -1)[:, :3 * per]
    cw = cw.reshape(N_SHARD, 3, SSD_CONV, D_XBC // N_SHARD)
    cw = (cw[:, 0] + cw[:, 1]) + cw[:, 2]
    w["ssd_conv_w"] = cw.transpose(1, 0, 2).reshape(SSD_CONV, D_XBC)
    return w


def kernel(x, mem, norm_mix, w_in, s5_a_re, s5_a_im, s5_log_dt, s5_b_re, s5_b_im, s5_c_re, s5_c_im, s5_d, s5_w_glu, ssd_conv_w, ssd_conv_b, ssd_dt_bias, ssd_a_log, ssd_d, ssd_norm, w_out, norm_xattn, norm_mem, xa_wq, xa_wk, xa_wv, xa_wo, norm_mlp, mlp_w1, mlp_w2, norm_final, loss_target, m_norm_mix, m_w_in, m_s5_a_re, m_s5_a_im, m_s5_log_dt, m_s5_b_re, m_s5_b_im, m_s5_c_re, m_s5_c_im, m_s5_d, m_s5_w_glu, m_ssd_conv_w, m_ssd_conv_b, m_ssd_dt_bias, m_ssd_a_log, m_ssd_d, m_ssd_norm, m_w_out, m_norm_xattn, m_norm_mem, m_xa_wq, m_xa_wk, m_xa_wv, m_xa_wo, m_norm_mlp, m_mlp_w1, m_mlp_w2, m_norm_final, v_norm_mix, v_w_in, v_s5_a_re, v_s5_a_im, v_s5_log_dt, v_s5_b_re, v_s5_b_im, v_s5_c_re, v_s5_c_im, v_s5_d, v_s5_w_glu, v_ssd_conv_w, v_ssd_conv_b, v_ssd_dt_bias, v_ssd_a_log, v_ssd_d, v_ssd_norm, v_w_out, v_norm_xattn, v_norm_mem, v_xa_wq, v_xa_wk, v_xa_wv, v_xa_wo, v_norm_mlp, v_mlp_w1, v_mlp_w2, v_norm_final):
    names = ("norm_mix", "w_in", "s5_a_re", "s5_a_im", "s5_log_dt", "s5_b_re", "s5_b_im", "s5_c_re", "s5_c_im", "s5_d",
             "s5_w_glu", "ssd_conv_w", "ssd_conv_b", "ssd_dt_bias", "ssd_a_log", "ssd_d", "ssd_norm", "w_out",
             "norm_xattn", "norm_mem", "xa_wq", "xa_wk", "xa_wv", "xa_wo", "norm_mlp", "mlp_w1", "mlp_w2", "norm_final")
    loc = locals()
    wts = {n: loc[n] for n in names}
    mom = {n: loc["m_" + n] for n in names}
    var = {n: loc["v_" + n] for n in names}
    shard = 2 * lax.axis_index("x") + lax.axis_index("y")
    core = lax.axis_index("c")

    cst = _consts()
    place_idx = jnp.stack([shard, core]).astype(jnp.int32)
    h, mem0 = x[0], mem[0]

    own0 = _own_pack(wts, 0)
    staged = _comm_only(gather_over_ici(own0, place_own(own0, place_idx[:1]), ROWS_LATE))[0]
    gathered = _comm_only(gather_to_sibling(staged, ROWS_LATE))[0]
    layers, saved = [], []
    for l in range(DEPTH):
        nxt = None
        if l + 1 < DEPTH:
            own = _own_pack(wts, l + 1)
            nxt = (own, place_own(own, place_idx[:1]))
        h, sv, gathered, w = _layer_fwd(h, mem0, _layer_weights(gathered, wts, l, late_only=l == 0), cst, next_pack=nxt,
                                        rest=(own0, gathered) if l == 0 else None)
        layers.append(w)
        saved.append(sv)
    loss, dx, dgf = loss_head(h, norm_final.reshape(1, -1), loss_target[0], tm=TM, name="loss_head")

    gshards = jnp.zeros((DEPTH, PACK_ROWS, D_MODEL), F32)
    smalls = jnp.zeros((DEPTH, N_SHARD, TAIL_ROWS, D_MODEL), F32)
    pending = None
    for l in reversed(range(DEPTH)):
        dx, gpack, first = _layer_bwd(dx, mem0, layers[l], saved[l], cst, extra_small=dgf[0] if l == DEPTH - 1 else None,
                                      reduce_hooks=pending, early=(l, place_idx) if l == 0 else None)
        if pending is not None:
            gshards, smalls = pending.gshards, pending.smalls
        pending = _Reduction(gpack, l, place_idx, gshards, smalls)
    gshards, smalls = _Reduction(gpack, 0, place_idx, first.gshards, first.smalls, ROWS_LATE).run_alone()

    g = {n: gshards[:, OFF[n]:OFF[n] + ROWS[n]].reshape(wts[n].shape) for n in OFF if n != "w_in"}
    g["w_in"] = gshards[:, OFF["w_in"]:OFF["w_in"] + ROWS["w_in"]].transpose(0, 2, 1)
    small_red = smalls[:, :, :SMALL_Q].reshape(DEPTH, -1)
    off = 0
    for n in SMALL_L:
        shape = (SSD_CONV, D_XBC) if n == "ssd_conv_w" else wts[n].shape[1:]
        size = math.prod(shape)
        g[n] = small_red[:, off:off + size].reshape((DEPTH,) + shape)
        off += size
    g["norm_final"] = small_red[DEPTH - 1, off:off + D_MODEL]
    g["ssd_conv_w"] = lax.dynamic_slice_in_dim(g["ssd_conv_w"], shard * (D_XBC // N_SHARD), D_XBC // N_SHARD, axis=2)

    deltas, new_m, new_v = {}, {}, {}
    for n in names:
        deltas[n], new_m[n], new_v[n] = adamw(wts[n], g[n], mom[n], var[n], name="adamw_" + n)
    loss_all = lax.psum(loss[0, 0], ("x", "y", "c"))
    return (loss_all, dx[None], *[g[n] for n in names], *[deltas[n] for n in names], *[new_m[n] for n in names],
            *[new_v[n] for n in names])
```

```python
import functools
import math

import jax
import jax.numpy as jnp
import numpy as np
from jax import lax
from jax.experimental import pallas as pl
from jax.experimental.pallas import tpu as pltpu

F32 = jnp.float32
BF16 = jnp.bfloat16
HIGHEST = lax.Precision.HIGHEST

D_MODEL = 1024
DEPTH = 4
D_S5 = 1024
D_SSD = 1024
S5_GROUP = 16
S5_GROUPS = 64
S5_STATE = 64
SSD_HEADS = 16
SSD_HEADDIM = 64
SSD_NGROUPS = 4
SSD_STATE = 128
SSD_CONV = 4
SSD_CHUNK = 128
D_XBC = 2048
D_MAIN = 4096
D_DT_PAD = 128
XA_HEADS = 4
XA_HEAD_DIM = 256
D_FF = 4096
EPS = 1e-5
ADAM_LR, ADAM_B1, ADAM_B2, ADAM_EPS, ADAM_WD, ADAM_STEP = 0.001, 0.9, 0.999, 1e-08, 0.01, 10

VMEM_LIMIT = 56 * 1024 * 1024
MESH_T = pl.DeviceIdType.MESH


def _cp(*sem):
    return pltpu.CompilerParams(dimension_semantics=tuple(sem) if sem else None, vmem_limit_bytes=VMEM_LIMIT)


_ANY = pl.BlockSpec(memory_space=pl.ANY)


class Comm:
    def __init__(self, name, inputs, out_shapes, sems, start, wait, aliases=None):
        self.name, self.inputs, self.out_shapes, self.sems = name, list(inputs), list(out_shapes), list(sems)
        self.start, self.wait, self.aliases = start, wait, dict(aliases or {})


def _call(body, *, grid, in_specs, out_specs, out_shape, args, scratch_shapes=(), sem, name, comm=None, aliases=None):
    in_specs, out_specs, out_shape = list(in_specs), list(out_specs), list(out_shape)
    scratch_shapes = list(scratch_shapes)
    aliases = dict(aliases or {})
    if comm is None:
        res = pl.pallas_call(body, grid=grid, in_specs=in_specs, out_specs=out_specs, out_shape=out_shape,
                             scratch_shapes=scratch_shapes, compiler_params=_cp(*sem), name=name,
                             input_output_aliases=aliases)(*args)
        return list(res), []
    n_in, n_out, n_scr = len(in_specs), len(out_specs), len(scratch_shapes)
    c_in, c_out = len(comm.inputs), len(comm.out_shapes)

    def wrapped(*refs):
        a, refs = refs[:n_in], refs[n_in:]
        ci, refs = refs[:c_in], refs[c_in:]
        o, refs = refs[:n_out], refs[n_out:]
        co, refs = refs[:c_out], refs[c_out:]
        s, cs = refs[:n_scr], refs[n_scr:]
        first = functools.reduce(jnp.logical_and, [pl.program_id(d) == 0 for d in range(len(grid))])
        last = functools.reduce(jnp.logical_and, [pl.program_id(d) == grid[d] - 1 for d in range(len(grid))])

        @pl.when(first)
        def _():
            comm.start(ci, co, cs)

        body(*a, *o, *s)

        @pl.when(last)
        def _():
            comm.wait(ci, co, cs)

    for i, j in comm.aliases.items():
        aliases[n_in + i] = n_out + j
    res = pl.pallas_call(wrapped, grid=grid, in_specs=in_specs + [_ANY] * c_in, out_specs=out_specs + [_ANY] * c_out,
                         out_shape=out_shape + comm.out_shapes, scratch_shapes=scratch_shapes + comm.sems,
                         compiler_params=_cp(*(("arbitrary",) * len(grid))), name=name + "_" + comm.name,
                         input_output_aliases=aliases)(*args, *comm.inputs)
    return list(res[:n_out]), list(res[n_out:])


def _combine(a, b):
    if a is None or b is None:
        return a or b
    ni, no, ns = len(a.inputs), len(a.out_shapes), len(a.sems)

    def start(ci, co, cs):
        a.start(ci[:ni], co[:no], cs[:ns])
        b.start(ci[ni:], co[no:], cs[ns:])

    def wait(ci, co, cs):
        a.wait(ci[:ni], co[:no], cs[:ns])
        b.wait(ci[ni:], co[no:], cs[ns:])

    aliases = dict(a.aliases)
    aliases.update({ni + i: no + j for i, j in b.aliases.items()})
    return Comm(a.name + "_" + b.name, a.inputs + b.inputs, a.out_shapes + b.out_shapes, a.sems + b.sems, start, wait,
                aliases)


def _comm_only(comm):
    def body(*refs):
        ci, refs = refs[:len(comm.inputs)], refs[len(comm.inputs):]
        co, cs = refs[:len(comm.out_shapes)], refs[len(comm.out_shapes):]
        comm.start(ci, co, cs)
        comm.wait(ci, co, cs)

    res = pl.pallas_call(body, in_specs=[_ANY] * len(comm.inputs), out_specs=[_ANY] * len(comm.out_shapes),
                         out_shape=comm.out_shapes, scratch_shapes=comm.sems, name=comm.name,
                         input_output_aliases=comm.aliases)(*comm.inputs)
    return list(res)


def _dot(a, b):
    return jnp.dot(a, b, preferred_element_type=F32)


def _dot_nt(a, b):
    return lax.dot_general(a, b, (((1,), (1,)), ((), ())), preferred_element_type=F32)


def _dot_tn(a, b):
    return lax.dot_general(a, b, (((0,), (0,)), ((), ())), preferred_element_type=F32)


def _dot_hi(a, b):
    return jnp.dot(a, b, precision=HIGHEST, preferred_element_type=F32)


def _gelu(x):
    c = math.sqrt(2.0 / math.pi)
    return 0.5 * x * (1.0 + jnp.tanh(c * (x + 0.044715 * x * x * x)))


def _gelu_grad(x):
    c = math.sqrt(2.0 / math.pi)
    t = jnp.tanh(c * (x + 0.044715 * x * x * x))
    return 0.5 * (1.0 + t) + 0.5 * x * (1.0 - t * t) * c * (1.0 + 3 * 0.044715 * x * x)


def _sigmoid(x):
    return 1.0 / (1.0 + jnp.exp(-x))


def _act(a, act):
    if act is None:
        return a.astype(BF16)
    a = a.astype(F32)
    if act == "relu2":
        a = jnp.maximum(a, 0.0)
        return (a * a).astype(BF16)
    if act == "gelu":
        return _gelu(a).astype(BF16)
    raise ValueError(act)


def _pack_block(ref):
    return ref[...].reshape(-1, ref.shape[-1])


def norm_matmul(x, g, w, w2=None, *, tm, tn, name, wspec=None, n_out=None, w_transposed=False, comm=None):
    T, D = x.shape
    N = n_out if wspec is not None else (w.shape[0] if w_transposed else w.shape[1])
    wget = (lambda r: r[...]) if wspec is None else _pack_block
    mm = _dot_nt if w_transposed else _dot
    has2 = w2 is not None

    def body(x_ref, g_ref, w_ref, *rest):
        if has2:
            w2_ref, o_ref, h_ref, r_ref, o2_ref = rest
        else:
            o_ref, h_ref, r_ref = rest
        j = pl.program_id(1)

        @pl.when(j == 0)
        def _():
            xv = x_ref[...]
            r = lax.rsqrt(jnp.mean(xv * xv, axis=-1, keepdims=True) + EPS)
            h = (xv * r * g_ref[...]).astype(BF16)
            h_ref[...] = h
            r_ref[...] = r
            if has2:
                o2_ref[...] = mm(h, w2_ref[...])

        o_ref[...] = mm(h_ref[...], wget(w_ref)).astype(o_ref.dtype)

    if wspec is None:
        wspec = pl.BlockSpec((tn, D), lambda i, j: (j, 0)) if w_transposed else pl.BlockSpec((D, tn), lambda i, j: (0, j))
    in_specs = [pl.BlockSpec((tm, D), lambda i, j: (i, 0)), pl.BlockSpec((1, D), lambda i, j: (0, 0)), wspec]
    out_shape = [jax.ShapeDtypeStruct((T, N), BF16), jax.ShapeDtypeStruct((T, D), BF16),
                 jax.ShapeDtypeStruct((T, 1), F32)]
    out_specs = [pl.BlockSpec((tm, tn), lambda i, j: (i, j)), pl.BlockSpec((tm, D), lambda i, j: (i, 0)),
                 pl.BlockSpec((tm, 1), lambda i, j: (i, 0))]
    args = [x, g, w]
    if has2:
        in_specs.append(pl.BlockSpec(w2.shape, lambda i, j: (0, 0)))
        out_shape.append(jax.ShapeDtypeStruct((T, D_DT_PAD), F32))
        out_specs.append(pl.BlockSpec((tm, D_DT_PAD), lambda i, j: (i, 0)))
        args.append(w2)
    res, cres = _call(body, grid=(T // tm, N // tn), in_specs=in_specs, out_specs=out_specs, out_shape=out_shape,
                      sem=("parallel", "arbitrary"), name=name, args=args, comm=comm)
    return res if comm is None else (res, cres)


def matmul_res(a, w, r, *, act=None, tm, tn, tk, name, wspec=None):
    T, K = a.shape
    N = r.shape[1]
    wget = (lambda r_: r_[...]) if wspec is None else _pack_block
    nk = K // tk

    def body(a_ref, w_ref, r_ref, o_ref):
        k = pl.program_id(2)

        @pl.when(k == 0)
        def _():
            o_ref[...] = r_ref[...]

        o_ref[...] += _dot(_act(a_ref[...], act), wget(w_ref))

    return pl.pallas_call(
        body, grid=(T // tm, N // tn, nk),
        in_specs=[pl.BlockSpec((tm, tk), lambda i, j, k: (i, k)),
                  pl.BlockSpec((tk, tn), lambda i, j, k: (k, j)) if wspec is None else wspec,
                  pl.BlockSpec((tm, tn), lambda i, j, k: (i, j))],
        out_specs=pl.BlockSpec((tm, tn), lambda i, j, k: (i, j)),
        out_shape=jax.ShapeDtypeStruct((T, N), F32),
        compiler_params=_cp("parallel", "parallel", "arbitrary"), name=name)(a, w, r)


def glu_fwd(ys, w, *, tm, name, comm=None):
    T, N = ys.shape

    def body(y_ref, w_ref, o_ref, t_ref):
        a = _gelu(y_ref[...].astype(F32))
        t = _dot(a.astype(BF16), w_ref[...])
        o_ref[...] = (a * _sigmoid(t)).astype(BF16)
        t_ref[...] = t.astype(BF16)

    return _call(
        body, grid=(T // tm,),
        in_specs=[pl.BlockSpec((tm, N), lambda i: (i, 0)), pl.BlockSpec((N, N), lambda i: (0, 0))],
        out_specs=[pl.BlockSpec((tm, N), lambda i: (i, 0)), pl.BlockSpec((tm, N), lambda i: (i, 0))],
        out_shape=[jax.ShapeDtypeStruct((T, 2 * N), BF16), jax.ShapeDtypeStruct((T, N), BF16)],
        sem=("parallel",), name=name, args=(ys, w), comm=comm)


def glu_bwd(dymix, ys, t, w, *, tm, name, comm=None):
    T, N = ys.shape

    def body(d_ref, y_ref, t_ref, w_ref, dys_ref):
        d = d_ref[...].astype(F32)
        ysv = y_ref[...].astype(F32)
        s = _sigmoid(t_ref[...].astype(F32))
        dt = (d * _gelu(ysv) * s * (1.0 - s)).astype(BF16)
        dys_ref[...] = ((d * s + _dot_nt(dt, w_ref[...])) * _gelu_grad(ysv)).astype(BF16)

    spec = pl.BlockSpec((tm, N), lambda i: (i, 0))
    return _call(body, grid=(T // tm,), in_specs=[spec, spec, spec, pl.BlockSpec((N, N), lambda i: (0, 0))],
                 out_specs=[spec], out_shape=[jax.ShapeDtypeStruct((T, N), BF16)], sem=("parallel",), name=name,
                 args=(dymix, ys, t, w), comm=comm)


def glu_dw(dymix, ys, t, pack, *, pack_spec, tt, name):
    T, N = ys.shape

    def body(d_ref, y_ref, t_ref, pack_ref, o_ref):
        i = pl.program_id(0)
        a = _gelu(y_ref[...].astype(F32))
        s = _sigmoid(t_ref[...].astype(F32))
        dt = (d_ref[...].astype(F32) * a * s * (1.0 - s)).astype(BF16)
        part = _dot_tn(a.astype(BF16), dt).reshape(o_ref.shape)

        @pl.when(i == 0)
        def _():
            o_ref[...] = part

        @pl.when(i > 0)
        def _():
            o_ref[...] += part

    spec = pl.BlockSpec((tt, N), lambda i: (i, 0))
    return pl.pallas_call(body, grid=(T // tt,), in_specs=[spec, spec, spec, _ANY], out_specs=pack_spec,
                          out_shape=jax.ShapeDtypeStruct(pack.shape, F32), input_output_aliases={3: 0},
                          compiler_params=_cp("arbitrary"), name=name)(dymix, ys, t, pack)


def matmul_nt(g, w, *, epi=None, epi_args=(), g2=None, w2=None, tm, tko, tn, out_dtype=BF16, name, wspec=None,
              k_out=None, comm=None, w_is_nk=False):
    T, N = g.shape
    K = k_out if wspec is not None else (w.shape[1] if w_is_nk else w.shape[0])
    wget = (lambda r_: r_[...]) if wspec is None else _pack_block
    mm = _dot if w_is_nk else _dot_nt
    nn = N // tn
    has2 = g2 is not None
    rms = epi == "rmsbwd"
    if rms:
        assert tko == K
    n_epi = len(epi_args)

    def body(*refs):
        g_ref, w_ref = refs[0], refs[1]
        pos = 2
        if has2:
            g2_ref, w2_ref = refs[2], refs[3]
            pos = 4
        e_refs = refs[pos:pos + n_epi]
        pos += n_epi
        o_ref = refs[pos]
        pos += 1
        if rms:
            dg_ref = refs[pos]
            pos += 1
        acc_ref = refs[pos]
        i = pl.program_id(0)
        n = pl.program_id(2)
        part = mm(g_ref[...].astype(BF16), wget(w_ref))

        if nn > 1:
            @pl.when(n == 0)
            def _():
                acc_ref[...] = part

            @pl.when(n > 0)
            def _():
                acc_ref[...] += part

        @pl.when(n == nn - 1)
        def _():
            acc = acc_ref[...] if nn > 1 else part
            if has2:
                acc = acc + mm(g2_ref[...].astype(BF16), w2_ref[...])
            if epi is None:
                o_ref[...] = acc.astype(o_ref.dtype)
            elif epi == "relu2bwd":
                h1 = e_refs[0][...].astype(F32)
                o_ref[...] = (acc * 2.0 * jnp.maximum(h1, 0.0)).astype(o_ref.dtype)
            elif epi == "glubwd":
                da1 = e_refs[0][...].astype(F32)
                ys = e_refs[1][...].astype(F32)
                o_ref[...] = ((da1 + acc) * _gelu_grad(ys)).astype(o_ref.dtype)
            elif epi == "rmsbwd":
                xv, rs, gain, rv = e_refs[0][...], e_refs[1][...], e_refs[2][...], e_refs[3][...]
                xhat = xv * rs
                gd = acc * gain
                o_ref[...] = rv + rs * (gd - xhat * jnp.mean(gd * xhat, axis=-1, keepdims=True))
                part_g = jnp.sum(acc * xhat, axis=0, keepdims=True)

                @pl.when(i == 0)
                def _():
                    dg_ref[...] = part_g

                @pl.when(i > 0)
                def _():
                    dg_ref[...] += part_g

    if wspec is None:
        wspec = (pl.BlockSpec((tn, tko), lambda i, k, n: (n, k)) if w_is_nk
                 else pl.BlockSpec((tko, tn), lambda i, k, n: (k, n)))
    in_specs = [pl.BlockSpec((tm, tn), lambda i, k, n: (i, n)), wspec]
    args = [g, w]
    if has2:
        n2 = g2.shape[1]
        in_specs += [pl.BlockSpec((tm, n2), lambda i, k, n: (i, 0)),
                     pl.BlockSpec((n2, tko), lambda i, k, n: (0, k)) if w_is_nk
                     else pl.BlockSpec((tko, n2), lambda i, k, n: (k, 0))]
        args += [g2, w2]
    if epi == "relu2bwd" or epi == "glubwd":
        in_specs += [pl.BlockSpec((tm, tko), lambda i, k, n: (i, k))] * n_epi
    elif rms:
        in_specs += [pl.BlockSpec((tm, K), lambda i, k, n: (i, 0)), pl.BlockSpec((tm, 1), lambda i, k, n: (i, 0)),
                     pl.BlockSpec((1, K), lambda i, k, n: (0, 0)), pl.BlockSpec((tm, K), lambda i, k, n: (i, 0))]
    args += list(epi_args)
    out_shape = [jax.ShapeDtypeStruct((T, K), F32 if rms else out_dtype)]
    out_specs = [pl.BlockSpec((tm, tko), lambda i, k, n: (i, k))]
    if rms:
        out_shape.append(jax.ShapeDtypeStruct((1, K), F32))
        out_specs.append(pl.BlockSpec((1, K), lambda i, k, n: (0, 0)))
    sem = ("arbitrary",) * 3 if rms else ("parallel", "parallel", "arbitrary")
    res, cres = _call(body, grid=(T // tm, K // tko, nn), in_specs=in_specs, out_specs=out_specs, out_shape=out_shape,
                      scratch_shapes=[pltpu.VMEM((tm, tko), F32)], sem=sem, name=name, args=args, comm=comm)
    res = res if rms else res[0]
    return res if comm is None else (res, cres)


def matmul_tn(a, g, *, act=None, tk, tn, tt, name, pack=None, pack_spec=None, pack_shape=None):
    T, K = a.shape
    N = g.shape[1]
    to_pack = pack_spec is not None

    def body(a_ref, g_ref, *rest):
        o_ref = rest[-1]
        t = pl.program_id(2)
        part = _dot_tn(_act(a_ref[...], act), g_ref[...].astype(BF16))
        part = part.reshape(o_ref.shape)

        @pl.when(t == 0)
        def _():
            o_ref[...] = part

        @pl.when(t > 0)
        def _():
            o_ref[...] += part

    in_specs = [pl.BlockSpec((tt, tk), lambda k, n, t: (t, k)), pl.BlockSpec((tt, tn), lambda k, n, t: (t, n))]
    args = [a, g]
    aliases = {}
    if pack is not None:
        in_specs.append(_ANY)
        args.append(pack)
        aliases = {2: 0}
    return pl.pallas_call(
        body, grid=(K // tk, N // tn, T // tt), in_specs=in_specs,
        out_specs=pack_spec if to_pack else pl.BlockSpec((tk, tn), lambda k, n, t: (k, n)),
        out_shape=jax.ShapeDtypeStruct(pack_shape if to_pack else (K, N), F32), input_output_aliases=aliases,
        compiler_params=_cp("parallel", "parallel", "arbitrary"), name=name)(*args)


def attn_fwd(q, kv, *, tm, name):
    T = q.shape[0]
    M = kv.shape[0]
    scale = XA_HEAD_DIM ** -0.5

    def body(q_ref, kv_ref, o_ref):
        for h in range(XA_HEADS):
            sl = slice(h * XA_HEAD_DIM, (h + 1) * XA_HEAD_DIM)
            kh = kv_ref[:, h * XA_HEAD_DIM:(h + 1) * XA_HEAD_DIM]
            vh = kv_ref[:, D_MODEL + h * XA_HEAD_DIM:D_MODEL + (h + 1) * XA_HEAD_DIM]
            s = _dot_nt(q_ref[:, sl], kh) * scale
            s = s - jnp.max(s, axis=-1, keepdims=True)
            p = jnp.exp(s)
            p = p / jnp.sum(p, axis=-1, keepdims=True)
            o_ref[:, sl] = _dot(p.astype(BF16), vh).astype(BF16)

    return pl.pallas_call(
        body, grid=(T // tm,),
        in_specs=[pl.BlockSpec((tm, D_MODEL), lambda i: (i, 0)), pl.BlockSpec((M, 2 * D_MODEL), lambda i: (0, 0))],
        out_specs=pl.BlockSpec((tm, D_MODEL), lambda i: (i, 0)),
        out_shape=jax.ShapeDtypeStruct((T, D_MODEL), BF16),
        compiler_params=_cp("parallel"), name=name)(q, kv)


def attn_bwd(q, kv, do, *, tm, name):
    T = q.shape[0]
    M = kv.shape[0]
    scale = XA_HEAD_DIM ** -0.5

    def body(q_ref, kv_ref, do_ref, dq_ref, dkv_ref):
        i = pl.program_id(0)

        @pl.when(i == 0)
        def _():
            dkv_ref[...] = jnp.zeros_like(dkv_ref)

        for h in range(XA_HEADS):
            sl = slice(h * XA_HEAD_DIM, (h + 1) * XA_HEAD_DIM)
            slv = slice(D_MODEL + h * XA_HEAD_DIM, D_MODEL + (h + 1) * XA_HEAD_DIM)
            qh = q_ref[:, sl]
            kh = kv_ref[:, sl]
            vh = kv_ref[:, slv]
            doh = do_ref[:, sl]
            s = _dot_nt(qh, kh) * scale
            s = s - jnp.max(s, axis=-1, keepdims=True)
            p = jnp.exp(s)
            p = p / jnp.sum(p, axis=-1, keepdims=True)
            pb = p.astype(BF16)
            dkv_ref[:, slv] += _dot_tn(pb, doh)
            dp = _dot_nt(doh, vh)
            ds = (p * (dp - jnp.sum(dp * p, axis=-1, keepdims=True)) * scale).astype(BF16)
            dq_ref[:, sl] = _dot(ds, kh).astype(BF16)
            dkv_ref[:, sl] += _dot_tn(ds, qh)

    spec = pl.BlockSpec((tm, D_MODEL), lambda i: (i, 0))
    kvspec = pl.BlockSpec((M, 2 * D_MODEL), lambda i: (0, 0))
    return pl.pallas_call(
        body, grid=(T // tm,), in_specs=[spec, kvspec, spec], out_specs=[spec, kvspec],
        out_shape=[jax.ShapeDtypeStruct((T, D_MODEL), BF16), jax.ShapeDtypeStruct((M, 2 * D_MODEL), F32)],
        compiler_params=_cp("arbitrary"), name=name)(q, kv, do)


def loss_head(x, g, target, *, tm, name):
    T, D = x.shape

    def body(x_ref, g_ref, t_ref, l_ref, dx_ref, dg_ref):
        i = pl.program_id(0)
        xv = x_ref[...]
        gain = g_ref[...]
        r = lax.rsqrt(jnp.mean(xv * xv, axis=-1, keepdims=True) + EPS)
        xhat = xv * r
        err = xhat * gain - t_ref[...]
        part_l = jnp.full((1, 128), 0.5 / D, F32) * jnp.sum(err * err)
        dy = err * (1.0 / D)
        gd = dy * gain
        dx_ref[...] = r * (gd - xhat * jnp.mean(gd * xhat, axis=-1, keepdims=True))
        part_g = jnp.sum(dy * xhat, axis=0, keepdims=True)

        @pl.when(i == 0)
        def _():
            l_ref[...] = part_l
            dg_ref[...] = part_g

        @pl.when(i > 0)
        def _():
            l_ref[...] += part_l
            dg_ref[...] += part_g

    spec = pl.BlockSpec((tm, D), lambda i: (i, 0))
    return pl.pallas_call(
        body, grid=(T // tm,), in_specs=[spec, pl.BlockSpec((1, D), lambda i: (0, 0)), spec],
        out_specs=[pl.BlockSpec((1, 128), lambda i: (0, 0)), spec, pl.BlockSpec((1, D), lambda i: (0, 0))],
        out_shape=[jax.ShapeDtypeStruct((1, 128), F32), jax.ShapeDtypeStruct((T, D), F32),
                   jax.ShapeDtypeStruct((1, D), F32)],
        compiler_params=_cp("arbitrary"), name=name)(x, g, target)


S5_LS = 128
S5_SEG = 8
S5_TB = S5_LS * S5_SEG
S5_CH = 8
S5_CW = 128
S5_NST = 512
S5_UNROLL = 4


def _cmul(ar, ai, br, bi):
    return ar * br - ai * bi, ar * bi + ai * br


def s5_params_fwd(a_re, a_im, log_dt, b_re, b_im, expand, *, name):
    G, P = a_re.shape

    def body(ar_ref, ai_ref, ld_ref, br_ref, bi_ref, e_ref, abr_ref, abi_ref, apr_ref, api_ref, bbr_ref, bbi_ref):
        ar, ai = ar_ref[...], ai_ref[...]
        dt = jnp.exp(ld_ref[...])
        mag = jnp.exp(dt * ar)
        abr, abi = mag * jnp.cos(dt * ai), mag * jnp.sin(dt * ai)
        den = ar * ar + ai * ai
        zr, zi = abr - 1.0, abi
        fr = (zr * ar + zi * ai) / den
        fi = (zi * ar - zr * ai) / den
        frx, fix = _dot_hi(fr, e_ref[...]), _dot_hi(fi, e_ref[...])
        br, bi = br_ref[...], bi_ref[...]
        bbr_ref[...] = frx * br - fix * bi
        bbi_ref[...] = frx * bi + fix * br
        abr_ref[...] = abr
        abi_ref[...] = abi
        pr, pi = abr, abi
        for _ in range(int(math.log2(S5_LS))):
            pr, pi = _cmul(pr, pi, pr, pi)
        apr_ref[...] = pr
        api_ref[...] = pi

    small = jax.ShapeDtypeStruct((G, P), F32)
    big = jax.ShapeDtypeStruct(b_re.shape, F32)
    return pl.pallas_call(body, out_shape=[small, small, small, small, big, big], name=name)(
        a_re, a_im, log_dt, b_re, b_im, expand)


def s5_params_bwd(a_re, a_im, log_dt, b_re, b_im, g_abr, g_abi, g_bbr, g_bbi, expand, expand_t, *, name):
    G, P = a_re.shape

    def body(ar_ref, ai_ref, ld_ref, br_ref, bi_ref, gar_ref, gai_ref, gbr_ref, gbi_ref, e_ref, et_ref,
             dar_ref, dai_ref, dld_ref, dbr_ref, dbi_ref):
        ar, ai = ar_ref[...], ai_ref[...]
        dt = jnp.exp(ld_ref[...])
        mag = jnp.exp(dt * ar)
        cs, sn = jnp.cos(dt * ai), jnp.sin(dt * ai)
        abr, abi = mag * cs, mag * sn
        den = ar * ar + ai * ai
        zr, zi = abr - 1.0, abi
        fr = (zr * ar + zi * ai) / den
        fi = (zi * ar - zr * ai) / den
        frx, fix = _dot_hi(fr, e_ref[...]), _dot_hi(fi, e_ref[...])
        br, bi = br_ref[...], bi_ref[...]
        gbr, gbi = gbr_ref[...], gbi_ref[...]
        dbr_ref[...] = frx * gbr + fix * gbi
        dbi_ref[...] = -fix * gbr + frx * gbi
        gfr = _dot_hi(br * gbr + bi * gbi, et_ref[...])
        gfi = _dot_hi(-bi * gbr + br * gbi, et_ref[...])
        g_zr = (gfr * ar - gfi * ai) / den
        g_zi = (gfr * ai + gfi * ar) / den
        g_ar = gfr * (zr - fr * 2.0 * ar) / den + gfi * (zi - fi * 2.0 * ar) / den
        g_ai = gfr * (zi - fr * 2.0 * ai) / den + gfi * (-zr - fi * 2.0 * ai) / den
        t_abr = gar_ref[...] + g_zr
        t_abi = gai_ref[...] + g_zi
        g_mag = t_abr * cs + t_abi * sn
        g_th = mag * (-t_abr * sn + t_abi * cs)
        dar_ref[...] = g_ar + g_mag * mag * dt
        dai_ref[...] = g_ai + g_th * dt
        g_dt = jnp.sum(g_mag * mag * ar + g_th * ai, axis=-1, keepdims=True)
        dld_ref[...] = g_dt * dt

    small = jax.ShapeDtypeStruct((G, P), F32)
    big = jax.ShapeDtypeStruct(b_re.shape, F32)
    return pl.pallas_call(body, out_shape=[small, small, jax.ShapeDtypeStruct((G, 1), F32), big, big], name=name)(
        a_re, a_im, log_dt, b_re, b_im, g_abr, g_abi, g_bbr, g_bbi, expand, expand_t)


def _s5_permute_in(src_ref, dst_ref):
    for i in range(S5_LS):
        dst_ref[pl.ds(8 * i, 8), :] = src_ref[pl.ds(i, 8, stride=S5_LS), :]


def _s5_permute_out(src_ref, dst_ref):
    for r in range(S5_SEG):
        for k in range(S5_LS // 8):
            dst_ref[pl.ds(r * S5_LS + 8 * k, 8), :] = src_ref[pl.ds(64 * k + r, 8, stride=8), :]


def _s5_scan(a_r, a_i, dr_ref, di_ref, init_r, init_i, store=None, reverse=False, conj=False):
    sgn = -1.0 if conj else 1.0

    def steps(n, c):
        sr, si = c
        for u in range(S5_UNROLL):
            k = n * S5_UNROLL + u
            i = S5_LS - 1 - k if reverse else k
            nr = a_r * sr - sgn * a_i * si + dr_ref[i]
            ni = a_r * si + sgn * a_i * sr + di_ref[i]
            if store is not None:
                store(i, nr, ni, sr, si)
            sr, si = nr, ni
        return sr, si

    return lax.fori_loop(0, S5_LS // S5_UNROLL, steps, (init_r, init_i))


def _s5_stitch(apr, api, fin_r, fin_i, car_r, car_i, reverse=False, conj=False):
    sgn = -1.0 if conj else 1.0
    rows_r, rows_i = [None] * S5_SEG, [None] * S5_SEG
    order = range(S5_SEG - 1, -1, -1) if reverse else range(S5_SEG)
    for r in order:
        rows_r[r], rows_i[r] = car_r, car_i
        fr, fi = fin_r[r:r + 1], fin_i[r:r + 1]
        car_r, car_i = (apr * car_r - sgn * api * car_i + fr, apr * car_i + sgn * api * car_r + fi)
    return jnp.concatenate(rows_r, 0), jnp.concatenate(rows_i, 0), car_r, car_i


def _s5_specs(nb, rev):
    blk = (lambda c, b: (nb - 1 - b, c)) if rev else (lambda c, b: (b, c))
    tok = pl.BlockSpec((S5_TB, S5_CW), blk)
    par_b = pl.BlockSpec((1, S5_CW, S5_NST), lambda c, b: (c, 0, 0))
    par_c = pl.BlockSpec((1, S5_NST, S5_CW), lambda c, b: (c, 0, 0))
    vec_s = pl.BlockSpec((1, 1, S5_NST), lambda c, b: (c, 0, 0))
    vec_c = pl.BlockSpec((1, 1, S5_CW), lambda c, b: (c, 0, 0))
    return tok, par_b, par_c, vec_s, vec_c


def s5_fwd(proj, bbr, bbi, ccr, cci, abr, abi, apr, api, dskip, *, name, comm=None):
    T = proj.shape[0]
    nb = T // S5_TB
    zeros8 = functools.partial(jnp.zeros, (S5_SEG, S5_NST), F32)

    def body(u_ref, bbr_ref, bbi_ref, ccr_ref, cci_ref, ar_ref, ai_ref, apr_ref, api_ref, d_ref,
             y_ref, csr_ref, csi_ref, uf_ref, up_ref, dr_ref, di_ref, sr_ref, si_ref, yp_ref, car_ref, cai_ref):
        b = pl.program_id(1)

        @pl.when(b == 0)
        def _():
            car_ref[...] = jnp.zeros_like(car_ref)
            cai_ref[...] = jnp.zeros_like(cai_ref)

        csr_ref[0, 0] = car_ref[...]
        csi_ref[0, 0] = cai_ref[...]
        uf_ref[...] = u_ref[...].astype(F32)
        _s5_permute_in(uf_ref, up_ref)
        upb = up_ref[...].astype(BF16)
        dr_ref[...] = _dot(upb, bbr_ref[0]).reshape(S5_LS, S5_SEG, S5_NST)
        di_ref[...] = _dot(upb, bbi_ref[0]).reshape(S5_LS, S5_SEG, S5_NST)
        a_r = jnp.broadcast_to(ar_ref[0], (S5_SEG, S5_NST))
        a_i = jnp.broadcast_to(ai_ref[0], (S5_SEG, S5_NST))
        fin_r, fin_i = _s5_scan(a_r, a_i, dr_ref, di_ref, zeros8(), zeros8())
        cin_r, cin_i, ncr, nci = _s5_stitch(apr_ref[0], api_ref[0], fin_r, fin_i, car_ref[...], cai_ref[...])
        car_ref[...] = ncr
        cai_ref[...] = nci

        def store(i, nr, ni, sr, si):
            sr_ref[i] = nr
            si_ref[i] = ni

        _s5_scan(a_r, a_i, dr_ref, di_ref, cin_r, cin_i, store=store)
        s_r = sr_ref[...].reshape(S5_TB, S5_NST).astype(BF16)
        s_i = si_ref[...].reshape(S5_TB, S5_NST).astype(BF16)
        yp_ref[...] = _dot(s_r, ccr_ref[0]) - _dot(s_i, cci_ref[0]) + d_ref[0] * up_ref[...]
        _s5_permute_out(yp_ref, uf_ref)
        y_ref[...] = uf_ref[...].astype(BF16)

    tok, par_b, par_c, vec_s, vec_c = _s5_specs(nb, False)
    cs_spec = pl.BlockSpec((1, 1, 1, S5_NST), lambda c, b: (b, c, 0, 0))
    cs_shape = jax.ShapeDtypeStruct((nb, S5_CH, 1, S5_NST), F32)
    tokbuf = pltpu.VMEM((S5_TB, S5_CW), F32)
    stbuf = pltpu.VMEM((S5_LS, S5_SEG, S5_NST), F32)
    return _call(
        body, grid=(S5_CH, nb),
        in_specs=[tok, par_b, par_b, par_c, par_c, vec_s, vec_s, vec_s, vec_s, vec_c],
        out_specs=[tok, cs_spec, cs_spec],
        out_shape=[jax.ShapeDtypeStruct((T, D_S5), BF16), cs_shape, cs_shape],
        scratch_shapes=[tokbuf, tokbuf, stbuf, stbuf, stbuf, stbuf, tokbuf,
                        pltpu.VMEM((1, S5_NST), F32), pltpu.VMEM((1, S5_NST), F32)],
        sem=("parallel", "arbitrary"), name=name, comm=comm,
        args=(proj, bbr, bbi, ccr, cci, abr, abi, apr, api, dskip))


def s5_bwd(proj, dys, csr, csi, bbr, bbi, ccr, cci, abr, abi, apr, api, dskip, dproj, *, name, comm=None):
    T = proj.shape[0]
    nb = T // S5_TB
    zeros8 = functools.partial(jnp.zeros, (S5_SEG, S5_NST), F32)

    def body(u_ref, gy_ref, csr_ref, csi_ref, bbr_ref, bbi_ref, ccr_ref, cci_ref, ar_ref, ai_ref, apr_ref, api_ref,
             d_ref, dproj_ref, du_ref, dbr_ref, dbi_ref, dcr_ref, dci_ref, dd_ref, dar_ref, dai_ref,
             tmp_ref, up_ref, gyp_ref, dr_ref, di_ref, sr_ref, si_ref, gr_ref, gi_ref, car_ref, cai_ref):
        b = pl.program_id(1)

        @pl.when(b == 0)
        def _():
            car_ref[...] = jnp.zeros_like(car_ref)
            cai_ref[...] = jnp.zeros_like(cai_ref)
            for ref in (dbr_ref, dbi_ref, dcr_ref, dci_ref, dd_ref, dar_ref, dai_ref):
                ref[...] = jnp.zeros_like(ref)

        tmp_ref[...] = u_ref[...].astype(F32)
        _s5_permute_in(tmp_ref, up_ref)
        tmp_ref[...] = gy_ref[...].astype(F32)
        _s5_permute_in(tmp_ref, gyp_ref)
        dr_ref[...] = _dot(up_ref[...].astype(BF16), bbr_ref[0]).reshape(S5_LS, S5_SEG, S5_NST)
        di_ref[...] = _dot(up_ref[...].astype(BF16), bbi_ref[0]).reshape(S5_LS, S5_SEG, S5_NST)
        a_r = jnp.broadcast_to(ar_ref[0], (S5_SEG, S5_NST))
        a_i = jnp.broadcast_to(ai_ref[0], (S5_SEG, S5_NST))
        fin_r, fin_i = _s5_scan(a_r, a_i, dr_ref, di_ref, zeros8(), zeros8())
        cin_r, cin_i, _, _ = _s5_stitch(apr_ref[0], api_ref[0], fin_r, fin_i, csr_ref[0, 0], csi_ref[0, 0])
        sr_ref[0] = cin_r
        si_ref[0] = cin_i

        def store_s(i, nr, ni, sr, si):
            sr_ref[i + 1] = nr
            si_ref[i + 1] = ni

        _s5_scan(a_r, a_i, dr_ref, di_ref, cin_r, cin_i, store=store_s)
        dcr_ref[0] += _dot_tn(sr_ref[pl.ds(1, S5_LS)].reshape(S5_TB, S5_NST).astype(BF16), gyp_ref[...].astype(BF16))
        dci_ref[0] -= _dot_tn(si_ref[pl.ds(1, S5_LS)].reshape(S5_TB, S5_NST).astype(BF16), gyp_ref[...].astype(BF16))
        dd_ref[0] += jnp.sum(gyp_ref[...] * up_ref[...], axis=0, keepdims=True)
        dr_ref[...] = _dot_nt(gyp_ref[...].astype(BF16), ccr_ref[0]).reshape(S5_LS, S5_SEG, S5_NST)
        di_ref[...] = (-_dot_nt(gyp_ref[...].astype(BF16), cci_ref[0])).reshape(S5_LS, S5_SEG, S5_NST)
        fin_r, fin_i = _s5_scan(a_r, a_i, dr_ref, di_ref, zeros8(), zeros8(), reverse=True, conj=True)
        gin_r, gin_i, ncr, nci = _s5_stitch(apr_ref[0], api_ref[0], fin_r, fin_i, car_ref[...], cai_ref[...],
                                            reverse=True, conj=True)
        car_ref[...] = ncr
        cai_ref[...] = nci
        def steps_g(n, carry):
            gr, gi, acc_r, acc_i = carry
            for u in range(S5_UNROLL):
                i = S5_LS - 1 - (n * S5_UNROLL + u)
                nr = a_r * gr + a_i * gi + dr_ref[i]
                ni = a_r * gi - a_i * gr + di_ref[i]
                gr_ref[i] = nr
                gi_ref[i] = ni
                pr, pi = sr_ref[i], si_ref[i]
                acc_r, acc_i = acc_r + (nr * pr + ni * pi), acc_i + (ni * pr - nr * pi)
                gr, gi = nr, ni
            return gr, gi, acc_r, acc_i

        _, _, acc_r, acc_i = lax.fori_loop(0, S5_LS // S5_UNROLL, steps_g, (gin_r, gin_i, zeros8(), zeros8()))
        dar_ref[0] += jnp.sum(acc_r, axis=0, keepdims=True)
        dai_ref[0] += jnp.sum(acc_i, axis=0, keepdims=True)
        g_r = lambda: gr_ref[...].reshape(S5_TB, S5_NST).astype(BF16)
        g_i = lambda: gi_ref[...].reshape(S5_TB, S5_NST).astype(BF16)
        dbr_ref[0] += _dot_tn(up_ref[...].astype(BF16), g_r())
        dbi_ref[0] += _dot_tn(up_ref[...].astype(BF16), g_i())
        gyp_ref[...] = _dot_nt(g_r(), bbr_ref[0]) + _dot_nt(g_i(), bbi_ref[0]) + d_ref[0] * gyp_ref[...]
        _s5_permute_out(gyp_ref, tmp_ref)
        du_ref[...] = tmp_ref[...].astype(BF16)

    tok, par_b, par_c, vec_s, vec_c = _s5_specs(nb, True)
    cs_spec = pl.BlockSpec((1, 1, 1, S5_NST), lambda c, b: (nb - 1 - b, c, 0, 0))
    tokbuf = pltpu.VMEM((S5_TB, S5_CW), F32)
    stbuf = pltpu.VMEM((S5_LS, S5_SEG, S5_NST), F32)
    stbuf1 = pltpu.VMEM((S5_LS + 1, S5_SEG, S5_NST), F32)
    return _call(
        body, grid=(S5_CH, nb), comm=comm,
        in_specs=[tok, tok, cs_spec, cs_spec, par_b, par_b, par_c, par_c, vec_s, vec_s, vec_s, vec_s, vec_c, _ANY],
        out_specs=[tok, par_b, par_b, par_c, par_c, vec_c, vec_s, vec_s], aliases={13: 0},
        out_shape=[jax.ShapeDtypeStruct(dproj.shape, BF16),
                   jax.ShapeDtypeStruct((S5_CH, S5_CW, S5_NST), F32), jax.ShapeDtypeStruct((S5_CH, S5_CW, S5_NST), F32),
                   jax.ShapeDtypeStruct((S5_CH, S5_NST, S5_CW), F32), jax.ShapeDtypeStruct((S5_CH, S5_NST, S5_CW), F32),
                   jax.ShapeDtypeStruct((S5_CH, 1, S5_CW), F32),
                   jax.ShapeDtypeStruct((S5_CH, 1, S5_NST), F32), jax.ShapeDtypeStruct((S5_CH, 1, S5_NST), F32)],
        scratch_shapes=[tokbuf, tokbuf, tokbuf, stbuf, stbuf, stbuf1, stbuf1, stbuf, stbuf,
                        pltpu.VMEM((1, S5_NST), F32), pltpu.VMEM((1, S5_NST), F32)],
        sem=("parallel", "arbitrary"), name=name,
        args=(proj, dys, csr, csi, bbr, bbi, ccr, cci, abr, abi, apr, api, dskip, dproj))


SSD_L = SSD_CHUNK
SSD_GW = 256
NEG = -1e30


def _expand16(v):
    lane = lax.broadcasted_iota(jnp.int32, (v.shape[0], 128), 1)
    parts = [jnp.where(lane < SSD_HEADDIM, v[:, 2 * j:2 * j + 1], v[:, 2 * j + 1:2 * j + 2]) for j in range(8)]
    return jnp.concatenate(parts, axis=1)


def _headsum(v, hsum):
    hi = v.astype(BF16)
    lo = (v - hi.astype(F32)).astype(BF16)
    return _dot(hi, hsum) + _dot(lo, hsum)


def _softplus(x):
    return jnp.maximum(x, 0.0) + jnp.log(1.0 + jnp.exp(-jnp.abs(x)))


def _ssd_chunk_fwd(z, xbc, tail, dtraw, hprev, cw, cb, dtb, alog, dsk, nw, tril):
    L = SSD_L
    f = {}
    xe = jnp.concatenate([tail, xbc], axis=0)
    sh = [xbc] + [pltpu.roll(xe, s, 0)[8:] for s in (1, 2, 3)]
    conv = cb + cw[3:4] * sh[0] + cw[2:3] * sh[1] + cw[1:2] * sh[2] + cw[0:1] * sh[3]
    sig = _sigmoid(conv)
    xa = conv * sig
    xs, bm, cm = xa[:, :D_SSD], xa[:, D_SSD:D_SSD + 512], xa[:, D_SSD + 512:]
    pre = dtraw + dtb
    dt = _softplus(pre)
    a_h = -jnp.exp(alog)
    acum = _dot_hi(tril, dt * a_h)
    acum_t = acum.T
    alast = acum[L - 1:L]
    exp_a = jnp.exp(acum)
    dec = jnp.exp(alast - acum)
    exp_al = jnp.exp(alast)
    dt_x, dec_x, exp_a_x, exp_al_x = _expand16(dt), _expand16(dec), _expand16(exp_a), _expand16(exp_al)
    d_x = _expand16(dsk)
    xh = xs * dt_x
    xhb = xh.astype(BF16)
    xd = (xh * dec_x).astype(BF16)
    row = lax.broadcasted_iota(jnp.int32, (L, L), 0)
    col = lax.broadcasted_iota(jnp.int32, (L, L), 1)
    causal = row >= col
    lane = lax.broadcasted_iota(jnp.int32, (L, 128), 1)
    low = lane < SSD_HEADDIM
    hb = hprev.astype(BF16)
    y_pairs, yoff_parts, st_parts, cbs, lms = [], [], [], [], []
    for g in range(SSD_NGROUPS):
        bg = bm[:, g * 128:(g + 1) * 128].astype(BF16)
        cg = cm[:, g * 128:(g + 1) * 128].astype(BF16)
        cbg = _dot_nt(cg, bg)
        cbs.append(cbg)
        for j in (2 * g, 2 * g + 1):
            xp = xhb[:, j * 128:(j + 1) * 128]
            ys = []
            for h in (2 * j, 2 * j + 1):
                lm = jnp.exp(jnp.where(causal, acum[:, h:h + 1] - acum_t[h:h + 1, :], NEG))
                lms.append(lm)
                ys.append(_dot((cbg * lm).astype(BF16), xp))
            y_pairs.append(jnp.where(low, ys[0], ys[1]))
        gs = slice(g * SSD_GW, (g + 1) * SSD_GW)
        yoff_parts.append(_dot(cg, hb[:, gs]) * exp_a_x[:, gs])
        st_parts.append(_dot_tn(bg, xd[:, gs]))
    yoff = jnp.concatenate(yoff_parts, axis=1)
    y = jnp.concatenate(y_pairs, axis=1) + yoff + d_x * xs
    hnew = exp_al_x * hprev + jnp.concatenate(st_parts, axis=1)
    sz = _sigmoid(z)
    gz = y * (z * sz)
    r = lax.rsqrt(jnp.mean(gz * gz, axis=-1, keepdims=True) + EPS)
    out = gz * r * nw
    f.update(sh=sh, conv=conv, sig=sig, xs=xs, bm=bm, cm=cm, pre=pre, dt=dt, a_h=a_h, exp_a_x=exp_a_x, dec_x=dec_x,
             exp_al=exp_al, exp_al_x=exp_al_x, dt_x=dt_x, d_x=d_x, xh=xh, xhb=xhb, xd=xd, causal=causal, low=low, hb=hb,
             cbs=cbs, lms=lms, yoff=yoff, y=y, sz=sz, gz=gz, r=r)
    return out, hnew, f


def _ssd_params(conv_w, conv_b, dt_bias, a_log, d_skip, norm_w):
    pad16 = lambda v: jnp.pad(v.reshape(1, SSD_HEADS), ((0, 0), (0, 128 - SSD_HEADS)))
    return (jnp.pad(conv_w, ((0, 8 - SSD_CONV), (0, 0))), conv_b.reshape(1, D_XBC), pad16(dt_bias), pad16(a_log),
            pad16(d_skip), norm_w.reshape(1, D_SSD))


def _ssd_param_specs():
    full = lambda shape: pl.BlockSpec(shape, lambda i: (0, 0))
    return [full((8, D_XBC)), full((1, D_XBC)), full((1, 128)), full((1, 128)), full((1, 128)), full((1, D_SSD))]


def ssd_fwd(proj, dtraw, params, tril, ymix, *, name, comm=None):
    T = proj.shape[0]
    nc = T // SSD_L

    def body(z_ref, x_ref, dt_ref, cw_ref, cb_ref, dtb_ref, al_ref, dsk_ref, nw_ref, tril_ref, ymix_ref,
             o_ref, hs_ref, h_ref, tail_ref):
        i = pl.program_id(0)

        @pl.when(i == 0)
        def _():
            h_ref[...] = jnp.zeros_like(h_ref)
            tail_ref[...] = jnp.zeros_like(tail_ref)

        xbc = x_ref[...].astype(F32)
        hprev = h_ref[...]
        hs_ref[0] = hprev
        out, hnew, _ = _ssd_chunk_fwd(z_ref[...].astype(F32), xbc, tail_ref[...], dt_ref[...], hprev, cw_ref[...],
                                      cb_ref[...], dtb_ref[...], al_ref[...], dsk_ref[...], nw_ref[...], tril_ref[...])
        o_ref[...] = out.astype(BF16)
        h_ref[...] = hnew
        tail_ref[...] = xbc[SSD_L - 8:]

    return _call(
        body, grid=(nc,),
        in_specs=[pl.BlockSpec((SSD_L, D_SSD), lambda i: (i, 1)), pl.BlockSpec((SSD_L, D_XBC), lambda i: (i, 1)),
                  pl.BlockSpec((SSD_L, 128), lambda i: (i, 0))] + _ssd_param_specs()
                 + [pl.BlockSpec((SSD_L, SSD_L), lambda i: (0, 0)), _ANY],
        out_specs=[pl.BlockSpec((SSD_L, D_SSD), lambda i: (i, 1)),
                   pl.BlockSpec((1, SSD_STATE, D_SSD), lambda i: (i, 0, 0))],
        out_shape=[jax.ShapeDtypeStruct(ymix.shape, BF16), jax.ShapeDtypeStruct((nc, SSD_STATE, D_SSD), F32)],
        scratch_shapes=[pltpu.VMEM((SSD_STATE, D_SSD), F32), pltpu.VMEM((8, D_XBC), F32)],
        sem=("arbitrary",), name=name, args=(proj, proj, dtraw, *params, tril, ymix), aliases={10: 0}, comm=comm)


def ssd_bwd(proj, dtraw, hs, dymix, params, tril, triu, trils, headsum, *, name, comm=None):
    T = proj.shape[0]
    nc = T // SSD_L
    L = SSD_L

    def body(z_ref, x_ref, xprev_ref, dt_ref, hs_ref, do_ref, cw_ref, cb_ref, dtb_ref, al_ref, dsk_ref, nw_ref,
             tril_ref, triu_ref, trils_ref, hsum_ref,
             dp_ref, ddt_ref, dcw_ref, dcb_ref, ddtb_ref, dal_ref, ddsk_ref, dnw_ref, dh_ref, dnext_ref):
        i = pl.program_id(0)

        @pl.when(i == 0)
        def _():
            dh_ref[...] = jnp.zeros_like(dh_ref)
            dnext_ref[...] = jnp.zeros_like(dnext_ref)
            for ref in (dcw_ref, dcb_ref, ddtb_ref, dal_ref, ddsk_ref, dnw_ref):
                ref[...] = jnp.zeros_like(ref)

        z = z_ref[...].astype(F32)
        xbc = x_ref[...].astype(F32)
        tail = jnp.where(i == nc - 1, 0.0, xprev_ref[...].astype(F32))
        hprev = hs_ref[0]
        cw, nw = cw_ref[...], nw_ref[...]
        hsum = hsum_ref[...]
        _, _, f = _ssd_chunk_fwd(z, xbc, tail, dt_ref[...], hprev, cw, cb_ref[...], dtb_ref[...], al_ref[...],
                                 dsk_ref[...], nw, tril_ref[...])
        dout = do_ref[...].astype(F32)
        dh = dh_ref[...]
        ghat = f["gz"] * f["r"]
        dn = dout * nw
        dgz = f["r"] * (dn - ghat * jnp.mean(dn * ghat, axis=-1, keepdims=True))
        dnw_ref[...] += jnp.sum(dout * ghat, axis=0, keepdims=True)
        sz = f["sz"]
        dy = dgz * (z * sz)
        dp_ref[:, :D_S5] = jnp.zeros((L, D_S5), BF16)
        dp_ref[:, D_S5:D_S5 + D_SSD] = (dgz * f["y"] * sz * (1.0 + z * (1.0 - sz))).astype(BF16)
        xs = f["xs"]
        ddsk_ref[...] += jnp.sum(_headsum(dy * xs, hsum), axis=0, keepdims=True)
        dyb = dy.astype(BF16)
        dye = (dy * f["exp_a_x"]).astype(BF16)
        dhb = dh.astype(BF16)
        lane = lax.broadcasted_iota(jnp.int32, (L, 128), 1)
        sub = lax.broadcasted_iota(jnp.int32, (128, L), 0)
        zero_b = jnp.zeros((L, 128), BF16)
        rsum = jnp.zeros((L, 128), F32)
        csum_t = jnp.zeros((128, L), F32)
        dx_pairs, dxst_parts, db_parts, dc_parts, dhp_parts = [], [], [], [], []
        for g in range(SSD_NGROUPS):
            gs = slice(g * SSD_GW, (g + 1) * SSD_GW)
            bg = f["bm"][:, g * 128:(g + 1) * 128].astype(BF16)
            cg = f["cm"][:, g * 128:(g + 1) * 128].astype(BF16)
            cbg = f["cbs"][g]
            dcb_g = jnp.zeros((L, L), F32)
            for j in (2 * g, 2 * g + 1):
                xp = f["xhb"][:, j * 128:(j + 1) * 128]
                dyp = dyb[:, j * 128:(j + 1) * 128]
                dxs = []
                for half, h in enumerate((2 * j, 2 * j + 1)):
                    lm = f["lms"][h]
                    dyh = jnp.where(f["low"], dyp, zero_b) if half == 0 else jnp.where(f["low"], zero_b, dyp)
                    dw = jnp.where(f["causal"], _dot_nt(dyh, xp), 0.0)
                    w = cbg * lm
                    e = dw * w
                    dcb_g = dcb_g + dw * lm
                    rsum = jnp.where(lane == h, jnp.sum(e, axis=1, keepdims=True), rsum)
                    csum_t = jnp.where(sub == h, jnp.sum(e, axis=0, keepdims=True), csum_t)
                    dxs.append(_dot_tn(w.astype(BF16), dyp))
                dx_pairs.append(jnp.where(f["low"], dxs[0], dxs[1]))
            dcbb = dcb_g.astype(BF16)
            dxst_parts.append(f["dec_x"][:, gs] * _dot(bg, dhb[:, gs]))
            dc_parts.append(_dot(dcbb, bg) + _dot_nt(dye[:, gs], f["hb"][:, gs]))
            db_parts.append(_dot_tn(dcbb, cg) + _dot_nt(f["xd"][:, gs], dhb[:, gs]))
            dhp_parts.append(f["exp_al_x"][:, gs] * dh[:, gs] + _dot_tn(cg, dye[:, gs]))
        dxst = jnp.concatenate(dxst_parts, axis=1)
        dxh = jnp.concatenate(dx_pairs, axis=1) + dxst
        q = _headsum(f["yoff"] * dy, hsum)
        dstate = _headsum(f["xh"] * dxst, hsum)
        h0t = jnp.sum(_headsum(dh * hprev, hsum), axis=0, keepdims=True) * f["exp_al"]
        da = _dot_hi(triu_ref[...], rsum - csum_t.T + q) + _dot_hi(trils_ref[...], dstate) + h0t
        dt, a_h = f["dt"], f["a_h"]
        ddt = _headsum(dxh * xs, hsum) + da * a_h
        dal_ref[...] += jnp.sum(da * dt, axis=0, keepdims=True) * a_h
        ddtraw = ddt * _sigmoid(f["pre"])
        first16 = lane < SSD_HEADS
        ddtraw = jnp.where(first16, ddtraw, 0.0)
        ddt_ref[...] = ddtraw
        ddtb_ref[...] += jnp.sum(ddtraw, axis=0, keepdims=True)
        dh_ref[...] = jnp.concatenate(dhp_parts, axis=1)
        dxa = jnp.concatenate([dxh * f["dt_x"] + f["d_x"] * dy] + db_parts + dc_parts, axis=1)
        sig, conv = f["sig"], f["conv"]
        dconv = dxa * sig * (1.0 + conv * (1.0 - sig))
        dcb_ref[...] += jnp.sum(dconv, axis=0, keepdims=True)
        rows = [jnp.sum(dconv * f["sh"][3 - k], axis=0, keepdims=True) for k in range(SSD_CONV)]
        dcw_ref[...] += jnp.concatenate(rows + [jnp.zeros((8 - SSD_CONV, D_XBC), F32)], axis=0)
        de = jnp.concatenate([dconv, dnext_ref[...]], axis=0)
        dxbc = cw[3:4] * dconv
        for s in (1, 2, 3):
            dxbc = dxbc + cw[3 - s:4 - s] * pltpu.roll(de, L + 8 - s, 0)[:L]
        dp_ref[:, D_S5 + D_SSD:] = dxbc.astype(BF16)
        dnext_ref[...] = dconv[:8]

    rev = lambda i: nc - 1 - i
    acc = lambda shape: pl.BlockSpec(shape, lambda i: (0, 0))
    tri = pl.BlockSpec((L, L), lambda i: (0, 0))
    return _call(
        body, grid=(nc,),
        in_specs=[pl.BlockSpec((L, D_SSD), lambda i: (rev(i), 1)), pl.BlockSpec((L, D_XBC), lambda i: (rev(i), 1)),
                  pl.BlockSpec((8, D_XBC), lambda i: (jnp.maximum(rev(i) * (L // 8) - 1, 0), 1)),
                  pl.BlockSpec((L, 128), lambda i: (rev(i), 0)),
                  pl.BlockSpec((1, SSD_STATE, D_SSD), lambda i: (rev(i), 0, 0)),
                  pl.BlockSpec((L, D_SSD), lambda i: (rev(i), 1))] + _ssd_param_specs()
                 + [tri, tri, tri, pl.BlockSpec((D_SSD, 128), lambda i: (0, 0))],
        out_specs=[pl.BlockSpec((L, D_MAIN), lambda i: (rev(i), 0)), pl.BlockSpec((L, 128), lambda i: (rev(i), 0)),
                   acc((8, D_XBC)), acc((1, D_XBC)), acc((1, 128)), acc((1, 128)), acc((1, 128)), acc((1, D_SSD))],
        out_shape=[jax.ShapeDtypeStruct((T, D_MAIN), BF16),
                   jax.ShapeDtypeStruct((T, 128), F32), jax.ShapeDtypeStruct((8, D_XBC), F32),
                   jax.ShapeDtypeStruct((1, D_XBC), F32), jax.ShapeDtypeStruct((1, 128), F32),
                   jax.ShapeDtypeStruct((1, 128), F32), jax.ShapeDtypeStruct((1, 128), F32),
                   jax.ShapeDtypeStruct((1, D_SSD), F32)],
        scratch_shapes=[pltpu.VMEM((SSD_STATE, D_SSD), F32), pltpu.VMEM((8, D_XBC), F32)],
        sem=("arbitrary",), name=name, comm=comm,
        args=(proj, proj, proj, dtraw, hs, dymix, *params, tril, triu, trils, headsum))


def _s5_blockdiag(v, rows_per_group, cols_per_group):
    eye = jnp.eye(S5_SEG, dtype=v.dtype)
    w = v[:, :, :, None, :] * eye[None, :, None, :, None]
    return w.reshape(S5_CH, 8 * rows_per_group, 8 * cols_per_group)


def _s5_blockdiag_extract(w, rows_per_group, cols_per_group):
    eye = jnp.eye(S5_SEG, dtype=w.dtype)
    w5 = w.reshape(S5_CH, 8, rows_per_group, 8, cols_per_group)
    return jnp.sum(w5 * eye[None, :, None, :, None], axis=3)


TM = 512

OFF = dict(mlp_w2=0, mlp_w1=1024, w_out=2048, s5_w_glu=2560, xa_wq=2816, xa_wk=3072, xa_wv=3328, xa_wo=3584, w_in=3840)
ROWS = dict(mlp_w2=1024, mlp_w1=1024, w_out=512, s5_w_glu=256, xa_wq=256, xa_wk=256, xa_wv=256, xa_wo=256, w_in=1028)
TAIL_OFF = 4872
TAIL_ROWS = 120
PACK_ROWS = TAIL_OFF + TAIL_ROWS
ROWS_ALL = (0, PACK_ROWS, 832, True)
ROWS_EARLY = (0, OFF["w_in"], 640, False)
ROWS_LATE = (OFF["w_in"], PACK_ROWS - OFF["w_in"], 192, True)
N_SHARD = 4
SMALL_L = ("norm_mix", "s5_a_re", "s5_a_im", "s5_log_dt", "s5_b_re", "s5_b_im", "s5_c_re", "s5_c_im", "s5_d",
           "ssd_conv_w", "ssd_conv_b", "ssd_dt_bias", "ssd_a_log", "ssd_d", "ssd_norm", "norm_xattn", "norm_mem",
           "norm_mlp")
SMALL_Q = 72
CONV_ROWS = 8


def _place():
    x, y, c = lax.axis_index("x"), lax.axis_index("y"), lax.axis_index("c")
    chips = [(1 - x, y), (x, 1 - y), (1 - x, 1 - y)]
    return x, y, c, 2 * x + y, chips, (x, y, 1 - c)


def _remote(src, dst, send_sem, recv_sem, to):
    return pltpu.make_async_remote_copy(src_ref=src, dst_ref=dst, send_sem=send_sem, recv_sem=recv_sem,
                                        device_id=to, device_id_type=MESH_T)


def _dma_sems(*counts):
    return [pltpu.SemaphoreType.DMA((n,)) for n in counts]


def place_own(wpack, shard_idx):
    tile = PACK_ROWS // 4

    def body(s_ref, w_ref, o_ref):
        o_ref[0] = w_ref[...]

    return pl.pallas_call(
        body,
        grid_spec=pltpu.PrefetchScalarGridSpec(
            num_scalar_prefetch=1, grid=(4,),
            in_specs=[pl.BlockSpec((tile, D_MODEL), lambda i, s_ref: (i, 0))],
            out_specs=pl.BlockSpec((1, tile, D_MODEL), lambda i, s_ref: (s_ref[0], i, 0))),
        out_shape=jax.ShapeDtypeStruct((N_SHARD,) + wpack.shape, wpack.dtype),
        compiler_params=_cp("parallel"), name="place_own")(shard_idx, wpack)


def _range_half(ref, rows, c):
    half = rows[1] // 2
    return pl.ds(pl.multiple_of(rows[0] + c * half, 16), half)


def gather_over_ici(wpack, placed, rows=ROWS_ALL):
    def copies(ci, co, cs):
        w_ref, (out_ref,), (send, recv) = ci[0], co, cs
        x, y, c, s, chips, sibling = _place()
        mine = _range_half(w_ref, rows, c)
        sends = [_remote(w_ref.at[mine, :], out_ref.at[s, mine, :], send.at[j], recv.at[j], (*chip, c))
                 for j, chip in enumerate(chips)]
        lands = [out_ref.at[2 * chip[0] + chip[1], mine, :] for chip in chips]
        arrivals = [_remote(land, land, send.at[j], recv.at[j], sibling) for j, land in enumerate(lands)]
        return sends, arrivals

    def start(ci, co, cs):
        for cp in copies(ci, co, cs)[0]:
            cp.start()

    def wait(ci, co, cs):
        sends, arrivals = copies(ci, co, cs)
        for cp in arrivals:
            cp.wait_recv()
        for cp in sends:
            cp.wait_send()

    return Comm("gather_ici", [wpack, placed], [jax.ShapeDtypeStruct(placed.shape, placed.dtype)], _dma_sems(3, 3),
                start, wait, aliases={1: 0})


def gather_to_sibling(landed, rows=ROWS_ALL):
    def copies(co, cs):
        (out_ref,), (send, recv) = co, cs
        x, y, c, s, chips, sibling = _place()
        slots = [2 * chip[0] + chip[1] for chip in chips]
        mine, other = _range_half(out_ref, rows, c), _range_half(out_ref, rows, 1 - c)
        sends = [_remote(out_ref.at[t, mine, :], out_ref.at[t, mine, :], send.at[j], recv.at[j], sibling)
                 for j, t in enumerate(slots)]
        arrivals = [_remote(out_ref.at[t, other, :], out_ref.at[t, other, :], send.at[j], recv.at[j], sibling)
                    for j, t in enumerate(slots)]
        return sends, arrivals

    def start(ci, co, cs):
        for cp in copies(co, cs)[0]:
            cp.start()

    def wait(ci, co, cs):
        sends, arrivals = copies(co, cs)
        for cp in arrivals:
            cp.wait_recv()
        for cp in sends:
            cp.wait_send()

    return Comm("gather_d2d", [landed], [jax.ShapeDtypeStruct(landed.shape, landed.dtype)], _dma_sems(3, 3), start, wait,
                aliases={0: 0})


def exchange_halves(gpack, rows):
    def copy(ci, co, cs):
        (g_ref,), (out_ref,), (send, recv) = ci, co, cs
        x, y, c, s, chips, sibling = _place()
        return _remote(g_ref.at[:, _range_half(g_ref, rows, 1 - c), :], out_ref, send.at[0], recv.at[0], sibling)

    return Comm("exchange", [gpack], [jax.ShapeDtypeStruct((N_SHARD, rows[1] // 2, D_MODEL), F32)], _dma_sems(1, 1),
                lambda ci, co, cs: copy(ci, co, cs).start(), lambda ci, co, cs: copy(ci, co, cs).wait())


def scatter_chips(csum):
    def copies(ci, co, cs):
        (c_ref,), (out_ref,), (send, recv) = ci, co, cs
        x, y, c, s, chips, sibling = _place()
        sends = [_remote(c_ref.at[2 * chip[0] + chip[1]], out_ref.at[s], send.at[j], recv.at[j], (*chip, c))
                 for j, chip in enumerate(chips)]
        arrivals = [_remote(c_ref.at[2 * chip[0] + chip[1]], out_ref.at[2 * chip[0] + chip[1]], send.at[j], recv.at[j],
                            (*chip, c)) for j, chip in enumerate(chips)]
        return sends, arrivals

    def start(ci, co, cs):
        for cp in copies(ci, co, cs)[0]:
            cp.start()

    def wait(ci, co, cs):
        sends, arrivals = copies(ci, co, cs)
        for cp in arrivals:
            cp.wait_recv()
        for cp in sends:
            cp.wait_send()

    return Comm("scatter", [csum], [jax.ShapeDtypeStruct(csum.shape, csum.dtype)], _dma_sems(3, 3), start, wait)


def share_reduced(gshards, smalls, l, rows):
    with_tail = rows[3]

    def copies(ci, co, cs):
        (g_ref, sm_ref), (send, recv, loc) = co, cs
        x, y, c, s, chips, sibling = _place()
        my_half = g_ref.at[l, _range_half(g_ref, rows, c), :]
        tail = g_ref.at[l, pl.ds(PACK_ROWS - TAIL_ROWS, TAIL_ROWS), :]
        big = _remote(my_half, my_half, send.at[0], recv.at[0], sibling)
        keep_tail = pltpu.make_async_copy(tail, sm_ref.at[l, s], loc.at[0])
        tails = [_remote(tail, sm_ref.at[l, s], send.at[1 + j], recv.at[1 + j], (*chip, 1)) for j, chip in enumerate(chips)]
        tails += [_remote(tail, sm_ref.at[l, s], send.at[4 + j], recv.at[1 + j], (*chip, 0)) for j, chip in enumerate(chips)]
        tails.append(_remote(tail, sm_ref.at[l, s], send.at[7], recv.at[4], sibling))
        other = g_ref.at[l, _range_half(g_ref, rows, 1 - c), :]
        big_in = _remote(other, other, send.at[0], recv.at[0], sibling)
        slots = [sm_ref.at[l, 2 * chip[0] + chip[1]] for chip in chips]
        tails_in = [_remote(slot, slot, send.at[1 + j], recv.at[1 + j], sibling) for j, slot in enumerate(slots)]
        sib_tail_in = _remote(sm_ref.at[l, s], sm_ref.at[l, s], send.at[7], recv.at[4], sibling)
        return c, big, keep_tail, tails, big_in, tails_in, sib_tail_in

    def start(ci, co, cs):
        c, big, keep_tail, tails, _, _, _ = copies(ci, co, cs)
        big.start()
        if with_tail:
            @pl.when(c == 1)
            def _():
                keep_tail.start()
                for cp in tails:
                    cp.start()

    def wait(ci, co, cs):
        c, big, keep_tail, tails, big_in, tails_in, sib_tail_in = copies(ci, co, cs)
        big_in.wait_recv()
        big.wait_send()
        if with_tail:
            for cp in tails_in:
                cp.wait_recv()

            @pl.when(c == 0)
            def _():
                sib_tail_in.wait_recv()

            @pl.when(c == 1)
            def _():
                for cp in tails:
                    cp.wait_send()
                keep_tail.wait()

    sds = lambda a: jax.ShapeDtypeStruct(a.shape, a.dtype)
    return Comm("share", [gshards, smalls], [sds(gshards), sds(smalls)], _dma_sems(8, 5, 1), start, wait,
                aliases={0: 0, 1: 1})


def _consts():
    e = np.zeros((S5_STATE, S5_STATE * S5_GROUP), np.float32)
    for p in range(S5_STATE):
        e[p, p * S5_GROUP:(p + 1) * S5_GROUP] = 1.0
    hs = np.zeros((D_SSD, 128), np.float32)
    for h in range(SSD_HEADS):
        hs[h * SSD_HEADDIM:(h + 1) * SSD_HEADDIM, h] = 1.0
    ones = np.ones((SSD_L, SSD_L), np.float32)
    return dict(expand=jnp.asarray(e), expand_t=jnp.asarray(e.T), headsum=jnp.asarray(hs).astype(BF16),
                tril=jnp.asarray(np.tril(ones)), triu=jnp.asarray(np.triu(ones)), trils=jnp.asarray(np.tril(ones, -1)))


def _s5_mats(w, cst):
    b_re = w["s5_b_re"].reshape(S5_GROUPS, S5_STATE * S5_GROUP)
    b_im = w["s5_b_im"].reshape(S5_GROUPS, S5_STATE * S5_GROUP)
    abr, abi, apr, api, bbr, bbi = s5_params_fwd(w["s5_a_re"], w["s5_a_im"], w["s5_log_dt"].reshape(S5_GROUPS, 1),
                                                 b_re, b_im, cst["expand"], name="s5_params_fwd")
    t = lambda v: v.reshape(S5_CH, 8, S5_STATE, S5_GROUP).transpose(0, 1, 3, 2)
    c4 = lambda v: v.reshape(S5_CH, 8, S5_GROUP, S5_STATE).transpose(0, 1, 3, 2)
    vec = lambda v: v.reshape(S5_CH, 1, S5_NST)
    return dict(bbr=_s5_blockdiag(t(bbr), S5_GROUP, S5_STATE).astype(BF16),
                bbi=_s5_blockdiag(t(bbi), S5_GROUP, S5_STATE).astype(BF16),
                ccr=_s5_blockdiag(c4(w["s5_c_re"]), S5_STATE, S5_GROUP).astype(BF16),
                cci=_s5_blockdiag(c4(w["s5_c_im"]), S5_STATE, S5_GROUP).astype(BF16),
                abr=vec(abr), abi=vec(abi), apr=vec(apr), api=vec(api),
                dsk=w["s5_d"].reshape(S5_CH, 1, S5_CW), b_re=b_re, b_im=b_im)


def _layer_fwd(x, mem, w, cst, next_pack=None, rest=None):
    sv = {}
    g = lambda n: w[n].reshape(1, -1)
    res = norm_matmul(x, g("norm_mix"), w["w_in_t"], w["w_dt_t"], tm=TM, tn=1024, name="in_proj", w_transposed=True,
                      comm=gather_over_ici(rest[0], rest[1], ROWS_EARLY) if rest else None)
    (proj, h0, r0, dtraw), staged = res if rest else (res, None)
    s5m = _s5_mats(w, cst)
    comm = _combine(gather_to_sibling(staged[0], ROWS_EARLY) if rest else None,
                    gather_over_ici(*next_pack) if next_pack else None)
    (ys, csr, csi), landed = s5_fwd(proj, s5m["bbr"], s5m["bbi"], s5m["ccr"], s5m["cci"], s5m["abr"], s5m["abi"],
                                    s5m["apr"], s5m["api"], s5m["dsk"], name="s5_fwd", comm=comm)
    if rest:
        w = dict(w, **_pack_weights(landed[0]))
        landed = landed[1:]
    pack = w["pack"]
    (ymix, tglu), gathered = glu_fwd(ys, w["s5_w_glu"], tm=TM, name="glu_fwd",
                                     comm=None if next_pack is None else gather_to_sibling(landed[0]))
    ssdp = _ssd_params(w["ssd_conv_w"], w["ssd_conv_b"], w["ssd_dt_bias"], w["ssd_a_log"], w["ssd_d"], w["ssd_norm"])
    (ymix, hs), _ = ssd_fwd(proj, dtraw, ssdp, cst["tril"], ymix, name="ssd_fwd")
    x1 = matmul_res(ymix, pack, x, tm=TM, tn=1024, tk=1024, name="out_proj",
                    wspec=pl.BlockSpec((2, 512, 1024), lambda i, j, k: (k, OFF["w_out"] // 512, j)))
    q, h1, r1 = norm_matmul(x1, g("norm_xattn"), w["xa_wq"], tm=TM, tn=1024, name="q_proj")
    kv, hm, rm = norm_matmul(mem, g("norm_mem"), w["xa_wkv"], tm=mem.shape[0], tn=1024, name="kv_proj")
    o = attn_fwd(q, kv, tm=TM, name="attn_fwd")
    x2 = matmul_res(o, w["xa_wo"], x1, tm=TM, tn=1024, tk=1024, name="attn_out")
    f1, h2, r2 = norm_matmul(x2, g("norm_mlp"), pack, tm=TM, tn=1024, name="mlp_up", n_out=D_FF,
                             wspec=pl.BlockSpec((1, D_MODEL, 1024), lambda i, j: (j, OFF["mlp_w1"] // 1024, 0)))
    x3 = matmul_res(f1, pack, x2, act="relu2", tm=TM, tn=1024, tk=1024, name="mlp_down",
                    wspec=pl.BlockSpec((1, 1024, 1024), lambda i, j, k: (k, OFF["mlp_w2"] // 1024, j)))
    sv.update(x=x, proj=proj, h0=h0, r0=r0, dtraw=dtraw, s5m=s5m, ys=ys, csr=csr, csi=csi, tglu=tglu, ssdp=ssdp,
              hs=hs, ymix=ymix, x1=x1, q=q, h1=h1, r1=r1, kv=kv, hm=hm, rm=rm, o=o, x2=x2, f1=f1, h2=h2, r2=r2)
    return x3, sv, (gathered[0] if next_pack is not None else None), w


def _layer_bwd(dx3, mem, w, sv, cst, extra_small=None, reduce_hooks=None, early=None):
    gr = {}
    g = lambda n: w[n].reshape(1, -1)
    pack = w["pack"]
    pshape = (N_SHARD, PACK_ROWS, D_MODEL)
    ps = lambda rows, f: pl.BlockSpec((1, rows, 1024), f)
    ps4 = lambda rows, f: pl.BlockSpec((N_SHARD, rows, 1024), f)
    dh1 = matmul_nt(dx3, pack, epi="relu2bwd", epi_args=(sv["f1"],), tm=TM, tko=1024, tn=1024, name="mlp_down_dx",
                    wspec=ps(1024, lambda i, k, n: (k, OFF["mlp_w2"] // 1024, n)), k_out=D_FF,
                    comm=reduce_hooks.exchange() if reduce_hooks else None)
    if reduce_hooks:
        dh1, recv = dh1
        reduce_hooks.after_exchange(recv[0])
    gp = matmul_tn(sv["f1"], dx3, act="relu2", tk=1024, tn=1024, tt=TM, name="mlp_down_dw", pack_shape=pshape,
                   pack_spec=ps(1024, lambda k, n, t: (k, OFF["mlp_w2"] // 1024, 0)))
    gp = matmul_tn(sv["h2"], dh1, tk=1024, tn=1024, tt=TM, name="mlp_up_dw", pack=gp, pack_shape=pshape,
                   pack_spec=ps(1024, lambda k, n, t: (n, OFF["mlp_w1"] // 1024, 0)))
    dx2, gr["norm_mlp"] = matmul_nt(dh1, pack, epi="rmsbwd", epi_args=(sv["x2"], sv["r2"], g("norm_mlp"), dx3),
                                    tm=TM, tko=1024, tn=1024, name="mlp_up_dx", k_out=D_MODEL,
                                    wspec=ps(1024, lambda i, k, n: (n, OFF["mlp_w1"] // 1024, 0)))
    do = matmul_nt(dx2, w["xa_wo"], tm=TM, tko=1024, tn=1024, name="attn_out_dx")
    gp = matmul_tn(sv["o"], dx2, tk=1024, tn=1024, tt=TM, name="attn_out_dw", pack=gp, pack_shape=pshape,
                   pack_spec=ps4(256, lambda k, n, t: (0, OFF["xa_wo"] // 256, 0)))
    dq, dkv = attn_bwd(sv["q"], sv["kv"], do, tm=TM, name="attn_bwd")
    gp = matmul_tn(sv["h1"], dq, tk=1024, tn=1024, tt=TM, name="q_proj_dw", pack=gp, pack_shape=pshape,
                   pack_spec=ps4(256, lambda k, n, t: (0, OFF["xa_wq"] // 256, 0)))
    dx1, gr["norm_xattn"] = matmul_nt(dq, w["xa_wq"], epi="rmsbwd", epi_args=(sv["x1"], sv["r1"], g("norm_xattn"), dx2),
                                      tm=TM, tko=1024, tn=1024, name="q_proj_dx")
    M = mem.shape[0]
    gp = matmul_tn(sv["hm"], dkv, tk=1024, tn=1024, tt=M, name="kv_proj_dw", pack=gp, pack_shape=pshape,
                   pack_spec=ps4(256, lambda k, n, t: (0, OFF["xa_wk"] // 256 + n, 0)))
    _, gr["norm_mem"] = matmul_nt(dkv, w["xa_wkv"], epi="rmsbwd",
                                  epi_args=(mem, sv["rm"], g("norm_mem"), jnp.zeros_like(mem)),
                                  tm=M, tko=1024, tn=1024, name="kv_proj_dx")
    dymix = matmul_nt(dx1, pack, tm=TM, tko=1024, tn=1024, name="out_proj_dx", k_out=2 * D_MODEL,
                      wspec=pl.BlockSpec((2, 512, 1024), lambda i, k, n: (k, OFF["w_out"] // 512, n)))
    gp = matmul_tn(sv["ymix"], dx1, tk=2048, tn=1024, tt=TM, name="out_proj_dw", pack=gp, pack_shape=pshape,
                   pack_spec=ps4(512, lambda k, n, t: (0, OFF["w_out"] // 512, 0)))
    (dproj, ddtraw, dcw, dcb, ddtb, dal, ddsk, dnw), parts = ssd_bwd(
        sv["proj"], sv["dtraw"], sv["hs"], dymix, sv["ssdp"], cst["tril"], cst["triu"], cst["trils"], cst["headsum"],
        name="ssd_bwd", comm=reduce_hooks.scatter() if reduce_hooks else None)
    if reduce_hooks:
        reduce_hooks.after_scatter(parts[0])
    gr["ssd_conv_w"], gr["ssd_conv_b"] = dcw[:SSD_CONV], dcb[0]
    gr["ssd_dt_bias"], gr["ssd_a_log"], gr["ssd_d"] = ddtb[0, :SSD_HEADS], dal[0, :SSD_HEADS], ddsk[0, :SSD_HEADS]
    gr["ssd_norm"] = dnw[0]
    gp = glu_dw(dymix, sv["ys"], sv["tglu"], gp, tt=TM, name="glu_dw",
                pack_spec=ps4(256, lambda i: (0, OFF["s5_w_glu"] // 256, 0)))
    first = _Reduction(gp, early[0], early[1], None, None, ROWS_EARLY) if early else None
    comm = _combine(reduce_hooks.share() if reduce_hooks else None, first.exchange() if first else None)
    (dys,), outs = glu_bwd(dymix, sv["ys"], sv["tglu"], w["s5_w_glu"], tm=TM, name="glu_dx", comm=comm)
    if comm is not None:
        if reduce_hooks:
            reduce_hooks.after_share(outs[:2])
            outs = outs[2:]
        if first:
            first.gshards, first.smalls = reduce_hooks.gshards, reduce_hooks.smalls
            first.after_exchange(outs[0])
    s5m = sv["s5m"]
    (dproj, dbbr, dbbi, dccr, dcci, dd, dabr, dabi), parts = s5_bwd(
        sv["proj"], dys, sv["csr"], sv["csi"], s5m["bbr"], s5m["bbi"], s5m["ccr"], s5m["cci"], s5m["abr"], s5m["abi"],
        s5m["apr"], s5m["api"], s5m["dsk"], dproj, name="s5_bwd", comm=first.scatter() if first else None)
    if first:
        first.after_scatter(parts[0])
    tb = lambda v: _s5_blockdiag_extract(v, S5_GROUP, S5_STATE).transpose(0, 1, 3, 2).reshape(S5_GROUPS, -1)
    tc = lambda v: _s5_blockdiag_extract(v, S5_STATE, S5_GROUP).transpose(0, 1, 3, 2).reshape(S5_GROUPS, S5_GROUP, S5_STATE)
    gr["s5_c_re"], gr["s5_c_im"] = tc(dccr), tc(dcci)
    gr["s5_d"] = dd.reshape(S5_GROUPS, S5_GROUP)
    dar, dai, dld, dbr, dbi = s5_params_bwd(
        w["s5_a_re"], w["s5_a_im"], w["s5_log_dt"].reshape(S5_GROUPS, 1), s5m["b_re"], s5m["b_im"],
        dabr.reshape(S5_GROUPS, S5_STATE), dabi.reshape(S5_GROUPS, S5_STATE), tb(dbbr), tb(dbbi),
        cst["expand"], cst["expand_t"], name="s5_params_bwd")
    gr["s5_a_re"], gr["s5_a_im"], gr["s5_log_dt"] = dar, dai, dld[:, 0]
    gr["s5_b_re"] = dbr.reshape(S5_GROUPS, S5_STATE, S5_GROUP)
    gr["s5_b_im"] = dbi.reshape(S5_GROUPS, S5_STATE, S5_GROUP)
    wt_shape = (D_MAIN + D_DT_PAD, D_MODEL)
    dwt = matmul_tn(dproj, sv["h0"], tk=1024, tn=1024, tt=TM, name="in_proj_dw", pack_shape=wt_shape,
                    pack_spec=pl.BlockSpec((1024, 1024), lambda k, n, t: (k, 0)))
    dwt = matmul_tn(ddtraw, sv["h0"], tk=D_DT_PAD, tn=1024, tt=TM, name="in_proj_dt_dw", pack=dwt, pack_shape=wt_shape,
                    pack_spec=pl.BlockSpec((D_DT_PAD, 1024), lambda k, n, t: (D_MAIN // D_DT_PAD, 0)))
    dx0 = matmul_nt(dproj, w["w_in_t"], g2=ddtraw, w2=w["w_dt_t"], w_is_nk=True, epi="rmsbwd",
                    epi_args=(sv["x"], sv["r0"], g("norm_mix"), dx1), tm=TM, tko=1024, tn=1024, name="in_proj_dx",
                    comm=first.share() if first else None)
    if first:
        dx0, shared = dx0
        first.after_share(shared)
    dx0, gr["norm_mix"] = dx0
    gr = {k: (v[0] if k.startswith("norm_") else v) for k, v in gr.items()}
    for t in range(N_SHARD):
        shard_rows = lax.slice_in_dim(dwt, t * ROWS["w_in"], (t + 1) * ROWS["w_in"], axis=0)
        gp = lax.dynamic_update_slice(gp, shard_rows[None], (t, OFF["w_in"], 0))
    small = [gr[n].reshape(-1) for n in SMALL_L] + ([] if extra_small is None else [extra_small.reshape(-1)])
    small = jnp.concatenate(small)
    small = jnp.pad(small, (0, N_SHARD * SMALL_Q * D_MODEL - small.size)).reshape(N_SHARD, SMALL_Q, D_MODEL)
    gap = TAIL_OFF - OFF["w_in"] - ROWS["w_in"]
    gp = lax.dynamic_update_slice(gp, jnp.pad(small, ((0, 0), (gap, TAIL_ROWS - SMALL_Q), (0, 0))),
                                  (0, TAIL_OFF - gap, 0))
    return dx0, gp, first


def _local_step(x, mem, target, layers, norm_final):
    cst = _consts()
    saved = []
    for l in range(DEPTH):
        x, sv, _, _ = _layer_fwd(x, mem, layers[l], cst)
        saved.append(sv)
    loss, dx, dgf = loss_head(x, norm_final.reshape(1, -1), target, tm=TM, name="loss_head")
    packs = [None] * DEPTH
    for l in reversed(range(DEPTH)):
        dx, packs[l], _ = _layer_bwd(dx, mem, layers[l], saved[l], cst, extra_small=dgf[0] if l == DEPTH - 1 else None)
    return loss, dx, packs


def sum_halves(gpack, recv, c_idx, rows):
    first, count, tile, _ = rows
    half = count // 2
    nb = half // tile

    def body(c_ref, a_ref, b_ref, o_ref):
        o_ref[...] = (a_ref[...] + b_ref[...]).astype(BF16)

    blk = (1, tile, D_MODEL)
    return pl.pallas_call(
        body,
        grid_spec=pltpu.PrefetchScalarGridSpec(
            num_scalar_prefetch=1, grid=(N_SHARD, nb),
            in_specs=[pl.BlockSpec(blk, lambda t, i, c_ref: (t, first // tile + c_ref[0] * nb + i, 0)),
                      pl.BlockSpec(blk, lambda t, i, c_ref: (t, i, 0))],
            out_specs=pl.BlockSpec(blk, lambda t, i, c_ref: (t, i, 0))),
        out_shape=jax.ShapeDtypeStruct((N_SHARD, half, D_MODEL), BF16),
        compiler_params=_cp("parallel", "parallel"), name="sum_halves")(c_idx, gpack, recv)


def sum_chips(parts, csum, gshards, l, place_idx, rows):
    first, count, tile, _ = rows
    nb = count // 2 // tile

    def body(pi_ref, p0, p1, p2, p3, own, g_ref, o_ref):
        s = pi_ref[0]
        vals = [jnp.where(s == k, own[0], p[0]).astype(F32) for k, p in enumerate((p0, p1, p2, p3))]
        o_ref[0] = ((vals[0] + vals[1]) + vals[2]) + vals[3]

    blk = (1, tile, D_MODEL)
    part_spec = lambda k: pl.BlockSpec(blk, lambda i, pi_ref: (jnp.where(pi_ref[0] == k, (k + 1) % N_SHARD, k), i, 0))
    return pl.pallas_call(
        body,
        grid_spec=pltpu.PrefetchScalarGridSpec(
            num_scalar_prefetch=1, grid=(nb,),
            in_specs=[part_spec(k) for k in range(N_SHARD)]
                     + [pl.BlockSpec(blk, lambda i, pi_ref: (pi_ref[0], i, 0)), _ANY],
            out_specs=pl.BlockSpec(blk, lambda i, pi_ref: (l, first // tile + pi_ref[1] * nb + i, 0))),
        out_shape=jax.ShapeDtypeStruct(gshards.shape, F32), input_output_aliases={6: 0},
        compiler_params=_cp("parallel"), name="sum_chips")(place_idx, parts, parts, parts, parts, csum, gshards)


class _Reduction:
    def __init__(self, gpack, layer, place_idx, gshards, smalls, rows=ROWS_ALL):
        self.gpack, self.layer, self.place_idx, self.gshards, self.smalls = gpack, layer, place_idx, gshards, smalls
        self.rows = rows

    def exchange(self):
        return exchange_halves(self.gpack, self.rows)

    def after_exchange(self, recv):
        self.csum = sum_halves(self.gpack, recv, self.place_idx[1:], self.rows)

    def scatter(self):
        return scatter_chips(self.csum)

    def after_scatter(self, parts):
        self.gshards = sum_chips(parts, self.csum, self.gshards, self.layer, self.place_idx, self.rows)

    def share(self):
        return share_reduced(self.gshards, self.smalls, self.layer, self.rows)

    def after_share(self, shared):
        self.gshards, self.smalls = shared

    def run_alone(self):
        self.after_exchange(_comm_only(self.exchange())[0])
        self.after_scatter(_comm_only(self.scatter())[0])
        self.after_share(_comm_only(self.share()))
        return self.gshards, self.smalls


def adamw(w, g, m, v, *, name):
    shape = w.shape
    cols = shape[-1]
    rows = w.size // cols
    tr = 512 if rows % 512 == 0 else rows
    c1 = 1.0 / (1.0 - ADAM_B1 ** ADAM_STEP)
    c2 = 1.0 / (1.0 - ADAM_B2 ** ADAM_STEP)

    def body(w_ref, g_ref, m_ref, v_ref, d_ref, nm_ref, nv_ref):
        gv = g_ref[...]
        nm = ADAM_B1 * m_ref[...] + (1.0 - ADAM_B1) * gv
        nv = ADAM_B2 * v_ref[...] + (1.0 - ADAM_B2) * (gv * gv)
        d_ref[...] = -ADAM_LR * ((nm * c1) / (jnp.sqrt(nv * c2) + ADAM_EPS) + ADAM_WD * w_ref[...])
        nm_ref[...] = nm
        nv_ref[...] = nv

    spec = pl.BlockSpec((tr, cols), lambda i: (i, 0))
    sds = jax.ShapeDtypeStruct((rows, cols), F32)
    outs = pl.pallas_call(body, grid=(rows // tr,), in_specs=[spec] * 4, out_specs=[spec] * 3, out_shape=[sds] * 3,
                          compiler_params=_cp("parallel"), name=name)(
                              *[a.reshape(rows, cols) for a in (w, g, m, v)])
    return [o.reshape(shape) for o in outs]


def _own_pack(wts, l):
    rows = [wts[n][l].T if n == "w_in" else wts[n][l].reshape(ROWS[n], D_MODEL) for n in sorted(OFF, key=OFF.get)]
    cw = wts["ssd_conv_w"][l].reshape(-1)
    hi = lax.reduce_precision(cw, 8, 7)
    mid = lax.reduce_precision(cw - hi, 8, 7)
    lo = lax.reduce_precision(cw - hi - mid, 8, 7)
    conv = jnp.pad(jnp.concatenate([hi, mid, lo]), (0, CONV_ROWS * D_MODEL - 3 * cw.size)).reshape(CONV_ROWS, D_MODEL)
    gap = jnp.zeros((TAIL_OFF - OFF["w_in"] - ROWS["w_in"], D_MODEL), F32)
    rest = jnp.zeros((TAIL_ROWS - CONV_ROWS, D_MODEL), F32)
    return jnp.concatenate(rows + [gap, conv, rest], axis=0).astype(BF16)


def _square(gathered, n):
    return gathered[:, OFF[n]:OFF[n] + ROWS[n]].reshape(N_SHARD * ROWS[n], D_MODEL)


def _pack_weights(gathered):
    w = {"pack": gathered}
    w["s5_w_glu"], w["xa_wq"], w["xa_wo"] = _square(gathered, "s5_w_glu"), _square(gathered, "xa_wq"), _square(gathered, "xa_wo")
    w["xa_wkv"] = jnp.concatenate([_square(gathered, "xa_wk"), _square(gathered, "xa_wv")], axis=1)
    return w


def _layer_weights(gathered, wts, l, late_only=False):
    w = {n: wts[n][l] for n in SMALL_L if n != "ssd_conv_w"}
    if not late_only:
        w.update(_pack_weights(gathered))
    w_in_t = _square(gathered, "w_in")
    w["w_in_t"] = w_in_t[:D_MAIN]
    w["w_dt_t"] = jnp.pad(w_in_t[D_MAIN:], ((0, D_DT_PAD - SSD_HEADS), (0, 0)))
    per = SSD_CONV * D_XBC // N_SHARD
    cw = gathered[:, TAIL_OFF:TAIL_OFF + CONV_ROWS].astype(F32).reshape(N_SHARD, -1)[:, :3 * per]
    cw = cw.reshape(N_SHARD, 3, SSD_CONV, D_XBC // N_SHARD)
    cw = (cw[:, 0] + cw[:, 1]) + cw[:, 2]
    w["ssd_conv_w"] = cw.transpose(1, 0, 2).reshape(SSD_CONV, D_XBC)
    return w


def kernel(x, mem, norm_mix, w_in, s5_a_re, s5_a_im, s5_log_dt, s5_b_re, s5_b_im, s5_c_re, s5_c_im, s5_d, s5_w_glu, ssd_conv_w, ssd_conv_b, ssd_dt_bias, ssd_a_log, ssd_d, ssd_norm, w_out, norm_xattn, norm_mem, xa_wq, xa_wk, xa_wv, xa_wo, norm_mlp, mlp_w1, mlp_w2, norm_final, loss_target, m_norm_mix, m_w_in, m_s5_a_re, m_s5_a_im, m_s5_log_dt, m_s5_b_re, m_s5_b_im, m_s5_c_re, m_s5_c_im, m_s5_d, m_s5_w_glu, m_ssd_conv_w, m_ssd_conv_b, m_ssd_dt_bias, m_ssd_a_log, m_ssd_d, m_ssd_norm, m_w_out, m_norm_xattn, m_norm_mem, m_xa_wq, m_xa_wk, m_xa_wv, m_xa_wo, m_norm_mlp, m_mlp_w1, m_mlp_w2, m_norm_final, v_norm_mix, v_w_in, v_s5_a_re, v_s5_a_im, v_s5_log_dt, v_s5_b_re, v_s5_b_im, v_s5_c_re, v_s5_c_im, v_s5_d, v_s5_w_glu, v_ssd_conv_w, v_ssd_conv_b, v_ssd_dt_bias, v_ssd_a_log, v_ssd_d, v_ssd_norm, v_w_out, v_norm_xattn, v_norm_mem, v_xa_wq, v_xa_wk, v_xa_wv, v_xa_wo, v_norm_mlp, v_mlp_w1, v_mlp_w2, v_norm_final):
    names = ("norm_mix", "w_in", "s5_a_re", "s5_a_im", "s5_log_dt", "s5_b_re", "s5_b_im", "s5_c_re", "s5_c_im", "s5_d",
             "s5_w_glu", "ssd_conv_w", "ssd_conv_b", "ssd_dt_bias", "ssd_a_log", "ssd_d", "ssd_norm", "w_out",
             "norm_xattn", "norm_mem", "xa_wq", "xa_wk", "xa_wv", "xa_wo", "norm_mlp", "mlp_w1", "mlp_w2", "norm_final")
    loc = locals()
    wts = {n: loc[n] for n in names}
    mom = {n: loc["m_" + n] for n in names}
    var = {n: loc["v_" + n] for n in names}
    shard = 2 * lax.axis_index("x") + lax.axis_index("y")
    core = lax.axis_index("c")

    cst = _consts()
    place_idx = jnp.stack([shard, core]).astype(jnp.int32)
    h, mem0 = x[0], mem[0]

    own0 = _own_pack(wts, 0)
    staged = _comm_only(gather_over_ici(own0, place_own(own0, place_idx[:1]), ROWS_LATE))[0]
    gathered = _comm_only(gather_to_sibling(staged, ROWS_LATE))[0]
    layers, saved = [], []
    for l in range(DEPTH):
        nxt = None
        if l + 1 < DEPTH:
            own = _own_pack(wts, l + 1)
            nxt = (own, place_own(own, place_idx[:1]))
        h, sv, gathered, w = _layer_fwd(h, mem0, _layer_weights(gathered, wts, l, late_only=l == 0), cst, next_pack=nxt,
                                        rest=(own0, gathered) if l == 0 else None)
        layers.append(w)
        saved.append(sv)
    loss, dx, dgf = loss_head(h, norm_final.reshape(1, -1), loss_target[0], tm=TM, name="loss_head")

    gshards = jnp.zeros((DEPTH, PACK_ROWS, D_MODEL), F32)
    smalls = jnp.zeros((DEPTH, N_SHARD, TAIL_ROWS, D_MODEL), F32)
    pending = None
    for l in reversed(range(DEPTH)):
        dx, gpack, first = _layer_bwd(dx, mem0, layers[l], saved[l], cst, extra_small=dgf[0] if l == DEPTH - 1 else None,
                                      reduce_hooks=pending, early=(l, place_idx) if l == 0 else None)
        if pending is not None:
            gshards, smalls = pending.gshards, pending.smalls
        pending = _Reduction(gpack, l, place_idx, gshards, smalls)
    gshards, smalls = _Reduction(gpack, 0, place_idx, first.gshards, first.smalls, ROWS_LATE).run_alone()

    g = {n: gshards[:, OFF[n]:OFF[n] + ROWS[n]].reshape(wts[n].shape) for n in OFF if n != "w_in"}
    g["w_in"] = gshards[:, OFF["w_in"]:OFF["w_in"] + ROWS["w_in"]].transpose(0, 2, 1)
    small_red = smalls[:, :, :SMALL_Q].reshape(DEPTH, -1)
    off = 0
    for n in SMALL_L:
        shape = (SSD_CONV, D_XBC) if n == "ssd_conv_w" else wts[n].shape[1:]
        size = math.prod(shape)
        g[n] = small_red[:, off:off + size].reshape((DEPTH,) + shape)
        off += size
    g["norm_final"] = small_red[DEPTH - 1, off:off + D_MODEL]
    g["ssd_conv_w"] = lax.dynamic_slice_in_dim(g["ssd_conv_w"], shard * (D_XBC // N_SHARD), D_XBC // N_SHARD, axis=2)

    deltas, new_m, new_v = {}, {}, {}
    for n in names:
        deltas[n], new_m[n], new_v[n] = adamw(wts[n], g[n], mom[n], var[n], name="adamw_" + n)
    loss_all = lax.psum(loss[0, 0], ("x", "y", "c"))
    return (loss_all, dx[None], *[g[n] for n in names], *[deltas[n] for n in names], *[new_m[n] for n in names],
            *[new_v[n] for n in names])
```
